```python
import math
import jax, jax.numpy as jnp
from jax import lax
import numpy as np

D_MODEL = 1024
BATCH = 8
SEQ = 8192
DEPTH = 2

CONV_DIM = 512
CONV_WIDTH = 3
FOX_HEADS = 8
FOX_HEAD_DIM = 64
FOX_DIM = FOX_HEADS * FOX_HEAD_DIM
Q_BLOCK = 128
EVEN_IN = 3 * CONV_DIM + 3 * FOX_DIM + FOX_HEADS
EVEN_MIX = CONV_DIM + FOX_DIM
SSM_INNER = 2 * D_MODEL
SSM_HEAD_DIM = 64
SSM_HEADS = SSM_INNER // SSM_HEAD_DIM
SSM_GROUPS = 4
SSM_STATE = 128
SSM_CONV_WIDTH = 4
SSM_CHUNK = 128
SSM_CONV_CH = SSM_INNER + 2 * SSM_GROUPS * SSM_STATE
ODD_IN = SSM_INNER + SSM_CONV_CH + SSM_HEADS
D_FF = 2816
FFN_CONV_WIDTH = 3
PLE_DIM = 256
N_EVEN = (DEPTH + 1) // 2
N_ODD = DEPTH // 2
LN_EPS = 1e-5
RMS_EPS = 1e-5

kernel_name = "hybrid_shortconv_fox_mamba2_deepnorm_convffn_ple"


def layer_norm(x, g, b):
    xf = x.astype(jnp.float32)
    mu = jnp.mean(xf, axis=-1, keepdims=True)
    var = jnp.mean(jnp.square(xf - mu), axis=-1, keepdims=True)
    return ((xf - mu) * lax.rsqrt(var + LN_EPS) * g + b).astype(x.dtype)


def causal_dwconv(x, w, b=None):
    K, C = w.shape
    out = lax.conv_general_dilated(
        x, w[:, None, :].astype(x.dtype), window_strides=(1,), padding=[(K - 1, 0)],
        dimension_numbers=('NWC', 'WIO', 'NWC'), feature_group_count=C)
    if b is not None:
        out = out + b
    return out


def forgetting_attention(q, k, v, f_logit):
    Bsz, L, _ = q.shape
    H, Dh = FOX_HEADS, FOX_HEAD_DIM
    def heads(t):
        return t.reshape(Bsz, L, H, Dh).transpose(0, 2, 1, 3)
    q, k, v = heads(q), heads(k), heads(v)
    log_f = jax.nn.log_sigmoid(f_logit.astype(jnp.float32))
    F = jnp.cumsum(log_f, axis=1).transpose(0, 2, 1)
    n_blk = L // Q_BLOCK
    qb = q.reshape(Bsz, H, n_blk, Q_BLOCK, Dh).transpose(2, 0, 1, 3, 4)
    Fb = F.reshape(Bsz, H, n_blk, Q_BLOCK).transpose(2, 0, 1, 3)
    k_pos = jnp.arange(L)
    scale = FOX_HEAD_DIM ** -0.5

    def one_block(args):
        i, q_i, F_i = args
        s = jnp.einsum('bhqd,bhkd->bhqk', q_i, k, preferred_element_type=jnp.float32) * scale
        s = s + F_i[..., None] - F[:, :, None, :]
        q_pos = i * Q_BLOCK + jnp.arange(Q_BLOCK)
        s = jnp.where(k_pos[None, :] <= q_pos[:, None], s, -jnp.inf)
        pr = jax.nn.softmax(s, axis=-1)
        return jnp.einsum('bhqk,bhkd->bhqd', pr.astype(v.dtype), v)

    out = lax.map(one_block, (jnp.arange(n_blk), qb, Fb))
    return out.transpose(1, 0, 3, 2, 4).reshape(Bsz, L, H * Dh)


def shortconv_fox_mixer(x, w_in, b_f, w_conv, w_out):
    sizes = [CONV_DIM] * 3 + [FOX_DIM] * 3 + [FOX_HEADS]
    splits = [int(s) for s in np.cumsum(sizes)[:-1]]
    gB, gC, h, q, k, v, f_logit = jnp.split(x @ w_in, splits, axis=-1)
    y_a = gB * causal_dwconv(gC * h, w_conv)
    y_b = forgetting_attention(q, k, v, f_logit + b_f)
    return jnp.concatenate([y_a, y_b], axis=-1) @ w_out


def ssd_chunked(x, dt, A, Bm, Cm):
    Bsz, L, H, P = x.shape
    G, N, Q = SSM_GROUPS, SSM_STATE, SSM_CHUNK
    R = H // G
    nc = L // Q
    xc = (x * dt[..., None]).reshape(Bsz, nc, Q, G, R, P)
    acs = jnp.cumsum((dt * A).reshape(Bsz, nc, Q, G, R), axis=2)
    Bc = Bm.reshape(Bsz, nc, Q, G, N)
    Cc = Cm.reshape(Bsz, nc, Q, G, N)
    seg = acs[:, :, :, None] - acs[:, :, None, :]
    causal = jnp.tril(jnp.ones((Q, Q), dtype=bool))[:, :, None, None]
    Lmat = jnp.exp(jnp.where(causal, seg, -jnp.inf))
    CB = jnp.einsum('bclgn,bcsgn->bclsg', Cc, Bc)
    y_diag = jnp.einsum('bclsg,bclsgr,bcsgrp->bclgrp', CB, Lmat, xc)
    decay_to_end = jnp.exp(acs[:, :, -1:] - acs)
    states = jnp.einsum('bcsgn,bcsgr,bcsgrp->bcgrpn', Bc, decay_to_end, xc)
    chunk_decay = jnp.exp(acs[:, :, -1])

    def step(hs, inp):
        s_c, d_c = inp
        return hs * d_c[..., None, None] + s_c, hs

    h0 = jnp.zeros((Bsz, G, R, P, N), jnp.float32)
    _, prev = lax.scan(step, h0, (states.astype(jnp.float32).transpose(1, 0, 2, 3, 4, 5),
                                  chunk_decay.transpose(1, 0, 2, 3)))
    prev = prev.transpose(1, 0, 2, 3, 4, 5)
    y_off = jnp.einsum('bclgn,bcgrpn,bclgr->bclgrp', Cc, prev, jnp.exp(acs))
    return (y_diag + y_off).reshape(Bsz, L, H, P)


def gated_group_rmsnorm(y, z, g):
    Bsz, L, Dn = y.shape
    u = (y * jax.nn.silu(z)).astype(jnp.float32).reshape(Bsz, L, SSM_GROUPS, Dn // SSM_GROUPS)
    u = u * lax.rsqrt(jnp.mean(jnp.square(u), axis=-1, keepdims=True) + RMS_EPS)
    return (u.reshape(Bsz, L, Dn) * g).astype(y.dtype)


def mamba2_mixer(x, w_in, conv_w, conv_b, dt_bias, a_log, d_skip, norm_g, w_out):
    Bsz, L, _ = x.shape
    z, xBC, dt = jnp.split(x @ w_in, [SSM_INNER, SSM_INNER + SSM_CONV_CH], axis=-1)
    xBC = jax.nn.silu(causal_dwconv(xBC, conv_w, conv_b))
    xs, Bm, Cm = jnp.split(xBC, [SSM_INNER, SSM_INNER + SSM_GROUPS * SSM_STATE], axis=-1)
    dt = jax.nn.softplus(dt.astype(jnp.float32) + dt_bias)
    A = -jnp.exp(a_log.astype(jnp.float32))
    xh = xs.reshape(Bsz, L, SSM_HEADS, SSM_HEAD_DIM)
    y = ssd_chunked(xh, dt, A,
                    Bm.reshape(Bsz, L, SSM_GROUPS, SSM_STATE),
                    Cm.reshape(Bsz, L, SSM_GROUPS, SSM_STATE))
    y = (y + d_skip[:, None] * xh).astype(x.dtype).reshape(Bsz, L, SSM_INNER)
    return gated_group_rmsnorm(y, z, norm_g) @ w_out


def conv_ffn(x, w_up, conv_w, conv_b, w_down):
    u = causal_dwconv(x @ w_up, conv_w, conv_b)
    g, v = jnp.split(u, 2, axis=-1)
    return (jax.nn.silu(g) * v) @ w_down


def per_layer_embed(x, p_i, w_proj, w_gate, b_gate):
    gate = jax.nn.sigmoid(x @ w_gate + b_gate)
    return x + gate * (p_i @ w_proj)


def _fwd_setup_inputs(seed: int = 0) -> dict:
    key = jax.random.key(seed)
    ks = iter(jax.random.split(key, 32))
    beta = (8.0 * DEPTH) ** -0.25

    def nrm(shape, scale):
        return jax.random.normal(next(ks), shape, jnp.float32) * scale

    x = nrm((BATCH, SEQ, D_MODEL), 1.0)
    p = nrm((DEPTH, BATCH, SEQ, PLE_DIM), 1.0)
    even_w_in = nrm((N_EVEN, D_MODEL, EVEN_IN), D_MODEL ** -0.5)
    even_b_f = 2.0 + nrm((N_EVEN, FOX_HEADS), 0.5)
    even_conv_w = nrm((N_EVEN, CONV_WIDTH, CONV_DIM), CONV_WIDTH ** -0.5)
    even_w_out = nrm((N_EVEN, EVEN_MIX, D_MODEL), beta * EVEN_MIX ** -0.5)
    odd_w_in = nrm((N_ODD, D_MODEL, ODD_IN), D_MODEL ** -0.5)
    odd_conv_w = nrm((N_ODD, SSM_CONV_WIDTH, SSM_CONV_CH), SSM_CONV_WIDTH ** -0.5)
    odd_conv_b = nrm((N_ODD, SSM_CONV_CH), 0.02)
    u = jax.random.uniform(next(ks), (N_ODD, SSM_HEADS), jnp.float32)
    dt0 = jnp.exp(u * (math.log(0.1) - math.log(0.001)) + math.log(0.001))
    odd_dt_bias = dt0 + jnp.log(-jnp.expm1(-dt0))
    odd_a_log = jnp.log(jax.random.uniform(next(ks), (N_ODD, SSM_HEADS), jnp.float32, 1.0, 16.0))
    odd_d_skip = 1.0 + nrm((N_ODD, SSM_HEADS), 0.1)
    odd_norm_g = 1.0 + nrm((N_ODD, SSM_INNER), 0.02)
    odd_w_out = nrm((N_ODD, SSM_INNER, D_MODEL), beta * SSM_INNER ** -0.5)
    ln_mix_g = 1.0 + nrm((DEPTH, D_MODEL), 0.02)
    ln_mix_b = nrm((DEPTH, D_MODEL), 0.02)
    ffn_w_up = nrm((DEPTH, D_MODEL, 2 * D_FF), D_MODEL ** -0.5)
    ffn_conv_w = nrm((DEPTH, FFN_CONV_WIDTH, 2 * D_FF), FFN_CONV_WIDTH ** -0.5)
    ffn_conv_b = nrm((DEPTH, 2 * D_FF), 0.02)
    ffn_w_down = nrm((DEPTH, D_FF, D_MODEL), beta * D_FF ** -0.5)
    ln_ffn_g = 1.0 + nrm((DEPTH, D_MODEL), 0.02)
    ln_ffn_b = nrm((DEPTH, D_MODEL), 0.02)
    ple_w_proj = nrm((DEPTH, PLE_DIM, D_MODEL), PLE_DIM ** -0.5)
    ple_w_gate = nrm((DEPTH, D_MODEL, D_MODEL), D_MODEL ** -0.5)
    ple_b_gate = nrm((DEPTH, D_MODEL), 0.02)
    return {"x": x, "p": p,
            "even_w_in": even_w_in, "even_b_f": even_b_f, "even_conv_w": even_conv_w,
            "even_w_out": even_w_out,
            "odd_w_in": odd_w_in, "odd_conv_w": odd_conv_w, "odd_conv_b": odd_conv_b,
            "odd_dt_bias": odd_dt_bias, "odd_a_log": odd_a_log, "odd_d_skip": odd_d_skip,
            "odd_norm_g": odd_norm_g, "odd_w_out": odd_w_out,
            "ln_mix_g": ln_mix_g, "ln_mix_b": ln_mix_b,
            "ffn_w_up": ffn_w_up, "ffn_conv_w": ffn_conv_w, "ffn_conv_b": ffn_conv_b,
            "ffn_w_down": ffn_w_down, "ln_ffn_g": ln_ffn_g, "ln_ffn_b": ln_ffn_b,
            "ple_w_proj": ple_w_proj, "ple_w_gate": ple_w_gate, "ple_b_gate": ple_b_gate}


def _fwd_reference(x, p, even_w_in, even_b_f, even_conv_w, even_w_out,
              odd_w_in, odd_conv_w, odd_conv_b, odd_dt_bias, odd_a_log, odd_d_skip,
              odd_norm_g, odd_w_out, ln_mix_g, ln_mix_b, ffn_w_up, ffn_conv_w, ffn_conv_b,
              ffn_w_down, ln_ffn_g, ln_ffn_b, ple_w_proj, ple_w_gate, ple_b_gate):
    alpha = (2.0 * DEPTH) ** 0.25
    h = x
    for i in range(DEPTH):
        j = i // 2
        if i % 2 == 0:
            mix = shortconv_fox_mixer(h, even_w_in[j], even_b_f[j], even_conv_w[j], even_w_out[j])
        else:
            mix = mamba2_mixer(h, odd_w_in[j], odd_conv_w[j], odd_conv_b[j], odd_dt_bias[j],
                               odd_a_log[j], odd_d_skip[j], odd_norm_g[j], odd_w_out[j])
        h = layer_norm(alpha * h + mix, ln_mix_g[i], ln_mix_b[i])
        ffn = conv_ffn(h, ffn_w_up[i], ffn_conv_w[i], ffn_conv_b[i], ffn_w_down[i])
        h = layer_norm(alpha * h + ffn, ln_ffn_g[i], ln_ffn_b[i])
        h = per_layer_embed(h, p[i], ple_w_proj[i], ple_w_gate[i], ple_b_gate[i])
    return h


import jax as _jax
import jax.numpy as _jnp

TWIN_FORMAT = 'train_step'
FWD_PARAMS = ['x', 'p', 'even_w_in', 'even_b_f', 'even_conv_w', 'even_w_out', 'odd_w_in', 'odd_conv_w', 'odd_conv_b', 'odd_dt_bias', 'odd_a_log', 'odd_d_skip', 'odd_norm_g', 'odd_w_out', 'ln_mix_g', 'ln_mix_b', 'ffn_w_up', 'ffn_conv_w', 'ffn_conv_b', 'ffn_w_down', 'ln_ffn_g', 'ln_ffn_b', 'ple_w_proj', 'ple_w_gate', 'ple_b_gate']
TWIN_WEIGHTS = ['even_w_in', 'even_b_f', 'even_conv_w', 'even_w_out', 'odd_w_in', 'odd_conv_w', 'odd_conv_b', 'odd_dt_bias', 'odd_a_log', 'odd_d_skip', 'odd_norm_g', 'odd_w_out', 'ln_mix_g', 'ln_mix_b', 'ffn_w_up', 'ffn_conv_w', 'ffn_conv_b', 'ffn_w_down', 'ln_ffn_g', 'ln_ffn_b', 'ple_w_proj', 'ple_w_gate', 'ple_b_gate']
TWIN_DIFF_INPUT = 'x'
TWIN_INPUTS = ['x', 'p', 'even_w_in', 'even_b_f', 'even_conv_w', 'even_w_out', 'odd_w_in', 'odd_conv_w', 'odd_conv_b', 'odd_dt_bias', 'odd_a_log', 'odd_d_skip', 'odd_norm_g', 'odd_w_out', 'ln_mix_g', 'ln_mix_b', 'ffn_w_up', 'ffn_conv_w', 'ffn_conv_b', 'ffn_w_down', 'ln_ffn_g', 'ln_ffn_b', 'ple_w_proj', 'ple_w_gate', 'ple_b_gate', 'loss_target', 'm_even_w_in', 'm_even_b_f', 'm_even_conv_w', 'm_even_w_out', 'm_odd_w_in', 'm_odd_conv_w', 'm_odd_conv_b', 'm_odd_dt_bias', 'm_odd_a_log', 'm_odd_d_skip', 'm_odd_norm_g', 'm_odd_w_out', 'm_ln_mix_g', 'm_ln_mix_b', 'm_ffn_w_up', 'm_ffn_conv_w', 'm_ffn_conv_b', 'm_ffn_w_down', 'm_ln_ffn_g', 'm_ln_ffn_b', 'm_ple_w_proj', 'm_ple_w_gate', 'm_ple_b_gate', 'v_even_w_in', 'v_even_b_f', 'v_even_conv_w', 'v_even_w_out', 'v_odd_w_in', 'v_odd_conv_w', 'v_odd_conv_b', 'v_odd_dt_bias', 'v_odd_a_log', 'v_odd_d_skip', 'v_odd_norm_g', 'v_odd_w_out', 'v_ln_mix_g', 'v_ln_mix_b', 'v_ffn_w_up', 'v_ffn_conv_w', 'v_ffn_conv_b', 'v_ffn_w_down', 'v_ln_ffn_g', 'v_ln_ffn_b', 'v_ple_w_proj', 'v_ple_w_gate', 'v_ple_b_gate']
TWIN_OUTPUTS = ['loss', 'grad_x', 'grad_even_w_in', 'grad_even_b_f', 'grad_even_conv_w', 'grad_even_w_out', 'grad_odd_w_in', 'grad_odd_conv_w', 'grad_odd_conv_b', 'grad_odd_dt_bias', 'grad_odd_a_log', 'grad_odd_d_skip', 'grad_odd_norm_g', 'grad_odd_w_out', 'grad_ln_mix_g', 'grad_ln_mix_b', 'grad_ffn_w_up', 'grad_ffn_conv_w', 'grad_ffn_conv_b', 'grad_ffn_w_down', 'grad_ln_ffn_g', 'grad_ln_ffn_b', 'grad_ple_w_proj', 'grad_ple_w_gate', 'grad_ple_b_gate', 'delta_even_w_in', 'delta_even_b_f', 'delta_even_conv_w', 'delta_even_w_out', 'delta_odd_w_in', 'delta_odd_conv_w', 'delta_odd_conv_b', 'delta_odd_dt_bias', 'delta_odd_a_log', 'delta_odd_d_skip', 'delta_odd_norm_g', 'delta_odd_w_out', 'delta_ln_mix_g', 'delta_ln_mix_b', 'delta_ffn_w_up', 'delta_ffn_conv_w', 'delta_ffn_conv_b', 'delta_ffn_w_down', 'delta_ln_ffn_g', 'delta_ln_ffn_b', 'delta_ple_w_proj', 'delta_ple_w_gate', 'delta_ple_b_gate', 'new_m_even_w_in', 'new_m_even_b_f', 'new_m_even_conv_w', 'new_m_even_w_out', 'new_m_odd_w_in', 'new_m_odd_conv_w', 'new_m_odd_conv_b', 'new_m_odd_dt_bias', 'new_m_odd_a_log', 'new_m_odd_d_skip', 'new_m_odd_norm_g', 'new_m_odd_w_out', 'new_m_ln_mix_g', 'new_m_ln_mix_b', 'new_m_ffn_w_up', 'new_m_ffn_conv_w', 'new_m_ffn_conv_b', 'new_m_ffn_w_down', 'new_m_ln_ffn_g', 'new_m_ln_ffn_b', 'new_m_ple_w_proj', 'new_m_ple_w_gate', 'new_m_ple_b_gate', 'new_v_even_w_in', 'new_v_even_b_f', 'new_v_even_conv_w', 'new_v_even_w_out', 'new_v_odd_w_in', 'new_v_odd_conv_w', 'new_v_odd_conv_b', 'new_v_odd_dt_bias', 'new_v_odd_a_log', 'new_v_odd_d_skip', 'new_v_odd_norm_g', 'new_v_odd_w_out', 'new_v_ln_mix_g', 'new_v_ln_mix_b', 'new_v_ffn_w_up', 'new_v_ffn_conv_w', 'new_v_ffn_conv_b', 'new_v_ffn_w_down', 'new_v_ln_ffn_g', 'new_v_ln_ffn_b', 'new_v_ple_w_proj', 'new_v_ple_w_gate', 'new_v_ple_b_gate']
TWIN_LEAF_KINDS = {'loss': 'loss', 'grad_x': 'grad_x', 'grad_even_w_in': 'grad_w', 'grad_even_b_f': 'grad_w', 'grad_even_conv_w': 'grad_w', 'grad_even_w_out': 'grad_w', 'grad_odd_w_in': 'grad_w', 'grad_odd_conv_w': 'grad_w', 'grad_odd_conv_b': 'grad_w', 'grad_odd_dt_bias': 'grad_w', 'grad_odd_a_log': 'grad_w', 'grad_odd_d_skip': 'grad_w', 'grad_odd_norm_g': 'grad_w', 'grad_odd_w_out': 'grad_w', 'grad_ln_mix_g': 'grad_w', 'grad_ln_mix_b': 'grad_w', 'grad_ffn_w_up': 'grad_w', 'grad_ffn_conv_w': 'grad_w', 'grad_ffn_conv_b': 'grad_w', 'grad_ffn_w_down': 'grad_w', 'grad_ln_ffn_g': 'grad_w', 'grad_ln_ffn_b': 'grad_w', 'grad_ple_w_proj': 'grad_w', 'grad_ple_w_gate': 'grad_w', 'grad_ple_b_gate': 'grad_w', 'delta_even_w_in': 'delta_w', 'delta_even_b_f': 'delta_w', 'delta_even_conv_w': 'delta_w', 'delta_even_w_out': 'delta_w', 'delta_odd_w_in': 'delta_w', 'delta_odd_conv_w': 'delta_w', 'delta_odd_conv_b': 'delta_w', 'delta_odd_dt_bias': 'delta_w', 'delta_odd_a_log': 'delta_w', 'delta_odd_d_skip': 'delta_w', 'delta_odd_norm_g': 'delta_w', 'delta_odd_w_out': 'delta_w', 'delta_ln_mix_g': 'delta_w', 'delta_ln_mix_b': 'delta_w', 'delta_ffn_w_up': 'delta_w', 'delta_ffn_conv_w': 'delta_w', 'delta_ffn_conv_b': 'delta_w', 'delta_ffn_w_down': 'delta_w', 'delta_ln_ffn_g': 'delta_w', 'delta_ln_ffn_b': 'delta_w', 'delta_ple_w_proj': 'delta_w', 'delta_ple_w_gate': 'delta_w', 'delta_ple_b_gate': 'delta_w', 'new_m_even_w_in': 'new_m', 'new_m_even_b_f': 'new_m', 'new_m_even_conv_w': 'new_m', 'new_m_even_w_out': 'new_m', 'new_m_odd_w_in': 'new_m', 'new_m_odd_conv_w': 'new_m', 'new_m_odd_conv_b': 'new_m', 'new_m_odd_dt_bias': 'new_m', 'new_m_odd_a_log': 'new_m', 'new_m_odd_d_skip': 'new_m', 'new_m_odd_norm_g': 'new_m', 'new_m_odd_w_out': 'new_m', 'new_m_ln_mix_g': 'new_m', 'new_m_ln_mix_b': 'new_m', 'new_m_ffn_w_up': 'new_m', 'new_m_ffn_conv_w': 'new_m', 'new_m_ffn_conv_b': 'new_m', 'new_m_ffn_w_down': 'new_m', 'new_m_ln_ffn_g': 'new_m', 'new_m_ln_ffn_b': 'new_m', 'new_m_ple_w_proj': 'new_m', 'new_m_ple_w_gate': 'new_m', 'new_m_ple_b_gate': 'new_m', 'new_v_even_w_in': 'new_v', 'new_v_even_b_f': 'new_v', 'new_v_even_conv_w': 'new_v', 'new_v_even_w_out': 'new_v', 'new_v_odd_w_in': 'new_v', 'new_v_odd_conv_w': 'new_v', 'new_v_odd_conv_b': 'new_v', 'new_v_odd_dt_bias': 'new_v', 'new_v_odd_a_log': 'new_v', 'new_v_odd_d_skip': 'new_v', 'new_v_odd_norm_g': 'new_v', 'new_v_odd_w_out': 'new_v', 'new_v_ln_mix_g': 'new_v', 'new_v_ln_mix_b': 'new_v', 'new_v_ffn_w_up': 'new_v', 'new_v_ffn_conv_w': 'new_v', 'new_v_ffn_conv_b': 'new_v', 'new_v_ffn_w_down': 'new_v', 'new_v_ln_ffn_g': 'new_v', 'new_v_ln_ffn_b': 'new_v', 'new_v_ple_w_proj': 'new_v', 'new_v_ple_w_gate': 'new_v', 'new_v_ple_b_gate': 'new_v'}


def _forward(args):
    return _fwd_reference(*[args[k] for k in FWD_PARAMS])


def _output_shape():
    def fwd():
        inp = _fwd_setup_inputs(0)
        return _fwd_reference(*[inp[k] for k in FWD_PARAMS])
    out = _jax.eval_shape(fwd)
    return out.shape, out.dtype

N_MICROBATCH = 1
ADAM_LR = 0.001
ADAM_B1 = 0.9
ADAM_B2 = 0.999
ADAM_EPS = 1e-08
ADAM_WD = 0.01
ADAM_STEP = 10
PER_EXAMPLE_BATCH_AXIS = {'x': 0, 'p': 1, 'loss_target': 0}
SHARED_INPUTS = []
_WEIGHT_DTYPES = {'even_w_in': _jnp.float32, 'even_b_f': _jnp.float32, 'even_conv_w': _jnp.float32, 'even_w_out': _jnp.float32, 'odd_w_in': _jnp.float32, 'odd_conv_w': _jnp.float32, 'odd_conv_b': _jnp.float32, 'odd_dt_bias': _jnp.float32, 'odd_a_log': _jnp.float32, 'odd_d_skip': _jnp.float32, 'odd_norm_g': _jnp.float32, 'odd_w_out': _jnp.float32, 'ln_mix_g': _jnp.float32, 'ln_mix_b': _jnp.float32, 'ffn_w_up': _jnp.float32, 'ffn_conv_w': _jnp.float32, 'ffn_conv_b': _jnp.float32, 'ffn_w_down': _jnp.float32, 'ln_ffn_g': _jnp.float32, 'ln_ffn_b': _jnp.float32, 'ple_w_proj': _jnp.float32, 'ple_w_gate': _jnp.float32, 'ple_b_gate': _jnp.float32}
MOMENT_SCALE = {'even_w_in': 7.588436e-02, 'even_b_f': 3.483556e-01, 'even_conv_w': 1.130145e-01, 'even_w_out': 1.562742e-01, 'odd_w_in': 6.285893e-02, 'odd_conv_w': 1.242272e-01, 'odd_conv_b': 3.863696e-01, 'odd_dt_bias': 2.322198e-01, 'odd_a_log': 7.673762e-01, 'odd_d_skip': 7.423584e-01, 'odd_norm_g': 2.437775e-01, 'odd_w_out': 6.978721e-01, 'ln_mix_g': 1.973679e+00, 'ln_mix_b': 5.977410e+00, 'ffn_w_up': 3.951569e-02, 'ffn_conv_w': 3.965440e-02, 'ffn_conv_b': 1.768863e-01, 'ffn_w_down': 1.310615e-01, 'ln_ffn_g': 4.670993e+01, 'ln_ffn_b': 6.734912e+00, 'ple_w_proj': 6.154647e-01, 'ple_w_gate': 2.587088e-01, 'ple_b_gate': 4.635145e+00}


def _to_microbatches(a, axis):
    t = _jnp.moveaxis(a, axis, 0)
    t = t.reshape((N_MICROBATCH, t.shape[0] // N_MICROBATCH) + t.shape[1:])
    return _jnp.moveaxis(t, 1, axis + 1)


def setup_inputs(seed: int = 0) -> dict:
    inp = _fwd_setup_inputs(seed)
    key = _jax.random.fold_in(_jax.random.key(seed), 7919)
    shape, _ = _output_shape()
    out = dict(inp)
    out["loss_target"] = _jax.random.normal(_jax.random.fold_in(key, 0), shape, _jnp.float32)
    for i, name in enumerate(TWIN_WEIGHTS):
        w = inp[name].astype(_jnp.float32)
        if MOMENT_SCALE is None:
            s = _jnp.sqrt(_jnp.mean(_jnp.square(w)) + 1e-30)
        else:
            s = MOMENT_SCALE[name]
        km, kv = _jax.random.split(_jax.random.fold_in(key, i + 1))
        out[name] = w
        out["m_" + name] = s * _jax.random.normal(km, w.shape, _jnp.float32)
        out["v_" + name] = (s * s) * _jax.random.uniform(kv, w.shape, _jnp.float32, 0.5, 1.5)
    if N_MICROBATCH > 1:
        for name, axis in PER_EXAMPLE_BATCH_AXIS.items():
            out[name] = _to_microbatches(out[name], axis)
    return {'x': out['x'], 'p': out['p'], 'even_w_in': out['even_w_in'], 'even_b_f': out['even_b_f'], 'even_conv_w': out['even_conv_w'], 'even_w_out': out['even_w_out'], 'odd_w_in': out['odd_w_in'], 'odd_conv_w': out['odd_conv_w'], 'odd_conv_b': out['odd_conv_b'], 'odd_dt_bias': out['odd_dt_bias'], 'odd_a_log': out['odd_a_log'], 'odd_d_skip': out['odd_d_skip'], 'odd_norm_g': out['odd_norm_g'], 'odd_w_out': out['odd_w_out'], 'ln_mix_g': out['ln_mix_g'], 'ln_mix_b': out['ln_mix_b'], 'ffn_w_up': out['ffn_w_up'], 'ffn_conv_w': out['ffn_conv_w'], 'ffn_conv_b': out['ffn_conv_b'], 'ffn_w_down': out['ffn_w_down'], 'ln_ffn_g': out['ln_ffn_g'], 'ln_ffn_b': out['ln_ffn_b'], 'ple_w_proj': out['ple_w_proj'], 'ple_w_gate': out['ple_w_gate'], 'ple_b_gate': out['ple_b_gate'], 'loss_target': out['loss_target'], 'm_even_w_in': out['m_even_w_in'], 'm_even_b_f': out['m_even_b_f'], 'm_even_conv_w': out['m_even_conv_w'], 'm_even_w_out': out['m_even_w_out'], 'm_odd_w_in': out['m_odd_w_in'], 'm_odd_conv_w': out['m_odd_conv_w'], 'm_odd_conv_b': out['m_odd_conv_b'], 'm_odd_dt_bias': out['m_odd_dt_bias'], 'm_odd_a_log': out['m_odd_a_log'], 'm_odd_d_skip': out['m_odd_d_skip'], 'm_odd_norm_g': out['m_odd_norm_g'], 'm_odd_w_out': out['m_odd_w_out'], 'm_ln_mix_g': out['m_ln_mix_g'], 'm_ln_mix_b': out['m_ln_mix_b'], 'm_ffn_w_up': out['m_ffn_w_up'], 'm_ffn_conv_w': out['m_ffn_conv_w'], 'm_ffn_conv_b': out['m_ffn_conv_b'], 'm_ffn_w_down': out['m_ffn_w_down'], 'm_ln_ffn_g': out['m_ln_ffn_g'], 'm_ln_ffn_b': out['m_ln_ffn_b'], 'm_ple_w_proj': out['m_ple_w_proj'], 'm_ple_w_gate': out['m_ple_w_gate'], 'm_ple_b_gate': out['m_ple_b_gate'], 'v_even_w_in': out['v_even_w_in'], 'v_even_b_f': out['v_even_b_f'], 'v_even_conv_w': out['v_even_conv_w'], 'v_even_w_out': out['v_even_w_out'], 'v_odd_w_in': out['v_odd_w_in'], 'v_odd_conv_w': out['v_odd_conv_w'], 'v_odd_conv_b': out['v_odd_conv_b'], 'v_odd_dt_bias': out['v_odd_dt_bias'], 'v_odd_a_log': out['v_odd_a_log'], 'v_odd_d_skip': out['v_odd_d_skip'], 'v_odd_norm_g': out['v_odd_norm_g'], 'v_odd_w_out': out['v_odd_w_out'], 'v_ln_mix_g': out['v_ln_mix_g'], 'v_ln_mix_b': out['v_ln_mix_b'], 'v_ffn_w_up': out['v_ffn_w_up'], 'v_ffn_conv_w': out['v_ffn_conv_w'], 'v_ffn_conv_b': out['v_ffn_conv_b'], 'v_ffn_w_down': out['v_ffn_w_down'], 'v_ln_ffn_g': out['v_ln_ffn_g'], 'v_ln_ffn_b': out['v_ln_ffn_b'], 'v_ple_w_proj': out['v_ple_w_proj'], 'v_ple_w_gate': out['v_ple_w_gate'], 'v_ple_b_gate': out['v_ple_b_gate']}


def _loss(weights, diff, rest, loss_target):
    with _jax.named_scope("forward"):
        args = {**rest, TWIN_DIFF_INPUT: diff, **{k: w.astype(_WEIGHT_DTYPES[k]) for k, w in weights.items()}}
        y = _forward(args)
    with _jax.named_scope("loss_head"):
        err = _jnp.square(y.astype(_jnp.float32) - loss_target)
        return 0.5 * _jnp.sum(_jnp.mean(err, axis=-1)) if err.ndim else 0.5 * err


def _adamw(w, g, m, v):
    m = ADAM_B1 * m + (1.0 - ADAM_B1) * g
    v = ADAM_B2 * v + (1.0 - ADAM_B2) * _jnp.square(g)
    m_hat = m / (1.0 - ADAM_B1 ** ADAM_STEP)
    v_hat = v / (1.0 - ADAM_B2 ** ADAM_STEP)
    delta = -ADAM_LR * (m_hat / (_jnp.sqrt(v_hat) + ADAM_EPS) + ADAM_WD * w)
    return delta, m, v


def reference(x, p, even_w_in, even_b_f, even_conv_w, even_w_out, odd_w_in, odd_conv_w, odd_conv_b, odd_dt_bias, odd_a_log, odd_d_skip, odd_norm_g, odd_w_out, ln_mix_g, ln_mix_b, ffn_w_up, ffn_conv_w, ffn_conv_b, ffn_w_down, ln_ffn_g, ln_ffn_b, ple_w_proj, ple_w_gate, ple_b_gate, loss_target, m_even_w_in, m_even_b_f, m_even_conv_w, m_even_w_out, m_odd_w_in, m_odd_conv_w, m_odd_conv_b, m_odd_dt_bias, m_odd_a_log, m_odd_d_skip, m_odd_norm_g, m_odd_w_out, m_ln_mix_g, m_ln_mix_b, m_ffn_w_up, m_ffn_conv_w, m_ffn_conv_b, m_ffn_w_down, m_ln_ffn_g, m_ln_ffn_b, m_ple_w_proj, m_ple_w_gate, m_ple_b_gate, v_even_w_in, v_even_b_f, v_even_conv_w, v_even_w_out, v_odd_w_in, v_odd_conv_w, v_odd_conv_b, v_odd_dt_bias, v_odd_a_log, v_odd_d_skip, v_odd_norm_g, v_odd_w_out, v_ln_mix_g, v_ln_mix_b, v_ffn_w_up, v_ffn_conv_w, v_ffn_conv_b, v_ffn_w_down, v_ln_ffn_g, v_ln_ffn_b, v_ple_w_proj, v_ple_w_gate, v_ple_b_gate):
    given = dict(x=x, p=p, even_w_in=even_w_in, even_b_f=even_b_f, even_conv_w=even_conv_w, even_w_out=even_w_out, odd_w_in=odd_w_in, odd_conv_w=odd_conv_w, odd_conv_b=odd_conv_b, odd_dt_bias=odd_dt_bias, odd_a_log=odd_a_log, odd_d_skip=odd_d_skip, odd_norm_g=odd_norm_g, odd_w_out=odd_w_out, ln_mix_g=ln_mix_g, ln_mix_b=ln_mix_b, ffn_w_up=ffn_w_up, ffn_conv_w=ffn_conv_w, ffn_conv_b=ffn_conv_b, ffn_w_down=ffn_w_down, ln_ffn_g=ln_ffn_g, ln_ffn_b=ln_ffn_b, ple_w_proj=ple_w_proj, ple_w_gate=ple_w_gate, ple_b_gate=ple_b_gate, loss_target=loss_target, m_even_w_in=m_even_w_in, m_even_b_f=m_even_b_f, m_even_conv_w=m_even_conv_w, m_even_w_out=m_even_w_out, m_odd_w_in=m_odd_w_in, m_odd_conv_w=m_odd_conv_w, m_odd_conv_b=m_odd_conv_b, m_odd_dt_bias=m_odd_dt_bias, m_odd_a_log=m_odd_a_log, m_odd_d_skip=m_odd_d_skip, m_odd_norm_g=m_odd_norm_g, m_odd_w_out=m_odd_w_out, m_ln_mix_g=m_ln_mix_g, m_ln_mix_b=m_ln_mix_b, m_ffn_w_up=m_ffn_w_up, m_ffn_conv_w=m_ffn_conv_w, m_ffn_conv_b=m_ffn_conv_b, m_ffn_w_down=m_ffn_w_down, m_ln_ffn_g=m_ln_ffn_g, m_ln_ffn_b=m_ln_ffn_b, m_ple_w_proj=m_ple_w_proj, m_ple_w_gate=m_ple_w_gate, m_ple_b_gate=m_ple_b_gate, v_even_w_in=v_even_w_in, v_even_b_f=v_even_b_f, v_even_conv_w=v_even_conv_w, v_even_w_out=v_even_w_out, v_odd_w_in=v_odd_w_in, v_odd_conv_w=v_odd_conv_w, v_odd_conv_b=v_odd_conv_b, v_odd_dt_bias=v_odd_dt_bias, v_odd_a_log=v_odd_a_log, v_odd_d_skip=v_odd_d_skip, v_odd_norm_g=v_odd_norm_g, v_odd_w_out=v_odd_w_out, v_ln_mix_g=v_ln_mix_g, v_ln_mix_b=v_ln_mix_b, v_ffn_w_up=v_ffn_w_up, v_ffn_conv_w=v_ffn_conv_w, v_ffn_conv_b=v_ffn_conv_b, v_ffn_w_down=v_ffn_w_down, v_ln_ffn_g=v_ln_ffn_g, v_ln_ffn_b=v_ln_ffn_b, v_ple_w_proj=v_ple_w_proj, v_ple_w_gate=v_ple_w_gate, v_ple_b_gate=v_ple_b_gate)
    weights = {n: given[n] for n in TWIN_WEIGHTS}
    shared = {n: given[n] for n in SHARED_INPUTS}
    per_example = {n: given[n] for n in ['x', 'p']}
    grad_fn = _jax.value_and_grad(_loss, argnums=(0, 1))

    def one_microbatch(ex, loss_target):
        ex = dict(ex)
        diff = ex.pop(TWIN_DIFF_INPUT)
        return grad_fn(weights, diff, {**shared, **ex}, loss_target)

    if N_MICROBATCH == 1:
        loss, (grad_w, grad_x) = one_microbatch(per_example, given["loss_target"])
    else:
        def body(carry, xs):
            loss_sum, grad_sum = carry
            l_k, (gw_k, gx_k) = one_microbatch(xs[0], xs[1])
            with _jax.named_scope("update"):
                return (loss_sum + l_k, _jax.tree.map(_jnp.add, grad_sum, gw_k)), gx_k

        init = (_jnp.zeros((), _jnp.float32), _jax.tree.map(_jnp.zeros_like, weights))
        (loss, grad_w), grad_x = _jax.lax.scan(body, init, (per_example, given["loss_target"]))
    with _jax.named_scope("update"):
        delta_w, new_m, new_v = {}, {}, {}
        for n in TWIN_WEIGHTS:
            delta_w[n], new_m[n], new_v[n] = _adamw(weights[n], grad_w[n], given["m_" + n], given["v_" + n])
    return (loss, grad_x, *[grad_w[n] for n in TWIN_WEIGHTS], *[delta_w[n] for n in TWIN_WEIGHTS],
            *[new_m[n] for n in TWIN_WEIGHTS], *[new_v[n] for n in TWIN_WEIGHTS])
```

```python
import jax
import jax.numpy as jnp
import numpy as np
from jax import lax
from jax.experimental import pallas as pl
from jax.experimental.pallas import tpu as pltpu

D_MODEL = 1024
SEQ = 8192
DEPTH = 2
CONV_DIM = 512
CONV_WIDTH = 3
FOX_HEADS = 8
FOX_HEAD_DIM = 64
SSM_HEAD_DIM = 64
SSM_GROUPS = 4
SSM_STATE = 128
SSM_CONV_WIDTH = 4
SSM_CHUNK = 128
D_FF = 2816
FFN_CONV_WIDTH = 3
PLE_DIM = 256
LN_EPS = 1e-5
RMS_EPS = 1e-5
ADAM_LR = 0.001
ADAM_B1 = 0.9
ADAM_B2 = 0.999
ADAM_EPS = 1e-08
ADAM_WD = 0.01
ADAM_STEP = 10
N_DEV = 8

F32 = jnp.float32
BF16 = jnp.bfloat16
NEG = -1e30
LANES = 128
SUBLANES = 8
PACK_W = 1024
VMEM_LIMIT = 48 * 1024 * 1024


def _dims():
    fox_dim = FOX_HEADS * FOX_HEAD_DIM
    ssm_inner = 2 * D_MODEL
    ssm_heads = ssm_inner // SSM_HEAD_DIM
    conv_ch = ssm_inner + 2 * SSM_GROUPS * SSM_STATE
    return dict(fox_dim=fox_dim, even_in=3 * CONV_DIM + 3 * fox_dim + FOX_HEADS, even_mix=CONV_DIM + fox_dim,
                ssm_inner=ssm_inner, ssm_heads=ssm_heads, conv_ch=conv_ch, odd_in=ssm_inner + conv_ch + ssm_heads)


def _alpha():
    return (2.0 * DEPTH) ** 0.25


def _pick(dim, prefs):
    for p in prefs:
        if dim % p == 0:
            return p
    return dim


def _pcall(body, **kw):
    return pl.pallas_call(body, **kw)


def _cparams(sem=None, **kw):
    if sem is not None:
        kw["dimension_semantics"] = sem
    return pltpu.CompilerParams(vmem_limit_bytes=VMEM_LIMIT, **kw)


def _sigmoid(x):
    return 1.0 / (1.0 + jnp.exp(-x))


def _softplus(x):
    return jnp.maximum(x, 0.0) + jnp.log(1.0 + jnp.exp(-jnp.abs(x)))


def _sum8(x):
    n, c = x.shape
    return x.reshape(n // SUBLANES, SUBLANES, c).sum(axis=0)


def _dot(a, b, dims):
    return lax.dot_general(a, b, (dims, ((), ())), preferred_element_type=F32)


NN = ((1,), (0,))
NT = ((1,), (1,))
TN = ((0,), (0,))


def _split3(x):
    hi = x.astype(BF16)
    r1 = x - hi.astype(F32)
    mid = r1.astype(BF16)
    lo = (r1 - mid.astype(F32)).astype(BF16)
    return hi, mid, lo


def _tri_mm(tri_bf16, x, tri_first=True):
    if tri_first:
        return sum(_dot(tri_bf16, part, NN) for part in _split3(x))
    return sum(_dot(part, tri_bf16, NN) for part in _split3(x))


def _tri(n, upper=False):
    r = lax.broadcasted_iota(jnp.int32, (n, n), 0)
    c = lax.broadcasted_iota(jnp.int32, (n, n), 1)
    return jnp.where((r <= c) if upper else (r >= c), 1.0, 0.0).astype(BF16)


def _shift_down(cur, prev8, k):
    if k == 0:
        return cur
    ext = jnp.concatenate([prev8, cur], axis=0)
    return pltpu.roll(ext, k, axis=0)[SUBLANES:]


def _shift_up(cur, next8, k):
    if k == 0:
        return cur
    n = cur.shape[0]
    ext = jnp.concatenate([cur, next8], axis=0)
    return pltpu.roll(ext, n + SUBLANES - k, axis=0)[:n]


def _mm(a, b, mode, out_dtype, name, add=None, add_scale=1.0):
    if mode == "nn":
        (M, K), (K2, N) = a.shape, b.shape
    elif mode == "nt":
        (M, K), (N, K2) = a.shape, b.shape
    else:
        (K, M), (K2, N) = a.shape, b.shape
    assert K == K2, (a.shape, b.shape, mode)
    tm = _pick(M, (512, 256, 128))
    tn = _pick(N, (512, 384, 256, 128))
    tk = _pick(K, (1024, 512, 256, 128))
    nk = K // tk
    dims = {"nn": NN, "nt": NT, "tn": TN}[mode]
    a_spec = pl.BlockSpec((tk, tm), lambda i, j, k: (k, i)) if mode == "tn" else pl.BlockSpec((tm, tk), lambda i, j, k: (i, k))
    b_spec = pl.BlockSpec((tn, tk), lambda i, j, k: (j, k)) if mode == "nt" else pl.BlockSpec((tk, tn), lambda i, j, k: (k, j))
    o_spec = pl.BlockSpec((tm, tn), lambda i, j, k: (i, j))
    has_add = add is not None

    def body(*refs):
        if has_add:
            a_ref, b_ref, add_ref, o_ref, acc = refs
        else:
            a_ref, b_ref, o_ref, acc = refs
        k = pl.program_id(2)

        @pl.when(k == 0)
        def _():
            if has_add:
                acc[...] = add_scale * add_ref[...].astype(F32)
            else:
                acc[...] = jnp.zeros_like(acc)

        acc[...] += _dot(a_ref[...].astype(BF16), b_ref[...].astype(BF16), dims)

        @pl.when(k == nk - 1)
        def _():
            o_ref[...] = acc[...].astype(out_dtype)

    ins = [a, b] + ([add] if has_add else [])
    specs = [a_spec, b_spec] + ([o_spec] if has_add else [])
    return _pcall(body, name=name, grid=(M // tm, N // tn, nk), in_specs=specs, out_specs=o_spec,
                  out_shape=jax.ShapeDtypeStruct((M, N), out_dtype), scratch_shapes=[pltpu.VMEM((tm, tn), F32)],
                  compiler_params=_cparams(("parallel", "parallel", "arbitrary")))(*ins)


def _row_tile(L):
    return _pick(L, (256, 128))


def _ln_fwd(h, mix, g, b, name):
    L, D = h.shape
    tl = _row_tile(L)
    alpha = _alpha()

    def body(h_ref, m_ref, g_ref, b_ref, r_ref, y_ref):
        r = alpha * h_ref[...] + m_ref[...]
        mu = jnp.mean(r, axis=-1, keepdims=True)
        xc = r - mu
        var = jnp.mean(xc * xc, axis=-1, keepdims=True)
        r_ref[...] = r
        y_ref[...] = xc * lax.rsqrt(var + LN_EPS) * g_ref[...] + b_ref[...]

    row = pl.BlockSpec((tl, D), lambda i: (i, 0))
    vec = pl.BlockSpec((1, D), lambda i: (0, 0))
    return _pcall(body, name=name, grid=(L // tl,), in_specs=[row, row, vec, vec], out_specs=[row, row],
                  out_shape=[jax.ShapeDtypeStruct((L, D), F32)] * 2, compiler_params=_cparams(("parallel",)))(
        h, mix, g.reshape(1, D), b.reshape(1, D))


def _ln_bwd(r, dy, g, name):
    L, D = r.shape
    tl = _row_tile(L)

    def body(r_ref, dy_ref, g_ref, dr_ref, dg_ref, db_ref):
        i = pl.program_id(0)
        r_ = r_ref[...]
        dy_ = dy_ref[...]
        mu = jnp.mean(r_, axis=-1, keepdims=True)
        xc = r_ - mu
        rstd = lax.rsqrt(jnp.mean(xc * xc, axis=-1, keepdims=True) + LN_EPS)
        xhat = xc * rstd
        dxh = dy_ * g_ref[...]
        dr_ref[...] = rstd * (dxh - jnp.mean(dxh, axis=-1, keepdims=True) - xhat * jnp.mean(dxh * xhat, axis=-1, keepdims=True))

        @pl.when(i == 0)
        def _():
            dg_ref[...] = jnp.zeros_like(dg_ref)
            db_ref[...] = jnp.zeros_like(db_ref)

        dg_ref[...] += _sum8(dy_ * xhat)
        db_ref[...] += _sum8(dy_)

    row = pl.BlockSpec((tl, D), lambda i: (i, 0))
    vec = pl.BlockSpec((1, D), lambda i: (0, 0))
    acc = pl.BlockSpec((SUBLANES, D), lambda i: (0, 0))
    return _pcall(body, name=name, grid=(L // tl,), in_specs=[row, row, vec], out_specs=[row, acc, acc],
                  out_shape=[jax.ShapeDtypeStruct((L, D), F32), jax.ShapeDtypeStruct((SUBLANES, D), F32),
                             jax.ShapeDtypeStruct((SUBLANES, D), F32)],
                  compiler_params=_cparams(("arbitrary",)))(r, dy, g.reshape(1, D))


def _ple_fwd(h2, G, bg, E, name):
    L, D = h2.shape
    tl = _row_tile(L)

    def body(h_ref, g_ref, b_ref, e_ref, o_ref):
        o_ref[...] = h_ref[...] + _sigmoid(g_ref[...] + b_ref[...]) * e_ref[...]

    row = pl.BlockSpec((tl, D), lambda i: (i, 0))
    vec = pl.BlockSpec((1, D), lambda i: (0, 0))
    return _pcall(body, name=name, grid=(L // tl,), in_specs=[row, row, vec, row], out_specs=row,
                  out_shape=jax.ShapeDtypeStruct((L, D), F32), compiler_params=_cparams(("parallel",)))(
        h2, G, bg.reshape(1, D), E)


def _ple_bwd(dh3, G, bg, E, name):
    L, D = dh3.shape
    tl = _row_tile(L)

    def body(d_ref, g_ref, b_ref, e_ref, de_ref, dg_ref, db_ref):
        i = pl.program_id(0)
        d = d_ref[...]
        sg = _sigmoid(g_ref[...] + b_ref[...])
        de_ref[...] = (d * sg).astype(BF16)
        dgp = d * e_ref[...] * sg * (1.0 - sg)
        dg_ref[...] = dgp.astype(BF16)

        @pl.when(i == 0)
        def _():
            db_ref[...] = jnp.zeros_like(db_ref)

        db_ref[...] += _sum8(dgp)

    row = pl.BlockSpec((tl, D), lambda i: (i, 0))
    vec = pl.BlockSpec((1, D), lambda i: (0, 0))
    acc = pl.BlockSpec((SUBLANES, D), lambda i: (0, 0))
    return _pcall(body, name=name, grid=(L // tl,), in_specs=[row, row, vec, row], out_specs=[row, row, acc],
                  out_shape=[jax.ShapeDtypeStruct((L, D), BF16), jax.ShapeDtypeStruct((L, D), BF16),
                             jax.ShapeDtypeStruct((SUBLANES, D), F32)],
                  compiler_params=_cparams(("arbitrary",)))(dh3, G, bg.reshape(1, D), E)


def _loss_head(h, target, name):
    L, D = h.shape
    tl = _row_tile(L)

    def body(h_ref, t_ref, d_ref, s_ref):
        i = pl.program_id(0)
        e = h_ref[...] - t_ref[...]
        d_ref[...] = e * (1.0 / D)

        @pl.when(i == 0)
        def _():
            s_ref[...] = jnp.zeros_like(s_ref)

        s_ref[...] += _sum8(e * e)

    row = pl.BlockSpec((tl, D), lambda i: (i, 0))
    acc = pl.BlockSpec((SUBLANES, D), lambda i: (0, 0))
    return _pcall(body, name=name, grid=(L // tl,), in_specs=[row, row], out_specs=[row, acc],
                  out_shape=[jax.ShapeDtypeStruct((L, D), F32), jax.ShapeDtypeStruct((SUBLANES, D), F32)],
                  compiler_params=_cparams(("arbitrary",)))(h, target)


def _halo_prev(tl, ncol_blocks_fn):
    return lambda j, i: (jnp.maximum(i * (tl // SUBLANES) - 1, 0), ncol_blocks_fn(j))


def _conv_taps(cur, prev, w_ref, K):
    acc = w_ref[K - 1:K, :] * cur
    for k in range(K - 1):
        acc = acc + w_ref[k:k + 1, :] * _shift_down(cur, prev, K - 1 - k)
    return acc


def _ffn_act_fwd(U, w, b, name):
    L, F2 = U.shape
    F = F2 // 2
    K = w.shape[0]
    tl = _row_tile(L)
    tc = _pick(F, (256, 128))

    def body(u_ref, up_ref, w_ref, b_ref, s_ref):
        i = pl.program_id(1)
        cur = u_ref[...]
        prev = jnp.where(i == 0, 0.0, up_ref[...])
        uc = _conv_taps(cur, prev, w_ref, K) + b_ref[...]
        g, v = uc[:, :tc], uc[:, tc:]
        s_ref[...] = (g * _sigmoid(g) * v).astype(BF16)

    return _pcall(body, name=name, grid=(F // tc, L // tl),
                  in_specs=[pl.BlockSpec((tl, 2 * tc), lambda j, i: (i, j)),
                            pl.BlockSpec((SUBLANES, 2 * tc), _halo_prev(tl, lambda j: j)),
                            pl.BlockSpec((K, 2 * tc), lambda j, i: (0, j)),
                            pl.BlockSpec((1, 2 * tc), lambda j, i: (0, j))],
                  out_specs=pl.BlockSpec((tl, tc), lambda j, i: (i, j)),
                  out_shape=jax.ShapeDtypeStruct((L, F), BF16),
                  compiler_params=_cparams(("parallel", "parallel")))(U, U, w, b.reshape(1, F2))


def _ffn_act_bwd(U, dS, w, b, name):
    L, F2 = U.shape
    F = F2 // 2
    K = w.shape[0]
    tl = _row_tile(L)
    tc = _pick(F, (256, 128))

    def body(u_ref, up_ref, ds_ref, w_ref, b_ref, du_ref, dw_ref, db_ref):
        i = pl.program_id(1)
        cur = u_ref[...]
        prev = jnp.where(i == 0, 0.0, up_ref[...])
        uc = _conv_taps(cur, prev, w_ref, K) + b_ref[...]
        g, v = uc[:, :tc], uc[:, tc:]
        ds = ds_ref[...].astype(F32)
        sg = _sigmoid(g)
        dg = ds * v * sg * (1.0 + g * (1.0 - sg))
        dv = ds * g * sg
        duc = jnp.concatenate([dg, dv], axis=1)
        du_ref[...] = duc

        @pl.when(i == 0)
        def _():
            dw_ref[...] = jnp.zeros_like(dw_ref)
            db_ref[...] = jnp.zeros_like(db_ref)

        db_ref[...] += _sum8(duc)
        for k in range(K):
            dw_ref[k * SUBLANES:(k + 1) * SUBLANES, :] += _sum8(duc * _shift_down(cur, prev, K - 1 - k))

    return _pcall(body, name=name, grid=(F // tc, L // tl),
                  in_specs=[pl.BlockSpec((tl, 2 * tc), lambda j, i: (i, j)),
                            pl.BlockSpec((SUBLANES, 2 * tc), _halo_prev(tl, lambda j: j)),
                            pl.BlockSpec((tl, tc), lambda j, i: (i, j)),
                            pl.BlockSpec((K, 2 * tc), lambda j, i: (0, j)),
                            pl.BlockSpec((1, 2 * tc), lambda j, i: (0, j))],
                  out_specs=[pl.BlockSpec((tl, 2 * tc), lambda j, i: (i, j)),
                             pl.BlockSpec((K * SUBLANES, 2 * tc), lambda j, i: (0, j)),
                             pl.BlockSpec((SUBLANES, 2 * tc), lambda j, i: (0, j))],
                  out_shape=[jax.ShapeDtypeStruct((L, F2), F32), jax.ShapeDtypeStruct((K * SUBLANES, F2), F32),
                             jax.ShapeDtypeStruct((SUBLANES, F2), F32)],
                  compiler_params=_cparams(("parallel", "arbitrary")))(U, U, dS, w, b.reshape(1, F2))


def _conv_t(dy, w, name):
    L, C = dy.shape
    K = w.shape[0]
    tl = _row_tile(L)
    tc = _pick(C, (512, 384, 256, 128))
    nl = L // tl

    def body(d_ref, dn_ref, w_ref, o_ref):
        i = pl.program_id(1)
        cur = d_ref[...]
        nxt = jnp.where(i == nl - 1, 0.0, dn_ref[...])
        acc = w_ref[K - 1:K, :] * cur
        for k in range(K - 1):
            acc = acc + w_ref[k:k + 1, :] * _shift_up(cur, nxt, K - 1 - k)
        o_ref[...] = acc

    return _pcall(body, name=name, grid=(C // tc, nl),
                  in_specs=[pl.BlockSpec((tl, tc), lambda j, i: (i, j)),
                            pl.BlockSpec((SUBLANES, tc), lambda j, i: (jnp.minimum((i + 1) * (tl // SUBLANES), L // SUBLANES - 1), j)),
                            pl.BlockSpec((K, tc), lambda j, i: (0, j))],
                  out_specs=pl.BlockSpec((tl, tc), lambda j, i: (i, j)),
                  out_shape=jax.ShapeDtypeStruct((L, C), F32),
                  compiler_params=_cparams(("parallel", "parallel")))(dy, dy, w)


def _sconv_fwd(Ac, w, name):
    L, C3 = Ac.shape
    C = C3 // 3
    K = w.shape[0]
    tl = _row_tile(L)
    tc = LANES

    def body(a_ref, ap_ref, w_ref, y_ref):
        i = pl.program_id(1)
        a = a_ref[...]
        ap = ap_ref[...]
        p = a[:, tc:2 * tc] * a[:, 2 * tc:]
        pp = jnp.where(i == 0, 0.0, ap[:, tc:2 * tc] * ap[:, 2 * tc:])
        y_ref[...] = (a[:, :tc] * _conv_taps(p, pp, w_ref, K)).astype(BF16)

    return _pcall(body, name=name, grid=(C // tc, L // tl),
                  in_specs=[pl.BlockSpec((tl, 3 * tc), lambda j, i: (i, j)),
                            pl.BlockSpec((SUBLANES, 3 * tc), _halo_prev(tl, lambda j: j)),
                            pl.BlockSpec((K, tc), lambda j, i: (0, j))],
                  out_specs=pl.BlockSpec((tl, tc), lambda j, i: (i, j)),
                  out_shape=jax.ShapeDtypeStruct((L, C), BF16),
                  compiler_params=_cparams(("parallel", "parallel")))(Ac, Ac, w)


def _sconv_bwd_dc(Ac, dy, w, name):
    L, C3 = Ac.shape
    C = C3 // 3
    K = w.shape[0]
    tl = _row_tile(L)
    tc = LANES

    def body(a_ref, ap_ref, dy_ref, dc_ref, dw_ref):
        i = pl.program_id(1)
        a = a_ref[...]
        ap = ap_ref[...]
        p = a[:, tc:2 * tc] * a[:, 2 * tc:]
        pp = jnp.where(i == 0, 0.0, ap[:, tc:2 * tc] * ap[:, 2 * tc:])
        dc = dy_ref[...] * a[:, :tc]
        dc_ref[...] = dc

        @pl.when(i == 0)
        def _():
            dw_ref[...] = jnp.zeros_like(dw_ref)

        for k in range(K):
            dw_ref[k * SUBLANES:(k + 1) * SUBLANES, :] += _sum8(dc * _shift_down(p, pp, K - 1 - k))

    return _pcall(body, name=name, grid=(C // tc, L // tl),
                  in_specs=[pl.BlockSpec((tl, 3 * tc), lambda j, i: (i, j)),
                            pl.BlockSpec((SUBLANES, 3 * tc), _halo_prev(tl, lambda j: j)),
                            pl.BlockSpec((tl, tc), lambda j, i: (i, j))],
                  out_specs=[pl.BlockSpec((tl, tc), lambda j, i: (i, j)),
                             pl.BlockSpec((K * SUBLANES, tc), lambda j, i: (0, j))],
                  out_shape=[jax.ShapeDtypeStruct((L, C), F32), jax.ShapeDtypeStruct((K * SUBLANES, C), F32)],
                  compiler_params=_cparams(("parallel", "arbitrary")))(Ac, Ac, dy)


def _sconv_bwd_da(Ac, dy, dc, w, name):
    L, C3 = Ac.shape
    C = C3 // 3
    K = w.shape[0]
    tl = _row_tile(L)
    tc = LANES
    nl = L // tl

    def body(a_ref, ap_ref, dy_ref, dc_ref, dcn_ref, w_ref, o_ref):
        i = pl.program_id(1)
        a = a_ref[...]
        ap = ap_ref[...]
        gc, h = a[:, tc:2 * tc], a[:, 2 * tc:]
        p = gc * h
        pp = jnp.where(i == 0, 0.0, ap[:, tc:2 * tc] * ap[:, 2 * tc:])
        dgb = dy_ref[...] * _conv_taps(p, pp, w_ref, K)
        cur = dc_ref[...]
        nxt = jnp.where(i == nl - 1, 0.0, dcn_ref[...])
        dp = w_ref[K - 1:K, :] * cur
        for k in range(K - 1):
            dp = dp + w_ref[k:k + 1, :] * _shift_up(cur, nxt, K - 1 - k)
        o_ref[...] = jnp.concatenate([dgb, dp * h, dp * gc], axis=1)

    return _pcall(body, name=name, grid=(C // tc, nl),
                  in_specs=[pl.BlockSpec((tl, 3 * tc), lambda j, i: (i, j)),
                            pl.BlockSpec((SUBLANES, 3 * tc), _halo_prev(tl, lambda j: j)),
                            pl.BlockSpec((tl, tc), lambda j, i: (i, j)),
                            pl.BlockSpec((tl, tc), lambda j, i: (i, j)),
                            pl.BlockSpec((SUBLANES, tc), lambda j, i: (jnp.minimum((i + 1) * (tl // SUBLANES), L // SUBLANES - 1), j)),
                            pl.BlockSpec((K, tc), lambda j, i: (0, j))],
                  out_specs=pl.BlockSpec((tl, 3 * tc), lambda j, i: (i, j)),
                  out_shape=jax.ShapeDtypeStruct((L, C3), F32),
                  compiler_params=_cparams(("parallel", "parallel")))(Ac, Ac, dy, dc, dc, w)


def _fox_gate_fwd(Af, bf, name):
    L, W = Af.shape
    tl = _pick(L, (512, 256, 128))

    def body(a_ref, b_ref, f_ref, carry):
        i = pl.program_id(0)

        @pl.when(i == 0)
        def _():
            carry[...] = jnp.zeros_like(carry)

        z = a_ref[...] + b_ref[...]
        logf = jnp.minimum(z, 0.0) - jnp.log(1.0 + jnp.exp(-jnp.abs(z)))
        f = _tri_mm(_tri(tl), logf) + carry[...]
        f_ref[...] = f
        carry[...] = f[tl - 1:tl, :]

    row = pl.BlockSpec((tl, W), lambda i: (i, 0))
    return _pcall(body, name=name, grid=(L // tl,), in_specs=[row, pl.BlockSpec((1, W), lambda i: (0, 0))], out_specs=row,
                  out_shape=jax.ShapeDtypeStruct((L, W), F32), scratch_shapes=[pltpu.VMEM((1, W), F32)],
                  compiler_params=_cparams(("arbitrary",)))(Af, bf)


def _fox_gate_bwd(Af, bf, dF, name):
    L, W = Af.shape
    tl = _pick(L, (512, 256, 128))
    nl = L // tl

    def body(a_ref, b_ref, df_ref, o_ref, db_ref, carry):
        i = pl.program_id(0)

        @pl.when(i == 0)
        def _():
            carry[...] = jnp.zeros_like(carry)
            db_ref[...] = jnp.zeros_like(db_ref)

        z = a_ref[...] + b_ref[...]
        dlogf = _tri_mm(_tri(tl, upper=True), df_ref[...]) + carry[...]
        carry[...] = dlogf[0:1, :]
        dz = dlogf * _sigmoid(-z)
        o_ref[...] = dz
        db_ref[...] += _sum8(dz)

    row = pl.BlockSpec((tl, W), lambda i: (nl - 1 - i, 0))
    return _pcall(body, name=name, grid=(nl,),
                  in_specs=[row, pl.BlockSpec((1, W), lambda i: (0, 0)), row],
                  out_specs=[row, pl.BlockSpec((SUBLANES, W), lambda i: (0, 0))],
                  out_shape=[jax.ShapeDtypeStruct((L, W), F32), jax.ShapeDtypeStruct((SUBLANES, W), F32)],
                  scratch_shapes=[pltpu.VMEM((1, W), F32)],
                  compiler_params=_cparams(("arbitrary",)))(Af, bf, dF)


def _attn_tiles(L):
    t = _pick(L, (512, 256, 128))
    return t, t


def _attn_scores(q, k, fq, fk, qi, ki, tq, tk, scale):
    s = _dot(q, k, NT) * scale + fq - fk
    row = lax.broadcasted_iota(jnp.int32, (tq, tk), 0) + qi * tq
    col = lax.broadcasted_iota(jnp.int32, (tq, tk), 1) + ki * tk
    return jnp.where(col <= row, s, NEG)


def _attn_fwd(q, k, v, Fq, Fk, name):
    H, L, Dh = q.shape
    tq, tk = _attn_tiles(L)
    scale = Dh ** -0.5

    def body(q_ref, k_ref, v_ref, fq_ref, fk_ref, o_ref, lse_ref, m_s, l_s, acc_s):
        qi, ki = pl.program_id(1), pl.program_id(2)

        @pl.when(ki == 0)
        def _():
            m_s[...] = jnp.full_like(m_s, NEG)
            l_s[...] = jnp.zeros_like(l_s)
            acc_s[...] = jnp.zeros_like(acc_s)

        @pl.when(ki <= qi)
        def _():
            s = _attn_scores(q_ref[...], k_ref[...], fq_ref[...], fk_ref[...], qi, ki, tq, tk, scale)
            m_new = jnp.maximum(m_s[...], jnp.max(s, axis=-1, keepdims=True))
            p = jnp.exp(s - m_new)
            a = jnp.exp(m_s[...] - m_new)
            l_s[...] = a * l_s[...] + jnp.sum(p, axis=-1, keepdims=True)
            acc_s[...] = a * acc_s[...] + _dot(p.astype(BF16), v_ref[...], NN)
            m_s[...] = m_new

        @pl.when(ki == qi)
        def _():
            o_ref[...] = acc_s[...] / l_s[...]
            lse_ref[...] = m_s[...] + jnp.log(l_s[...])

    qspec = pl.BlockSpec((None, tq, Dh), lambda h, i, j: (h, i, 0))
    kspec = pl.BlockSpec((None, tk, Dh), lambda h, i, j: (h, jnp.minimum(j, i), 0))
    return _pcall(body, name=name, grid=(H, L // tq, L // tk),
                  in_specs=[qspec, kspec, kspec,
                            pl.BlockSpec((None, tq, 1), lambda h, i, j: (h, i, 0)),
                            pl.BlockSpec((None, 1, tk), lambda h, i, j: (h, 0, jnp.minimum(j, i)))],
                  out_specs=[qspec, pl.BlockSpec((None, tq, 1), lambda h, i, j: (h, i, 0))],
                  out_shape=[jax.ShapeDtypeStruct((H, L, Dh), F32), jax.ShapeDtypeStruct((H, L, 1), F32)],
                  scratch_shapes=[pltpu.VMEM((tq, 1), F32), pltpu.VMEM((tq, 1), F32), pltpu.VMEM((tq, Dh), F32)],
                  compiler_params=_cparams(("parallel", "parallel", "arbitrary")))(q, k, v, Fq, Fk)


def _attn_bwd_delta(q, k, v, Fq, Fk, lse, do, name):
    H, L, Dh = q.shape
    tq, tk = _attn_tiles(L)
    scale = Dh ** -0.5

    def body(q_ref, k_ref, v_ref, fq_ref, fk_ref, lse_ref, do_ref, d_ref, acc_s):
        qi, ki = pl.program_id(1), pl.program_id(2)

        @pl.when(ki == 0)
        def _():
            acc_s[...] = jnp.zeros_like(acc_s)

        @pl.when(ki <= qi)
        def _():
            s = _attn_scores(q_ref[...], k_ref[...], fq_ref[...], fk_ref[...], qi, ki, tq, tk, scale)
            p = jnp.exp(s - lse_ref[...])
            dp = _dot(do_ref[...].astype(BF16), v_ref[...], NT)
            acc_s[...] += jnp.sum(p * dp, axis=-1, keepdims=True)

        @pl.when(ki == qi)
        def _():
            d_ref[...] = acc_s[...]

    qspec = pl.BlockSpec((None, tq, Dh), lambda h, i, j: (h, i, 0))
    kspec = pl.BlockSpec((None, tk, Dh), lambda h, i, j: (h, jnp.minimum(j, i), 0))
    cq = pl.BlockSpec((None, tq, 1), lambda h, i, j: (h, i, 0))
    return _pcall(body, name=name, grid=(H, L // tq, L // tk),
                  in_specs=[qspec, kspec, kspec, cq,
                            pl.BlockSpec((None, 1, tk), lambda h, i, j: (h, 0, jnp.minimum(j, i))), cq, qspec],
                  out_specs=cq, out_shape=jax.ShapeDtypeStruct((H, L, 1), F32),
                  scratch_shapes=[pltpu.VMEM((tq, 1), F32)],
                  compiler_params=_cparams(("parallel", "parallel", "arbitrary")))(q, k, v, Fq, Fk, lse, do)


def _attn_bwd_dq(q, k, v, Fq, Fk, delta, lse, do, name):
    H, L, Dh = q.shape
    tq, tk = _attn_tiles(L)
    scale = Dh ** -0.5

    def body(q_ref, k_ref, v_ref, fq_ref, fk_ref, dl_ref, lse_ref, do_ref, dq_ref, acc_s):
        qi, ki = pl.program_id(1), pl.program_id(2)

        @pl.when(ki == 0)
        def _():
            acc_s[...] = jnp.zeros_like(acc_s)

        @pl.when(ki <= qi)
        def _():
            s = _attn_scores(q_ref[...], k_ref[...], fq_ref[...], fk_ref[...], qi, ki, tq, tk, scale)
            p = jnp.exp(s - lse_ref[...])
            dp = _dot(do_ref[...].astype(BF16), v_ref[...], NT)
            ds = p * (dp - dl_ref[...])
            acc_s[...] += _dot(ds.astype(BF16), k_ref[...], NN)

        @pl.when(ki == qi)
        def _():
            dq_ref[...] = acc_s[...] * scale

    qspec = pl.BlockSpec((None, tq, Dh), lambda h, i, j: (h, i, 0))
    kspec = pl.BlockSpec((None, tk, Dh), lambda h, i, j: (h, jnp.minimum(j, i), 0))
    cq = pl.BlockSpec((None, tq, 1), lambda h, i, j: (h, i, 0))
    return _pcall(body, name=name, grid=(H, L // tq, L // tk),
                  in_specs=[qspec, kspec, kspec, cq,
                            pl.BlockSpec((None, 1, tk), lambda h, i, j: (h, 0, jnp.minimum(j, i))), cq, cq, qspec],
                  out_specs=qspec, out_shape=jax.ShapeDtypeStruct((H, L, Dh), F32),
                  scratch_shapes=[pltpu.VMEM((tq, Dh), F32)],
                  compiler_params=_cparams(("parallel", "parallel", "arbitrary")))(q, k, v, Fq, Fk, delta, lse, do)


def _attn_bwd_dkv(q, k, v, Fq, Fk, delta, lse, do, name):
    H, L, Dh = q.shape
    tq, tk = _attn_tiles(L)
    nq = L // tq
    scale = Dh ** -0.5

    def body(q_ref, k_ref, v_ref, fq_ref, fk_ref, dl_ref, lse_ref, do_ref, dk_ref, dv_ref, df_ref, dk_s, dv_s, df_s):
        ki, qi = pl.program_id(1), pl.program_id(2)

        @pl.when(qi == 0)
        def _():
            dk_s[...] = jnp.zeros_like(dk_s)
            dv_s[...] = jnp.zeros_like(dv_s)
            df_s[...] = jnp.zeros_like(df_s)

        @pl.when(qi >= ki)
        def _():
            s = _attn_scores(q_ref[...], k_ref[...], fq_ref[...], fk_ref[...], qi, ki, tq, tk, scale)
            p = jnp.exp(s - lse_ref[...])
            do_ = do_ref[...]
            dp = _dot(do_.astype(BF16), v_ref[...], NT)
            ds = p * (dp - dl_ref[...])
            dv_s[...] += _dot(p.astype(BF16), do_.astype(BF16), TN)
            dk_s[...] += _dot(ds.astype(BF16), q_ref[...], TN)
            df_s[...] -= jnp.sum(ds, axis=0, keepdims=True)

        @pl.when(qi == nq - 1)
        def _():
            dk_ref[...] = dk_s[...] * scale
            dv_ref[...] = dv_s[...]
            df_ref[...] = df_s[...]

    qspec = pl.BlockSpec((None, tq, Dh), lambda h, j, i: (h, jnp.maximum(i, j), 0))
    kspec = pl.BlockSpec((None, tk, Dh), lambda h, j, i: (h, j, 0))
    cq = pl.BlockSpec((None, tq, 1), lambda h, j, i: (h, jnp.maximum(i, j), 0))
    rk = pl.BlockSpec((None, 1, tk), lambda h, j, i: (h, 0, j))
    return _pcall(body, name=name, grid=(H, L // tk, nq),
                  in_specs=[qspec, kspec, kspec, cq, rk, cq, cq, qspec],
                  out_specs=[kspec, kspec, rk],
                  out_shape=[jax.ShapeDtypeStruct((H, L, Dh), F32), jax.ShapeDtypeStruct((H, L, Dh), F32),
                             jax.ShapeDtypeStruct((H, 1, L), F32)],
                  scratch_shapes=[pltpu.VMEM((tk, Dh), F32), pltpu.VMEM((tk, Dh), F32), pltpu.VMEM((1, tk), F32)],
                  compiler_params=_cparams(("parallel", "parallel", "arbitrary")))(q, k, v, Fq, Fk, delta, lse, do)


def _mconv_fwd(xr, w, b, name):
    L, C = xr.shape
    K = w.shape[0]
    tl = _row_tile(L)
    tc = _pick(C, (512, 384, 256, 128))

    def body(x_ref, xp_ref, w_ref, b_ref, o_ref):
        i = pl.program_id(1)
        prev = jnp.where(i == 0, 0.0, xp_ref[...])
        pre = _conv_taps(x_ref[...], prev, w_ref, K) + b_ref[...]
        o_ref[...] = pre * _sigmoid(pre)

    return _pcall(body, name=name, grid=(C // tc, L // tl),
                  in_specs=[pl.BlockSpec((tl, tc), lambda j, i: (i, j)),
                            pl.BlockSpec((SUBLANES, tc), _halo_prev(tl, lambda j: j)),
                            pl.BlockSpec((K, tc), lambda j, i: (0, j)),
                            pl.BlockSpec((1, tc), lambda j, i: (0, j))],
                  out_specs=pl.BlockSpec((tl, tc), lambda j, i: (i, j)),
                  out_shape=jax.ShapeDtypeStruct((L, C), F32),
                  compiler_params=_cparams(("parallel", "parallel")))(xr, xr, w, b.reshape(1, C))


def _mconv_bwd(xr, dact, w, b, name):
    L, C = xr.shape
    K = w.shape[0]
    tl = _row_tile(L)
    tc = _pick(C, (512, 384, 256, 128))

    def body(x_ref, xp_ref, d_ref, w_ref, b_ref, o_ref, dw_ref, db_ref):
        i = pl.program_id(1)
        cur = x_ref[...]
        prev = jnp.where(i == 0, 0.0, xp_ref[...])
        pre = _conv_taps(cur, prev, w_ref, K) + b_ref[...]
        sg = _sigmoid(pre)
        dpre = d_ref[...] * sg * (1.0 + pre * (1.0 - sg))
        o_ref[...] = dpre

        @pl.when(i == 0)
        def _():
            dw_ref[...] = jnp.zeros_like(dw_ref)
            db_ref[...] = jnp.zeros_like(db_ref)

        db_ref[...] += _sum8(dpre)
        for k in range(K):
            dw_ref[k * SUBLANES:(k + 1) * SUBLANES, :] += _sum8(dpre * _shift_down(cur, prev, K - 1 - k))

    return _pcall(body, name=name, grid=(C // tc, L // tl),
                  in_specs=[pl.BlockSpec((tl, tc), lambda j, i: (i, j)),
                            pl.BlockSpec((SUBLANES, tc), _halo_prev(tl, lambda j: j)),
                            pl.BlockSpec((tl, tc), lambda j, i: (i, j)),
                            pl.BlockSpec((K, tc), lambda j, i: (0, j)),
                            pl.BlockSpec((1, tc), lambda j, i: (0, j))],
                  out_specs=[pl.BlockSpec((tl, tc), lambda j, i: (i, j)),
                             pl.BlockSpec((K * SUBLANES, tc), lambda j, i: (0, j)),
                             pl.BlockSpec((SUBLANES, tc), lambda j, i: (0, j))],
                  out_shape=[jax.ShapeDtypeStruct((L, C), F32), jax.ShapeDtypeStruct((K * SUBLANES, C), F32),
                             jax.ShapeDtypeStruct((SUBLANES, C), F32)],
                  compiler_params=_cparams(("parallel", "arbitrary")))(xr, xr, dact, w, b.reshape(1, C))


def _ssd_common(x_r, dt_col, a_col, a_row, mask, CB):
    seg = jnp.where(mask, a_col - a_row, NEG)
    Lm = jnp.exp(seg)
    return Lm, CB * Lm, x_r * dt_col


def _ssd_fwd(x, Bm, Cm, dtc, dtr, bias_c, bias_r, alog_c, alog_r, dsk_c, name):
    Hs, L, P = x.shape
    G, _, N = Bm.shape
    R = Hs // G
    Q = SSM_CHUNK
    nc = L // Q

    def body(x_ref, b_ref, c_ref, dtc_ref, dtr_ref, bc_ref, br_ref, ac_ref, ar_ref, dk_ref, y_ref, hp_ref, st):
        c = pl.program_id(1)

        @pl.when(c == 0)
        def _():
            st[...] = jnp.zeros_like(st)

        dt_c = _softplus(dtc_ref[...] + bc_ref[...])
        dt_r = _softplus(dtr_ref[...] + br_ref[...])
        acs_c = _tri_mm(_tri(Q), dt_c * (-jnp.exp(ac_ref[...])))
        acs_r = _tri_mm(_tri(Q, upper=True), dt_r * (-jnp.exp(ar_ref[...])), tri_first=False)
        Bf = b_ref[...]
        Cb = c_ref[...].astype(BF16)
        CB = _dot(Cb, Bf.astype(BF16), NT)
        mask = lax.broadcasted_iota(jnp.int32, (Q, Q), 0) >= lax.broadcasted_iota(jnp.int32, (Q, Q), 1)
        for r in range(R):
            a_col = acs_c[:, r:r + 1]
            a_row = acs_r[r:r + 1, :]
            xr_ = x_ref[r]
            _, Gm, xt = _ssd_common(xr_, dt_c[:, r:r + 1], a_col, a_row, mask, CB)
            hp = st[r]
            hp_ref[r] = hp
            a_end = acs_c[Q - 1:Q, r:r + 1]
            dte = jnp.exp(a_end - a_col)
            ydiag = _dot(Gm.astype(BF16), xt.astype(BF16), NN)
            yoff = jnp.exp(a_col) * _dot(Cb, hp.astype(BF16), NT)
            st[r] = hp * jnp.exp(a_end) + _dot(xt.astype(BF16), (Bf * dte).astype(BF16), TN)
            y_ref[r] = ydiag + yoff + dk_ref[:, r:r + 1] * xr_

    pc = pl.BlockSpec((None, 1, R), lambda g, c: (g, 0, 0))
    pr = pl.BlockSpec((None, R, 1), lambda g, c: (g, 0, 0))
    xspec = pl.BlockSpec((R, Q, P), lambda g, c: (g, c, 0))
    bspec = pl.BlockSpec((None, Q, N), lambda g, c: (g, c, 0))
    return _pcall(body, name=name, grid=(G, nc),
                  in_specs=[xspec, bspec, bspec, pl.BlockSpec((None, Q, R), lambda g, c: (g, c, 0)),
                            pl.BlockSpec((None, R, Q), lambda g, c: (g, 0, c)), pc, pr, pc, pr, pc],
                  out_specs=[xspec, pl.BlockSpec((None, R, P, N), lambda g, c: (c, g, 0, 0))],
                  out_shape=[jax.ShapeDtypeStruct((Hs, L, P), F32), jax.ShapeDtypeStruct((nc, Hs, P, N), F32)],
                  scratch_shapes=[pltpu.VMEM((R, P, N), F32)],
                  compiler_params=_cparams(("parallel", "arbitrary")))(x, Bm, Cm, dtc, dtr, bias_c, bias_r, alog_c, alog_r, dsk_c)


def _ssd_bwd(x, Bm, Cm, dtc, dtr, bias_c, bias_r, alog_c, alog_r, dsk_c, hprev, dy, name):
    Hs, L, P = x.shape
    G, _, N = Bm.shape
    R = Hs // G
    Q = SSM_CHUNK
    nc = L // Q

    def body(x_ref, b_ref, c_ref, dtc_ref, dtr_ref, bc_ref, br_ref, ac_ref, ar_ref, dk_ref, hp_ref, dy_ref,
             dx_ref, db_ref, dc_ref, ddt_ref, gbias_ref, galog_ref, gdsk_ref, dst):
        c = pl.program_id(1)

        @pl.when(c == 0)
        def _():
            dst[...] = jnp.zeros_like(dst)
            gbias_ref[...] = jnp.zeros_like(gbias_ref)
            galog_ref[...] = jnp.zeros_like(galog_ref)
            gdsk_ref[...] = jnp.zeros_like(gdsk_ref)

        raw_c = dtc_ref[...] + bc_ref[...]
        dt_c = _softplus(raw_c)
        dt_r = _softplus(dtr_ref[...] + br_ref[...])
        A_c = -jnp.exp(ac_ref[...])
        acs_c = _tri_mm(_tri(Q), dt_c * A_c)
        acs_r = _tri_mm(_tri(Q, upper=True), dt_r * (-jnp.exp(ar_ref[...])), tri_first=False)
        Bf = b_ref[...]
        Cf = c_ref[...]
        Bb = Bf.astype(BF16)
        Cb = Cf.astype(BF16)
        CB = _dot(Cb, Bb, NT)
        mask = lax.broadcasted_iota(jnp.int32, (Q, Q), 0) >= lax.broadcasted_iota(jnp.int32, (Q, Q), 1)
        lane = lax.broadcasted_iota(jnp.int32, (Q, R), 1)
        lane1 = lax.broadcasted_iota(jnp.int32, (1, R), 1)
        rowi = lax.broadcasted_iota(jnp.int32, (Q, 1), 0)
        ones = jnp.ones((Q, LANES), F32)
        dCB = jnp.zeros((Q, Q), F32)
        dC = jnp.zeros((Q, N), F32)
        dB = jnp.zeros((Q, N), F32)
        da_mat = jnp.zeros((Q, R), F32)
        ddtx_mat = jnp.zeros((Q, R), F32)
        gd = jnp.zeros((1, R), F32)
        for r in range(R):
            a_col = acs_c[:, r:r + 1]
            a_row = acs_r[r:r + 1, :]
            xr_ = x_ref[r]
            dt_col = dt_c[:, r:r + 1]
            Lm, Gm, xt = _ssd_common(xr_, dt_col, a_col, a_row, mask, CB)
            dyr = dy_ref[r]
            dyb = dyr.astype(BF16)
            xtb = xt.astype(BF16)
            hp = hp_ref[r]
            hpb = hp.astype(BF16)
            dHn = dst[r]
            dHb = dHn.astype(BF16)
            a_end = acs_c[Q - 1:Q, r:r + 1]
            e_end = jnp.exp(a_end)
            ea = jnp.exp(a_col)
            dte = jnp.exp(a_end - a_col)
            dxt = _dot(Gm.astype(BF16), dyb, TN)
            dG = jnp.where(mask, _dot(dyb, xtb, NT), 0.0)
            Mm = dG * Gm
            dCB = dCB + dG * Lm
            colsum = lax.dot_general(Mm, ones, (TN, ((), ())), precision=lax.Precision.HIGHEST,
                                     preferred_element_type=F32)[:, :1]
            da = jnp.sum(Mm, axis=1, keepdims=True) - colsum
            edy = ea * dyr
            edyb = edy.astype(BF16)
            W = _dot(Cb, hpb, NT)
            dC = dC + _dot(edyb, hpb, NN)
            dhp = _dot(edyb, Cb, TN)
            da = da + jnp.sum(edy * W, axis=1, keepdims=True)
            dxt = dxt + _dot((Bf * dte).astype(BF16), dHb, NT)
            XH = _dot(xtb, dHb, NN)
            dB = dB + dte * XH
            t = jnp.sum(XH * Bf, axis=1, keepdims=True) * dte
            da = da - t
            da_end = jnp.sum(t, axis=0, keepdims=True) + jnp.sum(jnp.sum(dHn * hp, axis=1, keepdims=True), axis=0, keepdims=True) * e_end
            da = da + jnp.where(rowi == Q - 1, da_end, 0.0)
            dst[r] = dhp + dHn * e_end
            dsk = dk_ref[:, r:r + 1]
            dx_ref[r] = dxt * dt_col + dsk * dyr
            da_mat = jnp.where(lane == r, da, da_mat)
            ddtx_mat = jnp.where(lane == r, jnp.sum(dxt * xr_, axis=1, keepdims=True), ddtx_mat)
            gd = jnp.where(lane1 == r, jnp.sum(jnp.sum(dyr * xr_, axis=1, keepdims=True), axis=0, keepdims=True), gd)
        dCBb = dCB.astype(BF16)
        dc_ref[...] = dC + _dot(dCBb, Bb, NN)
        db_ref[...] = dB + _dot(dCBb, Cb, TN)
        ddtA = _tri_mm(_tri(Q, upper=True), da_mat)
        ddt_raw = (ddtA * A_c + ddtx_mat) * _sigmoid(raw_c)
        ddt_ref[...] = ddt_raw
        gbias_ref[...] += jnp.sum(ddt_raw, axis=0, keepdims=True)
        galog_ref[...] += jnp.sum(ddtA * dt_c, axis=0, keepdims=True) * A_c
        gdsk_ref[...] += gd

    rc = lambda c: nc - 1 - c
    pc = pl.BlockSpec((None, 1, R), lambda g, c: (g, 0, 0))
    pr = pl.BlockSpec((None, R, 1), lambda g, c: (g, 0, 0))
    xspec = pl.BlockSpec((R, Q, P), lambda g, c: (g, rc(c), 0))
    bspec = pl.BlockSpec((None, Q, N), lambda g, c: (g, rc(c), 0))
    dtspec = pl.BlockSpec((None, Q, R), lambda g, c: (g, rc(c), 0))
    return _pcall(body, name=name, grid=(G, nc),
                  in_specs=[xspec, bspec, bspec, dtspec, pl.BlockSpec((None, R, Q), lambda g, c: (g, 0, rc(c))),
                            pc, pr, pc, pr, pc, pl.BlockSpec((None, R, P, N), lambda g, c: (rc(c), g, 0, 0)), xspec],
                  out_specs=[xspec, bspec, bspec, dtspec, pc, pc, pc],
                  out_shape=[jax.ShapeDtypeStruct((Hs, L, P), F32), jax.ShapeDtypeStruct((G, L, N), F32),
                             jax.ShapeDtypeStruct((G, L, N), F32), jax.ShapeDtypeStruct((G, L, R), F32),
                             jax.ShapeDtypeStruct((G, 1, R), F32), jax.ShapeDtypeStruct((G, 1, R), F32),
                             jax.ShapeDtypeStruct((G, 1, R), F32)],
                  scratch_shapes=[pltpu.VMEM((R, P, N), F32)],
                  compiler_params=_cparams(("parallel", "arbitrary")))(
        x, Bm, Cm, dtc, dtr, bias_c, bias_r, alog_c, alog_r, dsk_c, hprev, dy)


def _gnorm_fwd(y, z, g, name):
    L, Dn = y.shape
    gs = Dn // SSM_GROUPS
    tl = _row_tile(L)

    def body(y_ref, z_ref, g_ref, o_ref):
        for k in range(SSM_GROUPS):
            sl = slice(k * gs, (k + 1) * gs)
            zz = z_ref[:, sl]
            u = y_ref[:, sl] * zz * _sigmoid(zz)
            rstd = lax.rsqrt(jnp.mean(u * u, axis=-1, keepdims=True) + RMS_EPS)
            o_ref[:, sl] = (u * rstd * g_ref[:, sl]).astype(BF16)

    row = pl.BlockSpec((tl, Dn), lambda i: (i, 0))
    return _pcall(body, name=name, grid=(L // tl,), in_specs=[row, row, pl.BlockSpec((1, Dn), lambda i: (0, 0))],
                  out_specs=row, out_shape=jax.ShapeDtypeStruct((L, Dn), BF16),
                  compiler_params=_cparams(("parallel",)))(y, z, g.reshape(1, Dn))


def _gnorm_bwd(y, z, g, dout, name):
    L, Dn = y.shape
    gs = Dn // SSM_GROUPS
    tl = _row_tile(L)

    def body(y_ref, z_ref, g_ref, d_ref, dy_ref, dz_ref, dg_ref):
        i = pl.program_id(0)

        @pl.when(i == 0)
        def _():
            dg_ref[...] = jnp.zeros_like(dg_ref)

        for k in range(SSM_GROUPS):
            sl = slice(k * gs, (k + 1) * gs)
            zz = z_ref[:, sl]
            yy = y_ref[:, sl]
            sg = _sigmoid(zz)
            sil = zz * sg
            u = yy * sil
            rstd = lax.rsqrt(jnp.mean(u * u, axis=-1, keepdims=True) + RMS_EPS)
            n = u * rstd
            d = d_ref[:, sl]
            dn = d * g_ref[:, sl]
            du = rstd * (dn - n * jnp.mean(dn * n, axis=-1, keepdims=True))
            dy_ref[:, sl] = du * sil
            dz_ref[:, sl] = du * yy * sg * (1.0 + zz * (1.0 - sg))
            dg_ref[:, sl] += _sum8(d * n)

    row = pl.BlockSpec((tl, Dn), lambda i: (i, 0))
    return _pcall(body, name=name, grid=(L // tl,), in_specs=[row, row, pl.BlockSpec((1, Dn), lambda i: (0, 0)), row],
                  out_specs=[row, row, pl.BlockSpec((SUBLANES, Dn), lambda i: (0, 0))],
                  out_shape=[jax.ShapeDtypeStruct((L, Dn), F32), jax.ShapeDtypeStruct((L, Dn), F32),
                             jax.ShapeDtypeStruct((SUBLANES, Dn), F32)],
                  compiler_params=_cparams(("arbitrary",)))(y, z, g.reshape(1, Dn), dout)


def _adamw(w, g, m, v, name):
    rows, W = w.shape
    tr = _pick(rows, (512, 256, 128, 64, 32, 16, 8))
    c1 = 1.0 / (1.0 - ADAM_B1 ** ADAM_STEP)
    c2 = 1.0 / (1.0 - ADAM_B2 ** ADAM_STEP)

    def body(w_ref, g_ref, m_ref, v_ref, d_ref, nm_ref, nv_ref):
        g_ = g_ref[...]
        nm = ADAM_B1 * m_ref[...] + (1.0 - ADAM_B1) * g_
        nv = ADAM_B2 * v_ref[...] + (1.0 - ADAM_B2) * (g_ * g_)
        nm_ref[...] = nm
        nv_ref[...] = nv
        d_ref[...] = -ADAM_LR * ((nm * c1) / (jnp.sqrt(nv * c2) + ADAM_EPS) + ADAM_WD * w_ref[...])

    blk = pl.BlockSpec((tr, W), lambda i: (i, 0))
    return _pcall(body, name=name, grid=(rows // tr,), in_specs=[blk] * 4, out_specs=[blk] * 3,
                  out_shape=[jax.ShapeDtypeStruct((rows, W), F32)] * 3, compiler_params=_cparams(("parallel",)))(w, g, m, v)


def _sum_slots(x, name, extra=None):
    n, rows, W = x.shape
    tr = _pick(rows, (512, 256, 128, 64, 32, 16, 8))
    has_extra = extra is not None

    def body(*refs):
        if has_extra:
            e_ref, x_ref, o_ref = refs
            acc = e_ref[...].astype(F32)
            start = 0
        else:
            x_ref, o_ref = refs
            acc = x_ref[0].astype(F32)
            start = 1
        for s in range(start, n):
            acc = acc + x_ref[s].astype(F32)
        o_ref[...] = acc

    blk = pl.BlockSpec((tr, W), lambda i: (i, 0))
    xblk = pl.BlockSpec((n, tr, W), lambda i: (0, i, 0))
    return _pcall(body, name=name, grid=(rows // tr,), in_specs=([blk] if has_extra else []) + [xblk], out_specs=blk,
                  out_shape=jax.ShapeDtypeStruct((rows, W), F32), compiler_params=_cparams(("parallel",)))(
        *(([extra] if has_extra else []) + [x]))


def _add_pairs(a, b, name):
    n, rows, W = a.shape
    tr = _pick(rows, (512, 256, 128, 64, 32, 16, 8))

    def body(a_ref, b_ref, o_ref):
        o_ref[...] = (a_ref[...].astype(F32) + b_ref[...].astype(F32)).astype(BF16)

    blk = pl.BlockSpec((None, tr, W), lambda s, i: (s, i, 0))
    return _pcall(body, name=name, grid=(n, rows // tr), in_specs=[blk, blk], out_specs=blk,
                  out_shape=jax.ShapeDtypeStruct((n, rows, W), BF16), compiler_params=_cparams(("parallel", "parallel")))(a, b)


MESH = pl.DeviceIdType.MESH
HBM_SPEC = pl.BlockSpec(memory_space=pl.ANY)


def _me():
    return lax.axis_index("x"), lax.axis_index("y"), lax.axis_index("c")


def _all_gather(arrs, name):
    n = len(arrs)

    def body(*refs):
        ins, outs = refs[:n], refs[n:2 * n]
        send_sems, recv_sems, local_sems = refs[2 * n:]
        x, y, c = _me()
        me, sib = (x, y, c), (x, y, 1 - c)
        chips = [(1 - x, y), (x, 1 - y), (1 - x, 1 - y)]

        def slot(a, dev):
            return outs[a].at[4 * dev[0] + 2 * dev[1] + dev[2]]

        def copy(a, k, block, to, src=None):
            return pltpu.make_async_remote_copy(src_ref=slot(a, block) if src is None else src, dst_ref=slot(a, block),
                                                send_sem=send_sems.at[a * 7 + k], recv_sem=recv_sems.at[a * 7 + k],
                                                device_id=to, device_id_type=MESH)

        mine = [pltpu.make_async_copy(ins[a], slot(a, me), local_sems.at[a]) for a in range(n)]
        for cp in mine:
            cp.start()
        first = []
        for a in range(n):
            first.append(copy(a, 0, me, sib, src=ins[a]))
            first += [copy(a, 1 + j, me, (*chip, c), src=ins[a]) for j, chip in enumerate(chips)]
        for cp in first:
            cp.start()
        passed = []
        for j, chip in enumerate(chips):
            for a in range(n):
                copy(a, 1 + j, (*chip, c), me).wait_recv()
                fw = copy(a, 4 + j, (*chip, c), sib)
                fw.start()
                passed.append(fw)
        for a in range(n):
            copy(a, 0, sib, me).wait_recv()
            for j, chip in enumerate(chips):
                copy(a, 4 + j, (*chip, 1 - c), me).wait_recv()
        for cp in first + passed:
            cp.wait_send()
        for cp in mine:
            cp.wait()

    return _pcall(body, name=name, in_specs=[HBM_SPEC] * n, out_specs=[HBM_SPEC] * n,
                  out_shape=[jax.ShapeDtypeStruct((N_DEV,) + a.shape, a.dtype) for a in arrs],
                  scratch_shapes=[pltpu.SemaphoreType.DMA((7 * n,)), pltpu.SemaphoreType.DMA((7 * n,)),
                                  pltpu.SemaphoreType.DMA((n,))])(*arrs)


def _rs_sibling(g, name):
    _, rows, W = g.shape

    def body(g_ref, o_ref, send_sems, recv_sems):
        x, y, c = _me()
        sib = (x, y, 1 - c)
        cps = [pltpu.make_async_remote_copy(src_ref=g_ref.at[2 * q + (1 - c)], dst_ref=o_ref.at[q],
                                            send_sem=send_sems.at[q], recv_sem=recv_sems.at[q],
                                            device_id=sib, device_id_type=MESH) for q in range(4)]
        for cp in cps:
            cp.start()
        for cp in cps:
            cp.wait()

    return _pcall(body, name=name, in_specs=[HBM_SPEC], out_specs=HBM_SPEC,
                  out_shape=jax.ShapeDtypeStruct((4, rows, W), g.dtype),
                  scratch_shapes=[pltpu.SemaphoreType.DMA((4,)), pltpu.SemaphoreType.DMA((4,))])(g)


def _rs_chips(p, name):
    _, rows, W = p.shape

    def body(p_ref, o_ref, send_sems, recv_sems):
        x, y, c = _me()
        chips = [(1 - x, y), (x, 1 - y), (1 - x, 1 - y)]
        cps = [pltpu.make_async_remote_copy(src_ref=p_ref.at[2 * chip[0] + chip[1]], dst_ref=o_ref.at[j],
                                            send_sem=send_sems.at[j], recv_sem=recv_sems.at[j],
                                            device_id=(*chip, c), device_id_type=MESH) for j, chip in enumerate(chips)]
        for cp in cps:
            cp.start()
        for cp in cps:
            cp.wait()

    return _pcall(body, name=name, in_specs=[HBM_SPEC], out_specs=HBM_SPEC,
                  out_shape=jax.ShapeDtypeStruct((3, rows, W), p.dtype),
                  scratch_shapes=[pltpu.SemaphoreType.DMA((3,)), pltpu.SemaphoreType.DMA((3,))])(p)


def _reduce_scatter(g, name):
    _, rows, W = g.shape
    x, y, c = _me()
    from_sib = _rs_sibling(g, name + "_sib")
    own = g.reshape(4, 2, rows, W)
    mine = jnp.where(c == 0, own[:, 0], own[:, 1])
    pair = _add_pairs(mine, from_sib, name + "_pair")
    from_chips = _rs_chips(pair, name + "_chips")
    own_pair = lax.dynamic_index_in_dim(pair, 2 * x + y, axis=0, keepdims=False)
    return _sum_slots(from_chips, name + "_sum", extra=own_pair)


BIG = ("even_w_in", "even_w_out", "odd_w_in", "odd_w_out", "ffn_w_up", "ffn_w_down", "ple_w_proj", "ple_w_gate")
SMALL_SHARDED = ("even_conv_w", "odd_conv_w", "odd_conv_b", "odd_norm_g", "ffn_conv_w")
REPLICATED = ("even_b_f", "odd_dt_bias", "odd_a_log", "odd_d_skip", "ln_mix_g", "ln_mix_b", "ffn_conv_b",
              "ln_ffn_g", "ln_ffn_b", "ple_b_gate")
WEIGHTS = ("even_w_in", "even_b_f", "even_conv_w", "even_w_out", "odd_w_in", "odd_conv_w", "odd_conv_b", "odd_dt_bias",
           "odd_a_log", "odd_d_skip", "odd_norm_g", "odd_w_out", "ln_mix_g", "ln_mix_b", "ffn_w_up", "ffn_conv_w",
           "ffn_conv_b", "ffn_w_down", "ln_ffn_g", "ln_ffn_b", "ple_w_proj", "ple_w_gate", "ple_b_gate")


def _full_shapes():
    d = _dims()
    return {
        "even_w_in": ((1, D_MODEL, d["even_in"]), 2), "even_b_f": ((1, FOX_HEADS), None),
        "even_conv_w": ((1, CONV_WIDTH, CONV_DIM), 2), "even_w_out": ((1, d["even_mix"], D_MODEL), 1),
        "odd_w_in": ((1, D_MODEL, d["odd_in"]), 2), "odd_conv_w": ((1, SSM_CONV_WIDTH, d["conv_ch"]), 2),
        "odd_conv_b": ((1, d["conv_ch"]), 1), "odd_dt_bias": ((1, d["ssm_heads"]), None),
        "odd_a_log": ((1, d["ssm_heads"]), None), "odd_d_skip": ((1, d["ssm_heads"]), None),
        "odd_norm_g": ((1, d["ssm_inner"]), 1), "odd_w_out": ((1, d["ssm_inner"], D_MODEL), 1),
        "ln_mix_g": ((DEPTH, D_MODEL), None), "ln_mix_b": ((DEPTH, D_MODEL), None),
        "ffn_w_up": ((DEPTH, D_MODEL, 2 * D_FF), 2), "ffn_conv_w": ((DEPTH, FFN_CONV_WIDTH, 2 * D_FF), 2),
        "ffn_conv_b": ((DEPTH, 2 * D_FF), None), "ffn_w_down": ((DEPTH, D_FF, D_MODEL), 1),
        "ln_ffn_g": ((DEPTH, D_MODEL), None), "ln_ffn_b": ((DEPTH, D_MODEL), None),
        "ple_w_proj": ((DEPTH, PLE_DIM, D_MODEL), 2), "ple_w_gate": ((DEPTH, D_MODEL, D_MODEL), 1),
        "ple_b_gate": ((DEPTH, D_MODEL), None),
    }


def _shard_shape(name):
    shape, ax = _full_shapes()[name]
    if ax is None:
        return shape
    return tuple(s // N_DEV if i == ax else s for i, s in enumerate(shape))


def _pack_rows(n_elems):
    rows = -(-n_elems // PACK_W)
    unit = 128 if rows > 128 else 16
    return -(-rows // unit) * unit


def _pack(parts, dtype):
    flat = jnp.concatenate([p.reshape(-1).astype(dtype) for p in parts])
    rows = _pack_rows(flat.shape[0])
    return jnp.pad(flat, (0, rows * PACK_W - flat.shape[0])).reshape(rows, PACK_W)


def _pack_dest(parts, dtype):
    flat = jnp.concatenate([p.reshape(N_DEV, -1).astype(dtype) for p in parts], axis=1)
    rows = _pack_rows(flat.shape[1])
    return jnp.pad(flat, ((0, 0), (0, rows * PACK_W - flat.shape[1]))).reshape(N_DEV, rows, PACK_W)


def _unpack(pack, shapes, lead=()):
    flat = pack.reshape(lead + (-1,))
    out, off = [], 0
    for s in shapes:
        n = int(np.prod(s))
        out.append(flat[..., off:off + n].reshape(lead + tuple(s)))
        off += n
    return out


def _assemble(gathered, name):
    shape, ax = _full_shapes()[name]
    return jnp.moveaxis(gathered, 0, ax).reshape(shape)


def _split_dest(full, name):
    shape, ax = _full_shapes()[name]
    sh = shape[:ax] + (N_DEV, shape[ax] // N_DEV) + shape[ax + 1:]
    return jnp.moveaxis(full.reshape(sh), ax, 0)


def _interleave_cols(w, parts, tc):
    C = w.shape[-1] // parts
    sh = w.shape[:-1]
    return w.reshape(sh + (parts, C // tc, tc)).swapaxes(-3, -2).reshape(sh + (parts * C,))


def _deinterleave_cols(w, parts, tc):
    C = w.shape[-1] // parts
    sh = w.shape[:-1]
    return w.reshape(sh + (C // tc, parts, tc)).swapaxes(-3, -2).reshape(sh + (parts * C,))


def _ffn_tc():
    return _pick(D_FF, (256, 128))


def _heads_first(a, heads):
    L = a.shape[0]
    return a.reshape(L, heads, -1).transpose(1, 0, 2)


def _heads_last(a):
    h, L, d = a.shape
    return a.transpose(1, 0, 2).reshape(L, h * d)


def _pad_cols(a, to):
    return jnp.pad(a, ((0, 0), (0, to - a.shape[1])))


def _tail_fwd(i, h_in, mix, p_i, W, sp):
    r1, h1 = _ln_fwd(h_in, mix, sp["ln_mix_g"][i], sp["ln_mix_b"][i], f"ln_mix_fwd{i}")
    U = _mm(h1, W["ffn_up"][i], "nn", F32, f"ffn_up{i}")
    S = _ffn_act_fwd(U, sp["ffn_conv_w_il"][i], sp["ffn_conv_b_il"][i], f"ffn_act_fwd{i}")
    ffn = _mm(S, W["ffn_down"][i], "nn", F32, f"ffn_down{i}")
    r2, h2 = _ln_fwd(h1, ffn, sp["ln_ffn_g"][i], sp["ln_ffn_b"][i], f"ln_ffn_fwd{i}")
    G = _mm(h2, W["ple_gate"][i], "nn", F32, f"ple_gate{i}")
    E = _mm(p_i, W["ple_proj"][i], "nn", F32, f"ple_proj{i}")
    h3 = _ple_fwd(h2, G, sp["ple_b_gate"][i], E, f"ple_fwd{i}")
    return h3, dict(r1=r1, h1=h1, U=U, S=S, r2=r2, h2=h2, G=G, E=E, p=p_i)


def _tail_bwd(i, dh3, sv, W, sp, grads):
    alpha = _alpha()
    dE, dGp, dbg = _ple_bwd(dh3, sv["G"], sp["ple_b_gate"][i], sv["E"], f"ple_bwd{i}")
    grads["ple_b_gate"][i] = dbg.sum(0)
    grads["ple_w_proj"][i] = _mm(sv["p"], dE, "tn", F32, f"d_ple_proj{i}")
    grads["ple_w_gate"][i] = _mm(sv["h2"], dGp, "tn", F32, f"d_ple_gate{i}")
    dh2 = _mm(dGp, W["ple_gate"][i], "nt", F32, f"dx_ple_gate{i}", add=dh3)
    dr2, dg, db = _ln_bwd(sv["r2"], dh2, sp["ln_ffn_g"][i], f"ln_ffn_bwd{i}")
    grads["ln_ffn_g"][i], grads["ln_ffn_b"][i] = dg.sum(0), db.sum(0)
    grads["ffn_w_down"][i] = _mm(sv["S"], dr2, "tn", F32, f"d_ffn_down{i}")
    dS = _mm(dr2, W["ffn_down"][i], "nt", F32, f"dx_ffn_down{i}")
    dUc, dcw, dcb = _ffn_act_bwd(sv["U"], dS, sp["ffn_conv_w_il"][i], sp["ffn_conv_b_il"][i], f"ffn_act_bwd{i}")
    K = FFN_CONV_WIDTH
    tc = _ffn_tc()
    grads["ffn_conv_w"][i] = _deinterleave_cols(dcw.reshape(K, SUBLANES, -1).sum(1), 2, tc)
    grads["ffn_conv_b"][i] = _deinterleave_cols(dcb.sum(0), 2, tc)
    dU = _conv_t(dUc, sp["ffn_conv_w_il"][i], f"ffn_conv_t{i}")
    grads["ffn_w_up"][i] = _deinterleave_cols(_mm(sv["h1"], dU, "tn", F32, f"d_ffn_up{i}"), 2, tc)
    dh1 = _mm(dU, W["ffn_up"][i], "nt", F32, f"dx_ffn_up{i}", add=dr2, add_scale=alpha)
    dr1, dg, db = _ln_bwd(sv["r1"], dh1, sp["ln_mix_g"][i], f"ln_mix_bwd{i}")
    grads["ln_mix_g"][i], grads["ln_mix_b"][i] = dg.sum(0), db.sum(0)
    return dr1


def _even_fwd(h, W, sp):
    L = h.shape[0]
    H, Dh = FOX_HEADS, FOX_HEAD_DIM
    Ac = _mm(h, W["even_in_conv"], "nn", F32, "even_in_conv")
    qkv = _mm(h, W["even_in_qkv"], "nn", BF16, "even_in_qkv")
    Af = _mm(h, W["even_in_f"], "nn", F32, "even_in_f")
    y_a = _sconv_fwd(Ac, sp["even_conv_w_il"], "sconv_fwd")
    Fc = _fox_gate_fwd(Af, sp["even_b_f_pad"], "fox_gate_fwd")
    Fh = Fc[:, :H].T
    Fq, Fk = Fh.reshape(H, L, 1), Fh.reshape(H, 1, L)
    fd = H * Dh
    q, k, v = (_heads_first(qkv[:, j * fd:(j + 1) * fd], H) for j in range(3))
    o, lse = _attn_fwd(q, k, v, Fq, Fk, "attn_fwd")
    Y = jnp.concatenate([y_a, _heads_last(o).astype(BF16)], axis=1)
    mix = _mm(Y, W["even_out"], "nn", F32, "even_out")
    return mix, dict(h=h, Ac=Ac, Af=Af, q=q, k=k, v=v, Fq=Fq, Fk=Fk, o=o, lse=lse, Y=Y)


def _even_bwd(dmix, dres, sv, W, sp, grads):
    H, Dh = FOX_HEADS, FOX_HEAD_DIM
    C = CONV_DIM
    L = dmix.shape[0]
    grads["even_w_out"][0] = _mm(sv["Y"], dmix, "tn", F32, "d_even_out")
    dY = _mm(dmix, W["even_out"], "nt", F32, "dx_even_out")
    dya = dY[:, :C]
    do = _heads_first(dY[:, C:], H)
    dc, dcw = _sconv_bwd_dc(sv["Ac"], dya, sp["even_conv_w_il"], "sconv_bwd_dc")
    grads["even_conv_w"][0] = dcw.reshape(CONV_WIDTH, SUBLANES, -1).sum(1)
    dAc = _sconv_bwd_da(sv["Ac"], dya, dc, sp["even_conv_w_il"], "sconv_bwd_da")
    delta = _attn_bwd_delta(sv["q"], sv["k"], sv["v"], sv["Fq"], sv["Fk"], sv["lse"], do, "attn_bwd_delta")
    args = (sv["q"], sv["k"], sv["v"], sv["Fq"], sv["Fk"], delta, sv["lse"], do)
    dq = _attn_bwd_dq(*args, "attn_bwd_dq")
    dk, dv, dFk = _attn_bwd_dkv(*args, "attn_bwd_dkv")
    dqkv = jnp.concatenate([_heads_last(dq), _heads_last(dk), _heads_last(dv)], axis=1).astype(BF16)
    dF = _pad_cols(dFk.reshape(H, L).T, LANES)
    dAf, dbf = _fox_gate_bwd(sv["Af"], sp["even_b_f_pad"], dF, "fox_gate_bwd")
    grads["even_b_f"][0] = dbf.sum(0)[:H]
    h = sv["h"]
    gc = _deinterleave_cols(_mm(h, dAc, "tn", F32, "d_even_in_conv"), 3, LANES)
    gq = _mm(h, dqkv, "tn", F32, "d_even_in_qkv")
    gf = _mm(h, dAf, "tn", F32, "d_even_in_f")[:, :H]
    grads["even_w_in"][0] = jnp.concatenate([gc, gq, gf], axis=1)
    dh = _mm(dAc, W["even_in_conv"], "nt", F32, "dx_even_in_conv", add=dres, add_scale=_alpha())
    dh = _mm(dqkv, W["even_in_qkv"], "nt", F32, "dx_even_in_qkv", add=dh)
    dh = _mm(dAf, W["even_in_f"], "nt", F32, "dx_even_in_f", add=dh)
    return dh


def _group_layouts(v, G):
    R = v.shape[0] // G
    return v.reshape(G, 1, R), v.reshape(G, R, 1)


def _odd_fwd(h, W, sp):
    d = _dims()
    L = h.shape[0]
    Hs, G, N, P = d["ssm_heads"], SSM_GROUPS, SSM_STATE, SSM_HEAD_DIM
    R = Hs // G
    inner = d["ssm_inner"]
    z = _mm(h, W["odd_in_z"], "nn", F32, "odd_in_z")
    xr = _mm(h, W["odd_in_x"], "nn", F32, "odd_in_x")
    dtp = _mm(h, W["odd_in_dt"], "nn", F32, "odd_in_dt")
    act = _mconv_fwd(xr, sp["odd_conv_w"], sp["odd_conv_b"], "mconv_fwd")
    xs = _heads_first(act[:, :inner], Hs)
    Bm = _heads_first(act[:, inner:inner + G * N], G)
    Cm = _heads_first(act[:, inner + G * N:], G)
    dtg = dtp[:, :Hs].reshape(L, G, R)
    dtc, dtr = dtg.transpose(1, 0, 2), dtg.transpose(1, 2, 0)
    prm = _group_layouts(sp["odd_dt_bias"], G) + _group_layouts(sp["odd_a_log"], G) + (_group_layouts(sp["odd_d_skip"], G)[0],)
    ssd_in = (xs, Bm, Cm, dtc, dtr) + prm
    y3, hprev = _ssd_fwd(*ssd_in, "ssd_fwd")
    y = _heads_last(y3)
    u = _gnorm_fwd(y, z, sp["odd_norm_g"], "gnorm_fwd")
    mix = _mm(u, W["odd_out"], "nn", F32, "odd_out")
    return mix, dict(h=h, z=z, xr=xr, ssd_in=ssd_in, hprev=hprev, y=y, u=u)


def _odd_bwd(dmix, dres, sv, W, sp, grads):
    d = _dims()
    L = dmix.shape[0]
    Hs, G, N, P = d["ssm_heads"], SSM_GROUPS, SSM_STATE, SSM_HEAD_DIM
    grads["odd_w_out"][0] = _mm(sv["u"], dmix, "tn", F32, "d_odd_out")
    du = _mm(dmix, W["odd_out"], "nt", F32, "dx_odd_out")
    dy, dz, dg = _gnorm_bwd(sv["y"], sv["z"], sp["odd_norm_g"], du, "gnorm_bwd")
    grads["odd_norm_g"][0] = dg.sum(0)
    dx3, dB, dC, ddt, gbias, galog, gdsk = _ssd_bwd(*sv["ssd_in"], sv["hprev"], _heads_first(dy, Hs), "ssd_bwd")
    grads["odd_dt_bias"][0] = gbias.reshape(Hs)
    grads["odd_a_log"][0] = galog.reshape(Hs)
    grads["odd_d_skip"][0] = gdsk.reshape(Hs)
    dact = jnp.concatenate([_heads_last(dx3), _heads_last(dB), _heads_last(dC)], axis=1)
    dpre, dcw, dcb = _mconv_bwd(sv["xr"], dact, sp["odd_conv_w"], sp["odd_conv_b"], "mconv_bwd")
    grads["odd_conv_w"][0] = dcw.reshape(SSM_CONV_WIDTH, SUBLANES, -1).sum(1)
    grads["odd_conv_b"][0] = dcb.sum(0)
    dxr = _conv_t(dpre, sp["odd_conv_w"], "mconv_t")
    ddtp = _pad_cols(ddt.transpose(1, 0, 2).reshape(L, Hs), W["odd_in_dt"].shape[1])
    h = sv["h"]
    gz = _mm(h, dz, "tn", F32, "d_odd_in_z")
    gx = _mm(h, dxr, "tn", F32, "d_odd_in_x")
    gdt = _mm(h, ddtp, "tn", F32, "d_odd_in_dt")[:, :Hs]
    grads["odd_w_in"][0] = jnp.concatenate([gz, gx, gdt], axis=1)
    dh = _mm(dz, W["odd_in_z"], "nt", F32, "dx_odd_in_z", add=dres, add_scale=_alpha())
    dh = _mm(dxr, W["odd_in_x"], "nt", F32, "dx_odd_in_x", add=dh)
    dh = _mm(ddtp, W["odd_in_dt"], "nt", F32, "dx_odd_in_dt", add=dh)
    return dh


def _prepare_weights(full):
    d = _dims()
    C, fd, H = CONV_DIM, d["fox_dim"], FOX_HEADS
    tc = _ffn_tc()
    W, sp = {}, {}
    ew = full["even_w_in"][0]
    W["even_in_conv"] = _interleave_cols(ew[:, :3 * C], 3, LANES)
    W["even_in_qkv"] = ew[:, 3 * C:3 * C + 3 * fd]
    W["even_in_f"] = _pad_cols(ew[:, 3 * C + 3 * fd:], LANES)
    W["even_out"] = full["even_w_out"][0]
    ow = full["odd_w_in"][0]
    inner, cch, Hs = d["ssm_inner"], d["conv_ch"], d["ssm_heads"]
    W["odd_in_z"] = ow[:, :inner]
    W["odd_in_x"] = ow[:, inner:inner + cch]
    W["odd_in_dt"] = _pad_cols(ow[:, inner + cch:], -(-Hs // LANES) * LANES)
    W["odd_out"] = full["odd_w_out"][0]
    W["ffn_up"] = [_interleave_cols(full["ffn_w_up"][i], 2, tc) for i in range(DEPTH)]
    W["ffn_down"] = [full["ffn_w_down"][i] for i in range(DEPTH)]
    W["ple_proj"] = [full["ple_w_proj"][i] for i in range(DEPTH)]
    W["ple_gate"] = [full["ple_w_gate"][i] for i in range(DEPTH)]
    sp["even_conv_w_il"] = full["even_conv_w"][0]
    sp["even_b_f_pad"] = _pad_cols(full["even_b_f"], LANES)
    sp["odd_conv_w"] = full["odd_conv_w"][0]
    sp["odd_conv_b"] = full["odd_conv_b"][0]
    sp["odd_norm_g"] = full["odd_norm_g"][0]
    for n in ("odd_dt_bias", "odd_a_log", "odd_d_skip"):
        sp[n] = full[n][0]
    for n in ("ln_mix_g", "ln_mix_b", "ln_ffn_g", "ln_ffn_b", "ple_b_gate"):
        sp[n] = full[n]
    sp["ffn_conv_w_il"] = [_interleave_cols(full["ffn_conv_w"][i], 2, tc) for i in range(DEPTH)]
    sp["ffn_conv_b_il"] = [_interleave_cols(full["ffn_conv_b"][i], 2, tc) for i in range(DEPTH)]
    return W, sp


def _local_step(x, p, target, full):
    W, sp = _prepare_weights(full)
    grads = {n: [None] * _full_shapes()[n][0][0] for n in WEIGHTS}
    mix0, sv_e = _even_fwd(x, W, sp)
    h3_0, sv_t0 = _tail_fwd(0, x, mix0, p[0], W, sp)
    mix1, sv_o = _odd_fwd(h3_0, W, sp)
    h3_1, sv_t1 = _tail_fwd(1, h3_0, mix1, p[1], W, sp)
    dh, sq = _loss_head(h3_1, target, "loss_head")
    dr1 = _tail_bwd(1, dh, sv_t1, W, sp, grads)
    dh = _odd_bwd(dr1, dr1, sv_o, W, sp, grads)
    dr1 = _tail_bwd(0, dh, sv_t0, W, sp, grads)
    dx = _even_bwd(dr1, dr1, sv_e, W, sp, grads)
    grads = {n: jnp.stack(v) for n, v in grads.items()}
    return jnp.sum(sq), dx, grads


def kernel(x, p, even_w_in, even_b_f, even_conv_w, even_w_out, odd_w_in, odd_conv_w, odd_conv_b, odd_dt_bias, odd_a_log, odd_d_skip, odd_norm_g, odd_w_out, ln_mix_g, ln_mix_b, ffn_w_up, ffn_conv_w, ffn_conv_b, ffn_w_down, ln_ffn_g, ln_ffn_b, ple_w_proj, ple_w_gate, ple_b_gate, loss_target, m_even_w_in, m_even_b_f, m_even_conv_w, m_even_w_out, m_odd_w_in, m_odd_conv_w, m_odd_conv_b, m_odd_dt_bias, m_odd_a_log, m_odd_d_skip, m_odd_norm_g, m_odd_w_out, m_ln_mix_g, m_ln_mix_b, m_ffn_w_up, m_ffn_conv_w, m_ffn_conv_b, m_ffn_w_down, m_ln_ffn_g, m_ln_ffn_b, m_ple_w_proj, m_ple_w_gate, m_ple_b_gate, v_even_w_in, v_even_b_f, v_even_conv_w, v_even_w_out, v_odd_w_in, v_odd_conv_w, v_odd_conv_b, v_odd_dt_bias, v_odd_a_log, v_odd_d_skip, v_odd_norm_g, v_odd_w_out, v_ln_mix_g, v_ln_mix_b, v_ffn_w_up, v_ffn_conv_w, v_ffn_conv_b, v_ffn_w_down, v_ln_ffn_g, v_ln_ffn_b, v_ple_w_proj, v_ple_w_gate, v_ple_b_gate):
    args = locals()
    w = {n: args[n] for n in WEIGHTS}
    m = {n: args["m_" + n] for n in WEIGHTS}
    v = {n: args["v_" + n] for n in WEIGHTS}
    me = 4 * lax.axis_index("x") + 2 * lax.axis_index("y") + lax.axis_index("c")

    big_g, small_g = _all_gather([_pack([w[n] for n in BIG], BF16), _pack([w[n] for n in SMALL_SHARDED], F32)], "ag_weights")
    full = dict(w)
    for n, g in zip(BIG, _unpack(big_g, [_shard_shape(n) for n in BIG], (N_DEV,))):
        full[n] = _assemble(g, n)
    for n, g in zip(SMALL_SHARDED, _unpack(small_g, [_shard_shape(n) for n in SMALL_SHARDED], (N_DEV,))):
        full[n] = _assemble(g, n)

    sq, dx, grads = _local_step(x[0], p[:, 0], loss_target[0], full)
    loss = lax.psum(0.5 * sq / D_MODEL, ("x", "y", "c"))

    gsum_big = _reduce_scatter(_pack_dest([_split_dest(grads[n], n) for n in BIG], BF16), "rs_grads")
    g_final = dict(zip(BIG, _unpack(gsum_big, [_shard_shape(n) for n in BIG])))
    small_names = SMALL_SHARDED + REPLICATED
    (small_all,) = _all_gather([_pack([grads[n] for n in small_names], F32)], "ag_small_grads")
    small_sum = _sum_slots(small_all, "sum_small_grads")
    for n, g in zip(small_names, _unpack(small_sum, [_full_shapes()[n][0] for n in small_names])):
        g_final[n] = lax.dynamic_index_in_dim(_split_dest(g, n), me, axis=0, keepdims=False) if n in SMALL_SHARDED else g

    shapes = [_shard_shape(n) for n in WEIGHTS]
    packs = [_pack([t[n] for n in WEIGHTS], F32) for t in (w, g_final, m, v)]
    delta, new_m, new_v = (_unpack(o, shapes) for o in _adamw(*packs, "adamw"))
    return (loss, dx[None], *[g_final[n] for n in WEIGHTS], *delta, *new_m, *new_v)
```

```python
import jax
import jax.numpy as jnp
import numpy as np
from jax import lax
from jax.experimental import pallas as pl
from jax.experimental.pallas import tpu as pltpu

D_MODEL = 1024
SEQ = 8192
DEPTH = 2
CONV_DIM = 512
CONV_WIDTH = 3
FOX_HEADS = 8
FOX_HEAD_DIM = 64
SSM_HEAD_DIM = 64
SSM_GROUPS = 4
SSM_STATE = 128
SSM_CONV_WIDTH = 4
SSM_CHUNK = 128
D_FF = 2816
FFN_CONV_WIDTH = 3
PLE_DIM = 256
LN_EPS = 1e-5
RMS_EPS = 1e-5
ADAM_LR = 0.001
ADAM_B1 = 0.9
ADAM_B2 = 0.999
ADAM_EPS = 1e-08
ADAM_WD = 0.01
ADAM_STEP = 10
N_DEV = 8

F32 = jnp.float32
BF16 = jnp.bfloat16
NEG = -1e30
LANES = 128
SUBLANES = 8
PACK_W = 1024
VMEM_LIMIT = 48 * 1024 * 1024


def _dims():
    fox_dim = FOX_HEADS * FOX_HEAD_DIM
    ssm_inner = 2 * D_MODEL
    ssm_heads = ssm_inner // SSM_HEAD_DIM
    conv_ch = ssm_inner + 2 * SSM_GROUPS * SSM_STATE
    return dict(fox_dim=fox_dim, even_in=3 * CONV_DIM + 3 * fox_dim + FOX_HEADS, even_mix=CONV_DIM + fox_dim,
                ssm_inner=ssm_inner, ssm_heads=ssm_heads, conv_ch=conv_ch, odd_in=ssm_inner + conv_ch + ssm_heads)


def _alpha():
    return (2.0 * DEPTH) ** 0.25


def _pick(dim, prefs):
    for p in prefs:
        if dim % p == 0:
            return p
    return dim


def _pcall(body, **kw):
    return pl.pallas_call(body, **kw)


def _cparams(sem=None, **kw):
    if sem is not None:
        kw["dimension_semantics"] = sem
    return pltpu.CompilerParams(vmem_limit_bytes=VMEM_LIMIT, **kw)


def _sigmoid(x):
    return 1.0 / (1.0 + jnp.exp(-x))


def _softplus(x):
    return jnp.maximum(x, 0.0) + jnp.log(1.0 + jnp.exp(-jnp.abs(x)))


def _sum8(x):
    n, c = x.shape
    return x.reshape(n // SUBLANES, SUBLANES, c).sum(axis=0)


def _dot(a, b, dims):
    return lax.dot_general(a, b, (dims, ((), ())), preferred_element_type=F32)


NN = ((1,), (0,))
NT = ((1,), (1,))
TN = ((0,), (0,))


def _split3(x):
    hi = x.astype(BF16)
    r1 = x - hi.astype(F32)
    mid = r1.astype(BF16)
    lo = (r1 - mid.astype(F32)).astype(BF16)
    return hi, mid, lo


def _tri_mm(tri_bf16, x, tri_first=True):
    if tri_first:
        return sum(_dot(tri_bf16, part, NN) for part in _split3(x))
    return sum(_dot(part, tri_bf16, NN) for part in _split3(x))


def _tri(n, upper=False):
    r = lax.broadcasted_iota(jnp.int32, (n, n), 0)
    c = lax.broadcasted_iota(jnp.int32, (n, n), 1)
    return jnp.where((r <= c) if upper else (r >= c), 1.0, 0.0).astype(BF16)


def _shift_down(cur, prev8, k):
    if k == 0:
        return cur
    ext = jnp.concatenate([prev8, cur], axis=0)
    return pltpu.roll(ext, k, axis=0)[SUBLANES:]


def _shift_up(cur, next8, k):
    if k == 0:
        return cur
    n = cur.shape[0]
    ext = jnp.concatenate([cur, next8], axis=0)
    return pltpu.roll(ext, n + SUBLANES - k, axis=0)[:n]


def _mm(a, b, mode, out_dtype, name, add=None, add_scale=1.0):
    if mode == "nn":
        (M, K), (K2, N) = a.shape, b.shape
    elif mode == "nt":
        (M, K), (N, K2) = a.shape, b.shape
    else:
        (K, M), (K2, N) = a.shape, b.shape
    assert K == K2, (a.shape, b.shape, mode)
    tm = _pick(M, (1024, 512, 256, 128))
    tn = _pick(N, (1408, 1024, 768, 512, 384, 256, 128))
    tk = K if K <= 2048 else _pick(K, (1408, 1024, 768, 512, 256, 128))
    nk = K // tk
    dims = {"nn": NN, "nt": NT, "tn": TN}[mode]
    a_spec = pl.BlockSpec((tk, tm), lambda i, j, k: (k, i)) if mode == "tn" else pl.BlockSpec((tm, tk), lambda i, j, k: (i, k))
    b_spec = pl.BlockSpec((tn, tk), lambda i, j, k: (j, k)) if mode == "nt" else pl.BlockSpec((tk, tn), lambda i, j, k: (k, j))
    o_spec = pl.BlockSpec((tm, tn), lambda i, j, k: (i, j))
    has_add = add is not None

    def body(*refs):
        a_ref, b_ref = refs[:2]
        add_ref = refs[2] if has_add else None
        o_ref = refs[2 + has_add]
        prod = _dot(a_ref[...].astype(BF16), b_ref[...].astype(BF16), dims)
        if nk == 1:
            if has_add:
                prod = prod + add_scale * add_ref[...].astype(F32)
            o_ref[...] = prod.astype(out_dtype)
            return
        acc = refs[3 + has_add]
        k = pl.program_id(2)

        @pl.when(k == 0)
        def _():
            if has_add:
                acc[...] = prod + add_scale * add_ref[...].astype(F32)
            else:
                acc[...] = prod

        @pl.when(k > 0)
        def _():
            acc[...] += prod

        @pl.when(k == nk - 1)
        def _():
            o_ref[...] = acc[...].astype(out_dtype)

    ins = [a, b] + ([add] if has_add else [])
    specs = [a_spec, b_spec] + ([o_spec] if has_add else [])
    return _pcall(body, name=name, grid=(M // tm, N // tn, nk), in_specs=specs, out_specs=o_spec,
                  out_shape=jax.ShapeDtypeStruct((M, N), out_dtype),
                  scratch_shapes=[pltpu.VMEM((tm, tn), F32)] if nk > 1 else [],
                  compiler_params=_cparams(("parallel", "parallel", "arbitrary")))(*ins)


def _row_tile(L):
    return _pick(L, (256, 128))


def _ln_fwd(h, mix, g, b, name):
    L, D = h.shape
    tl = _row_tile(L)
    alpha = _alpha()

    def body(h_ref, m_ref, g_ref, b_ref, r_ref, y_ref, yb_ref):
        r = alpha * h_ref[...] + m_ref[...]
        mu = jnp.mean(r, axis=-1, keepdims=True)
        xc = r - mu
        var = jnp.mean(xc * xc, axis=-1, keepdims=True)
        r_ref[...] = r
        y = xc * lax.rsqrt(var + LN_EPS) * g_ref[...] + b_ref[...]
        y_ref[...] = y
        yb_ref[...] = y.astype(BF16)

    row = pl.BlockSpec((tl, D), lambda i: (i, 0))
    vec = pl.BlockSpec((1, D), lambda i: (0, 0))
    return _pcall(body, name=name, grid=(L // tl,), in_specs=[row, row, vec, vec], out_specs=[row, row, row],
                  out_shape=[jax.ShapeDtypeStruct((L, D), F32)] * 2 + [jax.ShapeDtypeStruct((L, D), BF16)],
                  compiler_params=_cparams(("parallel",)))(h, mix, g.reshape(1, D), b.reshape(1, D))


def _ln_bwd(r, dy, g, name):
    L, D = r.shape
    tl = _row_tile(L)

    def body(r_ref, dy_ref, g_ref, dr_ref, drb_ref, dg_ref, db_ref):
        i = pl.program_id(0)
        r_ = r_ref[...]
        dy_ = dy_ref[...]
        mu = jnp.mean(r_, axis=-1, keepdims=True)
        xc = r_ - mu
        rstd = lax.rsqrt(jnp.mean(xc * xc, axis=-1, keepdims=True) + LN_EPS)
        xhat = xc * rstd
        dxh = dy_ * g_ref[...]
        dr = rstd * (dxh - jnp.mean(dxh, axis=-1, keepdims=True) - xhat * jnp.mean(dxh * xhat, axis=-1, keepdims=True))
        dr_ref[...] = dr
        drb_ref[...] = dr.astype(BF16)

        @pl.when(i == 0)
        def _():
            dg_ref[...] = jnp.zeros_like(dg_ref)
            db_ref[...] = jnp.zeros_like(db_ref)

        dg_ref[...] += _sum8(dy_ * xhat)
        db_ref[...] += _sum8(dy_)

    row = pl.BlockSpec((tl, D), lambda i: (i, 0))
    vec = pl.BlockSpec((1, D), lambda i: (0, 0))
    acc = pl.BlockSpec((SUBLANES, D), lambda i: (0, 0))
    return _pcall(body, name=name, grid=(L // tl,), in_specs=[row, row, vec], out_specs=[row, row, acc, acc],
                  out_shape=[jax.ShapeDtypeStruct((L, D), F32), jax.ShapeDtypeStruct((L, D), BF16),
                             jax.ShapeDtypeStruct((SUBLANES, D), F32), jax.ShapeDtypeStruct((SUBLANES, D), F32)],
                  compiler_params=_cparams(("arbitrary",)))(r, dy, g.reshape(1, D))


def _ple_fwd(h2, G, bg, E, name):
    L, D = h2.shape
    tl = _row_tile(L)

    def body(h_ref, g_ref, b_ref, e_ref, o_ref, ob_ref):
        o = h_ref[...] + _sigmoid(g_ref[...] + b_ref[...]) * e_ref[...]
        o_ref[...] = o
        ob_ref[...] = o.astype(BF16)

    row = pl.BlockSpec((tl, D), lambda i: (i, 0))
    vec = pl.BlockSpec((1, D), lambda i: (0, 0))
    return _pcall(body, name=name, grid=(L // tl,), in_specs=[row, row, vec, row], out_specs=[row, row],
                  out_shape=[jax.ShapeDtypeStruct((L, D), F32), jax.ShapeDtypeStruct((L, D), BF16)],
                  compiler_params=_cparams(("parallel",)))(h2, G, bg.reshape(1, D), E)


def _ple_bwd(dh3, G, bg, E, name):
    L, D = dh3.shape
    tl = _row_tile(L)

    def body(d_ref, g_ref, b_ref, e_ref, de_ref, dg_ref, db_ref):
        i = pl.program_id(0)
        d = d_ref[...]
        sg = _sigmoid(g_ref[...] + b_ref[...])
        de_ref[...] = (d * sg).astype(BF16)
        dgp = d * e_ref[...] * sg * (1.0 - sg)
        dg_ref[...] = dgp.astype(BF16)

        @pl.when(i == 0)
        def _():
            db_ref[...] = jnp.zeros_like(db_ref)

        db_ref[...] += _sum8(dgp)

    row = pl.BlockSpec((tl, D), lambda i: (i, 0))
    vec = pl.BlockSpec((1, D), lambda i: (0, 0))
    acc = pl.BlockSpec((SUBLANES, D), lambda i: (0, 0))
    return _pcall(body, name=name, grid=(L // tl,), in_specs=[row, row, vec, row], out_specs=[row, row, acc],
                  out_shape=[jax.ShapeDtypeStruct((L, D), BF16), jax.ShapeDtypeStruct((L, D), BF16),
                             jax.ShapeDtypeStruct((SUBLANES, D), F32)],
                  compiler_params=_cparams(("arbitrary",)))(dh3, G, bg.reshape(1, D), E)


def _loss_head(h, target, name):
    L, D = h.shape
    tl = _row_tile(L)

    def body(h_ref, t_ref, d_ref, s_ref):
        i = pl.program_id(0)
        e = h_ref[...] - t_ref[...]
        d_ref[...] = e * (1.0 / D)

        @pl.when(i == 0)
        def _():
            s_ref[...] = jnp.zeros_like(s_ref)

        s_ref[...] += _sum8(e * e)

    row = pl.BlockSpec((tl, D), lambda i: (i, 0))
    acc = pl.BlockSpec((SUBLANES, D), lambda i: (0, 0))
    return _pcall(body, name=name, grid=(L // tl,), in_specs=[row, row], out_specs=[row, acc],
                  out_shape=[jax.ShapeDtypeStruct((L, D), F32), jax.ShapeDtypeStruct((SUBLANES, D), F32)],
                  compiler_params=_cparams(("arbitrary",)))(h, target)


def _halo_prev(tl, ncol_blocks_fn):
    return lambda j, i: (jnp.maximum(i * (tl // SUBLANES) - 1, 0), ncol_blocks_fn(j))


def _conv_taps(cur, prev, w_ref, K):
    acc = w_ref[K - 1:K, :] * cur
    for k in range(K - 1):
        acc = acc + w_ref[k:k + 1, :] * _shift_down(cur, prev, K - 1 - k)
    return acc


def _ffn_act_fwd(U, w, b, name):
    L, F2 = U.shape
    F = F2 // 2
    K = w.shape[0]
    tl = _row_tile(L)
    tc = _pick(F, (256, 128))

    def body(u_ref, up_ref, w_ref, b_ref, s_ref):
        i = pl.program_id(1)
        cur = u_ref[...]
        prev = jnp.where(i == 0, 0.0, up_ref[...])
        uc = _conv_taps(cur, prev, w_ref, K) + b_ref[...]
        g, v = uc[:, :tc], uc[:, tc:]
        s_ref[...] = (g * _sigmoid(g) * v).astype(BF16)

    return _pcall(body, name=name, grid=(F // tc, L // tl),
                  in_specs=[pl.BlockSpec((tl, 2 * tc), lambda j, i: (i, j)),
                            pl.BlockSpec((SUBLANES, 2 * tc), _halo_prev(tl, lambda j: j)),
                            pl.BlockSpec((K, 2 * tc), lambda j, i: (0, j)),
                            pl.BlockSpec((1, 2 * tc), lambda j, i: (0, j))],
                  out_specs=pl.BlockSpec((tl, tc), lambda j, i: (i, j)),
                  out_shape=jax.ShapeDtypeStruct((L, F), BF16),
                  compiler_params=_cparams(("parallel", "parallel")))(U, U, w, b.reshape(1, F2))


def _halo_next(tl, L, rows):
    return lambda j, i: (jnp.minimum((i + 1) * (tl // rows), L // rows - 1), j)


def _conv_taps_t(cur, nxt, w_ref, K):
    acc = w_ref[K - 1:K, :] * cur
    for k in range(K - 1):
        acc = acc + w_ref[k:k + 1, :] * _shift_up(cur, nxt, K - 1 - k)
    return acc


BF16_ROWS = 16


def _ffn_act_bwd(U, dS, w, b, name):
    L, F2 = U.shape
    F = F2 // 2
    K = w.shape[0]
    tl = _row_tile(L)
    tc = _pick(F, (256, 128))
    nl = L // tl

    def body(u_ref, up_ref, un_ref, ds_ref, dsn_ref, w_ref, b_ref, du_ref, dw_ref, db_ref):
        i = pl.program_id(1)
        cur = u_ref[...]
        prev = jnp.where(i == 0, 0.0, up_ref[...])

        def at_conv_out(x, xprev, ds):
            uc = _conv_taps(x, xprev, w_ref, K) + b_ref[...]
            g, v = uc[:, :tc], uc[:, tc:]
            sg = _sigmoid(g)
            return jnp.concatenate([ds * v * sg * (1.0 + g * (1.0 - sg)), ds * g * sg], axis=1)

        duc = at_conv_out(cur, prev, ds_ref[...].astype(F32))
        duc_n = at_conv_out(un_ref[...], cur[tl - SUBLANES:], dsn_ref[...].astype(F32)[:SUBLANES])
        duc_n = jnp.where(i == nl - 1, 0.0, duc_n)
        du_ref[...] = _conv_taps_t(duc, duc_n, w_ref, K).astype(BF16)

        @pl.when(i == 0)
        def _():
            dw_ref[...] = jnp.zeros_like(dw_ref)
            db_ref[...] = jnp.zeros_like(db_ref)

        db_ref[...] += _sum8(duc)
        for k in range(K):
            dw_ref[k * SUBLANES:(k + 1) * SUBLANES, :] += _sum8(duc * _shift_down(cur, prev, K - 1 - k))

    return _pcall(body, name=name, grid=(F // tc, nl),
                  in_specs=[pl.BlockSpec((tl, 2 * tc), lambda j, i: (i, j)),
                            pl.BlockSpec((SUBLANES, 2 * tc), _halo_prev(tl, lambda j: j)),
                            pl.BlockSpec((SUBLANES, 2 * tc), _halo_next(tl, L, SUBLANES)),
                            pl.BlockSpec((tl, tc), lambda j, i: (i, j)),
                            pl.BlockSpec((BF16_ROWS, tc), _halo_next(tl, L, BF16_ROWS)),
                            pl.BlockSpec((K, 2 * tc), lambda j, i: (0, j)),
                            pl.BlockSpec((1, 2 * tc), lambda j, i: (0, j))],
                  out_specs=[pl.BlockSpec((tl, 2 * tc), lambda j, i: (i, j)),
                             pl.BlockSpec((K * SUBLANES, 2 * tc), lambda j, i: (0, j)),
                             pl.BlockSpec((SUBLANES, 2 * tc), lambda j, i: (0, j))],
                  out_shape=[jax.ShapeDtypeStruct((L, F2), BF16), jax.ShapeDtypeStruct((K * SUBLANES, F2), F32),
                             jax.ShapeDtypeStruct((SUBLANES, F2), F32)],
                  compiler_params=_cparams(("parallel", "arbitrary")))(U, U, U, dS, dS, w, b.reshape(1, F2))


def _sconv_fwd(Ac, w, name):
    L, C3 = Ac.shape
    C = C3 // 3
    K = w.shape[0]
    tl = _row_tile(L)
    tc = LANES

    def body(a_ref, ap_ref, w_ref, y_ref):
        i = pl.program_id(1)
        a = a_ref[...]
        ap = ap_ref[...]
        p = a[:, tc:2 * tc] * a[:, 2 * tc:]
        pp = jnp.where(i == 0, 0.0, ap[:, tc:2 * tc] * ap[:, 2 * tc:])
        y_ref[...] = (a[:, :tc] * _conv_taps(p, pp, w_ref, K)).astype(BF16)

    return _pcall(body, name=name, grid=(C // tc, L // tl),
                  in_specs=[pl.BlockSpec((tl, 3 * tc), lambda j, i: (i, j)),
                            pl.BlockSpec((SUBLANES, 3 * tc), _halo_prev(tl, lambda j: j)),
                            pl.BlockSpec((K, tc), lambda j, i: (0, j))],
                  out_specs=pl.BlockSpec((tl, tc), lambda j, i: (i, j)),
                  out_shape=jax.ShapeDtypeStruct((L, C), BF16),
                  compiler_params=_cparams(("parallel", "parallel")))(Ac, Ac, w)


def _sconv_bwd_dc(Ac, dy, w, name):
    L, C3 = Ac.shape
    C = C3 // 3
    K = w.shape[0]
    tl = _row_tile(L)
    tc = LANES

    def body(a_ref, ap_ref, dy_ref, dc_ref, dw_ref):
        i = pl.program_id(1)
        a = a_ref[...]
        ap = ap_ref[...]
        p = a[:, tc:2 * tc] * a[:, 2 * tc:]
        pp = jnp.where(i == 0, 0.0, ap[:, tc:2 * tc] * ap[:, 2 * tc:])
        dc = dy_ref[...] * a[:, :tc]
        dc_ref[...] = dc

        @pl.when(i == 0)
        def _():
            dw_ref[...] = jnp.zeros_like(dw_ref)

        for k in range(K):
            dw_ref[k * SUBLANES:(k + 1) * SUBLANES, :] += _sum8(dc * _shift_down(p, pp, K - 1 - k))

    return _pcall(body, name=name, grid=(C // tc, L // tl),
                  in_specs=[pl.BlockSpec((tl, 3 * tc), lambda j, i: (i, j)),
                            pl.BlockSpec((SUBLANES, 3 * tc), _halo_prev(tl, lambda j: j)),
                            pl.BlockSpec((tl, tc), lambda j, i: (i, j))],
                  out_specs=[pl.BlockSpec((tl, tc), lambda j, i: (i, j)),
                             pl.BlockSpec((K * SUBLANES, tc), lambda j, i: (0, j))],
                  out_shape=[jax.ShapeDtypeStruct((L, C), F32), jax.ShapeDtypeStruct((K * SUBLANES, C), F32)],
                  compiler_params=_cparams(("parallel", "arbitrary")))(Ac, Ac, dy)


def _sconv_bwd_da(Ac, dy, dc, w, name):
    L, C3 = Ac.shape
    C = C3 // 3
    K = w.shape[0]
    tl = _row_tile(L)
    tc = LANES
    nl = L // tl

    def body(a_ref, ap_ref, dy_ref, dc_ref, dcn_ref, w_ref, o_ref):
        i = pl.program_id(1)
        a = a_ref[...]
        ap = ap_ref[...]
        gc, h = a[:, tc:2 * tc], a[:, 2 * tc:]
        p = gc * h
        pp = jnp.where(i == 0, 0.0, ap[:, tc:2 * tc] * ap[:, 2 * tc:])
        dgb = dy_ref[...] * _conv_taps(p, pp, w_ref, K)
        cur = dc_ref[...]
        nxt = jnp.where(i == nl - 1, 0.0, dcn_ref[...])
        dp = w_ref[K - 1:K, :] * cur
        for k in range(K - 1):
            dp = dp + w_ref[k:k + 1, :] * _shift_up(cur, nxt, K - 1 - k)
        o_ref[...] = jnp.concatenate([dgb, dp * h, dp * gc], axis=1).astype(BF16)

    return _pcall(body, name=name, grid=(C // tc, nl),
                  in_specs=[pl.BlockSpec((tl, 3 * tc), lambda j, i: (i, j)),
                            pl.BlockSpec((SUBLANES, 3 * tc), _halo_prev(tl, lambda j: j)),
                            pl.BlockSpec((tl, tc), lambda j, i: (i, j)),
                            pl.BlockSpec((tl, tc), lambda j, i: (i, j)),
                            pl.BlockSpec((SUBLANES, tc), lambda j, i: (jnp.minimum((i + 1) * (tl // SUBLANES), L // SUBLANES - 1), j)),
                            pl.BlockSpec((K, tc), lambda j, i: (0, j))],
                  out_specs=pl.BlockSpec((tl, 3 * tc), lambda j, i: (i, j)),
                  out_shape=jax.ShapeDtypeStruct((L, C3), BF16),
                  compiler_params=_cparams(("parallel", "parallel")))(Ac, Ac, dy, dc, dc, w)


def _fox_gate_fwd(Af, bf, name):
    L, W = Af.shape
    tl = _pick(L, (512, 256, 128))

    def body(a_ref, b_ref, f_ref, carry):
        i = pl.program_id(0)

        @pl.when(i == 0)
        def _():
            carry[...] = jnp.zeros_like(carry)

        z = a_ref[...] + b_ref[...]
        logf = jnp.minimum(z, 0.0) - jnp.log(1.0 + jnp.exp(-jnp.abs(z)))
        f = _tri_mm(_tri(tl), logf) + carry[...]
        f_ref[...] = f
        carry[...] = f[tl - 1:tl, :]

    row = pl.BlockSpec((tl, W), lambda i: (i, 0))
    return _pcall(body, name=name, grid=(L // tl,), in_specs=[row, pl.BlockSpec((1, W), lambda i: (0, 0))], out_specs=row,
                  out_shape=jax.ShapeDtypeStruct((L, W), F32), scratch_shapes=[pltpu.VMEM((1, W), F32)],
                  compiler_params=_cparams(("arbitrary",)))(Af, bf)


def _fox_gate_bwd(Af, bf, dF, name):
    L, W = Af.shape
    tl = _pick(L, (512, 256, 128))
    nl = L // tl

    def body(a_ref, b_ref, df_ref, o_ref, db_ref, carry):
        i = pl.program_id(0)

        @pl.when(i == 0)
        def _():
            carry[...] = jnp.zeros_like(carry)
            db_ref[...] = jnp.zeros_like(db_ref)

        z = a_ref[...] + b_ref[...]
        dlogf = _tri_mm(_tri(tl, upper=True), df_ref[...]) + carry[...]
        carry[...] = dlogf[0:1, :]
        dz = dlogf * _sigmoid(-z)
        o_ref[...] = dz
        db_ref[...] += _sum8(dz)

    row = pl.BlockSpec((tl, W), lambda i: (nl - 1 - i, 0))
    return _pcall(body, name=name, grid=(nl,),
                  in_specs=[row, pl.BlockSpec((1, W), lambda i: (0, 0)), row],
                  out_specs=[row, pl.BlockSpec((SUBLANES, W), lambda i: (0, 0))],
                  out_shape=[jax.ShapeDtypeStruct((L, W), F32), jax.ShapeDtypeStruct((SUBLANES, W), F32)],
                  scratch_shapes=[pltpu.VMEM((1, W), F32)],
                  compiler_params=_cparams(("arbitrary",)))(Af, bf, dF)


def _attn_tiles(L):
    t = _pick(L, (512, 256, 128))
    return t, t


def _attn_scores(q, k, fq, fk, diag, scale):
    s = _dot(q, k, NT) * scale + (fq - fk)
    if not diag:
        return s
    row = lax.broadcasted_iota(jnp.int32, s.shape, 0)
    col = lax.broadcasted_iota(jnp.int32, s.shape, 1)
    return jnp.where(col <= row, s, NEG)


def _off_and_on_diagonal(qi, ki, step):
    @pl.when(ki < qi)
    def _():
        step(False)

    @pl.when(ki == qi)
    def _():
        step(True)


def _attn_fwd(q, k, v, Fq, Fk, name):
    H, L, Dh = q.shape
    tq, tk = _attn_tiles(L)
    scale = Dh ** -0.5

    def body(q_ref, k_ref, v_ref, fq_ref, fk_ref, o_ref, lse_ref, m_s, l_s, acc_s):
        qi, ki = pl.program_id(1), pl.program_id(2)

        @pl.when(ki == 0)
        def _():
            m_s[...] = jnp.full_like(m_s, NEG)
            l_s[...] = jnp.zeros_like(l_s)
            acc_s[...] = jnp.zeros_like(acc_s)

        def step(diag):
            s = _attn_scores(q_ref[...], k_ref[...], fq_ref[...], fk_ref[...], diag, scale)
            m_new = jnp.maximum(m_s[...], jnp.max(s, axis=-1, keepdims=True))
            p = jnp.exp(s - m_new)
            a = jnp.exp(m_s[...] - m_new)
            l_s[...] = a * l_s[...] + jnp.sum(p, axis=-1, keepdims=True)
            acc_s[...] = a * acc_s[...] + _dot(p.astype(BF16), v_ref[...], NN)
            m_s[...] = m_new
            if diag:
                o_ref[...] = acc_s[...] / l_s[...]
                lse_ref[...] = m_s[...] + jnp.log(l_s[...])

        _off_and_on_diagonal(qi, ki, step)

    qspec = pl.BlockSpec((None, tq, Dh), lambda h, i, j: (h, i, 0))
    kspec = pl.BlockSpec((None, tk, Dh), lambda h, i, j: (h, jnp.minimum(j, i), 0))
    return _pcall(body, name=name, grid=(H, L // tq, L // tk),
                  in_specs=[qspec, kspec, kspec,
                            pl.BlockSpec((None, tq, 1), lambda h, i, j: (h, i, 0)),
                            pl.BlockSpec((None, 1, tk), lambda h, i, j: (h, 0, jnp.minimum(j, i)))],
                  out_specs=[qspec, pl.BlockSpec((None, tq, 1), lambda h, i, j: (h, i, 0))],
                  out_shape=[jax.ShapeDtypeStruct((H, L, Dh), F32), jax.ShapeDtypeStruct((H, L, 1), F32)],
                  scratch_shapes=[pltpu.VMEM((tq, 1), F32), pltpu.VMEM((tq, 1), F32), pltpu.VMEM((tq, Dh), F32)],
                  compiler_params=_cparams(("parallel", "parallel", "arbitrary")))(q, k, v, Fq, Fk)


def _attn_bwd_delta(q, k, v, Fq, Fk, lse, do, name):
    H, L, Dh = q.shape
    tq, tk = _attn_tiles(L)
    scale = Dh ** -0.5

    def body(q_ref, k_ref, v_ref, fq_ref, fk_ref, lse_ref, do_ref, d_ref, acc_s):
        qi, ki = pl.program_id(1), pl.program_id(2)

        @pl.when(ki == 0)
        def _():
            acc_s[...] = jnp.zeros_like(acc_s)

        def step(diag):
            s = _attn_scores(q_ref[...], k_ref[...], fq_ref[...], fk_ref[...], diag, scale)
            p = jnp.exp(s - lse_ref[...])
            dp = _dot(do_ref[...].astype(BF16), v_ref[...], NT)
            acc_s[...] += jnp.sum(p * dp, axis=-1, keepdims=True)
            if diag:
                d_ref[...] = acc_s[...]

        _off_and_on_diagonal(qi, ki, step)

    qspec = pl.BlockSpec((None, tq, Dh), lambda h, i, j: (h, i, 0))
    kspec = pl.BlockSpec((None, tk, Dh), lambda h, i, j: (h, jnp.minimum(j, i), 0))
    cq = pl.BlockSpec((None, tq, 1), lambda h, i, j: (h, i, 0))
    return _pcall(body, name=name, grid=(H, L // tq, L // tk),
                  in_specs=[qspec, kspec, kspec, cq,
                            pl.BlockSpec((None, 1, tk), lambda h, i, j: (h, 0, jnp.minimum(j, i))), cq, qspec],
                  out_specs=cq, out_shape=jax.ShapeDtypeStruct((H, L, 1), F32),
                  scratch_shapes=[pltpu.VMEM((tq, 1), F32)],
                  compiler_params=_cparams(("parallel", "parallel", "arbitrary")))(q, k, v, Fq, Fk, lse, do)


def _attn_bwd_dq(q, k, v, Fq, Fk, delta, lse, do, name):
    H, L, Dh = q.shape
    tq, tk = _attn_tiles(L)
    scale = Dh ** -0.5

    def body(q_ref, k_ref, v_ref, fq_ref, fk_ref, dl_ref, lse_ref, do_ref, dq_ref, acc_s):
        qi, ki = pl.program_id(1), pl.program_id(2)

        @pl.when(ki == 0)
        def _():
            acc_s[...] = jnp.zeros_like(acc_s)

        def step(diag):
            s = _attn_scores(q_ref[...], k_ref[...], fq_ref[...], fk_ref[...], diag, scale)
            p = jnp.exp(s - lse_ref[...])
            dp = _dot(do_ref[...].astype(BF16), v_ref[...], NT)
            ds = p * (dp - dl_ref[...])
            acc_s[...] += _dot(ds.astype(BF16), k_ref[...], NN)
            if diag:
                dq_ref[...] = acc_s[...] * scale

        _off_and_on_diagonal(qi, ki, step)

    qspec = pl.BlockSpec((None, tq, Dh), lambda h, i, j: (h, i, 0))
    kspec = pl.BlockSpec((None, tk, Dh), lambda h, i, j: (h, jnp.minimum(j, i), 0))
    cq = pl.BlockSpec((None, tq, 1), lambda h, i, j: (h, i, 0))
    return _pcall(body, name=name, grid=(H, L // tq, L // tk),
                  in_specs=[qspec, kspec, kspec, cq,
                            pl.BlockSpec((None, 1, tk), lambda h, i, j: (h, 0, jnp.minimum(j, i))), cq, cq, qspec],
                  out_specs=qspec, out_shape=jax.ShapeDtypeStruct((H, L, Dh), F32),
                  scratch_shapes=[pltpu.VMEM((tq, Dh), F32)],
                  compiler_params=_cparams(("parallel", "parallel", "arbitrary")))(q, k, v, Fq, Fk, delta, lse, do)


def _attn_bwd_dkv(q, k, v, Fq, Fk, delta, lse, do, name):
    H, L, Dh = q.shape
    tq, tk = _attn_tiles(L)
    nq = L // tq
    scale = Dh ** -0.5

    def body(q_ref, k_ref, v_ref, fq_ref, fk_ref, dl_ref, lse_ref, do_ref, dk_ref, dv_ref, df_ref, dk_s, dv_s, df_s):
        ki, qi = pl.program_id(1), pl.program_id(2)

        @pl.when(qi == 0)
        def _():
            dk_s[...] = jnp.zeros_like(dk_s)
            dv_s[...] = jnp.zeros_like(dv_s)
            df_s[...] = jnp.zeros_like(df_s)

        def step(diag):
            s = _attn_scores(q_ref[...], k_ref[...], fq_ref[...], fk_ref[...], diag, scale)
            p = jnp.exp(s - lse_ref[...])
            do_ = do_ref[...].astype(BF16)
            dp = _dot(do_, v_ref[...], NT)
            ds = p * (dp - dl_ref[...])
            dv_s[...] += _dot(p.astype(BF16), do_, TN)
            dk_s[...] += _dot(ds.astype(BF16), q_ref[...], TN)
            df_s[...] -= jnp.sum(ds, axis=0, keepdims=True)

        _off_and_on_diagonal(qi, ki, step)

        @pl.when(qi == nq - 1)
        def _():
            dk_ref[...] = dk_s[...] * scale
            dv_ref[...] = dv_s[...]
            df_ref[...] = df_s[...]

    qspec = pl.BlockSpec((None, tq, Dh), lambda h, j, i: (h, jnp.maximum(i, j), 0))
    kspec = pl.BlockSpec((None, tk, Dh), lambda h, j, i: (h, j, 0))
    cq = pl.BlockSpec((None, tq, 1), lambda h, j, i: (h, jnp.maximum(i, j), 0))
    rk = pl.BlockSpec((None, 1, tk), lambda h, j, i: (h, 0, j))
    return _pcall(body, name=name, grid=(H, L // tk, nq),
                  in_specs=[qspec, kspec, kspec, cq, rk, cq, cq, qspec],
                  out_specs=[kspec, kspec, rk],
                  out_shape=[jax.ShapeDtypeStruct((H, L, Dh), F32), jax.ShapeDtypeStruct((H, L, Dh), F32),
                             jax.ShapeDtypeStruct((H, 1, L), F32)],
                  scratch_shapes=[pltpu.VMEM((tk, Dh), F32), pltpu.VMEM((tk, Dh), F32), pltpu.VMEM((1, tk), F32)],
                  compiler_params=_cparams(("parallel", "parallel", "arbitrary")))(q, k, v, Fq, Fk, delta, lse, do)


def _mconv_fwd(xr, w, b, name):
    L, C = xr.shape
    K = w.shape[0]
    tl = _row_tile(L)
    tc = _pick(C, (512, 384, 256, 128))

    def body(x_ref, xp_ref, w_ref, b_ref, o_ref):
        i = pl.program_id(1)
        prev = jnp.where(i == 0, 0.0, xp_ref[...])
        pre = _conv_taps(x_ref[...], prev, w_ref, K) + b_ref[...]
        o_ref[...] = pre * _sigmoid(pre)

    return _pcall(body, name=name, grid=(C // tc, L // tl),
                  in_specs=[pl.BlockSpec((tl, tc), lambda j, i: (i, j)),
                            pl.BlockSpec((SUBLANES, tc), _halo_prev(tl, lambda j: j)),
                            pl.BlockSpec((K, tc), lambda j, i: (0, j)),
                            pl.BlockSpec((1, tc), lambda j, i: (0, j))],
                  out_specs=pl.BlockSpec((tl, tc), lambda j, i: (i, j)),
                  out_shape=jax.ShapeDtypeStruct((L, C), F32),
                  compiler_params=_cparams(("parallel", "parallel")))(xr, xr, w, b.reshape(1, C))


def _mconv_bwd(xr, dact, w, b, name):
    L, C = xr.shape
    K = w.shape[0]
    tl = _row_tile(L)
    tc = _pick(C, (512, 384, 256, 128))
    nl = L // tl

    def body(x_ref, xp_ref, xn_ref, d_ref, dn_ref, w_ref, b_ref, o_ref, dw_ref, db_ref):
        i = pl.program_id(1)
        cur = x_ref[...]
        prev = jnp.where(i == 0, 0.0, xp_ref[...])

        def at_conv_out(x, xprev, d):
            pre = _conv_taps(x, xprev, w_ref, K) + b_ref[...]
            sg = _sigmoid(pre)
            return d * sg * (1.0 + pre * (1.0 - sg))

        dpre = at_conv_out(cur, prev, d_ref[...])
        dpre_n = jnp.where(i == nl - 1, 0.0, at_conv_out(xn_ref[...], cur[tl - SUBLANES:], dn_ref[...]))
        o_ref[...] = _conv_taps_t(dpre, dpre_n, w_ref, K).astype(BF16)

        @pl.when(i == 0)
        def _():
            dw_ref[...] = jnp.zeros_like(dw_ref)
            db_ref[...] = jnp.zeros_like(db_ref)

        db_ref[...] += _sum8(dpre)
        for k in range(K):
            dw_ref[k * SUBLANES:(k + 1) * SUBLANES, :] += _sum8(dpre * _shift_down(cur, prev, K - 1 - k))

    return _pcall(body, name=name, grid=(C // tc, nl),
                  in_specs=[pl.BlockSpec((tl, tc), lambda j, i: (i, j)),
                            pl.BlockSpec((SUBLANES, tc), _halo_prev(tl, lambda j: j)),
                            pl.BlockSpec((SUBLANES, tc), _halo_next(tl, L, SUBLANES)),
                            pl.BlockSpec((tl, tc), lambda j, i: (i, j)),
                            pl.BlockSpec((SUBLANES, tc), _halo_next(tl, L, SUBLANES)),
                            pl.BlockSpec((K, tc), lambda j, i: (0, j)),
                            pl.BlockSpec((1, tc), lambda j, i: (0, j))],
                  out_specs=[pl.BlockSpec((tl, tc), lambda j, i: (i, j)),
                             pl.BlockSpec((K * SUBLANES, tc), lambda j, i: (0, j)),
                             pl.BlockSpec((SUBLANES, tc), lambda j, i: (0, j))],
                  out_shape=[jax.ShapeDtypeStruct((L, C), BF16), jax.ShapeDtypeStruct((K * SUBLANES, C), F32),
                             jax.ShapeDtypeStruct((SUBLANES, C), F32)],
                  compiler_params=_cparams(("parallel", "arbitrary")))(xr, xr, xr, dact, dact, w, b.reshape(1, C))


def _ssd_common(x_r, dt_col, a_col, a_row, mask, CB):
    seg = jnp.where(mask, a_col - a_row, NEG)
    Lm = jnp.exp(seg)
    return Lm, CB * Lm, x_r * dt_col


def _ssd_fwd(x, Bm, Cm, dtc, dtr, bias_c, bias_r, alog_c, alog_r, dsk_c, name):
    Hs, L, P = x.shape
    G, _, N = Bm.shape
    R = Hs // G
    Q = SSM_CHUNK
    nc = L // Q

    def body(x_ref, b_ref, c_ref, dtc_ref, dtr_ref, bc_ref, br_ref, ac_ref, ar_ref, dk_ref, y_ref, hp_ref, st):
        c = pl.program_id(1)

        @pl.when(c == 0)
        def _():
            st[...] = jnp.zeros_like(st)

        dt_c = _softplus(dtc_ref[...] + bc_ref[...])
        dt_r = _softplus(dtr_ref[...] + br_ref[...])
        acs_c = _tri_mm(_tri(Q), dt_c * (-jnp.exp(ac_ref[...])))
        acs_r = _tri_mm(_tri(Q, upper=True), dt_r * (-jnp.exp(ar_ref[...])), tri_first=False)
        Bf = b_ref[...]
        Cb = c_ref[...].astype(BF16)
        CB = _dot(Cb, Bf.astype(BF16), NT)
        mask = lax.broadcasted_iota(jnp.int32, (Q, Q), 0) >= lax.broadcasted_iota(jnp.int32, (Q, Q), 1)
        for r in range(R):
            a_col = acs_c[:, r:r + 1]
            a_row = acs_r[r:r + 1, :]
            xr_ = x_ref[r]
            _, Gm, xt = _ssd_common(xr_, dt_c[:, r:r + 1], a_col, a_row, mask, CB)
            hp = st[r]
            hp_ref[r] = hp
            a_end = acs_c[Q - 1:Q, r:r + 1]
            dte = jnp.exp(a_end - a_col)
            ydiag = _dot(Gm.astype(BF16), xt.astype(BF16), NN)
            yoff = jnp.exp(a_col) * _dot(Cb, hp.astype(BF16), NT)
            st[r] = hp * jnp.exp(a_end) + _dot(xt.astype(BF16), (Bf * dte).astype(BF16), TN)
            y_ref[r] = ydiag + yoff + dk_ref[:, r:r + 1] * xr_

    pc = pl.BlockSpec((None, 1, R), lambda g, c: (g, 0, 0))
    pr = pl.BlockSpec((None, R, 1), lambda g, c: (g, 0, 0))
    xspec = pl.BlockSpec((R, Q, P), lambda g, c: (g, c, 0))
    bspec = pl.BlockSpec((None, Q, N), lambda g, c: (g, c, 0))
    return _pcall(body, name=name, grid=(G, nc),
                  in_specs=[xspec, bspec, bspec, pl.BlockSpec((None, Q, R), lambda g, c: (g, c, 0)),
                            pl.BlockSpec((None, R, Q), lambda g, c: (g, 0, c)), pc, pr, pc, pr, pc],
                  out_specs=[xspec, pl.BlockSpec((None, R, P, N), lambda g, c: (c, g, 0, 0))],
                  out_shape=[jax.ShapeDtypeStruct((Hs, L, P), F32), jax.ShapeDtypeStruct((nc, Hs, P, N), F32)],
                  scratch_shapes=[pltpu.VMEM((R, P, N), F32)],
                  compiler_params=_cparams(("parallel", "arbitrary")))(x, Bm, Cm, dtc, dtr, bias_c, bias_r, alog_c, alog_r, dsk_c)


def _ssd_bwd(x, Bm, Cm, dtc, dtr, bias_c, bias_r, alog_c, alog_r, dsk_c, hprev, dy, name):
    Hs, L, P = x.shape
    G, _, N = Bm.shape
    R = Hs // G
    Q = SSM_CHUNK
    nc = L // Q

    def body(x_ref, b_ref, c_ref, dtc_ref, dtr_ref, bc_ref, br_ref, ac_ref, ar_ref, dk_ref, hp_ref, dy_ref,
             dx_ref, db_ref, dc_ref, ddt_ref, gbias_ref, galog_ref, gdsk_ref, dst):
        c = pl.program_id(1)

        @pl.when(c == 0)
        def _():
            dst[...] = jnp.zeros_like(dst)
            gbias_ref[...] = jnp.zeros_like(gbias_ref)
            galog_ref[...] = jnp.zeros_like(galog_ref)
            gdsk_ref[...] = jnp.zeros_like(gdsk_ref)

        raw_c = dtc_ref[...] + bc_ref[...]
        dt_c = _softplus(raw_c)
        dt_r = _softplus(dtr_ref[...] + br_ref[...])
        A_c = -jnp.exp(ac_ref[...])
        acs_c = _tri_mm(_tri(Q), dt_c * A_c)
        acs_r = _tri_mm(_tri(Q, upper=True), dt_r * (-jnp.exp(ar_ref[...])), tri_first=False)
        Bf = b_ref[...]
        Cf = c_ref[...]
        Bb = Bf.astype(BF16)
        Cb = Cf.astype(BF16)
        CB = _dot(Cb, Bb, NT)
        mask = lax.broadcasted_iota(jnp.int32, (Q, Q), 0) >= lax.broadcasted_iota(jnp.int32, (Q, Q), 1)
        lane = lax.broadcasted_iota(jnp.int32, (Q, R), 1)
        lane1 = lax.broadcasted_iota(jnp.int32, (1, R), 1)
        rowi = lax.broadcasted_iota(jnp.int32, (Q, 1), 0)
        ones = jnp.ones((Q, LANES), F32)
        dCB = jnp.zeros((Q, Q), F32)
        dC = jnp.zeros((Q, N), F32)
        dB = jnp.zeros((Q, N), F32)
        da_mat = jnp.zeros((Q, R), F32)
        ddtx_mat = jnp.zeros((Q, R), F32)
        gd = jnp.zeros((1, R), F32)
        for r in range(R):
            a_col = acs_c[:, r:r + 1]
            a_row = acs_r[r:r + 1, :]
            xr_ = x_ref[r]
            dt_col = dt_c[:, r:r + 1]
            Lm, Gm, xt = _ssd_common(xr_, dt_col, a_col, a_row, mask, CB)
            dyr = dy_ref[r]
            dyb = dyr.astype(BF16)
            xtb = xt.astype(BF16)
            hp = hp_ref[r]
            hpb = hp.astype(BF16)
            dHn = dst[r]
            dHb = dHn.astype(BF16)
            a_end = acs_c[Q - 1:Q, r:r + 1]
            e_end = jnp.exp(a_end)
            ea = jnp.exp(a_col)
            dte = jnp.exp(a_end - a_col)
            dxt = _dot(Gm.astype(BF16), dyb, TN)
            dG = jnp.where(mask, _dot(dyb, xtb, NT), 0.0)
            Mm = dG * Gm
            dCB = dCB + dG * Lm
            colsum = lax.dot_general(Mm, ones, (TN, ((), ())), precision=lax.Precision.HIGHEST,
                                     preferred_element_type=F32)[:, :1]
            da = jnp.sum(Mm, axis=1, keepdims=True) - colsum
            edy = ea * dyr
            edyb = edy.astype(BF16)
            W = _dot(Cb, hpb, NT)
            dC = dC + _dot(edyb, hpb, NN)
            dhp = _dot(edyb, Cb, TN)
            da = da + jnp.sum(edy * W, axis=1, keepdims=True)
            dxt = dxt + _dot((Bf * dte).astype(BF16), dHb, NT)
            XH = _dot(xtb, dHb, NN)
            dB = dB + dte * XH
            t = jnp.sum(XH * Bf, axis=1, keepdims=True) * dte
            da = da - t
            da_end = jnp.sum(t, axis=0, keepdims=True) + jnp.sum(jnp.sum(dHn * hp, axis=1, keepdims=True), axis=0, keepdims=True) * e_end
            da = da + jnp.where(rowi == Q - 1, da_end, 0.0)
            dst[r] = dhp + dHn * e_end
            dsk = dk_ref[:, r:r + 1]
            dx_ref[r] = dxt * dt_col + dsk * dyr
            da_mat = jnp.where(lane == r, da, da_mat)
            ddtx_mat = jnp.where(lane == r, jnp.sum(dxt * xr_, axis=1, keepdims=True), ddtx_mat)
            gd = jnp.where(lane1 == r, jnp.sum(jnp.sum(dyr * xr_, axis=1, keepdims=True), axis=0, keepdims=True), gd)
        dCBb = dCB.astype(BF16)
        dc_ref[...] = dC + _dot(dCBb, Bb, NN)
        db_ref[...] = dB + _dot(dCBb, Cb, TN)
        ddtA = _tri_mm(_tri(Q, upper=True), da_mat)
        ddt_raw = (ddtA * A_c + ddtx_mat) * _sigmoid(raw_c)
        ddt_ref[...] = ddt_raw
        gbias_ref[...] += jnp.sum(ddt_raw, axis=0, keepdims=True)
        galog_ref[...] += jnp.sum(ddtA * dt_c, axis=0, keepdims=True) * A_c
        gdsk_ref[...] += gd

    rc = lambda c: nc - 1 - c
    pc = pl.BlockSpec((None, 1, R), lambda g, c: (g, 0, 0))
    pr = pl.BlockSpec((None, R, 1), lambda g, c: (g, 0, 0))
    xspec = pl.BlockSpec((R, Q, P), lambda g, c: (g, rc(c), 0))
    bspec = pl.BlockSpec((None, Q, N), lambda g, c: (g, rc(c), 0))
    dtspec = pl.BlockSpec((None, Q, R), lambda g, c: (g, rc(c), 0))
    return _pcall(body, name=name, grid=(G, nc),
                  in_specs=[xspec, bspec, bspec, dtspec, pl.BlockSpec((None, R, Q), lambda g, c: (g, 0, rc(c))),
                            pc, pr, pc, pr, pc, pl.BlockSpec((None, R, P, N), lambda g, c: (rc(c), g, 0, 0)), xspec],
                  out_specs=[xspec, bspec, bspec, dtspec, pc, pc, pc],
                  out_shape=[jax.ShapeDtypeStruct((Hs, L, P), F32), jax.ShapeDtypeStruct((G, L, N), F32),
                             jax.ShapeDtypeStruct((G, L, N), F32), jax.ShapeDtypeStruct((G, L, R), F32),
                             jax.ShapeDtypeStruct((G, 1, R), F32), jax.ShapeDtypeStruct((G, 1, R), F32),
                             jax.ShapeDtypeStruct((G, 1, R), F32)],
                  scratch_shapes=[pltpu.VMEM((R, P, N), F32)],
                  compiler_params=_cparams(("parallel", "arbitrary")))(
        x, Bm, Cm, dtc, dtr, bias_c, bias_r, alog_c, alog_r, dsk_c, hprev, dy)


def _gnorm_fwd(y, z, g, name):
    L, Dn = y.shape
    gs = Dn // SSM_GROUPS
    tl = _row_tile(L)

    def body(y_ref, z_ref, g_ref, o_ref):
        for k in range(SSM_GROUPS):
            sl = slice(k * gs, (k + 1) * gs)
            zz = z_ref[:, sl]
            u = y_ref[:, sl] * zz * _sigmoid(zz)
            rstd = lax.rsqrt(jnp.mean(u * u, axis=-1, keepdims=True) + RMS_EPS)
            o_ref[:, sl] = (u * rstd * g_ref[:, sl]).astype(BF16)

    row = pl.BlockSpec((tl, Dn), lambda i: (i, 0))
    return _pcall(body, name=name, grid=(L // tl,), in_specs=[row, row, pl.BlockSpec((1, Dn), lambda i: (0, 0))],
                  out_specs=row, out_shape=jax.ShapeDtypeStruct((L, Dn), BF16),
                  compiler_params=_cparams(("parallel",)))(y, z, g.reshape(1, Dn))


def _gnorm_bwd(y, z, g, dout, name):
    L, Dn = y.shape
    gs = Dn // SSM_GROUPS
    tl = _row_tile(L)

    def body(y_ref, z_ref, g_ref, d_ref, dy_ref, dz_ref, dg_ref):
        i = pl.program_id(0)

        @pl.when(i == 0)
        def _():
            dg_ref[...] = jnp.zeros_like(dg_ref)

        for k in range(SSM_GROUPS):
            sl = slice(k * gs, (k + 1) * gs)
            zz = z_ref[:, sl]
            yy = y_ref[:, sl]
            sg = _sigmoid(zz)
            sil = zz * sg
            u = yy * sil
            rstd = lax.rsqrt(jnp.mean(u * u, axis=-1, keepdims=True) + RMS_EPS)
            n = u * rstd
            d = d_ref[:, sl]
            dn = d * g_ref[:, sl]
            du = rstd * (dn - n * jnp.mean(dn * n, axis=-1, keepdims=True))
            dy_ref[:, sl] = du * sil
            dz_ref[:, sl] = (du * yy * sg * (1.0 + zz * (1.0 - sg))).astype(BF16)
            dg_ref[:, sl] += _sum8(d * n)

    row = pl.BlockSpec((tl, Dn), lambda i: (i, 0))
    return _pcall(body, name=name, grid=(L // tl,), in_specs=[row, row, pl.BlockSpec((1, Dn), lambda i: (0, 0)), row],
                  out_specs=[row, row, pl.BlockSpec((SUBLANES, Dn), lambda i: (0, 0))],
                  out_shape=[jax.ShapeDtypeStruct((L, Dn), F32), jax.ShapeDtypeStruct((L, Dn), BF16),
                             jax.ShapeDtypeStruct((SUBLANES, Dn), F32)],
                  compiler_params=_cparams(("arbitrary",)))(y, z, g.reshape(1, Dn), dout)


def _adamw(w, g, m, v, name):
    rows, W = w.shape
    tr = _pick(rows, (512, 256, 128, 64, 32, 16, 8))
    c1 = 1.0 / (1.0 - ADAM_B1 ** ADAM_STEP)
    c2 = 1.0 / (1.0 - ADAM_B2 ** ADAM_STEP)

    def body(w_ref, g_ref, m_ref, v_ref, d_ref, nm_ref, nv_ref):
        g_ = g_ref[...]
        nm = ADAM_B1 * m_ref[...] + (1.0 - ADAM_B1) * g_
        nv = ADAM_B2 * v_ref[...] + (1.0 - ADAM_B2) * (g_ * g_)
        nm_ref[...] = nm
        nv_ref[...] = nv
        d_ref[...] = -ADAM_LR * ((nm * c1) / (jnp.sqrt(nv * c2) + ADAM_EPS) + ADAM_WD * w_ref[...])

    blk = pl.BlockSpec((tr, W), lambda i: (i, 0))
    return _pcall(body, name=name, grid=(rows // tr,), in_specs=[blk] * 4, out_specs=[blk] * 3,
                  out_shape=[jax.ShapeDtypeStruct((rows, W), F32)] * 3, compiler_params=_cparams(("parallel",)))(w, g, m, v)


def _sum_slots(x, name, extra=None):
    n, rows, W = x.shape
    tr = _pick(rows, (512, 256, 128, 64, 32, 16, 8))
    has_extra = extra is not None

    def body(*refs):
        if has_extra:
            e_ref, x_ref, o_ref = refs
            acc = e_ref[...].astype(F32)
            start = 0
        else:
            x_ref, o_ref = refs
            acc = x_ref[0].astype(F32)
            start = 1
        for s in range(start, n):
            acc = acc + x_ref[s].astype(F32)
        o_ref[...] = acc

    blk = pl.BlockSpec((tr, W), lambda i: (i, 0))
    xblk = pl.BlockSpec((n, tr, W), lambda i: (0, i, 0))
    return _pcall(body, name=name, grid=(rows // tr,), in_specs=([blk] if has_extra else []) + [xblk], out_specs=blk,
                  out_shape=jax.ShapeDtypeStruct((rows, W), F32), compiler_params=_cparams(("parallel",)))(
        *(([extra] if has_extra else []) + [x]))


def _add_pairs(a, b, name):
    n, rows, W = a.shape
    tr = _pick(rows, (512, 256, 128, 64, 32, 16, 8))

    def body(a_ref, b_ref, o_ref):
        o_ref[...] = (a_ref[...].astype(F32) + b_ref[...].astype(F32)).astype(BF16)

    blk = pl.BlockSpec((None, tr, W), lambda s, i: (s, i, 0))
    return _pcall(body, name=name, grid=(n, rows // tr), in_specs=[blk, blk], out_specs=blk,
                  out_shape=jax.ShapeDtypeStruct((n, rows, W), BF16), compiler_params=_cparams(("parallel", "parallel")))(a, b)


MESH = pl.DeviceIdType.MESH
HBM_SPEC = pl.BlockSpec(memory_space=pl.ANY)


def _me():
    return lax.axis_index("x"), lax.axis_index("y"), lax.axis_index("c")


def _all_gather(arrs, name):
    n = len(arrs)

    def body(*refs):
        ins, outs = refs[:n], refs[n:2 * n]
        send_sems, recv_sems, local_sems = refs[2 * n:]
        x, y, c = _me()
        me, sib = (x, y, c), (x, y, 1 - c)
        chips = [(1 - x, y), (x, 1 - y), (1 - x, 1 - y)]

        def slot(a, dev):
            return outs[a].at[4 * dev[0] + 2 * dev[1] + dev[2]]

        def copy(a, k, block, to, src=None):
            return pltpu.make_async_remote_copy(src_ref=slot(a, block) if src is None else src, dst_ref=slot(a, block),
                                                send_sem=send_sems.at[a * 7 + k], recv_sem=recv_sems.at[a * 7 + k],
                                                device_id=to, device_id_type=MESH)

        mine = [pltpu.make_async_copy(ins[a], slot(a, me), local_sems.at[a]) for a in range(n)]
        for cp in mine:
            cp.start()
        first = []
        for a in range(n):
            first.append(copy(a, 0, me, sib, src=ins[a]))
            first += [copy(a, 1 + j, me, (*chip, c), src=ins[a]) for j, chip in enumerate(chips)]
        for cp in first:
            cp.start()
        passed = []
        for j, chip in enumerate(chips):
            for a in range(n):
                copy(a, 1 + j, (*chip, c), me).wait_recv()
                fw = copy(a, 4 + j, (*chip, c), sib)
                fw.start()
                passed.append(fw)
        for a in range(n):
            copy(a, 0, sib, me).wait_recv()
            for j, chip in enumerate(chips):
                copy(a, 4 + j, (*chip, 1 - c), me).wait_recv()
        for cp in first + passed:
            cp.wait_send()
        for cp in mine:
            cp.wait()

    return _pcall(body, name=name, in_specs=[HBM_SPEC] * n, out_specs=[HBM_SPEC] * n,
                  out_shape=[jax.ShapeDtypeStruct((N_DEV,) + a.shape, a.dtype) for a in arrs],
                  scratch_shapes=[pltpu.SemaphoreType.DMA((7 * n,)), pltpu.SemaphoreType.DMA((7 * n,)),
                                  pltpu.SemaphoreType.DMA((n,))])(*arrs)


def _rs_sibling(gs, name):
    n = len(gs)

    def body(*refs):
        g_refs, o_refs = refs[:n], refs[n:2 * n]
        send_sems, recv_sems = refs[2 * n:]
        x, y, c = _me()
        sib = (x, y, 1 - c)
        cps = [pltpu.make_async_remote_copy(src_ref=g_refs[a].at[2 * q + (1 - c)], dst_ref=o_refs[a].at[q],
                                            send_sem=send_sems.at[4 * a + q], recv_sem=recv_sems.at[4 * a + q],
                                            device_id=sib, device_id_type=MESH) for a in range(n) for q in range(4)]
        for cp in cps:
            cp.start()
        for cp in cps:
            cp.wait()

    return _pcall(body, name=name, in_specs=[HBM_SPEC] * n, out_specs=[HBM_SPEC] * n,
                  out_shape=[jax.ShapeDtypeStruct((4,) + g.shape[1:], g.dtype) for g in gs],
                  scratch_shapes=[pltpu.SemaphoreType.DMA((4 * n,)), pltpu.SemaphoreType.DMA((4 * n,))])(*gs)


def _rs_chips(ps, name):
    n = len(ps)

    def body(*refs):
        p_refs, o_refs = refs[:n], refs[n:2 * n]
        send_sems, recv_sems = refs[2 * n:]
        x, y, c = _me()
        chips = [(1 - x, y), (x, 1 - y), (1 - x, 1 - y)]
        cps = [pltpu.make_async_remote_copy(src_ref=p_refs[a].at[2 * chip[0] + chip[1]], dst_ref=o_refs[a].at[j],
                                            send_sem=send_sems.at[3 * a + j], recv_sem=recv_sems.at[3 * a + j],
                                            device_id=(*chip, c), device_id_type=MESH)
               for j, chip in enumerate(chips) for a in range(n)]
        for cp in cps:
            cp.start()
        for cp in cps:
            cp.wait()

    return _pcall(body, name=name, in_specs=[HBM_SPEC] * n, out_specs=[HBM_SPEC] * n,
                  out_shape=[jax.ShapeDtypeStruct((3,) + p.shape[1:], p.dtype) for p in ps],
                  scratch_shapes=[pltpu.SemaphoreType.DMA((3 * n,)), pltpu.SemaphoreType.DMA((3 * n,))])(*ps)


def _reduce_scatter(gs, name):
    x, y, c = _me()
    from_sib = _rs_sibling(gs, name + "_sib")
    pairs = []
    for a, (g, fs) in enumerate(zip(gs, from_sib)):
        own = g.reshape((4, 2) + g.shape[1:])
        pairs.append(_add_pairs(jnp.where(c == 0, own[:, 0], own[:, 1]), fs, f"{name}_pair{a}"))
    from_chips = _rs_chips(pairs, name + "_chips")
    return [_sum_slots(fc, f"{name}_sum{a}", extra=lax.dynamic_index_in_dim(p, 2 * x + y, axis=0, keepdims=False))
            for a, (p, fc) in enumerate(zip(pairs, from_chips))]


BIG = ("even_w_in", "even_w_out", "odd_w_in", "odd_w_out", "ffn_w_up", "ffn_w_down", "ple_w_proj", "ple_w_gate")
SMALL_SHARDED = ("even_conv_w", "odd_conv_w", "odd_conv_b", "odd_norm_g", "ffn_conv_w")
REPLICATED = ("even_b_f", "odd_dt_bias", "odd_a_log", "odd_d_skip", "ln_mix_g", "ln_mix_b", "ffn_conv_b",
              "ln_ffn_g", "ln_ffn_b", "ple_b_gate")
WEIGHTS = ("even_w_in", "even_b_f", "even_conv_w", "even_w_out", "odd_w_in", "odd_conv_w", "odd_conv_b", "odd_dt_bias",
           "odd_a_log", "odd_d_skip", "odd_norm_g", "odd_w_out", "ln_mix_g", "ln_mix_b", "ffn_w_up", "ffn_conv_w",
           "ffn_conv_b", "ffn_w_down", "ln_ffn_g", "ln_ffn_b", "ple_w_proj", "ple_w_gate", "ple_b_gate")


def _full_shapes():
    d = _dims()
    return {
        "even_w_in": ((1, D_MODEL, d["even_in"]), 2), "even_b_f": ((1, FOX_HEADS), None),
        "even_conv_w": ((1, CONV_WIDTH, CONV_DIM), 2), "even_w_out": ((1, d["even_mix"], D_MODEL), 1),
        "odd_w_in": ((1, D_MODEL, d["odd_in"]), 2), "odd_conv_w": ((1, SSM_CONV_WIDTH, d["conv_ch"]), 2),
        "odd_conv_b": ((1, d["conv_ch"]), 1), "odd_dt_bias": ((1, d["ssm_heads"]), None),
        "odd_a_log": ((1, d["ssm_heads"]), None), "odd_d_skip": ((1, d["ssm_heads"]), None),
        "odd_norm_g": ((1, d["ssm_inner"]), 1), "odd_w_out": ((1, d["ssm_inner"], D_MODEL), 1),
        "ln_mix_g": ((DEPTH, D_MODEL), None), "ln_mix_b": ((DEPTH, D_MODEL), None),
        "ffn_w_up": ((DEPTH, D_MODEL, 2 * D_FF), 2), "ffn_conv_w": ((DEPTH, FFN_CONV_WIDTH, 2 * D_FF), 2),
        "ffn_conv_b": ((DEPTH, 2 * D_FF), None), "ffn_w_down": ((DEPTH, D_FF, D_MODEL), 1),
        "ln_ffn_g": ((DEPTH, D_MODEL), None), "ln_ffn_b": ((DEPTH, D_MODEL), None),
        "ple_w_proj": ((DEPTH, PLE_DIM, D_MODEL), 2), "ple_w_gate": ((DEPTH, D_MODEL, D_MODEL), 1),
        "ple_b_gate": ((DEPTH, D_MODEL), None),
    }


def _shard_shape(name):
    shape, ax = _full_shapes()[name]
    if ax is None:
        return shape
    return tuple(s // N_DEV if i == ax else s for i, s in enumerate(shape))


def _as2d(a, lead=0):
    return a.reshape(a.shape[:lead] + (-1, a.shape[-1]))


def _part_rows(shape):
    n = int(np.prod(shape))
    return -(-(-(-n // PACK_W)) // SUBLANES) * SUBLANES


def _pack_small(parts):
    out = []
    for p in parts:
        n, rows = int(np.prod(p.shape)), _part_rows(p.shape)
        out.append(jnp.pad(p.reshape(-1).astype(F32), (0, rows * PACK_W - n)).reshape(rows, PACK_W))
    return jnp.concatenate(out, axis=0)


def _unpack_small(pack, shapes):
    lead = pack.shape[:-2]
    out, off = [], 0
    for s in shapes:
        n, rows = int(np.prod(s)), _part_rows(s)
        part = pack[..., off:off + rows, :].reshape(lead + (-1,))[..., :n]
        out.append(part.reshape(lead + tuple(s)))
        off += rows
    return out


def _assemble(gathered, name):
    shape, ax = _full_shapes()[name]
    return jnp.moveaxis(gathered, 0, ax).reshape(shape)


def _split_dest(full, name):
    shape, ax = _full_shapes()[name]
    sh = shape[:ax] + (N_DEV, shape[ax] // N_DEV) + shape[ax + 1:]
    return jnp.moveaxis(full.reshape(sh), ax, 0)


def _interleave_cols(w, parts, tc):
    C = w.shape[-1] // parts
    sh = w.shape[:-1]
    return w.reshape(sh + (parts, C // tc, tc)).swapaxes(-3, -2).reshape(sh + (parts * C,))


def _deinterleave_cols(w, parts, tc):
    C = w.shape[-1] // parts
    sh = w.shape[:-1]
    return w.reshape(sh + (C // tc, parts, tc)).swapaxes(-3, -2).reshape(sh + (parts * C,))


def _ffn_tc():
    return _pick(D_FF, (256, 128))


def _heads_first(a, heads):
    L = a.shape[0]
    return a.reshape(L, heads, -1).transpose(1, 0, 2)


def _heads_last(a):
    h, L, d = a.shape
    return a.transpose(1, 0, 2).reshape(L, h * d)


def _pad_cols(a, to):
    return jnp.pad(a, ((0, 0), (0, to - a.shape[1])))


def _tail_fwd(i, h_in, mix, p_i, W, sp):
    r1, h1, h1b = _ln_fwd(h_in, mix, sp["ln_mix_g"][i], sp["ln_mix_b"][i], f"ln_mix_fwd{i}")
    U = _mm(h1b, W["ffn_up"][i], "nn", F32, f"ffn_up{i}")
    S = _ffn_act_fwd(U, sp["ffn_conv_w_il"][i], sp["ffn_conv_b_il"][i], f"ffn_act_fwd{i}")
    ffn = _mm(S, W["ffn_down"][i], "nn", F32, f"ffn_down{i}")
    r2, h2, h2b = _ln_fwd(h1, ffn, sp["ln_ffn_g"][i], sp["ln_ffn_b"][i], f"ln_ffn_fwd{i}")
    G = _mm(h2b, W["ple_gate"][i], "nn", F32, f"ple_gate{i}")
    E = _mm(p_i, W["ple_proj"][i], "nn", F32, f"ple_proj{i}")
    h3, h3b = _ple_fwd(h2, G, sp["ple_b_gate"][i], E, f"ple_fwd{i}")
    return h3, h3b, dict(r1=r1, h1b=h1b, U=U, S=S, r2=r2, h2b=h2b, G=G, E=E, p=p_i)


def _tail_bwd(i, dh3, sv, W, sp, grads):
    alpha = _alpha()
    dE, dGp, dbg = _ple_bwd(dh3, sv["G"], sp["ple_b_gate"][i], sv["E"], f"ple_bwd{i}")
    grads["ple_b_gate"][i] = dbg.sum(0)
    grads["ple_w_proj"][i] = _mm(sv["p"], dE, "tn", F32, f"d_ple_proj{i}")
    grads["ple_w_gate"][i] = _mm(sv["h2b"], dGp, "tn", F32, f"d_ple_gate{i}")
    dh2 = _mm(dGp, W["ple_gate"][i], "nt", F32, f"dx_ple_gate{i}", add=dh3)
    dr2, dr2b, dg, db = _ln_bwd(sv["r2"], dh2, sp["ln_ffn_g"][i], f"ln_ffn_bwd{i}")
    grads["ln_ffn_g"][i], grads["ln_ffn_b"][i] = dg.sum(0), db.sum(0)
    grads["ffn_w_down"][i] = _mm(sv["S"], dr2b, "tn", F32, f"d_ffn_down{i}")
    dS = _mm(dr2b, W["ffn_down"][i], "nt", BF16, f"dx_ffn_down{i}")
    dU, dcw, dcb = _ffn_act_bwd(sv["U"], dS, sp["ffn_conv_w_il"][i], sp["ffn_conv_b_il"][i], f"ffn_act_bwd{i}")
    K = FFN_CONV_WIDTH
    tc = _ffn_tc()
    grads["ffn_conv_w"][i] = _deinterleave_cols(dcw.reshape(K, SUBLANES, -1).sum(1), 2, tc)
    grads["ffn_conv_b"][i] = _deinterleave_cols(dcb.sum(0), 2, tc)
    grads["ffn_w_up"][i] = _deinterleave_cols(_mm(sv["h1b"], dU, "tn", F32, f"d_ffn_up{i}"), 2, tc)
    dh1 = _mm(dU, W["ffn_up"][i], "nt", F32, f"dx_ffn_up{i}", add=dr2, add_scale=alpha)
    dr1, dr1b, dg, db = _ln_bwd(sv["r1"], dh1, sp["ln_mix_g"][i], f"ln_mix_bwd{i}")
    grads["ln_mix_g"][i], grads["ln_mix_b"][i] = dg.sum(0), db.sum(0)
    return dr1, dr1b


def _even_fwd(h, W, sp):
    L = h.shape[0]
    H, Dh = FOX_HEADS, FOX_HEAD_DIM
    Ac = _mm(h, W["even_in_conv"], "nn", F32, "even_in_conv")
    qkv = _mm(h, W["even_in_qkv"], "nn", BF16, "even_in_qkv")
    Af = _mm(h, W["even_in_f"], "nn", F32, "even_in_f")
    y_a = _sconv_fwd(Ac, sp["even_conv_w_il"], "sconv_fwd")
    Fc = _fox_gate_fwd(Af, sp["even_b_f_pad"], "fox_gate_fwd")
    Fh = Fc[:, :H].T
    Fq, Fk = Fh.reshape(H, L, 1), Fh.reshape(H, 1, L)
    fd = H * Dh
    q, k, v = (_heads_first(qkv[:, j * fd:(j + 1) * fd], H) for j in range(3))
    o, lse = _attn_fwd(q, k, v, Fq, Fk, "attn_fwd")
    Y = jnp.concatenate([y_a, _heads_last(o).astype(BF16)], axis=1)
    mix = _mm(Y, W["even_out"], "nn", F32, "even_out")
    return mix, dict(h=h, Ac=Ac, Af=Af, q=q, k=k, v=v, Fq=Fq, Fk=Fk, o=o, lse=lse, Y=Y)


def _even_bwd(dmix, dres, sv, W, sp, grads):
    H, Dh = FOX_HEADS, FOX_HEAD_DIM
    C = CONV_DIM
    L = dmix.shape[0]
    grads["even_w_out"][0] = _mm(sv["Y"], dmix, "tn", F32, "d_even_out")
    dY = _mm(dmix, W["even_out"], "nt", F32, "dx_even_out")
    dya = dY[:, :C]
    do = _heads_first(dY[:, C:], H)
    dc, dcw = _sconv_bwd_dc(sv["Ac"], dya, sp["even_conv_w_il"], "sconv_bwd_dc")
    grads["even_conv_w"][0] = dcw.reshape(CONV_WIDTH, SUBLANES, -1).sum(1)
    dAc = _sconv_bwd_da(sv["Ac"], dya, dc, sp["even_conv_w_il"], "sconv_bwd_da")
    delta = _attn_bwd_delta(sv["q"], sv["k"], sv["v"], sv["Fq"], sv["Fk"], sv["lse"], do, "attn_bwd_delta")
    args = (sv["q"], sv["k"], sv["v"], sv["Fq"], sv["Fk"], delta, sv["lse"], do)
    dq = _attn_bwd_dq(*args, "attn_bwd_dq")
    dk, dv, dFk = _attn_bwd_dkv(*args, "attn_bwd_dkv")
    dqkv = jnp.concatenate([_heads_last(dq), _heads_last(dk), _heads_last(dv)], axis=1).astype(BF16)
    dF = _pad_cols(dFk.reshape(H, L).T, LANES)
    dAf, dbf = _fox_gate_bwd(sv["Af"], sp["even_b_f_pad"], dF, "fox_gate_bwd")
    grads["even_b_f"][0] = dbf.sum(0)[:H]
    h = sv["h"]
    gc = _deinterleave_cols(_mm(h, dAc, "tn", F32, "d_even_in_conv"), 3, LANES)
    gq = _mm(h, dqkv, "tn", F32, "d_even_in_qkv")
    gf = _mm(h, dAf, "tn", F32, "d_even_in_f")[:, :H]
    grads["even_w_in"][0] = jnp.concatenate([gc, gq, gf], axis=1)
    dh = _mm(dAc, W["even_in_conv"], "nt", F32, "dx_even_in_conv", add=dres, add_scale=_alpha())
    dh = _mm(dqkv, W["even_in_qkv"], "nt", F32, "dx_even_in_qkv", add=dh)
    dh = _mm(dAf, W["even_in_f"], "nt", F32, "dx_even_in_f", add=dh)
    return dh


def _group_layouts(v, G):
    R = v.shape[0] // G
    return v.reshape(G, 1, R), v.reshape(G, R, 1)


def _odd_fwd(h, W, sp):
    d = _dims()
    L = h.shape[0]
    Hs, G, N, P = d["ssm_heads"], SSM_GROUPS, SSM_STATE, SSM_HEAD_DIM
    R = Hs // G
    inner = d["ssm_inner"]
    z = _mm(h, W["odd_in_z"], "nn", F32, "odd_in_z")
    xr = _mm(h, W["odd_in_x"], "nn", F32, "odd_in_x")
    dtp = _mm(h, W["odd_in_dt"], "nn", F32, "odd_in_dt")
    act = _mconv_fwd(xr, sp["odd_conv_w"], sp["odd_conv_b"], "mconv_fwd")
    xs = _heads_first(act[:, :inner], Hs)
    Bm = _heads_first(act[:, inner:inner + G * N], G)
    Cm = _heads_first(act[:, inner + G * N:], G)
    dtg = dtp[:, :Hs].reshape(L, G, R)
    dtc, dtr = dtg.transpose(1, 0, 2), dtg.transpose(1, 2, 0)
    prm = _group_layouts(sp["odd_dt_bias"], G) + _group_layouts(sp["odd_a_log"], G) + (_group_layouts(sp["odd_d_skip"], G)[0],)
    ssd_in = (xs, Bm, Cm, dtc, dtr) + prm
    y3, hprev = _ssd_fwd(*ssd_in, "ssd_fwd")
    y = _heads_last(y3)
    u = _gnorm_fwd(y, z, sp["odd_norm_g"], "gnorm_fwd")
    mix = _mm(u, W["odd_out"], "nn", F32, "odd_out")
    return mix, dict(h=h, z=z, xr=xr, ssd_in=ssd_in, hprev=hprev, y=y, u=u)


def _odd_bwd(dmix, dres, sv, W, sp, grads):
    d = _dims()
    L = dmix.shape[0]
    Hs, G, N, P = d["ssm_heads"], SSM_GROUPS, SSM_STATE, SSM_HEAD_DIM
    grads["odd_w_out"][0] = _mm(sv["u"], dmix, "tn", F32, "d_odd_out")
    du = _mm(dmix, W["odd_out"], "nt", F32, "dx_odd_out")
    dy, dz, dg = _gnorm_bwd(sv["y"], sv["z"], sp["odd_norm_g"], du, "gnorm_bwd")
    grads["odd_norm_g"][0] = dg.sum(0)
    dx3, dB, dC, ddt, gbias, galog, gdsk = _ssd_bwd(*sv["ssd_in"], sv["hprev"], _heads_first(dy, Hs), "ssd_bwd")
    grads["odd_dt_bias"][0] = gbias.reshape(Hs)
    grads["odd_a_log"][0] = galog.reshape(Hs)
    grads["odd_d_skip"][0] = gdsk.reshape(Hs)
    dact = jnp.concatenate([_heads_last(dx3), _heads_last(dB), _heads_last(dC)], axis=1)
    dxr, dcw, dcb = _mconv_bwd(sv["xr"], dact, sp["odd_conv_w"], sp["odd_conv_b"], "mconv_bwd")
    grads["odd_conv_w"][0] = dcw.reshape(SSM_CONV_WIDTH, SUBLANES, -1).sum(1)
    grads["odd_conv_b"][0] = dcb.sum(0)
    ddtp = _pad_cols(ddt.transpose(1, 0, 2).reshape(L, Hs), W["odd_in_dt"].shape[1])
    h = sv["h"]
    gz = _mm(h, dz, "tn", F32, "d_odd_in_z")
    gx = _mm(h, dxr, "tn", F32, "d_odd_in_x")
    gdt = _mm(h, ddtp, "tn", F32, "d_odd_in_dt")[:, :Hs]
    grads["odd_w_in"][0] = jnp.concatenate([gz, gx, gdt], axis=1)
    dh = _mm(dz, W["odd_in_z"], "nt", F32, "dx_odd_in_z", add=dres, add_scale=_alpha())
    dh = _mm(dxr, W["odd_in_x"], "nt", F32, "dx_odd_in_x", add=dh)
    dh = _mm(ddtp, W["odd_in_dt"], "nt", F32, "dx_odd_in_dt", add=dh)
    return dh


def _prepare_weights(full):
    d = _dims()
    C, fd, H = CONV_DIM, d["fox_dim"], FOX_HEADS
    tc = _ffn_tc()
    W, sp = {}, {}
    ew = full["even_w_in"][0]
    W["even_in_conv"] = _interleave_cols(ew[:, :3 * C], 3, LANES)
    W["even_in_qkv"] = ew[:, 3 * C:3 * C + 3 * fd]
    W["even_in_f"] = _pad_cols(ew[:, 3 * C + 3 * fd:], LANES)
    W["even_out"] = full["even_w_out"][0]
    ow = full["odd_w_in"][0]
    inner, cch, Hs = d["ssm_inner"], d["conv_ch"], d["ssm_heads"]
    W["odd_in_z"] = ow[:, :inner]
    W["odd_in_x"] = ow[:, inner:inner + cch]
    W["odd_in_dt"] = _pad_cols(ow[:, inner + cch:], -(-Hs // LANES) * LANES)
    W["odd_out"] = full["odd_w_out"][0]
    W["ffn_up"] = [_interleave_cols(full["ffn_w_up"][i], 2, tc) for i in range(DEPTH)]
    W["ffn_down"] = [full["ffn_w_down"][i] for i in range(DEPTH)]
    W["ple_proj"] = [full["ple_w_proj"][i] for i in range(DEPTH)]
    W["ple_gate"] = [full["ple_w_gate"][i] for i in range(DEPTH)]
    sp["even_conv_w_il"] = full["even_conv_w"][0]
    sp["even_b_f_pad"] = _pad_cols(full["even_b_f"], LANES)
    sp["odd_conv_w"] = full["odd_conv_w"][0]
    sp["odd_conv_b"] = full["odd_conv_b"][0]
    sp["odd_norm_g"] = full["odd_norm_g"][0]
    for n in ("odd_dt_bias", "odd_a_log", "odd_d_skip"):
        sp[n] = full[n][0]
    for n in ("ln_mix_g", "ln_mix_b", "ln_ffn_g", "ln_ffn_b", "ple_b_gate"):
        sp[n] = full[n]
    sp["ffn_conv_w_il"] = [_interleave_cols(full["ffn_conv_w"][i], 2, tc) for i in range(DEPTH)]
    sp["ffn_conv_b_il"] = [_interleave_cols(full["ffn_conv_b"][i], 2, tc) for i in range(DEPTH)]
    return W, sp


def _local_step(x, p, target, full):
    W, sp = _prepare_weights(full)
    grads = {n: [None] * _full_shapes()[n][0][0] for n in WEIGHTS}
    pb = p.astype(BF16)
    mix0, sv_e = _even_fwd(x.astype(BF16), W, sp)
    h3_0, h3_0b, sv_t0 = _tail_fwd(0, x, mix0, pb[0], W, sp)
    mix1, sv_o = _odd_fwd(h3_0b, W, sp)
    h3_1, _, sv_t1 = _tail_fwd(1, h3_0, mix1, pb[1], W, sp)
    dh, sq = _loss_head(h3_1, target, "loss_head")
    dr1, dr1b = _tail_bwd(1, dh, sv_t1, W, sp, grads)
    dh = _odd_bwd(dr1b, dr1, sv_o, W, sp, grads)
    dr1, dr1b = _tail_bwd(0, dh, sv_t0, W, sp, grads)
    dx = _even_bwd(dr1b, dr1, sv_e, W, sp, grads)
    grads = {n: jnp.stack(v) for n, v in grads.items()}
    return jnp.sum(sq), dx, grads


def kernel(x, p, even_w_in, even_b_f, even_conv_w, even_w_out, odd_w_in, odd_conv_w, odd_conv_b, odd_dt_bias, odd_a_log, odd_d_skip, odd_norm_g, odd_w_out, ln_mix_g, ln_mix_b, ffn_w_up, ffn_conv_w, ffn_conv_b, ffn_w_down, ln_ffn_g, ln_ffn_b, ple_w_proj, ple_w_gate, ple_b_gate, loss_target, m_even_w_in, m_even_b_f, m_even_conv_w, m_even_w_out, m_odd_w_in, m_odd_conv_w, m_odd_conv_b, m_odd_dt_bias, m_odd_a_log, m_odd_d_skip, m_odd_norm_g, m_odd_w_out, m_ln_mix_g, m_ln_mix_b, m_ffn_w_up, m_ffn_conv_w, m_ffn_conv_b, m_ffn_w_down, m_ln_ffn_g, m_ln_ffn_b, m_ple_w_proj, m_ple_w_gate, m_ple_b_gate, v_even_w_in, v_even_b_f, v_even_conv_w, v_even_w_out, v_odd_w_in, v_odd_conv_w, v_odd_conv_b, v_odd_dt_bias, v_odd_a_log, v_odd_d_skip, v_odd_norm_g, v_odd_w_out, v_ln_mix_g, v_ln_mix_b, v_ffn_w_up, v_ffn_conv_w, v_ffn_conv_b, v_ffn_w_down, v_ln_ffn_g, v_ln_ffn_b, v_ple_w_proj, v_ple_w_gate, v_ple_b_gate):
    args = locals()
    w = {n: args[n] for n in WEIGHTS}
    m = {n: args["m_" + n] for n in WEIGHTS}
    v = {n: args["v_" + n] for n in WEIGHTS}
    me = 4 * lax.axis_index("x") + 2 * lax.axis_index("y") + lax.axis_index("c")

    gathered = _all_gather([_as2d(w[n]).astype(BF16) for n in BIG] + [_pack_small([w[n] for n in SMALL_SHARDED])], "ag_weights")
    full = dict(w)
    for n, g in zip(BIG, gathered[:-1]):
        full[n] = _assemble(g.reshape((N_DEV,) + _shard_shape(n)), n)
    for n, g in zip(SMALL_SHARDED, _unpack_small(gathered[-1], [_shard_shape(n) for n in SMALL_SHARDED])):
        full[n] = _assemble(g, n)

    sq, dx, grads = _local_step(x[0], p[:, 0], loss_target[0], full)
    loss = lax.psum(0.5 * sq / D_MODEL, ("x", "y", "c"))

    gsum_big = _reduce_scatter([_as2d(_split_dest(grads[n], n), 1).astype(BF16) for n in BIG], "rs_grads")
    g_final = {n: g.reshape(_shard_shape(n)) for n, g in zip(BIG, gsum_big)}
    small_names = SMALL_SHARDED + REPLICATED
    (small_all,) = _all_gather([_pack_small([grads[n] for n in small_names])], "ag_small_grads")
    small_sum = _sum_slots(small_all, "sum_small_grads")
    for n, g in zip(small_names, _unpack_small(small_sum, [_full_shapes()[n][0] for n in small_names])):
        g_final[n] = lax.dynamic_index_in_dim(_split_dest(g, n), me, axis=0, keepdims=False) if n in SMALL_SHARDED else g

    out = {}
    for n in BIG:
        res = _adamw(*[_as2d(t[n]) for t in (w, g_final, m, v)], "adamw_" + n)
        out[n] = [r.reshape(_shard_shape(n)) for r in res]
    shapes = [_shard_shape(n) for n in small_names]
    res = _adamw(*[_pack_small([t[n] for n in small_names]) for t in (w, g_final, m, v)], "adamw_small")
    for n, d_, m_, v_ in zip(small_names, *[_unpack_small(r, shapes) for r in res]):
        out[n] = [d_, m_, v_]
    return (loss, dx[None], *[g_final[n] for n in WEIGHTS], *[out[n][0] for n in WEIGHTS],
            *[out[n][1] for n in WEIGHTS], *[out[n][2] for n in WEIGHTS])
```

```python
import jax
import jax.numpy as jnp
import numpy as np
from jax import lax
from jax.experimental import pallas as pl
from jax.experimental.pallas import tpu as pltpu

D_MODEL = 1024
SEQ = 8192
DEPTH = 2
CONV_DIM = 512
CONV_WIDTH = 3
FOX_HEADS = 8
FOX_HEAD_DIM = 64
SSM_HEAD_DIM = 64
SSM_GROUPS = 4
SSM_STATE = 128
SSM_CONV_WIDTH = 4
SSM_CHUNK = 128
D_FF = 2816
FFN_CONV_WIDTH = 3
PLE_DIM = 256
LN_EPS = 1e-5
RMS_EPS = 1e-5
ADAM_LR = 0.001
ADAM_B1 = 0.9
ADAM_B2 = 0.999
ADAM_EPS = 1e-08
ADAM_WD = 0.01
ADAM_STEP = 10
N_DEV = 8

F32 = jnp.float32
BF16 = jnp.bfloat16
NEG = -1e30
LANES = 128
SUBLANES = 8
PACK_W = 1024
VMEM_LIMIT = 48 * 1024 * 1024


def _dims():
    fox_dim = FOX_HEADS * FOX_HEAD_DIM
    ssm_inner = 2 * D_MODEL
    ssm_heads = ssm_inner // SSM_HEAD_DIM
    conv_ch = ssm_inner + 2 * SSM_GROUPS * SSM_STATE
    return dict(fox_dim=fox_dim, even_in=3 * CONV_DIM + 3 * fox_dim + FOX_HEADS, even_mix=CONV_DIM + fox_dim,
                ssm_inner=ssm_inner, ssm_heads=ssm_heads, conv_ch=conv_ch, odd_in=ssm_inner + conv_ch + ssm_heads)


def _alpha():
    return (2.0 * DEPTH) ** 0.25


def _pick(dim, prefs):
    for p in prefs:
        if dim % p == 0:
            return p
    return dim


def _pcall(body, **kw):
    return pl.pallas_call(body, **kw)


def _cparams(sem=None, **kw):
    if sem is not None:
        kw["dimension_semantics"] = sem
    return pltpu.CompilerParams(vmem_limit_bytes=VMEM_LIMIT, **kw)


def _sigmoid(x):
    return 1.0 / (1.0 + jnp.exp(-x))


def _softplus(x):
    return jnp.maximum(x, 0.0) + jnp.log(1.0 + jnp.exp(-jnp.abs(x)))


def _sum8(x):
    n, c = x.shape
    return x.reshape(n // SUBLANES, SUBLANES, c).sum(axis=0)


def _dot(a, b, dims):
    return lax.dot_general(a, b, (dims, ((), ())), preferred_element_type=F32)


NN = ((1,), (0,))
NT = ((1,), (1,))
TN = ((0,), (0,))


def _split3(x):
    hi = x.astype(BF16)
    r1 = x - hi.astype(F32)
    mid = r1.astype(BF16)
    lo = (r1 - mid.astype(F32)).astype(BF16)
    return hi, mid, lo


def _tri_mm(tri_bf16, x, tri_first=True):
    if tri_first:
        return sum(_dot(tri_bf16, part, NN) for part in _split3(x))
    return sum(_dot(part, tri_bf16, NN) for part in _split3(x))


def _tri(n, upper=False):
    r = lax.broadcasted_iota(jnp.int32, (n, n), 0)
    c = lax.broadcasted_iota(jnp.int32, (n, n), 1)
    return jnp.where((r <= c) if upper else (r >= c), 1.0, 0.0).astype(BF16)


def _shift_down(cur, prev8, k):
    if k == 0:
        return cur
    ext = jnp.concatenate([prev8, cur], axis=0)
    return pltpu.roll(ext, k, axis=0)[SUBLANES:]


def _shift_up(cur, next8, k):
    if k == 0:
        return cur
    n = cur.shape[0]
    ext = jnp.concatenate([cur, next8], axis=0)
    return pltpu.roll(ext, n + SUBLANES - k, axis=0)[:n]


def _mm(a, b, mode, out_dtype, name, add=None, add_scale=1.0):
    if mode == "nn":
        (M, K), (K2, N) = a.shape, b.shape
    elif mode == "nt":
        (M, K), (N, K2) = a.shape, b.shape
    else:
        (K, M), (K2, N) = a.shape, b.shape
    assert K == K2, (a.shape, b.shape, mode)
    tm = _pick(M, (1024, 512, 256, 128))
    tn = _pick(N, (1408, 1024, 768, 512, 384, 256, 128))
    tk = K if K <= 2048 else _pick(K, (1408, 1024, 768, 512, 256, 128))
    nk = K // tk
    dims = {"nn": NN, "nt": NT, "tn": TN}[mode]
    a_spec = pl.BlockSpec((tk, tm), lambda i, j, k: (k, i)) if mode == "tn" else pl.BlockSpec((tm, tk), lambda i, j, k: (i, k))
    b_spec = pl.BlockSpec((tn, tk), lambda i, j, k: (j, k)) if mode == "nt" else pl.BlockSpec((tk, tn), lambda i, j, k: (k, j))
    o_spec = pl.BlockSpec((tm, tn), lambda i, j, k: (i, j))
    has_add = add is not None

    def body(*refs):
        a_ref, b_ref = refs[:2]
        add_ref = refs[2] if has_add else None
        o_ref = refs[2 + has_add]
        prod = _dot(a_ref[...].astype(BF16), b_ref[...].astype(BF16), dims)
        if nk == 1:
            if has_add:
                prod = prod + add_scale * add_ref[...].astype(F32)
            o_ref[...] = prod.astype(out_dtype)
            return
        acc = refs[3 + has_add]
        k = pl.program_id(2)

        @pl.when(k == 0)
        def _():
            if has_add:
                acc[...] = prod + add_scale * add_ref[...].astype(F32)
            else:
                acc[...] = prod

        @pl.when(k > 0)
        def _():
            acc[...] += prod

        @pl.when(k == nk - 1)
        def _():
            o_ref[...] = acc[...].astype(out_dtype)

    ins = [a, b] + ([add] if has_add else [])
    specs = [a_spec, b_spec] + ([o_spec] if has_add else [])
    return _pcall(body, name=name, grid=(M // tm, N // tn, nk), in_specs=specs, out_specs=o_spec,
                  out_shape=jax.ShapeDtypeStruct((M, N), out_dtype),
                  scratch_shapes=[pltpu.VMEM((tm, tn), F32)] if nk > 1 else [],
                  compiler_params=_cparams(("parallel", "parallel", "arbitrary")))(*ins)


def _row_tile(L):
    return _pick(L, (256, 128))


def _ln_fwd(h, mix, g, b, name):
    L, D = h.shape
    tl = _row_tile(L)
    alpha = _alpha()

    def body(h_ref, m_ref, g_ref, b_ref, r_ref, y_ref, yb_ref):
        r = alpha * h_ref[...] + m_ref[...]
        mu = jnp.mean(r, axis=-1, keepdims=True)
        xc = r - mu
        var = jnp.mean(xc * xc, axis=-1, keepdims=True)
        r_ref[...] = r
        y = xc * lax.rsqrt(var + LN_EPS) * g_ref[...] + b_ref[...]
        y_ref[...] = y
        yb_ref[...] = y.astype(BF16)

    row = pl.BlockSpec((tl, D), lambda i: (i, 0))
    vec = pl.BlockSpec((1, D), lambda i: (0, 0))
    return _pcall(body, name=name, grid=(L // tl,), in_specs=[row, row, vec, vec], out_specs=[row, row, row],
                  out_shape=[jax.ShapeDtypeStruct((L, D), F32)] * 2 + [jax.ShapeDtypeStruct((L, D), BF16)],
                  compiler_params=_cparams(("parallel",)))(h, mix, g.reshape(1, D), b.reshape(1, D))


def _ln_bwd(r, dy, g, name):
    L, D = r.shape
    tl = _row_tile(L)

    def body(r_ref, dy_ref, g_ref, dr_ref, drb_ref, dg_ref, db_ref):
        i = pl.program_id(0)
        r_ = r_ref[...]
        dy_ = dy_ref[...]
        mu = jnp.mean(r_, axis=-1, keepdims=True)
        xc = r_ - mu
        rstd = lax.rsqrt(jnp.mean(xc * xc, axis=-1, keepdims=True) + LN_EPS)
        xhat = xc * rstd
        dxh = dy_ * g_ref[...]
        dr = rstd * (dxh - jnp.mean(dxh, axis=-1, keepdims=True) - xhat * jnp.mean(dxh * xhat, axis=-1, keepdims=True))
        dr_ref[...] = dr
        drb_ref[...] = dr.astype(BF16)

        @pl.when(i == 0)
        def _():
            dg_ref[...] = jnp.zeros_like(dg_ref)
            db_ref[...] = jnp.zeros_like(db_ref)

        dg_ref[...] += _sum8(dy_ * xhat)
        db_ref[...] += _sum8(dy_)

    row = pl.BlockSpec((tl, D), lambda i: (i, 0))
    vec = pl.BlockSpec((1, D), lambda i: (0, 0))
    acc = pl.BlockSpec((SUBLANES, D), lambda i: (0, 0))
    return _pcall(body, name=name, grid=(L // tl,), in_specs=[row, row, vec], out_specs=[row, row, acc, acc],
                  out_shape=[jax.ShapeDtypeStruct((L, D), F32), jax.ShapeDtypeStruct((L, D), BF16),
                             jax.ShapeDtypeStruct((SUBLANES, D), F32), jax.ShapeDtypeStruct((SUBLANES, D), F32)],
                  compiler_params=_cparams(("arbitrary",)))(r, dy, g.reshape(1, D))


def _ple_fwd(h2, G, bg, E, name):
    L, D = h2.shape
    tl = _row_tile(L)

    def body(h_ref, g_ref, b_ref, e_ref, o_ref, ob_ref):
        o = h_ref[...] + _sigmoid(g_ref[...] + b_ref[...]) * e_ref[...]
        o_ref[...] = o
        ob_ref[...] = o.astype(BF16)

    row = pl.BlockSpec((tl, D), lambda i: (i, 0))
    vec = pl.BlockSpec((1, D), lambda i: (0, 0))
    return _pcall(body, name=name, grid=(L // tl,), in_specs=[row, row, vec, row], out_specs=[row, row],
                  out_shape=[jax.ShapeDtypeStruct((L, D), F32), jax.ShapeDtypeStruct((L, D), BF16)],
                  compiler_params=_cparams(("parallel",)))(h2, G, bg.reshape(1, D), E)


def _ple_bwd(dh3, G, bg, E, name):
    L, D = dh3.shape
    tl = _row_tile(L)

    def body(d_ref, g_ref, b_ref, e_ref, de_ref, dg_ref, db_ref):
        i = pl.program_id(0)
        d = d_ref[...]
        sg = _sigmoid(g_ref[...] + b_ref[...])
        de_ref[...] = (d * sg).astype(BF16)
        dgp = d * e_ref[...] * sg * (1.0 - sg)
        dg_ref[...] = dgp.astype(BF16)

        @pl.when(i == 0)
        def _():
            db_ref[...] = jnp.zeros_like(db_ref)

        db_ref[...] += _sum8(dgp)

    row = pl.BlockSpec((tl, D), lambda i: (i, 0))
    vec = pl.BlockSpec((1, D), lambda i: (0, 0))
    acc = pl.BlockSpec((SUBLANES, D), lambda i: (0, 0))
    return _pcall(body, name=name, grid=(L // tl,), in_specs=[row, row, vec, row], out_specs=[row, row, acc],
                  out_shape=[jax.ShapeDtypeStruct((L, D), BF16), jax.ShapeDtypeStruct((L, D), BF16),
                             jax.ShapeDtypeStruct((SUBLANES, D), F32)],
                  compiler_params=_cparams(("arbitrary",)))(dh3, G, bg.reshape(1, D), E)


def _loss_head(h, target, name):
    L, D = h.shape
    tl = _row_tile(L)

    def body(h_ref, t_ref, d_ref, s_ref):
        i = pl.program_id(0)
        e = h_ref[...] - t_ref[...]
        d_ref[...] = e * (1.0 / D)

        @pl.when(i == 0)
        def _():
            s_ref[...] = jnp.zeros_like(s_ref)

        s_ref[...] += _sum8(e * e)

    row = pl.BlockSpec((tl, D), lambda i: (i, 0))
    acc = pl.BlockSpec((SUBLANES, D), lambda i: (0, 0))
    return _pcall(body, name=name, grid=(L // tl,), in_specs=[row, row], out_specs=[row, acc],
                  out_shape=[jax.ShapeDtypeStruct((L, D), F32), jax.ShapeDtypeStruct((SUBLANES, D), F32)],
                  compiler_params=_cparams(("arbitrary",)))(h, target)


def _halo_prev(tl, ncol_blocks_fn):
    return lambda j, i: (jnp.maximum(i * (tl // SUBLANES) - 1, 0), ncol_blocks_fn(j))


def _conv_taps(cur, prev, w_ref, K):
    acc = w_ref[K - 1:K, :] * cur
    for k in range(K - 1):
        acc = acc + w_ref[k:k + 1, :] * _shift_down(cur, prev, K - 1 - k)
    return acc


def _ffn_act_fwd(U, w, b, name):
    L, F2 = U.shape
    F = F2 // 2
    K = w.shape[0]
    tl = _row_tile(L)
    tc = _pick(F, (256, 128))

    def body(u_ref, up_ref, w_ref, b_ref, s_ref):
        i = pl.program_id(1)
        cur = u_ref[...]
        prev = jnp.where(i == 0, 0.0, up_ref[...])
        uc = _conv_taps(cur, prev, w_ref, K) + b_ref[...]
        g, v = uc[:, :tc], uc[:, tc:]
        s_ref[...] = (g * _sigmoid(g) * v).astype(BF16)

    return _pcall(body, name=name, grid=(F // tc, L // tl),
                  in_specs=[pl.BlockSpec((tl, 2 * tc), lambda j, i: (i, j)),
                            pl.BlockSpec((SUBLANES, 2 * tc), _halo_prev(tl, lambda j: j)),
                            pl.BlockSpec((K, 2 * tc), lambda j, i: (0, j)),
                            pl.BlockSpec((1, 2 * tc), lambda j, i: (0, j))],
                  out_specs=pl.BlockSpec((tl, tc), lambda j, i: (i, j)),
                  out_shape=jax.ShapeDtypeStruct((L, F), BF16),
                  compiler_params=_cparams(("parallel", "parallel")))(U, U, w, b.reshape(1, F2))


def _halo_next(tl, L, rows):
    return lambda j, i: (jnp.minimum((i + 1) * (tl // rows), L // rows - 1), j)


def _conv_taps_t(cur, nxt, w_ref, K):
    acc = w_ref[K - 1:K, :] * cur
    for k in range(K - 1):
        acc = acc + w_ref[k:k + 1, :] * _shift_up(cur, nxt, K - 1 - k)
    return acc


BF16_ROWS = 16


def _ffn_act_bwd(U, dS, w, b, name):
    L, F2 = U.shape
    F = F2 // 2
    K = w.shape[0]
    tl = _row_tile(L)
    tc = _pick(F, (256, 128))
    nl = L // tl

    def body(u_ref, up_ref, un_ref, ds_ref, dsn_ref, w_ref, b_ref, du_ref, dw_ref, db_ref):
        i = pl.program_id(1)
        cur = u_ref[...]
        prev = jnp.where(i == 0, 0.0, up_ref[...])

        def at_conv_out(x, xprev, ds):
            uc = _conv_taps(x, xprev, w_ref, K) + b_ref[...]
            g, v = uc[:, :tc], uc[:, tc:]
            sg = _sigmoid(g)
            return jnp.concatenate([ds * v * sg * (1.0 + g * (1.0 - sg)), ds * g * sg], axis=1)

        duc = at_conv_out(cur, prev, ds_ref[...].astype(F32))
        duc_n = at_conv_out(un_ref[...], cur[tl - SUBLANES:], dsn_ref[...].astype(F32)[:SUBLANES])
        duc_n = jnp.where(i == nl - 1, 0.0, duc_n)
        du_ref[...] = _conv_taps_t(duc, duc_n, w_ref, K).astype(BF16)

        @pl.when(i == 0)
        def _():
            dw_ref[...] = jnp.zeros_like(dw_ref)
            db_ref[...] = jnp.zeros_like(db_ref)

        db_ref[...] += _sum8(duc)
        for k in range(K):
            dw_ref[k * SUBLANES:(k + 1) * SUBLANES, :] += _sum8(duc * _shift_down(cur, prev, K - 1 - k))

    return _pcall(body, name=name, grid=(F // tc, nl),
                  in_specs=[pl.BlockSpec((tl, 2 * tc), lambda j, i: (i, j)),
                            pl.BlockSpec((SUBLANES, 2 * tc), _halo_prev(tl, lambda j: j)),
                            pl.BlockSpec((SUBLANES, 2 * tc), _halo_next(tl, L, SUBLANES)),
                            pl.BlockSpec((tl, tc), lambda j, i: (i, j)),
                            pl.BlockSpec((BF16_ROWS, tc), _halo_next(tl, L, BF16_ROWS)),
                            pl.BlockSpec((K, 2 * tc), lambda j, i: (0, j)),
                            pl.BlockSpec((1, 2 * tc), lambda j, i: (0, j))],
                  out_specs=[pl.BlockSpec((tl, 2 * tc), lambda j, i: (i, j)),
                             pl.BlockSpec((K * SUBLANES, 2 * tc), lambda j, i: (0, j)),
                             pl.BlockSpec((SUBLANES, 2 * tc), lambda j, i: (0, j))],
                  out_shape=[jax.ShapeDtypeStruct((L, F2), BF16), jax.ShapeDtypeStruct((K * SUBLANES, F2), F32),
                             jax.ShapeDtypeStruct((SUBLANES, F2), F32)],
                  compiler_params=_cparams(("parallel", "arbitrary")))(U, U, U, dS, dS, w, b.reshape(1, F2))


def _sconv_fwd(Ac, w, name):
    L, C3 = Ac.shape
    C = C3 // 3
    K = w.shape[0]
    tl = _row_tile(L)
    tc = LANES

    def body(a_ref, ap_ref, w_ref, y_ref):
        i = pl.program_id(1)
        a = a_ref[...]
        ap = ap_ref[...]
        p = a[:, tc:2 * tc] * a[:, 2 * tc:]
        pp = jnp.where(i == 0, 0.0, ap[:, tc:2 * tc] * ap[:, 2 * tc:])
        y_ref[...] = (a[:, :tc] * _conv_taps(p, pp, w_ref, K)).astype(BF16)

    return _pcall(body, name=name, grid=(C // tc, L // tl),
                  in_specs=[pl.BlockSpec((tl, 3 * tc), lambda j, i: (i, j)),
                            pl.BlockSpec((SUBLANES, 3 * tc), _halo_prev(tl, lambda j: j)),
                            pl.BlockSpec((K, tc), lambda j, i: (0, j))],
                  out_specs=pl.BlockSpec((tl, tc), lambda j, i: (i, j)),
                  out_shape=jax.ShapeDtypeStruct((L, C), BF16),
                  compiler_params=_cparams(("parallel", "parallel")))(Ac, Ac, w)


def _sconv_bwd_dc(Ac, dy, w, name):
    L, C3 = Ac.shape
    C = C3 // 3
    K = w.shape[0]
    tl = _row_tile(L)
    tc = LANES

    def body(a_ref, ap_ref, dy_ref, dc_ref, dw_ref):
        i = pl.program_id(1)
        a = a_ref[...]
        ap = ap_ref[...]
        p = a[:, tc:2 * tc] * a[:, 2 * tc:]
        pp = jnp.where(i == 0, 0.0, ap[:, tc:2 * tc] * ap[:, 2 * tc:])
        dc = dy_ref[...] * a[:, :tc]
        dc_ref[...] = dc

        @pl.when(i == 0)
        def _():
            dw_ref[...] = jnp.zeros_like(dw_ref)

        for k in range(K):
            dw_ref[k * SUBLANES:(k + 1) * SUBLANES, :] += _sum8(dc * _shift_down(p, pp, K - 1 - k))

    return _pcall(body, name=name, grid=(C // tc, L // tl),
                  in_specs=[pl.BlockSpec((tl, 3 * tc), lambda j, i: (i, j)),
                            pl.BlockSpec((SUBLANES, 3 * tc), _halo_prev(tl, lambda j: j)),
                            pl.BlockSpec((tl, tc), lambda j, i: (i, j))],
                  out_specs=[pl.BlockSpec((tl, tc), lambda j, i: (i, j)),
                             pl.BlockSpec((K * SUBLANES, tc), lambda j, i: (0, j))],
                  out_shape=[jax.ShapeDtypeStruct((L, C), F32), jax.ShapeDtypeStruct((K * SUBLANES, C), F32)],
                  compiler_params=_cparams(("parallel", "arbitrary")))(Ac, Ac, dy)


def _sconv_bwd_da(Ac, dy, dc, w, name):
    L, C3 = Ac.shape
    C = C3 // 3
    K = w.shape[0]
    tl = _row_tile(L)
    tc = LANES
    nl = L // tl

    def body(a_ref, ap_ref, dy_ref, dc_ref, dcn_ref, w_ref, o_ref):
        i = pl.program_id(1)
        a = a_ref[...]
        ap = ap_ref[...]
        gc, h = a[:, tc:2 * tc], a[:, 2 * tc:]
        p = gc * h
        pp = jnp.where(i == 0, 0.0, ap[:, tc:2 * tc] * ap[:, 2 * tc:])
        dgb = dy_ref[...] * _conv_taps(p, pp, w_ref, K)
        cur = dc_ref[...]
        nxt = jnp.where(i == nl - 1, 0.0, dcn_ref[...])
        dp = w_ref[K - 1:K, :] * cur
        for k in range(K - 1):
            dp = dp + w_ref[k:k + 1, :] * _shift_up(cur, nxt, K - 1 - k)
        o_ref[...] = jnp.concatenate([dgb, dp * h, dp * gc], axis=1).astype(BF16)

    return _pcall(body, name=name, grid=(C // tc, nl),
                  in_specs=[pl.BlockSpec((tl, 3 * tc), lambda j, i: (i, j)),
                            pl.BlockSpec((SUBLANES, 3 * tc), _halo_prev(tl, lambda j: j)),
                            pl.BlockSpec((tl, tc), lambda j, i: (i, j)),
                            pl.BlockSpec((tl, tc), lambda j, i: (i, j)),
                            pl.BlockSpec((SUBLANES, tc), lambda j, i: (jnp.minimum((i + 1) * (tl // SUBLANES), L // SUBLANES - 1), j)),
                            pl.BlockSpec((K, tc), lambda j, i: (0, j))],
                  out_specs=pl.BlockSpec((tl, 3 * tc), lambda j, i: (i, j)),
                  out_shape=jax.ShapeDtypeStruct((L, C3), BF16),
                  compiler_params=_cparams(("parallel", "parallel")))(Ac, Ac, dy, dc, dc, w)


def _fox_gate_fwd(Af, bf, name):
    L, W = Af.shape
    tl = _pick(L, (512, 256, 128))

    def body(a_ref, b_ref, f_ref, carry):
        i = pl.program_id(0)

        @pl.when(i == 0)
        def _():
            carry[...] = jnp.zeros_like(carry)

        z = a_ref[...] + b_ref[...]
        logf = jnp.minimum(z, 0.0) - jnp.log(1.0 + jnp.exp(-jnp.abs(z)))
        f = _tri_mm(_tri(tl), logf) + carry[...]
        f_ref[...] = f
        carry[...] = f[tl - 1:tl, :]

    row = pl.BlockSpec((tl, W), lambda i: (i, 0))
    return _pcall(body, name=name, grid=(L // tl,), in_specs=[row, pl.BlockSpec((1, W), lambda i: (0, 0))], out_specs=row,
                  out_shape=jax.ShapeDtypeStruct((L, W), F32), scratch_shapes=[pltpu.VMEM((1, W), F32)],
                  compiler_params=_cparams(("arbitrary",)))(Af, bf)


def _fox_gate_bwd(Af, bf, dF, name):
    L, W = Af.shape
    tl = _pick(L, (512, 256, 128))
    nl = L // tl

    def body(a_ref, b_ref, df_ref, o_ref, db_ref, carry):
        i = pl.program_id(0)

        @pl.when(i == 0)
        def _():
            carry[...] = jnp.zeros_like(carry)
            db_ref[...] = jnp.zeros_like(db_ref)

        z = a_ref[...] + b_ref[...]
        dlogf = _tri_mm(_tri(tl, upper=True), df_ref[...]) + carry[...]
        carry[...] = dlogf[0:1, :]
        dz = dlogf * _sigmoid(-z)
        o_ref[...] = dz
        db_ref[...] += _sum8(dz)

    row = pl.BlockSpec((tl, W), lambda i: (nl - 1 - i, 0))
    return _pcall(body, name=name, grid=(nl,),
                  in_specs=[row, pl.BlockSpec((1, W), lambda i: (0, 0)), row],
                  out_specs=[row, pl.BlockSpec((SUBLANES, W), lambda i: (0, 0))],
                  out_shape=[jax.ShapeDtypeStruct((L, W), F32), jax.ShapeDtypeStruct((SUBLANES, W), F32)],
                  scratch_shapes=[pltpu.VMEM((1, W), F32)],
                  compiler_params=_cparams(("arbitrary",)))(Af, bf, dF)


def _attn_tiles(L):
    t = _pick(L, (512, 256, 128))
    return t, t


def _attn_scores(q, k, fq, fk, diag, scale):
    s = _dot(q, k, NT) * scale + (fq - fk)
    if not diag:
        return s
    row = lax.broadcasted_iota(jnp.int32, s.shape, 0)
    col = lax.broadcasted_iota(jnp.int32, s.shape, 1)
    return jnp.where(col <= row, s, NEG)


def _attn_tile_spec(t, w):
    return pl.BlockSpec((None, t, w), lambda h, i: (h, i, 0))


def _attn_full_spec(L, w):
    return pl.BlockSpec((None, L, w), lambda h, i: (h, 0, 0))


def _attn_fwd(q, k, v, Fq, Fk, name):
    H, L, Dh = q.shape
    tq, tk = _attn_tiles(L)
    scale = Dh ** -0.5

    t = tq

    def body(q_ref, k_ref, v_ref, fq_ref, fk_ref, o_ref, lse_ref):
        qi = pl.program_id(1)
        qv, fq = q_ref[...], fq_ref[...]

        def chunk(j, carry, diag):
            m, l, acc = carry
            rows = pl.ds(pl.multiple_of(j * t, t), t)
            s = _attn_scores(qv, k_ref[rows, :], fq, fk_ref[:, rows], diag, scale)
            m_new = jnp.maximum(m, jnp.max(s, axis=-1, keepdims=True))
            p = jnp.exp(s - m_new)
            a = jnp.exp(m - m_new)
            return m_new, a * l + jnp.sum(p, axis=-1, keepdims=True), a * acc + _dot(p.astype(BF16), v_ref[rows, :], NN)

        init = (jnp.full((t, 1), NEG, F32), jnp.zeros((t, 1), F32), jnp.zeros((t, Dh), F32))
        m, l, acc = chunk(qi, lax.fori_loop(0, qi, lambda j, c: chunk(j, c, False), init), True)
        o_ref[...] = acc / l
        lse_ref[...] = m + jnp.log(l)

    return _pcall(body, name=name, grid=(H, L // t),
                  in_specs=[_attn_tile_spec(t, Dh), _attn_full_spec(L, Dh), _attn_full_spec(L, Dh),
                            _attn_tile_spec(t, 1), pl.BlockSpec((None, 1, L), lambda h, i: (h, 0, 0))],
                  out_specs=[_attn_tile_spec(t, Dh), _attn_tile_spec(t, 1)],
                  out_shape=[jax.ShapeDtypeStruct((H, L, Dh), F32), jax.ShapeDtypeStruct((H, L, 1), F32)],
                  compiler_params=_cparams(("parallel", "arbitrary")))(q, k, v, Fq, Fk)


def _attn_bwd_delta(q, k, v, Fq, Fk, lse, do, name):
    H, L, Dh = q.shape
    tq, tk = _attn_tiles(L)
    scale = Dh ** -0.5

    t = tq

    def body(q_ref, k_ref, v_ref, fq_ref, fk_ref, lse_ref, do_ref, d_ref):
        qi = pl.program_id(1)
        qv, fq, lse, dov = q_ref[...], fq_ref[...], lse_ref[...], do_ref[...]

        def chunk(j, acc, diag):
            rows = pl.ds(pl.multiple_of(j * t, t), t)
            s = _attn_scores(qv, k_ref[rows, :], fq, fk_ref[:, rows], diag, scale)
            return acc + jnp.sum(jnp.exp(s - lse) * _dot(dov, v_ref[rows, :], NT), axis=-1, keepdims=True)

        d_ref[...] = chunk(qi, lax.fori_loop(0, qi, lambda j, c: chunk(j, c, False), jnp.zeros((t, 1), F32)), True)

    return _pcall(body, name=name, grid=(H, L // t),
                  in_specs=[_attn_tile_spec(t, Dh), _attn_full_spec(L, Dh), _attn_full_spec(L, Dh), _attn_tile_spec(t, 1),
                            pl.BlockSpec((None, 1, L), lambda h, i: (h, 0, 0)), _attn_tile_spec(t, 1), _attn_tile_spec(t, Dh)],
                  out_specs=_attn_tile_spec(t, 1), out_shape=jax.ShapeDtypeStruct((H, L, 1), F32),
                  compiler_params=_cparams(("parallel", "arbitrary")))(q, k, v, Fq, Fk, lse, do)


def _attn_bwd_dq(q, k, v, Fq, Fk, delta, lse, do, name):
    H, L, Dh = q.shape
    tq, tk = _attn_tiles(L)
    scale = Dh ** -0.5

    t = tq

    def body(q_ref, k_ref, v_ref, fq_ref, fk_ref, dl_ref, lse_ref, do_ref, dq_ref):
        qi = pl.program_id(1)
        qv, fq, lse, dl, dov = q_ref[...], fq_ref[...], lse_ref[...], dl_ref[...], do_ref[...]

        def chunk(j, acc, diag):
            rows = pl.ds(pl.multiple_of(j * t, t), t)
            kc = k_ref[rows, :]
            s = _attn_scores(qv, kc, fq, fk_ref[:, rows], diag, scale)
            ds = jnp.exp(s - lse) * (_dot(dov, v_ref[rows, :], NT) - dl)
            return acc + _dot(ds.astype(BF16), kc, NN)

        acc = chunk(qi, lax.fori_loop(0, qi, lambda j, c: chunk(j, c, False), jnp.zeros((t, Dh), F32)), True)
        dq_ref[...] = acc * scale

    return _pcall(body, name=name, grid=(H, L // t),
                  in_specs=[_attn_tile_spec(t, Dh), _attn_full_spec(L, Dh), _attn_full_spec(L, Dh), _attn_tile_spec(t, 1),
                            pl.BlockSpec((None, 1, L), lambda h, i: (h, 0, 0)), _attn_tile_spec(t, 1), _attn_tile_spec(t, 1),
                            _attn_tile_spec(t, Dh)],
                  out_specs=_attn_tile_spec(t, Dh), out_shape=jax.ShapeDtypeStruct((H, L, Dh), F32),
                  compiler_params=_cparams(("parallel", "arbitrary")))(q, k, v, Fq, Fk, delta, lse, do)


def _attn_bwd_dkv(q, k, v, rowvals, Fk, do, name):
    H, L, Dh = q.shape
    t, _ = _attn_tiles(L)
    nq = L // t
    scale = Dh ** -0.5

    def body(q_ref, k_ref, v_ref, rv_ref, fk_ref, do_ref, dk_ref, dv_ref, df_ref):
        ki = pl.program_id(1)
        kv, vv, fk = k_ref[...], v_ref[...], fk_ref[...]

        def chunk(i, carry, diag):
            dk, dv, df = carry
            rows = pl.ds(pl.multiple_of(i * t, t), t)
            qc, dov, rv = q_ref[rows, :], do_ref[rows, :], rv_ref[rows, :]
            s = _attn_scores(qc, kv, rv[:, 0:1], fk, diag, scale)
            p = jnp.exp(s - rv[:, 1:2])
            ds = p * (_dot(dov, vv, NT) - rv[:, 2:3])
            return (dk + _dot(ds.astype(BF16), qc, TN), dv + _dot(p.astype(BF16), dov, TN),
                    df - jnp.sum(ds, axis=0, keepdims=True))

        init = chunk(ki, (jnp.zeros((t, Dh), F32), jnp.zeros((t, Dh), F32), jnp.zeros((1, t), F32)), True)
        dk, dv, df = lax.fori_loop(ki + 1, nq, lambda i, c: chunk(i, c, False), init)
        dk_ref[...] = dk * scale
        dv_ref[...] = dv
        df_ref[...] = df

    rk = pl.BlockSpec((None, 1, t), lambda h, j: (h, 0, j))
    return _pcall(body, name=name, grid=(H, nq),
                  in_specs=[_attn_full_spec(L, Dh), _attn_tile_spec(t, Dh), _attn_tile_spec(t, Dh), _attn_full_spec(L, LANES), rk,
                            _attn_full_spec(L, Dh)],
                  out_specs=[_attn_tile_spec(t, Dh), _attn_tile_spec(t, Dh), rk],
                  out_shape=[jax.ShapeDtypeStruct((H, L, Dh), F32), jax.ShapeDtypeStruct((H, L, Dh), F32),
                             jax.ShapeDtypeStruct((H, 1, L), F32)],
                  compiler_params=_cparams(("parallel", "arbitrary")))(q, k, v, rowvals, Fk, do)


def _mconv_fwd(xr, w, b, name):
    L, C = xr.shape
    K = w.shape[0]
    tl = _row_tile(L)
    tc = _pick(C, (512, 384, 256, 128))

    def body(x_ref, xp_ref, w_ref, b_ref, o_ref):
        i = pl.program_id(1)
        prev = jnp.where(i == 0, 0.0, xp_ref[...])
        pre = _conv_taps(x_ref[...], prev, w_ref, K) + b_ref[...]
        o_ref[...] = pre * _sigmoid(pre)

    return _pcall(body, name=name, grid=(C // tc, L // tl),
                  in_specs=[pl.BlockSpec((tl, tc), lambda j, i: (i, j)),
                            pl.BlockSpec((SUBLANES, tc), _halo_prev(tl, lambda j: j)),
                            pl.BlockSpec((K, tc), lambda j, i: (0, j)),
                            pl.BlockSpec((1, tc), lambda j, i: (0, j))],
                  out_specs=pl.BlockSpec((tl, tc), lambda j, i: (i, j)),
                  out_shape=jax.ShapeDtypeStruct((L, C), F32),
                  compiler_params=_cparams(("parallel", "parallel")))(xr, xr, w, b.reshape(1, C))


def _mconv_bwd(xr, dact, w, b, name):
    L, C = xr.shape
    K = w.shape[0]
    tl = _row_tile(L)
    tc = _pick(C, (512, 384, 256, 128))
    nl = L // tl

    def body(x_ref, xp_ref, xn_ref, d_ref, dn_ref, w_ref, b_ref, o_ref, dw_ref, db_ref):
        i = pl.program_id(1)
        cur = x_ref[...]
        prev = jnp.where(i == 0, 0.0, xp_ref[...])

        def at_conv_out(x, xprev, d):
            pre = _conv_taps(x, xprev, w_ref, K) + b_ref[...]
            sg = _sigmoid(pre)
            return d * sg * (1.0 + pre * (1.0 - sg))

        dpre = at_conv_out(cur, prev, d_ref[...])
        dpre_n = jnp.where(i == nl - 1, 0.0, at_conv_out(xn_ref[...], cur[tl - SUBLANES:], dn_ref[...]))
        o_ref[...] = _conv_taps_t(dpre, dpre_n, w_ref, K).astype(BF16)

        @pl.when(i == 0)
        def _():
            dw_ref[...] = jnp.zeros_like(dw_ref)
            db_ref[...] = jnp.zeros_like(db_ref)

        db_ref[...] += _sum8(dpre)
        for k in range(K):
            dw_ref[k * SUBLANES:(k + 1) * SUBLANES, :] += _sum8(dpre * _shift_down(cur, prev, K - 1 - k))

    return _pcall(body, name=name, grid=(C // tc, nl),
                  in_specs=[pl.BlockSpec((tl, tc), lambda j, i: (i, j)),
                            pl.BlockSpec((SUBLANES, tc), _halo_prev(tl, lambda j: j)),
                            pl.BlockSpec((SUBLANES, tc), _halo_next(tl, L, SUBLANES)),
                            pl.BlockSpec((tl, tc), lambda j, i: (i, j)),
                            pl.BlockSpec((SUBLANES, tc), _halo_next(tl, L, SUBLANES)),
                            pl.BlockSpec((K, tc), lambda j, i: (0, j)),
                            pl.BlockSpec((1, tc), lambda j, i: (0, j))],
                  out_specs=[pl.BlockSpec((tl, tc), lambda j, i: (i, j)),
                             pl.BlockSpec((K * SUBLANES, tc), lambda j, i: (0, j)),
                             pl.BlockSpec((SUBLANES, tc), lambda j, i: (0, j))],
                  out_shape=[jax.ShapeDtypeStruct((L, C), BF16), jax.ShapeDtypeStruct((K * SUBLANES, C), F32),
                             jax.ShapeDtypeStruct((SUBLANES, C), F32)],
                  compiler_params=_cparams(("parallel", "arbitrary")))(xr, xr, xr, dact, dact, w, b.reshape(1, C))


def _ssd_common(x_r, dt_col, a_col, a_row, mask, CB):
    seg = jnp.where(mask, a_col - a_row, NEG)
    Lm = jnp.exp(seg)
    return Lm, CB * Lm, x_r * dt_col


def _ssd_fwd(x, Bm, Cm, dtc, dtr, bias_c, bias_r, alog_c, alog_r, dsk_c, name):
    Hs, L, P = x.shape
    G, _, N = Bm.shape
    R = Hs // G
    Q = SSM_CHUNK
    nc = L // Q

    def body(x_ref, b_ref, c_ref, dtc_ref, dtr_ref, bc_ref, br_ref, ac_ref, ar_ref, dk_ref, y_ref, hp_ref, st):
        c = pl.program_id(1)

        @pl.when(c == 0)
        def _():
            st[...] = jnp.zeros_like(st)

        dt_c = _softplus(dtc_ref[...] + bc_ref[...])
        dt_r = _softplus(dtr_ref[...] + br_ref[...])
        acs_c = _tri_mm(_tri(Q), dt_c * (-jnp.exp(ac_ref[...])))
        acs_r = _tri_mm(_tri(Q, upper=True), dt_r * (-jnp.exp(ar_ref[...])), tri_first=False)
        Bf = b_ref[...]
        Cb = c_ref[...].astype(BF16)
        CB = _dot(Cb, Bf.astype(BF16), NT)
        mask = lax.broadcasted_iota(jnp.int32, (Q, Q), 0) >= lax.broadcasted_iota(jnp.int32, (Q, Q), 1)
        for r in range(R):
            a_col = acs_c[:, r:r + 1]
            a_row = acs_r[r:r + 1, :]
            xr_ = x_ref[r]
            _, Gm, xt = _ssd_common(xr_, dt_c[:, r:r + 1], a_col, a_row, mask, CB)
            hp = st[r]
            hp_ref[r] = hp
            a_end = acs_c[Q - 1:Q, r:r + 1]
            dte = jnp.exp(a_end - a_col)
            ydiag = _dot(Gm.astype(BF16), xt.astype(BF16), NN)
            yoff = jnp.exp(a_col) * _dot(Cb, hp.astype(BF16), NT)
            st[r] = hp * jnp.exp(a_end) + _dot(xt.astype(BF16), (Bf * dte).astype(BF16), TN)
            y_ref[r] = ydiag + yoff + dk_ref[:, r:r + 1] * xr_

    pc = pl.BlockSpec((None, 1, R), lambda g, c: (g, 0, 0))
    pr = pl.BlockSpec((None, R, 1), lambda g, c: (g, 0, 0))
    xspec = pl.BlockSpec((R, Q, P), lambda g, c: (g, c, 0))
    bspec = pl.BlockSpec((None, Q, N), lambda g, c: (g, c, 0))
    return _pcall(body, name=name, grid=(G, nc),
                  in_specs=[xspec, bspec, bspec, pl.BlockSpec((None, Q, R), lambda g, c: (g, c, 0)),
                            pl.BlockSpec((None, R, Q), lambda g, c: (g, 0, c)), pc, pr, pc, pr, pc],
                  out_specs=[xspec, pl.BlockSpec((None, R, P, N), lambda g, c: (c, g, 0, 0))],
                  out_shape=[jax.ShapeDtypeStruct((Hs, L, P), F32), jax.ShapeDtypeStruct((nc, Hs, P, N), F32)],
                  scratch_shapes=[pltpu.VMEM((R, P, N), F32)],
                  compiler_params=_cparams(("parallel", "arbitrary")))(x, Bm, Cm, dtc, dtr, bias_c, bias_r, alog_c, alog_r, dsk_c)


def _ssd_bwd(x, Bm, Cm, dtc, dtr, bias_c, bias_r, alog_c, alog_r, dsk_c, hprev, dy, name):
    Hs, L, P = x.shape
    G, _, N = Bm.shape
    R = Hs // G
    Q = SSM_CHUNK
    nc = L // Q

    def body(x_ref, b_ref, c_ref, dtc_ref, dtr_ref, bc_ref, br_ref, ac_ref, ar_ref, dk_ref, hp_ref, dy_ref,
             dx_ref, db_ref, dc_ref, ddt_ref, gbias_ref, galog_ref, gdsk_ref, dst):
        c = pl.program_id(1)

        @pl.when(c == 0)
        def _():
            dst[...] = jnp.zeros_like(dst)
            gbias_ref[...] = jnp.zeros_like(gbias_ref)
            galog_ref[...] = jnp.zeros_like(galog_ref)
            gdsk_ref[...] = jnp.zeros_like(gdsk_ref)

        raw_c = dtc_ref[...] + bc_ref[...]
        dt_c = _softplus(raw_c)
        dt_r = _softplus(dtr_ref[...] + br_ref[...])
        A_c = -jnp.exp(ac_ref[...])
        acs_c = _tri_mm(_tri(Q), dt_c * A_c)
        acs_r = _tri_mm(_tri(Q, upper=True), dt_r * (-jnp.exp(ar_ref[...])), tri_first=False)
        Bf = b_ref[...]
        Cf = c_ref[...]
        Bb = Bf.astype(BF16)
        Cb = Cf.astype(BF16)
        CB = _dot(Cb, Bb, NT)
        mask = lax.broadcasted_iota(jnp.int32, (Q, Q), 0) >= lax.broadcasted_iota(jnp.int32, (Q, Q), 1)
        lane = lax.broadcasted_iota(jnp.int32, (Q, R), 1)
        lane1 = lax.broadcasted_iota(jnp.int32, (1, R), 1)
        rowi = lax.broadcasted_iota(jnp.int32, (Q, 1), 0)
        ones = jnp.ones((Q, LANES), F32)
        dCB = jnp.zeros((Q, Q), F32)
        dC = jnp.zeros((Q, N), F32)
        dB = jnp.zeros((Q, N), F32)
        da_mat = jnp.zeros((Q, R), F32)
        ddtx_mat = jnp.zeros((Q, R), F32)
        gd = jnp.zeros((1, R), F32)
        for r in range(R):
            a_col = acs_c[:, r:r + 1]
            a_row = acs_r[r:r + 1, :]
            xr_ = x_ref[r]
            dt_col = dt_c[:, r:r + 1]
            Lm, Gm, xt = _ssd_common(xr_, dt_col, a_col, a_row, mask, CB)
            dyr = dy_ref[r]
            dyb = dyr.astype(BF16)
            xtb = xt.astype(BF16)
            hp = hp_ref[r]
            hpb = hp.astype(BF16)
            dHn = dst[r]
            dHb = dHn.astype(BF16)
            a_end = acs_c[Q - 1:Q, r:r + 1]
            e_end = jnp.exp(a_end)
            ea = jnp.exp(a_col)
            dte = jnp.exp(a_end - a_col)
            dxt = _dot(Gm.astype(BF16), dyb, TN)
            dG = jnp.where(mask, _dot(dyb, xtb, NT), 0.0)
            Mm = dG * Gm
            dCB = dCB + dG * Lm
            colsum = lax.dot_general(Mm, ones, (TN, ((), ())), precision=lax.Precision.HIGHEST,
                                     preferred_element_type=F32)[:, :1]
            da = jnp.sum(Mm, axis=1, keepdims=True) - colsum
            edy = ea * dyr
            edyb = edy.astype(BF16)
            W = _dot(Cb, hpb, NT)
            dC = dC + _dot(edyb, hpb, NN)
            dhp = _dot(edyb, Cb, TN)
            da = da + jnp.sum(edy * W, axis=1, keepdims=True)
            dxt = dxt + _dot((Bf * dte).astype(BF16), dHb, NT)
            XH = _dot(xtb, dHb, NN)
            dB = dB + dte * XH
            t = jnp.sum(XH * Bf, axis=1, keepdims=True) * dte
            da = da - t
            da_end = jnp.sum(t, axis=0, keepdims=True) + jnp.sum(jnp.sum(dHn * hp, axis=1, keepdims=True), axis=0, keepdims=True) * e_end
            da = da + jnp.where(rowi == Q - 1, da_end, 0.0)
            dst[r] = dhp + dHn * e_end
            dsk = dk_ref[:, r:r + 1]
            dx_ref[r] = dxt * dt_col + dsk * dyr
            da_mat = jnp.where(lane == r, da, da_mat)
            ddtx_mat = jnp.where(lane == r, jnp.sum(dxt * xr_, axis=1, keepdims=True), ddtx_mat)
            gd = jnp.where(lane1 == r, jnp.sum(jnp.sum(dyr * xr_, axis=1, keepdims=True), axis=0, keepdims=True), gd)
        dCBb = dCB.astype(BF16)
        dc_ref[...] = dC + _dot(dCBb, Bb, NN)
        db_ref[...] = dB + _dot(dCBb, Cb, TN)
        ddtA = _tri_mm(_tri(Q, upper=True), da_mat)
        ddt_raw = (ddtA * A_c + ddtx_mat) * _sigmoid(raw_c)
        ddt_ref[...] = ddt_raw
        gbias_ref[...] += jnp.sum(ddt_raw, axis=0, keepdims=True)
        galog_ref[...] += jnp.sum(ddtA * dt_c, axis=0, keepdims=True) * A_c
        gdsk_ref[...] += gd

    rc = lambda c: nc - 1 - c
    pc = pl.BlockSpec((None, 1, R), lambda g, c: (g, 0, 0))
    pr = pl.BlockSpec((None, R, 1), lambda g, c: (g, 0, 0))
    xspec = pl.BlockSpec((R, Q, P), lambda g, c: (g, rc(c), 0))
    bspec = pl.BlockSpec((None, Q, N), lambda g, c: (g, rc(c), 0))
    dtspec = pl.BlockSpec((None, Q, R), lambda g, c: (g, rc(c), 0))
    return _pcall(body, name=name, grid=(G, nc),
                  in_specs=[xspec, bspec, bspec, dtspec, pl.BlockSpec((None, R, Q), lambda g, c: (g, 0, rc(c))),
                            pc, pr, pc, pr, pc, pl.BlockSpec((None, R, P, N), lambda g, c: (rc(c), g, 0, 0)), xspec],
                  out_specs=[xspec, bspec, bspec, dtspec, pc, pc, pc],
                  out_shape=[jax.ShapeDtypeStruct((Hs, L, P), F32), jax.ShapeDtypeStruct((G, L, N), F32),
                             jax.ShapeDtypeStruct((G, L, N), F32), jax.ShapeDtypeStruct((G, L, R), F32),
                             jax.ShapeDtypeStruct((G, 1, R), F32), jax.ShapeDtypeStruct((G, 1, R), F32),
                             jax.ShapeDtypeStruct((G, 1, R), F32)],
                  scratch_shapes=[pltpu.VMEM((R, P, N), F32)],
                  compiler_params=_cparams(("parallel", "arbitrary")))(
        x, Bm, Cm, dtc, dtr, bias_c, bias_r, alog_c, alog_r, dsk_c, hprev, dy)


def _gnorm_fwd(y, z, g, name):
    L, Dn = y.shape
    gs = Dn // SSM_GROUPS
    tl = _row_tile(L)

    def body(y_ref, z_ref, g_ref, o_ref):
        for k in range(SSM_GROUPS):
            sl = slice(k * gs, (k + 1) * gs)
            zz = z_ref[:, sl]
            u = y_ref[:, sl] * zz * _sigmoid(zz)
            rstd = lax.rsqrt(jnp.mean(u * u, axis=-1, keepdims=True) + RMS_EPS)
            o_ref[:, sl] = (u * rstd * g_ref[:, sl]).astype(BF16)

    row = pl.BlockSpec((tl, Dn), lambda i: (i, 0))
    return _pcall(body, name=name, grid=(L // tl,), in_specs=[row, row, pl.BlockSpec((1, Dn), lambda i: (0, 0))],
                  out_specs=row, out_shape=jax.ShapeDtypeStruct((L, Dn), BF16),
                  compiler_params=_cparams(("parallel",)))(y, z, g.reshape(1, Dn))


def _gnorm_bwd(y, z, g, dout, name):
    L, Dn = y.shape
    gs = Dn // SSM_GROUPS
    tl = _row_tile(L)

    def body(y_ref, z_ref, g_ref, d_ref, dy_ref, dz_ref, dg_ref):
        i = pl.program_id(0)

        @pl.when(i == 0)
        def _():
            dg_ref[...] = jnp.zeros_like(dg_ref)

        for k in range(SSM_GROUPS):
            sl = slice(k * gs, (k + 1) * gs)
            zz = z_ref[:, sl]
            yy = y_ref[:, sl]
            sg = _sigmoid(zz)
            sil = zz * sg
            u = yy * sil
            rstd = lax.rsqrt(jnp.mean(u * u, axis=-1, keepdims=True) + RMS_EPS)
            n = u * rstd
            d = d_ref[:, sl]
            dn = d * g_ref[:, sl]
            du = rstd * (dn - n * jnp.mean(dn * n, axis=-1, keepdims=True))
            dy_ref[:, sl] = du * sil
            dz_ref[:, sl] = (du * yy * sg * (1.0 + zz * (1.0 - sg))).astype(BF16)
            dg_ref[:, sl] += _sum8(d * n)

    row = pl.BlockSpec((tl, Dn), lambda i: (i, 0))
    return _pcall(body, name=name, grid=(L // tl,), in_specs=[row, row, pl.BlockSpec((1, Dn), lambda i: (0, 0)), row],
                  out_specs=[row, row, pl.BlockSpec((SUBLANES, Dn), lambda i: (0, 0))],
                  out_shape=[jax.ShapeDtypeStruct((L, Dn), F32), jax.ShapeDtypeStruct((L, Dn), BF16),
                             jax.ShapeDtypeStruct((SUBLANES, Dn), F32)],
                  compiler_params=_cparams(("arbitrary",)))(y, z, g.reshape(1, Dn), dout)


def _adamw(w, g, m, v, name):
    rows, W = w.shape
    tr = _pick(rows, (512, 256, 128, 64, 32, 16, 8))
    c1 = 1.0 / (1.0 - ADAM_B1 ** ADAM_STEP)
    c2 = 1.0 / (1.0 - ADAM_B2 ** ADAM_STEP)

    def body(w_ref, g_ref, m_ref, v_ref, d_ref, nm_ref, nv_ref):
        g_ = g_ref[...]
        nm = ADAM_B1 * m_ref[...] + (1.0 - ADAM_B1) * g_
        nv = ADAM_B2 * v_ref[...] + (1.0 - ADAM_B2) * (g_ * g_)
        nm_ref[...] = nm
        nv_ref[...] = nv
        d_ref[...] = -ADAM_LR * ((nm * c1) / (jnp.sqrt(nv * c2) + ADAM_EPS) + ADAM_WD * w_ref[...])

    blk = pl.BlockSpec((tr, W), lambda i: (i, 0))
    return _pcall(body, name=name, grid=(rows // tr,), in_specs=[blk] * 4, out_specs=[blk] * 3,
                  out_shape=[jax.ShapeDtypeStruct((rows, W), F32)] * 3, compiler_params=_cparams(("parallel",)))(w, g, m, v)


def _sum_slots(x, name, extra=None):
    n, rows, W = x.shape
    tr = _pick(rows, (512, 256, 128, 64, 32, 16, 8))
    has_extra = extra is not None

    def body(*refs):
        if has_extra:
            e_ref, x_ref, o_ref = refs
            acc = e_ref[...].astype(F32)
            start = 0
        else:
            x_ref, o_ref = refs
            acc = x_ref[0].astype(F32)
            start = 1
        for s in range(start, n):
            acc = acc + x_ref[s].astype(F32)
        o_ref[...] = acc

    blk = pl.BlockSpec((tr, W), lambda i: (i, 0))
    xblk = pl.BlockSpec((n, tr, W), lambda i: (0, i, 0))
    return _pcall(body, name=name, grid=(rows // tr,), in_specs=([blk] if has_extra else []) + [xblk], out_specs=blk,
                  out_shape=jax.ShapeDtypeStruct((rows, W), F32), compiler_params=_cparams(("parallel",)))(
        *(([extra] if has_extra else []) + [x]))


def _add_pairs(a, b, name):
    n, rows, W = a.shape
    tr = _pick(rows, (512, 256, 128, 64, 32, 16, 8))

    def body(a_ref, b_ref, o_ref):
        o_ref[...] = (a_ref[...].astype(F32) + b_ref[...].astype(F32)).astype(BF16)

    blk = pl.BlockSpec((None, tr, W), lambda s, i: (s, i, 0))
    return _pcall(body, name=name, grid=(n, rows // tr), in_specs=[blk, blk], out_specs=blk,
                  out_shape=jax.ShapeDtypeStruct((n, rows, W), BF16), compiler_params=_cparams(("parallel", "parallel")))(a, b)


MESH = pl.DeviceIdType.MESH
HBM_SPEC = pl.BlockSpec(memory_space=pl.ANY)


def _me():
    return lax.axis_index("x"), lax.axis_index("y"), lax.axis_index("c")


def _all_gather(arrs, name):
    n = len(arrs)

    def body(*refs):
        ins, outs = refs[:n], refs[n:2 * n]
        send_sems, recv_sems, local_sems = refs[2 * n:]
        x, y, c = _me()
        me, sib = (x, y, c), (x, y, 1 - c)
        chips = [(1 - x, y), (x, 1 - y), (1 - x, 1 - y)]

        def slot(a, dev):
            return outs[a].at[4 * dev[0] + 2 * dev[1] + dev[2]]

        def copy(a, k, block, to, src=None):
            return pltpu.make_async_remote_copy(src_ref=slot(a, block) if src is None else src, dst_ref=slot(a, block),
                                                send_sem=send_sems.at[a * 7 + k], recv_sem=recv_sems.at[a * 7 + k],
                                                device_id=to, device_id_type=MESH)

        mine = [pltpu.make_async_copy(ins[a], slot(a, me), local_sems.at[a]) for a in range(n)]
        for cp in mine:
            cp.start()
        first = []
        for a in range(n):
            first.append(copy(a, 0, me, sib, src=ins[a]))
            first += [copy(a, 1 + j, me, (*chip, c), src=ins[a]) for j, chip in enumerate(chips)]
        for cp in first:
            cp.start()
        passed = []
        for j, chip in enumerate(chips):
            for a in range(n):
                copy(a, 1 + j, (*chip, c), me).wait_recv()
                fw = copy(a, 4 + j, (*chip, c), sib)
                fw.start()
                passed.append(fw)
        for a in range(n):
            copy(a, 0, sib, me).wait_recv()
            for j, chip in enumerate(chips):
                copy(a, 4 + j, (*chip, 1 - c), me).wait_recv()
        for cp in first + passed:
            cp.wait_send()
        for cp in mine:
            cp.wait()

    return _pcall(body, name=name, in_specs=[HBM_SPEC] * n, out_specs=[HBM_SPEC] * n,
                  out_shape=[jax.ShapeDtypeStruct((N_DEV,) + a.shape, a.dtype) for a in arrs],
                  scratch_shapes=[pltpu.SemaphoreType.DMA((7 * n,)), pltpu.SemaphoreType.DMA((7 * n,)),
                                  pltpu.SemaphoreType.DMA((n,))])(*arrs)


def _rs_sibling(gs, name):
    n = len(gs)

    def body(*refs):
        g_refs, o_refs = refs[:n], refs[n:2 * n]
        send_sems, recv_sems = refs[2 * n:]
        x, y, c = _me()
        sib = (x, y, 1 - c)
        cps = [pltpu.make_async_remote_copy(src_ref=g_refs[a].at[2 * q + (1 - c)], dst_ref=o_refs[a].at[q],
                                            send_sem=send_sems.at[4 * a + q], recv_sem=recv_sems.at[4 * a + q],
                                            device_id=sib, device_id_type=MESH) for a in range(n) for q in range(4)]
        for cp in cps:
            cp.start()
        for cp in cps:
            cp.wait()

    return _pcall(body, name=name, in_specs=[HBM_SPEC] * n, out_specs=[HBM_SPEC] * n,
                  out_shape=[jax.ShapeDtypeStruct((4,) + g.shape[1:], g.dtype) for g in gs],
                  scratch_shapes=[pltpu.SemaphoreType.DMA((4 * n,)), pltpu.SemaphoreType.DMA((4 * n,))])(*gs)


def _rs_chips(ps, name):
    n = len(ps)

    def body(*refs):
        p_refs, o_refs = refs[:n], refs[n:2 * n]
        send_sems, recv_sems = refs[2 * n:]
        x, y, c = _me()
        chips = [(1 - x, y), (x, 1 - y), (1 - x, 1 - y)]
        cps = [pltpu.make_async_remote_copy(src_ref=p_refs[a].at[2 * chip[0] + chip[1]], dst_ref=o_refs[a].at[j],
                                            send_sem=send_sems.at[3 * a + j], recv_sem=recv_sems.at[3 * a + j],
                                            device_id=(*chip, c), device_id_type=MESH)
               for j, chip in enumerate(chips) for a in range(n)]
        for cp in cps:
            cp.start()
        for cp in cps:
            cp.wait()

    return _pcall(body, name=name, in_specs=[HBM_SPEC] * n, out_specs=[HBM_SPEC] * n,
                  out_shape=[jax.ShapeDtypeStruct((3,) + p.shape[1:], p.dtype) for p in ps],
                  scratch_shapes=[pltpu.SemaphoreType.DMA((3 * n,)), pltpu.SemaphoreType.DMA((3 * n,))])(*ps)


def _reduce_scatter(gs, name):
    x, y, c = _me()
    from_sib = _rs_sibling(gs, name + "_sib")
    pairs = []
    for a, (g, fs) in enumerate(zip(gs, from_sib)):
        own = g.reshape((4, 2) + g.shape[1:])
        pairs.append(_add_pairs(jnp.where(c == 0, own[:, 0], own[:, 1]), fs, f"{name}_pair{a}"))
    from_chips = _rs_chips(pairs, name + "_chips")
    return [_sum_slots(fc, f"{name}_sum{a}", extra=lax.dynamic_index_in_dim(p, 2 * x + y, axis=0, keepdims=False))
            for a, (p, fc) in enumerate(zip(pairs, from_chips))]


BIG = ("even_w_in", "even_w_out", "odd_w_in", "odd_w_out", "ffn_w_up", "ffn_w_down", "ple_w_proj", "ple_w_gate")
SMALL_SHARDED = ("even_conv_w", "odd_conv_w", "odd_conv_b", "odd_norm_g", "ffn_conv_w")
REPLICATED = ("even_b_f", "odd_dt_bias", "odd_a_log", "odd_d_skip", "ln_mix_g", "ln_mix_b", "ffn_conv_b",
              "ln_ffn_g", "ln_ffn_b", "ple_b_gate")
WEIGHTS = ("even_w_in", "even_b_f", "even_conv_w", "even_w_out", "odd_w_in", "odd_conv_w", "odd_conv_b", "odd_dt_bias",
           "odd_a_log", "odd_d_skip", "odd_norm_g", "odd_w_out", "ln_mix_g", "ln_mix_b", "ffn_w_up", "ffn_conv_w",
           "ffn_conv_b", "ffn_w_down", "ln_ffn_g", "ln_ffn_b", "ple_w_proj", "ple_w_gate", "ple_b_gate")


def _full_shapes():
    d = _dims()
    return {
        "even_w_in": ((1, D_MODEL, d["even_in"]), 2), "even_b_f": ((1, FOX_HEADS), None),
        "even_conv_w": ((1, CONV_WIDTH, CONV_DIM), 2), "even_w_out": ((1, d["even_mix"], D_MODEL), 1),
        "odd_w_in": ((1, D_MODEL, d["odd_in"]), 2), "odd_conv_w": ((1, SSM_CONV_WIDTH, d["conv_ch"]), 2),
        "odd_conv_b": ((1, d["conv_ch"]), 1), "odd_dt_bias": ((1, d["ssm_heads"]), None),
        "odd_a_log": ((1, d["ssm_heads"]), None), "odd_d_skip": ((1, d["ssm_heads"]), None),
        "odd_norm_g": ((1, d["ssm_inner"]), 1), "odd_w_out": ((1, d["ssm_inner"], D_MODEL), 1),
        "ln_mix_g": ((DEPTH, D_MODEL), None), "ln_mix_b": ((DEPTH, D_MODEL), None),
        "ffn_w_up": ((DEPTH, D_MODEL, 2 * D_FF), 2), "ffn_conv_w": ((DEPTH, FFN_CONV_WIDTH, 2 * D_FF), 2),
        "ffn_conv_b": ((DEPTH, 2 * D_FF), None), "ffn_w_down": ((DEPTH, D_FF, D_MODEL), 1),
        "ln_ffn_g": ((DEPTH, D_MODEL), None), "ln_ffn_b": ((DEPTH, D_MODEL), None),
        "ple_w_proj": ((DEPTH, PLE_DIM, D_MODEL), 2), "ple_w_gate": ((DEPTH, D_MODEL, D_MODEL), 1),
        "ple_b_gate": ((DEPTH, D_MODEL), None),
    }


def _shard_shape(name):
    shape, ax = _full_shapes()[name]
    if ax is None:
        return shape
    return tuple(s // N_DEV if i == ax else s for i, s in enumerate(shape))


def _as2d(a, lead=0):
    return a.reshape(a.shape[:lead] + (-1, a.shape[-1]))


def _part_rows(shape):
    n = int(np.prod(shape))
    return -(-(-(-n // PACK_W)) // SUBLANES) * SUBLANES


def _pack_small(parts):
    out = []
    for p in parts:
        n, rows = int(np.prod(p.shape)), _part_rows(p.shape)
        out.append(jnp.pad(p.reshape(-1).astype(F32), (0, rows * PACK_W - n)).reshape(rows, PACK_W))
    return jnp.concatenate(out, axis=0)


def _unpack_small(pack, shapes):
    lead = pack.shape[:-2]
    out, off = [], 0
    for s in shapes:
        n, rows = int(np.prod(s)), _part_rows(s)
        part = pack[..., off:off + rows, :].reshape(lead + (-1,))[..., :n]
        out.append(part.reshape(lead + tuple(s)))
        off += rows
    return out


def _assemble(gathered, name):
    shape, ax = _full_shapes()[name]
    return jnp.moveaxis(gathered, 0, ax).reshape(shape)


def _split_dest(full, name):
    shape, ax = _full_shapes()[name]
    sh = shape[:ax] + (N_DEV, shape[ax] // N_DEV) + shape[ax + 1:]
    return jnp.moveaxis(full.reshape(sh), ax, 0)


def _interleave_cols(w, parts, tc):
    C = w.shape[-1] // parts
    sh = w.shape[:-1]
    return w.reshape(sh + (parts, C // tc, tc)).swapaxes(-3, -2).reshape(sh + (parts * C,))


def _deinterleave_cols(w, parts, tc):
    C = w.shape[-1] // parts
    sh = w.shape[:-1]
    return w.reshape(sh + (C // tc, parts, tc)).swapaxes(-3, -2).reshape(sh + (parts * C,))


def _ffn_tc():
    return _pick(D_FF, (256, 128))


def _heads_first(a, heads):
    L = a.shape[0]
    return a.reshape(L, heads, -1).transpose(1, 0, 2)


def _heads_last(a):
    h, L, d = a.shape
    return a.transpose(1, 0, 2).reshape(L, h * d)


def _pad_cols(a, to):
    return jnp.pad(a, ((0, 0), (0, to - a.shape[1])))


def _tail_fwd(i, h_in, mix, p_i, W, sp):
    r1, h1, h1b = _ln_fwd(h_in, mix, sp["ln_mix_g"][i], sp["ln_mix_b"][i], f"ln_mix_fwd{i}")
    U = _mm(h1b, W["ffn_up"][i], "nn", F32, f"ffn_up{i}")
    S = _ffn_act_fwd(U, sp["ffn_conv_w_il"][i], sp["ffn_conv_b_il"][i], f"ffn_act_fwd{i}")
    ffn = _mm(S, W["ffn_down"][i], "nn", F32, f"ffn_down{i}")
    r2, h2, h2b = _ln_fwd(h1, ffn, sp["ln_ffn_g"][i], sp["ln_ffn_b"][i], f"ln_ffn_fwd{i}")
    G = _mm(h2b, W["ple_gate"][i], "nn", F32, f"ple_gate{i}")
    E = _mm(p_i, W["ple_proj"][i], "nn", F32, f"ple_proj{i}")
    h3, h3b = _ple_fwd(h2, G, sp["ple_b_gate"][i], E, f"ple_fwd{i}")
    return h3, h3b, dict(r1=r1, h1b=h1b, U=U, S=S, r2=r2, h2b=h2b, G=G, E=E, p=p_i)


def _tail_bwd(i, dh3, sv, W, sp, grads):
    alpha = _alpha()
    dE, dGp, dbg = _ple_bwd(dh3, sv["G"], sp["ple_b_gate"][i], sv["E"], f"ple_bwd{i}")
    grads["ple_b_gate"][i] = dbg.sum(0)
    grads["ple_w_proj"][i] = _mm(sv["p"], dE, "tn", F32, f"d_ple_proj{i}")
    grads["ple_w_gate"][i] = _mm(sv["h2b"], dGp, "tn", F32, f"d_ple_gate{i}")
    dh2 = _mm(dGp, W["ple_gate"][i], "nt", F32, f"dx_ple_gate{i}", add=dh3)
    dr2, dr2b, dg, db = _ln_bwd(sv["r2"], dh2, sp["ln_ffn_g"][i], f"ln_ffn_bwd{i}")
    grads["ln_ffn_g"][i], grads["ln_ffn_b"][i] = dg.sum(0), db.sum(0)
    grads["ffn_w_down"][i] = _mm(sv["S"], dr2b, "tn", F32, f"d_ffn_down{i}")
    dS = _mm(dr2b, W["ffn_down"][i], "nt", BF16, f"dx_ffn_down{i}")
    dU, dcw, dcb = _ffn_act_bwd(sv["U"], dS, sp["ffn_conv_w_il"][i], sp["ffn_conv_b_il"][i], f"ffn_act_bwd{i}")
    K = FFN_CONV_WIDTH
    tc = _ffn_tc()
    grads["ffn_conv_w"][i] = _deinterleave_cols(dcw.reshape(K, SUBLANES, -1).sum(1), 2, tc)
    grads["ffn_conv_b"][i] = _deinterleave_cols(dcb.sum(0), 2, tc)
    grads["ffn_w_up"][i] = _deinterleave_cols(_mm(sv["h1b"], dU, "tn", F32, f"d_ffn_up{i}"), 2, tc)
    dh1 = _mm(dU, W["ffn_up"][i], "nt", F32, f"dx_ffn_up{i}", add=dr2, add_scale=alpha)
    dr1, dr1b, dg, db = _ln_bwd(sv["r1"], dh1, sp["ln_mix_g"][i], f"ln_mix_bwd{i}")
    grads["ln_mix_g"][i], grads["ln_mix_b"][i] = dg.sum(0), db.sum(0)
    return dr1, dr1b


def _even_fwd(h, W, sp):
    L = h.shape[0]
    H, Dh = FOX_HEADS, FOX_HEAD_DIM
    Ac = _mm(h, W["even_in_conv"], "nn", F32, "even_in_conv")
    qkv = _mm(h, W["even_in_qkv"], "nn", BF16, "even_in_qkv")
    Af = _mm(h, W["even_in_f"], "nn", F32, "even_in_f")
    y_a = _sconv_fwd(Ac, sp["even_conv_w_il"], "sconv_fwd")
    Fc = _fox_gate_fwd(Af, sp["even_b_f_pad"], "fox_gate_fwd")
    Fh = Fc[:, :H].T
    Fq, Fk = Fh.reshape(H, L, 1), Fh.reshape(H, 1, L)
    fd = H * Dh
    q, k, v = (_heads_first(qkv[:, j * fd:(j + 1) * fd], H) for j in range(3))
    o, lse = _attn_fwd(q, k, v, Fq, Fk, "attn_fwd")
    Y = jnp.concatenate([y_a, _heads_last(o).astype(BF16)], axis=1)
    mix = _mm(Y, W["even_out"], "nn", F32, "even_out")
    return mix, dict(h=h, Ac=Ac, Af=Af, q=q, k=k, v=v, Fq=Fq, Fk=Fk, o=o, lse=lse, Y=Y)


def _even_bwd(dmix, dres, sv, W, sp, grads):
    H, Dh = FOX_HEADS, FOX_HEAD_DIM
    C = CONV_DIM
    L = dmix.shape[0]
    grads["even_w_out"][0] = _mm(sv["Y"], dmix, "tn", F32, "d_even_out")
    dY = _mm(dmix, W["even_out"], "nt", F32, "dx_even_out")
    dya = dY[:, :C]
    do = _heads_first(dY[:, C:], H).astype(BF16)
    dc, dcw = _sconv_bwd_dc(sv["Ac"], dya, sp["even_conv_w_il"], "sconv_bwd_dc")
    grads["even_conv_w"][0] = dcw.reshape(CONV_WIDTH, SUBLANES, -1).sum(1)
    dAc = _sconv_bwd_da(sv["Ac"], dya, dc, sp["even_conv_w_il"], "sconv_bwd_da")
    delta = _attn_bwd_delta(sv["q"], sv["k"], sv["v"], sv["Fq"], sv["Fk"], sv["lse"], do, "attn_bwd_delta")
    dq = _attn_bwd_dq(sv["q"], sv["k"], sv["v"], sv["Fq"], sv["Fk"], delta, sv["lse"], do, "attn_bwd_dq")
    rowvals = jnp.pad(jnp.concatenate([sv["Fq"], sv["lse"], delta], axis=-1), ((0, 0), (0, 0), (0, LANES - 3)))
    dk, dv, dFk = _attn_bwd_dkv(sv["q"], sv["k"], sv["v"], rowvals, sv["Fk"], do, "attn_bwd_dkv")
    dqkv = jnp.concatenate([_heads_last(dq), _heads_last(dk), _heads_last(dv)], axis=1).astype(BF16)
    dF = _pad_cols(dFk.reshape(H, L).T, LANES)
    dAf, dbf = _fox_gate_bwd(sv["Af"], sp["even_b_f_pad"], dF, "fox_gate_bwd")
    grads["even_b_f"][0] = dbf.sum(0)[:H]
    h = sv["h"]
    gc = _deinterleave_cols(_mm(h, dAc, "tn", F32, "d_even_in_conv"), 3, LANES)
    gq = _mm(h, dqkv, "tn", F32, "d_even_in_qkv")
    gf = _mm(h, dAf, "tn", F32, "d_even_in_f")[:, :H]
    grads["even_w_in"][0] = jnp.concatenate([gc, gq, gf], axis=1)
    dh = _mm(dAc, W["even_in_conv"], "nt", F32, "dx_even_in_conv", add=dres, add_scale=_alpha())
    dh = _mm(dqkv, W["even_in_qkv"], "nt", F32, "dx_even_in_qkv", add=dh)
    dh = _mm(dAf, W["even_in_f"], "nt", F32, "dx_even_in_f", add=dh)
    return dh


def _group_layouts(v, G):
    R = v.shape[0] // G
    return v.reshape(G, 1, R), v.reshape(G, R, 1)


def _odd_fwd(h, W, sp):
    d = _dims()
    L = h.shape[0]
    Hs, G, N, P = d["ssm_heads"], SSM_GROUPS, SSM_STATE, SSM_HEAD_DIM
    R = Hs // G
    inner = d["ssm_inner"]
    z = _mm(h, W["odd_in_z"], "nn", F32, "odd_in_z")
    xr = _mm(h, W["odd_in_x"], "nn", F32, "odd_in_x")
    dtp = _mm(h, W["odd_in_dt"], "nn", F32, "odd_in_dt")
    act = _mconv_fwd(xr, sp["odd_conv_w"], sp["odd_conv_b"], "mconv_fwd")
    xs = _heads_first(act[:, :inner], Hs)
    Bm = _heads_first(act[:, inner:inner + G * N], G)
    Cm = _heads_first(act[:, inner + G * N:], G)
    dtg = dtp[:, :Hs].reshape(L, G, R)
    dtc, dtr = dtg.transpose(1, 0, 2), dtg.transpose(1, 2, 0)
    prm = _group_layouts(sp["odd_dt_bias"], G) + _group_layouts(sp["odd_a_log"], G) + (_group_layouts(sp["odd_d_skip"], G)[0],)
    ssd_in = (xs, Bm, Cm, dtc, dtr) + prm
    y3, hprev = _ssd_fwd(*ssd_in, "ssd_fwd")
    y = _heads_last(y3)
    u = _gnorm_fwd(y, z, sp["odd_norm_g"], "gnorm_fwd")
    mix = _mm(u, W["odd_out"], "nn", F32, "odd_out")
    return mix, dict(h=h, z=z, xr=xr, ssd_in=ssd_in, hprev=hprev, y=y, u=u)


def _odd_bwd(dmix, dres, sv, W, sp, grads):
    d = _dims()
    L = dmix.shape[0]
    Hs, G, N, P = d["ssm_heads"], SSM_GROUPS, SSM_STATE, SSM_HEAD_DIM
    grads["odd_w_out"][0] = _mm(sv["u"], dmix, "tn", F32, "d_odd_out")
    du = _mm(dmix, W["odd_out"], "nt", F32, "dx_odd_out")
    dy, dz, dg = _gnorm_bwd(sv["y"], sv["z"], sp["odd_norm_g"], du, "gnorm_bwd")
    grads["odd_norm_g"][0] = dg.sum(0)
    dx3, dB, dC, ddt, gbias, galog, gdsk = _ssd_bwd(*sv["ssd_in"], sv["hprev"], _heads_first(dy, Hs), "ssd_bwd")
    grads["odd_dt_bias"][0] = gbias.reshape(Hs)
    grads["odd_a_log"][0] = galog.reshape(Hs)
    grads["odd_d_skip"][0] = gdsk.reshape(Hs)
    dact = jnp.concatenate([_heads_last(dx3), _heads_last(dB), _heads_last(dC)], axis=1)
    dxr, dcw, dcb = _mconv_bwd(sv["xr"], dact, sp["odd_conv_w"], sp["odd_conv_b"], "mconv_bwd")
    grads["odd_conv_w"][0] = dcw.reshape(SSM_CONV_WIDTH, SUBLANES, -1).sum(1)
    grads["odd_conv_b"][0] = dcb.sum(0)
    ddtp = _pad_cols(ddt.transpose(1, 0, 2).reshape(L, Hs), W["odd_in_dt"].shape[1])
    h = sv["h"]
    gz = _mm(h, dz, "tn", F32, "d_odd_in_z")
    gx = _mm(h, dxr, "tn", F32, "d_odd_in_x")
    gdt = _mm(h, ddtp, "tn", F32, "d_odd_in_dt")[:, :Hs]
    grads["odd_w_in"][0] = jnp.concatenate([gz, gx, gdt], axis=1)
    dh = _mm(dz, W["odd_in_z"], "nt", F32, "dx_odd_in_z", add=dres, add_scale=_alpha())
    dh = _mm(dxr, W["odd_in_x"], "nt", F32, "dx_odd_in_x", add=dh)
    dh = _mm(ddtp, W["odd_in_dt"], "nt", F32, "dx_odd_in_dt", add=dh)
    return dh


def _prepare_weights(full):
    d = _dims()
    C, fd, H = CONV_DIM, d["fox_dim"], FOX_HEADS
    tc = _ffn_tc()
    W, sp = {}, {}
    ew = full["even_w_in"][0]
    W["even_in_conv"] = _interleave_cols(ew[:, :3 * C], 3, LANES)
    W["even_in_qkv"] = ew[:, 3 * C:3 * C + 3 * fd]
    W["even_in_f"] = _pad_cols(ew[:, 3 * C + 3 * fd:], LANES)
    W["even_out"] = full["even_w_out"][0]
    ow = full["odd_w_in"][0]
    inner, cch, Hs = d["ssm_inner"], d["conv_ch"], d["ssm_heads"]
    W["odd_in_z"] = ow[:, :inner]
    W["odd_in_x"] = ow[:, inner:inner + cch]
    W["odd_in_dt"] = _pad_cols(ow[:, inner + cch:], -(-Hs // LANES) * LANES)
    W["odd_out"] = full["odd_w_out"][0]
    W["ffn_up"] = [_interleave_cols(full["ffn_w_up"][i], 2, tc) for i in range(DEPTH)]
    W["ffn_down"] = [full["ffn_w_down"][i] for i in range(DEPTH)]
    W["ple_proj"] = [full["ple_w_proj"][i] for i in range(DEPTH)]
    W["ple_gate"] = [full["ple_w_gate"][i] for i in range(DEPTH)]
    sp["even_conv_w_il"] = full["even_conv_w"][0]
    sp["even_b_f_pad"] = _pad_cols(full["even_b_f"], LANES)
    sp["odd_conv_w"] = full["odd_conv_w"][0]
    sp["odd_conv_b"] = full["odd_conv_b"][0]
    sp["odd_norm_g"] = full["odd_norm_g"][0]
    for n in ("odd_dt_bias", "odd_a_log", "odd_d_skip"):
        sp[n] = full[n][0]
    for n in ("ln_mix_g", "ln_mix_b", "ln_ffn_g", "ln_ffn_b", "ple_b_gate"):
        sp[n] = full[n]
    sp["ffn_conv_w_il"] = [_interleave_cols(full["ffn_conv_w"][i], 2, tc) for i in range(DEPTH)]
    sp["ffn_conv_b_il"] = [_interleave_cols(full["ffn_conv_b"][i], 2, tc) for i in range(DEPTH)]
    return W, sp


def _local_step(x, p, target, full):
    W, sp = _prepare_weights(full)
    grads = {n: [None] * _full_shapes()[n][0][0] for n in WEIGHTS}
    pb = p.astype(BF16)
    mix0, sv_e = _even_fwd(x.astype(BF16), W, sp)
    h3_0, h3_0b, sv_t0 = _tail_fwd(0, x, mix0, pb[0], W, sp)
    mix1, sv_o = _odd_fwd(h3_0b, W, sp)
    h3_1, _, sv_t1 = _tail_fwd(1, h3_0, mix1, pb[1], W, sp)
    dh, sq = _loss_head(h3_1, target, "loss_head")
    dr1, dr1b = _tail_bwd(1, dh, sv_t1, W, sp, grads)
    dh = _odd_bwd(dr1b, dr1, sv_o, W, sp, grads)
    dr1, dr1b = _tail_bwd(0, dh, sv_t0, W, sp, grads)
    dx = _even_bwd(dr1b, dr1, sv_e, W, sp, grads)
    grads = {n: jnp.stack(v) for n, v in grads.items()}
    return jnp.sum(sq), dx, grads


def kernel(x, p, even_w_in, even_b_f, even_conv_w, even_w_out, odd_w_in, odd_conv_w, odd_conv_b, odd_dt_bias, odd_a_log, odd_d_skip, odd_norm_g, odd_w_out, ln_mix_g, ln_mix_b, ffn_w_up, ffn_conv_w, ffn_conv_b, ffn_w_down, ln_ffn_g, ln_ffn_b, ple_w_proj, ple_w_gate, ple_b_gate, loss_target, m_even_w_in, m_even_b_f, m_even_conv_w, m_even_w_out, m_odd_w_in, m_odd_conv_w, m_odd_conv_b, m_odd_dt_bias, m_odd_a_log, m_odd_d_skip, m_odd_norm_g, m_odd_w_out, m_ln_mix_g, m_ln_mix_b, m_ffn_w_up, m_ffn_conv_w, m_ffn_conv_b, m_ffn_w_down, m_ln_ffn_g, m_ln_ffn_b, m_ple_w_proj, m_ple_w_gate, m_ple_b_gate, v_even_w_in, v_even_b_f, v_even_conv_w, v_even_w_out, v_odd_w_in, v_odd_conv_w, v_odd_conv_b, v_odd_dt_bias, v_odd_a_log, v_odd_d_skip, v_odd_norm_g, v_odd_w_out, v_ln_mix_g, v_ln_mix_b, v_ffn_w_up, v_ffn_conv_w, v_ffn_conv_b, v_ffn_w_down, v_ln_ffn_g, v_ln_ffn_b, v_ple_w_proj, v_ple_w_gate, v_ple_b_gate):
    args = locals()
    w = {n: args[n] for n in WEIGHTS}
    m = {n: args["m_" + n] for n in WEIGHTS}
    v = {n: args["v_" + n] for n in WEIGHTS}
    me = 4 * lax.axis_index("x") + 2 * lax.axis_index("y") + lax.axis_index("c")

    gathered = _all_gather([_as2d(w[n]).astype(BF16) for n in BIG] + [_pack_small([w[n] for n in SMALL_SHARDED])], "ag_weights")
    full = dict(w)
    for n, g in zip(BIG, gathered[:-1]):
        full[n] = _assemble(g.reshape((N_DEV,) + _shard_shape(n)), n)
    for n, g in zip(SMALL_SHARDED, _unpack_small(gathered[-1], [_shard_shape(n) for n in SMALL_SHARDED])):
        full[n] = _assemble(g, n)

    sq, dx, grads = _local_step(x[0], p[:, 0], loss_target[0], full)
    loss = lax.psum(0.5 * sq / D_MODEL, ("x", "y", "c"))

    gsum_big = _reduce_scatter([_as2d(_split_dest(grads[n], n), 1).astype(BF16) for n in BIG], "rs_grads")
    g_final = {n: g.reshape(_shard_shape(n)) for n, g in zip(BIG, gsum_big)}
    small_names = SMALL_SHARDED + REPLICATED
    (small_all,) = _all_gather([_pack_small([grads[n] for n in small_names])], "ag_small_grads")
    small_sum = _sum_slots(small_all, "sum_small_grads")
    for n, g in zip(small_names, _unpack_small(small_sum, [_full_shapes()[n][0] for n in small_names])):
        g_final[n] = lax.dynamic_index_in_dim(_split_dest(g, n), me, axis=0, keepdims=False) if n in SMALL_SHARDED else g

    out = {}
    for n in BIG:
        res = _adamw(*[_as2d(t[n]) for t in (w, g_final, m, v)], "adamw_" + n)
        out[n] = [r.reshape(_shard_shape(n)) for r in res]
    shapes = [_shard_shape(n) for n in small_names]
    res = _adamw(*[_pack_small([t[n] for n in small_names]) for t in (w, g_final, m, v)], "adamw_small")
    for n, d_, m_, v_ in zip(small_names, *[_unpack_small(r, shapes) for r in res]):
        out[n] = [d_, m_, v_]
    return (loss, dx[None], *[g_final[n] for n in WEIGHTS], *[out[n][0] for n in WEIGHTS],
            *[out[n][1] for n in WEIGHTS], *[out[n][2] for n in WEIGHTS])
```

```python
import jax
import jax.numpy as jnp
import numpy as np
from jax import lax
from jax.experimental import pallas as pl
from jax.experimental.pallas import tpu as pltpu

D_MODEL = 1024
SEQ = 8192
DEPTH = 2
CONV_DIM = 512
CONV_WIDTH = 3
FOX_HEADS = 8
FOX_HEAD_DIM = 64
SSM_HEAD_DIM = 64
SSM_GROUPS = 4
SSM_STATE = 128
SSM_CONV_WIDTH = 4
SSM_CHUNK = 128
D_FF = 2816
FFN_CONV_WIDTH = 3
PLE_DIM = 256
LN_EPS = 1e-5
RMS_EPS = 1e-5
ADAM_LR = 0.001
ADAM_B1 = 0.9
ADAM_B2 = 0.999
ADAM_EPS = 1e-08
ADAM_WD = 0.01
ADAM_STEP = 10
N_DEV = 8

F32 = jnp.float32
BF16 = jnp.bfloat16
NEG = -1e30
LANES = 128
SUBLANES = 8
PACK_W = 1024
VMEM_LIMIT = 48 * 1024 * 1024


def _dims():
    fox_dim = FOX_HEADS * FOX_HEAD_DIM
    ssm_inner = 2 * D_MODEL
    ssm_heads = ssm_inner // SSM_HEAD_DIM
    conv_ch = ssm_inner + 2 * SSM_GROUPS * SSM_STATE
    return dict(fox_dim=fox_dim, even_in=3 * CONV_DIM + 3 * fox_dim + FOX_HEADS, even_mix=CONV_DIM + fox_dim,
                ssm_inner=ssm_inner, ssm_heads=ssm_heads, conv_ch=conv_ch, odd_in=ssm_inner + conv_ch + ssm_heads)


def _alpha():
    return (2.0 * DEPTH) ** 0.25


def _pick(dim, prefs):
    for p in prefs:
        if dim % p == 0:
            return p
    return dim


def _pcall(body, **kw):
    return pl.pallas_call(body, **kw)


def _cparams(sem=None, **kw):
    if sem is not None:
        kw["dimension_semantics"] = sem
    return pltpu.CompilerParams(vmem_limit_bytes=VMEM_LIMIT, **kw)


def _sigmoid(x):
    return 1.0 / (1.0 + jnp.exp(-x))


def _softplus(x):
    return jnp.maximum(x, 0.0) + jnp.log(1.0 + jnp.exp(-jnp.abs(x)))


def _sum8(x):
    n, c = x.shape
    return x.reshape(n // SUBLANES, SUBLANES, c).sum(axis=0)


def _dot(a, b, dims):
    return lax.dot_general(a, b, (dims, ((), ())), preferred_element_type=F32)


NN = ((1,), (0,))
NT = ((1,), (1,))
TN = ((0,), (0,))


def _split3(x):
    hi = x.astype(BF16)
    r1 = x - hi.astype(F32)
    mid = r1.astype(BF16)
    lo = (r1 - mid.astype(F32)).astype(BF16)
    return hi, mid, lo


def _tri_mm(tri_bf16, x, tri_first=True):
    if tri_first:
        return sum(_dot(tri_bf16, part, NN) for part in _split3(x))
    return sum(_dot(part, tri_bf16, NN) for part in _split3(x))


def _tri(n, upper=False):
    r = lax.broadcasted_iota(jnp.int32, (n, n), 0)
    c = lax.broadcasted_iota(jnp.int32, (n, n), 1)
    return jnp.where((r <= c) if upper else (r >= c), 1.0, 0.0).astype(BF16)


def _shift_down(cur, prev8, k):
    if k == 0:
        return cur
    ext = jnp.concatenate([prev8, cur], axis=0)
    return pltpu.roll(ext, k, axis=0)[SUBLANES:]


def _shift_up(cur, next8, k):
    if k == 0:
        return cur
    n = cur.shape[0]
    ext = jnp.concatenate([cur, next8], axis=0)
    return pltpu.roll(ext, n + SUBLANES - k, axis=0)[:n]


def _mm(a, b, mode, out_dtype, name, add=None, add_scale=1.0):
    if mode == "nn":
        (M, K), (K2, N) = a.shape, b.shape
    elif mode == "nt":
        (M, K), (N, K2) = a.shape, b.shape
    else:
        (K, M), (K2, N) = a.shape, b.shape
    assert K == K2, (a.shape, b.shape, mode)
    tm = _pick(M, (1024, 512, 256, 128))
    tn = _pick(N, (1408, 1024, 768, 512, 384, 256, 128))
    tk = K if K <= 2048 else _pick(K, (1408, 1024, 768, 512, 256, 128))
    nk = K // tk
    dims = {"nn": NN, "nt": NT, "tn": TN}[mode]
    a_spec = pl.BlockSpec((tk, tm), lambda i, j, k: (k, i)) if mode == "tn" else pl.BlockSpec((tm, tk), lambda i, j, k: (i, k))
    b_spec = pl.BlockSpec((tn, tk), lambda i, j, k: (j, k)) if mode == "nt" else pl.BlockSpec((tk, tn), lambda i, j, k: (k, j))
    o_spec = pl.BlockSpec((tm, tn), lambda i, j, k: (i, j))
    has_add = add is not None

    def body(*refs):
        a_ref, b_ref = refs[:2]
        add_ref = refs[2] if has_add else None
        o_ref = refs[2 + has_add]
        prod = _dot(a_ref[...].astype(BF16), b_ref[...].astype(BF16), dims)
        if nk == 1:
            if has_add:
                prod = prod + add_scale * add_ref[...].astype(F32)
            o_ref[...] = prod.astype(out_dtype)
            return
        acc = refs[3 + has_add]
        k = pl.program_id(2)

        @pl.when(k == 0)
        def _():
            if has_add:
                acc[...] = prod + add_scale * add_ref[...].astype(F32)
            else:
                acc[...] = prod

        @pl.when(k > 0)
        def _():
            acc[...] += prod

        @pl.when(k == nk - 1)
        def _():
            o_ref[...] = acc[...].astype(out_dtype)

    ins = [a, b] + ([add] if has_add else [])
    specs = [a_spec, b_spec] + ([o_spec] if has_add else [])
    return _pcall(body, name=name, grid=(M // tm, N // tn, nk), in_specs=specs, out_specs=o_spec,
                  out_shape=jax.ShapeDtypeStruct((M, N), out_dtype),
                  scratch_shapes=[pltpu.VMEM((tm, tn), F32)] if nk > 1 else [],
                  compiler_params=_cparams(("parallel", "parallel", "arbitrary")))(*ins)


def _row_tile(L):
    return _pick(L, (256, 128))


def _conv_row_tile(L, backward):
    return _pick(L, (512, 256, 128)) if backward else _pick(L, (1024, 512, 256, 128))


def _ln_fwd(h, mix, g, b, name):
    L, D = h.shape
    tl = _row_tile(L)
    alpha = _alpha()

    def body(h_ref, m_ref, g_ref, b_ref, r_ref, y_ref, yb_ref):
        r = alpha * h_ref[...] + m_ref[...]
        mu = jnp.mean(r, axis=-1, keepdims=True)
        xc = r - mu
        var = jnp.mean(xc * xc, axis=-1, keepdims=True)
        r_ref[...] = r
        y = xc * lax.rsqrt(var + LN_EPS) * g_ref[...] + b_ref[...]
        y_ref[...] = y
        yb_ref[...] = y.astype(BF16)

    row = pl.BlockSpec((tl, D), lambda i: (i, 0))
    vec = pl.BlockSpec((1, D), lambda i: (0, 0))
    return _pcall(body, name=name, grid=(L // tl,), in_specs=[row, row, vec, vec], out_specs=[row, row, row],
                  out_shape=[jax.ShapeDtypeStruct((L, D), F32)] * 2 + [jax.ShapeDtypeStruct((L, D), BF16)],
                  compiler_params=_cparams(("parallel",)))(h, mix, g.reshape(1, D), b.reshape(1, D))


def _ln_bwd(r, dy, g, name):
    L, D = r.shape
    tl = _row_tile(L)

    def body(r_ref, dy_ref, g_ref, dr_ref, drb_ref, dg_ref, db_ref):
        i = pl.program_id(0)
        r_ = r_ref[...]
        dy_ = dy_ref[...]
        mu = jnp.mean(r_, axis=-1, keepdims=True)
        xc = r_ - mu
        rstd = lax.rsqrt(jnp.mean(xc * xc, axis=-1, keepdims=True) + LN_EPS)
        xhat = xc * rstd
        dxh = dy_ * g_ref[...]
        dr = rstd * (dxh - jnp.mean(dxh, axis=-1, keepdims=True) - xhat * jnp.mean(dxh * xhat, axis=-1, keepdims=True))
        dr_ref[...] = dr
        drb_ref[...] = dr.astype(BF16)

        @pl.when(i == 0)
        def _():
            dg_ref[...] = jnp.zeros_like(dg_ref)
            db_ref[...] = jnp.zeros_like(db_ref)

        dg_ref[...] += _sum8(dy_ * xhat)
        db_ref[...] += _sum8(dy_)

    row = pl.BlockSpec((tl, D), lambda i: (i, 0))
    vec = pl.BlockSpec((1, D), lambda i: (0, 0))
    acc = pl.BlockSpec((SUBLANES, D), lambda i: (0, 0))
    return _pcall(body, name=name, grid=(L // tl,), in_specs=[row, row, vec], out_specs=[row, row, acc, acc],
                  out_shape=[jax.ShapeDtypeStruct((L, D), F32), jax.ShapeDtypeStruct((L, D), BF16),
                             jax.ShapeDtypeStruct((SUBLANES, D), F32), jax.ShapeDtypeStruct((SUBLANES, D), F32)],
                  compiler_params=_cparams(("arbitrary",)))(r, dy, g.reshape(1, D))


def _ple_fwd(h2, G, bg, E, name):
    L, D = h2.shape
    tl = _row_tile(L)

    def body(h_ref, g_ref, b_ref, e_ref, o_ref, ob_ref):
        o = h_ref[...] + _sigmoid(g_ref[...] + b_ref[...]) * e_ref[...]
        o_ref[...] = o
        ob_ref[...] = o.astype(BF16)

    row = pl.BlockSpec((tl, D), lambda i: (i, 0))
    vec = pl.BlockSpec((1, D), lambda i: (0, 0))
    return _pcall(body, name=name, grid=(L // tl,), in_specs=[row, row, vec, row], out_specs=[row, row],
                  out_shape=[jax.ShapeDtypeStruct((L, D), F32), jax.ShapeDtypeStruct((L, D), BF16)],
                  compiler_params=_cparams(("parallel",)))(h2, G, bg.reshape(1, D), E)


def _ple_bwd(dh3, G, bg, E, name):
    L, D = dh3.shape
    tl = _row_tile(L)

    def body(d_ref, g_ref, b_ref, e_ref, de_ref, dg_ref, db_ref):
        i = pl.program_id(0)
        d = d_ref[...]
        sg = _sigmoid(g_ref[...] + b_ref[...])
        de_ref[...] = (d * sg).astype(BF16)
        dgp = d * e_ref[...] * sg * (1.0 - sg)
        dg_ref[...] = dgp.astype(BF16)

        @pl.when(i == 0)
        def _():
            db_ref[...] = jnp.zeros_like(db_ref)

        db_ref[...] += _sum8(dgp)

    row = pl.BlockSpec((tl, D), lambda i: (i, 0))
    vec = pl.BlockSpec((1, D), lambda i: (0, 0))
    acc = pl.BlockSpec((SUBLANES, D), lambda i: (0, 0))
    return _pcall(body, name=name, grid=(L // tl,), in_specs=[row, row, vec, row], out_specs=[row, row, acc],
                  out_shape=[jax.ShapeDtypeStruct((L, D), BF16), jax.ShapeDtypeStruct((L, D), BF16),
                             jax.ShapeDtypeStruct((SUBLANES, D), F32)],
                  compiler_params=_cparams(("arbitrary",)))(dh3, G, bg.reshape(1, D), E)


def _loss_head(h, target, name):
    L, D = h.shape
    tl = _row_tile(L)

    def body(h_ref, t_ref, d_ref, s_ref):
        i = pl.program_id(0)
        e = h_ref[...] - t_ref[...]
        d_ref[...] = e * (1.0 / D)

        @pl.when(i == 0)
        def _():
            s_ref[...] = jnp.zeros_like(s_ref)

        s_ref[...] += _sum8(e * e)

    row = pl.BlockSpec((tl, D), lambda i: (i, 0))
    acc = pl.BlockSpec((SUBLANES, D), lambda i: (0, 0))
    return _pcall(body, name=name, grid=(L // tl,), in_specs=[row, row], out_specs=[row, acc],
                  out_shape=[jax.ShapeDtypeStruct((L, D), F32), jax.ShapeDtypeStruct((SUBLANES, D), F32)],
                  compiler_params=_cparams(("arbitrary",)))(h, target)


def _halo_prev(tl, ncol_blocks_fn):
    return lambda j, i: (jnp.maximum(i * (tl // SUBLANES) - 1, 0), ncol_blocks_fn(j))


def _conv_taps(cur, prev, w_ref, K):
    acc = w_ref[K - 1:K, :] * cur
    for k in range(K - 1):
        acc = acc + w_ref[k:k + 1, :] * _shift_down(cur, prev, K - 1 - k)
    return acc


def _ffn_act_fwd(U, w, b, name):
    L, F2 = U.shape
    F = F2 // 2
    K = w.shape[0]
    tl = _conv_row_tile(L, False)
    tc = _pick(F, (256, 128))

    def body(u_ref, up_ref, w_ref, b_ref, s_ref):
        i = pl.program_id(1)
        cur = u_ref[...]
        prev = jnp.where(i == 0, 0.0, up_ref[...])
        uc = _conv_taps(cur, prev, w_ref, K) + b_ref[...]
        g, v = uc[:, :tc], uc[:, tc:]
        s_ref[...] = (g * _sigmoid(g) * v).astype(BF16)

    return _pcall(body, name=name, grid=(F // tc, L // tl),
                  in_specs=[pl.BlockSpec((tl, 2 * tc), lambda j, i: (i, j)),
                            pl.BlockSpec((SUBLANES, 2 * tc), _halo_prev(tl, lambda j: j)),
                            pl.BlockSpec((K, 2 * tc), lambda j, i: (0, j)),
                            pl.BlockSpec((1, 2 * tc), lambda j, i: (0, j))],
                  out_specs=pl.BlockSpec((tl, tc), lambda j, i: (i, j)),
                  out_shape=jax.ShapeDtypeStruct((L, F), BF16),
                  compiler_params=_cparams(("parallel", "parallel")))(U, U, w, b.reshape(1, F2))


def _halo_next(tl, L, rows):
    return lambda j, i: (jnp.minimum((i + 1) * (tl // rows), L // rows - 1), j)


def _conv_taps_t(cur, nxt, w_ref, K):
    acc = w_ref[K - 1:K, :] * cur
    for k in range(K - 1):
        acc = acc + w_ref[k:k + 1, :] * _shift_up(cur, nxt, K - 1 - k)
    return acc


BF16_ROWS = 16


def _ffn_act_bwd(U, dS, w, b, name):
    L, F2 = U.shape
    F = F2 // 2
    K = w.shape[0]
    tl = _conv_row_tile(L, True)
    tc = _pick(F, (256, 128))
    nl = L // tl

    def body(u_ref, up_ref, un_ref, ds_ref, dsn_ref, w_ref, b_ref, du_ref, dw_ref, db_ref):
        i = pl.program_id(1)
        cur = u_ref[...]
        prev = jnp.where(i == 0, 0.0, up_ref[...])

        def at_conv_out(x, xprev, ds):
            uc = _conv_taps(x, xprev, w_ref, K) + b_ref[...]
            g, v = uc[:, :tc], uc[:, tc:]
            sg = _sigmoid(g)
            return jnp.concatenate([ds * v * sg * (1.0 + g * (1.0 - sg)), ds * g * sg], axis=1)

        duc = at_conv_out(cur, prev, ds_ref[...].astype(F32))
        duc_n = at_conv_out(un_ref[...], cur[tl - SUBLANES:], dsn_ref[...].astype(F32)[:SUBLANES])
        duc_n = jnp.where(i == nl - 1, 0.0, duc_n)
        du_ref[...] = _conv_taps_t(duc, duc_n, w_ref, K).astype(BF16)

        @pl.when(i == 0)
        def _():
            dw_ref[...] = jnp.zeros_like(dw_ref)
            db_ref[...] = jnp.zeros_like(db_ref)

        db_ref[...] += _sum8(duc)
        for k in range(K):
            dw_ref[k * SUBLANES:(k + 1) * SUBLANES, :] += _sum8(duc * _shift_down(cur, prev, K - 1 - k))

    return _pcall(body, name=name, grid=(F // tc, nl),
                  in_specs=[pl.BlockSpec((tl, 2 * tc), lambda j, i: (i, j)),
                            pl.BlockSpec((SUBLANES, 2 * tc), _halo_prev(tl, lambda j: j)),
                            pl.BlockSpec((SUBLANES, 2 * tc), _halo_next(tl, L, SUBLANES)),
                            pl.BlockSpec((tl, tc), lambda j, i: (i, j)),
                            pl.BlockSpec((BF16_ROWS, tc), _halo_next(tl, L, BF16_ROWS)),
                            pl.BlockSpec((K, 2 * tc), lambda j, i: (0, j)),
                            pl.BlockSpec((1, 2 * tc), lambda j, i: (0, j))],
                  out_specs=[pl.BlockSpec((tl, 2 * tc), lambda j, i: (i, j)),
                             pl.BlockSpec((K * SUBLANES, 2 * tc), lambda j, i: (0, j)),
                             pl.BlockSpec((SUBLANES, 2 * tc), lambda j, i: (0, j))],
                  out_shape=[jax.ShapeDtypeStruct((L, F2), BF16), jax.ShapeDtypeStruct((K * SUBLANES, F2), F32),
                             jax.ShapeDtypeStruct((SUBLANES, F2), F32)],
                  compiler_params=_cparams(("parallel", "arbitrary")))(U, U, U, dS, dS, w, b.reshape(1, F2))


def _sconv_fwd(Ac, w, name):
    L, C3 = Ac.shape
    C = C3 // 3
    K = w.shape[0]
    tl = _conv_row_tile(L, False)
    tc = LANES

    def body(a_ref, ap_ref, w_ref, y_ref):
        i = pl.program_id(1)
        a = a_ref[...]
        ap = ap_ref[...]
        p = a[:, tc:2 * tc] * a[:, 2 * tc:]
        pp = jnp.where(i == 0, 0.0, ap[:, tc:2 * tc] * ap[:, 2 * tc:])
        y_ref[...] = (a[:, :tc] * _conv_taps(p, pp, w_ref, K)).astype(BF16)

    return _pcall(body, name=name, grid=(C // tc, L // tl),
                  in_specs=[pl.BlockSpec((tl, 3 * tc), lambda j, i: (i, j)),
                            pl.BlockSpec((SUBLANES, 3 * tc), _halo_prev(tl, lambda j: j)),
                            pl.BlockSpec((K, tc), lambda j, i: (0, j))],
                  out_specs=pl.BlockSpec((tl, tc), lambda j, i: (i, j)),
                  out_shape=jax.ShapeDtypeStruct((L, C), BF16),
                  compiler_params=_cparams(("parallel", "parallel")))(Ac, Ac, w)


def _sconv_bwd_dc(Ac, dy, w, name):
    L, C3 = Ac.shape
    C = C3 // 3
    K = w.shape[0]
    tl = _conv_row_tile(L, False)
    tc = LANES

    def body(a_ref, ap_ref, dy_ref, dc_ref, dw_ref):
        i = pl.program_id(1)
        a = a_ref[...]
        ap = ap_ref[...]
        p = a[:, tc:2 * tc] * a[:, 2 * tc:]
        pp = jnp.where(i == 0, 0.0, ap[:, tc:2 * tc] * ap[:, 2 * tc:])
        dc = dy_ref[...] * a[:, :tc]
        dc_ref[...] = dc

        @pl.when(i == 0)
        def _():
            dw_ref[...] = jnp.zeros_like(dw_ref)

        for k in range(K):
            dw_ref[k * SUBLANES:(k + 1) * SUBLANES, :] += _sum8(dc * _shift_down(p, pp, K - 1 - k))

    return _pcall(body, name=name, grid=(C // tc, L // tl),
                  in_specs=[pl.BlockSpec((tl, 3 * tc), lambda j, i: (i, j)),
                            pl.BlockSpec((SUBLANES, 3 * tc), _halo_prev(tl, lambda j: j)),
                            pl.BlockSpec((tl, tc), lambda j, i: (i, j))],
                  out_specs=[pl.BlockSpec((tl, tc), lambda j, i: (i, j)),
                             pl.BlockSpec((K * SUBLANES, tc), lambda j, i: (0, j))],
                  out_shape=[jax.ShapeDtypeStruct((L, C), F32), jax.ShapeDtypeStruct((K * SUBLANES, C), F32)],
                  compiler_params=_cparams(("parallel", "arbitrary")))(Ac, Ac, dy)


def _sconv_bwd_da(Ac, dy, dc, w, name):
    L, C3 = Ac.shape
    C = C3 // 3
    K = w.shape[0]
    tl = _conv_row_tile(L, False)
    tc = LANES
    nl = L // tl

    def body(a_ref, ap_ref, dy_ref, dc_ref, dcn_ref, w_ref, o_ref):
        i = pl.program_id(1)
        a = a_ref[...]
        ap = ap_ref[...]
        gc, h = a[:, tc:2 * tc], a[:, 2 * tc:]
        p = gc * h
        pp = jnp.where(i == 0, 0.0, ap[:, tc:2 * tc] * ap[:, 2 * tc:])
        dgb = dy_ref[...] * _conv_taps(p, pp, w_ref, K)
        cur = dc_ref[...]
        nxt = jnp.where(i == nl - 1, 0.0, dcn_ref[...])
        dp = w_ref[K - 1:K, :] * cur
        for k in range(K - 1):
            dp = dp + w_ref[k:k + 1, :] * _shift_up(cur, nxt, K - 1 - k)
        o_ref[...] = jnp.concatenate([dgb, dp * h, dp * gc], axis=1).astype(BF16)

    return _pcall(body, name=name, grid=(C // tc, nl),
                  in_specs=[pl.BlockSpec((tl, 3 * tc), lambda j, i: (i, j)),
                            pl.BlockSpec((SUBLANES, 3 * tc), _halo_prev(tl, lambda j: j)),
                            pl.BlockSpec((tl, tc), lambda j, i: (i, j)),
                            pl.BlockSpec((tl, tc), lambda j, i: (i, j)),
                            pl.BlockSpec((SUBLANES, tc), lambda j, i: (jnp.minimum((i + 1) * (tl // SUBLANES), L // SUBLANES - 1), j)),
                            pl.BlockSpec((K, tc), lambda j, i: (0, j))],
                  out_specs=pl.BlockSpec((tl, 3 * tc), lambda j, i: (i, j)),
                  out_shape=jax.ShapeDtypeStruct((L, C3), BF16),
                  compiler_params=_cparams(("parallel", "parallel")))(Ac, Ac, dy, dc, dc, w)


def _fox_gate_fwd(Af, bf, name):
    L, W = Af.shape
    tl = _pick(L, (512, 256, 128))

    def body(a_ref, b_ref, f_ref, carry):
        i = pl.program_id(0)

        @pl.when(i == 0)
        def _():
            carry[...] = jnp.zeros_like(carry)

        z = a_ref[...] + b_ref[...]
        logf = jnp.minimum(z, 0.0) - jnp.log(1.0 + jnp.exp(-jnp.abs(z)))
        f = _tri_mm(_tri(tl), logf) + carry[...]
        f_ref[...] = f
        carry[...] = f[tl - 1:tl, :]

    row = pl.BlockSpec((tl, W), lambda i: (i, 0))
    return _pcall(body, name=name, grid=(L // tl,), in_specs=[row, pl.BlockSpec((1, W), lambda i: (0, 0))], out_specs=row,
                  out_shape=jax.ShapeDtypeStruct((L, W), F32), scratch_shapes=[pltpu.VMEM((1, W), F32)],
                  compiler_params=_cparams(("arbitrary",)))(Af, bf)


def _fox_gate_bwd(Af, bf, dF, name):
    L, W = Af.shape
    tl = _pick(L, (512, 256, 128))
    nl = L // tl

    def body(a_ref, b_ref, df_ref, o_ref, db_ref, carry):
        i = pl.program_id(0)

        @pl.when(i == 0)
        def _():
            carry[...] = jnp.zeros_like(carry)
            db_ref[...] = jnp.zeros_like(db_ref)

        z = a_ref[...] + b_ref[...]
        dlogf = _tri_mm(_tri(tl, upper=True), df_ref[...]) + carry[...]
        carry[...] = dlogf[0:1, :]
        dz = dlogf * _sigmoid(-z)
        o_ref[...] = dz
        db_ref[...] += _sum8(dz)

    row = pl.BlockSpec((tl, W), lambda i: (nl - 1 - i, 0))
    return _pcall(body, name=name, grid=(nl,),
                  in_specs=[row, pl.BlockSpec((1, W), lambda i: (0, 0)), row],
                  out_specs=[row, pl.BlockSpec((SUBLANES, W), lambda i: (0, 0))],
                  out_shape=[jax.ShapeDtypeStruct((L, W), F32), jax.ShapeDtypeStruct((SUBLANES, W), F32)],
                  scratch_shapes=[pltpu.VMEM((1, W), F32)],
                  compiler_params=_cparams(("arbitrary",)))(Af, bf, dF)


def _attn_tiles(L):
    t = _pick(L, (512, 256, 128))
    return t, t


def _attn_scores(q, k, fq, fk, diag, scale):
    s = _dot(q, k, NT) * scale + (fq - fk)
    if not diag:
        return s
    row = lax.broadcasted_iota(jnp.int32, s.shape, 0)
    col = lax.broadcasted_iota(jnp.int32, s.shape, 1)
    return jnp.where(col <= row, s, NEG)


def _attn_tile_spec(t, w):
    return pl.BlockSpec((None, t, w), lambda h, i: (h, i, 0))


def _attn_full_spec(L, w):
    return pl.BlockSpec((None, L, w), lambda h, i: (h, 0, 0))


def _attn_fwd(q, k, v, Fq, Fk, name):
    H, L, Dh = q.shape
    tq, tk = _attn_tiles(L)
    scale = Dh ** -0.5

    t = tq

    def body(q_ref, k_ref, v_ref, fq_ref, fk_ref, o_ref, lse_ref):
        qi = pl.program_id(1)
        qv, fq = q_ref[...], fq_ref[...]

        def chunk(j, carry, diag):
            m, l, acc = carry
            rows = pl.ds(pl.multiple_of(j * t, t), t)
            s = _attn_scores(qv, k_ref[rows, :], fq, fk_ref[:, rows], diag, scale)
            m_new = jnp.maximum(m, jnp.max(s, axis=-1, keepdims=True))
            p = jnp.exp(s - m_new)
            a = jnp.exp(m - m_new)
            return m_new, a * l + jnp.sum(p, axis=-1, keepdims=True), a * acc + _dot(p.astype(BF16), v_ref[rows, :], NN)

        init = (jnp.full((t, 1), NEG, F32), jnp.zeros((t, 1), F32), jnp.zeros((t, Dh), F32))
        m, l, acc = chunk(qi, lax.fori_loop(0, qi, lambda j, c: chunk(j, c, False), init), True)
        o_ref[...] = acc / l
        lse_ref[...] = m + jnp.log(l)

    return _pcall(body, name=name, grid=(H, L // t),
                  in_specs=[_attn_tile_spec(t, Dh), _attn_full_spec(L, Dh), _attn_full_spec(L, Dh),
                            _attn_tile_spec(t, 1), pl.BlockSpec((None, 1, L), lambda h, i: (h, 0, 0))],
                  out_specs=[_attn_tile_spec(t, Dh), _attn_tile_spec(t, 1)],
                  out_shape=[jax.ShapeDtypeStruct((H, L, Dh), F32), jax.ShapeDtypeStruct((H, L, 1), F32)],
                  compiler_params=_cparams(("parallel", "arbitrary")))(q, k, v, Fq, Fk)


def _attn_bwd_delta(q, k, v, Fq, Fk, lse, do, name):
    H, L, Dh = q.shape
    tq, tk = _attn_tiles(L)
    scale = Dh ** -0.5

    t = tq

    def body(q_ref, k_ref, v_ref, fq_ref, fk_ref, lse_ref, do_ref, d_ref):
        qi = pl.program_id(1)
        qv, fq, lse, dov = q_ref[...], fq_ref[...], lse_ref[...], do_ref[...]

        def chunk(j, acc, diag):
            rows = pl.ds(pl.multiple_of(j * t, t), t)
            s = _attn_scores(qv, k_ref[rows, :], fq, fk_ref[:, rows], diag, scale)
            return acc + jnp.sum(jnp.exp(s - lse) * _dot(dov, v_ref[rows, :], NT), axis=-1, keepdims=True)

        d_ref[...] = chunk(qi, lax.fori_loop(0, qi, lambda j, c: chunk(j, c, False), jnp.zeros((t, 1), F32)), True)

    return _pcall(body, name=name, grid=(H, L // t),
                  in_specs=[_attn_tile_spec(t, Dh), _attn_full_spec(L, Dh), _attn_full_spec(L, Dh), _attn_tile_spec(t, 1),
                            pl.BlockSpec((None, 1, L), lambda h, i: (h, 0, 0)), _attn_tile_spec(t, 1), _attn_tile_spec(t, Dh)],
                  out_specs=_attn_tile_spec(t, 1), out_shape=jax.ShapeDtypeStruct((H, L, 1), F32),
                  compiler_params=_cparams(("parallel", "arbitrary")))(q, k, v, Fq, Fk, lse, do)


def _attn_bwd_dq(q, k, v, Fq, Fk, delta, lse, do, name):
    H, L, Dh = q.shape
    tq, tk = _attn_tiles(L)
    scale = Dh ** -0.5

    t = tq

    def body(q_ref, k_ref, v_ref, fq_ref, fk_ref, dl_ref, lse_ref, do_ref, dq_ref):
        qi = pl.program_id(1)
        qv, fq, lse, dl, dov = q_ref[...], fq_ref[...], lse_ref[...], dl_ref[...], do_ref[...]

        def chunk(j, acc, diag):
            rows = pl.ds(pl.multiple_of(j * t, t), t)
            kc = k_ref[rows, :]
            s = _attn_scores(qv, kc, fq, fk_ref[:, rows], diag, scale)
            ds = jnp.exp(s - lse) * (_dot(dov, v_ref[rows, :], NT) - dl)
            return acc + _dot(ds.astype(BF16), kc, NN)

        acc = chunk(qi, lax.fori_loop(0, qi, lambda j, c: chunk(j, c, False), jnp.zeros((t, Dh), F32)), True)
        dq_ref[...] = acc * scale

    return _pcall(body, name=name, grid=(H, L // t),
                  in_specs=[_attn_tile_spec(t, Dh), _attn_full_spec(L, Dh), _attn_full_spec(L, Dh), _attn_tile_spec(t, 1),
                            pl.BlockSpec((None, 1, L), lambda h, i: (h, 0, 0)), _attn_tile_spec(t, 1), _attn_tile_spec(t, 1),
                            _attn_tile_spec(t, Dh)],
                  out_specs=_attn_tile_spec(t, Dh), out_shape=jax.ShapeDtypeStruct((H, L, Dh), F32),
                  compiler_params=_cparams(("parallel", "arbitrary")))(q, k, v, Fq, Fk, delta, lse, do)


def _attn_bwd_dkv(q, k, v, rowvals, Fk, do, name):
    H, L, Dh = q.shape
    t, _ = _attn_tiles(L)
    nq = L // t
    scale = Dh ** -0.5

    def body(q_ref, k_ref, v_ref, rv_ref, fk_ref, do_ref, dk_ref, dv_ref, df_ref):
        ki = pl.program_id(1)
        kv, vv, fk = k_ref[...], v_ref[...], fk_ref[...]

        def chunk(i, carry, diag):
            dk, dv, df = carry
            rows = pl.ds(pl.multiple_of(i * t, t), t)
            qc, dov, rv = q_ref[rows, :], do_ref[rows, :], rv_ref[rows, :]
            s = _attn_scores(qc, kv, rv[:, 0:1], fk, diag, scale)
            p = jnp.exp(s - rv[:, 1:2])
            ds = p * (_dot(dov, vv, NT) - rv[:, 2:3])
            return (dk + _dot(ds.astype(BF16), qc, TN), dv + _dot(p.astype(BF16), dov, TN),
                    df - jnp.sum(ds, axis=0, keepdims=True))

        init = chunk(ki, (jnp.zeros((t, Dh), F32), jnp.zeros((t, Dh), F32), jnp.zeros((1, t), F32)), True)
        dk, dv, df = lax.fori_loop(ki + 1, nq, lambda i, c: chunk(i, c, False), init)
        dk_ref[...] = dk * scale
        dv_ref[...] = dv
        df_ref[...] = df

    rk = pl.BlockSpec((None, 1, t), lambda h, j: (h, 0, j))
    return _pcall(body, name=name, grid=(H, nq),
                  in_specs=[_attn_full_spec(L, Dh), _attn_tile_spec(t, Dh), _attn_tile_spec(t, Dh), _attn_full_spec(L, LANES), rk,
                            _attn_full_spec(L, Dh)],
                  out_specs=[_attn_tile_spec(t, Dh), _attn_tile_spec(t, Dh), rk],
                  out_shape=[jax.ShapeDtypeStruct((H, L, Dh), F32), jax.ShapeDtypeStruct((H, L, Dh), F32),
                             jax.ShapeDtypeStruct((H, 1, L), F32)],
                  compiler_params=_cparams(("parallel", "arbitrary")))(q, k, v, rowvals, Fk, do)


def _mconv_fwd(xr, w, b, name):
    L, C = xr.shape
    K = w.shape[0]
    tl = _conv_row_tile(L, False)
    tc = _pick(C, (512, 384, 256, 128))

    def body(x_ref, xp_ref, w_ref, b_ref, o_ref):
        i = pl.program_id(1)
        prev = jnp.where(i == 0, 0.0, xp_ref[...])
        pre = _conv_taps(x_ref[...], prev, w_ref, K) + b_ref[...]
        o_ref[...] = pre * _sigmoid(pre)

    return _pcall(body, name=name, grid=(C // tc, L // tl),
                  in_specs=[pl.BlockSpec((tl, tc), lambda j, i: (i, j)),
                            pl.BlockSpec((SUBLANES, tc), _halo_prev(tl, lambda j: j)),
                            pl.BlockSpec((K, tc), lambda j, i: (0, j)),
                            pl.BlockSpec((1, tc), lambda j, i: (0, j))],
                  out_specs=pl.BlockSpec((tl, tc), lambda j, i: (i, j)),
                  out_shape=jax.ShapeDtypeStruct((L, C), F32),
                  compiler_params=_cparams(("parallel", "parallel")))(xr, xr, w, b.reshape(1, C))


def _mconv_bwd(xr, dact, w, b, name):
    L, C = xr.shape
    K = w.shape[0]
    tl = _conv_row_tile(L, True)
    tc = _pick(C, (512, 384, 256, 128))
    nl = L // tl

    def body(x_ref, xp_ref, xn_ref, d_ref, dn_ref, w_ref, b_ref, o_ref, dw_ref, db_ref):
        i = pl.program_id(1)
        cur = x_ref[...]
        prev = jnp.where(i == 0, 0.0, xp_ref[...])

        def at_conv_out(x, xprev, d):
            pre = _conv_taps(x, xprev, w_ref, K) + b_ref[...]
            sg = _sigmoid(pre)
            return d * sg * (1.0 + pre * (1.0 - sg))

        dpre = at_conv_out(cur, prev, d_ref[...])
        dpre_n = jnp.where(i == nl - 1, 0.0, at_conv_out(xn_ref[...], cur[tl - SUBLANES:], dn_ref[...]))
        o_ref[...] = _conv_taps_t(dpre, dpre_n, w_ref, K).astype(BF16)

        @pl.when(i == 0)
        def _():
            dw_ref[...] = jnp.zeros_like(dw_ref)
            db_ref[...] = jnp.zeros_like(db_ref)

        db_ref[...] += _sum8(dpre)
        for k in range(K):
            dw_ref[k * SUBLANES:(k + 1) * SUBLANES, :] += _sum8(dpre * _shift_down(cur, prev, K - 1 - k))

    return _pcall(body, name=name, grid=(C // tc, nl),
                  in_specs=[pl.BlockSpec((tl, tc), lambda j, i: (i, j)),
                            pl.BlockSpec((SUBLANES, tc), _halo_prev(tl, lambda j: j)),
                            pl.BlockSpec((SUBLANES, tc), _halo_next(tl, L, SUBLANES)),
                            pl.BlockSpec((tl, tc), lambda j, i: (i, j)),
                            pl.BlockSpec((SUBLANES, tc), _halo_next(tl, L, SUBLANES)),
                            pl.BlockSpec((K, tc), lambda j, i: (0, j)),
                            pl.BlockSpec((1, tc), lambda j, i: (0, j))],
                  out_specs=[pl.BlockSpec((tl, tc), lambda j, i: (i, j)),
                             pl.BlockSpec((K * SUBLANES, tc), lambda j, i: (0, j)),
                             pl.BlockSpec((SUBLANES, tc), lambda j, i: (0, j))],
                  out_shape=[jax.ShapeDtypeStruct((L, C), BF16), jax.ShapeDtypeStruct((K * SUBLANES, C), F32),
                             jax.ShapeDtypeStruct((SUBLANES, C), F32)],
                  compiler_params=_cparams(("parallel", "arbitrary")))(xr, xr, xr, dact, dact, w, b.reshape(1, C))


def _head_selector(R, P, heads_first):
    shape = (R, R * P) if heads_first else (R * P, R)
    head = lax.broadcasted_iota(jnp.int32, shape, 0 if heads_first else 1)
    lane = lax.broadcasted_iota(jnp.int32, shape, 1 if heads_first else 0)
    d = lane - head * P
    return jnp.where((d >= 0) & (d < P), 1.0, 0.0).astype(BF16)


def _ssd_prelude(dtc_ref, dtr_ref, bc_ref, br_ref, ac_ref, ar_ref, Q, R, P):
    raw_c = dtc_ref[...] + bc_ref[...]
    dt_c = _softplus(raw_c)
    dt_r = _softplus(dtr_ref[...] + br_ref[...])
    A_c = -jnp.exp(ac_ref[...])
    acs_c = _tri_mm(_tri(Q), dt_c * A_c)
    acs_r = _tri_mm(_tri(Q, upper=True), dt_r * (-jnp.exp(ar_ref[...])), tri_first=False)
    ea_c = jnp.exp(acs_c)
    dte_c = jnp.exp(acs_c[Q - 1:Q, :] - acs_c)
    wide = _tri_mm(_head_selector(R, P, True), jnp.concatenate([dt_c, ea_c, dte_c], axis=0), tri_first=False)
    return dict(raw_c=raw_c, dt_c=dt_c, A_c=A_c, acs_c=acs_c, acs_r=acs_r, ea_c=ea_c,
                DT=wide[:Q], EA=wide[Q:2 * Q], DTE=wide[2 * Q:])


def _ssd_decay_tile(pre, r, mask):
    return jnp.exp(jnp.where(mask, pre["acs_c"][:, r:r + 1] - pre["acs_r"][r:r + 1, :], NEG))


def _ssd_specs(Q, R, P, N, G, inner, rev=None):
    cc = (lambda c: c) if rev is None else rev
    return dict(
        x=pl.BlockSpec((Q, R * P), lambda g, c: (cc(c), g)),
        b=pl.BlockSpec((Q, N), lambda g, c: (cc(c), inner // N + g)),
        c=pl.BlockSpec((Q, N), lambda g, c: (cc(c), inner // N + G + g)),
        dtc=pl.BlockSpec((None, Q, R), lambda g, c: (g, cc(c), 0)),
        dtr=pl.BlockSpec((None, R, Q), lambda g, c: (g, 0, cc(c))),
        pc=pl.BlockSpec((None, 1, R), lambda g, c: (g, 0, 0)),
        pr=pl.BlockSpec((None, R, 1), lambda g, c: (g, 0, 0)),
        px=pl.BlockSpec((None, 1, R * P), lambda g, c: (g, 0, 0)),
        st=pl.BlockSpec((None, None, N, R * P), lambda g, c: (cc(c), g, 0, 0)))


def _ssd_fwd(act, dtc, dtr, bias_c, bias_r, alog_c, alog_r, dsk_x, name):
    G, L, R = dtc.shape
    N, P, Q = SSM_STATE, SSM_HEAD_DIM, SSM_CHUNK
    RP = R * P
    inner = G * RP
    nc = L // Q

    def body(x_ref, b_ref, c_ref, dtc_ref, dtr_ref, bc_ref, br_ref, ac_ref, ar_ref, dk_ref, y_ref, hp_ref, st):
        c = pl.program_id(1)

        @pl.when(c == 0)
        def _():
            st[...] = jnp.zeros_like(st)

        pre = _ssd_prelude(dtc_ref, dtr_ref, bc_ref, br_ref, ac_ref, ar_ref, Q, R, P)
        X = x_ref[...]
        XT = X * pre["DT"]
        Bb = b_ref[...].astype(BF16)
        Cb = c_ref[...].astype(BF16)
        CB = _dot(Cb, Bb, NT)
        mask = lax.broadcasted_iota(jnp.int32, (Q, Q), 0) >= lax.broadcasted_iota(jnp.int32, (Q, Q), 1)
        low = lax.broadcasted_iota(jnp.int32, (Q, 2 * P), 1) < P
        pieces = []
        for k in range(R // 2):
            xt2 = XT[:, 2 * P * k:2 * P * (k + 1)]
            acc = None
            for half in range(2):
                Gm = CB * _ssd_decay_tile(pre, 2 * k + half, mask)
                part = _dot(Gm.astype(BF16), jnp.where(low == (half == 0), xt2, 0.0).astype(BF16), NN)
                acc = part if acc is None else acc + part
            pieces.append(acc)
        HP = st[...]
        hp_ref[...] = HP
        yoff = pre["EA"] * _dot(Cb, HP.astype(BF16), NN)
        st[...] = HP * pre["EA"][Q - 1:Q, :] + _dot(Bb, (XT * pre["DTE"]).astype(BF16), TN)
        y_ref[...] = jnp.concatenate(pieces, axis=1) + yoff + dk_ref[...] * X

    sp = _ssd_specs(Q, R, P, N, G, inner)
    return _pcall(body, name=name, grid=(G, nc),
                  in_specs=[sp["x"], sp["b"], sp["c"], sp["dtc"], sp["dtr"], sp["pc"], sp["pr"], sp["pc"], sp["pr"], sp["px"]],
                  out_specs=[sp["x"], sp["st"]],
                  out_shape=[jax.ShapeDtypeStruct((L, inner), F32), jax.ShapeDtypeStruct((nc, G, N, RP), F32)],
                  scratch_shapes=[pltpu.VMEM((N, RP), F32)],
                  compiler_params=_cparams(("parallel", "arbitrary")))(act, act, act, dtc, dtr, bias_c, bias_r, alog_c, alog_r, dsk_x)


def _ssd_bwd(act, dtc, dtr, bias_c, bias_r, alog_c, alog_r, dsk_x, hprev, dy, name):
    G, L, R = dtc.shape
    N, P, Q = SSM_STATE, SSM_HEAD_DIM, SSM_CHUNK
    RP = R * P
    inner = G * RP
    nc = L // Q

    def body(x_ref, b_ref, c_ref, dtc_ref, dtr_ref, bc_ref, br_ref, ac_ref, ar_ref, dk_ref, hp_ref, dy_ref,
             dx_ref, db_ref, dc_ref, ddt_ref, gbias_ref, galog_ref, gdsk_ref, dst):
        c = pl.program_id(1)

        @pl.when(c == 0)
        def _():
            dst[...] = jnp.zeros_like(dst)
            gbias_ref[...] = jnp.zeros_like(gbias_ref)
            galog_ref[...] = jnp.zeros_like(galog_ref)
            gdsk_ref[...] = jnp.zeros_like(gdsk_ref)

        pre = _ssd_prelude(dtc_ref, dtr_ref, bc_ref, br_ref, ac_ref, ar_ref, Q, R, P)
        DT, EA, DTE = pre["DT"], pre["EA"], pre["DTE"]
        E_END = EA[Q - 1:Q, :]
        X, DY = x_ref[...], dy_ref[...]
        XT = X * DT
        Bb = b_ref[...].astype(BF16)
        Cb = c_ref[...].astype(BF16)
        CB = _dot(Cb, Bb, NT)
        HP, dH = hp_ref[...], dst[...]
        HPb, dHb = HP.astype(BF16), dH.astype(BF16)
        EDY = EA * DY
        EDYb = EDY.astype(BF16)
        dC = _dot(EDYb, HPb, NT)
        dHP = _dot(Cb, EDYb, TN)
        da_off = EDY * _dot(Cb, HPb, NN)
        Z = _dot(Bb, dHb, NN)
        XD = XT * DTE
        dB = _dot(XD.astype(BF16), dHb, NT)
        dXT = DTE * Z
        t_x = XD * Z
        hh = jnp.sum(dH * HP, axis=0, keepdims=True) * E_END
        dst[...] = dHP + dH * E_END
        mask = lax.broadcasted_iota(jnp.int32, (Q, Q), 0) >= lax.broadcasted_iota(jnp.int32, (Q, Q), 1)
        eye = lax.broadcasted_iota(jnp.int32, (Q, Q), 0) == lax.broadcasted_iota(jnp.int32, (Q, Q), 1)
        low = lax.broadcasted_iota(jnp.int32, (Q, 2 * P), 1) < P
        lane = lax.broadcasted_iota(jnp.int32, (Q, R), 1)
        dCB = jnp.zeros((Q, Q), F32)
        da_mat = jnp.zeros((Q, R), F32)
        pieces = []
        for k in range(R // 2):
            sl = slice(2 * P * k, 2 * P * (k + 1))
            xt2, dy2 = XT[:, sl], DY[:, sl]
            acc = None
            for half in range(2):
                r = 2 * k + half
                sel = low == (half == 0)
                Lm = _ssd_decay_tile(pre, r, mask)
                Gm = CB * Lm
                dyb = jnp.where(sel, dy2, 0.0).astype(BF16)
                part = _dot(Gm.astype(BF16), dyb, TN)
                acc = part if acc is None else acc + part
                dG = jnp.where(mask, _dot(dyb, jnp.where(sel, xt2, 0.0).astype(BF16), NT), 0.0)
                Mm = dG * Gm
                dCB = dCB + dG * Lm
                colsum = jnp.sum(jnp.where(eye, jnp.sum(Mm, axis=0, keepdims=True), 0.0), axis=1, keepdims=True)
                da_mat = jnp.where(lane == r, jnp.sum(Mm, axis=1, keepdims=True) - colsum, da_mat)
            pieces.append(acc)
        dXT = dXT + jnp.concatenate(pieces, axis=1)
        dCBb = dCB.astype(BF16)
        dc_ref[...] = dC + _dot(dCBb, Bb, NN)
        db_ref[...] = dB + _dot(dCBb, Cb, TN)
        dx_ref[...] = dXT * DT + dk_ref[...] * DY
        pad = jnp.zeros((SUBLANES - 1, RP), F32)
        sums = _tri_mm(_head_selector(R, P, False), jnp.concatenate([da_off, t_x, dXT * X, DY * X, hh, pad], axis=0), tri_first=False)
        t = sums[Q:2 * Q]
        da_end = jnp.sum(t, axis=0, keepdims=True) + sums[4 * Q:4 * Q + 1]
        rowi = lax.broadcasted_iota(jnp.int32, (Q, R), 0)
        da_mat = da_mat + sums[:Q] - t + jnp.where(rowi == Q - 1, da_end, 0.0)
        ddtA = _tri_mm(_tri(Q, upper=True), da_mat)
        ddt_raw = (ddtA * pre["A_c"] + sums[2 * Q:3 * Q]) * _sigmoid(pre["raw_c"])
        ddt_ref[...] = ddt_raw
        gbias_ref[...] += jnp.sum(ddt_raw, axis=0, keepdims=True)
        galog_ref[...] += jnp.sum(ddtA * pre["dt_c"], axis=0, keepdims=True) * pre["A_c"]
        gdsk_ref[...] += jnp.sum(sums[3 * Q:4 * Q], axis=0, keepdims=True)

    sp = _ssd_specs(Q, R, P, N, G, inner, rev=lambda c: nc - 1 - c)
    bout = pl.BlockSpec((Q, N), lambda g, c: (nc - 1 - c, g))
    return _pcall(body, name=name, grid=(G, nc),
                  in_specs=[sp["x"], sp["b"], sp["c"], sp["dtc"], sp["dtr"], sp["pc"], sp["pr"], sp["pc"], sp["pr"], sp["px"],
                            sp["st"], sp["x"]],
                  out_specs=[sp["x"], bout, bout, sp["dtc"], sp["pc"], sp["pc"], sp["pc"]],
                  out_shape=[jax.ShapeDtypeStruct((L, inner), F32), jax.ShapeDtypeStruct((L, G * N), F32),
                             jax.ShapeDtypeStruct((L, G * N), F32), jax.ShapeDtypeStruct((G, L, R), F32),
                             jax.ShapeDtypeStruct((G, 1, R), F32), jax.ShapeDtypeStruct((G, 1, R), F32),
                             jax.ShapeDtypeStruct((G, 1, R), F32)],
                  scratch_shapes=[pltpu.VMEM((N, RP), F32)],
                  compiler_params=_cparams(("parallel", "arbitrary")))(
        act, act, act, dtc, dtr, bias_c, bias_r, alog_c, alog_r, dsk_x, hprev, dy)


def _gnorm_fwd(y, z, g, name):
    L, Dn = y.shape
    gs = Dn // SSM_GROUPS
    tl = _row_tile(L)

    def body(y_ref, z_ref, g_ref, o_ref):
        for k in range(SSM_GROUPS):
            sl = slice(k * gs, (k + 1) * gs)
            zz = z_ref[:, sl]
            u = y_ref[:, sl] * zz * _sigmoid(zz)
            rstd = lax.rsqrt(jnp.mean(u * u, axis=-1, keepdims=True) + RMS_EPS)
            o_ref[:, sl] = (u * rstd * g_ref[:, sl]).astype(BF16)

    row = pl.BlockSpec((tl, Dn), lambda i: (i, 0))
    return _pcall(body, name=name, grid=(L // tl,), in_specs=[row, row, pl.BlockSpec((1, Dn), lambda i: (0, 0))],
                  out_specs=row, out_shape=jax.ShapeDtypeStruct((L, Dn), BF16),
                  compiler_params=_cparams(("parallel",)))(y, z, g.reshape(1, Dn))


def _gnorm_bwd(y, z, g, dout, name):
    L, Dn = y.shape
    gs = Dn // SSM_GROUPS
    tl = _row_tile(L)

    def body(y_ref, z_ref, g_ref, d_ref, dy_ref, dz_ref, dg_ref):
        i = pl.program_id(0)

        @pl.when(i == 0)
        def _():
            dg_ref[...] = jnp.zeros_like(dg_ref)

        for k in range(SSM_GROUPS):
            sl = slice(k * gs, (k + 1) * gs)
            zz = z_ref[:, sl]
            yy = y_ref[:, sl]
            sg = _sigmoid(zz)
            sil = zz * sg
            u = yy * sil
            rstd = lax.rsqrt(jnp.mean(u * u, axis=-1, keepdims=True) + RMS_EPS)
            n = u * rstd
            d = d_ref[:, sl]
            dn = d * g_ref[:, sl]
            du = rstd * (dn - n * jnp.mean(dn * n, axis=-1, keepdims=True))
            dy_ref[:, sl] = du * sil
            dz_ref[:, sl] = (du * yy * sg * (1.0 + zz * (1.0 - sg))).astype(BF16)
            dg_ref[:, sl] += _sum8(d * n)

    row = pl.BlockSpec((tl, Dn), lambda i: (i, 0))
    return _pcall(body, name=name, grid=(L // tl,), in_specs=[row, row, pl.BlockSpec((1, Dn), lambda i: (0, 0)), row],
                  out_specs=[row, row, pl.BlockSpec((SUBLANES, Dn), lambda i: (0, 0))],
                  out_shape=[jax.ShapeDtypeStruct((L, Dn), F32), jax.ShapeDtypeStruct((L, Dn), BF16),
                             jax.ShapeDtypeStruct((SUBLANES, Dn), F32)],
                  compiler_params=_cparams(("arbitrary",)))(y, z, g.reshape(1, Dn), dout)


def _adamw(w, g, m, v, name):
    rows, W = w.shape
    tr = _pick(rows, (512, 256, 128, 64, 32, 16, 8))
    c1 = 1.0 / (1.0 - ADAM_B1 ** ADAM_STEP)
    c2 = 1.0 / (1.0 - ADAM_B2 ** ADAM_STEP)

    def body(w_ref, g_ref, m_ref, v_ref, d_ref, nm_ref, nv_ref):
        g_ = g_ref[...]
        nm = ADAM_B1 * m_ref[...] + (1.0 - ADAM_B1) * g_
        nv = ADAM_B2 * v_ref[...] + (1.0 - ADAM_B2) * (g_ * g_)
        nm_ref[...] = nm
        nv_ref[...] = nv
        d_ref[...] = -ADAM_LR * ((nm * c1) / (jnp.sqrt(nv * c2) + ADAM_EPS) + ADAM_WD * w_ref[...])

    blk = pl.BlockSpec((tr, W), lambda i: (i, 0))
    return _pcall(body, name=name, grid=(rows // tr,), in_specs=[blk] * 4, out_specs=[blk] * 3,
                  out_shape=[jax.ShapeDtypeStruct((rows, W), F32)] * 3, compiler_params=_cparams(("parallel",)))(w, g, m, v)


def _sum_slots(x, name, extra=None):
    n, rows, W = x.shape
    tr = _pick(rows, (512, 256, 128, 64, 32, 16, 8))
    has_extra = extra is not None

    def body(*refs):
        if has_extra:
            e_ref, x_ref, o_ref = refs
            acc = e_ref[...].astype(F32)
            start = 0
        else:
            x_ref, o_ref = refs
            acc = x_ref[0].astype(F32)
            start = 1
        for s in range(start, n):
            acc = acc + x_ref[s].astype(F32)
        o_ref[...] = acc

    blk = pl.BlockSpec((tr, W), lambda i: (i, 0))
    xblk = pl.BlockSpec((n, tr, W), lambda i: (0, i, 0))
    return _pcall(body, name=name, grid=(rows // tr,), in_specs=([blk] if has_extra else []) + [xblk], out_specs=blk,
                  out_shape=jax.ShapeDtypeStruct((rows, W), F32), compiler_params=_cparams(("parallel",)))(
        *(([extra] if has_extra else []) + [x]))


def _add_pairs(a, b, name):
    n, rows, W = a.shape
    tr = _pick(rows, (512, 256, 128, 64, 32, 16, 8))

    def body(a_ref, b_ref, o_ref):
        o_ref[...] = (a_ref[...].astype(F32) + b_ref[...].astype(F32)).astype(BF16)

    blk = pl.BlockSpec((None, tr, W), lambda s, i: (s, i, 0))
    return _pcall(body, name=name, grid=(n, rows // tr), in_specs=[blk, blk], out_specs=blk,
                  out_shape=jax.ShapeDtypeStruct((n, rows, W), BF16), compiler_params=_cparams(("parallel", "parallel")))(a, b)


MESH = pl.DeviceIdType.MESH
HBM_SPEC = pl.BlockSpec(memory_space=pl.ANY)


def _me():
    return lax.axis_index("x"), lax.axis_index("y"), lax.axis_index("c")


def _all_gather(arrs, name):
    n = len(arrs)

    def body(*refs):
        ins, outs = refs[:n], refs[n:2 * n]
        send_sems, recv_sems, local_sems = refs[2 * n:]
        x, y, c = _me()
        me, sib = (x, y, c), (x, y, 1 - c)
        chips = [(1 - x, y), (x, 1 - y), (1 - x, 1 - y)]

        def slot(a, dev):
            return outs[a].at[4 * dev[0] + 2 * dev[1] + dev[2]]

        def copy(a, k, block, to, src=None):
            return pltpu.make_async_remote_copy(src_ref=slot(a, block) if src is None else src, dst_ref=slot(a, block),
                                                send_sem=send_sems.at[a * 7 + k], recv_sem=recv_sems.at[a * 7 + k],
                                                device_id=to, device_id_type=MESH)

        mine = [pltpu.make_async_copy(ins[a], slot(a, me), local_sems.at[a]) for a in range(n)]
        for cp in mine:
            cp.start()
        first = []
        for a in range(n):
            first.append(copy(a, 0, me, sib, src=ins[a]))
            first += [copy(a, 1 + j, me, (*chip, c), src=ins[a]) for j, chip in enumerate(chips)]
        for cp in first:
            cp.start()
        passed = []
        for j, chip in enumerate(chips):
            for a in range(n):
                copy(a, 1 + j, (*chip, c), me).wait_recv()
                fw = copy(a, 4 + j, (*chip, c), sib)
                fw.start()
                passed.append(fw)
        for a in range(n):
            copy(a, 0, sib, me).wait_recv()
            for j, chip in enumerate(chips):
                copy(a, 4 + j, (*chip, 1 - c), me).wait_recv()
        for cp in first + passed:
            cp.wait_send()
        for cp in mine:
            cp.wait()

    return _pcall(body, name=name, in_specs=[HBM_SPEC] * n, out_specs=[HBM_SPEC] * n,
                  out_shape=[jax.ShapeDtypeStruct((N_DEV,) + a.shape, a.dtype) for a in arrs],
                  scratch_shapes=[pltpu.SemaphoreType.DMA((7 * n,)), pltpu.SemaphoreType.DMA((7 * n,)),
                                  pltpu.SemaphoreType.DMA((n,))])(*arrs)


def _rs_sibling(gs, name):
    n = len(gs)

    def body(*refs):
        g_refs, o_refs = refs[:n], refs[n:2 * n]
        send_sems, recv_sems = refs[2 * n:]
        x, y, c = _me()
        sib = (x, y, 1 - c)
        cps = [pltpu.make_async_remote_copy(src_ref=g_refs[a].at[2 * q + (1 - c)], dst_ref=o_refs[a].at[q],
                                            send_sem=send_sems.at[4 * a + q], recv_sem=recv_sems.at[4 * a + q],
                                            device_id=sib, device_id_type=MESH) for a in range(n) for q in range(4)]
        for cp in cps:
            cp.start()
        for cp in cps:
            cp.wait()

    return _pcall(body, name=name, in_specs=[HBM_SPEC] * n, out_specs=[HBM_SPEC] * n,
                  out_shape=[jax.ShapeDtypeStruct((4,) + g.shape[1:], g.dtype) for g in gs],
                  scratch_shapes=[pltpu.SemaphoreType.DMA((4 * n,)), pltpu.SemaphoreType.DMA((4 * n,))])(*gs)


def _rs_chips(ps, name):
    n = len(ps)

    def body(*refs):
        p_refs, o_refs = refs[:n], refs[n:2 * n]
        send_sems, recv_sems = refs[2 * n:]
        x, y, c = _me()
        chips = [(1 - x, y), (x, 1 - y), (1 - x, 1 - y)]
        cps = [pltpu.make_async_remote_copy(src_ref=p_refs[a].at[2 * chip[0] + chip[1]], dst_ref=o_refs[a].at[j],
                                            send_sem=send_sems.at[3 * a + j], recv_sem=recv_sems.at[3 * a + j],
                                            device_id=(*chip, c), device_id_type=MESH)
               for j, chip in enumerate(chips) for a in range(n)]
        for cp in cps:
            cp.start()
        for cp in cps:
            cp.wait()

    return _pcall(body, name=name, in_specs=[HBM_SPEC] * n, out_specs=[HBM_SPEC] * n,
                  out_shape=[jax.ShapeDtypeStruct((3,) + p.shape[1:], p.dtype) for p in ps],
                  scratch_shapes=[pltpu.SemaphoreType.DMA((3 * n,)), pltpu.SemaphoreType.DMA((3 * n,))])(*ps)


def _reduce_scatter(gs, name):
    x, y, c = _me()
    from_sib = _rs_sibling(gs, name + "_sib")
    pairs = []
    for a, (g, fs) in enumerate(zip(gs, from_sib)):
        own = g.reshape((4, 2) + g.shape[1:])
        pairs.append(_add_pairs(jnp.where(c == 0, own[:, 0], own[:, 1]), fs, f"{name}_pair{a}"))
    from_chips = _rs_chips(pairs, name + "_chips")
    return [_sum_slots(fc, f"{name}_sum{a}", extra=lax.dynamic_index_in_dim(p, 2 * x + y, axis=0, keepdims=False))
            for a, (p, fc) in enumerate(zip(pairs, from_chips))]


BIG = ("even_w_in", "even_w_out", "odd_w_in", "odd_w_out", "ffn_w_up", "ffn_w_down", "ple_w_proj", "ple_w_gate")
SMALL_SHARDED = ("even_conv_w", "odd_conv_w", "odd_conv_b", "odd_norm_g", "ffn_conv_w")
REPLICATED = ("even_b_f", "odd_dt_bias", "odd_a_log", "odd_d_skip", "ln_mix_g", "ln_mix_b", "ffn_conv_b",
              "ln_ffn_g", "ln_ffn_b", "ple_b_gate")
WEIGHTS = ("even_w_in", "even_b_f", "even_conv_w", "even_w_out", "odd_w_in", "odd_conv_w", "odd_conv_b", "odd_dt_bias",
           "odd_a_log", "odd_d_skip", "odd_norm_g", "odd_w_out", "ln_mix_g", "ln_mix_b", "ffn_w_up", "ffn_conv_w",
           "ffn_conv_b", "ffn_w_down", "ln_ffn_g", "ln_ffn_b", "ple_w_proj", "ple_w_gate", "ple_b_gate")


def _full_shapes():
    d = _dims()
    return {
        "even_w_in": ((1, D_MODEL, d["even_in"]), 2), "even_b_f": ((1, FOX_HEADS), None),
        "even_conv_w": ((1, CONV_WIDTH, CONV_DIM), 2), "even_w_out": ((1, d["even_mix"], D_MODEL), 1),
        "odd_w_in": ((1, D_MODEL, d["odd_in"]), 2), "odd_conv_w": ((1, SSM_CONV_WIDTH, d["conv_ch"]), 2),
        "odd_conv_b": ((1, d["conv_ch"]), 1), "odd_dt_bias": ((1, d["ssm_heads"]), None),
        "odd_a_log": ((1, d["ssm_heads"]), None), "odd_d_skip": ((1, d["ssm_heads"]), None),
        "odd_norm_g": ((1, d["ssm_inner"]), 1), "odd_w_out": ((1, d["ssm_inner"], D_MODEL), 1),
        "ln_mix_g": ((DEPTH, D_MODEL), None), "ln_mix_b": ((DEPTH, D_MODEL), None),
        "ffn_w_up": ((DEPTH, D_MODEL, 2 * D_FF), 2), "ffn_conv_w": ((DEPTH, FFN_CONV_WIDTH, 2 * D_FF), 2),
        "ffn_conv_b": ((DEPTH, 2 * D_FF), None), "ffn_w_down": ((DEPTH, D_FF, D_MODEL), 1),
        "ln_ffn_g": ((DEPTH, D_MODEL), None), "ln_ffn_b": ((DEPTH, D_MODEL), None),
        "ple_w_proj": ((DEPTH, PLE_DIM, D_MODEL), 2), "ple_w_gate": ((DEPTH, D_MODEL, D_MODEL), 1),
        "ple_b_gate": ((DEPTH, D_MODEL), None),
    }


def _shard_shape(name):
    shape, ax = _full_shapes()[name]
    if ax is None:
        return shape
    return tuple(s // N_DEV if i == ax else s for i, s in enumerate(shape))


def _as2d(a, lead=0):
    return a.reshape(a.shape[:lead] + (-1, a.shape[-1]))


def _part_rows(shape):
    n = int(np.prod(shape))
    return -(-(-(-n // PACK_W)) // SUBLANES) * SUBLANES


def _pack_small(parts):
    out = []
    for p in parts:
        n, rows = int(np.prod(p.shape)), _part_rows(p.shape)
        out.append(jnp.pad(p.reshape(-1).astype(F32), (0, rows * PACK_W - n)).reshape(rows, PACK_W))
    return jnp.concatenate(out, axis=0)


def _unpack_small(pack, shapes):
    lead = pack.shape[:-2]
    out, off = [], 0
    for s in shapes:
        n, rows = int(np.prod(s)), _part_rows(s)
        part = pack[..., off:off + rows, :].reshape(lead + (-1,))[..., :n]
        out.append(part.reshape(lead + tuple(s)))
        off += rows
    return out


def _assemble(gathered, name):
    shape, ax = _full_shapes()[name]
    return jnp.moveaxis(gathered, 0, ax).reshape(shape)


def _split_dest(full, name):
    shape, ax = _full_shapes()[name]
    sh = shape[:ax] + (N_DEV, shape[ax] // N_DEV) + shape[ax + 1:]
    return jnp.moveaxis(full.reshape(sh), ax, 0)


def _interleave_cols(w, parts, tc):
    C = w.shape[-1] // parts
    sh = w.shape[:-1]
    return w.reshape(sh + (parts, C // tc, tc)).swapaxes(-3, -2).reshape(sh + (parts * C,))


def _deinterleave_cols(w, parts, tc):
    C = w.shape[-1] // parts
    sh = w.shape[:-1]
    return w.reshape(sh + (C // tc, parts, tc)).swapaxes(-3, -2).reshape(sh + (parts * C,))


def _ffn_tc():
    return _pick(D_FF, (256, 128))


def _heads_first(a, heads):
    L = a.shape[0]
    return a.reshape(L, heads, -1).transpose(1, 0, 2)


def _heads_last(a):
    h, L, d = a.shape
    return a.transpose(1, 0, 2).reshape(L, h * d)


def _pad_cols(a, to):
    return jnp.pad(a, ((0, 0), (0, to - a.shape[1])))


def _tail_fwd(i, h_in, mix, p_i, W, sp):
    r1, h1, h1b = _ln_fwd(h_in, mix, sp["ln_mix_g"][i], sp["ln_mix_b"][i], f"ln_mix_fwd{i}")
    U = _mm(h1b, W["ffn_up"][i], "nn", F32, f"ffn_up{i}")
    S = _ffn_act_fwd(U, sp["ffn_conv_w_il"][i], sp["ffn_conv_b_il"][i], f"ffn_act_fwd{i}")
    ffn = _mm(S, W["ffn_down"][i], "nn", F32, f"ffn_down{i}")
    r2, h2, h2b = _ln_fwd(h1, ffn, sp["ln_ffn_g"][i], sp["ln_ffn_b"][i], f"ln_ffn_fwd{i}")
    G = _mm(h2b, W["ple_gate"][i], "nn", F32, f"ple_gate{i}")
    E = _mm(p_i, W["ple_proj"][i], "nn", F32, f"ple_proj{i}")
    h3, h3b = _ple_fwd(h2, G, sp["ple_b_gate"][i], E, f"ple_fwd{i}")
    return h3, h3b, dict(r1=r1, h1b=h1b, U=U, S=S, r2=r2, h2b=h2b, G=G, E=E, p=p_i)


def _tail_bwd(i, dh3, sv, W, sp, grads):
    alpha = _alpha()
    dE, dGp, dbg = _ple_bwd(dh3, sv["G"], sp["ple_b_gate"][i], sv["E"], f"ple_bwd{i}")
    grads["ple_b_gate"][i] = dbg.sum(0)
    grads["ple_w_proj"][i] = _mm(sv["p"], dE, "tn", F32, f"d_ple_proj{i}")
    grads["ple_w_gate"][i] = _mm(sv["h2b"], dGp, "tn", F32, f"d_ple_gate{i}")
    dh2 = _mm(dGp, W["ple_gate"][i], "nt", F32, f"dx_ple_gate{i}", add=dh3)
    dr2, dr2b, dg, db = _ln_bwd(sv["r2"], dh2, sp["ln_ffn_g"][i], f"ln_ffn_bwd{i}")
    grads["ln_ffn_g"][i], grads["ln_ffn_b"][i] = dg.sum(0), db.sum(0)
    grads["ffn_w_down"][i] = _mm(sv["S"], dr2b, "tn", F32, f"d_ffn_down{i}")
    dS = _mm(dr2b, W["ffn_down"][i], "nt", BF16, f"dx_ffn_down{i}")
    dU, dcw, dcb = _ffn_act_bwd(sv["U"], dS, sp["ffn_conv_w_il"][i], sp["ffn_conv_b_il"][i], f"ffn_act_bwd{i}")
    K = FFN_CONV_WIDTH
    tc = _ffn_tc()
    grads["ffn_conv_w"][i] = _deinterleave_cols(dcw.reshape(K, SUBLANES, -1).sum(1), 2, tc)
    grads["ffn_conv_b"][i] = _deinterleave_cols(dcb.sum(0), 2, tc)
    grads["ffn_w_up"][i] = _deinterleave_cols(_mm(sv["h1b"], dU, "tn", F32, f"d_ffn_up{i}"), 2, tc)
    dh1 = _mm(dU, W["ffn_up"][i], "nt", F32, f"dx_ffn_up{i}", add=dr2, add_scale=alpha)
    dr1, dr1b, dg, db = _ln_bwd(sv["r1"], dh1, sp["ln_mix_g"][i], f"ln_mix_bwd{i}")
    grads["ln_mix_g"][i], grads["ln_mix_b"][i] = dg.sum(0), db.sum(0)
    return dr1, dr1b


def _even_fwd(h, W, sp):
    L = h.shape[0]
    H, Dh = FOX_HEADS, FOX_HEAD_DIM
    Ac = _mm(h, W["even_in_conv"], "nn", F32, "even_in_conv")
    qkv = _mm(h, W["even_in_qkv"], "nn", BF16, "even_in_qkv")
    Af = _mm(h, W["even_in_f"], "nn", F32, "even_in_f")
    y_a = _sconv_fwd(Ac, sp["even_conv_w_il"], "sconv_fwd")
    Fc = _fox_gate_fwd(Af, sp["even_b_f_pad"], "fox_gate_fwd")
    Fh = Fc[:, :H].T
    Fq, Fk = Fh.reshape(H, L, 1), Fh.reshape(H, 1, L)
    fd = H * Dh
    q, k, v = (_heads_first(qkv[:, j * fd:(j + 1) * fd], H) for j in range(3))
    o, lse = _attn_fwd(q, k, v, Fq, Fk, "attn_fwd")
    Y = jnp.concatenate([y_a, _heads_last(o).astype(BF16)], axis=1)
    mix = _mm(Y, W["even_out"], "nn", F32, "even_out")
    return mix, dict(h=h, Ac=Ac, Af=Af, q=q, k=k, v=v, Fq=Fq, Fk=Fk, o=o, lse=lse, Y=Y)


def _even_bwd(dmix, dres, sv, W, sp, grads):
    H, Dh = FOX_HEADS, FOX_HEAD_DIM
    C = CONV_DIM
    L = dmix.shape[0]
    grads["even_w_out"][0] = _mm(sv["Y"], dmix, "tn", F32, "d_even_out")
    dY = _mm(dmix, W["even_out"], "nt", F32, "dx_even_out")
    dya = dY[:, :C]
    do = _heads_first(dY[:, C:], H).astype(BF16)
    dc, dcw = _sconv_bwd_dc(sv["Ac"], dya, sp["even_conv_w_il"], "sconv_bwd_dc")
    grads["even_conv_w"][0] = dcw.reshape(CONV_WIDTH, SUBLANES, -1).sum(1)
    dAc = _sconv_bwd_da(sv["Ac"], dya, dc, sp["even_conv_w_il"], "sconv_bwd_da")
    delta = _attn_bwd_delta(sv["q"], sv["k"], sv["v"], sv["Fq"], sv["Fk"], sv["lse"], do, "attn_bwd_delta")
    dq = _attn_bwd_dq(sv["q"], sv["k"], sv["v"], sv["Fq"], sv["Fk"], delta, sv["lse"], do, "attn_bwd_dq")
    rowvals = jnp.pad(jnp.concatenate([sv["Fq"], sv["lse"], delta], axis=-1), ((0, 0), (0, 0), (0, LANES - 3)))
    dk, dv, dFk = _attn_bwd_dkv(sv["q"], sv["k"], sv["v"], rowvals, sv["Fk"], do, "attn_bwd_dkv")
    dqkv = jnp.concatenate([_heads_last(dq), _heads_last(dk), _heads_last(dv)], axis=1).astype(BF16)
    dF = _pad_cols(dFk.reshape(H, L).T, LANES)
    dAf, dbf = _fox_gate_bwd(sv["Af"], sp["even_b_f_pad"], dF, "fox_gate_bwd")
    grads["even_b_f"][0] = dbf.sum(0)[:H]
    h = sv["h"]
    gc = _deinterleave_cols(_mm(h, dAc, "tn", F32, "d_even_in_conv"), 3, LANES)
    gq = _mm(h, dqkv, "tn", F32, "d_even_in_qkv")
    gf = _mm(h, dAf, "tn", F32, "d_even_in_f")[:, :H]
    grads["even_w_in"][0] = jnp.concatenate([gc, gq, gf], axis=1)
    dh = _mm(dAc, W["even_in_conv"], "nt", F32, "dx_even_in_conv", add=dres, add_scale=_alpha())
    dh = _mm(dqkv, W["even_in_qkv"], "nt", F32, "dx_even_in_qkv", add=dh)
    dh = _mm(dAf, W["even_in_f"], "nt", F32, "dx_even_in_f", add=dh)
    return dh


def _group_layouts(v, G):
    R = v.shape[0] // G
    return v.reshape(G, 1, R), v.reshape(G, R, 1)


def _odd_fwd(h, W, sp):
    d = _dims()
    L = h.shape[0]
    Hs, G, N, P = d["ssm_heads"], SSM_GROUPS, SSM_STATE, SSM_HEAD_DIM
    R = Hs // G
    inner = d["ssm_inner"]
    z = _mm(h, W["odd_in_z"], "nn", F32, "odd_in_z")
    xr = _mm(h, W["odd_in_x"], "nn", F32, "odd_in_x")
    dtp = _mm(h, W["odd_in_dt"], "nn", F32, "odd_in_dt")
    act = _mconv_fwd(xr, sp["odd_conv_w"], sp["odd_conv_b"], "mconv_fwd")
    dtg = dtp[:, :Hs].reshape(L, G, R)
    dtc, dtr = dtg.transpose(1, 0, 2), dtg.transpose(1, 2, 0)
    dsk_x = jnp.repeat(sp["odd_d_skip"], P).reshape(G, 1, R * P)
    ssd_in = (act, dtc, dtr) + _group_layouts(sp["odd_dt_bias"], G) + _group_layouts(sp["odd_a_log"], G) + (dsk_x,)
    y, hprev = _ssd_fwd(*ssd_in, "ssd_fwd")
    u = _gnorm_fwd(y, z, sp["odd_norm_g"], "gnorm_fwd")
    mix = _mm(u, W["odd_out"], "nn", F32, "odd_out")
    return mix, dict(h=h, z=z, xr=xr, ssd_in=ssd_in, hprev=hprev, y=y, u=u)


def _odd_bwd(dmix, dres, sv, W, sp, grads):
    d = _dims()
    L = dmix.shape[0]
    Hs, G, N, P = d["ssm_heads"], SSM_GROUPS, SSM_STATE, SSM_HEAD_DIM
    grads["odd_w_out"][0] = _mm(sv["u"], dmix, "tn", F32, "d_odd_out")
    du = _mm(dmix, W["odd_out"], "nt", F32, "dx_odd_out")
    dy, dz, dg = _gnorm_bwd(sv["y"], sv["z"], sp["odd_norm_g"], du, "gnorm_bwd")
    grads["odd_norm_g"][0] = dg.sum(0)
    dxs, dB, dC, ddt, gbias, galog, gdsk = _ssd_bwd(*sv["ssd_in"], sv["hprev"], dy, "ssd_bwd")
    grads["odd_dt_bias"][0] = gbias.reshape(Hs)
    grads["odd_a_log"][0] = galog.reshape(Hs)
    grads["odd_d_skip"][0] = gdsk.reshape(Hs)
    dact = jnp.concatenate([dxs, dB, dC], axis=1)
    dxr, dcw, dcb = _mconv_bwd(sv["xr"], dact, sp["odd_conv_w"], sp["odd_conv_b"], "mconv_bwd")
    grads["odd_conv_w"][0] = dcw.reshape(SSM_CONV_WIDTH, SUBLANES, -1).sum(1)
    grads["odd_conv_b"][0] = dcb.sum(0)
    ddtp = _pad_cols(ddt.transpose(1, 0, 2).reshape(L, Hs), W["odd_in_dt"].shape[1])
    h = sv["h"]
    gz = _mm(h, dz, "tn", F32, "d_odd_in_z")
    gx = _mm(h, dxr, "tn", F32, "d_odd_in_x")
    gdt = _mm(h, ddtp, "tn", F32, "d_odd_in_dt")[:, :Hs]
    grads["odd_w_in"][0] = jnp.concatenate([gz, gx, gdt], axis=1)
    dh = _mm(dz, W["odd_in_z"], "nt", F32, "dx_odd_in_z", add=dres, add_scale=_alpha())
    dh = _mm(dxr, W["odd_in_x"], "nt", F32, "dx_odd_in_x", add=dh)
    dh = _mm(ddtp, W["odd_in_dt"], "nt", F32, "dx_odd_in_dt", add=dh)
    return dh


def _prepare_weights(full):
    d = _dims()
    C, fd, H = CONV_DIM, d["fox_dim"], FOX_HEADS
    tc = _ffn_tc()
    W, sp = {}, {}
    ew = full["even_w_in"][0]
    W["even_in_conv"] = _interleave_cols(ew[:, :3 * C], 3, LANES)
    W["even_in_qkv"] = ew[:, 3 * C:3 * C + 3 * fd]
    W["even_in_f"] = _pad_cols(ew[:, 3 * C + 3 * fd:], LANES)
    W["even_out"] = full["even_w_out"][0]
    ow = full["odd_w_in"][0]
    inner, cch, Hs = d["ssm_inner"], d["conv_ch"], d["ssm_heads"]
    W["odd_in_z"] = ow[:, :inner]
    W["odd_in_x"] = ow[:, inner:inner + cch]
    W["odd_in_dt"] = _pad_cols(ow[:, inner + cch:], -(-Hs // LANES) * LANES)
    W["odd_out"] = full["odd_w_out"][0]
    W["ffn_up"] = [_interleave_cols(full["ffn_w_up"][i], 2, tc) for i in range(DEPTH)]
    W["ffn_down"] = [full["ffn_w_down"][i] for i in range(DEPTH)]
    W["ple_proj"] = [full["ple_w_proj"][i] for i in range(DEPTH)]
    W["ple_gate"] = [full["ple_w_gate"][i] for i in range(DEPTH)]
    sp["even_conv_w_il"] = full["even_conv_w"][0]
    sp["even_b_f_pad"] = _pad_cols(full["even_b_f"], LANES)
    sp["odd_conv_w"] = full["odd_conv_w"][0]
    sp["odd_conv_b"] = full["odd_conv_b"][0]
    sp["odd_norm_g"] = full["odd_norm_g"][0]
    for n in ("odd_dt_bias", "odd_a_log", "odd_d_skip"):
        sp[n] = full[n][0]
    for n in ("ln_mix_g", "ln_mix_b", "ln_ffn_g", "ln_ffn_b", "ple_b_gate"):
        sp[n] = full[n]
    sp["ffn_conv_w_il"] = [_interleave_cols(full["ffn_conv_w"][i], 2, tc) for i in range(DEPTH)]
    sp["ffn_conv_b_il"] = [_interleave_cols(full["ffn_conv_b"][i], 2, tc) for i in range(DEPTH)]
    return W, sp


def _local_step(x, p, target, full):
    W, sp = _prepare_weights(full)
    grads = {n: [None] * _full_shapes()[n][0][0] for n in WEIGHTS}
    pb = p.astype(BF16)
    mix0, sv_e = _even_fwd(x.astype(BF16), W, sp)
    h3_0, h3_0b, sv_t0 = _tail_fwd(0, x, mix0, pb[0], W, sp)
    mix1, sv_o = _odd_fwd(h3_0b, W, sp)
    h3_1, _, sv_t1 = _tail_fwd(1, h3_0, mix1, pb[1], W, sp)
    dh, sq = _loss_head(h3_1, target, "loss_head")
    dr1, dr1b = _tail_bwd(1, dh, sv_t1, W, sp, grads)
    dh = _odd_bwd(dr1b, dr1, sv_o, W, sp, grads)
    dr1, dr1b = _tail_bwd(0, dh, sv_t0, W, sp, grads)
    dx = _even_bwd(dr1b, dr1, sv_e, W, sp, grads)
    grads = {n: jnp.stack(v) for n, v in grads.items()}
    return jnp.sum(sq), dx, grads


def kernel(x, p, even_w_in, even_b_f, even_conv_w, even_w_out, odd_w_in, odd_conv_w, odd_conv_b, odd_dt_bias, odd_a_log, odd_d_skip, odd_norm_g, odd_w_out, ln_mix_g, ln_mix_b, ffn_w_up, ffn_conv_w, ffn_conv_b, ffn_w_down, ln_ffn_g, ln_ffn_b, ple_w_proj, ple_w_gate, ple_b_gate, loss_target, m_even_w_in, m_even_b_f, m_even_conv_w, m_even_w_out, m_odd_w_in, m_odd_conv_w, m_odd_conv_b, m_odd_dt_bias, m_odd_a_log, m_odd_d_skip, m_odd_norm_g, m_odd_w_out, m_ln_mix_g, m_ln_mix_b, m_ffn_w_up, m_ffn_conv_w, m_ffn_conv_b, m_ffn_w_down, m_ln_ffn_g, m_ln_ffn_b, m_ple_w_proj, m_ple_w_gate, m_ple_b_gate, v_even_w_in, v_even_b_f, v_even_conv_w, v_even_w_out, v_odd_w_in, v_odd_conv_w, v_odd_conv_b, v_odd_dt_bias, v_odd_a_log, v_odd_d_skip, v_odd_norm_g, v_odd_w_out, v_ln_mix_g, v_ln_mix_b, v_ffn_w_up, v_ffn_conv_w, v_ffn_conv_b, v_ffn_w_down, v_ln_ffn_g, v_ln_ffn_b, v_ple_w_proj, v_ple_w_gate, v_ple_b_gate):
    args = locals()
    w = {n: args[n] for n in WEIGHTS}
    m = {n: args["m_" + n] for n in WEIGHTS}
    v = {n: args["v_" + n] for n in WEIGHTS}
    me = 4 * lax.axis_index("x") + 2 * lax.axis_index("y") + lax.axis_index("c")

    gathered = _all_gather([_as2d(w[n]).astype(BF16) for n in BIG] + [_pack_small([w[n] for n in SMALL_SHARDED])], "ag_weights")
    full = dict(w)
    for n, g in zip(BIG, gathered[:-1]):
        full[n] = _assemble(g.reshape((N_DEV,) + _shard_shape(n)), n)
    for n, g in zip(SMALL_SHARDED, _unpack_small(gathered[-1], [_shard_shape(n) for n in SMALL_SHARDED])):
        full[n] = _assemble(g, n)

    sq, dx, grads = _local_step(x[0], p[:, 0], loss_target[0], full)
    loss = lax.psum(0.5 * sq / D_MODEL, ("x", "y", "c"))

    gsum_big = _reduce_scatter([_as2d(_split_dest(grads[n], n), 1).astype(BF16) for n in BIG], "rs_grads")
    g_final = {n: g.reshape(_shard_shape(n)) for n, g in zip(BIG, gsum_big)}
    small_names = SMALL_SHARDED + REPLICATED
    (small_all,) = _all_gather([_pack_small([grads[n] for n in small_names])], "ag_small_grads")
    small_sum = _sum_slots(small_all, "sum_small_grads")
    for n, g in zip(small_names, _unpack_small(small_sum, [_full_shapes()[n][0] for n in small_names])):
        g_final[n] = lax.dynamic_index_in_dim(_split_dest(g, n), me, axis=0, keepdims=False) if n in SMALL_SHARDED else g

    out = {}
    for n in BIG:
        res = _adamw(*[_as2d(t[n]) for t in (w, g_final, m, v)], "adamw_" + n)
        out[n] = [r.reshape(_shard_shape(n)) for r in res]
    shapes = [_shard_shape(n) for n in small_names]
    res = _adamw(*[_pack_small([t[n] for n in small_names]) for t in (w, g_final, m, v)], "adamw_small")
    for n, d_, m_, v_ in zip(small_names, *[_unpack_small(r, shapes) for r in res]):
        out[n] = [d_, m_, v_]
    return (loss, dx[None], *[g_final[n] for n in WEIGHTS], *[out[n][0] for n in WEIGHTS],
            *[out[n][1] for n in WEIGHTS], *[out[n][2] for n in WEIGHTS])
```

```python
import jax
import jax.numpy as jnp
import numpy as np
from jax import lax
from jax.experimental import pallas as pl
from jax.experimental.pallas import tpu as pltpu

D_MODEL = 1024
SEQ = 8192
DEPTH = 2
CONV_DIM = 512
CONV_WIDTH = 3
FOX_HEADS = 8
FOX_HEAD_DIM = 64
SSM_HEAD_DIM = 64
SSM_GROUPS = 4
SSM_STATE = 128
SSM_CONV_WIDTH = 4
SSM_CHUNK = 128
D_FF = 2816
FFN_CONV_WIDTH = 3
PLE_DIM = 256
LN_EPS = 1e-5
RMS_EPS = 1e-5
ADAM_LR = 0.001
ADAM_B1 = 0.9
ADAM_B2 = 0.999
ADAM_EPS = 1e-08
ADAM_WD = 0.01
ADAM_STEP = 10
N_DEV = 8

F32 = jnp.float32
BF16 = jnp.bfloat16
NEG = -1e30
LANES = 128
SUBLANES = 8
PACK_W = 1024
VMEM_LIMIT = 48 * 1024 * 1024


def _dims():
    fox_dim = FOX_HEADS * FOX_HEAD_DIM
    ssm_inner = 2 * D_MODEL
    ssm_heads = ssm_inner // SSM_HEAD_DIM
    conv_ch = ssm_inner + 2 * SSM_GROUPS * SSM_STATE
    return dict(fox_dim=fox_dim, even_in=3 * CONV_DIM + 3 * fox_dim + FOX_HEADS, even_mix=CONV_DIM + fox_dim,
                ssm_inner=ssm_inner, ssm_heads=ssm_heads, conv_ch=conv_ch, odd_in=ssm_inner + conv_ch + ssm_heads)


def _alpha():
    return (2.0 * DEPTH) ** 0.25


def _pick(dim, prefs):
    for p in prefs:
        if dim % p == 0:
            return p
    return dim


def _pcall(body, **kw):
    return pl.pallas_call(body, **kw)


def _cparams(sem=None, **kw):
    if sem is not None:
        kw["dimension_semantics"] = sem
    return pltpu.CompilerParams(vmem_limit_bytes=VMEM_LIMIT, **kw)


def _sigmoid(x):
    return 1.0 / (1.0 + jnp.exp(-x))


def _softplus(x):
    return jnp.maximum(x, 0.0) + jnp.log(1.0 + jnp.exp(-jnp.abs(x)))


def _sum8(x):
    n, c = x.shape
    return x.reshape(n // SUBLANES, SUBLANES, c).sum(axis=0)


def _dot(a, b, dims):
    return lax.dot_general(a, b, (dims, ((), ())), preferred_element_type=F32)


NN = ((1,), (0,))
NT = ((1,), (1,))
TN = ((0,), (0,))


def _split3(x):
    hi = x.astype(BF16)
    r1 = x - hi.astype(F32)
    mid = r1.astype(BF16)
    lo = (r1 - mid.astype(F32)).astype(BF16)
    return hi, mid, lo


def _tri_mm(tri_bf16, x, tri_first=True):
    if tri_first:
        return sum(_dot(tri_bf16, part, NN) for part in _split3(x))
    return sum(_dot(part, tri_bf16, NN) for part in _split3(x))


def _tri(n, upper=False):
    r = lax.broadcasted_iota(jnp.int32, (n, n), 0)
    c = lax.broadcasted_iota(jnp.int32, (n, n), 1)
    return jnp.where((r <= c) if upper else (r >= c), 1.0, 0.0).astype(BF16)


def _shift_down(cur, prev8, k):
    if k == 0:
        return cur
    ext = jnp.concatenate([prev8, cur], axis=0)
    return pltpu.roll(ext, k, axis=0)[SUBLANES:]


def _shift_up(cur, next8, k):
    if k == 0:
        return cur
    n = cur.shape[0]
    ext = jnp.concatenate([cur, next8], axis=0)
    return pltpu.roll(ext, n + SUBLANES - k, axis=0)[:n]


def _mm(a, b, mode, out_dtype, name, add=None, add_scale=1.0):
    if mode == "nn":
        (M, K), (K2, N) = a.shape, b.shape
    elif mode == "nt":
        (M, K), (N, K2) = a.shape, b.shape
    else:
        (K, M), (K2, N) = a.shape, b.shape
    assert K == K2, (a.shape, b.shape, mode)
    tm = _pick(M, (1024, 512, 256, 128))
    tn = _pick(N, (1408, 1024, 768, 512, 384, 256, 128))
    tk = K if K <= 2048 else _pick(K, (1408, 1024, 768, 512, 256, 128))
    nk = K // tk
    dims = {"nn": NN, "nt": NT, "tn": TN}[mode]
    a_spec = pl.BlockSpec((tk, tm), lambda i, j, k: (k, i)) if mode == "tn" else pl.BlockSpec((tm, tk), lambda i, j, k: (i, k))
    b_spec = pl.BlockSpec((tn, tk), lambda i, j, k: (j, k)) if mode == "nt" else pl.BlockSpec((tk, tn), lambda i, j, k: (k, j))
    o_spec = pl.BlockSpec((tm, tn), lambda i, j, k: (i, j))
    has_add = add is not None

    def body(*refs):
        a_ref, b_ref = refs[:2]
        add_ref = refs[2] if has_add else None
        o_ref = refs[2 + has_add]
        prod = _dot(a_ref[...].astype(BF16), b_ref[...].astype(BF16), dims)
        if nk == 1:
            if has_add:
                prod = prod + add_scale * add_ref[...].astype(F32)
            o_ref[...] = prod.astype(out_dtype)
            return
        acc = refs[3 + has_add]
        k = pl.program_id(2)

        @pl.when(k == 0)
        def _():
            if has_add:
                acc[...] = prod + add_scale * add_ref[...].astype(F32)
            else:
                acc[...] = prod

        @pl.when(k > 0)
        def _():
            acc[...] += prod

        @pl.when(k == nk - 1)
        def _():
            o_ref[...] = acc[...].astype(out_dtype)

    ins = [a, b] + ([add] if has_add else [])
    specs = [a_spec, b_spec] + ([o_spec] if has_add else [])
    return _pcall(body, name=name, grid=(M // tm, N // tn, nk), in_specs=specs, out_specs=o_spec,
                  out_shape=jax.ShapeDtypeStruct((M, N), out_dtype),
                  scratch_shapes=[pltpu.VMEM((tm, tn), F32)] if nk > 1 else [],
                  compiler_params=_cparams(("parallel", "parallel", "arbitrary")))(*ins)


def _row_tile(L):
    return _pick(L, (256, 128))


def _conv_row_tile(L, backward):
    return _pick(L, (512, 256, 128)) if backward else _pick(L, (1024, 512, 256, 128))


def _ln_fwd(h, mix, g, b, name):
    L, D = h.shape
    tl = _row_tile(L)
    alpha = _alpha()

    def body(h_ref, m_ref, g_ref, b_ref, r_ref, y_ref, yb_ref):
        r = alpha * h_ref[...] + m_ref[...]
        mu = jnp.mean(r, axis=-1, keepdims=True)
        xc = r - mu
        var = jnp.mean(xc * xc, axis=-1, keepdims=True)
        r_ref[...] = r
        y = xc * lax.rsqrt(var + LN_EPS) * g_ref[...] + b_ref[...]
        y_ref[...] = y
        yb_ref[...] = y.astype(BF16)

    row = pl.BlockSpec((tl, D), lambda i: (i, 0))
    vec = pl.BlockSpec((1, D), lambda i: (0, 0))
    return _pcall(body, name=name, grid=(L // tl,), in_specs=[row, row, vec, vec], out_specs=[row, row, row],
                  out_shape=[jax.ShapeDtypeStruct((L, D), F32)] * 2 + [jax.ShapeDtypeStruct((L, D), BF16)],
                  compiler_params=_cparams(("parallel",)))(h, mix, g.reshape(1, D), b.reshape(1, D))


def _ln_bwd(r, dy, g, name):
    L, D = r.shape
    tl = _row_tile(L)

    def body(r_ref, dy_ref, g_ref, dr_ref, drb_ref, dg_ref, db_ref):
        i = pl.program_id(0)
        r_ = r_ref[...]
        dy_ = dy_ref[...]
        mu = jnp.mean(r_, axis=-1, keepdims=True)
        xc = r_ - mu
        rstd = lax.rsqrt(jnp.mean(xc * xc, axis=-1, keepdims=True) + LN_EPS)
        xhat = xc * rstd
        dxh = dy_ * g_ref[...]
        dr = rstd * (dxh - jnp.mean(dxh, axis=-1, keepdims=True) - xhat * jnp.mean(dxh * xhat, axis=-1, keepdims=True))
        dr_ref[...] = dr
        drb_ref[...] = dr.astype(BF16)

        @pl.when(i == 0)
        def _():
            dg_ref[...] = jnp.zeros_like(dg_ref)
            db_ref[...] = jnp.zeros_like(db_ref)

        dg_ref[...] += _sum8(dy_ * xhat)
        db_ref[...] += _sum8(dy_)

    row = pl.BlockSpec((tl, D), lambda i: (i, 0))
    vec = pl.BlockSpec((1, D), lambda i: (0, 0))
    acc = pl.BlockSpec((SUBLANES, D), lambda i: (0, 0))
    return _pcall(body, name=name, grid=(L // tl,), in_specs=[row, row, vec], out_specs=[row, row, acc, acc],
                  out_shape=[jax.ShapeDtypeStruct((L, D), F32), jax.ShapeDtypeStruct((L, D), BF16),
                             jax.ShapeDtypeStruct((SUBLANES, D), F32), jax.ShapeDtypeStruct((SUBLANES, D), F32)],
                  compiler_params=_cparams(("arbitrary",)))(r, dy, g.reshape(1, D))


def _ple_fwd(h2, G, bg, E, name):
    L, D = h2.shape
    tl = _row_tile(L)

    def body(h_ref, g_ref, b_ref, e_ref, o_ref, ob_ref):
        o = h_ref[...] + _sigmoid(g_ref[...] + b_ref[...]) * e_ref[...]
        o_ref[...] = o
        ob_ref[...] = o.astype(BF16)

    row = pl.BlockSpec((tl, D), lambda i: (i, 0))
    vec = pl.BlockSpec((1, D), lambda i: (0, 0))
    return _pcall(body, name=name, grid=(L // tl,), in_specs=[row, row, vec, row], out_specs=[row, row],
                  out_shape=[jax.ShapeDtypeStruct((L, D), F32), jax.ShapeDtypeStruct((L, D), BF16)],
                  compiler_params=_cparams(("parallel",)))(h2, G, bg.reshape(1, D), E)


def _ple_bwd(dh3, G, bg, E, name):
    L, D = dh3.shape
    tl = _row_tile(L)

    def body(d_ref, g_ref, b_ref, e_ref, de_ref, dg_ref, db_ref):
        i = pl.program_id(0)
        d = d_ref[...]
        sg = _sigmoid(g_ref[...] + b_ref[...])
        de_ref[...] = (d * sg).astype(BF16)
        dgp = d * e_ref[...] * sg * (1.0 - sg)
        dg_ref[...] = dgp.astype(BF16)

        @pl.when(i == 0)
        def _():
            db_ref[...] = jnp.zeros_like(db_ref)

        db_ref[...] += _sum8(dgp)

    row = pl.BlockSpec((tl, D), lambda i: (i, 0))
    vec = pl.BlockSpec((1, D), lambda i: (0, 0))
    acc = pl.BlockSpec((SUBLANES, D), lambda i: (0, 0))
    return _pcall(body, name=name, grid=(L // tl,), in_specs=[row, row, vec, row], out_specs=[row, row, acc],
                  out_shape=[jax.ShapeDtypeStruct((L, D), BF16), jax.ShapeDtypeStruct((L, D), BF16),
                             jax.ShapeDtypeStruct((SUBLANES, D), F32)],
                  compiler_params=_cparams(("arbitrary",)))(dh3, G, bg.reshape(1, D), E)


def _loss_head(h, target, name):
    L, D = h.shape
    tl = _row_tile(L)

    def body(h_ref, t_ref, d_ref, s_ref):
        i = pl.program_id(0)
        e = h_ref[...] - t_ref[...]
        d_ref[...] = e * (1.0 / D)

        @pl.when(i == 0)
        def _():
            s_ref[...] = jnp.zeros_like(s_ref)

        s_ref[...] += _sum8(e * e)

    row = pl.BlockSpec((tl, D), lambda i: (i, 0))
    acc = pl.BlockSpec((SUBLANES, D), lambda i: (0, 0))
    return _pcall(body, name=name, grid=(L // tl,), in_specs=[row, row], out_specs=[row, acc],
                  out_shape=[jax.ShapeDtypeStruct((L, D), F32), jax.ShapeDtypeStruct((SUBLANES, D), F32)],
                  compiler_params=_cparams(("arbitrary",)))(h, target)


def _halo_prev(tl, ncol_blocks_fn):
    return lambda j, i: (jnp.maximum(i * (tl // SUBLANES) - 1, 0), ncol_blocks_fn(j))


def _conv_taps(cur, prev, w_ref, K):
    acc = w_ref[K - 1:K, :] * cur
    for k in range(K - 1):
        acc = acc + w_ref[k:k + 1, :] * _shift_down(cur, prev, K - 1 - k)
    return acc


def _ffn_act_fwd(U, w, b, name):
    L, F2 = U.shape
    F = F2 // 2
    K = w.shape[0]
    tl = _conv_row_tile(L, False)
    tc = _pick(F, (256, 128))

    def body(u_ref, up_ref, w_ref, b_ref, s_ref):
        i = pl.program_id(1)
        cur = u_ref[...]
        prev = jnp.where(i == 0, 0.0, up_ref[...])
        uc = _conv_taps(cur, prev, w_ref, K) + b_ref[...]
        g, v = uc[:, :tc], uc[:, tc:]
        s_ref[...] = (g * _sigmoid(g) * v).astype(BF16)

    return _pcall(body, name=name, grid=(F // tc, L // tl),
                  in_specs=[pl.BlockSpec((tl, 2 * tc), lambda j, i: (i, j)),
                            pl.BlockSpec((SUBLANES, 2 * tc), _halo_prev(tl, lambda j: j)),
                            pl.BlockSpec((K, 2 * tc), lambda j, i: (0, j)),
                            pl.BlockSpec((1, 2 * tc), lambda j, i: (0, j))],
                  out_specs=pl.BlockSpec((tl, tc), lambda j, i: (i, j)),
                  out_shape=jax.ShapeDtypeStruct((L, F), BF16),
                  compiler_params=_cparams(("parallel", "parallel")))(U, U, w, b.reshape(1, F2))


def _halo_next(tl, L, rows):
    return lambda j, i: (jnp.minimum((i + 1) * (tl // rows), L // rows - 1), j)


def _conv_taps_t(cur, nxt, w_ref, K):
    acc = w_ref[K - 1:K, :] * cur
    for k in range(K - 1):
        acc = acc + w_ref[k:k + 1, :] * _shift_up(cur, nxt, K - 1 - k)
    return acc


BF16_ROWS = 16


def _ffn_act_bwd(U, dS, w, b, name):
    L, F2 = U.shape
    F = F2 // 2
    K = w.shape[0]
    tl = _conv_row_tile(L, True)
    tc = _pick(F, (256, 128))
    nl = L // tl

    def body(u_ref, up_ref, un_ref, ds_ref, dsn_ref, w_ref, b_ref, du_ref, dw_ref, db_ref):
        i = pl.program_id(1)
        cur = u_ref[...]
        prev = jnp.where(i == 0, 0.0, up_ref[...])

        def at_conv_out(x, xprev, ds):
            uc = _conv_taps(x, xprev, w_ref, K) + b_ref[...]
            g, v = uc[:, :tc], uc[:, tc:]
            sg = _sigmoid(g)
            return jnp.concatenate([ds * v * sg * (1.0 + g * (1.0 - sg)), ds * g * sg], axis=1)

        duc = at_conv_out(cur, prev, ds_ref[...].astype(F32))
        duc_n = at_conv_out(un_ref[...], cur[tl - SUBLANES:], dsn_ref[...].astype(F32)[:SUBLANES])
        duc_n = jnp.where(i == nl - 1, 0.0, duc_n)
        du_ref[...] = _conv_taps_t(duc, duc_n, w_ref, K).astype(BF16)

        @pl.when(i == 0)
        def _():
            dw_ref[...] = jnp.zeros_like(dw_ref)
            db_ref[...] = jnp.zeros_like(db_ref)

        db_ref[...] += _sum8(duc)
        for k in range(K):
            dw_ref[k * SUBLANES:(k + 1) * SUBLANES, :] += _sum8(duc * _shift_down(cur, prev, K - 1 - k))

    return _pcall(body, name=name, grid=(F // tc, nl),
                  in_specs=[pl.BlockSpec((tl, 2 * tc), lambda j, i: (i, j)),
                            pl.BlockSpec((SUBLANES, 2 * tc), _halo_prev(tl, lambda j: j)),
                            pl.BlockSpec((SUBLANES, 2 * tc), _halo_next(tl, L, SUBLANES)),
                            pl.BlockSpec((tl, tc), lambda j, i: (i, j)),
                            pl.BlockSpec((BF16_ROWS, tc), _halo_next(tl, L, BF16_ROWS)),
                            pl.BlockSpec((K, 2 * tc), lambda j, i: (0, j)),
                            pl.BlockSpec((1, 2 * tc), lambda j, i: (0, j))],
                  out_specs=[pl.BlockSpec((tl, 2 * tc), lambda j, i: (i, j)),
                             pl.BlockSpec((K * SUBLANES, 2 * tc), lambda j, i: (0, j)),
                             pl.BlockSpec((SUBLANES, 2 * tc), lambda j, i: (0, j))],
                  out_shape=[jax.ShapeDtypeStruct((L, F2), BF16), jax.ShapeDtypeStruct((K * SUBLANES, F2), F32),
                             jax.ShapeDtypeStruct((SUBLANES, F2), F32)],
                  compiler_params=_cparams(("parallel", "arbitrary")))(U, U, U, dS, dS, w, b.reshape(1, F2))


def _sconv_fwd(Ac, w, name):
    L, C3 = Ac.shape
    C = C3 // 3
    K = w.shape[0]
    tl = _conv_row_tile(L, False)
    tc = LANES

    def body(a_ref, ap_ref, w_ref, y_ref):
        i = pl.program_id(1)
        a = a_ref[...]
        ap = ap_ref[...]
        p = a[:, tc:2 * tc] * a[:, 2 * tc:]
        pp = jnp.where(i == 0, 0.0, ap[:, tc:2 * tc] * ap[:, 2 * tc:])
        y_ref[...] = (a[:, :tc] * _conv_taps(p, pp, w_ref, K)).astype(BF16)

    return _pcall(body, name=name, grid=(C // tc, L // tl),
                  in_specs=[pl.BlockSpec((tl, 3 * tc), lambda j, i: (i, j)),
                            pl.BlockSpec((SUBLANES, 3 * tc), _halo_prev(tl, lambda j: j)),
                            pl.BlockSpec((K, tc), lambda j, i: (0, j))],
                  out_specs=pl.BlockSpec((tl, tc), lambda j, i: (i, j)),
                  out_shape=jax.ShapeDtypeStruct((L, C), BF16),
                  compiler_params=_cparams(("parallel", "parallel")))(Ac, Ac, w)


def _sconv_bwd_dc(Ac, dy, w, name):
    L, C3 = Ac.shape
    C = C3 // 3
    K = w.shape[0]
    tl = _conv_row_tile(L, False)
    tc = LANES

    def body(a_ref, ap_ref, dy_ref, dc_ref, dw_ref):
        i = pl.program_id(1)
        a = a_ref[...]
        ap = ap_ref[...]
        p = a[:, tc:2 * tc] * a[:, 2 * tc:]
        pp = jnp.where(i == 0, 0.0, ap[:, tc:2 * tc] * ap[:, 2 * tc:])
        dc = dy_ref[...] * a[:, :tc]
        dc_ref[...] = dc

        @pl.when(i == 0)
        def _():
            dw_ref[...] = jnp.zeros_like(dw_ref)

        for k in range(K):
            dw_ref[k * SUBLANES:(k + 1) * SUBLANES, :] += _sum8(dc * _shift_down(p, pp, K - 1 - k))

    return _pcall(body, name=name, grid=(C // tc, L // tl),
                  in_specs=[pl.BlockSpec((tl, 3 * tc), lambda j, i: (i, j)),
                            pl.BlockSpec((SUBLANES, 3 * tc), _halo_prev(tl, lambda j: j)),
                            pl.BlockSpec((tl, tc), lambda j, i: (i, j))],
                  out_specs=[pl.BlockSpec((tl, tc), lambda j, i: (i, j)),
                             pl.BlockSpec((K * SUBLANES, tc), lambda j, i: (0, j))],
                  out_shape=[jax.ShapeDtypeStruct((L, C), F32), jax.ShapeDtypeStruct((K * SUBLANES, C), F32)],
                  compiler_params=_cparams(("parallel", "arbitrary")))(Ac, Ac, dy)


def _sconv_bwd_da(Ac, dy, dc, w, name):
    L, C3 = Ac.shape
    C = C3 // 3
    K = w.shape[0]
    tl = _conv_row_tile(L, False)
    tc = LANES
    nl = L // tl

    def body(a_ref, ap_ref, dy_ref, dc_ref, dcn_ref, w_ref, o_ref):
        i = pl.program_id(1)
        a = a_ref[...]
        ap = ap_ref[...]
        gc, h = a[:, tc:2 * tc], a[:, 2 * tc:]
        p = gc * h
        pp = jnp.where(i == 0, 0.0, ap[:, tc:2 * tc] * ap[:, 2 * tc:])
        dgb = dy_ref[...] * _conv_taps(p, pp, w_ref, K)
        cur = dc_ref[...]
        nxt = jnp.where(i == nl - 1, 0.0, dcn_ref[...])
        dp = w_ref[K - 1:K, :] * cur
        for k in range(K - 1):
            dp = dp + w_ref[k:k + 1, :] * _shift_up(cur, nxt, K - 1 - k)
        o_ref[...] = jnp.concatenate([dgb, dp * h, dp * gc], axis=1).astype(BF16)

    return _pcall(body, name=name, grid=(C // tc, nl),
                  in_specs=[pl.BlockSpec((tl, 3 * tc), lambda j, i: (i, j)),
                            pl.BlockSpec((SUBLANES, 3 * tc), _halo_prev(tl, lambda j: j)),
                            pl.BlockSpec((tl, tc), lambda j, i: (i, j)),
                            pl.BlockSpec((tl, tc), lambda j, i: (i, j)),
                            pl.BlockSpec((SUBLANES, tc), lambda j, i: (jnp.minimum((i + 1) * (tl // SUBLANES), L // SUBLANES - 1), j)),
                            pl.BlockSpec((K, tc), lambda j, i: (0, j))],
                  out_specs=pl.BlockSpec((tl, 3 * tc), lambda j, i: (i, j)),
                  out_shape=jax.ShapeDtypeStruct((L, C3), BF16),
                  compiler_params=_cparams(("parallel", "parallel")))(Ac, Ac, dy, dc, dc, w)


def _fox_gate_fwd(Af, bf, name):
    L, W = Af.shape
    tl = _pick(L, (512, 256, 128))

    def body(a_ref, b_ref, f_ref, carry):
        i = pl.program_id(0)

        @pl.when(i == 0)
        def _():
            carry[...] = jnp.zeros_like(carry)

        z = a_ref[...] + b_ref[...]
        logf = jnp.minimum(z, 0.0) - jnp.log(1.0 + jnp.exp(-jnp.abs(z)))
        f = _tri_mm(_tri(tl), logf) + carry[...]
        f_ref[...] = f
        carry[...] = f[tl - 1:tl, :]

    row = pl.BlockSpec((tl, W), lambda i: (i, 0))
    return _pcall(body, name=name, grid=(L // tl,), in_specs=[row, pl.BlockSpec((1, W), lambda i: (0, 0))], out_specs=row,
                  out_shape=jax.ShapeDtypeStruct((L, W), F32), scratch_shapes=[pltpu.VMEM((1, W), F32)],
                  compiler_params=_cparams(("arbitrary",)))(Af, bf)


def _fox_gate_bwd(Af, bf, dF, name):
    L, W = Af.shape
    tl = _pick(L, (512, 256, 128))
    nl = L // tl

    def body(a_ref, b_ref, df_ref, o_ref, db_ref, carry):
        i = pl.program_id(0)

        @pl.when(i == 0)
        def _():
            carry[...] = jnp.zeros_like(carry)
            db_ref[...] = jnp.zeros_like(db_ref)

        z = a_ref[...] + b_ref[...]
        dlogf = _tri_mm(_tri(tl, upper=True), df_ref[...]) + carry[...]
        carry[...] = dlogf[0:1, :]
        dz = dlogf * _sigmoid(-z)
        o_ref[...] = dz
        db_ref[...] += _sum8(dz)

    row = pl.BlockSpec((tl, W), lambda i: (nl - 1 - i, 0))
    return _pcall(body, name=name, grid=(nl,),
                  in_specs=[row, pl.BlockSpec((1, W), lambda i: (0, 0)), row],
                  out_specs=[row, pl.BlockSpec((SUBLANES, W), lambda i: (0, 0))],
                  out_shape=[jax.ShapeDtypeStruct((L, W), F32), jax.ShapeDtypeStruct((SUBLANES, W), F32)],
                  scratch_shapes=[pltpu.VMEM((1, W), F32)],
                  compiler_params=_cparams(("arbitrary",)))(Af, bf, dF)


def _attn_tiles(L):
    t = _pick(L, (512, 256, 128))
    return t, t


def _attn_scores(q, k, fq, fk, diag, scale):
    s = _dot(q, k, NT) * scale + (fq - fk)
    if not diag:
        return s
    row = lax.broadcasted_iota(jnp.int32, s.shape, 0)
    col = lax.broadcasted_iota(jnp.int32, s.shape, 1)
    return jnp.where(col <= row, s, NEG)


def _attn_geometry():
    Dh = FOX_HEAD_DIM
    hpt = LANES // Dh
    return Dh, hpt, FOX_HEADS // hpt


def _head_lanes(shape, Dh, hpt):
    lane = lax.broadcasted_iota(jnp.int32, shape, len(shape) - 1)
    return [(lane >= h * Dh) & (lane < (h + 1) * Dh) for h in range(hpt)]


def _attn_specs(t, L, hpt, ng):
    return dict(
        col=lambda off: pl.BlockSpec((t, LANES), lambda g, i: (i, g + off)),
        full=lambda off: pl.BlockSpec((L, LANES), lambda g, i: (0, g + off)),
        hq=pl.BlockSpec((hpt, t, 1), lambda g, i: (g, i, 0)),
        hk_full=pl.BlockSpec((hpt, 1, L), lambda g, i: (g, 0, 0)),
        hk=pl.BlockSpec((hpt, 1, t), lambda g, i: (g, 0, i)))


def _attn_fwd(qkv, Fq, Fk, name):
    L = qkv.shape[0]
    Dh, hpt, ng = _attn_geometry()
    t, _ = _attn_tiles(L)
    scale = Dh ** -0.5

    def body(q_ref, k_ref, v_ref, fq_ref, fk_ref, o_ref, lse_ref):
        qi = pl.program_id(1)
        sel = _head_lanes((t, LANES), Dh, hpt)
        q2 = q_ref[...]
        qh = [jnp.where(sel[h], q2, 0) for h in range(hpt)]
        fq = [fq_ref[h] for h in range(hpt)]

        def chunk(j, carry, diag):
            rows = pl.ds(pl.multiple_of(j * t, t), t)
            kc, vc = k_ref[rows, :], v_ref[rows, :]
            out = []
            for h in range(hpt):
                m, l, acc = carry[h]
                s = _attn_scores(qh[h], kc, fq[h], fk_ref[h, :, rows], diag, scale)
                m_new = jnp.maximum(m, jnp.max(s, axis=-1, keepdims=True))
                p = jnp.exp(s - m_new)
                a = jnp.exp(m - m_new)
                out.append((m_new, a * l + jnp.sum(p, axis=-1, keepdims=True), a * acc + _dot(p.astype(BF16), vc, NN)))
            return tuple(out)

        init = tuple((jnp.full((t, 1), NEG, F32), jnp.zeros((t, 1), F32), jnp.zeros((t, LANES), F32)) for _ in range(hpt))
        fin = chunk(qi, lax.fori_loop(0, qi, lambda j, c: chunk(j, c, False), init), True)
        o = jnp.zeros((t, LANES), F32)
        for h, (m, l, acc) in enumerate(fin):
            o = jnp.where(sel[h], acc / l, o)
            lse_ref[h] = m + jnp.log(l)
        o_ref[...] = o.astype(BF16)

    sp = _attn_specs(t, L, hpt, ng)
    return _pcall(body, name=name, grid=(ng, L // t),
                  in_specs=[sp["col"](0), sp["full"](ng), sp["full"](2 * ng), sp["hq"], sp["hk_full"]],
                  out_specs=[sp["col"](0), sp["hq"]],
                  out_shape=[jax.ShapeDtypeStruct((L, ng * LANES), BF16), jax.ShapeDtypeStruct((FOX_HEADS, L, 1), F32)],
                  compiler_params=_cparams(("parallel", "arbitrary")))(qkv, qkv, qkv, Fq, Fk)


def _attn_bwd_dq(qkv, Fq, Fk, lse, do, name):
    L = qkv.shape[0]
    Dh, hpt, ng = _attn_geometry()
    _, tk = _attn_tiles(L)
    tq = _pick(L, (256, 128))
    nkc = L // tk
    scale = Dh ** -0.5

    def body(q_ref, k_ref, v_ref, fq_ref, fk_ref, lse_ref, do_ref, dq_ref, dl_ref, p_s, dp_s):
        qi = pl.program_id(1)
        sel = _head_lanes((tq, LANES), Dh, hpt)
        q2, do2 = q_ref[...], do_ref[...]
        jd = (qi * tq) // tk
        off = qi * tq - jd * tk
        dq = jnp.zeros((tq, LANES), F32)
        for h in range(hpt):
            qh, doh = jnp.where(sel[h], q2, 0), jnp.where(sel[h], do2, 0)
            fq, lse = fq_ref[h], lse_ref[h]

            def first(j, acc, diag):
                rows = pl.ds(pl.multiple_of(j * tk, tk), tk)
                s = _dot(qh, k_ref[rows, :], NT) * scale + (fq - fk_ref[h, :, rows])
                if diag:
                    row = lax.broadcasted_iota(jnp.int32, s.shape, 0) + off
                    s = jnp.where(lax.broadcasted_iota(jnp.int32, s.shape, 1) <= row, s, NEG)
                p = jnp.exp(s - lse)
                dp = _dot(doh, v_ref[rows, :], NT)
                p_s[j] = p
                dp_s[j] = dp
                return acc + jnp.sum(p * dp, axis=-1, keepdims=True)

            delta = first(jd, lax.fori_loop(0, jd, lambda j, c: first(j, c, False), jnp.zeros((tq, 1), F32)), True)
            dl_ref[h] = delta

            def second(j, acc):
                rows = pl.ds(pl.multiple_of(j * tk, tk), tk)
                return acc + _dot((p_s[j] * (dp_s[j] - delta)).astype(BF16), k_ref[rows, :], NN)

            dq = jnp.where(sel[h], lax.fori_loop(0, jd + 1, second, jnp.zeros((tq, LANES), F32)), dq)
        dq_ref[...] = (dq * scale).astype(BF16)

    sp = _attn_specs(tq, L, hpt, ng)
    return _pcall(body, name=name, grid=(ng, L // tq),
                  in_specs=[sp["col"](0), sp["full"](ng), sp["full"](2 * ng), sp["hq"], sp["hk_full"], sp["hq"], sp["col"](0)],
                  out_specs=[sp["col"](0), sp["hq"]],
                  out_shape=[jax.ShapeDtypeStruct((L, ng * LANES), BF16), jax.ShapeDtypeStruct((FOX_HEADS, L, 1), F32)],
                  scratch_shapes=[pltpu.VMEM((nkc, tq, tk), F32), pltpu.VMEM((nkc, tq, tk), F32)],
                  compiler_params=_cparams(("parallel", "arbitrary")))(qkv, qkv, qkv, Fq, Fk, lse, do)


def _attn_bwd_dkv(qkv, rowvals, Fk, do, name):
    L = qkv.shape[0]
    Dh, hpt, ng = _attn_geometry()
    t, _ = _attn_tiles(L)
    nq = L // t
    scale = Dh ** -0.5

    def body(q_ref, k_ref, v_ref, rv_ref, fk_ref, do_ref, dk_ref, dv_ref, df_ref):
        ki = pl.program_id(1)
        sel = _head_lanes((t, LANES), Dh, hpt)
        k2, v2 = k_ref[...], v_ref[...]
        fk = [fk_ref[h] for h in range(hpt)]

        def chunk(i, carry, diag):
            dk, dv, df = carry
            rows = pl.ds(pl.multiple_of(i * t, t), t)
            q2, do2, rv = q_ref[rows, :], do_ref[rows, :], rv_ref[rows, :]
            dfs = []
            for h in range(hpt):
                qh, doh = jnp.where(sel[h], q2, 0), jnp.where(sel[h], do2, 0)
                s = _attn_scores(qh, k2, rv[:, 3 * h:3 * h + 1], fk[h], diag, scale)
                p = jnp.exp(s - rv[:, 3 * h + 1:3 * h + 2])
                ds = p * (_dot(doh, v2, NT) - rv[:, 3 * h + 2:3 * h + 3])
                dk = dk + _dot(ds.astype(BF16), qh, TN)
                dv = dv + _dot(p.astype(BF16), doh, TN)
                dfs.append(df[h] - jnp.sum(ds, axis=0, keepdims=True))
            return dk, dv, tuple(dfs)

        zero = jnp.zeros((t, LANES), F32)
        init = chunk(ki, (zero, zero, tuple(jnp.zeros((1, t), F32) for _ in range(hpt))), True)
        dk, dv, df = lax.fori_loop(ki + 1, nq, lambda i, c: chunk(i, c, False), init)
        dk_ref[...] = (dk * scale).astype(BF16)
        dv_ref[...] = dv.astype(BF16)
        for h in range(hpt):
            df_ref[h] = df[h]

    sp = _attn_specs(t, L, hpt, ng)
    return _pcall(body, name=name, grid=(ng, nq),
                  in_specs=[sp["full"](0), sp["col"](ng), sp["col"](2 * ng), pl.BlockSpec((None, L, LANES), lambda g, i: (g, 0, 0)),
                            sp["hk"], sp["full"](0)],
                  out_specs=[sp["col"](0), sp["col"](0), sp["hk"]],
                  out_shape=[jax.ShapeDtypeStruct((L, ng * LANES), BF16), jax.ShapeDtypeStruct((L, ng * LANES), BF16),
                             jax.ShapeDtypeStruct((FOX_HEADS, 1, L), F32)],
                  compiler_params=_cparams(("parallel", "arbitrary")))(qkv, qkv, qkv, rowvals, Fk, do)


def _mconv_fwd(xr, w, b, name):
    L, C = xr.shape
    K = w.shape[0]
    tl = _conv_row_tile(L, False)
    tc = _pick(C, (512, 384, 256, 128))

    def body(x_ref, xp_ref, w_ref, b_ref, o_ref):
        i = pl.program_id(1)
        prev = jnp.where(i == 0, 0.0, xp_ref[...])
        pre = _conv_taps(x_ref[...], prev, w_ref, K) + b_ref[...]
        o_ref[...] = pre * _sigmoid(pre)

    return _pcall(body, name=name, grid=(C // tc, L // tl),
                  in_specs=[pl.BlockSpec((tl, tc), lambda j, i: (i, j)),
                            pl.BlockSpec((SUBLANES, tc), _halo_prev(tl, lambda j: j)),
                            pl.BlockSpec((K, tc), lambda j, i: (0, j)),
                            pl.BlockSpec((1, tc), lambda j, i: (0, j))],
                  out_specs=pl.BlockSpec((tl, tc), lambda j, i: (i, j)),
                  out_shape=jax.ShapeDtypeStruct((L, C), F32),
                  compiler_params=_cparams(("parallel", "parallel")))(xr, xr, w, b.reshape(1, C))


def _mconv_bwd(xr, dact, w, b, name):
    L, C = xr.shape
    K = w.shape[0]
    tl = _conv_row_tile(L, True)
    tc = _pick(C, (512, 384, 256, 128))
    nl = L // tl

    def body(x_ref, xp_ref, xn_ref, d_ref, dn_ref, w_ref, b_ref, o_ref, dw_ref, db_ref):
        i = pl.program_id(1)
        cur = x_ref[...]
        prev = jnp.where(i == 0, 0.0, xp_ref[...])

        def at_conv_out(x, xprev, d):
            pre = _conv_taps(x, xprev, w_ref, K) + b_ref[...]
            sg = _sigmoid(pre)
            return d * sg * (1.0 + pre * (1.0 - sg))

        dpre = at_conv_out(cur, prev, d_ref[...])
        dpre_n = jnp.where(i == nl - 1, 0.0, at_conv_out(xn_ref[...], cur[tl - SUBLANES:], dn_ref[...]))
        o_ref[...] = _conv_taps_t(dpre, dpre_n, w_ref, K).astype(BF16)

        @pl.when(i == 0)
        def _():
            dw_ref[...] = jnp.zeros_like(dw_ref)
            db_ref[...] = jnp.zeros_like(db_ref)

        db_ref[...] += _sum8(dpre)
        for k in range(K):
            dw_ref[k * SUBLANES:(k + 1) * SUBLANES, :] += _sum8(dpre * _shift_down(cur, prev, K - 1 - k))

    return _pcall(body, name=name, grid=(C // tc, nl),
                  in_specs=[pl.BlockSpec((tl, tc), lambda j, i: (i, j)),
                            pl.BlockSpec((SUBLANES, tc), _halo_prev(tl, lambda j: j)),
                            pl.BlockSpec((SUBLANES, tc), _halo_next(tl, L, SUBLANES)),
                            pl.BlockSpec((tl, tc), lambda j, i: (i, j)),
                            pl.BlockSpec((SUBLANES, tc), _halo_next(tl, L, SUBLANES)),
                            pl.BlockSpec((K, tc), lambda j, i: (0, j)),
                            pl.BlockSpec((1, tc), lambda j, i: (0, j))],
                  out_specs=[pl.BlockSpec((tl, tc), lambda j, i: (i, j)),
                             pl.BlockSpec((K * SUBLANES, tc), lambda j, i: (0, j)),
                             pl.BlockSpec((SUBLANES, tc), lambda j, i: (0, j))],
                  out_shape=[jax.ShapeDtypeStruct((L, C), BF16), jax.ShapeDtypeStruct((K * SUBLANES, C), F32),
                             jax.ShapeDtypeStruct((SUBLANES, C), F32)],
                  compiler_params=_cparams(("parallel", "arbitrary")))(xr, xr, xr, dact, dact, w, b.reshape(1, C))


def _head_selector(R, P, heads_first):
    shape = (R, R * P) if heads_first else (R * P, R)
    head = lax.broadcasted_iota(jnp.int32, shape, 0 if heads_first else 1)
    lane = lax.broadcasted_iota(jnp.int32, shape, 1 if heads_first else 0)
    d = lane - head * P
    return jnp.where((d >= 0) & (d < P), 1.0, 0.0).astype(BF16)


def _ssd_prelude(dtc_ref, dtr_ref, bc_ref, br_ref, ac_ref, ar_ref, Q, R, P):
    raw_c = dtc_ref[...] + bc_ref[...]
    dt_c = _softplus(raw_c)
    dt_r = _softplus(dtr_ref[...] + br_ref[...])
    A_c = -jnp.exp(ac_ref[...])
    acs_c = _tri_mm(_tri(Q), dt_c * A_c)
    acs_r = _tri_mm(_tri(Q, upper=True), dt_r * (-jnp.exp(ar_ref[...])), tri_first=False)
    ea_c = jnp.exp(acs_c)
    dte_c = jnp.exp(acs_c[Q - 1:Q, :] - acs_c)
    wide = _tri_mm(_head_selector(R, P, True), jnp.concatenate([dt_c, ea_c, dte_c], axis=0), tri_first=False)
    return dict(raw_c=raw_c, dt_c=dt_c, A_c=A_c, acs_c=acs_c, acs_r=acs_r, ea_c=ea_c,
                DT=wide[:Q], EA=wide[Q:2 * Q], DTE=wide[2 * Q:])


def _ssd_decay_tile(pre, r, mask):
    return jnp.exp(jnp.where(mask, pre["acs_c"][:, r:r + 1] - pre["acs_r"][r:r + 1, :], NEG))


def _ssd_specs(Q, R, P, N, G, inner, rev=None):
    cc = (lambda c: c) if rev is None else rev
    return dict(
        x=pl.BlockSpec((Q, R * P), lambda g, c: (cc(c), g)),
        b=pl.BlockSpec((Q, N), lambda g, c: (cc(c), inner // N + g)),
        c=pl.BlockSpec((Q, N), lambda g, c: (cc(c), inner // N + G + g)),
        dtc=pl.BlockSpec((None, Q, R), lambda g, c: (g, cc(c), 0)),
        dtr=pl.BlockSpec((None, R, Q), lambda g, c: (g, 0, cc(c))),
        pc=pl.BlockSpec((None, 1, R), lambda g, c: (g, 0, 0)),
        pr=pl.BlockSpec((None, R, 1), lambda g, c: (g, 0, 0)),
        px=pl.BlockSpec((None, 1, R * P), lambda g, c: (g, 0, 0)),
        st=pl.BlockSpec((None, None, N, R * P), lambda g, c: (cc(c), g, 0, 0)))


def _ssd_fwd(act, dtc, dtr, bias_c, bias_r, alog_c, alog_r, dsk_x, name):
    G, L, R = dtc.shape
    N, P, Q = SSM_STATE, SSM_HEAD_DIM, SSM_CHUNK
    RP = R * P
    inner = G * RP
    nc = L // Q

    def body(x_ref, b_ref, c_ref, dtc_ref, dtr_ref, bc_ref, br_ref, ac_ref, ar_ref, dk_ref, y_ref, hp_ref, st):
        c = pl.program_id(1)

        @pl.when(c == 0)
        def _():
            st[...] = jnp.zeros_like(st)

        pre = _ssd_prelude(dtc_ref, dtr_ref, bc_ref, br_ref, ac_ref, ar_ref, Q, R, P)
        X = x_ref[...]
        XT = X * pre["DT"]
        Bb = b_ref[...].astype(BF16)
        Cb = c_ref[...].astype(BF16)
        CB = _dot(Cb, Bb, NT)
        mask = lax.broadcasted_iota(jnp.int32, (Q, Q), 0) >= lax.broadcasted_iota(jnp.int32, (Q, Q), 1)
        low = lax.broadcasted_iota(jnp.int32, (Q, 2 * P), 1) < P
        pieces = []
        for k in range(R // 2):
            xt2 = XT[:, 2 * P * k:2 * P * (k + 1)]
            acc = None
            for half in range(2):
                Gm = CB * _ssd_decay_tile(pre, 2 * k + half, mask)
                part = _dot(Gm.astype(BF16), jnp.where(low == (half == 0), xt2, 0.0).astype(BF16), NN)
                acc = part if acc is None else acc + part
            pieces.append(acc)
        HP = st[...]
        hp_ref[...] = HP
        yoff = pre["EA"] * _dot(Cb, HP.astype(BF16), NN)
        st[...] = HP * pre["EA"][Q - 1:Q, :] + _dot(Bb, (XT * pre["DTE"]).astype(BF16), TN)
        y_ref[...] = jnp.concatenate(pieces, axis=1) + yoff + dk_ref[...] * X

    sp = _ssd_specs(Q, R, P, N, G, inner)
    return _pcall(body, name=name, grid=(G, nc),
                  in_specs=[sp["x"], sp["b"], sp["c"], sp["dtc"], sp["dtr"], sp["pc"], sp["pr"], sp["pc"], sp["pr"], sp["px"]],
                  out_specs=[sp["x"], sp["st"]],
                  out_shape=[jax.ShapeDtypeStruct((L, inner), F32), jax.ShapeDtypeStruct((nc, G, N, RP), F32)],
                  scratch_shapes=[pltpu.VMEM((N, RP), F32)],
                  compiler_params=_cparams(("parallel", "arbitrary")))(act, act, act, dtc, dtr, bias_c, bias_r, alog_c, alog_r, dsk_x)


def _ssd_bwd(act, dtc, dtr, bias_c, bias_r, alog_c, alog_r, dsk_x, hprev, dy, name):
    G, L, R = dtc.shape
    N, P, Q = SSM_STATE, SSM_HEAD_DIM, SSM_CHUNK
    RP = R * P
    inner = G * RP
    nc = L // Q

    def body(x_ref, b_ref, c_ref, dtc_ref, dtr_ref, bc_ref, br_ref, ac_ref, ar_ref, dk_ref, hp_ref, dy_ref,
             dx_ref, db_ref, dc_ref, ddt_ref, gbias_ref, galog_ref, gdsk_ref, dst):
        c = pl.program_id(1)

        @pl.when(c == 0)
        def _():
            dst[...] = jnp.zeros_like(dst)
            gbias_ref[...] = jnp.zeros_like(gbias_ref)
            galog_ref[...] = jnp.zeros_like(galog_ref)
            gdsk_ref[...] = jnp.zeros_like(gdsk_ref)

        pre = _ssd_prelude(dtc_ref, dtr_ref, bc_ref, br_ref, ac_ref, ar_ref, Q, R, P)
        DT, EA, DTE = pre["DT"], pre["EA"], pre["DTE"]
        E_END = EA[Q - 1:Q, :]
        X, DY = x_ref[...], dy_ref[...]
        XT = X * DT
        Bb = b_ref[...].astype(BF16)
        Cb = c_ref[...].astype(BF16)
        CB = _dot(Cb, Bb, NT)
        HP, dH = hp_ref[...], dst[...]
        HPb, dHb = HP.astype(BF16), dH.astype(BF16)
        EDY = EA * DY
        EDYb = EDY.astype(BF16)
        dC = _dot(EDYb, HPb, NT)
        dHP = _dot(Cb, EDYb, TN)
        da_off = EDY * _dot(Cb, HPb, NN)
        Z = _dot(Bb, dHb, NN)
        XD = XT * DTE
        dB = _dot(XD.astype(BF16), dHb, NT)
        dXT = DTE * Z
        t_x = XD * Z
        hh = jnp.sum(dH * HP, axis=0, keepdims=True) * E_END
        dst[...] = dHP + dH * E_END
        mask = lax.broadcasted_iota(jnp.int32, (Q, Q), 0) >= lax.broadcasted_iota(jnp.int32, (Q, Q), 1)
        eye = lax.broadcasted_iota(jnp.int32, (Q, Q), 0) == lax.broadcasted_iota(jnp.int32, (Q, Q), 1)
        low = lax.broadcasted_iota(jnp.int32, (Q, 2 * P), 1) < P
        lane = lax.broadcasted_iota(jnp.int32, (Q, R), 1)
        dCB = jnp.zeros((Q, Q), F32)
        da_mat = jnp.zeros((Q, R), F32)
        pieces = []
        for k in range(R // 2):
            sl = slice(2 * P * k, 2 * P * (k + 1))
            xt2, dy2 = XT[:, sl], DY[:, sl]
            acc = None
            for half in range(2):
                r = 2 * k + half
                sel = low == (half == 0)
                Lm = _ssd_decay_tile(pre, r, mask)
                Gm = CB * Lm
                dyb = jnp.where(sel, dy2, 0.0).astype(BF16)
                part = _dot(Gm.astype(BF16), dyb, TN)
                acc = part if acc is None else acc + part
                dG = jnp.where(mask, _dot(dyb, jnp.where(sel, xt2, 0.0).astype(BF16), NT), 0.0)
                Mm = dG * Gm
                dCB = dCB + dG * Lm
                colsum = jnp.sum(jnp.where(eye, jnp.sum(Mm, axis=0, keepdims=True), 0.0), axis=1, keepdims=True)
                da_mat = jnp.where(lane == r, jnp.sum(Mm, axis=1, keepdims=True) - colsum, da_mat)
            pieces.append(acc)
        dXT = dXT + jnp.concatenate(pieces, axis=1)
        dCBb = dCB.astype(BF16)
        dc_ref[...] = dC + _dot(dCBb, Bb, NN)
        db_ref[...] = dB + _dot(dCBb, Cb, TN)
        dx_ref[...] = dXT * DT + dk_ref[...] * DY
        pad = jnp.zeros((SUBLANES - 1, RP), F32)
        sums = _tri_mm(_head_selector(R, P, False), jnp.concatenate([da_off, t_x, dXT * X, DY * X, hh, pad], axis=0), tri_first=False)
        t = sums[Q:2 * Q]
        da_end = jnp.sum(t, axis=0, keepdims=True) + sums[4 * Q:4 * Q + 1]
        rowi = lax.broadcasted_iota(jnp.int32, (Q, R), 0)
        da_mat = da_mat + sums[:Q] - t + jnp.where(rowi == Q - 1, da_end, 0.0)
        ddtA = _tri_mm(_tri(Q, upper=True), da_mat)
        ddt_raw = (ddtA * pre["A_c"] + sums[2 * Q:3 * Q]) * _sigmoid(pre["raw_c"])
        ddt_ref[...] = ddt_raw
        gbias_ref[...] += jnp.sum(ddt_raw, axis=0, keepdims=True)
        galog_ref[...] += jnp.sum(ddtA * pre["dt_c"], axis=0, keepdims=True) * pre["A_c"]
        gdsk_ref[...] += jnp.sum(sums[3 * Q:4 * Q], axis=0, keepdims=True)

    sp = _ssd_specs(Q, R, P, N, G, inner, rev=lambda c: nc - 1 - c)
    bout = pl.BlockSpec((Q, N), lambda g, c: (nc - 1 - c, g))
    return _pcall(body, name=name, grid=(G, nc),
                  in_specs=[sp["x"], sp["b"], sp["c"], sp["dtc"], sp["dtr"], sp["pc"], sp["pr"], sp["pc"], sp["pr"], sp["px"],
                            sp["st"], sp["x"]],
                  out_specs=[sp["x"], bout, bout, sp["dtc"], sp["pc"], sp["pc"], sp["pc"]],
                  out_shape=[jax.ShapeDtypeStruct((L, inner), F32), jax.ShapeDtypeStruct((L, G * N), F32),
                             jax.ShapeDtypeStruct((L, G * N), F32), jax.ShapeDtypeStruct((G, L, R), F32),
                             jax.ShapeDtypeStruct((G, 1, R), F32), jax.ShapeDtypeStruct((G, 1, R), F32),
                             jax.ShapeDtypeStruct((G, 1, R), F32)],
                  scratch_shapes=[pltpu.VMEM((N, RP), F32)],
                  compiler_params=_cparams(("parallel", "arbitrary")))(
        act, act, act, dtc, dtr, bias_c, bias_r, alog_c, alog_r, dsk_x, hprev, dy)


def _gnorm_fwd(y, z, g, name):
    L, Dn = y.shape
    gs = Dn // SSM_GROUPS
    tl = _row_tile(L)

    def body(y_ref, z_ref, g_ref, o_ref):
        for k in range(SSM_GROUPS):
            sl = slice(k * gs, (k + 1) * gs)
            zz = z_ref[:, sl]
            u = y_ref[:, sl] * zz * _sigmoid(zz)
            rstd = lax.rsqrt(jnp.mean(u * u, axis=-1, keepdims=True) + RMS_EPS)
            o_ref[:, sl] = (u * rstd * g_ref[:, sl]).astype(BF16)

    row = pl.BlockSpec((tl, Dn), lambda i: (i, 0))
    return _pcall(body, name=name, grid=(L // tl,), in_specs=[row, row, pl.BlockSpec((1, Dn), lambda i: (0, 0))],
                  out_specs=row, out_shape=jax.ShapeDtypeStruct((L, Dn), BF16),
                  compiler_params=_cparams(("parallel",)))(y, z, g.reshape(1, Dn))


def _gnorm_bwd(y, z, g, dout, name):
    L, Dn = y.shape
    gs = Dn // SSM_GROUPS
    tl = _row_tile(L)

    def body(y_ref, z_ref, g_ref, d_ref, dy_ref, dz_ref, dg_ref):
        i = pl.program_id(0)

        @pl.when(i == 0)
        def _():
            dg_ref[...] = jnp.zeros_like(dg_ref)

        for k in range(SSM_GROUPS):
            sl = slice(k * gs, (k + 1) * gs)
            zz = z_ref[:, sl]
            yy = y_ref[:, sl]
            sg = _sigmoid(zz)
            sil = zz * sg
            u = yy * sil
            rstd = lax.rsqrt(jnp.mean(u * u, axis=-1, keepdims=True) + RMS_EPS)
            n = u * rstd
            d = d_ref[:, sl]
            dn = d * g_ref[:, sl]
            du = rstd * (dn - n * jnp.mean(dn * n, axis=-1, keepdims=True))
            dy_ref[:, sl] = du * sil
            dz_ref[:, sl] = (du * yy * sg * (1.0 + zz * (1.0 - sg))).astype(BF16)
            dg_ref[:, sl] += _sum8(d * n)

    row = pl.BlockSpec((tl, Dn), lambda i: (i, 0))
    return _pcall(body, name=name, grid=(L // tl,), in_specs=[row, row, pl.BlockSpec((1, Dn), lambda i: (0, 0)), row],
                  out_specs=[row, row, pl.BlockSpec((SUBLANES, Dn), lambda i: (0, 0))],
                  out_shape=[jax.ShapeDtypeStruct((L, Dn), F32), jax.ShapeDtypeStruct((L, Dn), BF16),
                             jax.ShapeDtypeStruct((SUBLANES, Dn), F32)],
                  compiler_params=_cparams(("arbitrary",)))(y, z, g.reshape(1, Dn), dout)


def _adamw(w, g, m, v, name):
    rows, W = w.shape
    tr = _pick(rows, (512, 256, 128, 64, 32, 16, 8))
    c1 = 1.0 / (1.0 - ADAM_B1 ** ADAM_STEP)
    c2 = 1.0 / (1.0 - ADAM_B2 ** ADAM_STEP)

    def body(w_ref, g_ref, m_ref, v_ref, d_ref, nm_ref, nv_ref):
        g_ = g_ref[...]
        nm = ADAM_B1 * m_ref[...] + (1.0 - ADAM_B1) * g_
        nv = ADAM_B2 * v_ref[...] + (1.0 - ADAM_B2) * (g_ * g_)
        nm_ref[...] = nm
        nv_ref[...] = nv
        d_ref[...] = -ADAM_LR * ((nm * c1) / (jnp.sqrt(nv * c2) + ADAM_EPS) + ADAM_WD * w_ref[...])

    blk = pl.BlockSpec((tr, W), lambda i: (i, 0))
    return _pcall(body, name=name, grid=(rows // tr,), in_specs=[blk] * 4, out_specs=[blk] * 3,
                  out_shape=[jax.ShapeDtypeStruct((rows, W), F32)] * 3, compiler_params=_cparams(("parallel",)))(w, g, m, v)


def _sum_slots(x, name, extra=None):
    n, rows, W = x.shape
    tr = _pick(rows, (512, 256, 128, 64, 32, 16, 8))
    has_extra = extra is not None

    def body(*refs):
        if has_extra:
            e_ref, x_ref, o_ref = refs
            acc = e_ref[...].astype(F32)
            start = 0
        else:
            x_ref, o_ref = refs
            acc = x_ref[0].astype(F32)
            start = 1
        for s in range(start, n):
            acc = acc + x_ref[s].astype(F32)
        o_ref[...] = acc

    blk = pl.BlockSpec((tr, W), lambda i: (i, 0))
    xblk = pl.BlockSpec((n, tr, W), lambda i: (0, i, 0))
    return _pcall(body, name=name, grid=(rows // tr,), in_specs=([blk] if has_extra else []) + [xblk], out_specs=blk,
                  out_shape=jax.ShapeDtypeStruct((rows, W), F32), compiler_params=_cparams(("parallel",)))(
        *(([extra] if has_extra else []) + [x]))


def _add_pairs(a, b, name):
    n, rows, W = a.shape
    tr = _pick(rows, (512, 256, 128, 64, 32, 16, 8))

    def body(a_ref, b_ref, o_ref):
        o_ref[...] = (a_ref[...].astype(F32) + b_ref[...].astype(F32)).astype(BF16)

    blk = pl.BlockSpec((None, tr, W), lambda s, i: (s, i, 0))
    return _pcall(body, name=name, grid=(n, rows // tr), in_specs=[blk, blk], out_specs=blk,
                  out_shape=jax.ShapeDtypeStruct((n, rows, W), BF16), compiler_params=_cparams(("parallel", "parallel")))(a, b)


MESH = pl.DeviceIdType.MESH
HBM_SPEC = pl.BlockSpec(memory_space=pl.ANY)


def _me():
    return lax.axis_index("x"), lax.axis_index("y"), lax.axis_index("c")


def _all_gather(arrs, name):
    n = len(arrs)

    def body(*refs):
        ins, outs = refs[:n], refs[n:2 * n]
        send_sems, recv_sems, local_sems = refs[2 * n:]
        x, y, c = _me()
        me, sib = (x, y, c), (x, y, 1 - c)
        chips = [(1 - x, y), (x, 1 - y), (1 - x, 1 - y)]

        def slot(a, dev):
            return outs[a].at[4 * dev[0] + 2 * dev[1] + dev[2]]

        def copy(a, k, block, to, src=None):
            return pltpu.make_async_remote_copy(src_ref=slot(a, block) if src is None else src, dst_ref=slot(a, block),
                                                send_sem=send_sems.at[a * 7 + k], recv_sem=recv_sems.at[a * 7 + k],
                                                device_id=to, device_id_type=MESH)

        mine = [pltpu.make_async_copy(ins[a], slot(a, me), local_sems.at[a]) for a in range(n)]
        for cp in mine:
            cp.start()
        first = []
        for a in range(n):
            first.append(copy(a, 0, me, sib, src=ins[a]))
            first += [copy(a, 1 + j, me, (*chip, c), src=ins[a]) for j, chip in enumerate(chips)]
        for cp in first:
            cp.start()
        passed = []
        for j, chip in enumerate(chips):
            for a in range(n):
                copy(a, 1 + j, (*chip, c), me).wait_recv()
                fw = copy(a, 4 + j, (*chip, c), sib)
                fw.start()
                passed.append(fw)
        for a in range(n):
            copy(a, 0, sib, me).wait_recv()
            for j, chip in enumerate(chips):
                copy(a, 4 + j, (*chip, 1 - c), me).wait_recv()
        for cp in first + passed:
            cp.wait_send()
        for cp in mine:
            cp.wait()

    return _pcall(body, name=name, in_specs=[HBM_SPEC] * n, out_specs=[HBM_SPEC] * n,
                  out_shape=[jax.ShapeDtypeStruct((N_DEV,) + a.shape, a.dtype) for a in arrs],
                  scratch_shapes=[pltpu.SemaphoreType.DMA((7 * n,)), pltpu.SemaphoreType.DMA((7 * n,)),
                                  pltpu.SemaphoreType.DMA((n,))])(*arrs)


def _rs_sibling(gs, name):
    n = len(gs)

    def body(*refs):
        g_refs, o_refs = refs[:n], refs[n:2 * n]
        send_sems, recv_sems = refs[2 * n:]
        x, y, c = _me()
        sib = (x, y, 1 - c)
        cps = [pltpu.make_async_remote_copy(src_ref=g_refs[a].at[2 * q + (1 - c)], dst_ref=o_refs[a].at[q],
                                            send_sem=send_sems.at[4 * a + q], recv_sem=recv_sems.at[4 * a + q],
                                            device_id=sib, device_id_type=MESH) for a in range(n) for q in range(4)]
        for cp in cps:
            cp.start()
        for cp in cps:
            cp.wait()

    return _pcall(body, name=name, in_specs=[HBM_SPEC] * n, out_specs=[HBM_SPEC] * n,
                  out_shape=[jax.ShapeDtypeStruct((4,) + g.shape[1:], g.dtype) for g in gs],
                  scratch_shapes=[pltpu.SemaphoreType.DMA((4 * n,)), pltpu.SemaphoreType.DMA((4 * n,))])(*gs)


def _rs_chips(ps, name):
    n = len(ps)

    def body(*refs):
        p_refs, o_refs = refs[:n], refs[n:2 * n]
        send_sems, recv_sems = refs[2 * n:]
        x, y, c = _me()
        chips = [(1 - x, y), (x, 1 - y), (1 - x, 1 - y)]
        cps = [pltpu.make_async_remote_copy(src_ref=p_refs[a].at[2 * chip[0] + chip[1]], dst_ref=o_refs[a].at[j],
                                            send_sem=send_sems.at[3 * a + j], recv_sem=recv_sems.at[3 * a + j],
                                            device_id=(*chip, c), device_id_type=MESH)
               for j, chip in enumerate(chips) for a in range(n)]
        for cp in cps:
            cp.start()
        for cp in cps:
            cp.wait()

    return _pcall(body, name=name, in_specs=[HBM_SPEC] * n, out_specs=[HBM_SPEC] * n,
                  out_shape=[jax.ShapeDtypeStruct((3,) + p.shape[1:], p.dtype) for p in ps],
                  scratch_shapes=[pltpu.SemaphoreType.DMA((3 * n,)), pltpu.SemaphoreType.DMA((3 * n,))])(*ps)


def _reduce_scatter(gs, name):
    x, y, c = _me()
    from_sib = _rs_sibling(gs, name + "_sib")
    pairs = []
    for a, (g, fs) in enumerate(zip(gs, from_sib)):
        own = g.reshape((4, 2) + g.shape[1:])
        pairs.append(_add_pairs(jnp.where(c == 0, own[:, 0], own[:, 1]), fs, f"{name}_pair{a}"))
    from_chips = _rs_chips(pairs, name + "_chips")
    return [_sum_slots(fc, f"{name}_sum{a}", extra=lax.dynamic_index_in_dim(p, 2 * x + y, axis=0, keepdims=False))
            for a, (p, fc) in enumerate(zip(pairs, from_chips))]


BIG = ("even_w_in", "even_w_out", "odd_w_in", "odd_w_out", "ffn_w_up", "ffn_w_down", "ple_w_proj", "ple_w_gate")
SMALL_SHARDED = ("even_conv_w", "odd_conv_w", "odd_conv_b", "odd_norm_g", "ffn_conv_w")
REPLICATED = ("even_b_f", "odd_dt_bias", "odd_a_log", "odd_d_skip", "ln_mix_g", "ln_mix_b", "ffn_conv_b",
              "ln_ffn_g", "ln_ffn_b", "ple_b_gate")
WEIGHTS = ("even_w_in", "even_b_f", "even_conv_w", "even_w_out", "odd_w_in", "odd_conv_w", "odd_conv_b", "odd_dt_bias",
           "odd_a_log", "odd_d_skip", "odd_norm_g", "odd_w_out", "ln_mix_g", "ln_mix_b", "ffn_w_up", "ffn_conv_w",
           "ffn_conv_b", "ffn_w_down", "ln_ffn_g", "ln_ffn_b", "ple_w_proj", "ple_w_gate", "ple_b_gate")


def _full_shapes():
    d = _dims()
    return {
        "even_w_in": ((1, D_MODEL, d["even_in"]), 2), "even_b_f": ((1, FOX_HEADS), None),
        "even_conv_w": ((1, CONV_WIDTH, CONV_DIM), 2), "even_w_out": ((1, d["even_mix"], D_MODEL), 1),
        "odd_w_in": ((1, D_MODEL, d["odd_in"]), 2), "odd_conv_w": ((1, SSM_CONV_WIDTH, d["conv_ch"]), 2),
        "odd_conv_b": ((1, d["conv_ch"]), 1), "odd_dt_bias": ((1, d["ssm_heads"]), None),
        "odd_a_log": ((1, d["ssm_heads"]), None), "odd_d_skip": ((1, d["ssm_heads"]), None),
        "odd_norm_g": ((1, d["ssm_inner"]), 1), "odd_w_out": ((1, d["ssm_inner"], D_MODEL), 1),
        "ln_mix_g": ((DEPTH, D_MODEL), None), "ln_mix_b": ((DEPTH, D_MODEL), None),
        "ffn_w_up": ((DEPTH, D_MODEL, 2 * D_FF), 2), "ffn_conv_w": ((DEPTH, FFN_CONV_WIDTH, 2 * D_FF), 2),
        "ffn_conv_b": ((DEPTH, 2 * D_FF), None), "ffn_w_down": ((DEPTH, D_FF, D_MODEL), 1),
        "ln_ffn_g": ((DEPTH, D_MODEL), None), "ln_ffn_b": ((DEPTH, D_MODEL), None),
        "ple_w_proj": ((DEPTH, PLE_DIM, D_MODEL), 2), "ple_w_gate": ((DEPTH, D_MODEL, D_MODEL), 1),
        "ple_b_gate": ((DEPTH, D_MODEL), None),
    }


def _shard_shape(name):
    shape, ax = _full_shapes()[name]
    if ax is None:
        return shape
    return tuple(s // N_DEV if i == ax else s for i, s in enumerate(shape))


def _as2d(a, lead=0):
    return a.reshape(a.shape[:lead] + (-1, a.shape[-1]))


def _part_rows(shape):
    n = int(np.prod(shape))
    return -(-(-(-n // PACK_W)) // SUBLANES) * SUBLANES


def _pack_small(parts):
    out = []
    for p in parts:
        n, rows = int(np.prod(p.shape)), _part_rows(p.shape)
        out.append(jnp.pad(p.reshape(-1).astype(F32), (0, rows * PACK_W - n)).reshape(rows, PACK_W))
    return jnp.concatenate(out, axis=0)


def _unpack_small(pack, shapes):
    lead = pack.shape[:-2]
    out, off = [], 0
    for s in shapes:
        n, rows = int(np.prod(s)), _part_rows(s)
        part = pack[..., off:off + rows, :].reshape(lead + (-1,))[..., :n]
        out.append(part.reshape(lead + tuple(s)))
        off += rows
    return out


def _assemble(gathered, name):
    shape, ax = _full_shapes()[name]
    return jnp.moveaxis(gathered, 0, ax).reshape(shape)


def _split_dest(full, name):
    shape, ax = _full_shapes()[name]
    sh = shape[:ax] + (N_DEV, shape[ax] // N_DEV) + shape[ax + 1:]
    return jnp.moveaxis(full.reshape(sh), ax, 0)


def _interleave_cols(w, parts, tc):
    C = w.shape[-1] // parts
    sh = w.shape[:-1]
    return w.reshape(sh + (parts, C // tc, tc)).swapaxes(-3, -2).reshape(sh + (parts * C,))


def _deinterleave_cols(w, parts, tc):
    C = w.shape[-1] // parts
    sh = w.shape[:-1]
    return w.reshape(sh + (C // tc, parts, tc)).swapaxes(-3, -2).reshape(sh + (parts * C,))


def _ffn_tc():
    return _pick(D_FF, (256, 128))


def _heads_first(a, heads):
    L = a.shape[0]
    return a.reshape(L, heads, -1).transpose(1, 0, 2)


def _heads_last(a):
    h, L, d = a.shape
    return a.transpose(1, 0, 2).reshape(L, h * d)


def _pad_cols(a, to):
    return jnp.pad(a, ((0, 0), (0, to - a.shape[1])))


def _tail_fwd(i, h_in, mix, p_i, W, sp):
    r1, h1, h1b = _ln_fwd(h_in, mix, sp["ln_mix_g"][i], sp["ln_mix_b"][i], f"ln_mix_fwd{i}")
    U = _mm(h1b, W["ffn_up"][i], "nn", F32, f"ffn_up{i}")
    S = _ffn_act_fwd(U, sp["ffn_conv_w_il"][i], sp["ffn_conv_b_il"][i], f"ffn_act_fwd{i}")
    ffn = _mm(S, W["ffn_down"][i], "nn", F32, f"ffn_down{i}")
    r2, h2, h2b = _ln_fwd(h1, ffn, sp["ln_ffn_g"][i], sp["ln_ffn_b"][i], f"ln_ffn_fwd{i}")
    G = _mm(h2b, W["ple_gate"][i], "nn", F32, f"ple_gate{i}")
    E = _mm(p_i, W["ple_proj"][i], "nn", F32, f"ple_proj{i}")
    h3, h3b = _ple_fwd(h2, G, sp["ple_b_gate"][i], E, f"ple_fwd{i}")
    return h3, h3b, dict(r1=r1, h1b=h1b, U=U, S=S, r2=r2, h2b=h2b, G=G, E=E, p=p_i)


def _tail_bwd(i, dh3, sv, W, sp, grads):
    alpha = _alpha()
    dE, dGp, dbg = _ple_bwd(dh3, sv["G"], sp["ple_b_gate"][i], sv["E"], f"ple_bwd{i}")
    grads["ple_b_gate"][i] = dbg.sum(0)
    grads["ple_w_proj"][i] = _mm(sv["p"], dE, "tn", F32, f"d_ple_proj{i}")
    grads["ple_w_gate"][i] = _mm(sv["h2b"], dGp, "tn", F32, f"d_ple_gate{i}")
    dh2 = _mm(dGp, W["ple_gate"][i], "nt", F32, f"dx_ple_gate{i}", add=dh3)
    dr2, dr2b, dg, db = _ln_bwd(sv["r2"], dh2, sp["ln_ffn_g"][i], f"ln_ffn_bwd{i}")
    grads["ln_ffn_g"][i], grads["ln_ffn_b"][i] = dg.sum(0), db.sum(0)
    grads["ffn_w_down"][i] = _mm(sv["S"], dr2b, "tn", F32, f"d_ffn_down{i}")
    dS = _mm(dr2b, W["ffn_down"][i], "nt", BF16, f"dx_ffn_down{i}")
    dU, dcw, dcb = _ffn_act_bwd(sv["U"], dS, sp["ffn_conv_w_il"][i], sp["ffn_conv_b_il"][i], f"ffn_act_bwd{i}")
    K = FFN_CONV_WIDTH
    tc = _ffn_tc()
    grads["ffn_conv_w"][i] = _deinterleave_cols(dcw.reshape(K, SUBLANES, -1).sum(1), 2, tc)
    grads["ffn_conv_b"][i] = _deinterleave_cols(dcb.sum(0), 2, tc)
    grads["ffn_w_up"][i] = _deinterleave_cols(_mm(sv["h1b"], dU, "tn", F32, f"d_ffn_up{i}"), 2, tc)
    dh1 = _mm(dU, W["ffn_up"][i], "nt", F32, f"dx_ffn_up{i}", add=dr2, add_scale=alpha)
    dr1, dr1b, dg, db = _ln_bwd(sv["r1"], dh1, sp["ln_mix_g"][i], f"ln_mix_bwd{i}")
    grads["ln_mix_g"][i], grads["ln_mix_b"][i] = dg.sum(0), db.sum(0)
    return dr1, dr1b


def _even_fwd(h, W, sp):
    L = h.shape[0]
    H, Dh = FOX_HEADS, FOX_HEAD_DIM
    Ac = _mm(h, W["even_in_conv"], "nn", F32, "even_in_conv")
    qkv = _mm(h, W["even_in_qkv"], "nn", BF16, "even_in_qkv")
    Af = _mm(h, W["even_in_f"], "nn", F32, "even_in_f")
    y_a = _sconv_fwd(Ac, sp["even_conv_w_il"], "sconv_fwd")
    Fc = _fox_gate_fwd(Af, sp["even_b_f_pad"], "fox_gate_fwd")
    Fh = Fc[:, :H].T
    Fq, Fk = Fh.reshape(H, L, 1), Fh.reshape(H, 1, L)
    o, lse = _attn_fwd(qkv, Fq, Fk, "attn_fwd")
    Y = jnp.concatenate([y_a, o], axis=1)
    mix = _mm(Y, W["even_out"], "nn", F32, "even_out")
    return mix, dict(h=h, Ac=Ac, Af=Af, qkv=qkv, Fq=Fq, Fk=Fk, lse=lse, Y=Y)


def _even_bwd(dmix, dres, sv, W, sp, grads):
    H, Dh = FOX_HEADS, FOX_HEAD_DIM
    C = CONV_DIM
    L = dmix.shape[0]
    grads["even_w_out"][0] = _mm(sv["Y"], dmix, "tn", F32, "d_even_out")
    dY = _mm(dmix, W["even_out"], "nt", F32, "dx_even_out")
    dya = dY[:, :C]
    do = dY[:, C:].astype(BF16)
    dc, dcw = _sconv_bwd_dc(sv["Ac"], dya, sp["even_conv_w_il"], "sconv_bwd_dc")
    grads["even_conv_w"][0] = dcw.reshape(CONV_WIDTH, SUBLANES, -1).sum(1)
    dAc = _sconv_bwd_da(sv["Ac"], dya, dc, sp["even_conv_w_il"], "sconv_bwd_da")
    dq, delta = _attn_bwd_dq(sv["qkv"], sv["Fq"], sv["Fk"], sv["lse"], do, "attn_bwd_dq")
    _, hpt, ng = _attn_geometry()
    rowvals = jnp.concatenate([sv["Fq"], sv["lse"], delta], axis=-1).reshape(ng, hpt, L, 3).transpose(0, 2, 1, 3)
    rowvals = jnp.pad(rowvals.reshape(ng, L, 3 * hpt), ((0, 0), (0, 0), (0, LANES - 3 * hpt)))
    dk, dv, dFk = _attn_bwd_dkv(sv["qkv"], rowvals, sv["Fk"], do, "attn_bwd_dkv")
    dqkv = jnp.concatenate([dq, dk, dv], axis=1)
    dF = _pad_cols(dFk.reshape(H, L).T, LANES)
    dAf, dbf = _fox_gate_bwd(sv["Af"], sp["even_b_f_pad"], dF, "fox_gate_bwd")
    grads["even_b_f"][0] = dbf.sum(0)[:H]
    h = sv["h"]
    gc = _deinterleave_cols(_mm(h, dAc, "tn", F32, "d_even_in_conv"), 3, LANES)
    gq = _mm(h, dqkv, "tn", F32, "d_even_in_qkv")
    gf = _mm(h, dAf, "tn", F32, "d_even_in_f")[:, :H]
    grads["even_w_in"][0] = jnp.concatenate([gc, gq, gf], axis=1)
    dh = _mm(dAc, W["even_in_conv"], "nt", F32, "dx_even_in_conv", add=dres, add_scale=_alpha())
    dh = _mm(dqkv, W["even_in_qkv"], "nt", F32, "dx_even_in_qkv", add=dh)
    dh = _mm(dAf, W["even_in_f"], "nt", F32, "dx_even_in_f", add=dh)
    return dh


def _group_layouts(v, G):
    R = v.shape[0] // G
    return v.reshape(G, 1, R), v.reshape(G, R, 1)


def _odd_fwd(h, W, sp):
    d = _dims()
    L = h.shape[0]
    Hs, G, N, P = d["ssm_heads"], SSM_GROUPS, SSM_STATE, SSM_HEAD_DIM
    R = Hs // G
    inner = d["ssm_inner"]
    z = _mm(h, W["odd_in_z"], "nn", F32, "odd_in_z")
    xr = _mm(h, W["odd_in_x"], "nn", F32, "odd_in_x")
    dtp = _mm(h, W["odd_in_dt"], "nn", F32, "odd_in_dt")
    act = _mconv_fwd(xr, sp["odd_conv_w"], sp["odd_conv_b"], "mconv_fwd")
    dtg = dtp[:, :Hs].reshape(L, G, R)
    dtc, dtr = dtg.transpose(1, 0, 2), dtg.transpose(1, 2, 0)
    dsk_x = jnp.repeat(sp["odd_d_skip"], P).reshape(G, 1, R * P)
    ssd_in = (act, dtc, dtr) + _group_layouts(sp["odd_dt_bias"], G) + _group_layouts(sp["odd_a_log"], G) + (dsk_x,)
    y, hprev = _ssd_fwd(*ssd_in, "ssd_fwd")
    u = _gnorm_fwd(y, z, sp["odd_norm_g"], "gnorm_fwd")
    mix = _mm(u, W["odd_out"], "nn", F32, "odd_out")
    return mix, dict(h=h, z=z, xr=xr, ssd_in=ssd_in, hprev=hprev, y=y, u=u)


def _odd_bwd(dmix, dres, sv, W, sp, grads):
    d = _dims()
    L = dmix.shape[0]
    Hs, G, N, P = d["ssm_heads"], SSM_GROUPS, SSM_STATE, SSM_HEAD_DIM
    grads["odd_w_out"][0] = _mm(sv["u"], dmix, "tn", F32, "d_odd_out")
    du = _mm(dmix, W["odd_out"], "nt", F32, "dx_odd_out")
    dy, dz, dg = _gnorm_bwd(sv["y"], sv["z"], sp["odd_norm_g"], du, "gnorm_bwd")
    grads["odd_norm_g"][0] = dg.sum(0)
    dxs, dB, dC, ddt, gbias, galog, gdsk = _ssd_bwd(*sv["ssd_in"], sv["hprev"], dy, "ssd_bwd")
    grads["odd_dt_bias"][0] = gbias.reshape(Hs)
    grads["odd_a_log"][0] = galog.reshape(Hs)
    grads["odd_d_skip"][0] = gdsk.reshape(Hs)
    dact = jnp.concatenate([dxs, dB, dC], axis=1)
    dxr, dcw, dcb = _mconv_bwd(sv["xr"], dact, sp["odd_conv_w"], sp["odd_conv_b"], "mconv_bwd")
    grads["odd_conv_w"][0] = dcw.reshape(SSM_CONV_WIDTH, SUBLANES, -1).sum(1)
    grads["odd_conv_b"][0] = dcb.sum(0)
    ddtp = _pad_cols(ddt.transpose(1, 0, 2).reshape(L, Hs), W["odd_in_dt"].shape[1])
    h = sv["h"]
    gz = _mm(h, dz, "tn", F32, "d_odd_in_z")
    gx = _mm(h, dxr, "tn", F32, "d_odd_in_x")
    gdt = _mm(h, ddtp, "tn", F32, "d_odd_in_dt")[:, :Hs]
    grads["odd_w_in"][0] = jnp.concatenate([gz, gx, gdt], axis=1)
    dh = _mm(dz, W["odd_in_z"], "nt", F32, "dx_odd_in_z", add=dres, add_scale=_alpha())
    dh = _mm(dxr, W["odd_in_x"], "nt", F32, "dx_odd_in_x", add=dh)
    dh = _mm(ddtp, W["odd_in_dt"], "nt", F32, "dx_odd_in_dt", add=dh)
    return dh


def _prepare_weights(full):
    d = _dims()
    C, fd, H = CONV_DIM, d["fox_dim"], FOX_HEADS
    tc = _ffn_tc()
    W, sp = {}, {}
    ew = full["even_w_in"][0]
    W["even_in_conv"] = _interleave_cols(ew[:, :3 * C], 3, LANES)
    W["even_in_qkv"] = ew[:, 3 * C:3 * C + 3 * fd]
    W["even_in_f"] = _pad_cols(ew[:, 3 * C + 3 * fd:], LANES)
    W["even_out"] = full["even_w_out"][0]
    ow = full["odd_w_in"][0]
    inner, cch, Hs = d["ssm_inner"], d["conv_ch"], d["ssm_heads"]
    W["odd_in_z"] = ow[:, :inner]
    W["odd_in_x"] = ow[:, inner:inner + cch]
    W["odd_in_dt"] = _pad_cols(ow[:, inner + cch:], -(-Hs // LANES) * LANES)
    W["odd_out"] = full["odd_w_out"][0]
    W["ffn_up"] = [_interleave_cols(full["ffn_w_up"][i], 2, tc) for i in range(DEPTH)]
    W["ffn_down"] = [full["ffn_w_down"][i] for i in range(DEPTH)]
    W["ple_proj"] = [full["ple_w_proj"][i] for i in range(DEPTH)]
    W["ple_gate"] = [full["ple_w_gate"][i] for i in range(DEPTH)]
    sp["even_conv_w_il"] = full["even_conv_w"][0]
    sp["even_b_f_pad"] = _pad_cols(full["even_b_f"], LANES)
    sp["odd_conv_w"] = full["odd_conv_w"][0]
    sp["odd_conv_b"] = full["odd_conv_b"][0]
    sp["odd_norm_g"] = full["odd_norm_g"][0]
    for n in ("odd_dt_bias", "odd_a_log", "odd_d_skip"):
        sp[n] = full[n][0]
    for n in ("ln_mix_g", "ln_mix_b", "ln_ffn_g", "ln_ffn_b", "ple_b_gate"):
        sp[n] = full[n]
    sp["ffn_conv_w_il"] = [_interleave_cols(full["ffn_conv_w"][i], 2, tc) for i in range(DEPTH)]
    sp["ffn_conv_b_il"] = [_interleave_cols(full["ffn_conv_b"][i], 2, tc) for i in range(DEPTH)]
    return W, sp


def _local_step(x, p, target, full):
    W, sp = _prepare_weights(full)
    grads = {n: [None] * _full_shapes()[n][0][0] for n in WEIGHTS}
    pb = p.astype(BF16)
    mix0, sv_e = _even_fwd(x.astype(BF16), W, sp)
    h3_0, h3_0b, sv_t0 = _tail_fwd(0, x, mix0, pb[0], W, sp)
    mix1, sv_o = _odd_fwd(h3_0b, W, sp)
    h3_1, _, sv_t1 = _tail_fwd(1, h3_0, mix1, pb[1], W, sp)
    dh, sq = _loss_head(h3_1, target, "loss_head")
    dr1, dr1b = _tail_bwd(1, dh, sv_t1, W, sp, grads)
    dh = _odd_bwd(dr1b, dr1, sv_o, W, sp, grads)
    dr1, dr1b = _tail_bwd(0, dh, sv_t0, W, sp, grads)
    dx = _even_bwd(dr1b, dr1, sv_e, W, sp, grads)
    grads = {n: jnp.stack(v) for n, v in grads.items()}
    return jnp.sum(sq), dx, grads


def kernel(x, p, even_w_in, even_b_f, even_conv_w, even_w_out, odd_w_in, odd_conv_w, odd_conv_b, odd_dt_bias, odd_a_log, odd_d_skip, odd_norm_g, odd_w_out, ln_mix_g, ln_mix_b, ffn_w_up, ffn_conv_w, ffn_conv_b, ffn_w_down, ln_ffn_g, ln_ffn_b, ple_w_proj, ple_w_gate, ple_b_gate, loss_target, m_even_w_in, m_even_b_f, m_even_conv_w, m_even_w_out, m_odd_w_in, m_odd_conv_w, m_odd_conv_b, m_odd_dt_bias, m_odd_a_log, m_odd_d_skip, m_odd_norm_g, m_odd_w_out, m_ln_mix_g, m_ln_mix_b, m_ffn_w_up, m_ffn_conv_w, m_ffn_conv_b, m_ffn_w_down, m_ln_ffn_g, m_ln_ffn_b, m_ple_w_proj, m_ple_w_gate, m_ple_b_gate, v_even_w_in, v_even_b_f, v_even_conv_w, v_even_w_out, v_odd_w_in, v_odd_conv_w, v_odd_conv_b, v_odd_dt_bias, v_odd_a_log, v_odd_d_skip, v_odd_norm_g, v_odd_w_out, v_ln_mix_g, v_ln_mix_b, v_ffn_w_up, v_ffn_conv_w, v_ffn_conv_b, v_ffn_w_down, v_ln_ffn_g, v_ln_ffn_b, v_ple_w_proj, v_ple_w_gate, v_ple_b_gate):
    args = locals()
    w = {n: args[n] for n in WEIGHTS}
    m = {n: args["m_" + n] for n in WEIGHTS}
    v = {n: args["v_" + n] for n in WEIGHTS}
    me = 4 * lax.axis_index("x") + 2 * lax.axis_index("y") + lax.axis_index("c")

    gathered = _all_gather([_as2d(w[n]).astype(BF16) for n in BIG] + [_pack_small([w[n] for n in SMALL_SHARDED])], "ag_weights")
    full = dict(w)
    for n, g in zip(BIG, gathered[:-1]):
        full[n] = _assemble(g.reshape((N_DEV,) + _shard_shape(n)), n)
    for n, g in zip(SMALL_SHARDED, _unpack_small(gathered[-1], [_shard_shape(n) for n in SMALL_SHARDED])):
        full[n] = _assemble(g, n)

    sq, dx, grads = _local_step(x[0], p[:, 0], loss_target[0], full)
    loss = lax.psum(0.5 * sq / D_MODEL, ("x", "y", "c"))

    gsum_big = _reduce_scatter([_as2d(_split_dest(grads[n], n), 1).astype(BF16) for n in BIG], "rs_grads")
    g_final = {n: g.reshape(_shard_shape(n)) for n, g in zip(BIG, gsum_big)}
    small_names = SMALL_SHARDED + REPLICATED
    (small_all,) = _all_gather([_pack_small([grads[n] for n in small_names])], "ag_small_grads")
    small_sum = _sum_slots(small_all, "sum_small_grads")
    for n, g in zip(small_names, _unpack_small(small_sum, [_full_shapes()[n][0] for n in small_names])):
        g_final[n] = lax.dynamic_index_in_dim(_split_dest(g, n), me, axis=0, keepdims=False) if n in SMALL_SHARDED else g

    out = {}
    for n in BIG:
        res = _adamw(*[_as2d(t[n]) for t in (w, g_final, m, v)], "adamw_" + n)
        out[n] = [r.reshape(_shard_shape(n)) for r in res]
    shapes = [_shard_shape(n) for n in small_names]
    res = _adamw(*[_pack_small([t[n] for n in small_names]) for t in (w, g_final, m, v)], "adamw_small")
    for n, d_, m_, v_ in zip(small_names, *[_unpack_small(r, shapes) for r in res]):
        out[n] = [d_, m_, v_]
    return (loss, dx[None], *[g_final[n] for n in WEIGHTS], *[out[n][0] for n in WEIGHTS],
            *[out[n][1] for n in WEIGHTS], *[out[n][2] for n in WEIGHTS])
```

```python
import jax
import jax.numpy as jnp
import numpy as np
from jax import lax
from jax.experimental import pallas as pl
from jax.experimental.pallas import tpu as pltpu

D_MODEL = 1024
SEQ = 8192
DEPTH = 2
CONV_DIM = 512
CONV_WIDTH = 3
FOX_HEADS = 8
FOX_HEAD_DIM = 64
SSM_HEAD_DIM = 64
SSM_GROUPS = 4
SSM_STATE = 128
SSM_CONV_WIDTH = 4
SSM_CHUNK = 128
D_FF = 2816
FFN_CONV_WIDTH = 3
PLE_DIM = 256
LN_EPS = 1e-5
RMS_EPS = 1e-5
ADAM_LR = 0.001
ADAM_B1 = 0.9
ADAM_B2 = 0.999
ADAM_EPS = 1e-08
ADAM_WD = 0.01
ADAM_STEP = 10
N_DEV = 8

F32 = jnp.float32
BF16 = jnp.bfloat16
NEG = -1e30
LANES = 128
SUBLANES = 8
PACK_W = 1024
VMEM_LIMIT = 48 * 1024 * 1024
ATTN_BWD_VMEM_LIMIT = 56 * 1024 * 1024


def _dims():
    fox_dim = FOX_HEADS * FOX_HEAD_DIM
    ssm_inner = 2 * D_MODEL
    ssm_heads = ssm_inner // SSM_HEAD_DIM
    conv_ch = ssm_inner + 2 * SSM_GROUPS * SSM_STATE
    return dict(fox_dim=fox_dim, even_in=3 * CONV_DIM + 3 * fox_dim + FOX_HEADS, even_mix=CONV_DIM + fox_dim,
                ssm_inner=ssm_inner, ssm_heads=ssm_heads, conv_ch=conv_ch, odd_in=ssm_inner + conv_ch + ssm_heads)


def _alpha():
    return (2.0 * DEPTH) ** 0.25


def _pick(dim, prefs):
    for p in prefs:
        if dim % p == 0:
            return p
    return dim


def _pcall(body, **kw):
    return pl.pallas_call(body, **kw)


def _cparams(sem=None, **kw):
    if sem is not None:
        kw["dimension_semantics"] = sem
    return pltpu.CompilerParams(vmem_limit_bytes=VMEM_LIMIT, **kw)


def _sigmoid(x):
    return 1.0 / (1.0 + jnp.exp(-x))


def _softplus(x):
    return jnp.maximum(x, 0.0) + jnp.log(1.0 + jnp.exp(-jnp.abs(x)))


def _sum8(x):
    n, c = x.shape
    return x.reshape(n // SUBLANES, SUBLANES, c).sum(axis=0)


def _dot(a, b, dims):
    return lax.dot_general(a, b, (dims, ((), ())), preferred_element_type=F32)


NN = ((1,), (0,))
NT = ((1,), (1,))
TN = ((0,), (0,))


def _split3(x):
    hi = x.astype(BF16)
    r1 = x - hi.astype(F32)
    mid = r1.astype(BF16)
    lo = (r1 - mid.astype(F32)).astype(BF16)
    return hi, mid, lo


def _tri_mm(tri_bf16, x, tri_first=True):
    if tri_first:
        return sum(_dot(tri_bf16, part, NN) for part in _split3(x))
    return sum(_dot(part, tri_bf16, NN) for part in _split3(x))


def _tri(n, upper=False):
    r = lax.broadcasted_iota(jnp.int32, (n, n), 0)
    c = lax.broadcasted_iota(jnp.int32, (n, n), 1)
    return jnp.where((r <= c) if upper else (r >= c), 1.0, 0.0).astype(BF16)


def _shift_down(cur, prev8, k):
    if k == 0:
        return cur
    ext = jnp.concatenate([prev8, cur], axis=0)
    return pltpu.roll(ext, k, axis=0)[SUBLANES:]


def _shift_up(cur, next8, k):
    if k == 0:
        return cur
    n = cur.shape[0]
    ext = jnp.concatenate([cur, next8], axis=0)
    return pltpu.roll(ext, n + SUBLANES - k, axis=0)[:n]


def _mm(a, b, mode, out_dtype, name, add=None, add_scale=1.0, b_k_start=0):
    if mode == "nn":
        (M, K), (K2, N) = a.shape, b.shape
    elif mode == "nt":
        (M, K), N = a.shape, b.shape[0]
        K2 = K if b.shape[1] >= b_k_start + K else None
    else:
        (K, M), (K2, N) = a.shape, b.shape
    assert K == K2, (a.shape, b.shape, mode)
    tm = _pick(M, (1024, 512, 256, 128))
    tn = _pick(N, (1408, 1024, 768, 512, 384, 256, 128))
    tk = K if K <= 2048 and b_k_start % K == 0 else _pick(K, (1408, 1024, 768, 512, 256, 128))
    assert b_k_start % tk == 0
    k0 = b_k_start // tk
    nk = K // tk
    dims = {"nn": NN, "nt": NT, "tn": TN}[mode]
    a_spec = pl.BlockSpec((tk, tm), lambda i, j, k: (k, i)) if mode == "tn" else pl.BlockSpec((tm, tk), lambda i, j, k: (i, k))
    b_spec = pl.BlockSpec((tn, tk), lambda i, j, k: (j, k + k0)) if mode == "nt" else pl.BlockSpec((tk, tn), lambda i, j, k: (k, j))
    o_spec = pl.BlockSpec((tm, tn), lambda i, j, k: (i, j))
    has_add = add is not None

    def body(*refs):
        a_ref, b_ref = refs[:2]
        add_ref = refs[2] if has_add else None
        o_ref = refs[2 + has_add]
        prod = _dot(a_ref[...].astype(BF16), b_ref[...].astype(BF16), dims)
        if nk == 1:
            if has_add:
                prod = prod + add_scale * add_ref[...].astype(F32)
            o_ref[...] = prod.astype(out_dtype)
            return
        acc = refs[3 + has_add]
        k = pl.program_id(2)

        @pl.when(k == 0)
        def _():
            if has_add:
                acc[...] = prod + add_scale * add_ref[...].astype(F32)
            else:
                acc[...] = prod

        @pl.when(k > 0)
        def _():
            acc[...] += prod

        @pl.when(k == nk - 1)
        def _():
            o_ref[...] = acc[...].astype(out_dtype)

    ins = [a, b] + ([add] if has_add else [])
    specs = [a_spec, b_spec] + ([o_spec] if has_add else [])
    return _pcall(body, name=name, grid=(M // tm, N // tn, nk), in_specs=specs, out_specs=o_spec,
                  out_shape=jax.ShapeDtypeStruct((M, N), out_dtype),
                  scratch_shapes=[pltpu.VMEM((tm, tn), F32)] if nk > 1 else [],
                  compiler_params=_cparams(("parallel", "parallel", "arbitrary")))(*ins)


def _row_tile(L):
    return _pick(L, (256, 128))


def _conv_row_tile(L, backward):
    return _pick(L, (512, 256, 128)) if backward else _pick(L, (1024, 512, 256, 128))


def _ln_fwd(h, mix, g, b, name):
    L, D = h.shape
    tl = _row_tile(L)
    alpha = _alpha()

    def body(h_ref, m_ref, g_ref, b_ref, r_ref, y_ref, yb_ref):
        r = alpha * h_ref[...] + m_ref[...]
        mu = jnp.mean(r, axis=-1, keepdims=True)
        xc = r - mu
        var = jnp.mean(xc * xc, axis=-1, keepdims=True)
        r_ref[...] = r
        y = xc * lax.rsqrt(var + LN_EPS) * g_ref[...] + b_ref[...]
        y_ref[...] = y
        yb_ref[...] = y.astype(BF16)

    row = pl.BlockSpec((tl, D), lambda i: (i, 0))
    vec = pl.BlockSpec((1, D), lambda i: (0, 0))
    return _pcall(body, name=name, grid=(L // tl,), in_specs=[row, row, vec, vec], out_specs=[row, row, row],
                  out_shape=[jax.ShapeDtypeStruct((L, D), F32)] * 2 + [jax.ShapeDtypeStruct((L, D), BF16)],
                  compiler_params=_cparams(("parallel",)))(h, mix, g.reshape(1, D), b.reshape(1, D))


def _ln_bwd(r, dy, g, name):
    L, D = r.shape
    tl = _row_tile(L)

    def body(r_ref, dy_ref, g_ref, dr_ref, drb_ref, dg_ref, db_ref):
        i = pl.program_id(0)
        r_ = r_ref[...]
        dy_ = dy_ref[...]
        mu = jnp.mean(r_, axis=-1, keepdims=True)
        xc = r_ - mu
        rstd = lax.rsqrt(jnp.mean(xc * xc, axis=-1, keepdims=True) + LN_EPS)
        xhat = xc * rstd
        dxh = dy_ * g_ref[...]
        dr = rstd * (dxh - jnp.mean(dxh, axis=-1, keepdims=True) - xhat * jnp.mean(dxh * xhat, axis=-1, keepdims=True))
        dr_ref[...] = dr
        drb_ref[...] = dr.astype(BF16)

        @pl.when(i == 0)
        def _():
            dg_ref[...] = jnp.zeros_like(dg_ref)
            db_ref[...] = jnp.zeros_like(db_ref)

        dg_ref[...] += _sum8(dy_ * xhat)
        db_ref[...] += _sum8(dy_)

    row = pl.BlockSpec((tl, D), lambda i: (i, 0))
    vec = pl.BlockSpec((1, D), lambda i: (0, 0))
    acc = pl.BlockSpec((SUBLANES, D), lambda i: (0, 0))
    return _pcall(body, name=name, grid=(L // tl,), in_specs=[row, row, vec], out_specs=[row, row, acc, acc],
                  out_shape=[jax.ShapeDtypeStruct((L, D), F32), jax.ShapeDtypeStruct((L, D), BF16),
                             jax.ShapeDtypeStruct((SUBLANES, D), F32), jax.ShapeDtypeStruct((SUBLANES, D), F32)],
                  compiler_params=_cparams(("arbitrary",)))(r, dy, g.reshape(1, D))


def _ple_fwd(h2, G, bg, E, name):
    L, D = h2.shape
    tl = _row_tile(L)

    def body(h_ref, g_ref, b_ref, e_ref, o_ref, ob_ref):
        o = h_ref[...] + _sigmoid(g_ref[...] + b_ref[...]) * e_ref[...]
        o_ref[...] = o
        ob_ref[...] = o.astype(BF16)

    row = pl.BlockSpec((tl, D), lambda i: (i, 0))
    vec = pl.BlockSpec((1, D), lambda i: (0, 0))
    return _pcall(body, name=name, grid=(L // tl,), in_specs=[row, row, vec, row], out_specs=[row, row],
                  out_shape=[jax.ShapeDtypeStruct((L, D), F32), jax.ShapeDtypeStruct((L, D), BF16)],
                  compiler_params=_cparams(("parallel",)))(h2, G, bg.reshape(1, D), E)


def _ple_bwd(dh3, G, bg, E, name):
    L, D = dh3.shape
    tl = _row_tile(L)

    def body(d_ref, g_ref, b_ref, e_ref, de_ref, dg_ref, db_ref):
        i = pl.program_id(0)
        d = d_ref[...]
        sg = _sigmoid(g_ref[...] + b_ref[...])
        de_ref[...] = (d * sg).astype(BF16)
        dgp = d * e_ref[...] * sg * (1.0 - sg)
        dg_ref[...] = dgp.astype(BF16)

        @pl.when(i == 0)
        def _():
            db_ref[...] = jnp.zeros_like(db_ref)

        db_ref[...] += _sum8(dgp)

    row = pl.BlockSpec((tl, D), lambda i: (i, 0))
    vec = pl.BlockSpec((1, D), lambda i: (0, 0))
    acc = pl.BlockSpec((SUBLANES, D), lambda i: (0, 0))
    return _pcall(body, name=name, grid=(L // tl,), in_specs=[row, row, vec, row], out_specs=[row, row, acc],
                  out_shape=[jax.ShapeDtypeStruct((L, D), BF16), jax.ShapeDtypeStruct((L, D), BF16),
                             jax.ShapeDtypeStruct((SUBLANES, D), F32)],
                  compiler_params=_cparams(("arbitrary",)))(dh3, G, bg.reshape(1, D), E)


def _loss_head(h, target, name):
    L, D = h.shape
    tl = _row_tile(L)

    def body(h_ref, t_ref, d_ref, s_ref):
        i = pl.program_id(0)
        e = h_ref[...] - t_ref[...]
        d_ref[...] = e * (1.0 / D)

        @pl.when(i == 0)
        def _():
            s_ref[...] = jnp.zeros_like(s_ref)

        s_ref[...] += _sum8(e * e)

    row = pl.BlockSpec((tl, D), lambda i: (i, 0))
    acc = pl.BlockSpec((SUBLANES, D), lambda i: (0, 0))
    return _pcall(body, name=name, grid=(L // tl,), in_specs=[row, row], out_specs=[row, acc],
                  out_shape=[jax.ShapeDtypeStruct((L, D), F32), jax.ShapeDtypeStruct((SUBLANES, D), F32)],
                  compiler_params=_cparams(("arbitrary",)))(h, target)


def _halo_prev(tl, ncol_blocks_fn):
    return lambda j, i: (jnp.maximum(i * (tl // SUBLANES) - 1, 0), ncol_blocks_fn(j))


def _conv_taps(cur, prev, w_ref, K):
    acc = w_ref[K - 1:K, :] * cur
    for k in range(K - 1):
        acc = acc + w_ref[k:k + 1, :] * _shift_down(cur, prev, K - 1 - k)
    return acc


def _ffn_act_fwd(U, w, b, name):
    L, F2 = U.shape
    F = F2 // 2
    K = w.shape[0]
    tl = _conv_row_tile(L, False)
    tc = _pick(F, (256, 128))

    nj = F // tc

    def body(ug_ref, uv_ref, ugp_ref, uvp_ref, wg_ref, wv_ref, bg_ref, bv_ref, s_ref):
        i = pl.program_id(1)
        g = _conv_taps(ug_ref[...], jnp.where(i == 0, 0.0, ugp_ref[...]), wg_ref, K) + bg_ref[...]
        v = _conv_taps(uv_ref[...], jnp.where(i == 0, 0.0, uvp_ref[...]), wv_ref, K) + bv_ref[...]
        s_ref[...] = (g * _sigmoid(g) * v).astype(BF16)

    def both(shape, index):
        return [pl.BlockSpec(shape, lambda j, i: index(j, i)), pl.BlockSpec(shape, lambda j, i: index(j + nj, i))]

    b2 = b.reshape(1, F2)
    return _pcall(body, name=name, grid=(nj, L // tl),
                  in_specs=both((tl, tc), lambda j, i: (i, j)) + both((SUBLANES, tc), _halo_prev(tl, lambda j: j))
                  + both((K, tc), lambda j, i: (0, j)) + both((1, tc), lambda j, i: (0, j)),
                  out_specs=pl.BlockSpec((tl, tc), lambda j, i: (i, j)),
                  out_shape=jax.ShapeDtypeStruct((L, F), BF16),
                  compiler_params=_cparams(("parallel", "parallel")))(U, U, U, U, w, w, b2, b2)


def _halo_next(tl, L, rows):
    return lambda j, i: (jnp.minimum((i + 1) * (tl // rows), L // rows - 1), j)


def _conv_taps_t(cur, nxt, w_ref, K):
    acc = w_ref[K - 1:K, :] * cur
    for k in range(K - 1):
        acc = acc + w_ref[k:k + 1, :] * _shift_up(cur, nxt, K - 1 - k)
    return acc


BF16_ROWS = 16


def _ffn_act_bwd(U, dS, w, b, name):
    L, F2 = U.shape
    F = F2 // 2
    K = w.shape[0]
    tl = _conv_row_tile(L, True)
    tc = _pick(F, (256, 128))
    nl = L // tl

    nj = F // tc

    def body(ug_ref, uv_ref, ugp_ref, uvp_ref, ugn_ref, uvn_ref, ds_ref, dsn_ref, wg_ref, wv_ref, bg_ref, bv_ref,
             dug_ref, duv_ref, dwg_ref, dwv_ref, dbg_ref, dbv_ref):
        i = pl.program_id(1)
        cur = (ug_ref[...], uv_ref[...])
        prev = (jnp.where(i == 0, 0.0, ugp_ref[...]), jnp.where(i == 0, 0.0, uvp_ref[...]))

        def at_conv_out(x, xprev, ds):
            g = _conv_taps(x[0], xprev[0], wg_ref, K) + bg_ref[...]
            v = _conv_taps(x[1], xprev[1], wv_ref, K) + bv_ref[...]
            sg = _sigmoid(g)
            return ds * v * sg * (1.0 + g * (1.0 - sg)), ds * g * sg

        duc = at_conv_out(cur, prev, ds_ref[...].astype(F32))
        duc_n = at_conv_out((ugn_ref[...], uvn_ref[...]), (cur[0][tl - SUBLANES:], cur[1][tl - SUBLANES:]),
                            dsn_ref[...].astype(F32)[:SUBLANES])

        @pl.when(i == 0)
        def _():
            for r in (dwg_ref, dwv_ref, dbg_ref, dbv_ref):
                r[...] = jnp.zeros_like(r)

        halves = ((dug_ref, dwg_ref, dbg_ref, wg_ref), (duv_ref, dwv_ref, dbv_ref, wv_ref))
        for half, (du_ref, dw_ref, db_ref, w_ref) in enumerate(halves):
            d = duc[half]
            du_ref[...] = _conv_taps_t(d, jnp.where(i == nl - 1, 0.0, duc_n[half]), w_ref, K).astype(BF16)
            db_ref[...] += _sum8(d)
            for k in range(K):
                dw_ref[k * SUBLANES:(k + 1) * SUBLANES, :] += _sum8(d * _shift_down(cur[half], prev[half], K - 1 - k))

    def both(shape, index):
        return [pl.BlockSpec(shape, lambda j, i: index(j, i)), pl.BlockSpec(shape, lambda j, i: index(j + nj, i))]

    b2 = b.reshape(1, F2)
    du_specs, du_shapes = [pl.BlockSpec((tl, tc), lambda j, i: (i, j))] * 2, [jax.ShapeDtypeStruct((L, F), BF16)] * 2
    dw_specs = [pl.BlockSpec((K * SUBLANES, tc), lambda j, i: (0, j))] * 2
    dw_shapes = [jax.ShapeDtypeStruct((K * SUBLANES, F), F32)] * 2
    db_specs, db_shapes = [pl.BlockSpec((SUBLANES, tc), lambda j, i: (0, j))] * 2, [jax.ShapeDtypeStruct((SUBLANES, F), F32)] * 2
    return _pcall(body, name=name, grid=(nj, nl),
                  in_specs=both((tl, tc), lambda j, i: (i, j)) + both((SUBLANES, tc), _halo_prev(tl, lambda j: j))
                  + both((SUBLANES, tc), _halo_next(tl, L, SUBLANES))
                  + [pl.BlockSpec((tl, tc), lambda j, i: (i, j)), pl.BlockSpec((BF16_ROWS, tc), _halo_next(tl, L, BF16_ROWS))]
                  + both((K, tc), lambda j, i: (0, j)) + both((1, tc), lambda j, i: (0, j)),
                  out_specs=du_specs + dw_specs + db_specs, out_shape=du_shapes + dw_shapes + db_shapes,
                  compiler_params=_cparams(("parallel", "arbitrary")))(U, U, U, U, U, U, dS, dS, w, w, b2, b2)


def _sconv_fwd(Ac, w, name):
    L, C3 = Ac.shape
    C = C3 // 3
    K = w.shape[0]
    tl = _conv_row_tile(L, False)
    tc = LANES

    def body(a_ref, ap_ref, w_ref, y_ref):
        i = pl.program_id(1)
        a = a_ref[...]
        ap = ap_ref[...]
        p = a[:, tc:2 * tc] * a[:, 2 * tc:]
        pp = jnp.where(i == 0, 0.0, ap[:, tc:2 * tc] * ap[:, 2 * tc:])
        y_ref[...] = (a[:, :tc] * _conv_taps(p, pp, w_ref, K)).astype(BF16)

    return _pcall(body, name=name, grid=(C // tc, L // tl),
                  in_specs=[pl.BlockSpec((tl, 3 * tc), lambda j, i: (i, j)),
                            pl.BlockSpec((SUBLANES, 3 * tc), _halo_prev(tl, lambda j: j)),
                            pl.BlockSpec((K, tc), lambda j, i: (0, j))],
                  out_specs=pl.BlockSpec((tl, tc), lambda j, i: (i, j)),
                  out_shape=jax.ShapeDtypeStruct((L, C), BF16),
                  compiler_params=_cparams(("parallel", "parallel")))(Ac, Ac, w)


def _sconv_bwd_dc(Ac, dy, w, name):
    L, C3 = Ac.shape
    C = C3 // 3
    K = w.shape[0]
    tl = _conv_row_tile(L, False)
    tc = LANES

    def body(a_ref, ap_ref, dy_ref, dc_ref, dw_ref):
        i = pl.program_id(1)
        a = a_ref[...]
        ap = ap_ref[...]
        p = a[:, tc:2 * tc] * a[:, 2 * tc:]
        pp = jnp.where(i == 0, 0.0, ap[:, tc:2 * tc] * ap[:, 2 * tc:])
        dc = dy_ref[...] * a[:, :tc]
        dc_ref[...] = dc

        @pl.when(i == 0)
        def _():
            dw_ref[...] = jnp.zeros_like(dw_ref)

        for k in range(K):
            dw_ref[k * SUBLANES:(k + 1) * SUBLANES, :] += _sum8(dc * _shift_down(p, pp, K - 1 - k))

    return _pcall(body, name=name, grid=(C // tc, L // tl),
                  in_specs=[pl.BlockSpec((tl, 3 * tc), lambda j, i: (i, j)),
                            pl.BlockSpec((SUBLANES, 3 * tc), _halo_prev(tl, lambda j: j)),
                            pl.BlockSpec((tl, tc), lambda j, i: (i, j))],
                  out_specs=[pl.BlockSpec((tl, tc), lambda j, i: (i, j)),
                             pl.BlockSpec((K * SUBLANES, tc), lambda j, i: (0, j))],
                  out_shape=[jax.ShapeDtypeStruct((L, C), F32), jax.ShapeDtypeStruct((K * SUBLANES, C), F32)],
                  compiler_params=_cparams(("parallel", "arbitrary")))(Ac, Ac, dy)


def _sconv_bwd_da(Ac, dy, dc, w, name):
    L, C3 = Ac.shape
    C = C3 // 3
    K = w.shape[0]
    tl = _conv_row_tile(L, False)
    tc = LANES
    nl = L // tl

    def body(a_ref, ap_ref, dy_ref, dc_ref, dcn_ref, w_ref, o_ref):
        i = pl.program_id(1)
        a = a_ref[...]
        ap = ap_ref[...]
        gc, h = a[:, tc:2 * tc], a[:, 2 * tc:]
        p = gc * h
        pp = jnp.where(i == 0, 0.0, ap[:, tc:2 * tc] * ap[:, 2 * tc:])
        dgb = dy_ref[...] * _conv_taps(p, pp, w_ref, K)
        cur = dc_ref[...]
        nxt = jnp.where(i == nl - 1, 0.0, dcn_ref[...])
        dp = w_ref[K - 1:K, :] * cur
        for k in range(K - 1):
            dp = dp + w_ref[k:k + 1, :] * _shift_up(cur, nxt, K - 1 - k)
        o_ref[...] = jnp.concatenate([dgb, dp * h, dp * gc], axis=1).astype(BF16)

    return _pcall(body, name=name, grid=(C // tc, nl),
                  in_specs=[pl.BlockSpec((tl, 3 * tc), lambda j, i: (i, j)),
                            pl.BlockSpec((SUBLANES, 3 * tc), _halo_prev(tl, lambda j: j)),
                            pl.BlockSpec((tl, tc), lambda j, i: (i, j)),
                            pl.BlockSpec((tl, tc), lambda j, i: (i, j)),
                            pl.BlockSpec((SUBLANES, tc), lambda j, i: (jnp.minimum((i + 1) * (tl // SUBLANES), L // SUBLANES - 1), j)),
                            pl.BlockSpec((K, tc), lambda j, i: (0, j))],
                  out_specs=pl.BlockSpec((tl, 3 * tc), lambda j, i: (i, j)),
                  out_shape=jax.ShapeDtypeStruct((L, C3), BF16),
                  compiler_params=_cparams(("parallel", "parallel")))(Ac, Ac, dy, dc, dc, w)


def _fox_gate_fwd(Af, bf, name):
    L, W = Af.shape
    tl = _pick(L, (512, 256, 128))

    def body(a_ref, b_ref, f_ref, carry):
        i = pl.program_id(0)

        @pl.when(i == 0)
        def _():
            carry[...] = jnp.zeros_like(carry)

        z = a_ref[...] + b_ref[...]
        logf = jnp.minimum(z, 0.0) - jnp.log(1.0 + jnp.exp(-jnp.abs(z)))
        f = _tri_mm(_tri(tl), logf) + carry[...]
        f_ref[...] = f
        carry[...] = f[tl - 1:tl, :]

    row = pl.BlockSpec((tl, W), lambda i: (i, 0))
    return _pcall(body, name=name, grid=(L // tl,), in_specs=[row, pl.BlockSpec((1, W), lambda i: (0, 0))], out_specs=row,
                  out_shape=jax.ShapeDtypeStruct((L, W), F32), scratch_shapes=[pltpu.VMEM((1, W), F32)],
                  compiler_params=_cparams(("arbitrary",)))(Af, bf)


def _fox_gate_bwd(Af, bf, dF, name):
    L, W = Af.shape
    tl = _pick(L, (512, 256, 128))
    nl = L // tl

    def body(a_ref, b_ref, df_ref, o_ref, db_ref, carry):
        i = pl.program_id(0)

        @pl.when(i == 0)
        def _():
            carry[...] = jnp.zeros_like(carry)
            db_ref[...] = jnp.zeros_like(db_ref)

        z = a_ref[...] + b_ref[...]
        dlogf = _tri_mm(_tri(tl, upper=True), df_ref[...]) + carry[...]
        carry[...] = dlogf[0:1, :]
        dz = dlogf * _sigmoid(-z)
        o_ref[...] = dz
        db_ref[...] += _sum8(dz)

    row = pl.BlockSpec((tl, W), lambda i: (nl - 1 - i, 0))
    return _pcall(body, name=name, grid=(nl,),
                  in_specs=[row, pl.BlockSpec((1, W), lambda i: (0, 0)), row],
                  out_specs=[row, pl.BlockSpec((SUBLANES, W), lambda i: (0, 0))],
                  out_shape=[jax.ShapeDtypeStruct((L, W), F32), jax.ShapeDtypeStruct((SUBLANES, W), F32)],
                  scratch_shapes=[pltpu.VMEM((1, W), F32)],
                  compiler_params=_cparams(("arbitrary",)))(Af, bf, dF)


def _attn_tiles(L):
    t = _pick(L, (512, 256, 128))
    return t, t


def _attn_scores(q, k, fq, fk, diag, scale):
    s = _dot(q, k, NT) * scale + (fq - fk)
    if not diag:
        return s
    row = lax.broadcasted_iota(jnp.int32, s.shape, 0)
    col = lax.broadcasted_iota(jnp.int32, s.shape, 1)
    return jnp.where(col <= row, s, NEG)


def _attn_geometry():
    Dh = FOX_HEAD_DIM
    hpt = LANES // Dh
    return Dh, hpt, FOX_HEADS // hpt


def _head_lanes(shape, Dh, hpt):
    lane = lax.broadcasted_iota(jnp.int32, shape, len(shape) - 1)
    return [(lane >= h * Dh) & (lane < (h + 1) * Dh) for h in range(hpt)]


def _attn_specs(t, L, hpt, ng):
    return dict(
        col=lambda off: pl.BlockSpec((t, LANES), lambda g, i: (i, g + off)),
        full=lambda off: pl.BlockSpec((L, LANES), lambda g, i: (0, g + off)),
        hq=pl.BlockSpec((hpt, t, 1), lambda g, i: (g, i, 0)),
        hk_full=pl.BlockSpec((hpt, 1, L), lambda g, i: (g, 0, 0)),
        hk=pl.BlockSpec((hpt, 1, t), lambda g, i: (g, 0, i)))


def _attn_fwd(qkv, Fq, Fk, name):
    L = qkv.shape[0]
    Dh, hpt, ng = _attn_geometry()
    t, _ = _attn_tiles(L)
    scale = Dh ** -0.5

    def body(q_ref, k_ref, v_ref, fq_ref, fk_ref, o_ref, lse_ref):
        qi = pl.program_id(1)
        sel = _head_lanes((t, LANES), Dh, hpt)
        q2 = q_ref[...]
        qh = [jnp.where(sel[h], q2, 0) for h in range(hpt)]
        fq = [fq_ref[h] for h in range(hpt)]

        def chunk(j, carry, diag):
            rows = pl.ds(pl.multiple_of(j * t, t), t)
            kc, vc = k_ref[rows, :], v_ref[rows, :]
            out = []
            for h in range(hpt):
                m, l, acc = carry[h]
                s = _attn_scores(qh[h], kc, fq[h], fk_ref[h, :, rows], diag, scale)
                m_new = jnp.maximum(m, jnp.max(s, axis=-1, keepdims=True))
                p = jnp.exp(s - m_new)
                a = jnp.exp(m - m_new)
                out.append((m_new, a * l + jnp.sum(p, axis=-1, keepdims=True), a * acc + _dot(p.astype(BF16), vc, NN)))
            return tuple(out)

        init = tuple((jnp.full((t, 1), NEG, F32), jnp.zeros((t, 1), F32), jnp.zeros((t, LANES), F32)) for _ in range(hpt))
        fin = chunk(qi, lax.fori_loop(0, qi, lambda j, c: chunk(j, c, False), init), True)
        o = jnp.zeros((t, LANES), F32)
        for h, (m, l, acc) in enumerate(fin):
            o = jnp.where(sel[h], acc / l, o)
            lse_ref[h] = m + jnp.log(l)
        o_ref[...] = o.astype(BF16)

    sp = _attn_specs(t, L, hpt, ng)
    return _pcall(body, name=name, grid=(ng, L // t),
                  in_specs=[sp["col"](0), sp["full"](ng), sp["full"](2 * ng), sp["hq"], sp["hk_full"]],
                  out_specs=[sp["col"](0), sp["hq"]],
                  out_shape=[jax.ShapeDtypeStruct((L, ng * LANES), BF16), jax.ShapeDtypeStruct((FOX_HEADS, L, 1), F32)],
                  compiler_params=_cparams(("parallel", "arbitrary")))(qkv, qkv, qkv, Fq, Fk)


def _attn_bwd(qkv, Fq, Fk, lse, do, name):
    L = qkv.shape[0]
    Dh, hpt, ng = _attn_geometry()
    _, tk = _attn_tiles(L)
    tq = _pick(L, (256, 128))
    nkc = L // tk
    scale = Dh ** -0.5

    def body(q_ref, k_ref, v_ref, fq_ref, fk_ref, lse_ref, do_ref, dq_ref, dk_ref, dv_ref, df_ref, p_s, dp_s):
        qi = pl.program_id(1)

        @pl.when(qi == 0)
        def _():
            dk_ref[...] = jnp.zeros_like(dk_ref)
            dv_ref[...] = jnp.zeros_like(dv_ref)
            df_ref[...] = jnp.zeros_like(df_ref)

        sel = _head_lanes((tq, LANES), Dh, hpt)
        q2, do2 = q_ref[...], do_ref[...]
        jd = (qi * tq) // tk
        off = qi * tq - jd * tk
        dq = jnp.zeros((tq, LANES), F32)
        for h in range(hpt):
            qh, doh = jnp.where(sel[h], q2, 0), jnp.where(sel[h], do2, 0)
            fq, lse = fq_ref[h], lse_ref[h]

            def first(j, acc, diag):
                rows = pl.ds(pl.multiple_of(j * tk, tk), tk)
                s = _dot(qh, k_ref[rows, :], NT) * scale + (fq - fk_ref[h, :, rows])
                if diag:
                    row = lax.broadcasted_iota(jnp.int32, s.shape, 0) + off
                    s = jnp.where(lax.broadcasted_iota(jnp.int32, s.shape, 1) <= row, s, NEG)
                p = jnp.exp(s - lse)
                dp = _dot(doh, v_ref[rows, :], NT)
                p_s[j] = p
                dp_s[j] = dp
                return acc + jnp.sum(p * dp, axis=-1, keepdims=True)

            delta = first(jd, lax.fori_loop(0, jd, lambda j, c: first(j, c, False), jnp.zeros((tq, 1), F32)), True)

            def second(j, acc):
                rows = pl.ds(pl.multiple_of(j * tk, tk), tk)
                p = p_s[j]
                ds = p * (dp_s[j] - delta)
                dsb = ds.astype(BF16)
                dk_ref[rows, :] += _dot(dsb, qh, TN)
                dv_ref[rows, :] += _dot(p.astype(BF16), doh, TN)
                df_ref[h, :, rows] -= jnp.sum(ds, axis=0, keepdims=True)
                return acc + _dot(dsb, k_ref[rows, :], NN)

            dq = jnp.where(sel[h], lax.fori_loop(0, jd + 1, second, jnp.zeros((tq, LANES), F32)), dq)
        dq_ref[...] = (dq * scale).astype(BF16)

        @pl.when(qi == L // tq - 1)
        def _():
            dk_ref[...] *= scale

    sp = _attn_specs(tq, L, hpt, ng)
    return _pcall(body, name=name, grid=(ng, L // tq),
                  in_specs=[sp["col"](0), sp["full"](ng), sp["full"](2 * ng), sp["hq"], sp["hk_full"], sp["hq"], sp["col"](0)],
                  out_specs=[sp["col"](0), sp["full"](0), sp["full"](0), sp["hk_full"]],
                  out_shape=[jax.ShapeDtypeStruct((L, ng * LANES), BF16), jax.ShapeDtypeStruct((L, ng * LANES), F32),
                             jax.ShapeDtypeStruct((L, ng * LANES), F32), jax.ShapeDtypeStruct((FOX_HEADS, 1, L), F32)],
                  scratch_shapes=[pltpu.VMEM((nkc, tq, tk), F32), pltpu.VMEM((nkc, tq, tk), F32)],
                  compiler_params=pltpu.CompilerParams(vmem_limit_bytes=ATTN_BWD_VMEM_LIMIT,
                                                       dimension_semantics=("parallel", "arbitrary")))(qkv, qkv, qkv, Fq, Fk, lse, do)


def _mconv_fwd(xr, w, b, name):
    L, C = xr.shape
    K = w.shape[0]
    tl = _conv_row_tile(L, False)
    tc = _pick(C, (512, 384, 256, 128))

    def body(x_ref, xp_ref, w_ref, b_ref, o_ref):
        i = pl.program_id(1)
        prev = jnp.where(i == 0, 0.0, xp_ref[...])
        pre = _conv_taps(x_ref[...], prev, w_ref, K) + b_ref[...]
        o_ref[...] = pre * _sigmoid(pre)

    return _pcall(body, name=name, grid=(C // tc, L // tl),
                  in_specs=[pl.BlockSpec((tl, tc), lambda j, i: (i, j)),
                            pl.BlockSpec((SUBLANES, tc), _halo_prev(tl, lambda j: j)),
                            pl.BlockSpec((K, tc), lambda j, i: (0, j)),
                            pl.BlockSpec((1, tc), lambda j, i: (0, j))],
                  out_specs=pl.BlockSpec((tl, tc), lambda j, i: (i, j)),
                  out_shape=jax.ShapeDtypeStruct((L, C), F32),
                  compiler_params=_cparams(("parallel", "parallel")))(xr, xr, w, b.reshape(1, C))


def _mconv_bwd(xr, dact, w, b, name):
    L, C = xr.shape
    K = w.shape[0]
    tl = _conv_row_tile(L, True)
    tc = _pick(C, (512, 384, 256, 128))
    nl = L // tl

    def body(x_ref, xp_ref, xn_ref, d_ref, dn_ref, w_ref, b_ref, o_ref, dw_ref, db_ref):
        i = pl.program_id(1)
        cur = x_ref[...]
        prev = jnp.where(i == 0, 0.0, xp_ref[...])

        def at_conv_out(x, xprev, d):
            pre = _conv_taps(x, xprev, w_ref, K) + b_ref[...]
            sg = _sigmoid(pre)
            return d * sg * (1.0 + pre * (1.0 - sg))

        dpre = at_conv_out(cur, prev, d_ref[...])
        dpre_n = jnp.where(i == nl - 1, 0.0, at_conv_out(xn_ref[...], cur[tl - SUBLANES:], dn_ref[...]))
        o_ref[...] = _conv_taps_t(dpre, dpre_n, w_ref, K).astype(BF16)

        @pl.when(i == 0)
        def _():
            dw_ref[...] = jnp.zeros_like(dw_ref)
            db_ref[...] = jnp.zeros_like(db_ref)

        db_ref[...] += _sum8(dpre)
        for k in range(K):
            dw_ref[k * SUBLANES:(k + 1) * SUBLANES, :] += _sum8(dpre * _shift_down(cur, prev, K - 1 - k))

    return _pcall(body, name=name, grid=(C // tc, nl),
                  in_specs=[pl.BlockSpec((tl, tc), lambda j, i: (i, j)),
                            pl.BlockSpec((SUBLANES, tc), _halo_prev(tl, lambda j: j)),
                            pl.BlockSpec((SUBLANES, tc), _halo_next(tl, L, SUBLANES)),
                            pl.BlockSpec((tl, tc), lambda j, i: (i, j)),
                            pl.BlockSpec((SUBLANES, tc), _halo_next(tl, L, SUBLANES)),
                            pl.BlockSpec((K, tc), lambda j, i: (0, j)),
                            pl.BlockSpec((1, tc), lambda j, i: (0, j))],
                  out_specs=[pl.BlockSpec((tl, tc), lambda j, i: (i, j)),
                             pl.BlockSpec((K * SUBLANES, tc), lambda j, i: (0, j)),
                             pl.BlockSpec((SUBLANES, tc), lambda j, i: (0, j))],
                  out_shape=[jax.ShapeDtypeStruct((L, C), BF16), jax.ShapeDtypeStruct((K * SUBLANES, C), F32),
                             jax.ShapeDtypeStruct((SUBLANES, C), F32)],
                  compiler_params=_cparams(("parallel", "arbitrary")))(xr, xr, xr, dact, dact, w, b.reshape(1, C))


def _head_selector(R, P, heads_first):
    shape = (R, R * P) if heads_first else (R * P, R)
    head = lax.broadcasted_iota(jnp.int32, shape, 0 if heads_first else 1)
    lane = lax.broadcasted_iota(jnp.int32, shape, 1 if heads_first else 0)
    d = lane - head * P
    return jnp.where((d >= 0) & (d < P), 1.0, 0.0).astype(BF16)


def _ssd_prelude(dtc_ref, dtr_ref, bc_ref, br_ref, ac_ref, ar_ref, Q, R, P):
    raw_c = dtc_ref[...] + bc_ref[...]
    dt_c = _softplus(raw_c)
    dt_r = _softplus(dtr_ref[...] + br_ref[...])
    A_c = -jnp.exp(ac_ref[...])
    acs_c = _tri_mm(_tri(Q), dt_c * A_c)
    acs_r = _tri_mm(_tri(Q, upper=True), dt_r * (-jnp.exp(ar_ref[...])), tri_first=False)
    ea_c = jnp.exp(acs_c)
    dte_c = jnp.exp(acs_c[Q - 1:Q, :] - acs_c)
    wide = _tri_mm(_head_selector(R, P, True), jnp.concatenate([dt_c, ea_c, dte_c], axis=0), tri_first=False)
    return dict(raw_c=raw_c, dt_c=dt_c, A_c=A_c, acs_c=acs_c, acs_r=acs_r, ea_c=ea_c,
                DT=wide[:Q], EA=wide[Q:2 * Q], DTE=wide[2 * Q:])


def _ssd_decay_tile(pre, r, mask):
    return jnp.exp(jnp.where(mask, pre["acs_c"][:, r:r + 1] - pre["acs_r"][r:r + 1, :], NEG))


def _ssd_specs(Q, R, P, N, G, inner, rev=None):
    cc = (lambda c: c) if rev is None else rev
    return dict(
        x=pl.BlockSpec((Q, R * P), lambda g, c: (cc(c), g)),
        b=pl.BlockSpec((Q, N), lambda g, c: (cc(c), inner // N + g)),
        c=pl.BlockSpec((Q, N), lambda g, c: (cc(c), inner // N + G + g)),
        dtc=pl.BlockSpec((None, Q, R), lambda g, c: (g, cc(c), 0)),
        dtr=pl.BlockSpec((None, R, Q), lambda g, c: (g, 0, cc(c))),
        pc=pl.BlockSpec((None, 1, R), lambda g, c: (g, 0, 0)),
        pr=pl.BlockSpec((None, R, 1), lambda g, c: (g, 0, 0)),
        px=pl.BlockSpec((None, 1, R * P), lambda g, c: (g, 0, 0)),
        st=pl.BlockSpec((None, None, N, R * P), lambda g, c: (cc(c), g, 0, 0)))


def _ssd_fwd(act, dtc, dtr, bias_c, bias_r, alog_c, alog_r, dsk_x, name):
    G, L, R = dtc.shape
    N, P, Q = SSM_STATE, SSM_HEAD_DIM, SSM_CHUNK
    RP = R * P
    inner = G * RP
    nc = L // Q

    def body(x_ref, b_ref, c_ref, dtc_ref, dtr_ref, bc_ref, br_ref, ac_ref, ar_ref, dk_ref, y_ref, hp_ref, st):
        c = pl.program_id(1)

        @pl.when(c == 0)
        def _():
            st[...] = jnp.zeros_like(st)

        pre = _ssd_prelude(dtc_ref, dtr_ref, bc_ref, br_ref, ac_ref, ar_ref, Q, R, P)
        X = x_ref[...]
        XT = X * pre["DT"]
        Bb = b_ref[...].astype(BF16)
        Cb = c_ref[...].astype(BF16)
        CB = _dot(Cb, Bb, NT)
        mask = lax.broadcasted_iota(jnp.int32, (Q, Q), 0) >= lax.broadcasted_iota(jnp.int32, (Q, Q), 1)
        low = lax.broadcasted_iota(jnp.int32, (Q, 2 * P), 1) < P
        pieces = []
        for k in range(R // 2):
            xt2 = XT[:, 2 * P * k:2 * P * (k + 1)]
            acc = None
            for half in range(2):
                Gm = CB * _ssd_decay_tile(pre, 2 * k + half, mask)
                part = _dot(Gm.astype(BF16), jnp.where(low == (half == 0), xt2, 0.0).astype(BF16), NN)
                acc = part if acc is None else acc + part
            pieces.append(acc)
        HP = st[...]
        hp_ref[...] = HP
        yoff = pre["EA"] * _dot(Cb, HP.astype(BF16), NN)
        st[...] = HP * pre["EA"][Q - 1:Q, :] + _dot(Bb, (XT * pre["DTE"]).astype(BF16), TN)
        y_ref[...] = jnp.concatenate(pieces, axis=1) + yoff + dk_ref[...] * X

    sp = _ssd_specs(Q, R, P, N, G, inner)
    return _pcall(body, name=name, grid=(G, nc),
                  in_specs=[sp["x"], sp["b"], sp["c"], sp["dtc"], sp["dtr"], sp["pc"], sp["pr"], sp["pc"], sp["pr"], sp["px"]],
                  out_specs=[sp["x"], sp["st"]],
                  out_shape=[jax.ShapeDtypeStruct((L, inner), F32), jax.ShapeDtypeStruct((nc, G, N, RP), F32)],
                  scratch_shapes=[pltpu.VMEM((N, RP), F32)],
                  compiler_params=_cparams(("parallel", "arbitrary")))(act, act, act, dtc, dtr, bias_c, bias_r, alog_c, alog_r, dsk_x)


def _ssd_bwd(act, dtc, dtr, bias_c, bias_r, alog_c, alog_r, dsk_x, hprev, dy, name):
    G, L, R = dtc.shape
    N, P, Q = SSM_STATE, SSM_HEAD_DIM, SSM_CHUNK
    RP = R * P
    inner = G * RP
    nc = L // Q

    def body(x_ref, b_ref, c_ref, dtc_ref, dtr_ref, bc_ref, br_ref, ac_ref, ar_ref, dk_ref, hp_ref, dy_ref,
             dx_ref, db_ref, dc_ref, ddt_ref, gbias_ref, galog_ref, gdsk_ref, dst):
        c = pl.program_id(1)

        @pl.when(c == 0)
        def _():
            dst[...] = jnp.zeros_like(dst)
            gbias_ref[...] = jnp.zeros_like(gbias_ref)
            galog_ref[...] = jnp.zeros_like(galog_ref)
            gdsk_ref[...] = jnp.zeros_like(gdsk_ref)

        pre = _ssd_prelude(dtc_ref, dtr_ref, bc_ref, br_ref, ac_ref, ar_ref, Q, R, P)
        DT, EA, DTE = pre["DT"], pre["EA"], pre["DTE"]
        E_END = EA[Q - 1:Q, :]
        X, DY = x_ref[...], dy_ref[...]
        XT = X * DT
        Bb = b_ref[...].astype(BF16)
        Cb = c_ref[...].astype(BF16)
        CB = _dot(Cb, Bb, NT)
        HP, dH = hp_ref[...], dst[...]
        HPb, dHb = HP.astype(BF16), dH.astype(BF16)
        EDY = EA * DY
        EDYb = EDY.astype(BF16)
        dC = _dot(EDYb, HPb, NT)
        dHP = _dot(Cb, EDYb, TN)
        da_off = EDY * _dot(Cb, HPb, NN)
        Z = _dot(Bb, dHb, NN)
        XD = XT * DTE
        dB = _dot(XD.astype(BF16), dHb, NT)
        dXT = DTE * Z
        t_x = XD * Z
        hh = jnp.sum(dH * HP, axis=0, keepdims=True) * E_END
        dst[...] = dHP + dH * E_END
        mask = lax.broadcasted_iota(jnp.int32, (Q, Q), 0) >= lax.broadcasted_iota(jnp.int32, (Q, Q), 1)
        eye = lax.broadcasted_iota(jnp.int32, (Q, Q), 0) == lax.broadcasted_iota(jnp.int32, (Q, Q), 1)
        low = lax.broadcasted_iota(jnp.int32, (Q, 2 * P), 1) < P
        lane = lax.broadcasted_iota(jnp.int32, (Q, R), 1)
        dCB = jnp.zeros((Q, Q), F32)
        da_mat = jnp.zeros((Q, R), F32)
        pieces = []
        for k in range(R // 2):
            sl = slice(2 * P * k, 2 * P * (k + 1))
            xt2, dy2 = XT[:, sl], DY[:, sl]
            acc = None
            for half in range(2):
                r = 2 * k + half
                sel = low == (half == 0)
                Lm = _ssd_decay_tile(pre, r, mask)
                Gm = CB * Lm
                dyb = jnp.where(sel, dy2, 0.0).astype(BF16)
                part = _dot(Gm.astype(BF16), dyb, TN)
                acc = part if acc is None else acc + part
                dG = jnp.where(mask, _dot(dyb, jnp.where(sel, xt2, 0.0).astype(BF16), NT), 0.0)
                Mm = dG * Gm
                dCB = dCB + dG * Lm
                colsum = jnp.sum(jnp.where(eye, jnp.sum(Mm, axis=0, keepdims=True), 0.0), axis=1, keepdims=True)
                da_mat = jnp.where(lane == r, jnp.sum(Mm, axis=1, keepdims=True) - colsum, da_mat)
            pieces.append(acc)
        dXT = dXT + jnp.concatenate(pieces, axis=1)
        dCBb = dCB.astype(BF16)
        dc_ref[...] = dC + _dot(dCBb, Bb, NN)
        db_ref[...] = dB + _dot(dCBb, Cb, TN)
        dx_ref[...] = dXT * DT + dk_ref[...] * DY
        pad = jnp.zeros((SUBLANES - 1, RP), F32)
        sums = _tri_mm(_head_selector(R, P, False), jnp.concatenate([da_off, t_x, dXT * X, DY * X, hh, pad], axis=0), tri_first=False)
        t = sums[Q:2 * Q]
        da_end = jnp.sum(t, axis=0, keepdims=True) + sums[4 * Q:4 * Q + 1]
        rowi = lax.broadcasted_iota(jnp.int32, (Q, R), 0)
        da_mat = da_mat + sums[:Q] - t + jnp.where(rowi == Q - 1, da_end, 0.0)
        ddtA = _tri_mm(_tri(Q, upper=True), da_mat)
        ddt_raw = (ddtA * pre["A_c"] + sums[2 * Q:3 * Q]) * _sigmoid(pre["raw_c"])
        ddt_ref[...] = ddt_raw
        gbias_ref[...] += jnp.sum(ddt_raw, axis=0, keepdims=True)
        galog_ref[...] += jnp.sum(ddtA * pre["dt_c"], axis=0, keepdims=True) * pre["A_c"]
        gdsk_ref[...] += jnp.sum(sums[3 * Q:4 * Q], axis=0, keepdims=True)

    sp = _ssd_specs(Q, R, P, N, G, inner, rev=lambda c: nc - 1 - c)
    bout = pl.BlockSpec((Q, N), lambda g, c: (nc - 1 - c, g))
    return _pcall(body, name=name, grid=(G, nc),
                  in_specs=[sp["x"], sp["b"], sp["c"], sp["dtc"], sp["dtr"], sp["pc"], sp["pr"], sp["pc"], sp["pr"], sp["px"],
                            sp["st"], sp["x"]],
                  out_specs=[sp["x"], bout, bout, sp["dtc"], sp["pc"], sp["pc"], sp["pc"]],
                  out_shape=[jax.ShapeDtypeStruct((L, inner), F32), jax.ShapeDtypeStruct((L, G * N), F32),
                             jax.ShapeDtypeStruct((L, G * N), F32), jax.ShapeDtypeStruct((G, L, R), F32),
                             jax.ShapeDtypeStruct((G, 1, R), F32), jax.ShapeDtypeStruct((G, 1, R), F32),
                             jax.ShapeDtypeStruct((G, 1, R), F32)],
                  scratch_shapes=[pltpu.VMEM((N, RP), F32)],
                  compiler_params=_cparams(("parallel", "arbitrary")))(
        act, act, act, dtc, dtr, bias_c, bias_r, alog_c, alog_r, dsk_x, hprev, dy)


def _gnorm_fwd(y, z, g, name):
    L, Dn = y.shape
    gs = Dn // SSM_GROUPS
    tl = _row_tile(L)

    def body(y_ref, z_ref, g_ref, o_ref):
        for k in range(SSM_GROUPS):
            sl = slice(k * gs, (k + 1) * gs)
            zz = z_ref[:, sl]
            u = y_ref[:, sl] * zz * _sigmoid(zz)
            rstd = lax.rsqrt(jnp.mean(u * u, axis=-1, keepdims=True) + RMS_EPS)
            o_ref[:, sl] = (u * rstd * g_ref[:, sl]).astype(BF16)

    row = pl.BlockSpec((tl, Dn), lambda i: (i, 0))
    return _pcall(body, name=name, grid=(L // tl,), in_specs=[row, row, pl.BlockSpec((1, Dn), lambda i: (0, 0))],
                  out_specs=row, out_shape=jax.ShapeDtypeStruct((L, Dn), BF16),
                  compiler_params=_cparams(("parallel",)))(y, z, g.reshape(1, Dn))


def _gnorm_bwd(y, z, g, dout, name):
    L, Dn = y.shape
    gs = Dn // SSM_GROUPS
    tl = _row_tile(L)

    def body(y_ref, z_ref, g_ref, d_ref, dy_ref, dz_ref, dg_ref):
        i = pl.program_id(0)

        @pl.when(i == 0)
        def _():
            dg_ref[...] = jnp.zeros_like(dg_ref)

        for k in range(SSM_GROUPS):
            sl = slice(k * gs, (k + 1) * gs)
            zz = z_ref[:, sl]
            yy = y_ref[:, sl]
            sg = _sigmoid(zz)
            sil = zz * sg
            u = yy * sil
            rstd = lax.rsqrt(jnp.mean(u * u, axis=-1, keepdims=True) + RMS_EPS)
            n = u * rstd
            d = d_ref[:, sl]
            dn = d * g_ref[:, sl]
            du = rstd * (dn - n * jnp.mean(dn * n, axis=-1, keepdims=True))
            dy_ref[:, sl] = du * sil
            dz_ref[:, sl] = (du * yy * sg * (1.0 + zz * (1.0 - sg))).astype(BF16)
            dg_ref[:, sl] += _sum8(d * n)

    row = pl.BlockSpec((tl, Dn), lambda i: (i, 0))
    return _pcall(body, name=name, grid=(L // tl,), in_specs=[row, row, pl.BlockSpec((1, Dn), lambda i: (0, 0)), row],
                  out_specs=[row, row, pl.BlockSpec((SUBLANES, Dn), lambda i: (0, 0))],
                  out_shape=[jax.ShapeDtypeStruct((L, Dn), F32), jax.ShapeDtypeStruct((L, Dn), BF16),
                             jax.ShapeDtypeStruct((SUBLANES, Dn), F32)],
                  compiler_params=_cparams(("arbitrary",)))(y, z, g.reshape(1, Dn), dout)


def _adamw(w, g, m, v, name):
    rows, W = w.shape
    tr = _pick(rows, (512, 256, 128, 64, 32, 16, 8))
    c1 = 1.0 / (1.0 - ADAM_B1 ** ADAM_STEP)
    c2 = 1.0 / (1.0 - ADAM_B2 ** ADAM_STEP)

    def body(w_ref, g_ref, m_ref, v_ref, d_ref, nm_ref, nv_ref):
        g_ = g_ref[...]
        nm = ADAM_B1 * m_ref[...] + (1.0 - ADAM_B1) * g_
        nv = ADAM_B2 * v_ref[...] + (1.0 - ADAM_B2) * (g_ * g_)
        nm_ref[...] = nm
        nv_ref[...] = nv
        d_ref[...] = -ADAM_LR * ((nm * c1) / (jnp.sqrt(nv * c2) + ADAM_EPS) + ADAM_WD * w_ref[...])

    blk = pl.BlockSpec((tr, W), lambda i: (i, 0))
    return _pcall(body, name=name, grid=(rows // tr,), in_specs=[blk] * 4, out_specs=[blk] * 3,
                  out_shape=[jax.ShapeDtypeStruct((rows, W), F32)] * 3, compiler_params=_cparams(("parallel",)))(w, g, m, v)


def _sum_slots(x, name, extra=None):
    n, rows, W = x.shape
    tr = _pick(rows, (512, 256, 128, 64, 32, 16, 8))
    has_extra = extra is not None

    def body(*refs):
        if has_extra:
            e_ref, x_ref, o_ref = refs
            acc = e_ref[...].astype(F32)
            start = 0
        else:
            x_ref, o_ref = refs
            acc = x_ref[0].astype(F32)
            start = 1
        for s in range(start, n):
            acc = acc + x_ref[s].astype(F32)
        o_ref[...] = acc

    blk = pl.BlockSpec((tr, W), lambda i: (i, 0))
    xblk = pl.BlockSpec((n, tr, W), lambda i: (0, i, 0))
    return _pcall(body, name=name, grid=(rows // tr,), in_specs=([blk] if has_extra else []) + [xblk], out_specs=blk,
                  out_shape=jax.ShapeDtypeStruct((rows, W), F32), compiler_params=_cparams(("parallel",)))(
        *(([extra] if has_extra else []) + [x]))


def _add_pairs(a, b, name):
    n, rows, W = a.shape
    tr = _pick(rows, (512, 256, 128, 64, 32, 16, 8))

    def body(a_ref, b_ref, o_ref):
        o_ref[...] = (a_ref[...].astype(F32) + b_ref[...].astype(F32)).astype(BF16)

    blk = pl.BlockSpec((None, tr, W), lambda s, i: (s, i, 0))
    return _pcall(body, name=name, grid=(n, rows // tr), in_specs=[blk, blk], out_specs=blk,
                  out_shape=jax.ShapeDtypeStruct((n, rows, W), BF16), compiler_params=_cparams(("parallel", "parallel")))(a, b)


MESH = pl.DeviceIdType.MESH
HBM_SPEC = pl.BlockSpec(memory_space=pl.ANY)


def _me():
    return lax.axis_index("x"), lax.axis_index("y"), lax.axis_index("c")


def _all_gather(arrs, name):
    n = len(arrs)

    def body(*refs):
        ins, outs = refs[:n], refs[n:2 * n]
        send_sems, recv_sems, local_sems = refs[2 * n:]
        x, y, c = _me()
        me, sib = (x, y, c), (x, y, 1 - c)
        chips = [(1 - x, y), (x, 1 - y), (1 - x, 1 - y)]

        def slot(a, dev):
            return outs[a].at[4 * dev[0] + 2 * dev[1] + dev[2]]

        def copy(a, k, block, to, src=None):
            return pltpu.make_async_remote_copy(src_ref=slot(a, block) if src is None else src, dst_ref=slot(a, block),
                                                send_sem=send_sems.at[a * 7 + k], recv_sem=recv_sems.at[a * 7 + k],
                                                device_id=to, device_id_type=MESH)

        mine = [pltpu.make_async_copy(ins[a], slot(a, me), local_sems.at[a]) for a in range(n)]
        for cp in mine:
            cp.start()
        first = []
        for a in range(n):
            first.append(copy(a, 0, me, sib, src=ins[a]))
            first += [copy(a, 1 + j, me, (*chip, c), src=ins[a]) for j, chip in enumerate(chips)]
        for cp in first:
            cp.start()
        passed = []
        for j, chip in enumerate(chips):
            for a in range(n):
                copy(a, 1 + j, (*chip, c), me).wait_recv()
                fw = copy(a, 4 + j, (*chip, c), sib)
                fw.start()
                passed.append(fw)
        for a in range(n):
            copy(a, 0, sib, me).wait_recv()
            for j, chip in enumerate(chips):
                copy(a, 4 + j, (*chip, 1 - c), me).wait_recv()
        for cp in first + passed:
            cp.wait_send()
        for cp in mine:
            cp.wait()

    return _pcall(body, name=name, in_specs=[HBM_SPEC] * n, out_specs=[HBM_SPEC] * n,
                  out_shape=[jax.ShapeDtypeStruct((N_DEV,) + a.shape, a.dtype) for a in arrs],
                  scratch_shapes=[pltpu.SemaphoreType.DMA((7 * n,)), pltpu.SemaphoreType.DMA((7 * n,)),
                                  pltpu.SemaphoreType.DMA((n,))])(*arrs)


def _rs_sibling(gs, name):
    n = len(gs)

    def body(*refs):
        g_refs, o_refs = refs[:n], refs[n:2 * n]
        send_sems, recv_sems = refs[2 * n:]
        x, y, c = _me()
        sib = (x, y, 1 - c)
        cps = [pltpu.make_async_remote_copy(src_ref=g_refs[a].at[2 * q + (1 - c)], dst_ref=o_refs[a].at[q],
                                            send_sem=send_sems.at[4 * a + q], recv_sem=recv_sems.at[4 * a + q],
                                            device_id=sib, device_id_type=MESH) for a in range(n) for q in range(4)]
        for cp in cps:
            cp.start()
        for cp in cps:
            cp.wait()

    return _pcall(body, name=name, in_specs=[HBM_SPEC] * n, out_specs=[HBM_SPEC] * n,
                  out_shape=[jax.ShapeDtypeStruct((4,) + g.shape[1:], g.dtype) for g in gs],
                  scratch_shapes=[pltpu.SemaphoreType.DMA((4 * n,)), pltpu.SemaphoreType.DMA((4 * n,))])(*gs)


def _rs_chips(ps, name):
    n = len(ps)

    def body(*refs):
        p_refs, o_refs = refs[:n], refs[n:2 * n]
        send_sems, recv_sems = refs[2 * n:]
        x, y, c = _me()
        chips = [(1 - x, y), (x, 1 - y), (1 - x, 1 - y)]
        cps = [pltpu.make_async_remote_copy(src_ref=p_refs[a].at[2 * chip[0] + chip[1]], dst_ref=o_refs[a].at[j],
                                            send_sem=send_sems.at[3 * a + j], recv_sem=recv_sems.at[3 * a + j],
                                            device_id=(*chip, c), device_id_type=MESH)
               for j, chip in enumerate(chips) for a in range(n)]
        for cp in cps:
            cp.start()
        for cp in cps:
            cp.wait()

    return _pcall(body, name=name, in_specs=[HBM_SPEC] * n, out_specs=[HBM_SPEC] * n,
                  out_shape=[jax.ShapeDtypeStruct((3,) + p.shape[1:], p.dtype) for p in ps],
                  scratch_shapes=[pltpu.SemaphoreType.DMA((3 * n,)), pltpu.SemaphoreType.DMA((3 * n,))])(*ps)


def _reduce_scatter(gs, name):
    x, y, c = _me()
    from_sib = _rs_sibling(gs, name + "_sib")
    pairs = []
    for a, (g, fs) in enumerate(zip(gs, from_sib)):
        own = g.reshape((4, 2) + g.shape[1:])
        pairs.append(_add_pairs(jnp.where(c == 0, own[:, 0], own[:, 1]), fs, f"{name}_pair{a}"))
    from_chips = _rs_chips(pairs, name + "_chips")
    return [_sum_slots(fc, f"{name}_sum{a}", extra=lax.dynamic_index_in_dim(p, 2 * x + y, axis=0, keepdims=False))
            for a, (p, fc) in enumerate(zip(pairs, from_chips))]


BIG = ("even_w_in", "even_w_out", "odd_w_in", "odd_w_out", "ffn_w_up", "ffn_w_down", "ple_w_proj", "ple_w_gate")
SMALL_SHARDED = ("even_conv_w", "odd_conv_w", "odd_conv_b", "odd_norm_g", "ffn_conv_w")
REPLICATED = ("even_b_f", "odd_dt_bias", "odd_a_log", "odd_d_skip", "ln_mix_g", "ln_mix_b", "ffn_conv_b",
              "ln_ffn_g", "ln_ffn_b", "ple_b_gate")
WEIGHTS = ("even_w_in", "even_b_f", "even_conv_w", "even_w_out", "odd_w_in", "odd_conv_w", "odd_conv_b", "odd_dt_bias",
           "odd_a_log", "odd_d_skip", "odd_norm_g", "odd_w_out", "ln_mix_g", "ln_mix_b", "ffn_w_up", "ffn_conv_w",
           "ffn_conv_b", "ffn_w_down", "ln_ffn_g", "ln_ffn_b", "ple_w_proj", "ple_w_gate", "ple_b_gate")


def _full_shapes():
    d = _dims()
    return {
        "even_w_in": ((1, D_MODEL, d["even_in"]), 2), "even_b_f": ((1, FOX_HEADS), None),
        "even_conv_w": ((1, CONV_WIDTH, CONV_DIM), 2), "even_w_out": ((1, d["even_mix"], D_MODEL), 1),
        "odd_w_in": ((1, D_MODEL, d["odd_in"]), 2), "odd_conv_w": ((1, SSM_CONV_WIDTH, d["conv_ch"]), 2),
        "odd_conv_b": ((1, d["conv_ch"]), 1), "odd_dt_bias": ((1, d["ssm_heads"]), None),
        "odd_a_log": ((1, d["ssm_heads"]), None), "odd_d_skip": ((1, d["ssm_heads"]), None),
        "odd_norm_g": ((1, d["ssm_inner"]), 1), "odd_w_out": ((1, d["ssm_inner"], D_MODEL), 1),
        "ln_mix_g": ((DEPTH, D_MODEL), None), "ln_mix_b": ((DEPTH, D_MODEL), None),
        "ffn_w_up": ((DEPTH, D_MODEL, 2 * D_FF), 2), "ffn_conv_w": ((DEPTH, FFN_CONV_WIDTH, 2 * D_FF), 2),
        "ffn_conv_b": ((DEPTH, 2 * D_FF), None), "ffn_w_down": ((DEPTH, D_FF, D_MODEL), 1),
        "ln_ffn_g": ((DEPTH, D_MODEL), None), "ln_ffn_b": ((DEPTH, D_MODEL), None),
        "ple_w_proj": ((DEPTH, PLE_DIM, D_MODEL), 2), "ple_w_gate": ((DEPTH, D_MODEL, D_MODEL), 1),
        "ple_b_gate": ((DEPTH, D_MODEL), None),
    }


def _shard_shape(name):
    shape, ax = _full_shapes()[name]
    if ax is None:
        return shape
    return tuple(s // N_DEV if i == ax else s for i, s in enumerate(shape))


def _as2d(a, lead=0):
    return a.reshape(a.shape[:lead] + (-1, a.shape[-1]))


def _part_rows(shape):
    n = int(np.prod(shape))
    return -(-(-(-n // PACK_W)) // SUBLANES) * SUBLANES


def _pack_small(parts):
    out = []
    for p in parts:
        n, rows = int(np.prod(p.shape)), _part_rows(p.shape)
        out.append(jnp.pad(p.reshape(-1).astype(F32), (0, rows * PACK_W - n)).reshape(rows, PACK_W))
    return jnp.concatenate(out, axis=0)


def _unpack_small(pack, shapes):
    lead = pack.shape[:-2]
    out, off = [], 0
    for s in shapes:
        n, rows = int(np.prod(s)), _part_rows(s)
        part = pack[..., off:off + rows, :].reshape(lead + (-1,))[..., :n]
        out.append(part.reshape(lead + tuple(s)))
        off += rows
    return out


def _assemble(gathered, name):
    shape, ax = _full_shapes()[name]
    return jnp.moveaxis(gathered, 0, ax).reshape(shape)


def _split_dest(full, name):
    shape, ax = _full_shapes()[name]
    sh = shape[:ax] + (N_DEV, shape[ax] // N_DEV) + shape[ax + 1:]
    return jnp.moveaxis(full.reshape(sh), ax, 0)


def _interleave_cols(w, parts, tc):
    C = w.shape[-1] // parts
    sh = w.shape[:-1]
    return w.reshape(sh + (parts, C // tc, tc)).swapaxes(-3, -2).reshape(sh + (parts * C,))


def _deinterleave_cols(w, parts, tc):
    C = w.shape[-1] // parts
    sh = w.shape[:-1]
    return w.reshape(sh + (C // tc, parts, tc)).swapaxes(-3, -2).reshape(sh + (parts * C,))


def _pad_cols(a, to):
    return jnp.pad(a, ((0, 0), (0, to - a.shape[1])))


def _tail_fwd(i, h_in, mix, p_i, W, sp):
    r1, h1, h1b = _ln_fwd(h_in, mix, sp["ln_mix_g"][i], sp["ln_mix_b"][i], f"ln_mix_fwd{i}")
    U = _mm(h1b, W["ffn_up"][i], "nn", F32, f"ffn_up{i}")
    S = _ffn_act_fwd(U, sp["ffn_conv_w"][i], sp["ffn_conv_b"][i], f"ffn_act_fwd{i}")
    ffn = _mm(S, W["ffn_down"][i], "nn", F32, f"ffn_down{i}")
    r2, h2, h2b = _ln_fwd(h1, ffn, sp["ln_ffn_g"][i], sp["ln_ffn_b"][i], f"ln_ffn_fwd{i}")
    G = _mm(h2b, W["ple_gate"][i], "nn", F32, f"ple_gate{i}")
    E = _mm(p_i, W["ple_proj"][i], "nn", F32, f"ple_proj{i}")
    h3, h3b = _ple_fwd(h2, G, sp["ple_b_gate"][i], E, f"ple_fwd{i}")
    return h3, h3b, dict(r1=r1, h1b=h1b, U=U, S=S, r2=r2, h2b=h2b, G=G, E=E, p=p_i)


def _tail_bwd(i, dh3, sv, W, sp, grads):
    alpha = _alpha()
    dE, dGp, dbg = _ple_bwd(dh3, sv["G"], sp["ple_b_gate"][i], sv["E"], f"ple_bwd{i}")
    grads["ple_b_gate"][i] = dbg.sum(0)
    grads["ple_w_proj"][i] = _mm(sv["p"], dE, "tn", F32, f"d_ple_proj{i}")
    grads["ple_w_gate"][i] = _mm(sv["h2b"], dGp, "tn", F32, f"d_ple_gate{i}")
    dh2 = _mm(dGp, W["ple_gate"][i], "nt", F32, f"dx_ple_gate{i}", add=dh3)
    dr2, dr2b, dg, db = _ln_bwd(sv["r2"], dh2, sp["ln_ffn_g"][i], f"ln_ffn_bwd{i}")
    grads["ln_ffn_g"][i], grads["ln_ffn_b"][i] = dg.sum(0), db.sum(0)
    grads["ffn_w_down"][i] = _mm(sv["S"], dr2b, "tn", F32, f"d_ffn_down{i}")
    dS = _mm(dr2b, W["ffn_down"][i], "nt", BF16, f"dx_ffn_down{i}")
    dUg, dUv, dwg, dwv, dbg, dbv = _ffn_act_bwd(sv["U"], dS, sp["ffn_conv_w"][i], sp["ffn_conv_b"][i], f"ffn_act_bwd{i}")
    K = FFN_CONV_WIDTH
    grads["ffn_conv_w"][i] = jnp.concatenate([dwg.reshape(K, SUBLANES, -1).sum(1), dwv.reshape(K, SUBLANES, -1).sum(1)], axis=1)
    grads["ffn_conv_b"][i] = jnp.concatenate([dbg.sum(0), dbv.sum(0)])
    grads["ffn_w_up"][i] = jnp.concatenate([_mm(sv["h1b"], dUg, "tn", F32, f"d_ffn_up_g{i}"),
                                            _mm(sv["h1b"], dUv, "tn", F32, f"d_ffn_up_v{i}")], axis=1)
    dh1 = _mm(dUg, W["ffn_up"][i], "nt", F32, f"dx_ffn_up_g{i}", add=dr2, add_scale=alpha)
    dh1 = _mm(dUv, W["ffn_up"][i], "nt", F32, f"dx_ffn_up_v{i}", add=dh1, b_k_start=D_FF)
    dr1, dr1b, dg, db = _ln_bwd(sv["r1"], dh1, sp["ln_mix_g"][i], f"ln_mix_bwd{i}")
    grads["ln_mix_g"][i], grads["ln_mix_b"][i] = dg.sum(0), db.sum(0)
    return dr1, dr1b


def _even_fwd(h, W, sp):
    L = h.shape[0]
    H, Dh = FOX_HEADS, FOX_HEAD_DIM
    Ac = _mm(h, W["even_in_conv"], "nn", F32, "even_in_conv")
    qkv = _mm(h, W["even_in_qkv"], "nn", BF16, "even_in_qkv")
    Af = _mm(h, W["even_in_f"], "nn", F32, "even_in_f")
    y_a = _sconv_fwd(Ac, sp["even_conv_w_il"], "sconv_fwd")
    Fc = _fox_gate_fwd(Af, sp["even_b_f_pad"], "fox_gate_fwd")
    Fh = Fc[:, :H].T
    Fq, Fk = Fh.reshape(H, L, 1), Fh.reshape(H, 1, L)
    o, lse = _attn_fwd(qkv, Fq, Fk, "attn_fwd")
    Y = jnp.concatenate([y_a, o], axis=1)
    mix = _mm(Y, W["even_out"], "nn", F32, "even_out")
    return mix, dict(h=h, Ac=Ac, Af=Af, qkv=qkv, Fq=Fq, Fk=Fk, lse=lse, Y=Y)


def _even_bwd(dmix, dres, sv, W, sp, grads):
    H, Dh = FOX_HEADS, FOX_HEAD_DIM
    C = CONV_DIM
    L = dmix.shape[0]
    grads["even_w_out"][0] = _mm(sv["Y"], dmix, "tn", F32, "d_even_out")
    dY = _mm(dmix, W["even_out"], "nt", F32, "dx_even_out")
    dya = dY[:, :C]
    do = dY[:, C:].astype(BF16)
    dc, dcw = _sconv_bwd_dc(sv["Ac"], dya, sp["even_conv_w_il"], "sconv_bwd_dc")
    grads["even_conv_w"][0] = dcw.reshape(CONV_WIDTH, SUBLANES, -1).sum(1)
    dAc = _sconv_bwd_da(sv["Ac"], dya, dc, sp["even_conv_w_il"], "sconv_bwd_da")
    dq, dk, dv, dFk = _attn_bwd(sv["qkv"], sv["Fq"], sv["Fk"], sv["lse"], do, "attn_bwd")
    dqkv = jnp.concatenate([dq, dk.astype(BF16), dv.astype(BF16)], axis=1)
    dF = _pad_cols(dFk.reshape(H, L).T, LANES)
    dAf, dbf = _fox_gate_bwd(sv["Af"], sp["even_b_f_pad"], dF, "fox_gate_bwd")
    grads["even_b_f"][0] = dbf.sum(0)[:H]
    h = sv["h"]
    gc = _deinterleave_cols(_mm(h, dAc, "tn", F32, "d_even_in_conv"), 3, LANES)
    gq = _mm(h, dqkv, "tn", F32, "d_even_in_qkv")
    gf = _mm(h, dAf, "tn", F32, "d_even_in_f")[:, :H]
    grads["even_w_in"][0] = jnp.concatenate([gc, gq, gf], axis=1)
    dh = _mm(dAc, W["even_in_conv"], "nt", F32, "dx_even_in_conv", add=dres, add_scale=_alpha())
    dh = _mm(dqkv, W["even_in_qkv"], "nt", F32, "dx_even_in_qkv", add=dh)
    dh = _mm(dAf, W["even_in_f"], "nt", F32, "dx_even_in_f", add=dh)
    return dh


def _group_layouts(v, G):
    R = v.shape[0] // G
    return v.reshape(G, 1, R), v.reshape(G, R, 1)


def _odd_fwd(h, W, sp):
    d = _dims()
    L = h.shape[0]
    Hs, G, N, P = d["ssm_heads"], SSM_GROUPS, SSM_STATE, SSM_HEAD_DIM
    R = Hs // G
    inner = d["ssm_inner"]
    z = _mm(h, W["odd_in_z"], "nn", F32, "odd_in_z")
    xr = _mm(h, W["odd_in_x"], "nn", F32, "odd_in_x")
    dtp = _mm(h, W["odd_in_dt"], "nn", F32, "odd_in_dt")
    act = _mconv_fwd(xr, sp["odd_conv_w"], sp["odd_conv_b"], "mconv_fwd")
    dtg = dtp[:, :Hs].reshape(L, G, R)
    dtc, dtr = dtg.transpose(1, 0, 2), dtg.transpose(1, 2, 0)
    dsk_x = jnp.repeat(sp["odd_d_skip"], P).reshape(G, 1, R * P)
    ssd_in = (act, dtc, dtr) + _group_layouts(sp["odd_dt_bias"], G) + _group_layouts(sp["odd_a_log"], G) + (dsk_x,)
    y, hprev = _ssd_fwd(*ssd_in, "ssd_fwd")
    u = _gnorm_fwd(y, z, sp["odd_norm_g"], "gnorm_fwd")
    mix = _mm(u, W["odd_out"], "nn", F32, "odd_out")
    return mix, dict(h=h, z=z, xr=xr, ssd_in=ssd_in, hprev=hprev, y=y, u=u)


def _odd_bwd(dmix, dres, sv, W, sp, grads):
    d = _dims()
    L = dmix.shape[0]
    Hs, G, N, P = d["ssm_heads"], SSM_GROUPS, SSM_STATE, SSM_HEAD_DIM
    grads["odd_w_out"][0] = _mm(sv["u"], dmix, "tn", F32, "d_odd_out")
    du = _mm(dmix, W["odd_out"], "nt", F32, "dx_odd_out")
    dy, dz, dg = _gnorm_bwd(sv["y"], sv["z"], sp["odd_norm_g"], du, "gnorm_bwd")
    grads["odd_norm_g"][0] = dg.sum(0)
    dxs, dB, dC, ddt, gbias, galog, gdsk = _ssd_bwd(*sv["ssd_in"], sv["hprev"], dy, "ssd_bwd")
    grads["odd_dt_bias"][0] = gbias.reshape(Hs)
    grads["odd_a_log"][0] = galog.reshape(Hs)
    grads["odd_d_skip"][0] = gdsk.reshape(Hs)
    dact = jnp.concatenate([dxs, dB, dC], axis=1)
    dxr, dcw, dcb = _mconv_bwd(sv["xr"], dact, sp["odd_conv_w"], sp["odd_conv_b"], "mconv_bwd")
    grads["odd_conv_w"][0] = dcw.reshape(SSM_CONV_WIDTH, SUBLANES, -1).sum(1)
    grads["odd_conv_b"][0] = dcb.sum(0)
    ddtp = _pad_cols(ddt.transpose(1, 0, 2).reshape(L, Hs), W["odd_in_dt"].shape[1])
    h = sv["h"]
    gz = _mm(h, dz, "tn", F32, "d_odd_in_z")
    gx = _mm(h, dxr, "tn", F32, "d_odd_in_x")
    gdt = _mm(h, ddtp, "tn", F32, "d_odd_in_dt")[:, :Hs]
    grads["odd_w_in"][0] = jnp.concatenate([gz, gx, gdt], axis=1)
    dh = _mm(dz, W["odd_in_z"], "nt", F32, "dx_odd_in_z", add=dres, add_scale=_alpha())
    dh = _mm(dxr, W["odd_in_x"], "nt", F32, "dx_odd_in_x", add=dh)
    dh = _mm(ddtp, W["odd_in_dt"], "nt", F32, "dx_odd_in_dt", add=dh)
    return dh


def _prepare_weights(full):
    d = _dims()
    C, fd, H = CONV_DIM, d["fox_dim"], FOX_HEADS
    W, sp = {}, {}
    ew = full["even_w_in"][0]
    W["even_in_conv"] = _interleave_cols(ew[:, :3 * C], 3, LANES)
    W["even_in_qkv"] = ew[:, 3 * C:3 * C + 3 * fd]
    W["even_in_f"] = _pad_cols(ew[:, 3 * C + 3 * fd:], LANES)
    W["even_out"] = full["even_w_out"][0]
    ow = full["odd_w_in"][0]
    inner, cch, Hs = d["ssm_inner"], d["conv_ch"], d["ssm_heads"]
    W["odd_in_z"] = ow[:, :inner]
    W["odd_in_x"] = ow[:, inner:inner + cch]
    W["odd_in_dt"] = _pad_cols(ow[:, inner + cch:], -(-Hs // LANES) * LANES)
    W["odd_out"] = full["odd_w_out"][0]
    W["ffn_up"] = [full["ffn_w_up"][i] for i in range(DEPTH)]
    W["ffn_down"] = [full["ffn_w_down"][i] for i in range(DEPTH)]
    W["ple_proj"] = [full["ple_w_proj"][i] for i in range(DEPTH)]
    W["ple_gate"] = [full["ple_w_gate"][i] for i in range(DEPTH)]
    sp["even_conv_w_il"] = full["even_conv_w"][0]
    sp["even_b_f_pad"] = _pad_cols(full["even_b_f"], LANES)
    sp["odd_conv_w"] = full["odd_conv_w"][0]
    sp["odd_conv_b"] = full["odd_conv_b"][0]
    sp["odd_norm_g"] = full["odd_norm_g"][0]
    for n in ("odd_dt_bias", "odd_a_log", "odd_d_skip"):
        sp[n] = full[n][0]
    for n in ("ln_mix_g", "ln_mix_b", "ln_ffn_g", "ln_ffn_b", "ple_b_gate"):
        sp[n] = full[n]
    sp["ffn_conv_w"] = [full["ffn_conv_w"][i] for i in range(DEPTH)]
    sp["ffn_conv_b"] = [full["ffn_conv_b"][i] for i in range(DEPTH)]
    return W, sp


def _local_step(x, p, target, full):
    W, sp = _prepare_weights(full)
    grads = {n: [None] * _full_shapes()[n][0][0] for n in WEIGHTS}
    pb = p.astype(BF16)
    mix0, sv_e = _even_fwd(x.astype(BF16), W, sp)
    h3_0, h3_0b, sv_t0 = _tail_fwd(0, x, mix0, pb[0], W, sp)
    mix1, sv_o = _odd_fwd(h3_0b, W, sp)
    h3_1, _, sv_t1 = _tail_fwd(1, h3_0, mix1, pb[1], W, sp)
    dh, sq = _loss_head(h3_1, target, "loss_head")
    dr1, dr1b = _tail_bwd(1, dh, sv_t1, W, sp, grads)
    dh = _odd_bwd(dr1b, dr1, sv_o, W, sp, grads)
    dr1, dr1b = _tail_bwd(0, dh, sv_t0, W, sp, grads)
    dx = _even_bwd(dr1b, dr1, sv_e, W, sp, grads)
    grads = {n: jnp.stack(v) for n, v in grads.items()}
    return jnp.sum(sq), dx, grads


def kernel(x, p, even_w_in, even_b_f, even_conv_w, even_w_out, odd_w_in, odd_conv_w, odd_conv_b, odd_dt_bias, odd_a_log, odd_d_skip, odd_norm_g, odd_w_out, ln_mix_g, ln_mix_b, ffn_w_up, ffn_conv_w, ffn_conv_b, ffn_w_down, ln_ffn_g, ln_ffn_b, ple_w_proj, ple_w_gate, ple_b_gate, loss_target, m_even_w_in, m_even_b_f, m_even_conv_w, m_even_w_out, m_odd_w_in, m_odd_conv_w, m_odd_conv_b, m_odd_dt_bias, m_odd_a_log, m_odd_d_skip, m_odd_norm_g, m_odd_w_out, m_ln_mix_g, m_ln_mix_b, m_ffn_w_up, m_ffn_conv_w, m_ffn_conv_b, m_ffn_w_down, m_ln_ffn_g, m_ln_ffn_b, m_ple_w_proj, m_ple_w_gate, m_ple_b_gate, v_even_w_in, v_even_b_f, v_even_conv_w, v_even_w_out, v_odd_w_in, v_odd_conv_w, v_odd_conv_b, v_odd_dt_bias, v_odd_a_log, v_odd_d_skip, v_odd_norm_g, v_odd_w_out, v_ln_mix_g, v_ln_mix_b, v_ffn_w_up, v_ffn_conv_w, v_ffn_conv_b, v_ffn_w_down, v_ln_ffn_g, v_ln_ffn_b, v_ple_w_proj, v_ple_w_gate, v_ple_b_gate):
    args = locals()
    w = {n: args[n] for n in WEIGHTS}
    m = {n: args["m_" + n] for n in WEIGHTS}
    v = {n: args["v_" + n] for n in WEIGHTS}
    me = 4 * lax.axis_index("x") + 2 * lax.axis_index("y") + lax.axis_index("c")

    gathered = _all_gather([_as2d(w[n]).astype(BF16) for n in BIG] + [_pack_small([w[n] for n in SMALL_SHARDED])], "ag_weights")
    full = dict(w)
    for n, g in zip(BIG, gathered[:-1]):
        full[n] = _assemble(g.reshape((N_DEV,) + _shard_shape(n)), n)
    for n, g in zip(SMALL_SHARDED, _unpack_small(gathered[-1], [_shard_shape(n) for n in SMALL_SHARDED])):
        full[n] = _assemble(g, n)

    sq, dx, grads = _local_step(x[0], p[:, 0], loss_target[0], full)
    loss = lax.psum(0.5 * sq / D_MODEL, ("x", "y", "c"))

    gsum_big = _reduce_scatter([_as2d(_split_dest(grads[n], n), 1).astype(BF16) for n in BIG], "rs_grads")
    g_final = {n: g.reshape(_shard_shape(n)) for n, g in zip(BIG, gsum_big)}
    small_names = SMALL_SHARDED + REPLICATED
    (small_all,) = _all_gather([_pack_small([grads[n] for n in small_names])], "ag_small_grads")
    small_sum = _sum_slots(small_all, "sum_small_grads")
    for n, g in zip(small_names, _unpack_small(small_sum, [_full_shapes()[n][0] for n in small_names])):
        g_final[n] = lax.dynamic_index_in_dim(_split_dest(g, n), me, axis=0, keepdims=False) if n in SMALL_SHARDED else g

    out = {}
    for n in BIG:
        res = _adamw(*[_as2d(t[n]) for t in (w, g_final, m, v)], "adamw_" + n)
        out[n] = [r.reshape(_shard_shape(n)) for r in res]
    shapes = [_shard_shape(n) for n in small_names]
    res = _adamw(*[_pack_small([t[n] for n in small_names]) for t in (w, g_final, m, v)], "adamw_small")
    for n, d_, m_, v_ in zip(small_names, *[_unpack_small(r, shapes) for r in res]):
        out[n] = [d_, m_, v_]
    return (loss, dx[None], *[g_final[n] for n in WEIGHTS], *[out[n][0] for n in WEIGHTS],
            *[out[n][1] for n in WEIGHTS], *[out[n][2] for n in WEIGHTS])
```

```python
import jax
import jax.numpy as jnp
import numpy as np
from jax import lax
from jax.experimental import pallas as pl
from jax.experimental.pallas import tpu as pltpu

D_MODEL = 1024
SEQ = 8192
DEPTH = 2
CONV_DIM = 512
CONV_WIDTH = 3
FOX_HEADS = 8
FOX_HEAD_DIM = 64
SSM_HEAD_DIM = 64
SSM_GROUPS = 4
SSM_STATE = 128
SSM_CONV_WIDTH = 4
SSM_CHUNK = 128
D_FF = 2816
FFN_CONV_WIDTH = 3
PLE_DIM = 256
LN_EPS = 1e-5
RMS_EPS = 1e-5
ADAM_LR = 0.001
ADAM_B1 = 0.9
ADAM_B2 = 0.999
ADAM_EPS = 1e-08
ADAM_WD = 0.01
ADAM_STEP = 10
N_DEV = 8

F32 = jnp.float32
BF16 = jnp.bfloat16
NEG = -1e30
LANES = 128
SUBLANES = 8
PACK_W = 1024
VMEM_LIMIT = 48 * 1024 * 1024
ATTN_BWD_VMEM_LIMIT = 56 * 1024 * 1024


def _dims():
    fox_dim = FOX_HEADS * FOX_HEAD_DIM
    ssm_inner = 2 * D_MODEL
    ssm_heads = ssm_inner // SSM_HEAD_DIM
    conv_ch = ssm_inner + 2 * SSM_GROUPS * SSM_STATE
    return dict(fox_dim=fox_dim, even_in=3 * CONV_DIM + 3 * fox_dim + FOX_HEADS, even_mix=CONV_DIM + fox_dim,
                ssm_inner=ssm_inner, ssm_heads=ssm_heads, conv_ch=conv_ch, odd_in=ssm_inner + conv_ch + ssm_heads)


def _alpha():
    return (2.0 * DEPTH) ** 0.25


def _pick(dim, prefs):
    for p in prefs:
        if dim % p == 0:
            return p
    return dim


def _pcall(body, **kw):
    return pl.pallas_call(body, **kw)


def _cparams(sem=None, **kw):
    if sem is not None:
        kw["dimension_semantics"] = sem
    return pltpu.CompilerParams(vmem_limit_bytes=VMEM_LIMIT, **kw)


def _sigmoid(x):
    return 1.0 / (1.0 + jnp.exp(-x))


def _softplus(x):
    return jnp.maximum(x, 0.0) + jnp.log(1.0 + jnp.exp(-jnp.abs(x)))


def _sum8(x):
    n, c = x.shape
    return x.reshape(n // SUBLANES, SUBLANES, c).sum(axis=0)


def _dot(a, b, dims):
    return lax.dot_general(a, b, (dims, ((), ())), preferred_element_type=F32)


NN = ((1,), (0,))
NT = ((1,), (1,))
TN = ((0,), (0,))


def _split3(x):
    hi = x.astype(BF16)
    r1 = x - hi.astype(F32)
    mid = r1.astype(BF16)
    lo = (r1 - mid.astype(F32)).astype(BF16)
    return hi, mid, lo


def _tri_mm(tri_bf16, x, tri_first=True):
    if tri_first:
        return sum(_dot(tri_bf16, part, NN) for part in _split3(x))
    return sum(_dot(part, tri_bf16, NN) for part in _split3(x))


def _tri(n, upper=False):
    r = lax.broadcasted_iota(jnp.int32, (n, n), 0)
    c = lax.broadcasted_iota(jnp.int32, (n, n), 1)
    return jnp.where((r <= c) if upper else (r >= c), 1.0, 0.0).astype(BF16)


def _shift_down(cur, prev8, k):
    if k == 0:
        return cur
    ext = jnp.concatenate([prev8, cur], axis=0)
    return pltpu.roll(ext, k, axis=0)[SUBLANES:]


def _shift_up(cur, next8, k):
    if k == 0:
        return cur
    n = cur.shape[0]
    ext = jnp.concatenate([cur, next8], axis=0)
    return pltpu.roll(ext, n + SUBLANES - k, axis=0)[:n]


def _mm(a, b, mode, out_dtype, name, add=None, add_scale=1.0, b_k_start=0):
    if mode == "nn":
        (M, K), (K2, N) = a.shape, b.shape
    elif mode == "nt":
        (M, K), N = a.shape, b.shape[0]
        K2 = K if b.shape[1] >= b_k_start + K else None
    else:
        (K, M), (K2, N) = a.shape, b.shape
    assert K == K2, (a.shape, b.shape, mode)
    tm = _pick(M, (1024, 1408, 512, 256, 128))
    tn = _pick(N, (1408, 1024, 768, 512, 384, 256, 128))
    tk = K if K <= 2048 and b_k_start % K == 0 else _pick(K, (1408, 1024, 768, 512, 256, 128))
    assert b_k_start % tk == 0
    k0 = b_k_start // tk
    nk = K // tk
    dims = {"nn": NN, "nt": NT, "tn": TN}[mode]
    a_spec = pl.BlockSpec((tk, tm), lambda i, j, k: (k, i)) if mode == "tn" else pl.BlockSpec((tm, tk), lambda i, j, k: (i, k))
    b_spec = pl.BlockSpec((tn, tk), lambda i, j, k: (j, k + k0)) if mode == "nt" else pl.BlockSpec((tk, tn), lambda i, j, k: (k, j))
    o_spec = pl.BlockSpec((tm, tn), lambda i, j, k: (i, j))
    has_add = add is not None

    def body(*refs):
        a_ref, b_ref = refs[:2]
        add_ref = refs[2] if has_add else None
        o_ref = refs[2 + has_add]
        prod = _dot(a_ref[...].astype(BF16), b_ref[...].astype(BF16), dims)
        if nk == 1:
            if has_add:
                prod = prod + add_scale * add_ref[...].astype(F32)
            o_ref[...] = prod.astype(out_dtype)
            return
        acc = refs[3 + has_add]
        k = pl.program_id(2)

        @pl.when(k == 0)
        def _():
            if has_add:
                acc[...] = prod + add_scale * add_ref[...].astype(F32)
            else:
                acc[...] = prod

        @pl.when(k > 0)
        def _():
            acc[...] += prod

        @pl.when(k == nk - 1)
        def _():
            o_ref[...] = acc[...].astype(out_dtype)

    ins = [a, b] + ([add] if has_add else [])
    specs = [a_spec, b_spec] + ([o_spec] if has_add else [])
    return _pcall(body, name=name, grid=(M // tm, N // tn, nk), in_specs=specs, out_specs=o_spec,
                  out_shape=jax.ShapeDtypeStruct((M, N), out_dtype),
                  scratch_shapes=[pltpu.VMEM((tm, tn), F32)] if nk > 1 else [],
                  compiler_params=_cparams(("parallel", "parallel", "arbitrary")))(*ins)


def _row_tile(L):
    return _pick(L, (256, 128))


def _conv_row_tile(L, backward):
    return _pick(L, (512, 256, 128)) if backward else _pick(L, (1024, 512, 256, 128))


def _ln_fwd(h, mix, g, b, name):
    L, D = h.shape
    tl = _row_tile(L)
    alpha = _alpha()

    def body(h_ref, m_ref, g_ref, b_ref, r_ref, y_ref, yb_ref):
        r = alpha * h_ref[...] + m_ref[...]
        mu = jnp.mean(r, axis=-1, keepdims=True)
        xc = r - mu
        var = jnp.mean(xc * xc, axis=-1, keepdims=True)
        r_ref[...] = r
        y = xc * lax.rsqrt(var + LN_EPS) * g_ref[...] + b_ref[...]
        y_ref[...] = y
        yb_ref[...] = y.astype(BF16)

    row = pl.BlockSpec((tl, D), lambda i: (i, 0))
    vec = pl.BlockSpec((1, D), lambda i: (0, 0))
    return _pcall(body, name=name, grid=(L // tl,), in_specs=[row, row, vec, vec], out_specs=[row, row, row],
                  out_shape=[jax.ShapeDtypeStruct((L, D), F32)] * 2 + [jax.ShapeDtypeStruct((L, D), BF16)],
                  compiler_params=_cparams(("parallel",)))(h, mix, g.reshape(1, D), b.reshape(1, D))


def _ln_bwd(r, dy, g, name):
    L, D = r.shape
    tl = _row_tile(L)

    def body(r_ref, dy_ref, g_ref, dr_ref, drb_ref, dg_ref, db_ref):
        i = pl.program_id(0)
        r_ = r_ref[...]
        dy_ = dy_ref[...]
        mu = jnp.mean(r_, axis=-1, keepdims=True)
        xc = r_ - mu
        rstd = lax.rsqrt(jnp.mean(xc * xc, axis=-1, keepdims=True) + LN_EPS)
        xhat = xc * rstd
        dxh = dy_ * g_ref[...]
        dr = rstd * (dxh - jnp.mean(dxh, axis=-1, keepdims=True) - xhat * jnp.mean(dxh * xhat, axis=-1, keepdims=True))
        dr_ref[...] = dr
        drb_ref[...] = dr.astype(BF16)

        @pl.when(i == 0)
        def _():
            dg_ref[...] = jnp.zeros_like(dg_ref)
            db_ref[...] = jnp.zeros_like(db_ref)

        dg_ref[...] += _sum8(dy_ * xhat)
        db_ref[...] += _sum8(dy_)

    row = pl.BlockSpec((tl, D), lambda i: (i, 0))
    vec = pl.BlockSpec((1, D), lambda i: (0, 0))
    acc = pl.BlockSpec((SUBLANES, D), lambda i: (0, 0))
    return _pcall(body, name=name, grid=(L // tl,), in_specs=[row, row, vec], out_specs=[row, row, acc, acc],
                  out_shape=[jax.ShapeDtypeStruct((L, D), F32), jax.ShapeDtypeStruct((L, D), BF16),
                             jax.ShapeDtypeStruct((SUBLANES, D), F32), jax.ShapeDtypeStruct((SUBLANES, D), F32)],
                  compiler_params=_cparams(("arbitrary",)))(r, dy, g.reshape(1, D))


def _ple_fwd(h2, G, bg, E, name):
    L, D = h2.shape
    tl = _row_tile(L)

    def body(h_ref, g_ref, b_ref, e_ref, o_ref, ob_ref):
        o = h_ref[...] + _sigmoid(g_ref[...] + b_ref[...]) * e_ref[...]
        o_ref[...] = o
        ob_ref[...] = o.astype(BF16)

    row = pl.BlockSpec((tl, D), lambda i: (i, 0))
    vec = pl.BlockSpec((1, D), lambda i: (0, 0))
    return _pcall(body, name=name, grid=(L // tl,), in_specs=[row, row, vec, row], out_specs=[row, row],
                  out_shape=[jax.ShapeDtypeStruct((L, D), F32), jax.ShapeDtypeStruct((L, D), BF16)],
                  compiler_params=_cparams(("parallel",)))(h2, G, bg.reshape(1, D), E)


def _ple_bwd(dh3, G, bg, E, name):
    L, D = dh3.shape
    tl = _row_tile(L)

    def body(d_ref, g_ref, b_ref, e_ref, de_ref, dg_ref, db_ref):
        i = pl.program_id(0)
        d = d_ref[...]
        sg = _sigmoid(g_ref[...] + b_ref[...])
        de_ref[...] = (d * sg).astype(BF16)
        dgp = d * e_ref[...] * sg * (1.0 - sg)
        dg_ref[...] = dgp.astype(BF16)

        @pl.when(i == 0)
        def _():
            db_ref[...] = jnp.zeros_like(db_ref)

        db_ref[...] += _sum8(dgp)

    row = pl.BlockSpec((tl, D), lambda i: (i, 0))
    vec = pl.BlockSpec((1, D), lambda i: (0, 0))
    acc = pl.BlockSpec((SUBLANES, D), lambda i: (0, 0))
    return _pcall(body, name=name, grid=(L // tl,), in_specs=[row, row, vec, row], out_specs=[row, row, acc],
                  out_shape=[jax.ShapeDtypeStruct((L, D), BF16), jax.ShapeDtypeStruct((L, D), BF16),
                             jax.ShapeDtypeStruct((SUBLANES, D), F32)],
                  compiler_params=_cparams(("arbitrary",)))(dh3, G, bg.reshape(1, D), E)


def _loss_head(h, target, name):
    L, D = h.shape
    tl = _row_tile(L)

    def body(h_ref, t_ref, d_ref, s_ref):
        i = pl.program_id(0)
        e = h_ref[...] - t_ref[...]
        d_ref[...] = e * (1.0 / D)

        @pl.when(i == 0)
        def _():
            s_ref[...] = jnp.zeros_like(s_ref)

        s_ref[...] += _sum8(e * e)

    row = pl.BlockSpec((tl, D), lambda i: (i, 0))
    acc = pl.BlockSpec((SUBLANES, D), lambda i: (0, 0))
    return _pcall(body, name=name, grid=(L // tl,), in_specs=[row, row], out_specs=[row, acc],
                  out_shape=[jax.ShapeDtypeStruct((L, D), F32), jax.ShapeDtypeStruct((SUBLANES, D), F32)],
                  compiler_params=_cparams(("arbitrary",)))(h, target)


def _halo_prev(tl, ncol_blocks_fn):
    return lambda j, i: (jnp.maximum(i * (tl // SUBLANES) - 1, 0), ncol_blocks_fn(j))


def _conv_taps(cur, prev, w_ref, K):
    acc = w_ref[K - 1:K, :] * cur
    for k in range(K - 1):
        acc = acc + w_ref[k:k + 1, :] * _shift_down(cur, prev, K - 1 - k)
    return acc


def _ffn_act_fwd(U, w, b, name):
    L, F2 = U.shape
    F = F2 // 2
    K = w.shape[0]
    tc = _pick(F, (1408, 256, 128))
    tl = _pick(L, (512, 256, 128)) if tc > 256 else _conv_row_tile(L, False)

    nj = F // tc

    def body(ug_ref, uv_ref, ugp_ref, uvp_ref, wg_ref, wv_ref, bg_ref, bv_ref, s_ref):
        i = pl.program_id(1)
        g = _conv_taps(ug_ref[...], jnp.where(i == 0, 0.0, ugp_ref[...]), wg_ref, K) + bg_ref[...]
        v = _conv_taps(uv_ref[...], jnp.where(i == 0, 0.0, uvp_ref[...]), wv_ref, K) + bv_ref[...]
        s_ref[...] = (g * _sigmoid(g) * v).astype(BF16)

    def both(shape, index):
        return [pl.BlockSpec(shape, lambda j, i: index(j, i)), pl.BlockSpec(shape, lambda j, i: index(j + nj, i))]

    b2 = b.reshape(1, F2)
    return _pcall(body, name=name, grid=(nj, L // tl),
                  in_specs=both((tl, tc), lambda j, i: (i, j)) + both((SUBLANES, tc), _halo_prev(tl, lambda j: j))
                  + both((K, tc), lambda j, i: (0, j)) + both((1, tc), lambda j, i: (0, j)),
                  out_specs=pl.BlockSpec((tl, tc), lambda j, i: (i, j)),
                  out_shape=jax.ShapeDtypeStruct((L, F), BF16),
                  compiler_params=_cparams(("parallel", "parallel")))(U, U, U, U, w, w, b2, b2)


def _halo_next(tl, L, rows):
    return lambda j, i: (jnp.minimum((i + 1) * (tl // rows), L // rows - 1), j)


def _conv_taps_t(cur, nxt, w_ref, K):
    acc = w_ref[K - 1:K, :] * cur
    for k in range(K - 1):
        acc = acc + w_ref[k:k + 1, :] * _shift_up(cur, nxt, K - 1 - k)
    return acc


BF16_ROWS = 16


def _ffn_act_bwd(U, dS, w, b, name):
    L, F2 = U.shape
    F = F2 // 2
    K = w.shape[0]
    tc = _pick(F, (1408, 256, 128))
    tl = _pick(L, (256, 128)) if tc > 256 else _conv_row_tile(L, True)
    nl = L // tl

    nj = F // tc

    def body(ug_ref, uv_ref, ugp_ref, uvp_ref, ugn_ref, uvn_ref, ds_ref, dsn_ref, wg_ref, wv_ref, bg_ref, bv_ref,
             dug_ref, duv_ref, dwg_ref, dwv_ref, dbg_ref, dbv_ref):
        i = pl.program_id(1)
        cur = (ug_ref[...], uv_ref[...])
        prev = (jnp.where(i == 0, 0.0, ugp_ref[...]), jnp.where(i == 0, 0.0, uvp_ref[...]))

        def at_conv_out(x, xprev, ds):
            g = _conv_taps(x[0], xprev[0], wg_ref, K) + bg_ref[...]
            v = _conv_taps(x[1], xprev[1], wv_ref, K) + bv_ref[...]
            sg = _sigmoid(g)
            return ds * v * sg * (1.0 + g * (1.0 - sg)), ds * g * sg

        duc = at_conv_out(cur, prev, ds_ref[...].astype(F32))
        duc_n = at_conv_out((ugn_ref[...], uvn_ref[...]), (cur[0][tl - SUBLANES:], cur[1][tl - SUBLANES:]),
                            dsn_ref[...].astype(F32)[:SUBLANES])

        @pl.when(i == 0)
        def _():
            for r in (dwg_ref, dwv_ref, dbg_ref, dbv_ref):
                r[...] = jnp.zeros_like(r)

        halves = ((dug_ref, dwg_ref, dbg_ref, wg_ref), (duv_ref, dwv_ref, dbv_ref, wv_ref))
        for half, (du_ref, dw_ref, db_ref, w_ref) in enumerate(halves):
            d = duc[half]
            du_ref[...] = _conv_taps_t(d, jnp.where(i == nl - 1, 0.0, duc_n[half]), w_ref, K).astype(BF16)
            db_ref[...] += _sum8(d)
            for k in range(K):
                dw_ref[k * SUBLANES:(k + 1) * SUBLANES, :] += _sum8(d * _shift_down(cur[half], prev[half], K - 1 - k))

    def both(shape, index):
        return [pl.BlockSpec(shape, lambda j, i: index(j, i)), pl.BlockSpec(shape, lambda j, i: index(j + nj, i))]

    b2 = b.reshape(1, F2)
    du_specs, du_shapes = [pl.BlockSpec((tl, tc), lambda j, i: (i, j))] * 2, [jax.ShapeDtypeStruct((L, F), BF16)] * 2
    dw_specs = [pl.BlockSpec((K * SUBLANES, tc), lambda j, i: (0, j))] * 2
    dw_shapes = [jax.ShapeDtypeStruct((K * SUBLANES, F), F32)] * 2
    db_specs, db_shapes = [pl.BlockSpec((SUBLANES, tc), lambda j, i: (0, j))] * 2, [jax.ShapeDtypeStruct((SUBLANES, F), F32)] * 2
    return _pcall(body, name=name, grid=(nj, nl),
                  in_specs=both((tl, tc), lambda j, i: (i, j)) + both((SUBLANES, tc), _halo_prev(tl, lambda j: j))
                  + both((SUBLANES, tc), _halo_next(tl, L, SUBLANES))
                  + [pl.BlockSpec((tl, tc), lambda j, i: (i, j)), pl.BlockSpec((BF16_ROWS, tc), _halo_next(tl, L, BF16_ROWS))]
                  + both((K, tc), lambda j, i: (0, j)) + both((1, tc), lambda j, i: (0, j)),
                  out_specs=du_specs + dw_specs + db_specs, out_shape=du_shapes + dw_shapes + db_shapes,
                  compiler_params=_cparams(("parallel", "arbitrary")))(U, U, U, U, U, U, dS, dS, w, w, b2, b2)


def _sconv_fwd(Ac, w, name):
    L, C3 = Ac.shape
    C = C3 // 3
    K = w.shape[0]
    tl = _conv_row_tile(L, False)
    tc = LANES

    def body(a_ref, ap_ref, w_ref, y_ref):
        i = pl.program_id(1)
        a = a_ref[...]
        ap = ap_ref[...]
        p = a[:, tc:2 * tc] * a[:, 2 * tc:]
        pp = jnp.where(i == 0, 0.0, ap[:, tc:2 * tc] * ap[:, 2 * tc:])
        y_ref[...] = (a[:, :tc] * _conv_taps(p, pp, w_ref, K)).astype(BF16)

    return _pcall(body, name=name, grid=(C // tc, L // tl),
                  in_specs=[pl.BlockSpec((tl, 3 * tc), lambda j, i: (i, j)),
                            pl.BlockSpec((SUBLANES, 3 * tc), _halo_prev(tl, lambda j: j)),
                            pl.BlockSpec((K, tc), lambda j, i: (0, j))],
                  out_specs=pl.BlockSpec((tl, tc), lambda j, i: (i, j)),
                  out_shape=jax.ShapeDtypeStruct((L, C), BF16),
                  compiler_params=_cparams(("parallel", "parallel")))(Ac, Ac, w)


def _sconv_bwd_dc(Ac, dy, w, name):
    L, C3 = Ac.shape
    C = C3 // 3
    K = w.shape[0]
    tl = _conv_row_tile(L, False)
    tc = LANES

    def body(a_ref, ap_ref, dy_ref, dc_ref, dw_ref):
        i = pl.program_id(1)
        a = a_ref[...]
        ap = ap_ref[...]
        p = a[:, tc:2 * tc] * a[:, 2 * tc:]
        pp = jnp.where(i == 0, 0.0, ap[:, tc:2 * tc] * ap[:, 2 * tc:])
        dc = dy_ref[...] * a[:, :tc]
        dc_ref[...] = dc

        @pl.when(i == 0)
        def _():
            dw_ref[...] = jnp.zeros_like(dw_ref)

        for k in range(K):
            dw_ref[k * SUBLANES:(k + 1) * SUBLANES, :] += _sum8(dc * _shift_down(p, pp, K - 1 - k))

    return _pcall(body, name=name, grid=(C // tc, L // tl),
                  in_specs=[pl.BlockSpec((tl, 3 * tc), lambda j, i: (i, j)),
                            pl.BlockSpec((SUBLANES, 3 * tc), _halo_prev(tl, lambda j: j)),
                            pl.BlockSpec((tl, tc), lambda j, i: (i, j))],
                  out_specs=[pl.BlockSpec((tl, tc), lambda j, i: (i, j)),
                             pl.BlockSpec((K * SUBLANES, tc), lambda j, i: (0, j))],
                  out_shape=[jax.ShapeDtypeStruct((L, C), F32), jax.ShapeDtypeStruct((K * SUBLANES, C), F32)],
                  compiler_params=_cparams(("parallel", "arbitrary")))(Ac, Ac, dy)


def _sconv_bwd_da(Ac, dy, dc, w, name):
    L, C3 = Ac.shape
    C = C3 // 3
    K = w.shape[0]
    tl = _conv_row_tile(L, False)
    tc = LANES
    nl = L // tl

    def body(a_ref, ap_ref, dy_ref, dc_ref, dcn_ref, w_ref, o_ref):
        i = pl.program_id(1)
        a = a_ref[...]
        ap = ap_ref[...]
        gc, h = a[:, tc:2 * tc], a[:, 2 * tc:]
        p = gc * h
        pp = jnp.where(i == 0, 0.0, ap[:, tc:2 * tc] * ap[:, 2 * tc:])
        dgb = dy_ref[...] * _conv_taps(p, pp, w_ref, K)
        cur = dc_ref[...]
        nxt = jnp.where(i == nl - 1, 0.0, dcn_ref[...])
        dp = w_ref[K - 1:K, :] * cur
        for k in range(K - 1):
            dp = dp + w_ref[k:k + 1, :] * _shift_up(cur, nxt, K - 1 - k)
        o_ref[...] = jnp.concatenate([dgb, dp * h, dp * gc], axis=1).astype(BF16)

    return _pcall(body, name=name, grid=(C // tc, nl),
                  in_specs=[pl.BlockSpec((tl, 3 * tc), lambda j, i: (i, j)),
                            pl.BlockSpec((SUBLANES, 3 * tc), _halo_prev(tl, lambda j: j)),
                            pl.BlockSpec((tl, tc), lambda j, i: (i, j)),
                            pl.BlockSpec((tl, tc), lambda j, i: (i, j)),
                            pl.BlockSpec((SUBLANES, tc), lambda j, i: (jnp.minimum((i + 1) * (tl // SUBLANES), L // SUBLANES - 1), j)),
                            pl.BlockSpec((K, tc), lambda j, i: (0, j))],
                  out_specs=pl.BlockSpec((tl, 3 * tc), lambda j, i: (i, j)),
                  out_shape=jax.ShapeDtypeStruct((L, C3), BF16),
                  compiler_params=_cparams(("parallel", "parallel")))(Ac, Ac, dy, dc, dc, w)


def _fox_gate_fwd(Af, bf, name):
    L, W = Af.shape
    tl = _pick(L, (512, 256, 128))

    def body(a_ref, b_ref, f_ref, carry):
        i = pl.program_id(0)

        @pl.when(i == 0)
        def _():
            carry[...] = jnp.zeros_like(carry)

        z = a_ref[...] + b_ref[...]
        logf = jnp.minimum(z, 0.0) - jnp.log(1.0 + jnp.exp(-jnp.abs(z)))
        f = _tri_mm(_tri(tl), logf) + carry[...]
        f_ref[...] = f
        carry[...] = f[tl - 1:tl, :]

    row = pl.BlockSpec((tl, W), lambda i: (i, 0))
    return _pcall(body, name=name, grid=(L // tl,), in_specs=[row, pl.BlockSpec((1, W), lambda i: (0, 0))], out_specs=row,
                  out_shape=jax.ShapeDtypeStruct((L, W), F32), scratch_shapes=[pltpu.VMEM((1, W), F32)],
                  compiler_params=_cparams(("arbitrary",)))(Af, bf)


def _fox_gate_bwd(Af, bf, dF, name):
    L, W = Af.shape
    tl = _pick(L, (512, 256, 128))
    nl = L // tl

    def body(a_ref, b_ref, df_ref, o_ref, db_ref, carry):
        i = pl.program_id(0)

        @pl.when(i == 0)
        def _():
            carry[...] = jnp.zeros_like(carry)
            db_ref[...] = jnp.zeros_like(db_ref)

        z = a_ref[...] + b_ref[...]
        dlogf = _tri_mm(_tri(tl, upper=True), df_ref[...]) + carry[...]
        carry[...] = dlogf[0:1, :]
        dz = dlogf * _sigmoid(-z)
        o_ref[...] = dz
        db_ref[...] += _sum8(dz)

    row = pl.BlockSpec((tl, W), lambda i: (nl - 1 - i, 0))
    return _pcall(body, name=name, grid=(nl,),
                  in_specs=[row, pl.BlockSpec((1, W), lambda i: (0, 0)), row],
                  out_specs=[row, pl.BlockSpec((SUBLANES, W), lambda i: (0, 0))],
                  out_shape=[jax.ShapeDtypeStruct((L, W), F32), jax.ShapeDtypeStruct((SUBLANES, W), F32)],
                  scratch_shapes=[pltpu.VMEM((1, W), F32)],
                  compiler_params=_cparams(("arbitrary",)))(Af, bf, dF)


def _attn_tiles(L):
    t = _pick(L, (512, 256, 128))
    return t, t


def _attn_scores(q, k, fq, fk, diag, scale):
    s = _dot(q, k, NT) * scale + (fq - fk)
    if not diag:
        return s
    row = lax.broadcasted_iota(jnp.int32, s.shape, 0)
    col = lax.broadcasted_iota(jnp.int32, s.shape, 1)
    return jnp.where(col <= row, s, NEG)


def _attn_geometry():
    Dh = FOX_HEAD_DIM
    hpt = LANES // Dh
    return Dh, hpt, FOX_HEADS // hpt


def _head_lanes(shape, Dh, hpt):
    lane = lax.broadcasted_iota(jnp.int32, shape, len(shape) - 1)
    return [(lane >= h * Dh) & (lane < (h + 1) * Dh) for h in range(hpt)]


def _attn_specs(t, L, hpt, ng):
    return dict(
        col=lambda off: pl.BlockSpec((t, LANES), lambda g, i: (i, g + off)),
        full=lambda off: pl.BlockSpec((L, LANES), lambda g, i: (0, g + off)),
        hq=pl.BlockSpec((hpt, t, 1), lambda g, i: (g, i, 0)),
        hk_full=pl.BlockSpec((hpt, 1, L), lambda g, i: (g, 0, 0)),
        hk=pl.BlockSpec((hpt, 1, t), lambda g, i: (g, 0, i)))


def _attn_fwd(qkv, Fq, Fk, name, gather=()):
    L = qkv.shape[0]
    Dh, hpt, ng = _attn_geometry()
    t, _ = _attn_tiles(L)
    scale = Dh ** -0.5
    n = len(gather)
    nsteps = ng * (L // t)

    def body(*refs):
        q_ref, k_ref, v_ref, fq_ref, fk_ref = refs[:5]
        o_ref, lse_ref = refs[5 + n:7 + n]
        qi = pl.program_id(1)
        step = pl.program_id(0) * (L // t) + qi
        if n:
            start, forward, finish = _gather_phases(refs[5:5 + n], refs[7 + n:7 + 2 * n], *refs[7 + 2 * n:])
            pl.when(step == 0)(start)
            pl.when(step == nsteps // 2)(forward)
        sel = _head_lanes((t, LANES), Dh, hpt)
        q2 = q_ref[...]
        qh = [jnp.where(sel[h], q2, 0) for h in range(hpt)]
        fq = [fq_ref[h] for h in range(hpt)]

        def chunk(j, carry, diag):
            rows = pl.ds(pl.multiple_of(j * t, t), t)
            kc, vc = k_ref[rows, :], v_ref[rows, :]
            out = []
            for h in range(hpt):
                m, l, acc = carry[h]
                s = _attn_scores(qh[h], kc, fq[h], fk_ref[h, :, rows], diag, scale)
                m_new = jnp.maximum(m, jnp.max(s, axis=-1, keepdims=True))
                p = jnp.exp(s - m_new)
                a = jnp.exp(m - m_new)
                out.append((m_new, a * l + jnp.sum(p, axis=-1, keepdims=True), a * acc + _dot(p.astype(BF16), vc, NN)))
            return tuple(out)

        init = tuple((jnp.full((t, 1), NEG, F32), jnp.zeros((t, 1), F32), jnp.zeros((t, LANES), F32)) for _ in range(hpt))
        fin = chunk(qi, lax.fori_loop(0, qi, lambda j, c: chunk(j, c, False), init), True)
        o = jnp.zeros((t, LANES), F32)
        for h, (m, l, acc) in enumerate(fin):
            o = jnp.where(sel[h], acc / l, o)
            lse_ref[h] = m + jnp.log(l)
        o_ref[...] = o.astype(BF16)
        if n:
            pl.when(step == nsteps - 1)(finish)

    sp = _attn_specs(t, L, hpt, ng)
    return _pcall(body, name=name, grid=(ng, L // t),
                  in_specs=[sp["col"](0), sp["full"](ng), sp["full"](2 * ng), sp["hq"], sp["hk_full"]] + [HBM_SPEC] * n,
                  out_specs=[sp["col"](0), sp["hq"]] + [HBM_SPEC] * n,
                  out_shape=[jax.ShapeDtypeStruct((L, ng * LANES), BF16), jax.ShapeDtypeStruct((FOX_HEADS, L, 1), F32)]
                  + _gather_shapes(gather),
                  scratch_shapes=_gather_sems(n) if n else [],
                  compiler_params=_cparams(("arbitrary", "arbitrary") if n else ("parallel", "arbitrary")))(
        qkv, qkv, qkv, Fq, Fk, *gather)


def _attn_bwd(qkv, Fq, Fk, lse, do, name, exchange=()):
    L = qkv.shape[0]
    Dh, hpt, ng = _attn_geometry()
    _, tk = _attn_tiles(L)
    tq = _pick(L, (256, 128))
    nkc = L // tk
    scale = Dh ** -0.5

    n = len(exchange)
    nsteps = ng * (L // tq)

    def body(*refs):
        q_ref, k_ref, v_ref, fq_ref, fk_ref, lse_ref, do_ref = refs[:7]
        dq_ref, dk_ref, dv_ref, df_ref = refs[7 + n:11 + n]
        p_s, dp_s = refs[11 + 2 * n:13 + 2 * n]
        qi = pl.program_id(1)
        step = pl.program_id(0) * (L // tq) + qi
        if n:
            start, finish = _exchange_phases(refs[7:7 + n], refs[11 + n:11 + 2 * n], *refs[13 + 2 * n:])
            pl.when(step == 0)(start)

        @pl.when(qi == 0)
        def _():
            dk_ref[...] = jnp.zeros_like(dk_ref)
            dv_ref[...] = jnp.zeros_like(dv_ref)
            df_ref[...] = jnp.zeros_like(df_ref)

        sel = _head_lanes((tq, LANES), Dh, hpt)
        q2, do2 = q_ref[...], do_ref[...]
        jd = (qi * tq) // tk
        off = qi * tq - jd * tk
        dq = jnp.zeros((tq, LANES), F32)
        for h in range(hpt):
            qh, doh = jnp.where(sel[h], q2, 0), jnp.where(sel[h], do2, 0)
            fq, lse = fq_ref[h], lse_ref[h]

            def first(j, acc, diag):
                rows = pl.ds(pl.multiple_of(j * tk, tk), tk)
                s = _dot(qh, k_ref[rows, :], NT) * scale + (fq - fk_ref[h, :, rows])
                if diag:
                    row = lax.broadcasted_iota(jnp.int32, s.shape, 0) + off
                    s = jnp.where(lax.broadcasted_iota(jnp.int32, s.shape, 1) <= row, s, NEG)
                p = jnp.exp(s - lse)
                dp = _dot(doh, v_ref[rows, :], NT)
                p_s[j] = p
                dp_s[j] = dp
                return acc + jnp.sum(p * dp, axis=-1, keepdims=True)

            delta = first(jd, lax.fori_loop(0, jd, lambda j, c: first(j, c, False), jnp.zeros((tq, 1), F32)), True)

            def second(j, acc):
                rows = pl.ds(pl.multiple_of(j * tk, tk), tk)
                p = p_s[j]
                ds = p * (dp_s[j] - delta)
                dsb = ds.astype(BF16)
                dk_ref[rows, :] += _dot(dsb, qh, TN)
                dv_ref[rows, :] += _dot(p.astype(BF16), doh, TN)
                df_ref[h, :, rows] -= jnp.sum(ds, axis=0, keepdims=True)
                return acc + _dot(dsb, k_ref[rows, :], NN)

            dq = jnp.where(sel[h], lax.fori_loop(0, jd + 1, second, jnp.zeros((tq, LANES), F32)), dq)
        dq_ref[...] = (dq * scale).astype(BF16)

        @pl.when(qi == L // tq - 1)
        def _():
            dk_ref[...] *= scale

        if n:
            pl.when(step == nsteps - 1)(finish)

    sp = _attn_specs(tq, L, hpt, ng)
    return _pcall(body, name=name, grid=(ng, L // tq),
                  in_specs=[sp["col"](0), sp["full"](ng), sp["full"](2 * ng), sp["hq"], sp["hk_full"], sp["hq"], sp["col"](0)]
                  + [HBM_SPEC] * n,
                  out_specs=[sp["col"](0), sp["full"](0), sp["full"](0), sp["hk_full"]] + [HBM_SPEC] * n,
                  out_shape=[jax.ShapeDtypeStruct((L, ng * LANES), BF16), jax.ShapeDtypeStruct((L, ng * LANES), F32),
                             jax.ShapeDtypeStruct((L, ng * LANES), F32), jax.ShapeDtypeStruct((FOX_HEADS, 1, L), F32)]
                  + [jax.ShapeDtypeStruct(g.shape, g.dtype) for g in exchange],
                  scratch_shapes=[pltpu.VMEM((nkc, tq, tk), F32), pltpu.VMEM((nkc, tq, tk), F32)] + (_gather_sems(n) if n else []),
                  compiler_params=pltpu.CompilerParams(
                      vmem_limit_bytes=ATTN_BWD_VMEM_LIMIT,
                      dimension_semantics=("arbitrary", "arbitrary") if n else ("parallel", "arbitrary")))(
        qkv, qkv, qkv, Fq, Fk, lse, do, *exchange)


def _mconv_fwd(xr, w, b, name):
    L, C = xr.shape
    K = w.shape[0]
    tl = _conv_row_tile(L, False)
    tc = _pick(C, (512, 384, 256, 128))

    def body(x_ref, xp_ref, w_ref, b_ref, o_ref):
        i = pl.program_id(1)
        prev = jnp.where(i == 0, 0.0, xp_ref[...])
        pre = _conv_taps(x_ref[...], prev, w_ref, K) + b_ref[...]
        o_ref[...] = pre * _sigmoid(pre)

    return _pcall(body, name=name, grid=(C // tc, L // tl),
                  in_specs=[pl.BlockSpec((tl, tc), lambda j, i: (i, j)),
                            pl.BlockSpec((SUBLANES, tc), _halo_prev(tl, lambda j: j)),
                            pl.BlockSpec((K, tc), lambda j, i: (0, j)),
                            pl.BlockSpec((1, tc), lambda j, i: (0, j))],
                  out_specs=pl.BlockSpec((tl, tc), lambda j, i: (i, j)),
                  out_shape=jax.ShapeDtypeStruct((L, C), F32),
                  compiler_params=_cparams(("parallel", "parallel")))(xr, xr, w, b.reshape(1, C))


def _mconv_bwd(xr, dact, w, b, name):
    L, C = xr.shape
    K = w.shape[0]
    tl = _conv_row_tile(L, True)
    tc = _pick(C, (512, 384, 256, 128))
    nl = L // tl

    def body(x_ref, xp_ref, xn_ref, d_ref, dn_ref, w_ref, b_ref, o_ref, dw_ref, db_ref):
        i = pl.program_id(1)
        cur = x_ref[...]
        prev = jnp.where(i == 0, 0.0, xp_ref[...])

        def at_conv_out(x, xprev, d):
            pre = _conv_taps(x, xprev, w_ref, K) + b_ref[...]
            sg = _sigmoid(pre)
            return d * sg * (1.0 + pre * (1.0 - sg))

        dpre = at_conv_out(cur, prev, d_ref[...])
        dpre_n = jnp.where(i == nl - 1, 0.0, at_conv_out(xn_ref[...], cur[tl - SUBLANES:], dn_ref[...]))
        o_ref[...] = _conv_taps_t(dpre, dpre_n, w_ref, K).astype(BF16)

        @pl.when(i == 0)
        def _():
            dw_ref[...] = jnp.zeros_like(dw_ref)
            db_ref[...] = jnp.zeros_like(db_ref)

        db_ref[...] += _sum8(dpre)
        for k in range(K):
            dw_ref[k * SUBLANES:(k + 1) * SUBLANES, :] += _sum8(dpre * _shift_down(cur, prev, K - 1 - k))

    return _pcall(body, name=name, grid=(C // tc, nl),
                  in_specs=[pl.BlockSpec((tl, tc), lambda j, i: (i, j)),
                            pl.BlockSpec((SUBLANES, tc), _halo_prev(tl, lambda j: j)),
                            pl.BlockSpec((SUBLANES, tc), _halo_next(tl, L, SUBLANES)),
                            pl.BlockSpec((tl, tc), lambda j, i: (i, j)),
                            pl.BlockSpec((SUBLANES, tc), _halo_next(tl, L, SUBLANES)),
                            pl.BlockSpec((K, tc), lambda j, i: (0, j)),
                            pl.BlockSpec((1, tc), lambda j, i: (0, j))],
                  out_specs=[pl.BlockSpec((tl, tc), lambda j, i: (i, j)),
                             pl.BlockSpec((K * SUBLANES, tc), lambda j, i: (0, j)),
                             pl.BlockSpec((SUBLANES, tc), lambda j, i: (0, j))],
                  out_shape=[jax.ShapeDtypeStruct((L, C), BF16), jax.ShapeDtypeStruct((K * SUBLANES, C), F32),
                             jax.ShapeDtypeStruct((SUBLANES, C), F32)],
                  compiler_params=_cparams(("parallel", "arbitrary")))(xr, xr, xr, dact, dact, w, b.reshape(1, C))


def _head_selector(R, P, heads_first):
    shape = (R, R * P) if heads_first else (R * P, R)
    head = lax.broadcasted_iota(jnp.int32, shape, 0 if heads_first else 1)
    lane = lax.broadcasted_iota(jnp.int32, shape, 1 if heads_first else 0)
    d = lane - head * P
    return jnp.where((d >= 0) & (d < P), 1.0, 0.0).astype(BF16)


def _ssd_prelude(dtc_ref, dtr_ref, bc_ref, br_ref, ac_ref, ar_ref, Q, R, P):
    raw_c = dtc_ref[...] + bc_ref[...]
    dt_c = _softplus(raw_c)
    dt_r = _softplus(dtr_ref[...] + br_ref[...])
    A_c = -jnp.exp(ac_ref[...])
    acs_c = _tri_mm(_tri(Q), dt_c * A_c)
    acs_r = _tri_mm(_tri(Q, upper=True), dt_r * (-jnp.exp(ar_ref[...])), tri_first=False)
    ea_c = jnp.exp(acs_c)
    dte_c = jnp.exp(acs_c[Q - 1:Q, :] - acs_c)
    wide = _tri_mm(_head_selector(R, P, True), jnp.concatenate([dt_c, ea_c, dte_c], axis=0), tri_first=False)
    return dict(raw_c=raw_c, dt_c=dt_c, A_c=A_c, acs_c=acs_c, acs_r=acs_r, ea_c=ea_c,
                DT=wide[:Q], EA=wide[Q:2 * Q], DTE=wide[2 * Q:])


def _ssd_decay_tile(pre, r, mask):
    return jnp.exp(jnp.where(mask, pre["acs_c"][:, r:r + 1] - pre["acs_r"][r:r + 1, :], NEG))


def _ssd_specs(Q, R, P, N, G, inner, rev=None):
    cc = (lambda c: c) if rev is None else rev
    return dict(
        x=pl.BlockSpec((Q, R * P), lambda g, c: (cc(c), g)),
        b=pl.BlockSpec((Q, N), lambda g, c: (cc(c), inner // N + g)),
        c=pl.BlockSpec((Q, N), lambda g, c: (cc(c), inner // N + G + g)),
        dtc=pl.BlockSpec((None, Q, R), lambda g, c: (g, cc(c), 0)),
        dtr=pl.BlockSpec((None, R, Q), lambda g, c: (g, 0, cc(c))),
        pc=pl.BlockSpec((None, 1, R), lambda g, c: (g, 0, 0)),
        pr=pl.BlockSpec((None, R, 1), lambda g, c: (g, 0, 0)),
        px=pl.BlockSpec((None, 1, R * P), lambda g, c: (g, 0, 0)),
        st=pl.BlockSpec((None, None, N, R * P), lambda g, c: (cc(c), g, 0, 0)))


def _ssd_fwd(act, dtc, dtr, bias_c, bias_r, alog_c, alog_r, dsk_x, name):
    G, L, R = dtc.shape
    N, P, Q = SSM_STATE, SSM_HEAD_DIM, SSM_CHUNK
    RP = R * P
    inner = G * RP
    nc = L // Q

    def body(x_ref, b_ref, c_ref, dtc_ref, dtr_ref, bc_ref, br_ref, ac_ref, ar_ref, dk_ref, y_ref, hp_ref, st):
        c = pl.program_id(1)

        @pl.when(c == 0)
        def _():
            st[...] = jnp.zeros_like(st)

        pre = _ssd_prelude(dtc_ref, dtr_ref, bc_ref, br_ref, ac_ref, ar_ref, Q, R, P)
        X = x_ref[...]
        XT = X * pre["DT"]
        Bb = b_ref[...].astype(BF16)
        Cb = c_ref[...].astype(BF16)
        CB = _dot(Cb, Bb, NT)
        mask = lax.broadcasted_iota(jnp.int32, (Q, Q), 0) >= lax.broadcasted_iota(jnp.int32, (Q, Q), 1)
        low = lax.broadcasted_iota(jnp.int32, (Q, 2 * P), 1) < P
        pieces = []
        for k in range(R // 2):
            xt2 = XT[:, 2 * P * k:2 * P * (k + 1)]
            acc = None
            for half in range(2):
                Gm = CB * _ssd_decay_tile(pre, 2 * k + half, mask)
                part = _dot(Gm.astype(BF16), jnp.where(low == (half == 0), xt2, 0.0).astype(BF16), NN)
                acc = part if acc is None else acc + part
            pieces.append(acc)
        HP = st[...]
        hp_ref[...] = HP
        yoff = pre["EA"] * _dot(Cb, HP.astype(BF16), NN)
        st[...] = HP * pre["EA"][Q - 1:Q, :] + _dot(Bb, (XT * pre["DTE"]).astype(BF16), TN)
        y_ref[...] = jnp.concatenate(pieces, axis=1) + yoff + dk_ref[...] * X

    sp = _ssd_specs(Q, R, P, N, G, inner)
    return _pcall(body, name=name, grid=(G, nc),
                  in_specs=[sp["x"], sp["b"], sp["c"], sp["dtc"], sp["dtr"], sp["pc"], sp["pr"], sp["pc"], sp["pr"], sp["px"]],
                  out_specs=[sp["x"], sp["st"]],
                  out_shape=[jax.ShapeDtypeStruct((L, inner), F32), jax.ShapeDtypeStruct((nc, G, N, RP), F32)],
                  scratch_shapes=[pltpu.VMEM((N, RP), F32)],
                  compiler_params=_cparams(("parallel", "arbitrary")))(act, act, act, dtc, dtr, bias_c, bias_r, alog_c, alog_r, dsk_x)


def _ssd_bwd(act, dtc, dtr, bias_c, bias_r, alog_c, alog_r, dsk_x, hprev, dy, name):
    G, L, R = dtc.shape
    N, P, Q = SSM_STATE, SSM_HEAD_DIM, SSM_CHUNK
    RP = R * P
    inner = G * RP
    nc = L // Q

    def body(x_ref, b_ref, c_ref, dtc_ref, dtr_ref, bc_ref, br_ref, ac_ref, ar_ref, dk_ref, hp_ref, dy_ref,
             dx_ref, db_ref, dc_ref, ddt_ref, gbias_ref, galog_ref, gdsk_ref, dst):
        c = pl.program_id(1)

        @pl.when(c == 0)
        def _():
            dst[...] = jnp.zeros_like(dst)
            gbias_ref[...] = jnp.zeros_like(gbias_ref)
            galog_ref[...] = jnp.zeros_like(galog_ref)
            gdsk_ref[...] = jnp.zeros_like(gdsk_ref)

        pre = _ssd_prelude(dtc_ref, dtr_ref, bc_ref, br_ref, ac_ref, ar_ref, Q, R, P)
        DT, EA, DTE = pre["DT"], pre["EA"], pre["DTE"]
        E_END = EA[Q - 1:Q, :]
        X, DY = x_ref[...], dy_ref[...]
        XT = X * DT
        Bb = b_ref[...].astype(BF16)
        Cb = c_ref[...].astype(BF16)
        CB = _dot(Cb, Bb, NT)
        HP, dH = hp_ref[...], dst[...]
        HPb, dHb = HP.astype(BF16), dH.astype(BF16)
        EDY = EA * DY
        EDYb = EDY.astype(BF16)
        dC = _dot(EDYb, HPb, NT)
        dHP = _dot(Cb, EDYb, TN)
        da_off = EDY * _dot(Cb, HPb, NN)
        Z = _dot(Bb, dHb, NN)
        XD = XT * DTE
        dB = _dot(XD.astype(BF16), dHb, NT)
        dXT = DTE * Z
        t_x = XD * Z
        hh = jnp.sum(dH * HP, axis=0, keepdims=True) * E_END
        dst[...] = dHP + dH * E_END
        mask = lax.broadcasted_iota(jnp.int32, (Q, Q), 0) >= lax.broadcasted_iota(jnp.int32, (Q, Q), 1)
        eye = lax.broadcasted_iota(jnp.int32, (Q, Q), 0) == lax.broadcasted_iota(jnp.int32, (Q, Q), 1)
        low = lax.broadcasted_iota(jnp.int32, (Q, 2 * P), 1) < P
        lane = lax.broadcasted_iota(jnp.int32, (Q, R), 1)
        dCB = jnp.zeros((Q, Q), F32)
        da_mat = jnp.zeros((Q, R), F32)
        pieces = []
        for k in range(R // 2):
            sl = slice(2 * P * k, 2 * P * (k + 1))
            xt2, dy2 = XT[:, sl], DY[:, sl]
            acc = None
            for half in range(2):
                r = 2 * k + half
                sel = low == (half == 0)
                Lm = _ssd_decay_tile(pre, r, mask)
                Gm = CB * Lm
                dyb = jnp.where(sel, dy2, 0.0).astype(BF16)
                part = _dot(Gm.astype(BF16), dyb, TN)
                acc = part if acc is None else acc + part
                dG = jnp.where(mask, _dot(dyb, jnp.where(sel, xt2, 0.0).astype(BF16), NT), 0.0)
                Mm = dG * Gm
                dCB = dCB + dG * Lm
                colsum = jnp.sum(jnp.where(eye, jnp.sum(Mm, axis=0, keepdims=True), 0.0), axis=1, keepdims=True)
                da_mat = jnp.where(lane == r, jnp.sum(Mm, axis=1, keepdims=True) - colsum, da_mat)
            pieces.append(acc)
        dXT = dXT + jnp.concatenate(pieces, axis=1)
        dCBb = dCB.astype(BF16)
        dc_ref[...] = dC + _dot(dCBb, Bb, NN)
        db_ref[...] = dB + _dot(dCBb, Cb, TN)
        dx_ref[...] = dXT * DT + dk_ref[...] * DY
        pad = jnp.zeros((SUBLANES - 1, RP), F32)
        sums = _tri_mm(_head_selector(R, P, False), jnp.concatenate([da_off, t_x, dXT * X, DY * X, hh, pad], axis=0), tri_first=False)
        t = sums[Q:2 * Q]
        da_end = jnp.sum(t, axis=0, keepdims=True) + sums[4 * Q:4 * Q + 1]
        rowi = lax.broadcasted_iota(jnp.int32, (Q, R), 0)
        da_mat = da_mat + sums[:Q] - t + jnp.where(rowi == Q - 1, da_end, 0.0)
        ddtA = _tri_mm(_tri(Q, upper=True), da_mat)
        ddt_raw = (ddtA * pre["A_c"] + sums[2 * Q:3 * Q]) * _sigmoid(pre["raw_c"])
        ddt_ref[...] = ddt_raw
        gbias_ref[...] += jnp.sum(ddt_raw, axis=0, keepdims=True)
        galog_ref[...] += jnp.sum(ddtA * pre["dt_c"], axis=0, keepdims=True) * pre["A_c"]
        gdsk_ref[...] += jnp.sum(sums[3 * Q:4 * Q], axis=0, keepdims=True)

    sp = _ssd_specs(Q, R, P, N, G, inner, rev=lambda c: nc - 1 - c)
    bout = pl.BlockSpec((Q, N), lambda g, c: (nc - 1 - c, g))
    return _pcall(body, name=name, grid=(G, nc),
                  in_specs=[sp["x"], sp["b"], sp["c"], sp["dtc"], sp["dtr"], sp["pc"], sp["pr"], sp["pc"], sp["pr"], sp["px"],
                            sp["st"], sp["x"]],
                  out_specs=[sp["x"], bout, bout, sp["dtc"], sp["pc"], sp["pc"], sp["pc"]],
                  out_shape=[jax.ShapeDtypeStruct((L, inner), F32), jax.ShapeDtypeStruct((L, G * N), F32),
                             jax.ShapeDtypeStruct((L, G * N), F32), jax.ShapeDtypeStruct((G, L, R), F32),
                             jax.ShapeDtypeStruct((G, 1, R), F32), jax.ShapeDtypeStruct((G, 1, R), F32),
                             jax.ShapeDtypeStruct((G, 1, R), F32)],
                  scratch_shapes=[pltpu.VMEM((N, RP), F32)],
                  compiler_params=_cparams(("parallel", "arbitrary")))(
        act, act, act, dtc, dtr, bias_c, bias_r, alog_c, alog_r, dsk_x, hprev, dy)


def _gnorm_fwd(y, z, g, name):
    L, Dn = y.shape
    gs = Dn // SSM_GROUPS
    tl = _row_tile(L)

    def body(y_ref, z_ref, g_ref, o_ref):
        for k in range(SSM_GROUPS):
            sl = slice(k * gs, (k + 1) * gs)
            zz = z_ref[:, sl]
            u = y_ref[:, sl] * zz * _sigmoid(zz)
            rstd = lax.rsqrt(jnp.mean(u * u, axis=-1, keepdims=True) + RMS_EPS)
            o_ref[:, sl] = (u * rstd * g_ref[:, sl]).astype(BF16)

    row = pl.BlockSpec((tl, Dn), lambda i: (i, 0))
    return _pcall(body, name=name, grid=(L // tl,), in_specs=[row, row, pl.BlockSpec((1, Dn), lambda i: (0, 0))],
                  out_specs=row, out_shape=jax.ShapeDtypeStruct((L, Dn), BF16),
                  compiler_params=_cparams(("parallel",)))(y, z, g.reshape(1, Dn))


def _gnorm_bwd(y, z, g, dout, name):
    L, Dn = y.shape
    gs = Dn // SSM_GROUPS
    tl = _row_tile(L)

    def body(y_ref, z_ref, g_ref, d_ref, dy_ref, dz_ref, dg_ref):
        i = pl.program_id(0)

        @pl.when(i == 0)
        def _():
            dg_ref[...] = jnp.zeros_like(dg_ref)

        for k in range(SSM_GROUPS):
            sl = slice(k * gs, (k + 1) * gs)
            zz = z_ref[:, sl]
            yy = y_ref[:, sl]
            sg = _sigmoid(zz)
            sil = zz * sg
            u = yy * sil
            rstd = lax.rsqrt(jnp.mean(u * u, axis=-1, keepdims=True) + RMS_EPS)
            n = u * rstd
            d = d_ref[:, sl]
            dn = d * g_ref[:, sl]
            du = rstd * (dn - n * jnp.mean(dn * n, axis=-1, keepdims=True))
            dy_ref[:, sl] = du * sil
            dz_ref[:, sl] = (du * yy * sg * (1.0 + zz * (1.0 - sg))).astype(BF16)
            dg_ref[:, sl] += _sum8(d * n)

    row = pl.BlockSpec((tl, Dn), lambda i: (i, 0))
    return _pcall(body, name=name, grid=(L // tl,), in_specs=[row, row, pl.BlockSpec((1, Dn), lambda i: (0, 0)), row],
                  out_specs=[row, row, pl.BlockSpec((SUBLANES, Dn), lambda i: (0, 0))],
                  out_shape=[jax.ShapeDtypeStruct((L, Dn), F32), jax.ShapeDtypeStruct((L, Dn), BF16),
                             jax.ShapeDtypeStruct((SUBLANES, Dn), F32)],
                  compiler_params=_cparams(("arbitrary",)))(y, z, g.reshape(1, Dn), dout)


def _adamw(w, g, m, v, name):
    rows, W = w.shape
    tr = _pick(rows, (512, 256, 128, 64, 32, 16, 8))
    c1 = 1.0 / (1.0 - ADAM_B1 ** ADAM_STEP)
    c2 = 1.0 / (1.0 - ADAM_B2 ** ADAM_STEP)

    def body(w_ref, g_ref, m_ref, v_ref, d_ref, nm_ref, nv_ref):
        g_ = g_ref[...]
        nm = ADAM_B1 * m_ref[...] + (1.0 - ADAM_B1) * g_
        nv = ADAM_B2 * v_ref[...] + (1.0 - ADAM_B2) * (g_ * g_)
        nm_ref[...] = nm
        nv_ref[...] = nv
        d_ref[...] = -ADAM_LR * ((nm * c1) / (jnp.sqrt(nv * c2) + ADAM_EPS) + ADAM_WD * w_ref[...])

    blk = pl.BlockSpec((tr, W), lambda i: (i, 0))
    return _pcall(body, name=name, grid=(rows // tr,), in_specs=[blk] * 4, out_specs=[blk] * 3,
                  out_shape=[jax.ShapeDtypeStruct((rows, W), F32)] * 3, compiler_params=_cparams(("parallel",)))(w, g, m, v)


def _sum_slots(x, name, extra=None):
    n, rows, W = x.shape
    tr = _pick(rows, (512, 256, 128, 64, 32, 16, 8))
    has_extra = extra is not None

    def body(*refs):
        if has_extra:
            e_ref, x_ref, o_ref = refs
            acc = e_ref[...].astype(F32)
            start = 0
        else:
            x_ref, o_ref = refs
            acc = x_ref[0].astype(F32)
            start = 1
        for s in range(start, n):
            acc = acc + x_ref[s].astype(F32)
        o_ref[...] = acc

    blk = pl.BlockSpec((tr, W), lambda i: (i, 0))
    xblk = pl.BlockSpec((n, tr, W), lambda i: (0, i, 0))
    return _pcall(body, name=name, grid=(rows // tr,), in_specs=([blk] if has_extra else []) + [xblk], out_specs=blk,
                  out_shape=jax.ShapeDtypeStruct((rows, W), F32), compiler_params=_cparams(("parallel",)))(
        *(([extra] if has_extra else []) + [x]))


def _add_pairs(a, b, name):
    n, rows, W = a.shape
    tr = _pick(rows, (512, 256, 128, 64, 32, 16, 8))

    def body(a_ref, b_ref, o_ref):
        o_ref[...] = (a_ref[...].astype(F32) + b_ref[...].astype(F32)).astype(BF16)

    blk = pl.BlockSpec((None, tr, W), lambda s, i: (s, i, 0))
    return _pcall(body, name=name, grid=(n, rows // tr), in_specs=[blk, blk], out_specs=blk,
                  out_shape=jax.ShapeDtypeStruct((n, rows, W), BF16), compiler_params=_cparams(("parallel", "parallel")))(a, b)


MESH = pl.DeviceIdType.MESH
HBM_SPEC = pl.BlockSpec(memory_space=pl.ANY)


def _me():
    return lax.axis_index("x"), lax.axis_index("y"), lax.axis_index("c")


def _all_gather(arrs, name):
    n = len(arrs)

    def body(*refs):
        start, forward, finish = _gather_phases(refs[:n], refs[n:2 * n], *refs[2 * n:])
        start()
        forward()
        finish()

    return _pcall(body, name=name, in_specs=[HBM_SPEC] * n, out_specs=[HBM_SPEC] * n,
                  out_shape=_gather_shapes(arrs), scratch_shapes=_gather_sems(n))(*arrs)


def _gather_shapes(arrs):
    return [jax.ShapeDtypeStruct((N_DEV,) + a.shape, a.dtype) for a in arrs]


def _gather_sems(n):
    return [pltpu.SemaphoreType.DMA((7 * n,)), pltpu.SemaphoreType.DMA((7 * n,)), pltpu.SemaphoreType.DMA((n,))]


def _gather_phases(ins, outs, send_sems, recv_sems, local_sems):
    n = len(ins)
    x, y, c = _me()
    me, sib = (x, y, c), (x, y, 1 - c)
    chips = [(1 - x, y), (x, 1 - y), (1 - x, 1 - y)]

    def slot(a, dev):
        return outs[a].at[4 * dev[0] + 2 * dev[1] + dev[2]]

    def copy(a, k, block, to, src=None):
        return pltpu.make_async_remote_copy(src_ref=slot(a, block) if src is None else src, dst_ref=slot(a, block),
                                            send_sem=send_sems.at[a * 7 + k], recv_sem=recv_sems.at[a * 7 + k],
                                            device_id=to, device_id_type=MESH)

    def mine():
        return [pltpu.make_async_copy(ins[a], slot(a, me), local_sems.at[a]) for a in range(n)]

    def first():
        out = []
        for a in range(n):
            out.append(copy(a, 0, me, sib, src=ins[a]))
            out += [copy(a, 1 + j, me, (*chip, c), src=ins[a]) for j, chip in enumerate(chips)]
        return out

    def passed():
        return [copy(a, 4 + j, (*chip, c), sib) for j, chip in enumerate(chips) for a in range(n)]

    def start():
        for cp in mine() + first():
            cp.start()

    def forward():
        fws = passed()
        for j, chip in enumerate(chips):
            for a in range(n):
                copy(a, 1 + j, (*chip, c), me).wait_recv()
                fws[j * n + a].start()

    def finish():
        for a in range(n):
            copy(a, 0, sib, me).wait_recv()
            for j, chip in enumerate(chips):
                copy(a, 4 + j, (*chip, 1 - c), me).wait_recv()
        for cp in first() + passed():
            cp.wait_send()
        for cp in mine():
            cp.wait()

    return start, forward, finish


def _exchange_phases(gs, outs, send_sems, recv_sems, local_sems):
    n = len(gs)
    x, y, c = _me()
    my_slot = 4 * x + 2 * y + c
    flips = [(fx, fy, fc) for fx in (0, 1) for fy in (0, 1) for fc in (0, 1) if fx or fy or fc]

    def peer(f):
        return tuple(1 - v if flip else v for v, flip in zip((x, y, c), f))

    def copies():
        out = []
        for a in range(n):
            for k, f in enumerate(flips):
                px, py, pc = peer(f)
                out.append(pltpu.make_async_remote_copy(
                    src_ref=gs[a].at[4 * px + 2 * py + pc], dst_ref=outs[a].at[my_slot],
                    send_sem=send_sems.at[a * 7 + k], recv_sem=recv_sems.at[a * 7 + k],
                    device_id=(px, py, pc), device_id_type=MESH))
        return out

    def arrivals():
        out = []
        for a in range(n):
            for k, f in enumerate(flips):
                px, py, pc = peer(f)
                slot = outs[a].at[4 * px + 2 * py + pc]
                out.append(pltpu.make_async_remote_copy(src_ref=slot, dst_ref=slot, send_sem=send_sems.at[a * 7 + k],
                                                        recv_sem=recv_sems.at[a * 7 + k], device_id=(px, py, pc),
                                                        device_id_type=MESH))
        return out

    def mine():
        return [pltpu.make_async_copy(gs[a].at[my_slot], outs[a].at[my_slot], local_sems.at[a]) for a in range(n)]

    def start():
        for cp in mine() + copies():
            cp.start()

    def finish():
        for cp in arrivals():
            cp.wait_recv()
        for cp in copies():
            cp.wait_send()
        for cp in mine():
            cp.wait()

    return start, finish


def _rs_sibling(gs, name):
    n = len(gs)

    def body(*refs):
        g_refs, o_refs = refs[:n], refs[n:2 * n]
        send_sems, recv_sems = refs[2 * n:]
        x, y, c = _me()
        sib = (x, y, 1 - c)
        cps = [pltpu.make_async_remote_copy(src_ref=g_refs[a].at[2 * q + (1 - c)], dst_ref=o_refs[a].at[q],
                                            send_sem=send_sems.at[4 * a + q], recv_sem=recv_sems.at[4 * a + q],
                                            device_id=sib, device_id_type=MESH) for a in range(n) for q in range(4)]
        for cp in cps:
            cp.start()
        for cp in cps:
            cp.wait()

    return _pcall(body, name=name, in_specs=[HBM_SPEC] * n, out_specs=[HBM_SPEC] * n,
                  out_shape=[jax.ShapeDtypeStruct((4,) + g.shape[1:], g.dtype) for g in gs],
                  scratch_shapes=[pltpu.SemaphoreType.DMA((4 * n,)), pltpu.SemaphoreType.DMA((4 * n,))])(*gs)


def _rs_chips(ps, name):
    n = len(ps)

    def body(*refs):
        p_refs, o_refs = refs[:n], refs[n:2 * n]
        send_sems, recv_sems = refs[2 * n:]
        x, y, c = _me()
        chips = [(1 - x, y), (x, 1 - y), (1 - x, 1 - y)]
        cps = [pltpu.make_async_remote_copy(src_ref=p_refs[a].at[2 * chip[0] + chip[1]], dst_ref=o_refs[a].at[j],
                                            send_sem=send_sems.at[3 * a + j], recv_sem=recv_sems.at[3 * a + j],
                                            device_id=(*chip, c), device_id_type=MESH)
               for j, chip in enumerate(chips) for a in range(n)]
        for cp in cps:
            cp.start()
        for cp in cps:
            cp.wait()

    return _pcall(body, name=name, in_specs=[HBM_SPEC] * n, out_specs=[HBM_SPEC] * n,
                  out_shape=[jax.ShapeDtypeStruct((3,) + p.shape[1:], p.dtype) for p in ps],
                  scratch_shapes=[pltpu.SemaphoreType.DMA((3 * n,)), pltpu.SemaphoreType.DMA((3 * n,))])(*ps)


def _reduce_scatter(gs, name):
    x, y, c = _me()
    from_sib = _rs_sibling(gs, name + "_sib")
    pairs = []
    for a, (g, fs) in enumerate(zip(gs, from_sib)):
        own = g.reshape((4, 2) + g.shape[1:])
        pairs.append(_add_pairs(jnp.where(c == 0, own[:, 0], own[:, 1]), fs, f"{name}_pair{a}"))
    from_chips = _rs_chips(pairs, name + "_chips")
    return [_sum_slots(fc, f"{name}_sum{a}", extra=lax.dynamic_index_in_dim(p, 2 * x + y, axis=0, keepdims=False))
            for a, (p, fc) in enumerate(zip(pairs, from_chips))]


BIG = ("even_w_in", "even_w_out", "odd_w_in", "odd_w_out", "ffn_w_up", "ffn_w_down", "ple_w_proj", "ple_w_gate")
SMALL_SHARDED = ("even_conv_w", "odd_conv_w", "odd_conv_b", "odd_norm_g", "ffn_conv_w")
REPLICATED = ("even_b_f", "odd_dt_bias", "odd_a_log", "odd_d_skip", "ln_mix_g", "ln_mix_b", "ffn_conv_b",
              "ln_ffn_g", "ln_ffn_b", "ple_b_gate")
WEIGHTS = ("even_w_in", "even_b_f", "even_conv_w", "even_w_out", "odd_w_in", "odd_conv_w", "odd_conv_b", "odd_dt_bias",
           "odd_a_log", "odd_d_skip", "odd_norm_g", "odd_w_out", "ln_mix_g", "ln_mix_b", "ffn_w_up", "ffn_conv_w",
           "ffn_conv_b", "ffn_w_down", "ln_ffn_g", "ln_ffn_b", "ple_w_proj", "ple_w_gate", "ple_b_gate")


def _full_shapes():
    d = _dims()
    return {
        "even_w_in": ((1, D_MODEL, d["even_in"]), 2), "even_b_f": ((1, FOX_HEADS), None),
        "even_conv_w": ((1, CONV_WIDTH, CONV_DIM), 2), "even_w_out": ((1, d["even_mix"], D_MODEL), 1),
        "odd_w_in": ((1, D_MODEL, d["odd_in"]), 2), "odd_conv_w": ((1, SSM_CONV_WIDTH, d["conv_ch"]), 2),
        "odd_conv_b": ((1, d["conv_ch"]), 1), "odd_dt_bias": ((1, d["ssm_heads"]), None),
        "odd_a_log": ((1, d["ssm_heads"]), None), "odd_d_skip": ((1, d["ssm_heads"]), None),
        "odd_norm_g": ((1, d["ssm_inner"]), 1), "odd_w_out": ((1, d["ssm_inner"], D_MODEL), 1),
        "ln_mix_g": ((DEPTH, D_MODEL), None), "ln_mix_b": ((DEPTH, D_MODEL), None),
        "ffn_w_up": ((DEPTH, D_MODEL, 2 * D_FF), 2), "ffn_conv_w": ((DEPTH, FFN_CONV_WIDTH, 2 * D_FF), 2),
        "ffn_conv_b": ((DEPTH, 2 * D_FF), None), "ffn_w_down": ((DEPTH, D_FF, D_MODEL), 1),
        "ln_ffn_g": ((DEPTH, D_MODEL), None), "ln_ffn_b": ((DEPTH, D_MODEL), None),
        "ple_w_proj": ((DEPTH, PLE_DIM, D_MODEL), 2), "ple_w_gate": ((DEPTH, D_MODEL, D_MODEL), 1),
        "ple_b_gate": ((DEPTH, D_MODEL), None),
    }


def _shard_shape(name):
    shape, ax = _full_shapes()[name]
    if ax is None:
        return shape
    return tuple(s // N_DEV if i == ax else s for i, s in enumerate(shape))


def _as2d(a, lead=0):
    return a.reshape(a.shape[:lead] + (-1, a.shape[-1]))


def _part_rows(shape):
    n = int(np.prod(shape))
    return -(-(-(-n // PACK_W)) // SUBLANES) * SUBLANES


def _pack_small(parts):
    out = []
    for p in parts:
        n, rows = int(np.prod(p.shape)), _part_rows(p.shape)
        out.append(jnp.pad(p.reshape(-1).astype(F32), (0, rows * PACK_W - n)).reshape(rows, PACK_W))
    return jnp.concatenate(out, axis=0)


def _unpack_small(pack, shapes):
    lead = pack.shape[:-2]
    out, off = [], 0
    for s in shapes:
        n, rows = int(np.prod(s)), _part_rows(s)
        part = pack[..., off:off + rows, :].reshape(lead + (-1,))[..., :n]
        out.append(part.reshape(lead + tuple(s)))
        off += rows
    return out


def _assemble(gathered, name):
    shape, ax = _full_shapes()[name]
    return jnp.moveaxis(gathered, 0, ax).reshape(shape)


def _split_dest(full, name):
    shape, ax = _full_shapes()[name]
    sh = shape[:ax] + (N_DEV, shape[ax] // N_DEV) + shape[ax + 1:]
    return jnp.moveaxis(full.reshape(sh), ax, 0)


def _interleave_cols(w, parts, tc):
    C = w.shape[-1] // parts
    sh = w.shape[:-1]
    return w.reshape(sh + (parts, C // tc, tc)).swapaxes(-3, -2).reshape(sh + (parts * C,))


def _deinterleave_cols(w, parts, tc):
    C = w.shape[-1] // parts
    sh = w.shape[:-1]
    return w.reshape(sh + (C // tc, parts, tc)).swapaxes(-3, -2).reshape(sh + (parts * C,))


def _pad_cols(a, to):
    return jnp.pad(a, ((0, 0), (0, to - a.shape[1])))


def _tail_fwd(i, h_in, mix, p_i, W, sp):
    r1, h1, h1b = _ln_fwd(h_in, mix, sp["ln_mix_g"][i], sp["ln_mix_b"][i], f"ln_mix_fwd{i}")
    U = _mm(h1b, W["ffn_up"][i], "nn", F32, f"ffn_up{i}")
    S = _ffn_act_fwd(U, sp["ffn_conv_w"][i], sp["ffn_conv_b"][i], f"ffn_act_fwd{i}")
    ffn = _mm(S, W["ffn_down"][i], "nn", F32, f"ffn_down{i}")
    r2, h2, h2b = _ln_fwd(h1, ffn, sp["ln_ffn_g"][i], sp["ln_ffn_b"][i], f"ln_ffn_fwd{i}")
    G = _mm(h2b, W["ple_gate"][i], "nn", F32, f"ple_gate{i}")
    E = _mm(p_i, W["ple_proj"][i], "nn", F32, f"ple_proj{i}")
    h3, h3b = _ple_fwd(h2, G, sp["ple_b_gate"][i], E, f"ple_fwd{i}")
    return h3, h3b, dict(r1=r1, h1b=h1b, U=U, S=S, r2=r2, h2b=h2b, G=G, E=E, p=p_i)


def _tail_bwd(i, dh3, sv, W, sp, grads):
    alpha = _alpha()
    dE, dGp, dbg = _ple_bwd(dh3, sv["G"], sp["ple_b_gate"][i], sv["E"], f"ple_bwd{i}")
    grads["ple_b_gate"][i] = dbg.sum(0)
    grads["ple_w_proj"][i] = _mm(sv["p"], dE, "tn", F32, f"d_ple_proj{i}")
    grads["ple_w_gate"][i] = _mm(sv["h2b"], dGp, "tn", F32, f"d_ple_gate{i}")
    dh2 = _mm(dGp, W["ple_gate"][i], "nt", F32, f"dx_ple_gate{i}", add=dh3)
    dr2, dr2b, dg, db = _ln_bwd(sv["r2"], dh2, sp["ln_ffn_g"][i], f"ln_ffn_bwd{i}")
    grads["ln_ffn_g"][i], grads["ln_ffn_b"][i] = dg.sum(0), db.sum(0)
    grads["ffn_w_down"][i] = _mm(sv["S"], dr2b, "tn", F32, f"d_ffn_down{i}")
    dS = _mm(dr2b, W["ffn_down"][i], "nt", BF16, f"dx_ffn_down{i}")
    dUg, dUv, dwg, dwv, dbg, dbv = _ffn_act_bwd(sv["U"], dS, sp["ffn_conv_w"][i], sp["ffn_conv_b"][i], f"ffn_act_bwd{i}")
    K = FFN_CONV_WIDTH
    grads["ffn_conv_w"][i] = jnp.concatenate([dwg.reshape(K, SUBLANES, -1).sum(1), dwv.reshape(K, SUBLANES, -1).sum(1)], axis=1)
    grads["ffn_conv_b"][i] = jnp.concatenate([dbg.sum(0), dbv.sum(0)])
    grads["ffn_w_up"][i] = jnp.concatenate([_mm(sv["h1b"], dUg, "tn", F32, f"d_ffn_up_g{i}"),
                                            _mm(sv["h1b"], dUv, "tn", F32, f"d_ffn_up_v{i}")], axis=1)
    dh1 = _mm(dUg, W["ffn_up"][i], "nt", F32, f"dx_ffn_up_g{i}", add=dr2, add_scale=alpha)
    dh1 = _mm(dUv, W["ffn_up"][i], "nt", F32, f"dx_ffn_up_v{i}", add=dh1, b_k_start=D_FF)
    dr1, dr1b, dg, db = _ln_bwd(sv["r1"], dh1, sp["ln_mix_g"][i], f"ln_mix_bwd{i}")
    grads["ln_mix_g"][i], grads["ln_mix_b"][i] = dg.sum(0), db.sum(0)
    return dr1, dr1b


def _even_fwd(h, W, sp, gather):
    L = h.shape[0]
    H, Dh = FOX_HEADS, FOX_HEAD_DIM
    Ac = _mm(h, W["even_in_conv"], "nn", F32, "even_in_conv")
    qkv = _mm(h, W["even_in_qkv"], "nn", BF16, "even_in_qkv")
    Af = _mm(h, W["even_in_f"], "nn", F32, "even_in_f")
    y_a = _sconv_fwd(Ac, sp["even_conv_w_il"], "sconv_fwd")
    Fc = _fox_gate_fwd(Af, sp["even_b_f_pad"], "fox_gate_fwd")
    Fh = Fc[:, :H].T
    Fq, Fk = Fh.reshape(H, L, 1), Fh.reshape(H, 1, L)
    o, lse, *gathered = _attn_fwd(qkv, Fq, Fk, "attn_fwd", gather)
    Y = jnp.concatenate([y_a, o], axis=1)
    mix = _mm(Y, W["even_out"], "nn", F32, "even_out")
    return mix, dict(h=h, Ac=Ac, Af=Af, qkv=qkv, Fq=Fq, Fk=Fk, lse=lse, Y=Y), gathered


def _even_bwd(dmix, dres, sv, W, sp, grads, exchange):
    H, Dh = FOX_HEADS, FOX_HEAD_DIM
    C = CONV_DIM
    L = dmix.shape[0]
    grads["even_w_out"][0] = _mm(sv["Y"], dmix, "tn", F32, "d_even_out")
    dY = _mm(dmix, W["even_out"], "nt", F32, "dx_even_out")
    dya = dY[:, :C]
    do = dY[:, C:].astype(BF16)
    dc, dcw = _sconv_bwd_dc(sv["Ac"], dya, sp["even_conv_w_il"], "sconv_bwd_dc")
    grads["even_conv_w"][0] = dcw.reshape(CONV_WIDTH, SUBLANES, -1).sum(1)
    dAc = _sconv_bwd_da(sv["Ac"], dya, dc, sp["even_conv_w_il"], "sconv_bwd_da")
    dq, dk, dv, dFk, *arrived = _attn_bwd(sv["qkv"], sv["Fq"], sv["Fk"], sv["lse"], do, "attn_bwd", exchange)
    dqkv = jnp.concatenate([dq, dk.astype(BF16), dv.astype(BF16)], axis=1)
    dF = _pad_cols(dFk.reshape(H, L).T, LANES)
    dAf, dbf = _fox_gate_bwd(sv["Af"], sp["even_b_f_pad"], dF, "fox_gate_bwd")
    grads["even_b_f"][0] = dbf.sum(0)[:H]
    h = sv["h"]
    gc = _deinterleave_cols(_mm(h, dAc, "tn", F32, "d_even_in_conv"), 3, LANES)
    gq = _mm(h, dqkv, "tn", F32, "d_even_in_qkv")
    gf = _mm(h, dAf, "tn", F32, "d_even_in_f")[:, :H]
    grads["even_w_in"][0] = jnp.concatenate([gc, gq, gf], axis=1)
    dh = _mm(dAc, W["even_in_conv"], "nt", F32, "dx_even_in_conv", add=dres, add_scale=_alpha())
    dh = _mm(dqkv, W["even_in_qkv"], "nt", F32, "dx_even_in_qkv", add=dh)
    dh = _mm(dAf, W["even_in_f"], "nt", F32, "dx_even_in_f", add=dh)
    return dh, arrived


def _group_layouts(v, G):
    R = v.shape[0] // G
    return v.reshape(G, 1, R), v.reshape(G, R, 1)


def _odd_fwd(h, W, sp):
    d = _dims()
    L = h.shape[0]
    Hs, G, N, P = d["ssm_heads"], SSM_GROUPS, SSM_STATE, SSM_HEAD_DIM
    R = Hs // G
    inner = d["ssm_inner"]
    z = _mm(h, W["odd_in_z"], "nn", F32, "odd_in_z")
    xr = _mm(h, W["odd_in_x"], "nn", F32, "odd_in_x")
    dtp = _mm(h, W["odd_in_dt"], "nn", F32, "odd_in_dt")
    act = _mconv_fwd(xr, sp["odd_conv_w"], sp["odd_conv_b"], "mconv_fwd")
    dtg = dtp[:, :Hs].reshape(L, G, R)
    dtc, dtr = dtg.transpose(1, 0, 2), dtg.transpose(1, 2, 0)
    dsk_x = jnp.repeat(sp["odd_d_skip"], P).reshape(G, 1, R * P)
    ssd_in = (act, dtc, dtr) + _group_layouts(sp["odd_dt_bias"], G) + _group_layouts(sp["odd_a_log"], G) + (dsk_x,)
    y, hprev = _ssd_fwd(*ssd_in, "ssd_fwd")
    u = _gnorm_fwd(y, z, sp["odd_norm_g"], "gnorm_fwd")
    mix = _mm(u, W["odd_out"], "nn", F32, "odd_out")
    return mix, dict(h=h, z=z, xr=xr, ssd_in=ssd_in, hprev=hprev, y=y, u=u)


def _odd_bwd(dmix, dres, sv, W, sp, grads):
    d = _dims()
    L = dmix.shape[0]
    Hs, G, N, P = d["ssm_heads"], SSM_GROUPS, SSM_STATE, SSM_HEAD_DIM
    grads["odd_w_out"][0] = _mm(sv["u"], dmix, "tn", F32, "d_odd_out")
    du = _mm(dmix, W["odd_out"], "nt", F32, "dx_odd_out")
    dy, dz, dg = _gnorm_bwd(sv["y"], sv["z"], sp["odd_norm_g"], du, "gnorm_bwd")
    grads["odd_norm_g"][0] = dg.sum(0)
    dxs, dB, dC, ddt, gbias, galog, gdsk = _ssd_bwd(*sv["ssd_in"], sv["hprev"], dy, "ssd_bwd")
    grads["odd_dt_bias"][0] = gbias.reshape(Hs)
    grads["odd_a_log"][0] = galog.reshape(Hs)
    grads["odd_d_skip"][0] = gdsk.reshape(Hs)
    dact = jnp.concatenate([dxs, dB, dC], axis=1)
    dxr, dcw, dcb = _mconv_bwd(sv["xr"], dact, sp["odd_conv_w"], sp["odd_conv_b"], "mconv_bwd")
    grads["odd_conv_w"][0] = dcw.reshape(SSM_CONV_WIDTH, SUBLANES, -1).sum(1)
    grads["odd_conv_b"][0] = dcb.sum(0)
    ddtp = _pad_cols(ddt.transpose(1, 0, 2).reshape(L, Hs), W["odd_in_dt"].shape[1])
    h = sv["h"]
    gz = _mm(h, dz, "tn", F32, "d_odd_in_z")
    gx = _mm(h, dxr, "tn", F32, "d_odd_in_x")
    gdt = _mm(h, ddtp, "tn", F32, "d_odd_in_dt")[:, :Hs]
    grads["odd_w_in"][0] = jnp.concatenate([gz, gx, gdt], axis=1)
    dh = _mm(dz, W["odd_in_z"], "nt", F32, "dx_odd_in_z", add=dres, add_scale=_alpha())
    dh = _mm(dxr, W["odd_in_x"], "nt", F32, "dx_odd_in_x", add=dh)
    dh = _mm(ddtp, W["odd_in_dt"], "nt", F32, "dx_odd_in_dt", add=dh)
    return dh


def _big_units(layer=None):
    out = []
    for n in BIG:
        for i in range(_full_shapes()[n][0][0]):
            lay = 0 if n.startswith("even") else 1 if n.startswith("odd") else i
            if layer is None or lay == layer:
                out.append((n, i))
    return out


def _assemble_unit(gathered, name):
    shape, ax = _full_shapes()[name]
    g = gathered.reshape((N_DEV,) + _shard_shape(name)[1:])
    return jnp.moveaxis(g, 0, ax - 1).reshape(shape[1:])


def _split_unit(full_layer, name):
    shape, ax = _full_shapes()[name]
    sh = shape[1:ax] + (N_DEV, shape[ax] // N_DEV) + shape[ax + 1:]
    return _as2d(jnp.moveaxis(full_layer.reshape(sh), ax - 1, 0), 1)


def _prepare_layer(W, full, layer):
    d = _dims()
    if layer == 0:
        C, fd = CONV_DIM, d["fox_dim"]
        ew = full["even_w_in"][0]
        W["even_in_conv"] = _interleave_cols(ew[:, :3 * C], 3, LANES)
        W["even_in_qkv"] = ew[:, 3 * C:3 * C + 3 * fd]
        W["even_in_f"] = _pad_cols(ew[:, 3 * C + 3 * fd:], LANES)
        W["even_out"] = full["even_w_out"][0]
    else:
        ow = full["odd_w_in"][0]
        inner, cch, Hs = d["ssm_inner"], d["conv_ch"], d["ssm_heads"]
        W["odd_in_z"] = ow[:, :inner]
        W["odd_in_x"] = ow[:, inner:inner + cch]
        W["odd_in_dt"] = _pad_cols(ow[:, inner + cch:], -(-Hs // LANES) * LANES)
        W["odd_out"] = full["odd_w_out"][0]
    for key, name in (("ffn_up", "ffn_w_up"), ("ffn_down", "ffn_w_down"), ("ple_proj", "ple_w_proj"), ("ple_gate", "ple_w_gate")):
        W.setdefault(key, [None] * DEPTH)[layer] = full[name][layer]


def _prepare_small(full):
    sp = {}
    sp["even_conv_w_il"] = full["even_conv_w"][0]
    sp["even_b_f_pad"] = _pad_cols(full["even_b_f"], LANES)
    sp["odd_conv_w"] = full["odd_conv_w"][0]
    sp["odd_conv_b"] = full["odd_conv_b"][0]
    sp["odd_norm_g"] = full["odd_norm_g"][0]
    for n in ("odd_dt_bias", "odd_a_log", "odd_d_skip"):
        sp[n] = full[n][0]
    for n in ("ln_mix_g", "ln_mix_b", "ln_ffn_g", "ln_ffn_b", "ple_b_gate"):
        sp[n] = full[n]
    sp["ffn_conv_w"] = [full["ffn_conv_w"][i] for i in range(DEPTH)]
    sp["ffn_conv_b"] = [full["ffn_conv_b"][i] for i in range(DEPTH)]
    return sp


def _local_step(x, p, target, full, layer1_shards):
    sp = _prepare_small(full)
    W = {}
    _prepare_layer(W, full, 0)
    grads = {n: [None] * _full_shapes()[n][0][0] for n in WEIGHTS}
    pb = p.astype(BF16)
    mix0, sv_e, gathered = _even_fwd(x.astype(BF16), W, sp, layer1_shards)
    for (n, i), g in zip(_big_units(1), gathered):
        full[n][i] = _assemble_unit(g, n)
    _prepare_layer(W, full, 1)
    h3_0, h3_0b, sv_t0 = _tail_fwd(0, x, mix0, pb[0], W, sp)
    mix1, sv_o = _odd_fwd(h3_0b, W, sp)
    h3_1, _, sv_t1 = _tail_fwd(1, h3_0, mix1, pb[1], W, sp)
    dh, sq = _loss_head(h3_1, target, "loss_head")
    dr1, dr1b = _tail_bwd(1, dh, sv_t1, W, sp, grads)
    dh = _odd_bwd(dr1b, dr1, sv_o, W, sp, grads)
    outgoing = [_split_unit(grads[n][i], n).astype(BF16) for n, i in _big_units(1)]
    dr1, dr1b = _tail_bwd(0, dh, sv_t0, W, sp, grads)
    dx, incoming = _even_bwd(dr1b, dr1, sv_e, W, sp, grads, outgoing)
    reduced1 = {u: _sum_slots(r, f"sum_grads_{u[0]}{u[1]}") for u, r in zip(_big_units(1), incoming)}
    grads = {n: v if n in BIG else jnp.stack(v) for n, v in grads.items()}
    return jnp.sum(sq), dx, grads, reduced1


def kernel(x, p, even_w_in, even_b_f, even_conv_w, even_w_out, odd_w_in, odd_conv_w, odd_conv_b, odd_dt_bias, odd_a_log, odd_d_skip, odd_norm_g, odd_w_out, ln_mix_g, ln_mix_b, ffn_w_up, ffn_conv_w, ffn_conv_b, ffn_w_down, ln_ffn_g, ln_ffn_b, ple_w_proj, ple_w_gate, ple_b_gate, loss_target, m_even_w_in, m_even_b_f, m_even_conv_w, m_even_w_out, m_odd_w_in, m_odd_conv_w, m_odd_conv_b, m_odd_dt_bias, m_odd_a_log, m_odd_d_skip, m_odd_norm_g, m_odd_w_out, m_ln_mix_g, m_ln_mix_b, m_ffn_w_up, m_ffn_conv_w, m_ffn_conv_b, m_ffn_w_down, m_ln_ffn_g, m_ln_ffn_b, m_ple_w_proj, m_ple_w_gate, m_ple_b_gate, v_even_w_in, v_even_b_f, v_even_conv_w, v_even_w_out, v_odd_w_in, v_odd_conv_w, v_odd_conv_b, v_odd_dt_bias, v_odd_a_log, v_odd_d_skip, v_odd_norm_g, v_odd_w_out, v_ln_mix_g, v_ln_mix_b, v_ffn_w_up, v_ffn_conv_w, v_ffn_conv_b, v_ffn_w_down, v_ln_ffn_g, v_ln_ffn_b, v_ple_w_proj, v_ple_w_gate, v_ple_b_gate):
    args = locals()
    w = {n: args[n] for n in WEIGHTS}
    m = {n: args["m_" + n] for n in WEIGHTS}
    v = {n: args["v_" + n] for n in WEIGHTS}
    me = 4 * lax.axis_index("x") + 2 * lax.axis_index("y") + lax.axis_index("c")

    units0, units1 = _big_units(0), _big_units(1)

    def shard(unit):
        return _as2d(w[unit[0]][unit[1]]).astype(BF16)

    gathered = _all_gather([shard(u) for u in units0] + [_pack_small([w[n] for n in SMALL_SHARDED])], "ag_weights")
    full = dict(w)
    for n in BIG:
        full[n] = [None] * _full_shapes()[n][0][0]
    for (n, i), g in zip(units0, gathered[:-1]):
        full[n][i] = _assemble_unit(g, n)
    for n, g in zip(SMALL_SHARDED, _unpack_small(gathered[-1], [_shard_shape(n) for n in SMALL_SHARDED])):
        full[n] = _assemble(g, n)

    sq, dx, grads, reduced = _local_step(x[0], p[:, 0], loss_target[0], full, [shard(u) for u in units1])
    loss = lax.psum(0.5 * sq / D_MODEL, ("x", "y", "c"))

    reduced.update(zip(units0, _reduce_scatter([_split_unit(grads[n][i], n).astype(BF16) for n, i in units0], "rs_grads")))
    g_final = {n: jnp.stack([reduced[(n, i)] for i in range(_full_shapes()[n][0][0])]).reshape(_shard_shape(n)) for n in BIG}
    small_names = SMALL_SHARDED + REPLICATED
    (small_all,) = _all_gather([_pack_small([grads[n] for n in small_names])], "ag_small_grads")
    small_sum = _sum_slots(small_all, "sum_small_grads")
    for n, g in zip(small_names, _unpack_small(small_sum, [_full_shapes()[n][0] for n in small_names])):
        g_final[n] = lax.dynamic_index_in_dim(_split_dest(g, n), me, axis=0, keepdims=False) if n in SMALL_SHARDED else g

    out = {}
    for n in BIG:
        res = _adamw(*[_as2d(t[n]) for t in (w, g_final, m, v)], "adamw_" + n)
        out[n] = [r.reshape(_shard_shape(n)) for r in res]
    shapes = [_shard_shape(n) for n in small_names]
    res = _adamw(*[_pack_small([t[n] for n in small_names]) for t in (w, g_final, m, v)], "adamw_small")
    for n, d_, m_, v_ in zip(small_names, *[_unpack_small(r, shapes) for r in res]):
        out[n] = [d_, m_, v_]
    return (loss, dx[None], *[g_final[n] for n in WEIGHTS], *[out[n][0] for n in WEIGHTS],
            *[out[n][1] for n in WEIGHTS], *[out[n][2] for n in WEIGHTS])
```

```python
import jax
import jax.numpy as jnp
import numpy as np
from jax import lax
from jax.experimental import pallas as pl
from jax.experimental.pallas import tpu as pltpu

D_MODEL = 1024
SEQ = 8192
DEPTH = 2
CONV_DIM = 512
CONV_WIDTH = 3
FOX_HEADS = 8
FOX_HEAD_DIM = 64
SSM_HEAD_DIM = 64
SSM_GROUPS = 4
SSM_STATE = 128
SSM_CONV_WIDTH = 4
SSM_CHUNK = 128
D_FF = 2816
FFN_CONV_WIDTH = 3
PLE_DIM = 256
LN_EPS = 1e-5
RMS_EPS = 1e-5
ADAM_LR = 0.001
ADAM_B1 = 0.9
ADAM_B2 = 0.999
ADAM_EPS = 1e-08
ADAM_WD = 0.01
ADAM_STEP = 10
N_DEV = 8

F32 = jnp.float32
BF16 = jnp.bfloat16
NEG = -1e30
LANES = 128
SUBLANES = 8
PACK_W = 1024
VMEM_LIMIT = 48 * 1024 * 1024
ATTN_BWD_VMEM_LIMIT = 56 * 1024 * 1024


def _dims():
    fox_dim = FOX_HEADS * FOX_HEAD_DIM
    ssm_inner = 2 * D_MODEL
    ssm_heads = ssm_inner // SSM_HEAD_DIM
    conv_ch = ssm_inner + 2 * SSM_GROUPS * SSM_STATE
    return dict(fox_dim=fox_dim, even_in=3 * CONV_DIM + 3 * fox_dim + FOX_HEADS, even_mix=CONV_DIM + fox_dim,
                ssm_inner=ssm_inner, ssm_heads=ssm_heads, conv_ch=conv_ch, odd_in=ssm_inner + conv_ch + ssm_heads)


def _alpha():
    return (2.0 * DEPTH) ** 0.25


def _pick(dim, prefs):
    for p in prefs:
        if dim % p == 0:
            return p
    return dim


def _pcall(body, **kw):
    return pl.pallas_call(body, **kw)


def _cparams(sem=None, **kw):
    if sem is not None:
        kw["dimension_semantics"] = sem
    return pltpu.CompilerParams(vmem_limit_bytes=VMEM_LIMIT, **kw)


def _sigmoid(x):
    return 1.0 / (1.0 + jnp.exp(-x))


def _softplus(x):
    return jnp.maximum(x, 0.0) + jnp.log(1.0 + jnp.exp(-jnp.abs(x)))


def _sum8(x):
    n, c = x.shape
    return x.reshape(n // SUBLANES, SUBLANES, c).sum(axis=0)


def _dot(a, b, dims):
    return lax.dot_general(a, b, (dims, ((), ())), preferred_element_type=F32)


NN = ((1,), (0,))
NT = ((1,), (1,))
TN = ((0,), (0,))


def _split3(x):
    hi = x.astype(BF16)
    r1 = x - hi.astype(F32)
    mid = r1.astype(BF16)
    lo = (r1 - mid.astype(F32)).astype(BF16)
    return hi, mid, lo


def _tri_mm(tri_bf16, x, tri_first=True):
    if tri_first:
        return sum(_dot(tri_bf16, part, NN) for part in _split3(x))
    return sum(_dot(part, tri_bf16, NN) for part in _split3(x))


def _tri(n, upper=False):
    r = lax.broadcasted_iota(jnp.int32, (n, n), 0)
    c = lax.broadcasted_iota(jnp.int32, (n, n), 1)
    return jnp.where((r <= c) if upper else (r >= c), 1.0, 0.0).astype(BF16)


def _shift_down(cur, prev8, k):
    if k == 0:
        return cur
    ext = jnp.concatenate([prev8, cur], axis=0)
    return pltpu.roll(ext, k, axis=0)[SUBLANES:]


def _shift_up(cur, next8, k):
    if k == 0:
        return cur
    n = cur.shape[0]
    ext = jnp.concatenate([cur, next8], axis=0)
    return pltpu.roll(ext, n + SUBLANES - k, axis=0)[:n]


def _mm(a, b, mode, out_dtype, name, add=None, add_scale=1.0, b_k_start=0):
    if mode == "nn":
        (M, K), (K2, N) = a.shape, b.shape
    elif mode == "nt":
        (M, K), N = a.shape, b.shape[0]
        K2 = K if b.shape[1] >= b_k_start + K else None
    else:
        (K, M), (K2, N) = a.shape, b.shape
    assert K == K2, (a.shape, b.shape, mode)
    tm = _pick(M, (1024, 1408, 512, 256, 128))
    tn = _pick(N, (1408, 1024, 768, 512, 384, 256, 128))
    tk = K if K <= 2048 and b_k_start % K == 0 else _pick(K, (1408, 1024, 768, 512, 256, 128))
    assert b_k_start % tk == 0
    k0 = b_k_start // tk
    nk = K // tk
    dims = {"nn": NN, "nt": NT, "tn": TN}[mode]
    a_spec = pl.BlockSpec((tk, tm), lambda i, j, k: (k, i)) if mode == "tn" else pl.BlockSpec((tm, tk), lambda i, j, k: (i, k))
    b_spec = pl.BlockSpec((tn, tk), lambda i, j, k: (j, k + k0)) if mode == "nt" else pl.BlockSpec((tk, tn), lambda i, j, k: (k, j))
    o_spec = pl.BlockSpec((tm, tn), lambda i, j, k: (i, j))
    has_add = add is not None

    def body(*refs):
        a_ref, b_ref = refs[:2]
        add_ref = refs[2] if has_add else None
        o_ref = refs[2 + has_add]
        prod = _dot(a_ref[...].astype(BF16), b_ref[...].astype(BF16), dims)
        if nk == 1:
            if has_add:
                prod = prod + add_scale * add_ref[...].astype(F32)
            o_ref[...] = prod.astype(out_dtype)
            return
        acc = refs[3 + has_add]
        k = pl.program_id(2)

        @pl.when(k == 0)
        def _():
            if has_add:
                acc[...] = prod + add_scale * add_ref[...].astype(F32)
            else:
                acc[...] = prod

        @pl.when(k > 0)
        def _():
            acc[...] += prod

        @pl.when(k == nk - 1)
        def _():
            o_ref[...] = acc[...].astype(out_dtype)

    ins = [a, b] + ([add] if has_add else [])
    specs = [a_spec, b_spec] + ([o_spec] if has_add else [])
    return _pcall(body, name=name, grid=(M // tm, N // tn, nk), in_specs=specs, out_specs=o_spec,
                  out_shape=jax.ShapeDtypeStruct((M, N), out_dtype),
                  scratch_shapes=[pltpu.VMEM((tm, tn), F32)] if nk > 1 else [],
                  compiler_params=_cparams(("parallel", "parallel", "arbitrary")))(*ins)


def _row_tile(L):
    return _pick(L, (256, 128))


def _conv_row_tile(L, backward):
    return _pick(L, (512, 256, 128)) if backward else _pick(L, (1024, 512, 256, 128))


def _ln_fwd(h, mix, g, b, name):
    L, D = h.shape
    tl = _row_tile(L)
    alpha = _alpha()

    def body(h_ref, m_ref, g_ref, b_ref, r_ref, y_ref, yb_ref):
        r = alpha * h_ref[...] + m_ref[...]
        mu = jnp.mean(r, axis=-1, keepdims=True)
        xc = r - mu
        var = jnp.mean(xc * xc, axis=-1, keepdims=True)
        r_ref[...] = r
        y = xc * lax.rsqrt(var + LN_EPS) * g_ref[...] + b_ref[...]
        y_ref[...] = y
        yb_ref[...] = y.astype(BF16)

    row = pl.BlockSpec((tl, D), lambda i: (i, 0))
    vec = pl.BlockSpec((1, D), lambda i: (0, 0))
    return _pcall(body, name=name, grid=(L // tl,), in_specs=[row, row, vec, vec], out_specs=[row, row, row],
                  out_shape=[jax.ShapeDtypeStruct((L, D), F32)] * 2 + [jax.ShapeDtypeStruct((L, D), BF16)],
                  compiler_params=_cparams(("parallel",)))(h, mix, g.reshape(1, D), b.reshape(1, D))


def _ln_bwd(r, dy, g, name):
    L, D = r.shape
    tl = _row_tile(L)

    def body(r_ref, dy_ref, g_ref, dr_ref, drb_ref, dg_ref, db_ref):
        i = pl.program_id(0)
        r_ = r_ref[...]
        dy_ = dy_ref[...]
        mu = jnp.mean(r_, axis=-1, keepdims=True)
        xc = r_ - mu
        rstd = lax.rsqrt(jnp.mean(xc * xc, axis=-1, keepdims=True) + LN_EPS)
        xhat = xc * rstd
        dxh = dy_ * g_ref[...]
        dr = rstd * (dxh - jnp.mean(dxh, axis=-1, keepdims=True) - xhat * jnp.mean(dxh * xhat, axis=-1, keepdims=True))
        dr_ref[...] = dr
        drb_ref[...] = dr.astype(BF16)

        @pl.when(i == 0)
        def _():
            dg_ref[...] = jnp.zeros_like(dg_ref)
            db_ref[...] = jnp.zeros_like(db_ref)

        dg_ref[...] += _sum8(dy_ * xhat)
        db_ref[...] += _sum8(dy_)

    row = pl.BlockSpec((tl, D), lambda i: (i, 0))
    vec = pl.BlockSpec((1, D), lambda i: (0, 0))
    acc = pl.BlockSpec((SUBLANES, D), lambda i: (0, 0))
    return _pcall(body, name=name, grid=(L // tl,), in_specs=[row, row, vec], out_specs=[row, row, acc, acc],
                  out_shape=[jax.ShapeDtypeStruct((L, D), F32), jax.ShapeDtypeStruct((L, D), BF16),
                             jax.ShapeDtypeStruct((SUBLANES, D), F32), jax.ShapeDtypeStruct((SUBLANES, D), F32)],
                  compiler_params=_cparams(("arbitrary",)))(r, dy, g.reshape(1, D))


def _ple_fwd(h2, G, bg, E, name):
    L, D = h2.shape
    tl = _row_tile(L)

    def body(h_ref, g_ref, b_ref, e_ref, o_ref, ob_ref):
        o = h_ref[...] + _sigmoid(g_ref[...] + b_ref[...]) * e_ref[...]
        o_ref[...] = o
        ob_ref[...] = o.astype(BF16)

    row = pl.BlockSpec((tl, D), lambda i: (i, 0))
    vec = pl.BlockSpec((1, D), lambda i: (0, 0))
    return _pcall(body, name=name, grid=(L // tl,), in_specs=[row, row, vec, row], out_specs=[row, row],
                  out_shape=[jax.ShapeDtypeStruct((L, D), F32), jax.ShapeDtypeStruct((L, D), BF16)],
                  compiler_params=_cparams(("parallel",)))(h2, G, bg.reshape(1, D), E)


def _ple_bwd(dh3, G, bg, E, name):
    L, D = dh3.shape
    tl = _row_tile(L)

    def body(d_ref, g_ref, b_ref, e_ref, de_ref, dg_ref, db_ref):
        i = pl.program_id(0)
        d = d_ref[...]
        sg = _sigmoid(g_ref[...] + b_ref[...])
        de_ref[...] = (d * sg).astype(BF16)
        dgp = d * e_ref[...] * sg * (1.0 - sg)
        dg_ref[...] = dgp.astype(BF16)

        @pl.when(i == 0)
        def _():
            db_ref[...] = jnp.zeros_like(db_ref)

        db_ref[...] += _sum8(dgp)

    row = pl.BlockSpec((tl, D), lambda i: (i, 0))
    vec = pl.BlockSpec((1, D), lambda i: (0, 0))
    acc = pl.BlockSpec((SUBLANES, D), lambda i: (0, 0))
    return _pcall(body, name=name, grid=(L // tl,), in_specs=[row, row, vec, row], out_specs=[row, row, acc],
                  out_shape=[jax.ShapeDtypeStruct((L, D), BF16), jax.ShapeDtypeStruct((L, D), BF16),
                             jax.ShapeDtypeStruct((SUBLANES, D), F32)],
                  compiler_params=_cparams(("arbitrary",)))(dh3, G, bg.reshape(1, D), E)


def _loss_head(h, target, name):
    L, D = h.shape
    tl = _row_tile(L)

    def body(h_ref, t_ref, d_ref, s_ref):
        i = pl.program_id(0)
        e = h_ref[...] - t_ref[...]
        d_ref[...] = e * (1.0 / D)

        @pl.when(i == 0)
        def _():
            s_ref[...] = jnp.zeros_like(s_ref)

        s_ref[...] += _sum8(e * e)

    row = pl.BlockSpec((tl, D), lambda i: (i, 0))
    acc = pl.BlockSpec((SUBLANES, D), lambda i: (0, 0))
    return _pcall(body, name=name, grid=(L // tl,), in_specs=[row, row], out_specs=[row, acc],
                  out_shape=[jax.ShapeDtypeStruct((L, D), F32), jax.ShapeDtypeStruct((SUBLANES, D), F32)],
                  compiler_params=_cparams(("arbitrary",)))(h, target)


def _halo_prev(tl, ncol_blocks_fn):
    return lambda j, i: (jnp.maximum(i * (tl // SUBLANES) - 1, 0), ncol_blocks_fn(j))


def _conv_taps(cur, prev, w_ref, K):
    acc = w_ref[K - 1:K, :] * cur
    for k in range(K - 1):
        acc = acc + w_ref[k:k + 1, :] * _shift_down(cur, prev, K - 1 - k)
    return acc


def _ffn_act_fwd(U, w, b, name):
    L, F2 = U.shape
    F = F2 // 2
    K = w.shape[0]
    tc = _pick(F, (1408, 256, 128))
    tl = _pick(L, (512, 256, 128)) if tc > 256 else _conv_row_tile(L, False)

    nj = F // tc

    def body(ug_ref, uv_ref, ugp_ref, uvp_ref, wg_ref, wv_ref, bg_ref, bv_ref, s_ref):
        i = pl.program_id(1)
        g = _conv_taps(ug_ref[...], jnp.where(i == 0, 0.0, ugp_ref[...]), wg_ref, K) + bg_ref[...]
        v = _conv_taps(uv_ref[...], jnp.where(i == 0, 0.0, uvp_ref[...]), wv_ref, K) + bv_ref[...]
        s_ref[...] = (g * _sigmoid(g) * v).astype(BF16)

    def both(shape, index):
        return [pl.BlockSpec(shape, lambda j, i: index(j, i)), pl.BlockSpec(shape, lambda j, i: index(j + nj, i))]

    b2 = b.reshape(1, F2)
    return _pcall(body, name=name, grid=(nj, L // tl),
                  in_specs=both((tl, tc), lambda j, i: (i, j)) + both((SUBLANES, tc), _halo_prev(tl, lambda j: j))
                  + both((K, tc), lambda j, i: (0, j)) + both((1, tc), lambda j, i: (0, j)),
                  out_specs=pl.BlockSpec((tl, tc), lambda j, i: (i, j)),
                  out_shape=jax.ShapeDtypeStruct((L, F), BF16),
                  compiler_params=_cparams(("parallel", "parallel")))(U, U, U, U, w, w, b2, b2)


def _halo_next(tl, L, rows):
    return lambda j, i: (jnp.minimum((i + 1) * (tl // rows), L // rows - 1), j)


def _conv_taps_t(cur, nxt, w_ref, K):
    acc = w_ref[K - 1:K, :] * cur
    for k in range(K - 1):
        acc = acc + w_ref[k:k + 1, :] * _shift_up(cur, nxt, K - 1 - k)
    return acc


BF16_ROWS = 16


def _ffn_act_bwd(U, dS, w, b, name):
    L, F2 = U.shape
    F = F2 // 2
    K = w.shape[0]
    tc = _pick(F, (1408, 256, 128))
    tl = _pick(L, (256, 128)) if tc > 256 else _conv_row_tile(L, True)
    nl = L // tl

    nj = F // tc

    def body(ug_ref, uv_ref, ugp_ref, uvp_ref, ugn_ref, uvn_ref, ds_ref, dsn_ref, wg_ref, wv_ref, bg_ref, bv_ref,
             dug_ref, duv_ref, dwg_ref, dwv_ref, dbg_ref, dbv_ref):
        i = pl.program_id(1)
        cur = (ug_ref[...], uv_ref[...])
        prev = (jnp.where(i == 0, 0.0, ugp_ref[...]), jnp.where(i == 0, 0.0, uvp_ref[...]))

        def at_conv_out(x, xprev, ds):
            g = _conv_taps(x[0], xprev[0], wg_ref, K) + bg_ref[...]
            v = _conv_taps(x[1], xprev[1], wv_ref, K) + bv_ref[...]
            sg = _sigmoid(g)
            return ds * v * sg * (1.0 + g * (1.0 - sg)), ds * g * sg

        duc = at_conv_out(cur, prev, ds_ref[...].astype(F32))
        duc_n = at_conv_out((ugn_ref[...], uvn_ref[...]), (cur[0][tl - SUBLANES:], cur[1][tl - SUBLANES:]),
                            dsn_ref[...].astype(F32)[:SUBLANES])

        @pl.when(i == 0)
        def _():
            for r in (dwg_ref, dwv_ref, dbg_ref, dbv_ref):
                r[...] = jnp.zeros_like(r)

        halves = ((dug_ref, dwg_ref, dbg_ref, wg_ref), (duv_ref, dwv_ref, dbv_ref, wv_ref))
        for half, (du_ref, dw_ref, db_ref, w_ref) in enumerate(halves):
            d = duc[half]
            du_ref[...] = _conv_taps_t(d, jnp.where(i == nl - 1, 0.0, duc_n[half]), w_ref, K).astype(BF16)
            db_ref[...] += _sum8(d)
            for k in range(K):
                dw_ref[k * SUBLANES:(k + 1) * SUBLANES, :] += _sum8(d * _shift_down(cur[half], prev[half], K - 1 - k))

    def both(shape, index):
        return [pl.BlockSpec(shape, lambda j, i: index(j, i)), pl.BlockSpec(shape, lambda j, i: index(j + nj, i))]

    b2 = b.reshape(1, F2)
    du_specs, du_shapes = [pl.BlockSpec((tl, tc), lambda j, i: (i, j))] * 2, [jax.ShapeDtypeStruct((L, F), BF16)] * 2
    dw_specs = [pl.BlockSpec((K * SUBLANES, tc), lambda j, i: (0, j))] * 2
    dw_shapes = [jax.ShapeDtypeStruct((K * SUBLANES, F), F32)] * 2
    db_specs, db_shapes = [pl.BlockSpec((SUBLANES, tc), lambda j, i: (0, j))] * 2, [jax.ShapeDtypeStruct((SUBLANES, F), F32)] * 2
    return _pcall(body, name=name, grid=(nj, nl),
                  in_specs=both((tl, tc), lambda j, i: (i, j)) + both((SUBLANES, tc), _halo_prev(tl, lambda j: j))
                  + both((SUBLANES, tc), _halo_next(tl, L, SUBLANES))
                  + [pl.BlockSpec((tl, tc), lambda j, i: (i, j)), pl.BlockSpec((BF16_ROWS, tc), _halo_next(tl, L, BF16_ROWS))]
                  + both((K, tc), lambda j, i: (0, j)) + both((1, tc), lambda j, i: (0, j)),
                  out_specs=du_specs + dw_specs + db_specs, out_shape=du_shapes + dw_shapes + db_shapes,
                  compiler_params=_cparams(("parallel", "arbitrary")))(U, U, U, U, U, U, dS, dS, w, w, b2, b2)


def _sconv_fwd(Ac, w, name):
    L, C3 = Ac.shape
    C = C3 // 3
    K = w.shape[0]
    tl = _conv_row_tile(L, False)
    tc = LANES

    def body(a_ref, ap_ref, w_ref, y_ref):
        i = pl.program_id(1)
        a = a_ref[...]
        ap = ap_ref[...]
        p = a[:, tc:2 * tc] * a[:, 2 * tc:]
        pp = jnp.where(i == 0, 0.0, ap[:, tc:2 * tc] * ap[:, 2 * tc:])
        y_ref[...] = (a[:, :tc] * _conv_taps(p, pp, w_ref, K)).astype(BF16)

    return _pcall(body, name=name, grid=(C // tc, L // tl),
                  in_specs=[pl.BlockSpec((tl, 3 * tc), lambda j, i: (i, j)),
                            pl.BlockSpec((SUBLANES, 3 * tc), _halo_prev(tl, lambda j: j)),
                            pl.BlockSpec((K, tc), lambda j, i: (0, j))],
                  out_specs=pl.BlockSpec((tl, tc), lambda j, i: (i, j)),
                  out_shape=jax.ShapeDtypeStruct((L, C), BF16),
                  compiler_params=_cparams(("parallel", "parallel")))(Ac, Ac, w)


def _sconv_bwd_dc(Ac, dy, w, name):
    L, C3 = Ac.shape
    C = C3 // 3
    K = w.shape[0]
    tl = _conv_row_tile(L, False)
    tc = LANES

    def body(a_ref, ap_ref, dy_ref, dc_ref, dw_ref):
        i = pl.program_id(1)
        a = a_ref[...]
        ap = ap_ref[...]
        p = a[:, tc:2 * tc] * a[:, 2 * tc:]
        pp = jnp.where(i == 0, 0.0, ap[:, tc:2 * tc] * ap[:, 2 * tc:])
        dc = dy_ref[...] * a[:, :tc]
        dc_ref[...] = dc

        @pl.when(i == 0)
        def _():
            dw_ref[...] = jnp.zeros_like(dw_ref)

        for k in range(K):
            dw_ref[k * SUBLANES:(k + 1) * SUBLANES, :] += _sum8(dc * _shift_down(p, pp, K - 1 - k))

    return _pcall(body, name=name, grid=(C // tc, L // tl),
                  in_specs=[pl.BlockSpec((tl, 3 * tc), lambda j, i: (i, j)),
                            pl.BlockSpec((SUBLANES, 3 * tc), _halo_prev(tl, lambda j: j)),
                            pl.BlockSpec((tl, tc), lambda j, i: (i, j))],
                  out_specs=[pl.BlockSpec((tl, tc), lambda j, i: (i, j)),
                             pl.BlockSpec((K * SUBLANES, tc), lambda j, i: (0, j))],
                  out_shape=[jax.ShapeDtypeStruct((L, C), F32), jax.ShapeDtypeStruct((K * SUBLANES, C), F32)],
                  compiler_params=_cparams(("parallel", "arbitrary")))(Ac, Ac, dy)


def _sconv_bwd_da(Ac, dy, dc, w, name):
    L, C3 = Ac.shape
    C = C3 // 3
    K = w.shape[0]
    tl = _conv_row_tile(L, False)
    tc = LANES
    nl = L // tl

    def body(a_ref, ap_ref, dy_ref, dc_ref, dcn_ref, w_ref, o_ref):
        i = pl.program_id(1)
        a = a_ref[...]
        ap = ap_ref[...]
        gc, h = a[:, tc:2 * tc], a[:, 2 * tc:]
        p = gc * h
        pp = jnp.where(i == 0, 0.0, ap[:, tc:2 * tc] * ap[:, 2 * tc:])
        dgb = dy_ref[...] * _conv_taps(p, pp, w_ref, K)
        cur = dc_ref[...]
        nxt = jnp.where(i == nl - 1, 0.0, dcn_ref[...])
        dp = w_ref[K - 1:K, :] * cur
        for k in range(K - 1):
            dp = dp + w_ref[k:k + 1, :] * _shift_up(cur, nxt, K - 1 - k)
        o_ref[...] = jnp.concatenate([dgb, dp * h, dp * gc], axis=1).astype(BF16)

    return _pcall(body, name=name, grid=(C // tc, nl),
                  in_specs=[pl.BlockSpec((tl, 3 * tc), lambda j, i: (i, j)),
                            pl.BlockSpec((SUBLANES, 3 * tc), _halo_prev(tl, lambda j: j)),
                            pl.BlockSpec((tl, tc), lambda j, i: (i, j)),
                            pl.BlockSpec((tl, tc), lambda j, i: (i, j)),
                            pl.BlockSpec((SUBLANES, tc), lambda j, i: (jnp.minimum((i + 1) * (tl // SUBLANES), L // SUBLANES - 1), j)),
                            pl.BlockSpec((K, tc), lambda j, i: (0, j))],
                  out_specs=pl.BlockSpec((tl, 3 * tc), lambda j, i: (i, j)),
                  out_shape=jax.ShapeDtypeStruct((L, C3), BF16),
                  compiler_params=_cparams(("parallel", "parallel")))(Ac, Ac, dy, dc, dc, w)


def _fox_gate_fwd(Af, bf, name):
    L, W = Af.shape
    tl = _pick(L, (512, 256, 128))

    def body(a_ref, b_ref, f_ref, carry):
        i = pl.program_id(0)

        @pl.when(i == 0)
        def _():
            carry[...] = jnp.zeros_like(carry)

        z = a_ref[...] + b_ref[...]
        logf = jnp.minimum(z, 0.0) - jnp.log(1.0 + jnp.exp(-jnp.abs(z)))
        f = _tri_mm(_tri(tl), logf) + carry[...]
        f_ref[...] = f
        carry[...] = f[tl - 1:tl, :]

    row = pl.BlockSpec((tl, W), lambda i: (i, 0))
    return _pcall(body, name=name, grid=(L // tl,), in_specs=[row, pl.BlockSpec((1, W), lambda i: (0, 0))], out_specs=row,
                  out_shape=jax.ShapeDtypeStruct((L, W), F32), scratch_shapes=[pltpu.VMEM((1, W), F32)],
                  compiler_params=_cparams(("arbitrary",)))(Af, bf)


def _fox_gate_bwd(Af, bf, dF, name):
    L, W = Af.shape
    tl = _pick(L, (512, 256, 128))
    nl = L // tl

    def body(a_ref, b_ref, df_ref, o_ref, db_ref, carry):
        i = pl.program_id(0)

        @pl.when(i == 0)
        def _():
            carry[...] = jnp.zeros_like(carry)
            db_ref[...] = jnp.zeros_like(db_ref)

        z = a_ref[...] + b_ref[...]
        dlogf = _tri_mm(_tri(tl, upper=True), df_ref[...]) + carry[...]
        carry[...] = dlogf[0:1, :]
        dz = dlogf * _sigmoid(-z)
        o_ref[...] = dz
        db_ref[...] += _sum8(dz)

    row = pl.BlockSpec((tl, W), lambda i: (nl - 1 - i, 0))
    return _pcall(body, name=name, grid=(nl,),
                  in_specs=[row, pl.BlockSpec((1, W), lambda i: (0, 0)), row],
                  out_specs=[row, pl.BlockSpec((SUBLANES, W), lambda i: (0, 0))],
                  out_shape=[jax.ShapeDtypeStruct((L, W), F32), jax.ShapeDtypeStruct((SUBLANES, W), F32)],
                  scratch_shapes=[pltpu.VMEM((1, W), F32)],
                  compiler_params=_cparams(("arbitrary",)))(Af, bf, dF)


def _attn_tiles(L):
    t = _pick(L, (512, 256, 128))
    return t, t


def _attn_scores(q, k, fq, fk, diag, scale):
    s = _dot(q, k, NT) * scale + (fq - fk)
    if not diag:
        return s
    row = lax.broadcasted_iota(jnp.int32, s.shape, 0)
    col = lax.broadcasted_iota(jnp.int32, s.shape, 1)
    return jnp.where(col <= row, s, NEG)


def _attn_geometry():
    Dh = FOX_HEAD_DIM
    hpt = LANES // Dh
    return Dh, hpt, FOX_HEADS // hpt


def _head_lanes(shape, Dh, hpt):
    lane = lax.broadcasted_iota(jnp.int32, shape, len(shape) - 1)
    return [(lane >= h * Dh) & (lane < (h + 1) * Dh) for h in range(hpt)]


def _attn_specs(t, L, hpt, ng):
    return dict(
        col=lambda off: pl.BlockSpec((t, LANES), lambda g, i: (i, g + off)),
        full=lambda off: pl.BlockSpec((L, LANES), lambda g, i: (0, g + off)),
        hq=pl.BlockSpec((hpt, t, 1), lambda g, i: (g, i, 0)),
        hk_full=pl.BlockSpec((hpt, 1, L), lambda g, i: (g, 0, 0)),
        hk=pl.BlockSpec((hpt, 1, t), lambda g, i: (g, 0, i)))


def _attn_fwd(qkv, Fq, Fk, name, gather=()):
    L = qkv.shape[0]
    Dh, hpt, ng = _attn_geometry()
    t, _ = _attn_tiles(L)
    scale = Dh ** -0.5
    n = len(gather)
    nsteps = ng * (L // t)

    def body(*refs):
        q_ref, k_ref, v_ref, fq_ref, fk_ref = refs[:5]
        o_ref, lse_ref = refs[5 + n:7 + n]
        qi = pl.program_id(1)
        step = pl.program_id(0) * (L // t) + qi
        if n:
            start, forward, finish = _gather_phases(refs[5:5 + n], refs[7 + n:7 + 2 * n], *refs[7 + 2 * n:])
            pl.when(step == 0)(start)
            pl.when(step == nsteps // 2)(forward)
        sel = _head_lanes((t, LANES), Dh, hpt)
        q2 = q_ref[...]
        qh = [jnp.where(sel[h], q2, 0) for h in range(hpt)]
        fq = [fq_ref[h] for h in range(hpt)]

        def chunk(j, carry, diag):
            rows = pl.ds(pl.multiple_of(j * t, t), t)
            kc, vc = k_ref[rows, :], v_ref[rows, :]
            out = []
            for h in range(hpt):
                m, l, acc = carry[h]
                s = _attn_scores(qh[h], kc, fq[h], fk_ref[h, :, rows], diag, scale)
                m_new = jnp.maximum(m, jnp.max(s, axis=-1, keepdims=True))
                p = jnp.exp(s - m_new)
                a = jnp.exp(m - m_new)
                out.append((m_new, a * l + jnp.sum(p, axis=-1, keepdims=True), a * acc + _dot(p.astype(BF16), vc, NN)))
            return tuple(out)

        init = tuple((jnp.full((t, 1), NEG, F32), jnp.zeros((t, 1), F32), jnp.zeros((t, LANES), F32)) for _ in range(hpt))
        fin = chunk(qi, lax.fori_loop(0, qi, lambda j, c: chunk(j, c, False), init), True)
        o = jnp.zeros((t, LANES), F32)
        for h, (m, l, acc) in enumerate(fin):
            o = jnp.where(sel[h], acc / l, o)
            lse_ref[h] = m + jnp.log(l)
        o_ref[...] = o.astype(BF16)
        if n:
            pl.when(step == nsteps - 1)(finish)

    sp = _attn_specs(t, L, hpt, ng)
    return _pcall(body, name=name, grid=(ng, L // t),
                  in_specs=[sp["col"](0), sp["full"](ng), sp["full"](2 * ng), sp["hq"], sp["hk_full"]] + [HBM_SPEC] * n,
                  out_specs=[sp["col"](0), sp["hq"]] + [HBM_SPEC] * n,
                  out_shape=[jax.ShapeDtypeStruct((L, ng * LANES), BF16), jax.ShapeDtypeStruct((FOX_HEADS, L, 1), F32)]
                  + _gather_shapes(gather),
                  scratch_shapes=_gather_sems(n) if n else [],
                  compiler_params=_cparams(("arbitrary", "arbitrary") if n else ("parallel", "arbitrary")))(
        qkv, qkv, qkv, Fq, Fk, *gather)


def _attn_bwd(qkv, Fq, Fk, lse, do, name, exchange=()):
    L = qkv.shape[0]
    Dh, hpt, ng = _attn_geometry()
    _, tk = _attn_tiles(L)
    tq = _pick(L, (256, 128))
    nkc = L // tk
    scale = Dh ** -0.5

    n = len(exchange)
    nsteps = ng * (L // tq)

    def body(*refs):
        q_ref, k_ref, v_ref, fq_ref, fk_ref, lse_ref, do_ref = refs[:7]
        dq_ref, dk_ref, dv_ref, df_ref = refs[7 + n:11 + n]
        p_s, dp_s = refs[11 + 2 * n:13 + 2 * n]
        qi = pl.program_id(1)
        step = pl.program_id(0) * (L // tq) + qi
        if n:
            start, finish = _exchange_phases(refs[7:7 + n], refs[11 + n:11 + 2 * n], *refs[13 + 2 * n:])
            pl.when(step == 0)(start)

        @pl.when(qi == 0)
        def _():
            dk_ref[...] = jnp.zeros_like(dk_ref)
            dv_ref[...] = jnp.zeros_like(dv_ref)
            df_ref[...] = jnp.zeros_like(df_ref)

        sel = _head_lanes((tq, LANES), Dh, hpt)
        q2, do2 = q_ref[...], do_ref[...]
        jd = (qi * tq) // tk
        off = qi * tq - jd * tk
        dq = jnp.zeros((tq, LANES), F32)
        for h in range(hpt):
            qh, doh = jnp.where(sel[h], q2, 0), jnp.where(sel[h], do2, 0)
            fq, lse = fq_ref[h], lse_ref[h]

            def first(j, acc, diag):
                rows = pl.ds(pl.multiple_of(j * tk, tk), tk)
                s = _dot(qh, k_ref[rows, :], NT) * scale + (fq - fk_ref[h, :, rows])
                if diag:
                    row = lax.broadcasted_iota(jnp.int32, s.shape, 0) + off
                    s = jnp.where(lax.broadcasted_iota(jnp.int32, s.shape, 1) <= row, s, NEG)
                p = jnp.exp(s - lse)
                dp = _dot(doh, v_ref[rows, :], NT)
                p_s[j] = p
                dp_s[j] = dp
                return acc + jnp.sum(p * dp, axis=-1, keepdims=True)

            delta = first(jd, lax.fori_loop(0, jd, lambda j, c: first(j, c, False), jnp.zeros((tq, 1), F32)), True)

            def second(j, acc):
                rows = pl.ds(pl.multiple_of(j * tk, tk), tk)
                p = p_s[j]
                ds = p * (dp_s[j] - delta)
                dsb = ds.astype(BF16)
                dk_ref[rows, :] += _dot(dsb, qh, TN)
                dv_ref[rows, :] += _dot(p.astype(BF16), doh, TN)
                df_ref[h, :, rows] -= jnp.sum(ds, axis=0, keepdims=True)
                return acc + _dot(dsb, k_ref[rows, :], NN)

            dq = jnp.where(sel[h], lax.fori_loop(0, jd + 1, second, jnp.zeros((tq, LANES), F32)), dq)
        dq_ref[...] = (dq * scale).astype(BF16)

        @pl.when(qi == L // tq - 1)
        def _():
            dk_ref[...] *= scale

        if n:
            pl.when(step == nsteps - 1)(finish)

    sp = _attn_specs(tq, L, hpt, ng)
    return _pcall(body, name=name, grid=(ng, L // tq),
                  in_specs=[sp["col"](0), sp["full"](ng), sp["full"](2 * ng), sp["hq"], sp["hk_full"], sp["hq"], sp["col"](0)]
                  + [HBM_SPEC] * n,
                  out_specs=[sp["col"](0), sp["full"](0), sp["full"](0), sp["hk_full"]] + [HBM_SPEC] * n,
                  out_shape=[jax.ShapeDtypeStruct((L, ng * LANES), BF16), jax.ShapeDtypeStruct((L, ng * LANES), F32),
                             jax.ShapeDtypeStruct((L, ng * LANES), F32), jax.ShapeDtypeStruct((FOX_HEADS, 1, L), F32)]
                  + [jax.ShapeDtypeStruct(g.shape, g.dtype) for g in exchange],
                  scratch_shapes=[pltpu.VMEM((nkc, tq, tk), F32), pltpu.VMEM((nkc, tq, tk), F32)] + (_gather_sems(n) if n else []),
                  compiler_params=pltpu.CompilerParams(
                      vmem_limit_bytes=ATTN_BWD_VMEM_LIMIT,
                      dimension_semantics=("arbitrary", "arbitrary") if n else ("parallel", "arbitrary")))(
        qkv, qkv, qkv, Fq, Fk, lse, do, *exchange)


def _mconv_fwd(xr, w, b, name):
    L, C = xr.shape
    K = w.shape[0]
    tl = _conv_row_tile(L, False)
    tc = _pick(C, (512, 384, 256, 128))

    def body(x_ref, xp_ref, w_ref, b_ref, o_ref):
        i = pl.program_id(1)
        prev = jnp.where(i == 0, 0.0, xp_ref[...])
        pre = _conv_taps(x_ref[...], prev, w_ref, K) + b_ref[...]
        o_ref[...] = pre * _sigmoid(pre)

    return _pcall(body, name=name, grid=(C // tc, L // tl),
                  in_specs=[pl.BlockSpec((tl, tc), lambda j, i: (i, j)),
                            pl.BlockSpec((SUBLANES, tc), _halo_prev(tl, lambda j: j)),
                            pl.BlockSpec((K, tc), lambda j, i: (0, j)),
                            pl.BlockSpec((1, tc), lambda j, i: (0, j))],
                  out_specs=pl.BlockSpec((tl, tc), lambda j, i: (i, j)),
                  out_shape=jax.ShapeDtypeStruct((L, C), F32),
                  compiler_params=_cparams(("parallel", "parallel")))(xr, xr, w, b.reshape(1, C))


def _mconv_bwd(xr, dact, w, b, name):
    L, C = xr.shape
    K = w.shape[0]
    tl = _conv_row_tile(L, True)
    tc = _pick(C, (512, 384, 256, 128))
    nl = L // tl

    def body(x_ref, xp_ref, xn_ref, d_ref, dn_ref, w_ref, b_ref, o_ref, dw_ref, db_ref):
        i = pl.program_id(1)
        cur = x_ref[...]
        prev = jnp.where(i == 0, 0.0, xp_ref[...])

        def at_conv_out(x, xprev, d):
            pre = _conv_taps(x, xprev, w_ref, K) + b_ref[...]
            sg = _sigmoid(pre)
            return d * sg * (1.0 + pre * (1.0 - sg))

        dpre = at_conv_out(cur, prev, d_ref[...])
        dpre_n = jnp.where(i == nl - 1, 0.0, at_conv_out(xn_ref[...], cur[tl - SUBLANES:], dn_ref[...]))
        o_ref[...] = _conv_taps_t(dpre, dpre_n, w_ref, K).astype(BF16)

        @pl.when(i == 0)
        def _():
            dw_ref[...] = jnp.zeros_like(dw_ref)
            db_ref[...] = jnp.zeros_like(db_ref)

        db_ref[...] += _sum8(dpre)
        for k in range(K):
            dw_ref[k * SUBLANES:(k + 1) * SUBLANES, :] += _sum8(dpre * _shift_down(cur, prev, K - 1 - k))

    return _pcall(body, name=name, grid=(C // tc, nl),
                  in_specs=[pl.BlockSpec((tl, tc), lambda j, i: (i, j)),
                            pl.BlockSpec((SUBLANES, tc), _halo_prev(tl, lambda j: j)),
                            pl.BlockSpec((SUBLANES, tc), _halo_next(tl, L, SUBLANES)),
                            pl.BlockSpec((tl, tc), lambda j, i: (i, j)),
                            pl.BlockSpec((SUBLANES, tc), _halo_next(tl, L, SUBLANES)),
                            pl.BlockSpec((K, tc), lambda j, i: (0, j)),
                            pl.BlockSpec((1, tc), lambda j, i: (0, j))],
                  out_specs=[pl.BlockSpec((tl, tc), lambda j, i: (i, j)),
                             pl.BlockSpec((K * SUBLANES, tc), lambda j, i: (0, j)),
                             pl.BlockSpec((SUBLANES, tc), lambda j, i: (0, j))],
                  out_shape=[jax.ShapeDtypeStruct((L, C), BF16), jax.ShapeDtypeStruct((K * SUBLANES, C), F32),
                             jax.ShapeDtypeStruct((SUBLANES, C), F32)],
                  compiler_params=_cparams(("parallel", "arbitrary")))(xr, xr, xr, dact, dact, w, b.reshape(1, C))


def _head_selector(R, P, heads_first):
    shape = (R, R * P) if heads_first else (R * P, R)
    head = lax.broadcasted_iota(jnp.int32, shape, 0 if heads_first else 1)
    lane = lax.broadcasted_iota(jnp.int32, shape, 1 if heads_first else 0)
    d = lane - head * P
    return jnp.where((d >= 0) & (d < P), 1.0, 0.0).astype(BF16)


def _ssd_prelude(dtc_ref, dtr_ref, bc_ref, br_ref, ac_ref, ar_ref, Q, R, P):
    raw_c = dtc_ref[...] + bc_ref[...]
    dt_c = _softplus(raw_c)
    dt_r = _softplus(dtr_ref[...] + br_ref[...])
    A_c = -jnp.exp(ac_ref[...])
    acs_c = _tri_mm(_tri(Q), dt_c * A_c)
    acs_r = _tri_mm(_tri(Q, upper=True), dt_r * (-jnp.exp(ar_ref[...])), tri_first=False)
    ea_c = jnp.exp(acs_c)
    dte_c = jnp.exp(acs_c[Q - 1:Q, :] - acs_c)
    wide = _tri_mm(_head_selector(R, P, True), jnp.concatenate([dt_c, ea_c, dte_c], axis=0), tri_first=False)
    return dict(raw_c=raw_c, dt_c=dt_c, A_c=A_c, acs_c=acs_c, acs_r=acs_r, ea_c=ea_c,
                DT=wide[:Q], EA=wide[Q:2 * Q], DTE=wide[2 * Q:])


def _ssd_decay_tile(pre, r, mask):
    return jnp.exp(jnp.where(mask, pre["acs_c"][:, r:r + 1] - pre["acs_r"][r:r + 1, :], NEG))


def _ssd_specs(Q, R, P, N, G, inner, rev=None):
    cc = (lambda c: c) if rev is None else rev
    return dict(
        x=pl.BlockSpec((Q, R * P), lambda g, c: (cc(c), g)),
        b=pl.BlockSpec((Q, N), lambda g, c: (cc(c), inner // N + g)),
        c=pl.BlockSpec((Q, N), lambda g, c: (cc(c), inner // N + G + g)),
        dtc=pl.BlockSpec((None, Q, R), lambda g, c: (g, cc(c), 0)),
        dtr=pl.BlockSpec((None, R, Q), lambda g, c: (g, 0, cc(c))),
        pc=pl.BlockSpec((None, 1, R), lambda g, c: (g, 0, 0)),
        pr=pl.BlockSpec((None, R, 1), lambda g, c: (g, 0, 0)),
        px=pl.BlockSpec((None, 1, R * P), lambda g, c: (g, 0, 0)),
        st=pl.BlockSpec((None, None, N, R * P), lambda g, c: (cc(c), g, 0, 0)))


def _ssd_fwd(act, dtc, dtr, bias_c, bias_r, alog_c, alog_r, dsk_x, name):
    G, L, R = dtc.shape
    N, P, Q = SSM_STATE, SSM_HEAD_DIM, SSM_CHUNK
    RP = R * P
    inner = G * RP
    nc = L // Q

    def body(x_ref, b_ref, c_ref, dtc_ref, dtr_ref, bc_ref, br_ref, ac_ref, ar_ref, dk_ref, y_ref, hp_ref, st):
        c = pl.program_id(1)

        @pl.when(c == 0)
        def _():
            st[...] = jnp.zeros_like(st)

        pre = _ssd_prelude(dtc_ref, dtr_ref, bc_ref, br_ref, ac_ref, ar_ref, Q, R, P)
        X = x_ref[...]
        XT = X * pre["DT"]
        Bb = b_ref[...].astype(BF16)
        Cb = c_ref[...].astype(BF16)
        CB = _dot(Cb, Bb, NT)
        mask = lax.broadcasted_iota(jnp.int32, (Q, Q), 0) >= lax.broadcasted_iota(jnp.int32, (Q, Q), 1)
        low = lax.broadcasted_iota(jnp.int32, (Q, 2 * P), 1) < P
        pieces = []
        for k in range(R // 2):
            xt2 = XT[:, 2 * P * k:2 * P * (k + 1)]
            acc = None
            for half in range(2):
                Gm = CB * _ssd_decay_tile(pre, 2 * k + half, mask)
                part = _dot(Gm.astype(BF16), jnp.where(low == (half == 0), xt2, 0.0).astype(BF16), NN)
                acc = part if acc is None else acc + part
            pieces.append(acc)
        HP = st[...]
        hp_ref[...] = HP
        yoff = pre["EA"] * _dot(Cb, HP.astype(BF16), NN)
        st[...] = HP * pre["EA"][Q - 1:Q, :] + _dot(Bb, (XT * pre["DTE"]).astype(BF16), TN)
        y_ref[...] = jnp.concatenate(pieces, axis=1) + yoff + dk_ref[...] * X

    sp = _ssd_specs(Q, R, P, N, G, inner)
    return _pcall(body, name=name, grid=(G, nc),
                  in_specs=[sp["x"], sp["b"], sp["c"], sp["dtc"], sp["dtr"], sp["pc"], sp["pr"], sp["pc"], sp["pr"], sp["px"]],
                  out_specs=[sp["x"], sp["st"]],
                  out_shape=[jax.ShapeDtypeStruct((L, inner), F32), jax.ShapeDtypeStruct((nc, G, N, RP), F32)],
                  scratch_shapes=[pltpu.VMEM((N, RP), F32)],
                  compiler_params=_cparams(("parallel", "arbitrary")))(act, act, act, dtc, dtr, bias_c, bias_r, alog_c, alog_r, dsk_x)


def _ssd_bwd(act, dtc, dtr, bias_c, bias_r, alog_c, alog_r, dsk_x, hprev, dy, name):
    G, L, R = dtc.shape
    N, P, Q = SSM_STATE, SSM_HEAD_DIM, SSM_CHUNK
    RP = R * P
    inner = G * RP
    nc = L // Q

    def body(x_ref, b_ref, c_ref, dtc_ref, dtr_ref, bc_ref, br_ref, ac_ref, ar_ref, dk_ref, hp_ref, dy_ref,
             dx_ref, db_ref, dc_ref, ddt_ref, gbias_ref, galog_ref, gdsk_ref, dst):
        c = pl.program_id(1)

        @pl.when(c == 0)
        def _():
            dst[...] = jnp.zeros_like(dst)
            gbias_ref[...] = jnp.zeros_like(gbias_ref)
            galog_ref[...] = jnp.zeros_like(galog_ref)
            gdsk_ref[...] = jnp.zeros_like(gdsk_ref)

        pre = _ssd_prelude(dtc_ref, dtr_ref, bc_ref, br_ref, ac_ref, ar_ref, Q, R, P)
        DT, EA, DTE = pre["DT"], pre["EA"], pre["DTE"]
        E_END = EA[Q - 1:Q, :]
        X, DY = x_ref[...], dy_ref[...]
        XT = X * DT
        Bb = b_ref[...].astype(BF16)
        Cb = c_ref[...].astype(BF16)
        CB = _dot(Cb, Bb, NT)
        HP, dH = hp_ref[...], dst[...]
        HPb, dHb = HP.astype(BF16), dH.astype(BF16)
        EDY = EA * DY
        EDYb = EDY.astype(BF16)
        dC = _dot(EDYb, HPb, NT)
        dHP = _dot(Cb, EDYb, TN)
        da_off = EDY * _dot(Cb, HPb, NN)
        Z = _dot(Bb, dHb, NN)
        XD = XT * DTE
        dB = _dot(XD.astype(BF16), dHb, NT)
        dXT = DTE * Z
        t_x = XD * Z
        hh = jnp.sum(dH * HP, axis=0, keepdims=True) * E_END
        dst[...] = dHP + dH * E_END
        mask = lax.broadcasted_iota(jnp.int32, (Q, Q), 0) >= lax.broadcasted_iota(jnp.int32, (Q, Q), 1)
        eye = lax.broadcasted_iota(jnp.int32, (Q, Q), 0) == lax.broadcasted_iota(jnp.int32, (Q, Q), 1)
        low = lax.broadcasted_iota(jnp.int32, (Q, 2 * P), 1) < P
        lane = lax.broadcasted_iota(jnp.int32, (Q, R), 1)
        dCB = jnp.zeros((Q, Q), F32)
        da_mat = jnp.zeros((Q, R), F32)
        pieces = []
        for k in range(R // 2):
            sl = slice(2 * P * k, 2 * P * (k + 1))
            xt2, dy2 = XT[:, sl], DY[:, sl]
            acc = None
            for half in range(2):
                r = 2 * k + half
                sel = low == (half == 0)
                Lm = _ssd_decay_tile(pre, r, mask)
                Gm = CB * Lm
                dyb = jnp.where(sel, dy2, 0.0).astype(BF16)
                part = _dot(Gm.astype(BF16), dyb, TN)
                acc = part if acc is None else acc + part
                dG = jnp.where(mask, _dot(dyb, jnp.where(sel, xt2, 0.0).astype(BF16), NT), 0.0)
                Mm = dG * Gm
                dCB = dCB + dG * Lm
                colsum = jnp.sum(jnp.where(eye, jnp.sum(Mm, axis=0, keepdims=True), 0.0), axis=1, keepdims=True)
                da_mat = jnp.where(lane == r, jnp.sum(Mm, axis=1, keepdims=True) - colsum, da_mat)
            pieces.append(acc)
        dXT = dXT + jnp.concatenate(pieces, axis=1)
        dCBb = dCB.astype(BF16)
        dc_ref[...] = dC + _dot(dCBb, Bb, NN)
        db_ref[...] = dB + _dot(dCBb, Cb, TN)
        dx_ref[...] = dXT * DT + dk_ref[...] * DY
        pad = jnp.zeros((SUBLANES - 1, RP), F32)
        sums = _tri_mm(_head_selector(R, P, False), jnp.concatenate([da_off, t_x, dXT * X, DY * X, hh, pad], axis=0), tri_first=False)
        t = sums[Q:2 * Q]
        da_end = jnp.sum(t, axis=0, keepdims=True) + sums[4 * Q:4 * Q + 1]
        rowi = lax.broadcasted_iota(jnp.int32, (Q, R), 0)
        da_mat = da_mat + sums[:Q] - t + jnp.where(rowi == Q - 1, da_end, 0.0)
        ddtA = _tri_mm(_tri(Q, upper=True), da_mat)
        ddt_raw = (ddtA * pre["A_c"] + sums[2 * Q:3 * Q]) * _sigmoid(pre["raw_c"])
        ddt_ref[...] = ddt_raw
        gbias_ref[...] += jnp.sum(ddt_raw, axis=0, keepdims=True)
        galog_ref[...] += jnp.sum(ddtA * pre["dt_c"], axis=0, keepdims=True) * pre["A_c"]
        gdsk_ref[...] += jnp.sum(sums[3 * Q:4 * Q], axis=0, keepdims=True)

    sp = _ssd_specs(Q, R, P, N, G, inner, rev=lambda c: nc - 1 - c)
    bout = pl.BlockSpec((Q, N), lambda g, c: (nc - 1 - c, g))
    return _pcall(body, name=name, grid=(G, nc),
                  in_specs=[sp["x"], sp["b"], sp["c"], sp["dtc"], sp["dtr"], sp["pc"], sp["pr"], sp["pc"], sp["pr"], sp["px"],
                            sp["st"], sp["x"]],
                  out_specs=[sp["x"], bout, bout, sp["dtc"], sp["pc"], sp["pc"], sp["pc"]],
                  out_shape=[jax.ShapeDtypeStruct((L, inner), F32), jax.ShapeDtypeStruct((L, G * N), F32),
                             jax.ShapeDtypeStruct((L, G * N), F32), jax.ShapeDtypeStruct((G, L, R), F32),
                             jax.ShapeDtypeStruct((G, 1, R), F32), jax.ShapeDtypeStruct((G, 1, R), F32),
                             jax.ShapeDtypeStruct((G, 1, R), F32)],
                  scratch_shapes=[pltpu.VMEM((N, RP), F32)],
                  compiler_params=_cparams(("parallel", "arbitrary")))(
        act, act, act, dtc, dtr, bias_c, bias_r, alog_c, alog_r, dsk_x, hprev, dy)


def _gnorm_fwd(y, z, g, name):
    L, Dn = y.shape
    gs = Dn // SSM_GROUPS
    tl = _row_tile(L)

    def body(y_ref, z_ref, g_ref, o_ref):
        for k in range(SSM_GROUPS):
            sl = slice(k * gs, (k + 1) * gs)
            zz = z_ref[:, sl]
            u = y_ref[:, sl] * zz * _sigmoid(zz)
            rstd = lax.rsqrt(jnp.mean(u * u, axis=-1, keepdims=True) + RMS_EPS)
            o_ref[:, sl] = (u * rstd * g_ref[:, sl]).astype(BF16)

    row = pl.BlockSpec((tl, Dn), lambda i: (i, 0))
    return _pcall(body, name=name, grid=(L // tl,), in_specs=[row, row, pl.BlockSpec((1, Dn), lambda i: (0, 0))],
                  out_specs=row, out_shape=jax.ShapeDtypeStruct((L, Dn), BF16),
                  compiler_params=_cparams(("parallel",)))(y, z, g.reshape(1, Dn))


def _gnorm_bwd(y, z, g, dout, name):
    L, Dn = y.shape
    gs = Dn // SSM_GROUPS
    tl = _row_tile(L)

    def body(y_ref, z_ref, g_ref, d_ref, dy_ref, dz_ref, dg_ref):
        i = pl.program_id(0)

        @pl.when(i == 0)
        def _():
            dg_ref[...] = jnp.zeros_like(dg_ref)

        for k in range(SSM_GROUPS):
            sl = slice(k * gs, (k + 1) * gs)
            zz = z_ref[:, sl]
            yy = y_ref[:, sl]
            sg = _sigmoid(zz)
            sil = zz * sg
            u = yy * sil
            rstd = lax.rsqrt(jnp.mean(u * u, axis=-1, keepdims=True) + RMS_EPS)
            n = u * rstd
            d = d_ref[:, sl]
            dn = d * g_ref[:, sl]
            du = rstd * (dn - n * jnp.mean(dn * n, axis=-1, keepdims=True))
            dy_ref[:, sl] = du * sil
            dz_ref[:, sl] = (du * yy * sg * (1.0 + zz * (1.0 - sg))).astype(BF16)
            dg_ref[:, sl] += _sum8(d * n)

    row = pl.BlockSpec((tl, Dn), lambda i: (i, 0))
    return _pcall(body, name=name, grid=(L // tl,), in_specs=[row, row, pl.BlockSpec((1, Dn), lambda i: (0, 0)), row],
                  out_specs=[row, row, pl.BlockSpec((SUBLANES, Dn), lambda i: (0, 0))],
                  out_shape=[jax.ShapeDtypeStruct((L, Dn), F32), jax.ShapeDtypeStruct((L, Dn), BF16),
                             jax.ShapeDtypeStruct((SUBLANES, Dn), F32)],
                  compiler_params=_cparams(("arbitrary",)))(y, z, g.reshape(1, Dn), dout)


def _adamw(w, g, m, v, name):
    rows, W = w.shape
    tr = _pick(rows, (512, 256, 128, 64, 32, 16, 8))
    c1 = 1.0 / (1.0 - ADAM_B1 ** ADAM_STEP)
    c2 = 1.0 / (1.0 - ADAM_B2 ** ADAM_STEP)

    def body(w_ref, g_ref, m_ref, v_ref, d_ref, nm_ref, nv_ref):
        g_ = g_ref[...]
        nm = ADAM_B1 * m_ref[...] + (1.0 - ADAM_B1) * g_
        nv = ADAM_B2 * v_ref[...] + (1.0 - ADAM_B2) * (g_ * g_)
        nm_ref[...] = nm
        nv_ref[...] = nv
        d_ref[...] = -ADAM_LR * ((nm * c1) / (jnp.sqrt(nv * c2) + ADAM_EPS) + ADAM_WD * w_ref[...])

    blk = pl.BlockSpec((tr, W), lambda i: (i, 0))
    return _pcall(body, name=name, grid=(rows // tr,), in_specs=[blk] * 4, out_specs=[blk] * 3,
                  out_shape=[jax.ShapeDtypeStruct((rows, W), F32)] * 3, compiler_params=_cparams(("parallel",)))(w, g, m, v)


def _sum_slots(x, name, extra=None):
    n, rows, W = x.shape
    tr = _pick(rows, (512, 256, 128, 64, 32, 16, 8))
    has_extra = extra is not None

    def body(*refs):
        if has_extra:
            e_ref, x_ref, o_ref = refs
            acc = e_ref[...].astype(F32)
            start = 0
        else:
            x_ref, o_ref = refs
            acc = x_ref[0].astype(F32)
            start = 1
        for s in range(start, n):
            acc = acc + x_ref[s].astype(F32)
        o_ref[...] = acc

    blk = pl.BlockSpec((tr, W), lambda i: (i, 0))
    xblk = pl.BlockSpec((n, tr, W), lambda i: (0, i, 0))
    return _pcall(body, name=name, grid=(rows // tr,), in_specs=([blk] if has_extra else []) + [xblk], out_specs=blk,
                  out_shape=jax.ShapeDtypeStruct((rows, W), F32), compiler_params=_cparams(("parallel",)))(
        *(([extra] if has_extra else []) + [x]))


def _add_pairs(a, b, name):
    n, rows, W = a.shape
    tr = _pick(rows, (512, 256, 128, 64, 32, 16, 8))

    def body(a_ref, b_ref, o_ref):
        o_ref[...] = (a_ref[...].astype(F32) + b_ref[...].astype(F32)).astype(BF16)

    blk = pl.BlockSpec((None, tr, W), lambda s, i: (s, i, 0))
    return _pcall(body, name=name, grid=(n, rows // tr), in_specs=[blk, blk], out_specs=blk,
                  out_shape=jax.ShapeDtypeStruct((n, rows, W), BF16), compiler_params=_cparams(("parallel", "parallel")))(a, b)


MESH = pl.DeviceIdType.MESH
HBM_SPEC = pl.BlockSpec(memory_space=pl.ANY)


def _me():
    return lax.axis_index("x"), lax.axis_index("y"), lax.axis_index("c")


def _all_gather(arrs, name):
    n = len(arrs)

    def body(*refs):
        start, forward, finish = _gather_phases(refs[:n], refs[n:2 * n], *refs[2 * n:])
        start()
        forward()
        finish()

    return _pcall(body, name=name, in_specs=[HBM_SPEC] * n, out_specs=[HBM_SPEC] * n,
                  out_shape=_gather_shapes(arrs), scratch_shapes=_gather_sems(n))(*arrs)


def _gather_shapes(arrs):
    return [jax.ShapeDtypeStruct((N_DEV,) + a.shape, a.dtype) for a in arrs]


def _gather_sems(n):
    return [pltpu.SemaphoreType.DMA((7 * n,)), pltpu.SemaphoreType.DMA((7 * n,)), pltpu.SemaphoreType.DMA((n,))]


def _gather_phases(ins, outs, send_sems, recv_sems, local_sems):
    n = len(ins)
    x, y, c = _me()
    me, sib = (x, y, c), (x, y, 1 - c)
    chips = [(1 - x, y), (x, 1 - y), (1 - x, 1 - y)]

    def slot(a, dev):
        return outs[a].at[4 * dev[0] + 2 * dev[1] + dev[2]]

    def copy(a, k, block, to, src=None):
        return pltpu.make_async_remote_copy(src_ref=slot(a, block) if src is None else src, dst_ref=slot(a, block),
                                            send_sem=send_sems.at[a * 7 + k], recv_sem=recv_sems.at[a * 7 + k],
                                            device_id=to, device_id_type=MESH)

    def mine():
        return [pltpu.make_async_copy(ins[a], slot(a, me), local_sems.at[a]) for a in range(n)]

    def first():
        out = []
        for a in range(n):
            out.append(copy(a, 0, me, sib, src=ins[a]))
            out += [copy(a, 1 + j, me, (*chip, c), src=ins[a]) for j, chip in enumerate(chips)]
        return out

    def passed():
        return [copy(a, 4 + j, (*chip, c), sib) for j, chip in enumerate(chips) for a in range(n)]

    def start():
        for cp in mine() + first():
            cp.start()

    def forward():
        fws = passed()
        for j, chip in enumerate(chips):
            for a in range(n):
                copy(a, 1 + j, (*chip, c), me).wait_recv()
                fws[j * n + a].start()

    def finish():
        for a in range(n):
            copy(a, 0, sib, me).wait_recv()
            for j, chip in enumerate(chips):
                copy(a, 4 + j, (*chip, 1 - c), me).wait_recv()
        for cp in first() + passed():
            cp.wait_send()
        for cp in mine():
            cp.wait()

    return start, forward, finish


def _exchange_phases(gs, outs, send_sems, recv_sems, local_sems):
    n = len(gs)
    x, y, c = _me()
    my_slot = 4 * x + 2 * y + c
    flips = [(fx, fy, fc) for fx in (0, 1) for fy in (0, 1) for fc in (0, 1) if fx or fy or fc]

    def peer(f):
        return tuple(1 - v if flip else v for v, flip in zip((x, y, c), f))

    def copies():
        out = []
        for a in range(n):
            for k, f in enumerate(flips):
                px, py, pc = peer(f)
                out.append(pltpu.make_async_remote_copy(
                    src_ref=gs[a].at[4 * px + 2 * py + pc], dst_ref=outs[a].at[my_slot],
                    send_sem=send_sems.at[a * 7 + k], recv_sem=recv_sems.at[a * 7 + k],
                    device_id=(px, py, pc), device_id_type=MESH))
        return out

    def arrivals():
        out = []
        for a in range(n):
            for k, f in enumerate(flips):
                px, py, pc = peer(f)
                slot = outs[a].at[4 * px + 2 * py + pc]
                out.append(pltpu.make_async_remote_copy(src_ref=slot, dst_ref=slot, send_sem=send_sems.at[a * 7 + k],
                                                        recv_sem=recv_sems.at[a * 7 + k], device_id=(px, py, pc),
                                                        device_id_type=MESH))
        return out

    def mine():
        return [pltpu.make_async_copy(gs[a].at[my_slot], outs[a].at[my_slot], local_sems.at[a]) for a in range(n)]

    def start():
        for cp in mine() + copies():
            cp.start()

    def finish():
        for cp in arrivals():
            cp.wait_recv()
        for cp in copies():
            cp.wait_send()
        for cp in mine():
            cp.wait()

    return start, finish


def _rs_sibling(gs, name):
    n = len(gs)

    def body(*refs):
        g_refs, o_refs = refs[:n], refs[n:2 * n]
        send_sems, recv_sems = refs[2 * n:]
        x, y, c = _me()
        sib = (x, y, 1 - c)
        cps = [pltpu.make_async_remote_copy(src_ref=g_refs[a].at[2 * q + (1 - c)], dst_ref=o_refs[a].at[q],
                                            send_sem=send_sems.at[4 * a + q], recv_sem=recv_sems.at[4 * a + q],
                                            device_id=sib, device_id_type=MESH) for a in range(n) for q in range(4)]
        for cp in cps:
            cp.start()
        for cp in cps:
            cp.wait()

    return _pcall(body, name=name, in_specs=[HBM_SPEC] * n, out_specs=[HBM_SPEC] * n,
                  out_shape=[jax.ShapeDtypeStruct((4,) + g.shape[1:], g.dtype) for g in gs],
                  scratch_shapes=[pltpu.SemaphoreType.DMA((4 * n,)), pltpu.SemaphoreType.DMA((4 * n,))])(*gs)


def _rs_chips(ps, name):
    n = len(ps)

    def body(*refs):
        p_refs, o_refs = refs[:n], refs[n:2 * n]
        send_sems, recv_sems = refs[2 * n:]
        x, y, c = _me()
        chips = [(1 - x, y), (x, 1 - y), (1 - x, 1 - y)]
        cps = [pltpu.make_async_remote_copy(src_ref=p_refs[a].at[2 * chip[0] + chip[1]], dst_ref=o_refs[a].at[j],
                                            send_sem=send_sems.at[3 * a + j], recv_sem=recv_sems.at[3 * a + j],
                                            device_id=(*chip, c), device_id_type=MESH)
               for j, chip in enumerate(chips) for a in range(n)]
        for cp in cps:
            cp.start()
        for cp in cps:
            cp.wait()

    return _pcall(body, name=name, in_specs=[HBM_SPEC] * n, out_specs=[HBM_SPEC] * n,
                  out_shape=[jax.ShapeDtypeStruct((3,) + p.shape[1:], p.dtype) for p in ps],
                  scratch_shapes=[pltpu.SemaphoreType.DMA((3 * n,)), pltpu.SemaphoreType.DMA((3 * n,))])(*ps)


def _reduce_scatter(gs, name):
    x, y, c = _me()
    from_sib = _rs_sibling(gs, name + "_sib")
    pairs = []
    for a, (g, fs) in enumerate(zip(gs, from_sib)):
        own = g.reshape((4, 2) + g.shape[1:])
        pairs.append(_add_pairs(jnp.where(c == 0, own[:, 0], own[:, 1]), fs, f"{name}_pair{a}"))
    from_chips = _rs_chips(pairs, name + "_chips")
    return [_sum_slots(fc, f"{name}_sum{a}", extra=lax.dynamic_index_in_dim(p, 2 * x + y, axis=0, keepdims=False))
            for a, (p, fc) in enumerate(zip(pairs, from_chips))]


BIG = ("even_w_in", "even_w_out", "odd_w_in", "odd_w_out", "ffn_w_up", "ffn_w_down", "ple_w_proj", "ple_w_gate")
SMALL_SHARDED = ("even_conv_w", "odd_conv_w", "odd_conv_b", "odd_norm_g", "ffn_conv_w")
REPLICATED = ("even_b_f", "odd_dt_bias", "odd_a_log", "odd_d_skip", "ln_mix_g", "ln_mix_b", "ffn_conv_b",
              "ln_ffn_g", "ln_ffn_b", "ple_b_gate")
WEIGHTS = ("even_w_in", "even_b_f", "even_conv_w", "even_w_out", "odd_w_in", "odd_conv_w", "odd_conv_b", "odd_dt_bias",
           "odd_a_log", "odd_d_skip", "odd_norm_g", "odd_w_out", "ln_mix_g", "ln_mix_b", "ffn_w_up", "ffn_conv_w",
           "ffn_conv_b", "ffn_w_down", "ln_ffn_g", "ln_ffn_b", "ple_w_proj", "ple_w_gate", "ple_b_gate")


def _full_shapes():
    d = _dims()
    return {
        "even_w_in": ((1, D_MODEL, d["even_in"]), 2), "even_b_f": ((1, FOX_HEADS), None),
        "even_conv_w": ((1, CONV_WIDTH, CONV_DIM), 2), "even_w_out": ((1, d["even_mix"], D_MODEL), 1),
        "odd_w_in": ((1, D_MODEL, d["odd_in"]), 2), "odd_conv_w": ((1, SSM_CONV_WIDTH, d["conv_ch"]), 2),
        "odd_conv_b": ((1, d["conv_ch"]), 1), "odd_dt_bias": ((1, d["ssm_heads"]), None),
        "odd_a_log": ((1, d["ssm_heads"]), None), "odd_d_skip": ((1, d["ssm_heads"]), None),
        "odd_norm_g": ((1, d["ssm_inner"]), 1), "odd_w_out": ((1, d["ssm_inner"], D_MODEL), 1),
        "ln_mix_g": ((DEPTH, D_MODEL), None), "ln_mix_b": ((DEPTH, D_MODEL), None),
        "ffn_w_up": ((DEPTH, D_MODEL, 2 * D_FF), 2), "ffn_conv_w": ((DEPTH, FFN_CONV_WIDTH, 2 * D_FF), 2),
        "ffn_conv_b": ((DEPTH, 2 * D_FF), None), "ffn_w_down": ((DEPTH, D_FF, D_MODEL), 1),
        "ln_ffn_g": ((DEPTH, D_MODEL), None), "ln_ffn_b": ((DEPTH, D_MODEL), None),
        "ple_w_proj": ((DEPTH, PLE_DIM, D_MODEL), 2), "ple_w_gate": ((DEPTH, D_MODEL, D_MODEL), 1),
        "ple_b_gate": ((DEPTH, D_MODEL), None),
    }


def _shard_shape(name):
    shape, ax = _full_shapes()[name]
    if ax is None:
        return shape
    return tuple(s // N_DEV if i == ax else s for i, s in enumerate(shape))


def _as2d(a, lead=0):
    return a.reshape(a.shape[:lead] + (-1, a.shape[-1]))


def _part_rows(shape):
    n = int(np.prod(shape))
    return -(-(-(-n // PACK_W)) // SUBLANES) * SUBLANES


def _pack_small(parts):
    out = []
    for p in parts:
        n, rows = int(np.prod(p.shape)), _part_rows(p.shape)
        out.append(jnp.pad(p.reshape(-1).astype(F32), (0, rows * PACK_W - n)).reshape(rows, PACK_W))
    return jnp.concatenate(out, axis=0)


def _unpack_small(pack, shapes):
    lead = pack.shape[:-2]
    out, off = [], 0
    for s in shapes:
        n, rows = int(np.prod(s)), _part_rows(s)
        part = pack[..., off:off + rows, :].reshape(lead + (-1,))[..., :n]
        out.append(part.reshape(lead + tuple(s)))
        off += rows
    return out


def _assemble(gathered, name):
    shape, ax = _full_shapes()[name]
    return jnp.moveaxis(gathered, 0, ax).reshape(shape)


def _split_dest(full, name):
    shape, ax = _full_shapes()[name]
    sh = shape[:ax] + (N_DEV, shape[ax] // N_DEV) + shape[ax + 1:]
    return jnp.moveaxis(full.reshape(sh), ax, 0)


def _interleave_cols(w, parts, tc):
    C = w.shape[-1] // parts
    sh = w.shape[:-1]
    return w.reshape(sh + (parts, C // tc, tc)).swapaxes(-3, -2).reshape(sh + (parts * C,))


def _deinterleave_cols(w, parts, tc):
    C = w.shape[-1] // parts
    sh = w.shape[:-1]
    return w.reshape(sh + (C // tc, parts, tc)).swapaxes(-3, -2).reshape(sh + (parts * C,))


def _pad_cols(a, to):
    return jnp.pad(a, ((0, 0), (0, to - a.shape[1])))


def _tail_fwd(i, h_in, mix, p_i, W, sp):
    r1, h1, h1b = _ln_fwd(h_in, mix, sp["ln_mix_g"][i], sp["ln_mix_b"][i], f"ln_mix_fwd{i}")
    U = _mm(h1b, W["ffn_up"][i], "nn", F32, f"ffn_up{i}")
    S = _ffn_act_fwd(U, sp["ffn_conv_w"][i], sp["ffn_conv_b"][i], f"ffn_act_fwd{i}")
    ffn = _mm(S, W["ffn_down"][i], "nn", F32, f"ffn_down{i}")
    r2, h2, h2b = _ln_fwd(h1, ffn, sp["ln_ffn_g"][i], sp["ln_ffn_b"][i], f"ln_ffn_fwd{i}")
    G = _mm(h2b, W["ple_gate"][i], "nn", F32, f"ple_gate{i}")
    E = _mm(p_i, W["ple_proj"][i], "nn", F32, f"ple_proj{i}")
    h3, h3b = _ple_fwd(h2, G, sp["ple_b_gate"][i], E, f"ple_fwd{i}")
    return h3, h3b, dict(r1=r1, h1b=h1b, U=U, S=S, r2=r2, h2b=h2b, G=G, E=E, p=p_i)


def _tail_bwd(i, dh3, sv, W, sp, grads):
    alpha = _alpha()
    dE, dGp, dbg = _ple_bwd(dh3, sv["G"], sp["ple_b_gate"][i], sv["E"], f"ple_bwd{i}")
    grads["ple_b_gate"][i] = dbg.sum(0)
    grads["ple_w_proj"][i] = _mm(sv["p"], dE, "tn", F32, f"d_ple_proj{i}")
    grads["ple_w_gate"][i] = _mm(sv["h2b"], dGp, "tn", F32, f"d_ple_gate{i}")
    dh2 = _mm(dGp, W["ple_gate"][i], "nt", F32, f"dx_ple_gate{i}", add=dh3)
    dr2, dr2b, dg, db = _ln_bwd(sv["r2"], dh2, sp["ln_ffn_g"][i], f"ln_ffn_bwd{i}")
    grads["ln_ffn_g"][i], grads["ln_ffn_b"][i] = dg.sum(0), db.sum(0)
    grads["ffn_w_down"][i] = _mm(sv["S"], dr2b, "tn", F32, f"d_ffn_down{i}")
    dS = _mm(dr2b, W["ffn_down"][i], "nt", BF16, f"dx_ffn_down{i}")
    dUg, dUv, dwg, dwv, dbg, dbv = _ffn_act_bwd(sv["U"], dS, sp["ffn_conv_w"][i], sp["ffn_conv_b"][i], f"ffn_act_bwd{i}")
    K = FFN_CONV_WIDTH
    grads["ffn_conv_w"][i] = jnp.concatenate([dwg.reshape(K, SUBLANES, -1).sum(1), dwv.reshape(K, SUBLANES, -1).sum(1)], axis=1)
    grads["ffn_conv_b"][i] = jnp.concatenate([dbg.sum(0), dbv.sum(0)])
    grads["ffn_w_up"][i] = jnp.concatenate([_mm(sv["h1b"], dUg, "tn", F32, f"d_ffn_up_g{i}"),
                                            _mm(sv["h1b"], dUv, "tn", F32, f"d_ffn_up_v{i}")], axis=1)
    dh1 = _mm(dUg, W["ffn_up"][i], "nt", F32, f"dx_ffn_up_g{i}", add=dr2, add_scale=alpha)
    dh1 = _mm(dUv, W["ffn_up"][i], "nt", F32, f"dx_ffn_up_v{i}", add=dh1, b_k_start=D_FF)
    dr1, dr1b, dg, db = _ln_bwd(sv["r1"], dh1, sp["ln_mix_g"][i], f"ln_mix_bwd{i}")
    grads["ln_mix_g"][i], grads["ln_mix_b"][i] = dg.sum(0), db.sum(0)
    return dr1, dr1b


def _even_fwd(h, W, sp, full, gather):
    L = h.shape[0]
    H, Dh = FOX_HEADS, FOX_HEAD_DIM
    Ac = _mm(h, W["even_in_conv"], "nn", F32, "even_in_conv")
    qkv = _mm(h, W["even_in_qkv"], "nn", BF16, "even_in_qkv")
    Af = _mm(h, W["even_in_f"], "nn", F32, "even_in_f")
    y_a = _sconv_fwd(Ac, sp["even_conv_w_il"], "sconv_fwd")
    Fc = _fox_gate_fwd(Af, sp["even_b_f_pad"], "fox_gate_fwd")
    Fh = Fc[:, :H].T
    Fq, Fk = Fh.reshape(H, L, 1), Fh.reshape(H, 1, L)
    o, lse, *gathered = _attn_fwd(qkv, Fq, Fk, "attn_fwd", gather)
    for (n, i), g in zip(_late_units(), gathered):
        full[n][i] = _assemble_unit(g, n)
    _prepare_late(W, full)
    Y = jnp.concatenate([y_a, o], axis=1)
    mix = _mm(Y, W["even_out"], "nn", F32, "even_out")
    return mix, dict(h=h, Ac=Ac, Af=Af, qkv=qkv, Fq=Fq, Fk=Fk, lse=lse, Y=Y)


def _even_bwd(dmix, dres, sv, W, sp, grads, exchange):
    H, Dh = FOX_HEADS, FOX_HEAD_DIM
    C = CONV_DIM
    L = dmix.shape[0]
    grads["even_w_out"][0] = _mm(sv["Y"], dmix, "tn", F32, "d_even_out")
    exchange = [_split_unit(grads["even_w_out"][0], "even_w_out").astype(BF16)] + list(exchange)
    dY = _mm(dmix, W["even_out"], "nt", F32, "dx_even_out")
    dya = dY[:, :C]
    do = dY[:, C:].astype(BF16)
    dc, dcw = _sconv_bwd_dc(sv["Ac"], dya, sp["even_conv_w_il"], "sconv_bwd_dc")
    grads["even_conv_w"][0] = dcw.reshape(CONV_WIDTH, SUBLANES, -1).sum(1)
    dAc = _sconv_bwd_da(sv["Ac"], dya, dc, sp["even_conv_w_il"], "sconv_bwd_da")
    dq, dk, dv, dFk, *arrived = _attn_bwd(sv["qkv"], sv["Fq"], sv["Fk"], sv["lse"], do, "attn_bwd", exchange)
    dqkv = jnp.concatenate([dq, dk.astype(BF16), dv.astype(BF16)], axis=1)
    dF = _pad_cols(dFk.reshape(H, L).T, LANES)
    dAf, dbf = _fox_gate_bwd(sv["Af"], sp["even_b_f_pad"], dF, "fox_gate_bwd")
    grads["even_b_f"][0] = dbf.sum(0)[:H]
    h = sv["h"]
    gc = _deinterleave_cols(_mm(h, dAc, "tn", F32, "d_even_in_conv"), 3, LANES)
    gq = _mm(h, dqkv, "tn", F32, "d_even_in_qkv")
    gf = _mm(h, dAf, "tn", F32, "d_even_in_f")[:, :H]
    grads["even_w_in"][0] = jnp.concatenate([gc, gq, gf], axis=1)
    dh = _mm(dAc, W["even_in_conv"], "nt", F32, "dx_even_in_conv", add=dres, add_scale=_alpha())
    dh = _mm(dqkv, W["even_in_qkv"], "nt", F32, "dx_even_in_qkv", add=dh)
    dh = _mm(dAf, W["even_in_f"], "nt", F32, "dx_even_in_f", add=dh)
    return dh, arrived


def _group_layouts(v, G):
    R = v.shape[0] // G
    return v.reshape(G, 1, R), v.reshape(G, R, 1)


def _odd_fwd(h, W, sp):
    d = _dims()
    L = h.shape[0]
    Hs, G, N, P = d["ssm_heads"], SSM_GROUPS, SSM_STATE, SSM_HEAD_DIM
    R = Hs // G
    inner = d["ssm_inner"]
    z = _mm(h, W["odd_in_z"], "nn", F32, "odd_in_z")
    xr = _mm(h, W["odd_in_x"], "nn", F32, "odd_in_x")
    dtp = _mm(h, W["odd_in_dt"], "nn", F32, "odd_in_dt")
    act = _mconv_fwd(xr, sp["odd_conv_w"], sp["odd_conv_b"], "mconv_fwd")
    dtg = dtp[:, :Hs].reshape(L, G, R)
    dtc, dtr = dtg.transpose(1, 0, 2), dtg.transpose(1, 2, 0)
    dsk_x = jnp.repeat(sp["odd_d_skip"], P).reshape(G, 1, R * P)
    ssd_in = (act, dtc, dtr) + _group_layouts(sp["odd_dt_bias"], G) + _group_layouts(sp["odd_a_log"], G) + (dsk_x,)
    y, hprev = _ssd_fwd(*ssd_in, "ssd_fwd")
    u = _gnorm_fwd(y, z, sp["odd_norm_g"], "gnorm_fwd")
    mix = _mm(u, W["odd_out"], "nn", F32, "odd_out")
    return mix, dict(h=h, z=z, xr=xr, ssd_in=ssd_in, hprev=hprev, y=y, u=u)


def _odd_bwd(dmix, dres, sv, W, sp, grads):
    d = _dims()
    L = dmix.shape[0]
    Hs, G, N, P = d["ssm_heads"], SSM_GROUPS, SSM_STATE, SSM_HEAD_DIM
    grads["odd_w_out"][0] = _mm(sv["u"], dmix, "tn", F32, "d_odd_out")
    du = _mm(dmix, W["odd_out"], "nt", F32, "dx_odd_out")
    dy, dz, dg = _gnorm_bwd(sv["y"], sv["z"], sp["odd_norm_g"], du, "gnorm_bwd")
    grads["odd_norm_g"][0] = dg.sum(0)
    dxs, dB, dC, ddt, gbias, galog, gdsk = _ssd_bwd(*sv["ssd_in"], sv["hprev"], dy, "ssd_bwd")
    grads["odd_dt_bias"][0] = gbias.reshape(Hs)
    grads["odd_a_log"][0] = galog.reshape(Hs)
    grads["odd_d_skip"][0] = gdsk.reshape(Hs)
    dact = jnp.concatenate([dxs, dB, dC], axis=1)
    dxr, dcw, dcb = _mconv_bwd(sv["xr"], dact, sp["odd_conv_w"], sp["odd_conv_b"], "mconv_bwd")
    grads["odd_conv_w"][0] = dcw.reshape(SSM_CONV_WIDTH, SUBLANES, -1).sum(1)
    grads["odd_conv_b"][0] = dcb.sum(0)
    ddtp = _pad_cols(ddt.transpose(1, 0, 2).reshape(L, Hs), W["odd_in_dt"].shape[1])
    h = sv["h"]
    gz = _mm(h, dz, "tn", F32, "d_odd_in_z")
    gx = _mm(h, dxr, "tn", F32, "d_odd_in_x")
    gdt = _mm(h, ddtp, "tn", F32, "d_odd_in_dt")[:, :Hs]
    grads["odd_w_in"][0] = jnp.concatenate([gz, gx, gdt], axis=1)
    dh = _mm(dz, W["odd_in_z"], "nt", F32, "dx_odd_in_z", add=dres, add_scale=_alpha())
    dh = _mm(dxr, W["odd_in_x"], "nt", F32, "dx_odd_in_x", add=dh)
    dh = _mm(ddtp, W["odd_in_dt"], "nt", F32, "dx_odd_in_dt", add=dh)
    return dh


FIRST_UNIT = ("even_w_in", 0)


def _late_units():
    return [(n, i) for n in BIG for i in range(_full_shapes()[n][0][0]) if (n, i) != FIRST_UNIT]


def _assemble_unit(gathered, name):
    shape, ax = _full_shapes()[name]
    g = gathered.reshape((N_DEV,) + _shard_shape(name)[1:])
    return jnp.moveaxis(g, 0, ax - 1).reshape(shape[1:])


def _split_unit(full_layer, name):
    shape, ax = _full_shapes()[name]
    sh = shape[1:ax] + (N_DEV, shape[ax] // N_DEV) + shape[ax + 1:]
    return _as2d(jnp.moveaxis(full_layer.reshape(sh), ax - 1, 0), 1)


def _prepare_first(W, full):
    C, fd = CONV_DIM, _dims()["fox_dim"]
    ew = full["even_w_in"][0]
    W["even_in_conv"] = _interleave_cols(ew[:, :3 * C], 3, LANES)
    W["even_in_qkv"] = ew[:, 3 * C:3 * C + 3 * fd]
    W["even_in_f"] = _pad_cols(ew[:, 3 * C + 3 * fd:], LANES)


def _prepare_late(W, full):
    d = _dims()
    W["even_out"] = full["even_w_out"][0]
    ow = full["odd_w_in"][0]
    inner, cch, Hs = d["ssm_inner"], d["conv_ch"], d["ssm_heads"]
    W["odd_in_z"] = ow[:, :inner]
    W["odd_in_x"] = ow[:, inner:inner + cch]
    W["odd_in_dt"] = _pad_cols(ow[:, inner + cch:], -(-Hs // LANES) * LANES)
    W["odd_out"] = full["odd_w_out"][0]
    for key, name in (("ffn_up", "ffn_w_up"), ("ffn_down", "ffn_w_down"), ("ple_proj", "ple_w_proj"), ("ple_gate", "ple_w_gate")):
        W[key] = list(full[name])


def _prepare_small(full):
    sp = {}
    sp["even_conv_w_il"] = full["even_conv_w"][0]
    sp["even_b_f_pad"] = _pad_cols(full["even_b_f"], LANES)
    sp["odd_conv_w"] = full["odd_conv_w"][0]
    sp["odd_conv_b"] = full["odd_conv_b"][0]
    sp["odd_norm_g"] = full["odd_norm_g"][0]
    for n in ("odd_dt_bias", "odd_a_log", "odd_d_skip"):
        sp[n] = full[n][0]
    for n in ("ln_mix_g", "ln_mix_b", "ln_ffn_g", "ln_ffn_b", "ple_b_gate"):
        sp[n] = full[n]
    sp["ffn_conv_w"] = [full["ffn_conv_w"][i] for i in range(DEPTH)]
    sp["ffn_conv_b"] = [full["ffn_conv_b"][i] for i in range(DEPTH)]
    return sp


def _local_step(x, p, target, full, late_shards):
    sp = _prepare_small(full)
    W = {}
    _prepare_first(W, full)
    grads = {n: [None] * _full_shapes()[n][0][0] for n in WEIGHTS}
    pb = p.astype(BF16)
    mix0, sv_e = _even_fwd(x.astype(BF16), W, sp, full, late_shards)
    h3_0, h3_0b, sv_t0 = _tail_fwd(0, x, mix0, pb[0], W, sp)
    mix1, sv_o = _odd_fwd(h3_0b, W, sp)
    h3_1, _, sv_t1 = _tail_fwd(1, h3_0, mix1, pb[1], W, sp)
    dh, sq = _loss_head(h3_1, target, "loss_head")
    dr1, dr1b = _tail_bwd(1, dh, sv_t1, W, sp, grads)
    dh = _odd_bwd(dr1b, dr1, sv_o, W, sp, grads)
    dr1, dr1b = _tail_bwd(0, dh, sv_t0, W, sp, grads)
    outgoing = [_split_unit(grads[n][i], n).astype(BF16) for n, i in _late_units()[1:]]
    dx, incoming = _even_bwd(dr1b, dr1, sv_e, W, sp, grads, outgoing)
    reduced = {u: _sum_slots(r, f"sum_grads_{u[0]}{u[1]}") for u, r in zip(_late_units(), incoming)}
    grads = {n: v if n in BIG else jnp.stack(v) for n, v in grads.items()}
    return jnp.sum(sq), dx, grads, reduced


def kernel(x, p, even_w_in, even_b_f, even_conv_w, even_w_out, odd_w_in, odd_conv_w, odd_conv_b, odd_dt_bias, odd_a_log, odd_d_skip, odd_norm_g, odd_w_out, ln_mix_g, ln_mix_b, ffn_w_up, ffn_conv_w, ffn_conv_b, ffn_w_down, ln_ffn_g, ln_ffn_b, ple_w_proj, ple_w_gate, ple_b_gate, loss_target, m_even_w_in, m_even_b_f, m_even_conv_w, m_even_w_out, m_odd_w_in, m_odd_conv_w, m_odd_conv_b, m_odd_dt_bias, m_odd_a_log, m_odd_d_skip, m_odd_norm_g, m_odd_w_out, m_ln_mix_g, m_ln_mix_b, m_ffn_w_up, m_ffn_conv_w, m_ffn_conv_b, m_ffn_w_down, m_ln_ffn_g, m_ln_ffn_b, m_ple_w_proj, m_ple_w_gate, m_ple_b_gate, v_even_w_in, v_even_b_f, v_even_conv_w, v_even_w_out, v_odd_w_in, v_odd_conv_w, v_odd_conv_b, v_odd_dt_bias, v_odd_a_log, v_odd_d_skip, v_odd_norm_g, v_odd_w_out, v_ln_mix_g, v_ln_mix_b, v_ffn_w_up, v_ffn_conv_w, v_ffn_conv_b, v_ffn_w_down, v_ln_ffn_g, v_ln_ffn_b, v_ple_w_proj, v_ple_w_gate, v_ple_b_gate):
    args = locals()
    w = {n: args[n] for n in WEIGHTS}
    m = {n: args["m_" + n] for n in WEIGHTS}
    v = {n: args["v_" + n] for n in WEIGHTS}
    me = 4 * lax.axis_index("x") + 2 * lax.axis_index("y") + lax.axis_index("c")

    def shard(unit):
        return _as2d(w[unit[0]][unit[1]]).astype(BF16)

    first, small = _all_gather([shard(FIRST_UNIT), _pack_small([w[n] for n in SMALL_SHARDED])], "ag_weights")
    full = dict(w)
    for n in BIG:
        full[n] = [None] * _full_shapes()[n][0][0]
    full[FIRST_UNIT[0]][FIRST_UNIT[1]] = _assemble_unit(first, FIRST_UNIT[0])
    for n, g in zip(SMALL_SHARDED, _unpack_small(small, [_shard_shape(n) for n in SMALL_SHARDED])):
        full[n] = _assemble(g, n)

    sq, dx, grads, reduced = _local_step(x[0], p[:, 0], loss_target[0], full, [shard(u) for u in _late_units()])
    loss = lax.psum(0.5 * sq / D_MODEL, ("x", "y", "c"))

    n0, i0 = FIRST_UNIT
    (reduced[FIRST_UNIT],) = _reduce_scatter([_split_unit(grads[n0][i0], n0).astype(BF16)], "rs_grads")
    g_final = {n: jnp.stack([reduced[(n, i)] for i in range(_full_shapes()[n][0][0])]).reshape(_shard_shape(n)) for n in BIG}
    small_names = SMALL_SHARDED + REPLICATED
    (small_all,) = _all_gather([_pack_small([grads[n] for n in small_names])], "ag_small_grads")
    small_sum = _sum_slots(small_all, "sum_small_grads")
    for n, g in zip(small_names, _unpack_small(small_sum, [_full_shapes()[n][0] for n in small_names])):
        g_final[n] = lax.dynamic_index_in_dim(_split_dest(g, n), me, axis=0, keepdims=False) if n in SMALL_SHARDED else g

    out = {}
    for n in BIG:
        res = _adamw(*[_as2d(t[n]) for t in (w, g_final, m, v)], "adamw_" + n)
        out[n] = [r.reshape(_shard_shape(n)) for r in res]
    shapes = [_shard_shape(n) for n in small_names]
    res = _adamw(*[_pack_small([t[n] for n in small_names]) for t in (w, g_final, m, v)], "adamw_small")
    for n, d_, m_, v_ in zip(small_names, *[_unpack_small(r, shapes) for r in res]):
        out[n] = [d_, m_, v_]
    return (loss, dx[None], *[g_final[n] for n in WEIGHTS], *[out[n][0] for n in WEIGHTS],
            *[out[n][1] for n in WEIGHTS], *[out[n][2] for n in WEIGHTS])
```

```python
import jax
import jax.numpy as jnp
import numpy as np
from jax import lax
from jax.experimental import pallas as pl
from jax.experimental.pallas import tpu as pltpu

D_MODEL = 1024
SEQ = 8192
DEPTH = 2
CONV_DIM = 512
CONV_WIDTH = 3
FOX_HEADS = 8
FOX_HEAD_DIM = 64
SSM_HEAD_DIM = 64
SSM_GROUPS = 4
SSM_STATE = 128
SSM_CONV_WIDTH = 4
SSM_CHUNK = 128
D_FF = 2816
FFN_CONV_WIDTH = 3
PLE_DIM = 256
LN_EPS = 1e-5
RMS_EPS = 1e-5
ADAM_LR = 0.001
ADAM_B1 = 0.9
ADAM_B2 = 0.999
ADAM_EPS = 1e-08
ADAM_WD = 0.01
ADAM_STEP = 10
N_DEV = 8

F32 = jnp.float32
BF16 = jnp.bfloat16
NEG = -1e30
LANES = 128
SUBLANES = 8
PACK_W = 1024
VMEM_LIMIT = 48 * 1024 * 1024
ATTN_BWD_VMEM_LIMIT = 56 * 1024 * 1024


def _dims():
    fox_dim = FOX_HEADS * FOX_HEAD_DIM
    ssm_inner = 2 * D_MODEL
    ssm_heads = ssm_inner // SSM_HEAD_DIM
    conv_ch = ssm_inner + 2 * SSM_GROUPS * SSM_STATE
    return dict(fox_dim=fox_dim, even_in=3 * CONV_DIM + 3 * fox_dim + FOX_HEADS, even_mix=CONV_DIM + fox_dim,
                ssm_inner=ssm_inner, ssm_heads=ssm_heads, conv_ch=conv_ch, odd_in=ssm_inner + conv_ch + ssm_heads)


def _alpha():
    return (2.0 * DEPTH) ** 0.25


def _pick(dim, prefs):
    for p in prefs:
        if dim % p == 0:
            return p
    return dim


def _pcall(body, **kw):
    return pl.pallas_call(body, **kw)


def _cparams(sem=None, **kw):
    if sem is not None:
        kw["dimension_semantics"] = sem
    return pltpu.CompilerParams(vmem_limit_bytes=VMEM_LIMIT, **kw)


def _sigmoid(x):
    return 1.0 / (1.0 + jnp.exp(-x))


def _softplus(x):
    return jnp.maximum(x, 0.0) + jnp.log(1.0 + jnp.exp(-jnp.abs(x)))


def _sum8(x):
    n, c = x.shape
    return x.reshape(n // SUBLANES, SUBLANES, c).sum(axis=0)


def _dot(a, b, dims):
    return lax.dot_general(a, b, (dims, ((), ())), preferred_element_type=F32)


NN = ((1,), (0,))
NT = ((1,), (1,))
TN = ((0,), (0,))


def _split3(x):
    hi = x.astype(BF16)
    r1 = x - hi.astype(F32)
    mid = r1.astype(BF16)
    lo = (r1 - mid.astype(F32)).astype(BF16)
    return hi, mid, lo


def _tri_mm(tri_bf16, x, tri_first=True):
    if tri_first:
        return sum(_dot(tri_bf16, part, NN) for part in _split3(x))
    return sum(_dot(part, tri_bf16, NN) for part in _split3(x))


def _tri(n, upper=False):
    r = lax.broadcasted_iota(jnp.int32, (n, n), 0)
    c = lax.broadcasted_iota(jnp.int32, (n, n), 1)
    return jnp.where((r <= c) if upper else (r >= c), 1.0, 0.0).astype(BF16)


def _shift_down(cur, prev8, k):
    if k == 0:
        return cur
    ext = jnp.concatenate([prev8, cur], axis=0)
    return pltpu.roll(ext, k, axis=0)[SUBLANES:]


def _shift_up(cur, next8, k):
    if k == 0:
        return cur
    n = cur.shape[0]
    ext = jnp.concatenate([cur, next8], axis=0)
    return pltpu.roll(ext, n + SUBLANES - k, axis=0)[:n]


def _mm(a, b, mode, out_dtype, name, add=None, add_scale=1.0, b_k_start=0):
    if mode == "nn":
        (M, K), (K2, N) = a.shape, b.shape
    elif mode == "nt":
        (M, K), N = a.shape, b.shape[0]
        K2 = K if b.shape[1] >= b_k_start + K else None
    else:
        (K, M), (K2, N) = a.shape, b.shape
    assert K == K2, (a.shape, b.shape, mode)
    tm = _pick(M, (1024, 1408, 512, 256, 128))
    tn = _pick(N, (1408, 1024, 768, 512, 384, 256, 128))
    tk = K if K <= 2048 and b_k_start % K == 0 else _pick(K, (1408, 1024, 768, 512, 256, 128))
    assert b_k_start % tk == 0
    k0 = b_k_start // tk
    nk = K // tk
    dims = {"nn": NN, "nt": NT, "tn": TN}[mode]
    a_spec = pl.BlockSpec((tk, tm), lambda i, j, k: (k, i)) if mode == "tn" else pl.BlockSpec((tm, tk), lambda i, j, k: (i, k))
    b_spec = pl.BlockSpec((tn, tk), lambda i, j, k: (j, k + k0)) if mode == "nt" else pl.BlockSpec((tk, tn), lambda i, j, k: (k, j))
    o_spec = pl.BlockSpec((tm, tn), lambda i, j, k: (i, j))
    has_add = add is not None

    def body(*refs):
        a_ref, b_ref = refs[:2]
        add_ref = refs[2] if has_add else None
        o_ref = refs[2 + has_add]
        prod = _dot(a_ref[...].astype(BF16), b_ref[...].astype(BF16), dims)
        if nk == 1:
            if has_add:
                prod = prod + add_scale * add_ref[...].astype(F32)
            o_ref[...] = prod.astype(out_dtype)
            return
        acc = refs[3 + has_add]
        k = pl.program_id(2)

        @pl.when(k == 0)
        def _():
            if has_add:
                acc[...] = prod + add_scale * add_ref[...].astype(F32)
            else:
                acc[...] = prod

        @pl.when(k > 0)
        def _():
            acc[...] += prod

        @pl.when(k == nk - 1)
        def _():
            o_ref[...] = acc[...].astype(out_dtype)

    ins = [a, b] + ([add] if has_add else [])
    specs = [a_spec, b_spec] + ([o_spec] if has_add else [])
    return _pcall(body, name=name, grid=(M // tm, N // tn, nk), in_specs=specs, out_specs=o_spec,
                  out_shape=jax.ShapeDtypeStruct((M, N), out_dtype),
                  scratch_shapes=[pltpu.VMEM((tm, tn), F32)] if nk > 1 else [],
                  compiler_params=_cparams(("parallel", "parallel", "arbitrary")))(*ins)


def _row_tile(L):
    return _pick(L, (256, 128))


def _conv_row_tile(L, backward):
    return _pick(L, (512, 256, 128)) if backward else _pick(L, (1024, 512, 256, 128))


def _ln_fwd(h, mix, g, b, name):
    L, D = h.shape
    tl = _row_tile(L)
    alpha = _alpha()

    def body(h_ref, m_ref, g_ref, b_ref, r_ref, y_ref, yb_ref):
        r = alpha * h_ref[...] + m_ref[...]
        mu = jnp.mean(r, axis=-1, keepdims=True)
        xc = r - mu
        var = jnp.mean(xc * xc, axis=-1, keepdims=True)
        r_ref[...] = r
        y = xc * lax.rsqrt(var + LN_EPS) * g_ref[...] + b_ref[...]
        y_ref[...] = y
        yb_ref[...] = y.astype(BF16)

    row = pl.BlockSpec((tl, D), lambda i: (i, 0))
    vec = pl.BlockSpec((1, D), lambda i: (0, 0))
    return _pcall(body, name=name, grid=(L // tl,), in_specs=[row, row, vec, vec], out_specs=[row, row, row],
                  out_shape=[jax.ShapeDtypeStruct((L, D), F32)] * 2 + [jax.ShapeDtypeStruct((L, D), BF16)],
                  compiler_params=_cparams(("parallel",)))(h, mix, g.reshape(1, D), b.reshape(1, D))


def _ln_bwd(r, dy, g, name):
    L, D = r.shape
    tl = _row_tile(L)

    def body(r_ref, dy_ref, g_ref, dr_ref, drb_ref, dg_ref, db_ref):
        i = pl.program_id(0)
        r_ = r_ref[...]
        dy_ = dy_ref[...]
        mu = jnp.mean(r_, axis=-1, keepdims=True)
        xc = r_ - mu
        rstd = lax.rsqrt(jnp.mean(xc * xc, axis=-1, keepdims=True) + LN_EPS)
        xhat = xc * rstd
        dxh = dy_ * g_ref[...]
        dr = rstd * (dxh - jnp.mean(dxh, axis=-1, keepdims=True) - xhat * jnp.mean(dxh * xhat, axis=-1, keepdims=True))
        dr_ref[...] = dr
        drb_ref[...] = dr.astype(BF16)

        @pl.when(i == 0)
        def _():
            dg_ref[...] = jnp.zeros_like(dg_ref)
            db_ref[...] = jnp.zeros_like(db_ref)

        dg_ref[...] += _sum8(dy_ * xhat)
        db_ref[...] += _sum8(dy_)

    row = pl.BlockSpec((tl, D), lambda i: (i, 0))
    vec = pl.BlockSpec((1, D), lambda i: (0, 0))
    acc = pl.BlockSpec((SUBLANES, D), lambda i: (0, 0))
    return _pcall(body, name=name, grid=(L // tl,), in_specs=[row, row, vec], out_specs=[row, row, acc, acc],
                  out_shape=[jax.ShapeDtypeStruct((L, D), F32), jax.ShapeDtypeStruct((L, D), BF16),
                             jax.ShapeDtypeStruct((SUBLANES, D), F32), jax.ShapeDtypeStruct((SUBLANES, D), F32)],
                  compiler_params=_cparams(("arbitrary",)))(r, dy, g.reshape(1, D))


def _ple_fwd(h2, G, bg, E, name):
    L, D = h2.shape
    tl = _row_tile(L)

    def body(h_ref, g_ref, b_ref, e_ref, o_ref, ob_ref):
        o = h_ref[...] + _sigmoid(g_ref[...] + b_ref[...]) * e_ref[...]
        o_ref[...] = o
        ob_ref[...] = o.astype(BF16)

    row = pl.BlockSpec((tl, D), lambda i: (i, 0))
    vec = pl.BlockSpec((1, D), lambda i: (0, 0))
    return _pcall(body, name=name, grid=(L // tl,), in_specs=[row, row, vec, row], out_specs=[row, row],
                  out_shape=[jax.ShapeDtypeStruct((L, D), F32), jax.ShapeDtypeStruct((L, D), BF16)],
                  compiler_params=_cparams(("parallel",)))(h2, G, bg.reshape(1, D), E)


def _ple_bwd(dh3, G, bg, E, name):
    L, D = dh3.shape
    tl = _row_tile(L)

    def body(d_ref, g_ref, b_ref, e_ref, de_ref, dg_ref, db_ref):
        i = pl.program_id(0)
        d = d_ref[...]
        sg = _sigmoid(g_ref[...] + b_ref[...])
        de_ref[...] = (d * sg).astype(BF16)
        dgp = d * e_ref[...] * sg * (1.0 - sg)
        dg_ref[...] = dgp.astype(BF16)

        @pl.when(i == 0)
        def _():
            db_ref[...] = jnp.zeros_like(db_ref)

        db_ref[...] += _sum8(dgp)

    row = pl.BlockSpec((tl, D), lambda i: (i, 0))
    vec = pl.BlockSpec((1, D), lambda i: (0, 0))
    acc = pl.BlockSpec((SUBLANES, D), lambda i: (0, 0))
    return _pcall(body, name=name, grid=(L // tl,), in_specs=[row, row, vec, row], out_specs=[row, row, acc],
                  out_shape=[jax.ShapeDtypeStruct((L, D), BF16), jax.ShapeDtypeStruct((L, D), BF16),
                             jax.ShapeDtypeStruct((SUBLANES, D), F32)],
                  compiler_params=_cparams(("arbitrary",)))(dh3, G, bg.reshape(1, D), E)


def _loss_head(h, target, name):
    L, D = h.shape
    tl = _row_tile(L)

    def body(h_ref, t_ref, d_ref, s_ref):
        i = pl.program_id(0)
        e = h_ref[...] - t_ref[...]
        d_ref[...] = e * (1.0 / D)

        @pl.when(i == 0)
        def _():
            s_ref[...] = jnp.zeros_like(s_ref)

        s_ref[...] += _sum8(e * e)

    row = pl.BlockSpec((tl, D), lambda i: (i, 0))
    acc = pl.BlockSpec((SUBLANES, D), lambda i: (0, 0))
    return _pcall(body, name=name, grid=(L // tl,), in_specs=[row, row], out_specs=[row, acc],
                  out_shape=[jax.ShapeDtypeStruct((L, D), F32), jax.ShapeDtypeStruct((SUBLANES, D), F32)],
                  compiler_params=_cparams(("arbitrary",)))(h, target)


def _halo_prev(tl, ncol_blocks_fn):
    return lambda j, i: (jnp.maximum(i * (tl // SUBLANES) - 1, 0), ncol_blocks_fn(j))


STRIP_ROWS = 64


def _strip(s, R):
    return pl.ds(s * R if isinstance(s, int) else pl.multiple_of(s * R, R), R)


def _conv_taps(cur, prev, w_ref, K):
    acc = w_ref[K - 1:K, :] * cur
    for k in range(K - 1):
        acc = acc + w_ref[k:k + 1, :] * _shift_down(cur, prev, K - 1 - k)
    return acc


def _ffn_act_fwd(U, w, b, name):
    L, F2 = U.shape
    F = F2 // 2
    K = w.shape[0]
    tc = _pick(F, (256, 128))
    tl = _pick(L, (1024, 512, 256, 128))
    nj = F // tc
    R = STRIP_ROWS
    ns = tl // R

    def body(ug_ref, uv_ref, ugp_ref, uvp_ref, wg_ref, wv_ref, bg_ref, bv_ref, s_ref):
        i = pl.program_id(1)
        for lanes in (slice(c, c + LANES) for c in range(0, tc, LANES)):
            wts = [[w_ref[k:k + 1, lanes] for k in range(K)] for w_ref in (wg_ref, wv_ref)]
            bias = [b_ref[:, lanes] for b_ref in (bg_ref, bv_ref)]

            def gated(x, xprev8):
                g, v = (sum(wts[h][k] * _shift_down(x[h], xprev8[h], K - 1 - k) for k in range(K)) + bias[h] for h in range(2))
                return (g * _sigmoid(g) * v).astype(BF16)

            def step(s, xprev8):
                rows = _strip(s, R)
                x = (ug_ref[rows, lanes], uv_ref[rows, lanes])
                s_ref[rows, lanes] = gated(x, xprev8)
                return tuple(a[R - SUBLANES:] for a in x)

            first = step(0, (jnp.where(i == 0, 0.0, ugp_ref[:, lanes]), jnp.where(i == 0, 0.0, uvp_ref[:, lanes])))
            lax.fori_loop(1, ns, step, first)

    def both(shape, index):
        return [pl.BlockSpec(shape, lambda j, i: index(j, i)), pl.BlockSpec(shape, lambda j, i: index(j + nj, i))]

    b2 = b.reshape(1, F2)
    return _pcall(body, name=name, grid=(nj, L // tl),
                  in_specs=both((tl, tc), lambda j, i: (i, j)) + both((SUBLANES, tc), _halo_prev(tl, lambda j: j))
                  + both((K, tc), lambda j, i: (0, j)) + both((1, tc), lambda j, i: (0, j)),
                  out_specs=pl.BlockSpec((tl, tc), lambda j, i: (i, j)),
                  out_shape=jax.ShapeDtypeStruct((L, F), BF16),
                  compiler_params=_cparams(("parallel", "parallel")))(U, U, U, U, w, w, b2, b2)


def _halo_next(tl, L, rows):
    return lambda j, i: (jnp.minimum((i + 1) * (tl // rows), L // rows - 1), j)


BF16_ROWS = 16


def _ffn_act_bwd(U, dS, w, b, name):
    L, F2 = U.shape
    F = F2 // 2
    K = w.shape[0]
    tc = _pick(F, (256, 128))
    tl = _pick(L, (1024, 512, 256, 128))
    nl = L // tl
    nj = F // tc
    R = STRIP_ROWS
    ns = tl // R

    def body(ug_ref, uv_ref, ugp_ref, uvp_ref, ugn_ref, uvn_ref, ds_ref, dsn_ref, wg_ref, wv_ref, bg_ref, bv_ref,
             dug_ref, duv_ref, dwg_ref, dwv_ref, dbg_ref, dbv_ref):
        i = pl.program_id(1)

        @pl.when(i == 0)
        def _():
            for r in (dwg_ref, dwv_ref, dbg_ref, dbv_ref):
                r[...] = jnp.zeros_like(r)

        for lanes in (slice(c, c + LANES) for c in range(0, tc, LANES)):
            wts = [[w_ref[k:k + 1, lanes] for k in range(K)] for w_ref in (wg_ref, wv_ref)]
            bias = [b_ref[:, lanes] for b_ref in (bg_ref, bv_ref)]

            def strip(s):
                return _strip(s, R)

            def at_conv_out(x, xprev8, ds):
                sh = [[_shift_down(x[h], xprev8[h], K - 1 - k) for k in range(K)] for h in range(2)]
                g, v = (sum(wts[h][k] * sh[h][k] for k in range(K)) + bias[h] for h in range(2))
                sg = _sigmoid(g)
                return (ds * v * sg * (1.0 + g * (1.0 - sg)), ds * g * sg), sh

            def emit(rows, duc, duc_next8):
                for h, du_ref in enumerate((dug_ref, duv_ref)):
                    du = sum(wts[h][k] * _shift_up(duc[h], duc_next8[h], K - 1 - k) for k in range(K))
                    du_ref[rows, lanes] = du.astype(BF16)

            def accumulate(acc, duc, sh):
                dw, db = acc
                return (tuple(tuple(dw[h][k] + _sum8(duc[h] * sh[h][k]) for k in range(K)) for h in range(2)),
                        tuple(db[h] + _sum8(duc[h]) for h in range(2)))

            def last8(x):
                return tuple(a[R - SUBLANES:] for a in x)

            zero = jnp.zeros((SUBLANES, LANES), F32)
            x0 = (ug_ref[strip(0), lanes], uv_ref[strip(0), lanes])
            prev8 = (jnp.where(i == 0, 0.0, ugp_ref[:, lanes]), jnp.where(i == 0, 0.0, uvp_ref[:, lanes]))
            duc0, sh0 = at_conv_out(x0, prev8, ds_ref[strip(0), lanes].astype(F32))
            acc0 = accumulate(((((zero,) * K,) * 2), (zero,) * 2), duc0, sh0)

            def step(s, carry):
                xprev, ducprev, acc = carry
                x = (ug_ref[strip(s), lanes], uv_ref[strip(s), lanes])
                duc, sh = at_conv_out(x, last8(xprev), ds_ref[strip(s), lanes].astype(F32))
                emit(strip(s - 1), ducprev, tuple(d[:SUBLANES] for d in duc))
                return x, duc, accumulate(acc, duc, sh)

            xl, ducl, (dw, db) = lax.fori_loop(1, ns, step, (x0, duc0, acc0))
            ducn, _ = at_conv_out((ugn_ref[:, lanes], uvn_ref[:, lanes]), last8(xl), dsn_ref[:, lanes].astype(F32)[:SUBLANES])
            emit(strip(ns - 1), ducl, tuple(jnp.where(i == nl - 1, 0.0, d) for d in ducn))
            for h, (dw_ref, db_ref) in enumerate(((dwg_ref, dbg_ref), (dwv_ref, dbv_ref))):
                db_ref[:, lanes] += db[h]
                for k in range(K):
                    dw_ref[k * SUBLANES:(k + 1) * SUBLANES, lanes] += dw[h][k]

    def both(shape, index):
        return [pl.BlockSpec(shape, lambda j, i: index(j, i)), pl.BlockSpec(shape, lambda j, i: index(j + nj, i))]

    b2 = b.reshape(1, F2)
    du_specs, du_shapes = [pl.BlockSpec((tl, tc), lambda j, i: (i, j))] * 2, [jax.ShapeDtypeStruct((L, F), BF16)] * 2
    dw_specs = [pl.BlockSpec((K * SUBLANES, tc), lambda j, i: (0, j))] * 2
    dw_shapes = [jax.ShapeDtypeStruct((K * SUBLANES, F), F32)] * 2
    db_specs, db_shapes = [pl.BlockSpec((SUBLANES, tc), lambda j, i: (0, j))] * 2, [jax.ShapeDtypeStruct((SUBLANES, F), F32)] * 2
    return _pcall(body, name=name, grid=(nj, nl),
                  in_specs=both((tl, tc), lambda j, i: (i, j)) + both((SUBLANES, tc), _halo_prev(tl, lambda j: j))
                  + both((SUBLANES, tc), _halo_next(tl, L, SUBLANES))
                  + [pl.BlockSpec((tl, tc), lambda j, i: (i, j)), pl.BlockSpec((BF16_ROWS, tc), _halo_next(tl, L, BF16_ROWS))]
                  + both((K, tc), lambda j, i: (0, j)) + both((1, tc), lambda j, i: (0, j)),
                  out_specs=du_specs + dw_specs + db_specs, out_shape=du_shapes + dw_shapes + db_shapes,
                  compiler_params=_cparams(("parallel", "arbitrary")))(U, U, U, U, U, U, dS, dS, w, w, b2, b2)


def _sconv_fwd(Ac, w, name):
    L, C3 = Ac.shape
    C = C3 // 3
    K = w.shape[0]
    tl = _conv_row_tile(L, False)
    tc = LANES

    def body(a_ref, ap_ref, w_ref, y_ref):
        i = pl.program_id(1)
        a = a_ref[...]
        ap = ap_ref[...]
        p = a[:, tc:2 * tc] * a[:, 2 * tc:]
        pp = jnp.where(i == 0, 0.0, ap[:, tc:2 * tc] * ap[:, 2 * tc:])
        y_ref[...] = (a[:, :tc] * _conv_taps(p, pp, w_ref, K)).astype(BF16)

    return _pcall(body, name=name, grid=(C // tc, L // tl),
                  in_specs=[pl.BlockSpec((tl, 3 * tc), lambda j, i: (i, j)),
                            pl.BlockSpec((SUBLANES, 3 * tc), _halo_prev(tl, lambda j: j)),
                            pl.BlockSpec((K, tc), lambda j, i: (0, j))],
                  out_specs=pl.BlockSpec((tl, tc), lambda j, i: (i, j)),
                  out_shape=jax.ShapeDtypeStruct((L, C), BF16),
                  compiler_params=_cparams(("parallel", "parallel")))(Ac, Ac, w)


def _sconv_bwd_dc(Ac, dy, w, name):
    L, C3 = Ac.shape
    C = C3 // 3
    K = w.shape[0]
    tl = _conv_row_tile(L, False)
    tc = LANES

    def body(a_ref, ap_ref, dy_ref, dc_ref, dw_ref):
        i = pl.program_id(1)
        a = a_ref[...]
        ap = ap_ref[...]
        p = a[:, tc:2 * tc] * a[:, 2 * tc:]
        pp = jnp.where(i == 0, 0.0, ap[:, tc:2 * tc] * ap[:, 2 * tc:])
        dc = dy_ref[...] * a[:, :tc]
        dc_ref[...] = dc

        @pl.when(i == 0)
        def _():
            dw_ref[...] = jnp.zeros_like(dw_ref)

        for k in range(K):
            dw_ref[k * SUBLANES:(k + 1) * SUBLANES, :] += _sum8(dc * _shift_down(p, pp, K - 1 - k))

    return _pcall(body, name=name, grid=(C // tc, L // tl),
                  in_specs=[pl.BlockSpec((tl, 3 * tc), lambda j, i: (i, j)),
                            pl.BlockSpec((SUBLANES, 3 * tc), _halo_prev(tl, lambda j: j)),
                            pl.BlockSpec((tl, tc), lambda j, i: (i, j))],
                  out_specs=[pl.BlockSpec((tl, tc), lambda j, i: (i, j)),
                             pl.BlockSpec((K * SUBLANES, tc), lambda j, i: (0, j))],
                  out_shape=[jax.ShapeDtypeStruct((L, C), F32), jax.ShapeDtypeStruct((K * SUBLANES, C), F32)],
                  compiler_params=_cparams(("parallel", "arbitrary")))(Ac, Ac, dy)


def _sconv_bwd_da(Ac, dy, dc, w, name):
    L, C3 = Ac.shape
    C = C3 // 3
    K = w.shape[0]
    tl = _conv_row_tile(L, False)
    tc = LANES
    nl = L // tl

    def body(a_ref, ap_ref, dy_ref, dc_ref, dcn_ref, w_ref, o_ref):
        i = pl.program_id(1)
        a = a_ref[...]
        ap = ap_ref[...]
        gc, h = a[:, tc:2 * tc], a[:, 2 * tc:]
        p = gc * h
        pp = jnp.where(i == 0, 0.0, ap[:, tc:2 * tc] * ap[:, 2 * tc:])
        dgb = dy_ref[...] * _conv_taps(p, pp, w_ref, K)
        cur = dc_ref[...]
        nxt = jnp.where(i == nl - 1, 0.0, dcn_ref[...])
        dp = w_ref[K - 1:K, :] * cur
        for k in range(K - 1):
            dp = dp + w_ref[k:k + 1, :] * _shift_up(cur, nxt, K - 1 - k)
        o_ref[...] = jnp.concatenate([dgb, dp * h, dp * gc], axis=1).astype(BF16)

    return _pcall(body, name=name, grid=(C // tc, nl),
                  in_specs=[pl.BlockSpec((tl, 3 * tc), lambda j, i: (i, j)),
                            pl.BlockSpec((SUBLANES, 3 * tc), _halo_prev(tl, lambda j: j)),
                            pl.BlockSpec((tl, tc), lambda j, i: (i, j)),
                            pl.BlockSpec((tl, tc), lambda j, i: (i, j)),
                            pl.BlockSpec((SUBLANES, tc), lambda j, i: (jnp.minimum((i + 1) * (tl // SUBLANES), L // SUBLANES - 1), j)),
                            pl.BlockSpec((K, tc), lambda j, i: (0, j))],
                  out_specs=pl.BlockSpec((tl, 3 * tc), lambda j, i: (i, j)),
                  out_shape=jax.ShapeDtypeStruct((L, C3), BF16),
                  compiler_params=_cparams(("parallel", "parallel")))(Ac, Ac, dy, dc, dc, w)


def _fox_gate_fwd(Af, bf, name):
    L, W = Af.shape
    tl = _pick(L, (512, 256, 128))

    def body(a_ref, b_ref, f_ref, carry):
        i = pl.program_id(0)

        @pl.when(i == 0)
        def _():
            carry[...] = jnp.zeros_like(carry)

        z = a_ref[...] + b_ref[...]
        logf = jnp.minimum(z, 0.0) - jnp.log(1.0 + jnp.exp(-jnp.abs(z)))
        f = _tri_mm(_tri(tl), logf) + carry[...]
        f_ref[...] = f
        carry[...] = f[tl - 1:tl, :]

    row = pl.BlockSpec((tl, W), lambda i: (i, 0))
    return _pcall(body, name=name, grid=(L // tl,), in_specs=[row, pl.BlockSpec((1, W), lambda i: (0, 0))], out_specs=row,
                  out_shape=jax.ShapeDtypeStruct((L, W), F32), scratch_shapes=[pltpu.VMEM((1, W), F32)],
                  compiler_params=_cparams(("arbitrary",)))(Af, bf)


def _fox_gate_bwd(Af, bf, dF, name):
    L, W = Af.shape
    tl = _pick(L, (512, 256, 128))
    nl = L // tl

    def body(a_ref, b_ref, df_ref, o_ref, db_ref, carry):
        i = pl.program_id(0)

        @pl.when(i == 0)
        def _():
            carry[...] = jnp.zeros_like(carry)
            db_ref[...] = jnp.zeros_like(db_ref)

        z = a_ref[...] + b_ref[...]
        dlogf = _tri_mm(_tri(tl, upper=True), df_ref[...]) + carry[...]
        carry[...] = dlogf[0:1, :]
        dz = dlogf * _sigmoid(-z)
        o_ref[...] = dz
        db_ref[...] += _sum8(dz)

    row = pl.BlockSpec((tl, W), lambda i: (nl - 1 - i, 0))
    return _pcall(body, name=name, grid=(nl,),
                  in_specs=[row, pl.BlockSpec((1, W), lambda i: (0, 0)), row],
                  out_specs=[row, pl.BlockSpec((SUBLANES, W), lambda i: (0, 0))],
                  out_shape=[jax.ShapeDtypeStruct((L, W), F32), jax.ShapeDtypeStruct((SUBLANES, W), F32)],
                  scratch_shapes=[pltpu.VMEM((1, W), F32)],
                  compiler_params=_cparams(("arbitrary",)))(Af, bf, dF)


def _attn_tiles(L):
    t = _pick(L, (512, 256, 128))
    return t, t


def _attn_scores(q, k, fq, fk, diag, scale):
    s = _dot(q, k, NT) * scale + (fq - fk)
    if not diag:
        return s
    row = lax.broadcasted_iota(jnp.int32, s.shape, 0)
    col = lax.broadcasted_iota(jnp.int32, s.shape, 1)
    return jnp.where(col <= row, s, NEG)


def _attn_geometry():
    Dh = FOX_HEAD_DIM
    hpt = LANES // Dh
    return Dh, hpt, FOX_HEADS // hpt


def _head_lanes(shape, Dh, hpt):
    lane = lax.broadcasted_iota(jnp.int32, shape, len(shape) - 1)
    return [(lane >= h * Dh) & (lane < (h + 1) * Dh) for h in range(hpt)]


def _attn_specs(t, L, hpt, ng):
    return dict(
        col=lambda off: pl.BlockSpec((t, LANES), lambda g, i: (i, g + off)),
        full=lambda off: pl.BlockSpec((L, LANES), lambda g, i: (0, g + off)),
        hq=pl.BlockSpec((hpt, t, 1), lambda g, i: (g, i, 0)),
        hk_full=pl.BlockSpec((hpt, 1, L), lambda g, i: (g, 0, 0)),
        hk=pl.BlockSpec((hpt, 1, t), lambda g, i: (g, 0, i)))


def _attn_fwd(qkv, Fq, Fk, name, gather=()):
    L = qkv.shape[0]
    Dh, hpt, ng = _attn_geometry()
    t, _ = _attn_tiles(L)
    scale = Dh ** -0.5
    n = len(gather)
    nsteps = ng * (L // t)

    def body(*refs):
        q_ref, k_ref, v_ref, fq_ref, fk_ref = refs[:5]
        o_ref, lse_ref = refs[5 + n:7 + n]
        qi = pl.program_id(1)
        step = pl.program_id(0) * (L // t) + qi
        if n:
            start, forward, finish = _gather_phases(refs[5:5 + n], refs[7 + n:7 + 2 * n], *refs[7 + 2 * n:])
            pl.when(step == 0)(start)
            pl.when(step == nsteps // 2)(forward)
        sel = _head_lanes((t, LANES), Dh, hpt)
        q2 = q_ref[...]
        qh = [jnp.where(sel[h], q2, 0) for h in range(hpt)]
        fq = [fq_ref[h] for h in range(hpt)]

        def chunk(j, carry, diag):
            rows = pl.ds(pl.multiple_of(j * t, t), t)
            kc, vc = k_ref[rows, :], v_ref[rows, :]
            out = []
            for h in range(hpt):
                m, l, acc = carry[h]
                s = _attn_scores(qh[h], kc, fq[h], fk_ref[h, :, rows], diag, scale)
                m_new = jnp.maximum(m, jnp.max(s, axis=-1, keepdims=True))
                p = jnp.exp(s - m_new)
                a = jnp.exp(m - m_new)
                out.append((m_new, a * l + jnp.sum(p, axis=-1, keepdims=True), a * acc + _dot(p.astype(BF16), vc, NN)))
            return tuple(out)

        init = tuple((jnp.full((t, 1), NEG, F32), jnp.zeros((t, 1), F32), jnp.zeros((t, LANES), F32)) for _ in range(hpt))
        fin = chunk(qi, lax.fori_loop(0, qi, lambda j, c: chunk(j, c, False), init), True)
        o = jnp.zeros((t, LANES), F32)
        for h, (m, l, acc) in enumerate(fin):
            o = jnp.where(sel[h], acc / l, o)
            lse_ref[h] = m + jnp.log(l)
        o_ref[...] = o.astype(BF16)
        if n:
            pl.when(step == nsteps - 1)(finish)

    sp = _attn_specs(t, L, hpt, ng)
    return _pcall(body, name=name, grid=(ng, L // t),
                  in_specs=[sp["col"](0), sp["full"](ng), sp["full"](2 * ng), sp["hq"], sp["hk_full"]] + [HBM_SPEC] * n,
                  out_specs=[sp["col"](0), sp["hq"]] + [HBM_SPEC] * n,
                  out_shape=[jax.ShapeDtypeStruct((L, ng * LANES), BF16), jax.ShapeDtypeStruct((FOX_HEADS, L, 1), F32)]
                  + _gather_shapes(gather),
                  scratch_shapes=_gather_sems(n) if n else [],
                  compiler_params=_cparams(("arbitrary", "arbitrary") if n else ("parallel", "arbitrary")))(
        qkv, qkv, qkv, Fq, Fk, *gather)


def _attn_bwd(qkv, Fq, Fk, lse, do, name, exchange=()):
    L = qkv.shape[0]
    Dh, hpt, ng = _attn_geometry()
    _, tk = _attn_tiles(L)
    tq = _pick(L, (256, 128))
    nkc = L // tk
    scale = Dh ** -0.5

    n = len(exchange)
    nsteps = ng * (L // tq)

    def body(*refs):
        q_ref, k_ref, v_ref, fq_ref, fk_ref, lse_ref, do_ref = refs[:7]
        dq_ref, dk_ref, dv_ref, df_ref = refs[7 + n:11 + n]
        p_s, dp_s = refs[11 + 2 * n:13 + 2 * n]
        qi = pl.program_id(1)
        step = pl.program_id(0) * (L // tq) + qi
        if n:
            start, finish = _exchange_phases(refs[7:7 + n], refs[11 + n:11 + 2 * n], *refs[13 + 2 * n:])
            pl.when(step == 0)(start)

        @pl.when(qi == 0)
        def _():
            dk_ref[...] = jnp.zeros_like(dk_ref)
            dv_ref[...] = jnp.zeros_like(dv_ref)
            df_ref[...] = jnp.zeros_like(df_ref)

        sel = _head_lanes((tq, LANES), Dh, hpt)
        q2, do2 = q_ref[...], do_ref[...]
        jd = (qi * tq) // tk
        off = qi * tq - jd * tk
        dq = jnp.zeros((tq, LANES), F32)
        for h in range(hpt):
            qh, doh = jnp.where(sel[h], q2, 0), jnp.where(sel[h], do2, 0)
            fq, lse = fq_ref[h], lse_ref[h]

            def first(j, acc, diag):
                rows = pl.ds(pl.multiple_of(j * tk, tk), tk)
                s = _dot(qh, k_ref[rows, :], NT) * scale + (fq - fk_ref[h, :, rows])
                if diag:
                    row = lax.broadcasted_iota(jnp.int32, s.shape, 0) + off
                    s = jnp.where(lax.broadcasted_iota(jnp.int32, s.shape, 1) <= row, s, NEG)
                p = jnp.exp(s - lse)
                dp = _dot(doh, v_ref[rows, :], NT)
                p_s[j] = p
                dp_s[j] = dp
                return acc + jnp.sum(p * dp, axis=-1, keepdims=True)

            delta = first(jd, lax.fori_loop(0, jd, lambda j, c: first(j, c, False), jnp.zeros((tq, 1), F32)), True)

            def second(j, acc):
                rows = pl.ds(pl.multiple_of(j * tk, tk), tk)
                p = p_s[j]
                ds = p * (dp_s[j] - delta)
                dsb = ds.astype(BF16)
                dk_ref[rows, :] += _dot(dsb, qh, TN)
                dv_ref[rows, :] += _dot(p.astype(BF16), doh, TN)
                df_ref[h, :, rows] -= jnp.sum(ds, axis=0, keepdims=True)
                return acc + _dot(dsb, k_ref[rows, :], NN)

            dq = jnp.where(sel[h], lax.fori_loop(0, jd + 1, second, jnp.zeros((tq, LANES), F32)), dq)
        dq_ref[...] = (dq * scale).astype(BF16)

        @pl.when(qi == L // tq - 1)
        def _():
            dk_ref[...] *= scale

        if n:
            pl.when(step == nsteps - 1)(finish)

    sp = _attn_specs(tq, L, hpt, ng)
    return _pcall(body, name=name, grid=(ng, L // tq),
                  in_specs=[sp["col"](0), sp["full"](ng), sp["full"](2 * ng), sp["hq"], sp["hk_full"], sp["hq"], sp["col"](0)]
                  + [HBM_SPEC] * n,
                  out_specs=[sp["col"](0), sp["full"](0), sp["full"](0), sp["hk_full"]] + [HBM_SPEC] * n,
                  out_shape=[jax.ShapeDtypeStruct((L, ng * LANES), BF16), jax.ShapeDtypeStruct((L, ng * LANES), F32),
                             jax.ShapeDtypeStruct((L, ng * LANES), F32), jax.ShapeDtypeStruct((FOX_HEADS, 1, L), F32)]
                  + [jax.ShapeDtypeStruct(g.shape, g.dtype) for g in exchange],
                  scratch_shapes=[pltpu.VMEM((nkc, tq, tk), F32), pltpu.VMEM((nkc, tq, tk), F32)] + (_gather_sems(n) if n else []),
                  compiler_params=pltpu.CompilerParams(
                      vmem_limit_bytes=ATTN_BWD_VMEM_LIMIT,
                      dimension_semantics=("arbitrary", "arbitrary") if n else ("parallel", "arbitrary")))(
        qkv, qkv, qkv, Fq, Fk, lse, do, *exchange)


def _mconv_fwd(xr, w, b, name):
    L, C = xr.shape
    K = w.shape[0]
    tl = _conv_row_tile(L, False)
    tc = _pick(C, (512, 384, 256, 128))

    R = STRIP_ROWS

    def body(x_ref, xp_ref, w_ref, b_ref, o_ref):
        i = pl.program_id(1)
        for lanes in (slice(c, c + LANES) for c in range(0, tc, LANES)):
            wts = [w_ref[k:k + 1, lanes] for k in range(K)]
            bias = b_ref[:, lanes]

            def step(s, xprev8):
                rows = _strip(s, R)
                x = x_ref[rows, lanes]
                pre = sum(wts[k] * _shift_down(x, xprev8, K - 1 - k) for k in range(K)) + bias
                o_ref[rows, lanes] = pre * _sigmoid(pre)
                return x[R - SUBLANES:]

            lax.fori_loop(1, tl // R, step, step(0, jnp.where(i == 0, 0.0, xp_ref[:, lanes])))

    return _pcall(body, name=name, grid=(C // tc, L // tl),
                  in_specs=[pl.BlockSpec((tl, tc), lambda j, i: (i, j)),
                            pl.BlockSpec((SUBLANES, tc), _halo_prev(tl, lambda j: j)),
                            pl.BlockSpec((K, tc), lambda j, i: (0, j)),
                            pl.BlockSpec((1, tc), lambda j, i: (0, j))],
                  out_specs=pl.BlockSpec((tl, tc), lambda j, i: (i, j)),
                  out_shape=jax.ShapeDtypeStruct((L, C), F32),
                  compiler_params=_cparams(("parallel", "parallel")))(xr, xr, w, b.reshape(1, C))


def _mconv_bwd(xr, dact, w, b, name):
    L, C = xr.shape
    K = w.shape[0]
    tl = _conv_row_tile(L, True)
    tc = _pick(C, (512, 384, 256, 128))
    nl = L // tl

    R = STRIP_ROWS
    ns = tl // R

    def body(x_ref, xp_ref, xn_ref, d_ref, dn_ref, w_ref, b_ref, o_ref, dw_ref, db_ref):
        i = pl.program_id(1)

        @pl.when(i == 0)
        def _():
            dw_ref[...] = jnp.zeros_like(dw_ref)
            db_ref[...] = jnp.zeros_like(db_ref)

        for lanes in (slice(c, c + LANES) for c in range(0, tc, LANES)):
            wts = [w_ref[k:k + 1, lanes] for k in range(K)]
            bias = b_ref[:, lanes]

            def at_conv_out(x, xprev8, d):
                sh = [_shift_down(x, xprev8, K - 1 - k) for k in range(K)]
                pre = sum(wts[k] * sh[k] for k in range(K)) + bias
                sg = _sigmoid(pre)
                return d * sg * (1.0 + pre * (1.0 - sg)), sh

            def emit(rows, dpre, dpre_next8):
                o_ref[rows, lanes] = sum(wts[k] * _shift_up(dpre, dpre_next8, K - 1 - k) for k in range(K)).astype(BF16)

            def accumulate(acc, dpre, sh):
                return tuple(acc[k] + _sum8(dpre * sh[k]) for k in range(K)) + (acc[K] + _sum8(dpre),)

            x0 = x_ref[_strip(0, R), lanes]
            dpre0, sh0 = at_conv_out(x0, jnp.where(i == 0, 0.0, xp_ref[:, lanes]), d_ref[_strip(0, R), lanes])
            acc0 = accumulate((jnp.zeros((SUBLANES, LANES), F32),) * (K + 1), dpre0, sh0)

            def step(s, carry):
                xprev, dprev, acc = carry
                x = x_ref[_strip(s, R), lanes]
                dpre, sh = at_conv_out(x, xprev[R - SUBLANES:], d_ref[_strip(s, R), lanes])
                emit(_strip(s - 1, R), dprev, dpre[:SUBLANES])
                return x, dpre, accumulate(acc, dpre, sh)

            xl, dl, acc = lax.fori_loop(1, ns, step, (x0, dpre0, acc0))
            dn, _ = at_conv_out(xn_ref[:, lanes], xl[R - SUBLANES:], dn_ref[:, lanes])
            emit(_strip(ns - 1, R), dl, jnp.where(i == nl - 1, 0.0, dn))
            db_ref[:, lanes] += acc[K]
            for k in range(K):
                dw_ref[k * SUBLANES:(k + 1) * SUBLANES, lanes] += acc[k]

    return _pcall(body, name=name, grid=(C // tc, nl),
                  in_specs=[pl.BlockSpec((tl, tc), lambda j, i: (i, j)),
                            pl.BlockSpec((SUBLANES, tc), _halo_prev(tl, lambda j: j)),
                            pl.BlockSpec((SUBLANES, tc), _halo_next(tl, L, SUBLANES)),
                            pl.BlockSpec((tl, tc), lambda j, i: (i, j)),
                            pl.BlockSpec((SUBLANES, tc), _halo_next(tl, L, SUBLANES)),
                            pl.BlockSpec((K, tc), lambda j, i: (0, j)),
                            pl.BlockSpec((1, tc), lambda j, i: (0, j))],
                  out_specs=[pl.BlockSpec((tl, tc), lambda j, i: (i, j)),
                             pl.BlockSpec((K * SUBLANES, tc), lambda j, i: (0, j)),
                             pl.BlockSpec((SUBLANES, tc), lambda j, i: (0, j))],
                  out_shape=[jax.ShapeDtypeStruct((L, C), BF16), jax.ShapeDtypeStruct((K * SUBLANES, C), F32),
                             jax.ShapeDtypeStruct((SUBLANES, C), F32)],
                  compiler_params=_cparams(("parallel", "arbitrary")))(xr, xr, xr, dact, dact, w, b.reshape(1, C))


def _head_selector(R, P, heads_first):
    shape = (R, R * P) if heads_first else (R * P, R)
    head = lax.broadcasted_iota(jnp.int32, shape, 0 if heads_first else 1)
    lane = lax.broadcasted_iota(jnp.int32, shape, 1 if heads_first else 0)
    d = lane - head * P
    return jnp.where((d >= 0) & (d < P), 1.0, 0.0).astype(BF16)


def _ssd_prelude(dtc_ref, dtr_ref, bc_ref, br_ref, ac_ref, ar_ref, Q, R, P):
    raw_c = dtc_ref[...] + bc_ref[...]
    dt_c = _softplus(raw_c)
    dt_r = _softplus(dtr_ref[...] + br_ref[...])
    A_c = -jnp.exp(ac_ref[...])
    acs_c = _tri_mm(_tri(Q), dt_c * A_c)
    acs_r = _tri_mm(_tri(Q, upper=True), dt_r * (-jnp.exp(ar_ref[...])), tri_first=False)
    ea_c = jnp.exp(acs_c)
    dte_c = jnp.exp(acs_c[Q - 1:Q, :] - acs_c)
    wide = _tri_mm(_head_selector(R, P, True), jnp.concatenate([dt_c, ea_c, dte_c], axis=0), tri_first=False)
    return dict(raw_c=raw_c, dt_c=dt_c, A_c=A_c, acs_c=acs_c, acs_r=acs_r, ea_c=ea_c,
                DT=wide[:Q], EA=wide[Q:2 * Q], DTE=wide[2 * Q:])


def _ssd_decay_tile(pre, r, mask):
    return jnp.exp(jnp.where(mask, pre["acs_c"][:, r:r + 1] - pre["acs_r"][r:r + 1, :], NEG))


def _ssd_specs(Q, R, P, N, G, inner, rev=None):
    cc = (lambda c: c) if rev is None else rev
    return dict(
        x=pl.BlockSpec((Q, R * P), lambda g, c: (cc(c), g)),
        b=pl.BlockSpec((Q, N), lambda g, c: (cc(c), inner // N + g)),
        c=pl.BlockSpec((Q, N), lambda g, c: (cc(c), inner // N + G + g)),
        dtc=pl.BlockSpec((None, Q, R), lambda g, c: (g, cc(c), 0)),
        dtr=pl.BlockSpec((None, R, Q), lambda g, c: (g, 0, cc(c))),
        pc=pl.BlockSpec((None, 1, R), lambda g, c: (g, 0, 0)),
        pr=pl.BlockSpec((None, R, 1), lambda g, c: (g, 0, 0)),
        px=pl.BlockSpec((None, 1, R * P), lambda g, c: (g, 0, 0)),
        st=pl.BlockSpec((None, None, N, R * P), lambda g, c: (cc(c), g, 0, 0)))


def _ssd_fwd(act, dtc, dtr, bias_c, bias_r, alog_c, alog_r, dsk_x, name):
    G, L, R = dtc.shape
    N, P, Q = SSM_STATE, SSM_HEAD_DIM, SSM_CHUNK
    RP = R * P
    inner = G * RP
    nc = L // Q

    def body(x_ref, b_ref, c_ref, dtc_ref, dtr_ref, bc_ref, br_ref, ac_ref, ar_ref, dk_ref, y_ref, hp_ref, st):
        c = pl.program_id(1)

        @pl.when(c == 0)
        def _():
            st[...] = jnp.zeros_like(st)

        pre = _ssd_prelude(dtc_ref, dtr_ref, bc_ref, br_ref, ac_ref, ar_ref, Q, R, P)
        X = x_ref[...]
        XT = X * pre["DT"]
        Bb = b_ref[...].astype(BF16)
        Cb = c_ref[...].astype(BF16)
        CB = _dot(Cb, Bb, NT)
        mask = lax.broadcasted_iota(jnp.int32, (Q, Q), 0) >= lax.broadcasted_iota(jnp.int32, (Q, Q), 1)
        low = lax.broadcasted_iota(jnp.int32, (Q, 2 * P), 1) < P
        pieces = []
        for k in range(R // 2):
            xt2 = XT[:, 2 * P * k:2 * P * (k + 1)]
            acc = None
            for half in range(2):
                Gm = CB * _ssd_decay_tile(pre, 2 * k + half, mask)
                part = _dot(Gm.astype(BF16), jnp.where(low == (half == 0), xt2, 0.0).astype(BF16), NN)
                acc = part if acc is None else acc + part
            pieces.append(acc)
        HP = st[...]
        hp_ref[...] = HP
        yoff = pre["EA"] * _dot(Cb, HP.astype(BF16), NN)
        st[...] = HP * pre["EA"][Q - 1:Q, :] + _dot(Bb, (XT * pre["DTE"]).astype(BF16), TN)
        y_ref[...] = jnp.concatenate(pieces, axis=1) + yoff + dk_ref[...] * X

    sp = _ssd_specs(Q, R, P, N, G, inner)
    return _pcall(body, name=name, grid=(G, nc),
                  in_specs=[sp["x"], sp["b"], sp["c"], sp["dtc"], sp["dtr"], sp["pc"], sp["pr"], sp["pc"], sp["pr"], sp["px"]],
                  out_specs=[sp["x"], sp["st"]],
                  out_shape=[jax.ShapeDtypeStruct((L, inner), F32), jax.ShapeDtypeStruct((nc, G, N, RP), F32)],
                  scratch_shapes=[pltpu.VMEM((N, RP), F32)],
                  compiler_params=_cparams(("parallel", "arbitrary")))(act, act, act, dtc, dtr, bias_c, bias_r, alog_c, alog_r, dsk_x)


def _ssd_bwd(act, dtc, dtr, bias_c, bias_r, alog_c, alog_r, dsk_x, hprev, dy, name):
    G, L, R = dtc.shape
    N, P, Q = SSM_STATE, SSM_HEAD_DIM, SSM_CHUNK
    RP = R * P
    inner = G * RP
    nc = L // Q

    def body(x_ref, b_ref, c_ref, dtc_ref, dtr_ref, bc_ref, br_ref, ac_ref, ar_ref, dk_ref, hp_ref, dy_ref,
             dx_ref, db_ref, dc_ref, ddt_ref, gbias_ref, galog_ref, gdsk_ref, dst):
        c = pl.program_id(1)

        @pl.when(c == 0)
        def _():
            dst[...] = jnp.zeros_like(dst)
            gbias_ref[...] = jnp.zeros_like(gbias_ref)
            galog_ref[...] = jnp.zeros_like(galog_ref)
            gdsk_ref[...] = jnp.zeros_like(gdsk_ref)

        pre = _ssd_prelude(dtc_ref, dtr_ref, bc_ref, br_ref, ac_ref, ar_ref, Q, R, P)
        DT, EA, DTE = pre["DT"], pre["EA"], pre["DTE"]
        E_END = EA[Q - 1:Q, :]
        X, DY = x_ref[...], dy_ref[...]
        XT = X * DT
        Bb = b_ref[...].astype(BF16)
        Cb = c_ref[...].astype(BF16)
        CB = _dot(Cb, Bb, NT)
        HP, dH = hp_ref[...], dst[...]
        HPb, dHb = HP.astype(BF16), dH.astype(BF16)
        EDY = EA * DY
        EDYb = EDY.astype(BF16)
        dC = _dot(EDYb, HPb, NT)
        dHP = _dot(Cb, EDYb, TN)
        da_off = EDY * _dot(Cb, HPb, NN)
        Z = _dot(Bb, dHb, NN)
        XD = XT * DTE
        dB = _dot(XD.astype(BF16), dHb, NT)
        dXT = DTE * Z
        t_x = XD * Z
        hh = jnp.sum(dH * HP, axis=0, keepdims=True) * E_END
        dst[...] = dHP + dH * E_END
        mask = lax.broadcasted_iota(jnp.int32, (Q, Q), 0) >= lax.broadcasted_iota(jnp.int32, (Q, Q), 1)
        eye = lax.broadcasted_iota(jnp.int32, (Q, Q), 0) == lax.broadcasted_iota(jnp.int32, (Q, Q), 1)
        low = lax.broadcasted_iota(jnp.int32, (Q, 2 * P), 1) < P
        lane = lax.broadcasted_iota(jnp.int32, (Q, R), 1)
        dCB = jnp.zeros((Q, Q), F32)
        da_mat = jnp.zeros((Q, R), F32)
        pieces = []
        for k in range(R // 2):
            sl = slice(2 * P * k, 2 * P * (k + 1))
            xt2, dy2 = XT[:, sl], DY[:, sl]
            acc = None
            for half in range(2):
                r = 2 * k + half
                sel = low == (half == 0)
                Lm = _ssd_decay_tile(pre, r, mask)
                Gm = CB * Lm
                dyb = jnp.where(sel, dy2, 0.0).astype(BF16)
                part = _dot(Gm.astype(BF16), dyb, TN)
                acc = part if acc is None else acc + part
                dG = jnp.where(mask, _dot(dyb, jnp.where(sel, xt2, 0.0).astype(BF16), NT), 0.0)
                Mm = dG * Gm
                dCB = dCB + dG * Lm
                colsum = jnp.sum(jnp.where(eye, jnp.sum(Mm, axis=0, keepdims=True), 0.0), axis=1, keepdims=True)
                da_mat = jnp.where(lane == r, jnp.sum(Mm, axis=1, keepdims=True) - colsum, da_mat)
            pieces.append(acc)
        dXT = dXT + jnp.concatenate(pieces, axis=1)
        dCBb = dCB.astype(BF16)
        dc_ref[...] = dC + _dot(dCBb, Bb, NN)
        db_ref[...] = dB + _dot(dCBb, Cb, TN)
        dx_ref[...] = dXT * DT + dk_ref[...] * DY
        pad = jnp.zeros((SUBLANES - 1, RP), F32)
        sums = _tri_mm(_head_selector(R, P, False), jnp.concatenate([da_off, t_x, dXT * X, DY * X, hh, pad], axis=0), tri_first=False)
        t = sums[Q:2 * Q]
        da_end = jnp.sum(t, axis=0, keepdims=True) + sums[4 * Q:4 * Q + 1]
        rowi = lax.broadcasted_iota(jnp.int32, (Q, R), 0)
        da_mat = da_mat + sums[:Q] - t + jnp.where(rowi == Q - 1, da_end, 0.0)
        ddtA = _tri_mm(_tri(Q, upper=True), da_mat)
        ddt_raw = (ddtA * pre["A_c"] + sums[2 * Q:3 * Q]) * _sigmoid(pre["raw_c"])
        ddt_ref[...] = ddt_raw
        gbias_ref[...] += jnp.sum(ddt_raw, axis=0, keepdims=True)
        galog_ref[...] += jnp.sum(ddtA * pre["dt_c"], axis=0, keepdims=True) * pre["A_c"]
        gdsk_ref[...] += jnp.sum(sums[3 * Q:4 * Q], axis=0, keepdims=True)

    sp = _ssd_specs(Q, R, P, N, G, inner, rev=lambda c: nc - 1 - c)
    bout = pl.BlockSpec((Q, N), lambda g, c: (nc - 1 - c, g))
    return _pcall(body, name=name, grid=(G, nc),
                  in_specs=[sp["x"], sp["b"], sp["c"], sp["dtc"], sp["dtr"], sp["pc"], sp["pr"], sp["pc"], sp["pr"], sp["px"],
                            sp["st"], sp["x"]],
                  out_specs=[sp["x"], bout, bout, sp["dtc"], sp["pc"], sp["pc"], sp["pc"]],
                  out_shape=[jax.ShapeDtypeStruct((L, inner), F32), jax.ShapeDtypeStruct((L, G * N), F32),
                             jax.ShapeDtypeStruct((L, G * N), F32), jax.ShapeDtypeStruct((G, L, R), F32),
                             jax.ShapeDtypeStruct((G, 1, R), F32), jax.ShapeDtypeStruct((G, 1, R), F32),
                             jax.ShapeDtypeStruct((G, 1, R), F32)],
                  scratch_shapes=[pltpu.VMEM((N, RP), F32)],
                  compiler_params=_cparams(("parallel", "arbitrary")))(
        act, act, act, dtc, dtr, bias_c, bias_r, alog_c, alog_r, dsk_x, hprev, dy)


def _gnorm_fwd(y, z, g, name):
    L, Dn = y.shape
    gs = Dn // SSM_GROUPS
    tl = _row_tile(L)

    def body(y_ref, z_ref, g_ref, o_ref):
        for k in range(SSM_GROUPS):
            sl = slice(k * gs, (k + 1) * gs)
            zz = z_ref[:, sl]
            u = y_ref[:, sl] * zz * _sigmoid(zz)
            rstd = lax.rsqrt(jnp.mean(u * u, axis=-1, keepdims=True) + RMS_EPS)
            o_ref[:, sl] = (u * rstd * g_ref[:, sl]).astype(BF16)

    row = pl.BlockSpec((tl, Dn), lambda i: (i, 0))
    return _pcall(body, name=name, grid=(L // tl,), in_specs=[row, row, pl.BlockSpec((1, Dn), lambda i: (0, 0))],
                  out_specs=row, out_shape=jax.ShapeDtypeStruct((L, Dn), BF16),
                  compiler_params=_cparams(("parallel",)))(y, z, g.reshape(1, Dn))


def _gnorm_bwd(y, z, g, dout, name):
    L, Dn = y.shape
    gs = Dn // SSM_GROUPS
    tl = _row_tile(L)

    def body(y_ref, z_ref, g_ref, d_ref, dy_ref, dz_ref, dg_ref):
        i = pl.program_id(0)

        @pl.when(i == 0)
        def _():
            dg_ref[...] = jnp.zeros_like(dg_ref)

        for k in range(SSM_GROUPS):
            sl = slice(k * gs, (k + 1) * gs)
            zz = z_ref[:, sl]
            yy = y_ref[:, sl]
            sg = _sigmoid(zz)
            sil = zz * sg
            u = yy * sil
            rstd = lax.rsqrt(jnp.mean(u * u, axis=-1, keepdims=True) + RMS_EPS)
            n = u * rstd
            d = d_ref[:, sl]
            dn = d * g_ref[:, sl]
            du = rstd * (dn - n * jnp.mean(dn * n, axis=-1, keepdims=True))
            dy_ref[:, sl] = du * sil
            dz_ref[:, sl] = (du * yy * sg * (1.0 + zz * (1.0 - sg))).astype(BF16)
            dg_ref[:, sl] += _sum8(d * n)

    row = pl.BlockSpec((tl, Dn), lambda i: (i, 0))
    return _pcall(body, name=name, grid=(L // tl,), in_specs=[row, row, pl.BlockSpec((1, Dn), lambda i: (0, 0)), row],
                  out_specs=[row, row, pl.BlockSpec((SUBLANES, Dn), lambda i: (0, 0))],
                  out_shape=[jax.ShapeDtypeStruct((L, Dn), F32), jax.ShapeDtypeStruct((L, Dn), BF16),
                             jax.ShapeDtypeStruct((SUBLANES, Dn), F32)],
                  compiler_params=_cparams(("arbitrary",)))(y, z, g.reshape(1, Dn), dout)


def _adamw(w, g, m, v, name):
    rows, W = w.shape
    tr = _pick(rows, (512, 256, 128, 64, 32, 16, 8))
    c1 = 1.0 / (1.0 - ADAM_B1 ** ADAM_STEP)
    c2 = 1.0 / (1.0 - ADAM_B2 ** ADAM_STEP)

    def body(w_ref, g_ref, m_ref, v_ref, d_ref, nm_ref, nv_ref):
        g_ = g_ref[...]
        nm = ADAM_B1 * m_ref[...] + (1.0 - ADAM_B1) * g_
        nv = ADAM_B2 * v_ref[...] + (1.0 - ADAM_B2) * (g_ * g_)
        nm_ref[...] = nm
        nv_ref[...] = nv
        d_ref[...] = -ADAM_LR * ((nm * c1) / (jnp.sqrt(nv * c2) + ADAM_EPS) + ADAM_WD * w_ref[...])

    blk = pl.BlockSpec((tr, W), lambda i: (i, 0))
    return _pcall(body, name=name, grid=(rows // tr,), in_specs=[blk] * 4, out_specs=[blk] * 3,
                  out_shape=[jax.ShapeDtypeStruct((rows, W), F32)] * 3, compiler_params=_cparams(("parallel",)))(w, g, m, v)


def _sum_slots(x, name, extra=None):
    n, rows, W = x.shape
    tr = _pick(rows, (512, 256, 128, 64, 32, 16, 8))
    has_extra = extra is not None

    def body(*refs):
        if has_extra:
            e_ref, x_ref, o_ref = refs
            acc = e_ref[...].astype(F32)
            start = 0
        else:
            x_ref, o_ref = refs
            acc = x_ref[0].astype(F32)
            start = 1
        for s in range(start, n):
            acc = acc + x_ref[s].astype(F32)
        o_ref[...] = acc

    blk = pl.BlockSpec((tr, W), lambda i: (i, 0))
    xblk = pl.BlockSpec((n, tr, W), lambda i: (0, i, 0))
    return _pcall(body, name=name, grid=(rows // tr,), in_specs=([blk] if has_extra else []) + [xblk], out_specs=blk,
                  out_shape=jax.ShapeDtypeStruct((rows, W), F32), compiler_params=_cparams(("parallel",)))(
        *(([extra] if has_extra else []) + [x]))


def _add_pairs(a, b, name):
    n, rows, W = a.shape
    tr = _pick(rows, (512, 256, 128, 64, 32, 16, 8))

    def body(a_ref, b_ref, o_ref):
        o_ref[...] = (a_ref[...].astype(F32) + b_ref[...].astype(F32)).astype(BF16)

    blk = pl.BlockSpec((None, tr, W), lambda s, i: (s, i, 0))
    return _pcall(body, name=name, grid=(n, rows // tr), in_specs=[blk, blk], out_specs=blk,
                  out_shape=jax.ShapeDtypeStruct((n, rows, W), BF16), compiler_params=_cparams(("parallel", "parallel")))(a, b)


MESH = pl.DeviceIdType.MESH
HBM_SPEC = pl.BlockSpec(memory_space=pl.ANY)


def _me():
    return lax.axis_index("x"), lax.axis_index("y"), lax.axis_index("c")


def _all_gather(arrs, name):
    n = len(arrs)

    def body(*refs):
        start, forward, finish = _gather_phases(refs[:n], refs[n:2 * n], *refs[2 * n:])
        start()
        forward()
        finish()

    return _pcall(body, name=name, in_specs=[HBM_SPEC] * n, out_specs=[HBM_SPEC] * n,
                  out_shape=_gather_shapes(arrs), scratch_shapes=_gather_sems(n))(*arrs)


def _gather_shapes(arrs):
    return [jax.ShapeDtypeStruct((N_DEV,) + a.shape, a.dtype) for a in arrs]


def _gather_sems(n):
    return [pltpu.SemaphoreType.DMA((7 * n,)), pltpu.SemaphoreType.DMA((7 * n,)), pltpu.SemaphoreType.DMA((n,))]


def _gather_phases(ins, outs, send_sems, recv_sems, local_sems):
    n = len(ins)
    x, y, c = _me()
    me, sib = (x, y, c), (x, y, 1 - c)
    chips = [(1 - x, y), (x, 1 - y), (1 - x, 1 - y)]

    def slot(a, dev):
        return outs[a].at[4 * dev[0] + 2 * dev[1] + dev[2]]

    def copy(a, k, block, to, src=None):
        return pltpu.make_async_remote_copy(src_ref=slot(a, block) if src is None else src, dst_ref=slot(a, block),
                                            send_sem=send_sems.at[a * 7 + k], recv_sem=recv_sems.at[a * 7 + k],
                                            device_id=to, device_id_type=MESH)

    def mine():
        return [pltpu.make_async_copy(ins[a], slot(a, me), local_sems.at[a]) for a in range(n)]

    def first():
        out = []
        for a in range(n):
            out.append(copy(a, 0, me, sib, src=ins[a]))
            out += [copy(a, 1 + j, me, (*chip, c), src=ins[a]) for j, chip in enumerate(chips)]
        return out

    def passed():
        return [copy(a, 4 + j, (*chip, c), sib) for j, chip in enumerate(chips) for a in range(n)]

    def start():
        for cp in mine() + first():
            cp.start()

    def forward():
        fws = passed()
        for j, chip in enumerate(chips):
            for a in range(n):
                copy(a, 1 + j, (*chip, c), me).wait_recv()
                fws[j * n + a].start()

    def finish():
        for a in range(n):
            copy(a, 0, sib, me).wait_recv()
            for j, chip in enumerate(chips):
                copy(a, 4 + j, (*chip, 1 - c), me).wait_recv()
        for cp in first() + passed():
            cp.wait_send()
        for cp in mine():
            cp.wait()

    return start, forward, finish


def _exchange_phases(gs, outs, send_sems, recv_sems, local_sems):
    n = len(gs)
    x, y, c = _me()
    my_slot = 4 * x + 2 * y + c
    flips = [(fx, fy, fc) for fx in (0, 1) for fy in (0, 1) for fc in (0, 1) if fx or fy or fc]

    def peer(f):
        return tuple(1 - v if flip else v for v, flip in zip((x, y, c), f))

    def copies():
        out = []
        for a in range(n):
            for k, f in enumerate(flips):
                px, py, pc = peer(f)
                out.append(pltpu.make_async_remote_copy(
                    src_ref=gs[a].at[4 * px + 2 * py + pc], dst_ref=outs[a].at[my_slot],
                    send_sem=send_sems.at[a * 7 + k], recv_sem=recv_sems.at[a * 7 + k],
                    device_id=(px, py, pc), device_id_type=MESH))
        return out

    def arrivals():
        out = []
        for a in range(n):
            for k, f in enumerate(flips):
                px, py, pc = peer(f)
                slot = outs[a].at[4 * px + 2 * py + pc]
                out.append(pltpu.make_async_remote_copy(src_ref=slot, dst_ref=slot, send_sem=send_sems.at[a * 7 + k],
                                                        recv_sem=recv_sems.at[a * 7 + k], device_id=(px, py, pc),
                                                        device_id_type=MESH))
        return out

    def mine():
        return [pltpu.make_async_copy(gs[a].at[my_slot], outs[a].at[my_slot], local_sems.at[a]) for a in range(n)]

    def start():
        for cp in mine() + copies():
            cp.start()

    def finish():
        for cp in arrivals():
            cp.wait_recv()
        for cp in copies():
            cp.wait_send()
        for cp in mine():
            cp.wait()

    return start, finish


def _rs_sibling(gs, name):
    n = len(gs)

    def body(*refs):
        g_refs, o_refs = refs[:n], refs[n:2 * n]
        send_sems, recv_sems = refs[2 * n:]
        x, y, c = _me()
        sib = (x, y, 1 - c)
        cps = [pltpu.make_async_remote_copy(src_ref=g_refs[a].at[2 * q + (1 - c)], dst_ref=o_refs[a].at[q],
                                            send_sem=send_sems.at[4 * a + q], recv_sem=recv_sems.at[4 * a + q],
                                            device_id=sib, device_id_type=MESH) for a in range(n) for q in range(4)]
        for cp in cps:
            cp.start()
        for cp in cps:
            cp.wait()

    return _pcall(body, name=name, in_specs=[HBM_SPEC] * n, out_specs=[HBM_SPEC] * n,
                  out_shape=[jax.ShapeDtypeStruct((4,) + g.shape[1:], g.dtype) for g in gs],
                  scratch_shapes=[pltpu.SemaphoreType.DMA((4 * n,)), pltpu.SemaphoreType.DMA((4 * n,))])(*gs)


def _rs_chips(ps, name):
    n = len(ps)

    def body(*refs):
        p_refs, o_refs = refs[:n], refs[n:2 * n]
        send_sems, recv_sems = refs[2 * n:]
        x, y, c = _me()
        chips = [(1 - x, y), (x, 1 - y), (1 - x, 1 - y)]
        cps = [pltpu.make_async_remote_copy(src_ref=p_refs[a].at[2 * chip[0] + chip[1]], dst_ref=o_refs[a].at[j],
                                            send_sem=send_sems.at[3 * a + j], recv_sem=recv_sems.at[3 * a + j],
                                            device_id=(*chip, c), device_id_type=MESH)
               for j, chip in enumerate(chips) for a in range(n)]
        for cp in cps:
            cp.start()
        for cp in cps:
            cp.wait()

    return _pcall(body, name=name, in_specs=[HBM_SPEC] * n, out_specs=[HBM_SPEC] * n,
                  out_shape=[jax.ShapeDtypeStruct((3,) + p.shape[1:], p.dtype) for p in ps],
                  scratch_shapes=[pltpu.SemaphoreType.DMA((3 * n,)), pltpu.SemaphoreType.DMA((3 * n,))])(*ps)


def _reduce_scatter(gs, name):
    x, y, c = _me()
    from_sib = _rs_sibling(gs, name + "_sib")
    pairs = []
    for a, (g, fs) in enumerate(zip(gs, from_sib)):
        own = g.reshape((4, 2) + g.shape[1:])
        pairs.append(_add_pairs(jnp.where(c == 0, own[:, 0], own[:, 1]), fs, f"{name}_pair{a}"))
    from_chips = _rs_chips(pairs, name + "_chips")
    return [_sum_slots(fc, f"{name}_sum{a}", extra=lax.dynamic_index_in_dim(p, 2 * x + y, axis=0, keepdims=False))
            for a, (p, fc) in enumerate(zip(pairs, from_chips))]


BIG = ("even_w_in", "even_w_out", "odd_w_in", "odd_w_out", "ffn_w_up", "ffn_w_down", "ple_w_proj", "ple_w_gate")
SMALL_SHARDED = ("even_conv_w", "odd_conv_w", "odd_conv_b", "odd_norm_g", "ffn_conv_w")
REPLICATED = ("even_b_f", "odd_dt_bias", "odd_a_log", "odd_d_skip", "ln_mix_g", "ln_mix_b", "ffn_conv_b",
              "ln_ffn_g", "ln_ffn_b", "ple_b_gate")
WEIGHTS = ("even_w_in", "even_b_f", "even_conv_w", "even_w_out", "odd_w_in", "odd_conv_w", "odd_conv_b", "odd_dt_bias",
           "odd_a_log", "odd_d_skip", "odd_norm_g", "odd_w_out", "ln_mix_g", "ln_mix_b", "ffn_w_up", "ffn_conv_w",
           "ffn_conv_b", "ffn_w_down", "ln_ffn_g", "ln_ffn_b", "ple_w_proj", "ple_w_gate", "ple_b_gate")


def _full_shapes():
    d = _dims()
    return {
        "even_w_in": ((1, D_MODEL, d["even_in"]), 2), "even_b_f": ((1, FOX_HEADS), None),
        "even_conv_w": ((1, CONV_WIDTH, CONV_DIM), 2), "even_w_out": ((1, d["even_mix"], D_MODEL), 1),
        "odd_w_in": ((1, D_MODEL, d["odd_in"]), 2), "odd_conv_w": ((1, SSM_CONV_WIDTH, d["conv_ch"]), 2),
        "odd_conv_b": ((1, d["conv_ch"]), 1), "odd_dt_bias": ((1, d["ssm_heads"]), None),
        "odd_a_log": ((1, d["ssm_heads"]), None), "odd_d_skip": ((1, d["ssm_heads"]), None),
        "odd_norm_g": ((1, d["ssm_inner"]), 1), "odd_w_out": ((1, d["ssm_inner"], D_MODEL), 1),
        "ln_mix_g": ((DEPTH, D_MODEL), None), "ln_mix_b": ((DEPTH, D_MODEL), None),
        "ffn_w_up": ((DEPTH, D_MODEL, 2 * D_FF), 2), "ffn_conv_w": ((DEPTH, FFN_CONV_WIDTH, 2 * D_FF), 2),
        "ffn_conv_b": ((DEPTH, 2 * D_FF), None), "ffn_w_down": ((DEPTH, D_FF, D_MODEL), 1),
        "ln_ffn_g": ((DEPTH, D_MODEL), None), "ln_ffn_b": ((DEPTH, D_MODEL), None),
        "ple_w_proj": ((DEPTH, PLE_DIM, D_MODEL), 2), "ple_w_gate": ((DEPTH, D_MODEL, D_MODEL), 1),
        "ple_b_gate": ((DEPTH, D_MODEL), None),
    }


def _shard_shape(name):
    shape, ax = _full_shapes()[name]
    if ax is None:
        return shape
    return tuple(s // N_DEV if i == ax else s for i, s in enumerate(shape))


def _as2d(a, lead=0):
    return a.reshape(a.shape[:lead] + (-1, a.shape[-1]))


def _part_rows(shape):
    n = int(np.prod(shape))
    return -(-(-(-n // PACK_W)) // SUBLANES) * SUBLANES


def _pack_small(parts):
    out = []
    for p in parts:
        n, rows = int(np.prod(p.shape)), _part_rows(p.shape)
        out.append(jnp.pad(p.reshape(-1).astype(F32), (0, rows * PACK_W - n)).reshape(rows, PACK_W))
    return jnp.concatenate(out, axis=0)


def _unpack_small(pack, shapes):
    lead = pack.shape[:-2]
    out, off = [], 0
    for s in shapes:
        n, rows = int(np.prod(s)), _part_rows(s)
        part = pack[..., off:off + rows, :].reshape(lead + (-1,))[..., :n]
        out.append(part.reshape(lead + tuple(s)))
        off += rows
    return out


def _assemble(gathered, name):
    shape, ax = _full_shapes()[name]
    return jnp.moveaxis(gathered, 0, ax).reshape(shape)


def _split_dest(full, name):
    shape, ax = _full_shapes()[name]
    sh = shape[:ax] + (N_DEV, shape[ax] // N_DEV) + shape[ax + 1:]
    return jnp.moveaxis(full.reshape(sh), ax, 0)


def _interleave_cols(w, parts, tc):
    C = w.shape[-1] // parts
    sh = w.shape[:-1]
    return w.reshape(sh + (parts, C // tc, tc)).swapaxes(-3, -2).reshape(sh + (parts * C,))


def _deinterleave_cols(w, parts, tc):
    C = w.shape[-1] // parts
    sh = w.shape[:-1]
    return w.reshape(sh + (C // tc, parts, tc)).swapaxes(-3, -2).reshape(sh + (parts * C,))


def _pad_cols(a, to):
    return jnp.pad(a, ((0, 0), (0, to - a.shape[1])))


def _tail_fwd(i, h_in, mix, p_i, W, sp):
    r1, h1, h1b = _ln_fwd(h_in, mix, sp["ln_mix_g"][i], sp["ln_mix_b"][i], f"ln_mix_fwd{i}")
    U = _mm(h1b, W["ffn_up"][i], "nn", F32, f"ffn_up{i}")
    S = _ffn_act_fwd(U, sp["ffn_conv_w"][i], sp["ffn_conv_b"][i], f"ffn_act_fwd{i}")
    ffn = _mm(S, W["ffn_down"][i], "nn", F32, f"ffn_down{i}")
    r2, h2, h2b = _ln_fwd(h1, ffn, sp["ln_ffn_g"][i], sp["ln_ffn_b"][i], f"ln_ffn_fwd{i}")
    G = _mm(h2b, W["ple_gate"][i], "nn", F32, f"ple_gate{i}")
    E = _mm(p_i, W["ple_proj"][i], "nn", F32, f"ple_proj{i}")
    h3, h3b = _ple_fwd(h2, G, sp["ple_b_gate"][i], E, f"ple_fwd{i}")
    return h3, h3b, dict(r1=r1, h1b=h1b, U=U, S=S, r2=r2, h2b=h2b, G=G, E=E, p=p_i)


def _tail_bwd(i, dh3, sv, W, sp, grads):
    alpha = _alpha()
    dE, dGp, dbg = _ple_bwd(dh3, sv["G"], sp["ple_b_gate"][i], sv["E"], f"ple_bwd{i}")
    grads["ple_b_gate"][i] = dbg.sum(0)
    grads["ple_w_proj"][i] = _mm(sv["p"], dE, "tn", F32, f"d_ple_proj{i}")
    grads["ple_w_gate"][i] = _mm(sv["h2b"], dGp, "tn", F32, f"d_ple_gate{i}")
    dh2 = _mm(dGp, W["ple_gate"][i], "nt", F32, f"dx_ple_gate{i}", add=dh3)
    dr2, dr2b, dg, db = _ln_bwd(sv["r2"], dh2, sp["ln_ffn_g"][i], f"ln_ffn_bwd{i}")
    grads["ln_ffn_g"][i], grads["ln_ffn_b"][i] = dg.sum(0), db.sum(0)
    grads["ffn_w_down"][i] = _mm(sv["S"], dr2b, "tn", F32, f"d_ffn_down{i}")
    dS = _mm(dr2b, W["ffn_down"][i], "nt", BF16, f"dx_ffn_down{i}")
    dUg, dUv, dwg, dwv, dbg, dbv = _ffn_act_bwd(sv["U"], dS, sp["ffn_conv_w"][i], sp["ffn_conv_b"][i], f"ffn_act_bwd{i}")
    K = FFN_CONV_WIDTH
    grads["ffn_conv_w"][i] = jnp.concatenate([dwg.reshape(K, SUBLANES, -1).sum(1), dwv.reshape(K, SUBLANES, -1).sum(1)], axis=1)
    grads["ffn_conv_b"][i] = jnp.concatenate([dbg.sum(0), dbv.sum(0)])
    grads["ffn_w_up"][i] = jnp.concatenate([_mm(sv["h1b"], dUg, "tn", F32, f"d_ffn_up_g{i}"),
                                            _mm(sv["h1b"], dUv, "tn", F32, f"d_ffn_up_v{i}")], axis=1)
    dh1 = _mm(dUg, W["ffn_up"][i], "nt", F32, f"dx_ffn_up_g{i}", add=dr2, add_scale=alpha)
    dh1 = _mm(dUv, W["ffn_up"][i], "nt", F32, f"dx_ffn_up_v{i}", add=dh1, b_k_start=D_FF)
    dr1, dr1b, dg, db = _ln_bwd(sv["r1"], dh1, sp["ln_mix_g"][i], f"ln_mix_bwd{i}")
    grads["ln_mix_g"][i], grads["ln_mix_b"][i] = dg.sum(0), db.sum(0)
    return dr1, dr1b


def _even_fwd(h, W, sp, full, gather):
    L = h.shape[0]
    H, Dh = FOX_HEADS, FOX_HEAD_DIM
    Ac = _mm(h, W["even_in_conv"], "nn", F32, "even_in_conv")
    qkv = _mm(h, W["even_in_qkv"], "nn", BF16, "even_in_qkv")
    Af = _mm(h, W["even_in_f"], "nn", F32, "even_in_f")
    y_a = _sconv_fwd(Ac, sp["even_conv_w_il"], "sconv_fwd")
    Fc = _fox_gate_fwd(Af, sp["even_b_f_pad"], "fox_gate_fwd")
    Fh = Fc[:, :H].T
    Fq, Fk = Fh.reshape(H, L, 1), Fh.reshape(H, 1, L)
    o, lse, *gathered = _attn_fwd(qkv, Fq, Fk, "attn_fwd", gather)
    for (n, i), g in zip(_late_units(), gathered):
        full[n][i] = _assemble_unit(g, n)
    _prepare_late(W, full)
    Y = jnp.concatenate([y_a, o], axis=1)
    mix = _mm(Y, W["even_out"], "nn", F32, "even_out")
    return mix, dict(h=h, Ac=Ac, Af=Af, qkv=qkv, Fq=Fq, Fk=Fk, lse=lse, Y=Y)


def _even_bwd(dmix, dres, sv, W, sp, grads, exchange):
    H, Dh = FOX_HEADS, FOX_HEAD_DIM
    C = CONV_DIM
    L = dmix.shape[0]
    grads["even_w_out"][0] = _mm(sv["Y"], dmix, "tn", F32, "d_even_out")
    exchange = [_split_unit(grads["even_w_out"][0], "even_w_out").astype(BF16)] + list(exchange)
    dY = _mm(dmix, W["even_out"], "nt", F32, "dx_even_out")
    dya = dY[:, :C]
    do = dY[:, C:].astype(BF16)
    dc, dcw = _sconv_bwd_dc(sv["Ac"], dya, sp["even_conv_w_il"], "sconv_bwd_dc")
    grads["even_conv_w"][0] = dcw.reshape(CONV_WIDTH, SUBLANES, -1).sum(1)
    dAc = _sconv_bwd_da(sv["Ac"], dya, dc, sp["even_conv_w_il"], "sconv_bwd_da")
    dq, dk, dv, dFk, *arrived = _attn_bwd(sv["qkv"], sv["Fq"], sv["Fk"], sv["lse"], do, "attn_bwd", exchange)
    dqkv = jnp.concatenate([dq, dk.astype(BF16), dv.astype(BF16)], axis=1)
    dF = _pad_cols(dFk.reshape(H, L).T, LANES)
    dAf, dbf = _fox_gate_bwd(sv["Af"], sp["even_b_f_pad"], dF, "fox_gate_bwd")
    grads["even_b_f"][0] = dbf.sum(0)[:H]
    h = sv["h"]
    gc = _deinterleave_cols(_mm(h, dAc, "tn", F32, "d_even_in_conv"), 3, LANES)
    gq = _mm(h, dqkv, "tn", F32, "d_even_in_qkv")
    gf = _mm(h, dAf, "tn", F32, "d_even_in_f")[:, :H]
    grads["even_w_in"][0] = jnp.concatenate([gc, gq, gf], axis=1)
    dh = _mm(dAc, W["even_in_conv"], "nt", F32, "dx_even_in_conv", add=dres, add_scale=_alpha())
    dh = _mm(dqkv, W["even_in_qkv"], "nt", F32, "dx_even_in_qkv", add=dh)
    dh = _mm(dAf, W["even_in_f"], "nt", F32, "dx_even_in_f", add=dh)
    return dh, arrived


def _group_layouts(v, G):
    R = v.shape[0] // G
    return v.reshape(G, 1, R), v.reshape(G, R, 1)


def _odd_fwd(h, W, sp):
    d = _dims()
    L = h.shape[0]
    Hs, G, N, P = d["ssm_heads"], SSM_GROUPS, SSM_STATE, SSM_HEAD_DIM
    R = Hs // G
    inner = d["ssm_inner"]
    z = _mm(h, W["odd_in_z"], "nn", F32, "odd_in_z")
    xr = _mm(h, W["odd_in_x"], "nn", F32, "odd_in_x")
    dtp = _mm(h, W["odd_in_dt"], "nn", F32, "odd_in_dt")
    act = _mconv_fwd(xr, sp["odd_conv_w"], sp["odd_conv_b"], "mconv_fwd")
    dtg = dtp[:, :Hs].reshape(L, G, R)
    dtc, dtr = dtg.transpose(1, 0, 2), dtg.transpose(1, 2, 0)
    dsk_x = jnp.repeat(sp["odd_d_skip"], P).reshape(G, 1, R * P)
    ssd_in = (act, dtc, dtr) + _group_layouts(sp["odd_dt_bias"], G) + _group_layouts(sp["odd_a_log"], G) + (dsk_x,)
    y, hprev = _ssd_fwd(*ssd_in, "ssd_fwd")
    u = _gnorm_fwd(y, z, sp["odd_norm_g"], "gnorm_fwd")
    mix = _mm(u, W["odd_out"], "nn", F32, "odd_out")
    return mix, dict(h=h, z=z, xr=xr, ssd_in=ssd_in, hprev=hprev, y=y, u=u)


def _odd_bwd(dmix, dres, sv, W, sp, grads):
    d = _dims()
    L = dmix.shape[0]
    Hs, G, N, P = d["ssm_heads"], SSM_GROUPS, SSM_STATE, SSM_HEAD_DIM
    grads["odd_w_out"][0] = _mm(sv["u"], dmix, "tn", F32, "d_odd_out")
    du = _mm(dmix, W["odd_out"], "nt", F32, "dx_odd_out")
    dy, dz, dg = _gnorm_bwd(sv["y"], sv["z"], sp["odd_norm_g"], du, "gnorm_bwd")
    grads["odd_norm_g"][0] = dg.sum(0)
    dxs, dB, dC, ddt, gbias, galog, gdsk = _ssd_bwd(*sv["ssd_in"], sv["hprev"], dy, "ssd_bwd")
    grads["odd_dt_bias"][0] = gbias.reshape(Hs)
    grads["odd_a_log"][0] = galog.reshape(Hs)
    grads["odd_d_skip"][0] = gdsk.reshape(Hs)
    dact = jnp.concatenate([dxs, dB, dC], axis=1)
    dxr, dcw, dcb = _mconv_bwd(sv["xr"], dact, sp["odd_conv_w"], sp["odd_conv_b"], "mconv_bwd")
    grads["odd_conv_w"][0] = dcw.reshape(SSM_CONV_WIDTH, SUBLANES, -1).sum(1)
    grads["odd_conv_b"][0] = dcb.sum(0)
    ddtp = _pad_cols(ddt.transpose(1, 0, 2).reshape(L, Hs), W["odd_in_dt"].shape[1])
    h = sv["h"]
    gz = _mm(h, dz, "tn", F32, "d_odd_in_z")
    gx = _mm(h, dxr, "tn", F32, "d_odd_in_x")
    gdt = _mm(h, ddtp, "tn", F32, "d_odd_in_dt")[:, :Hs]
    grads["odd_w_in"][0] = jnp.concatenate([gz, gx, gdt], axis=1)
    dh = _mm(dz, W["odd_in_z"], "nt", F32, "dx_odd_in_z", add=dres, add_scale=_alpha())
    dh = _mm(dxr, W["odd_in_x"], "nt", F32, "dx_odd_in_x", add=dh)
    dh = _mm(ddtp, W["odd_in_dt"], "nt", F32, "dx_odd_in_dt", add=dh)
    return dh


FIRST_UNIT = ("even_w_in", 0)


def _late_units():
    return [(n, i) for n in BIG for i in range(_full_shapes()[n][0][0]) if (n, i) != FIRST_UNIT]


def _assemble_unit(gathered, name):
    shape, ax = _full_shapes()[name]
    g = gathered.reshape((N_DEV,) + _shard_shape(name)[1:])
    return jnp.moveaxis(g, 0, ax - 1).reshape(shape[1:])


def _split_unit(full_layer, name):
    shape, ax = _full_shapes()[name]
    sh = shape[1:ax] + (N_DEV, shape[ax] // N_DEV) + shape[ax + 1:]
    return _as2d(jnp.moveaxis(full_layer.reshape(sh), ax - 1, 0), 1)


def _prepare_first(W, full):
    C, fd = CONV_DIM, _dims()["fox_dim"]
    ew = full["even_w_in"][0]
    W["even_in_conv"] = _interleave_cols(ew[:, :3 * C], 3, LANES)
    W["even_in_qkv"] = ew[:, 3 * C:3 * C + 3 * fd]
    W["even_in_f"] = _pad_cols(ew[:, 3 * C + 3 * fd:], LANES)


def _prepare_late(W, full):
    d = _dims()
    W["even_out"] = full["even_w_out"][0]
    ow = full["odd_w_in"][0]
    inner, cch, Hs = d["ssm_inner"], d["conv_ch"], d["ssm_heads"]
    W["odd_in_z"] = ow[:, :inner]
    W["odd_in_x"] = ow[:, inner:inner + cch]
    W["odd_in_dt"] = _pad_cols(ow[:, inner + cch:], -(-Hs // LANES) * LANES)
    W["odd_out"] = full["odd_w_out"][0]
    for key, name in (("ffn_up", "ffn_w_up"), ("ffn_down", "ffn_w_down"), ("ple_proj", "ple_w_proj"), ("ple_gate", "ple_w_gate")):
        W[key] = list(full[name])


def _prepare_small(full):
    sp = {}
    sp["even_conv_w_il"] = full["even_conv_w"][0]
    sp["even_b_f_pad"] = _pad_cols(full["even_b_f"], LANES)
    sp["odd_conv_w"] = full["odd_conv_w"][0]
    sp["odd_conv_b"] = full["odd_conv_b"][0]
    sp["odd_norm_g"] = full["odd_norm_g"][0]
    for n in ("odd_dt_bias", "odd_a_log", "odd_d_skip"):
        sp[n] = full[n][0]
    for n in ("ln_mix_g", "ln_mix_b", "ln_ffn_g", "ln_ffn_b", "ple_b_gate"):
        sp[n] = full[n]
    sp["ffn_conv_w"] = [full["ffn_conv_w"][i] for i in range(DEPTH)]
    sp["ffn_conv_b"] = [full["ffn_conv_b"][i] for i in range(DEPTH)]
    return sp


def _local_step(x, p, target, full, late_shards):
    sp = _prepare_small(full)
    W = {}
    _prepare_first(W, full)
    grads = {n: [None] * _full_shapes()[n][0][0] for n in WEIGHTS}
    pb = p.astype(BF16)
    mix0, sv_e = _even_fwd(x.astype(BF16), W, sp, full, late_shards)
    h3_0, h3_0b, sv_t0 = _tail_fwd(0, x, mix0, pb[0], W, sp)
    mix1, sv_o = _odd_fwd(h3_0b, W, sp)
    h3_1, _, sv_t1 = _tail_fwd(1, h3_0, mix1, pb[1], W, sp)
    dh, sq = _loss_head(h3_1, target, "loss_head")
    dr1, dr1b = _tail_bwd(1, dh, sv_t1, W, sp, grads)
    dh = _odd_bwd(dr1b, dr1, sv_o, W, sp, grads)
    dr1, dr1b = _tail_bwd(0, dh, sv_t0, W, sp, grads)
    outgoing = [_split_unit(grads[n][i], n).astype(BF16) for n, i in _late_units()[1:]]
    dx, incoming = _even_bwd(dr1b, dr1, sv_e, W, sp, grads, outgoing)
    reduced = {u: _sum_slots(r, f"sum_grads_{u[0]}{u[1]}") for u, r in zip(_late_units(), incoming)}
    grads = {n: v if n in BIG else jnp.stack(v) for n, v in grads.items()}
    return jnp.sum(sq), dx, grads, reduced


def kernel(x, p, even_w_in, even_b_f, even_conv_w, even_w_out, odd_w_in, odd_conv_w, odd_conv_b, odd_dt_bias, odd_a_log, odd_d_skip, odd_norm_g, odd_w_out, ln_mix_g, ln_mix_b, ffn_w_up, ffn_conv_w, ffn_conv_b, ffn_w_down, ln_ffn_g, ln_ffn_b, ple_w_proj, ple_w_gate, ple_b_gate, loss_target, m_even_w_in, m_even_b_f, m_even_conv_w, m_even_w_out, m_odd_w_in, m_odd_conv_w, m_odd_conv_b, m_odd_dt_bias, m_odd_a_log, m_odd_d_skip, m_odd_norm_g, m_odd_w_out, m_ln_mix_g, m_ln_mix_b, m_ffn_w_up, m_ffn_conv_w, m_ffn_conv_b, m_ffn_w_down, m_ln_ffn_g, m_ln_ffn_b, m_ple_w_proj, m_ple_w_gate, m_ple_b_gate, v_even_w_in, v_even_b_f, v_even_conv_w, v_even_w_out, v_odd_w_in, v_odd_conv_w, v_odd_conv_b, v_odd_dt_bias, v_odd_a_log, v_odd_d_skip, v_odd_norm_g, v_odd_w_out, v_ln_mix_g, v_ln_mix_b, v_ffn_w_up, v_ffn_conv_w, v_ffn_conv_b, v_ffn_w_down, v_ln_ffn_g, v_ln_ffn_b, v_ple_w_proj, v_ple_w_gate, v_ple_b_gate):
    args = locals()
    w = {n: args[n] for n in WEIGHTS}
    m = {n: args["m_" + n] for n in WEIGHTS}
    v = {n: args["v_" + n] for n in WEIGHTS}
    me = 4 * lax.axis_index("x") + 2 * lax.axis_index("y") + lax.axis_index("c")

    def shard(unit):
        return _as2d(w[unit[0]][unit[1]]).astype(BF16)

    first, small = _all_gather([shard(FIRST_UNIT), _pack_small([w[n] for n in SMALL_SHARDED])], "ag_weights")
    full = dict(w)
    for n in BIG:
        full[n] = [None] * _full_shapes()[n][0][0]
    full[FIRST_UNIT[0]][FIRST_UNIT[1]] = _assemble_unit(first, FIRST_UNIT[0])
    for n, g in zip(SMALL_SHARDED, _unpack_small(small, [_shard_shape(n) for n in SMALL_SHARDED])):
        full[n] = _assemble(g, n)

    sq, dx, grads, reduced = _local_step(x[0], p[:, 0], loss_target[0], full, [shard(u) for u in _late_units()])
    loss = lax.psum(0.5 * sq / D_MODEL, ("x", "y", "c"))

    n0, i0 = FIRST_UNIT
    (reduced[FIRST_UNIT],) = _reduce_scatter([_split_unit(grads[n0][i0], n0).astype(BF16)], "rs_grads")
    g_final = {n: jnp.stack([reduced[(n, i)] for i in range(_full_shapes()[n][0][0])]).reshape(_shard_shape(n)) for n in BIG}
    small_names = SMALL_SHARDED + REPLICATED
    (small_all,) = _all_gather([_pack_small([grads[n] for n in small_names])], "ag_small_grads")
    small_sum = _sum_slots(small_all, "sum_small_grads")
    for n, g in zip(small_names, _unpack_small(small_sum, [_full_shapes()[n][0] for n in small_names])):
        g_final[n] = lax.dynamic_index_in_dim(_split_dest(g, n), me, axis=0, keepdims=False) if n in SMALL_SHARDED else g

    out = {}
    for n in BIG:
        res = _adamw(*[_as2d(t[n]) for t in (w, g_final, m, v)], "adamw_" + n)
        out[n] = [r.reshape(_shard_shape(n)) for r in res]
    shapes = [_shard_shape(n) for n in small_names]
    res = _adamw(*[_pack_small([t[n] for n in small_names]) for t in (w, g_final, m, v)], "adamw_small")
    for n, d_, m_, v_ in zip(small_names, *[_unpack_small(r, shapes) for r in res]):
        out[n] = [d_, m_, v_]
    return (loss, dx[None], *[g_final[n] for n in WEIGHTS], *[out[n][0] for n in WEIGHTS],
            *[out[n][1] for n in WEIGHTS], *[out[n][2] for n in WEIGHTS])
```

```python
import jax
import jax.numpy as jnp
import numpy as np
from jax import lax
from jax.experimental import pallas as pl
from jax.experimental.pallas import tpu as pltpu

D_MODEL = 1024
SEQ = 8192
DEPTH = 2
CONV_DIM = 512
CONV_WIDTH = 3
FOX_HEADS = 8
FOX_HEAD_DIM = 64
SSM_HEAD_DIM = 64
SSM_GROUPS = 4
SSM_STATE = 128
SSM_CONV_WIDTH = 4
SSM_CHUNK = 128
D_FF = 2816
FFN_CONV_WIDTH = 3
PLE_DIM = 256
LN_EPS = 1e-5
RMS_EPS = 1e-5
ADAM_LR = 0.001
ADAM_B1 = 0.9
ADAM_B2 = 0.999
ADAM_EPS = 1e-08
ADAM_WD = 0.01
ADAM_STEP = 10
N_DEV = 8

F32 = jnp.float32
BF16 = jnp.bfloat16
NEG = -1e30
LANES = 128
SUBLANES = 8
PACK_W = 1024
VMEM_LIMIT = 48 * 1024 * 1024
ATTN_BWD_VMEM_LIMIT = 56 * 1024 * 1024


def _dims():
    fox_dim = FOX_HEADS * FOX_HEAD_DIM
    ssm_inner = 2 * D_MODEL
    ssm_heads = ssm_inner // SSM_HEAD_DIM
    conv_ch = ssm_inner + 2 * SSM_GROUPS * SSM_STATE
    return dict(fox_dim=fox_dim, even_in=3 * CONV_DIM + 3 * fox_dim + FOX_HEADS, even_mix=CONV_DIM + fox_dim,
                ssm_inner=ssm_inner, ssm_heads=ssm_heads, conv_ch=conv_ch, odd_in=ssm_inner + conv_ch + ssm_heads)


def _alpha():
    return (2.0 * DEPTH) ** 0.25


def _pick(dim, prefs):
    for p in prefs:
        if dim % p == 0:
            return p
    return dim


def _pcall(body, **kw):
    return pl.pallas_call(body, **kw)


def _cparams(sem=None, **kw):
    if sem is not None:
        kw["dimension_semantics"] = sem
    return pltpu.CompilerParams(vmem_limit_bytes=VMEM_LIMIT, **kw)


def _sigmoid(x):
    return 1.0 / (1.0 + jnp.exp(-x))


def _softplus(x):
    return jnp.maximum(x, 0.0) + jnp.log(1.0 + jnp.exp(-jnp.abs(x)))


def _sum8(x):
    n, c = x.shape
    return x.reshape(n // SUBLANES, SUBLANES, c).sum(axis=0)


def _dot(a, b, dims):
    return lax.dot_general(a, b, (dims, ((), ())), preferred_element_type=F32)


NN = ((1,), (0,))
NT = ((1,), (1,))
TN = ((0,), (0,))


def _split3(x):
    hi = x.astype(BF16)
    r1 = x - hi.astype(F32)
    mid = r1.astype(BF16)
    lo = (r1 - mid.astype(F32)).astype(BF16)
    return hi, mid, lo


def _tri_mm(tri_bf16, x, tri_first=True):
    if tri_first:
        return sum(_dot(tri_bf16, part, NN) for part in _split3(x))
    return sum(_dot(part, tri_bf16, NN) for part in _split3(x))


def _tri(n, upper=False):
    r = lax.broadcasted_iota(jnp.int32, (n, n), 0)
    c = lax.broadcasted_iota(jnp.int32, (n, n), 1)
    return jnp.where((r <= c) if upper else (r >= c), 1.0, 0.0).astype(BF16)


def _shift_down(cur, prev8, k):
    if k == 0:
        return cur
    ext = jnp.concatenate([prev8, cur], axis=0)
    return pltpu.roll(ext, k, axis=0)[SUBLANES:]


def _shift_up(cur, next8, k):
    if k == 0:
        return cur
    n = cur.shape[0]
    ext = jnp.concatenate([cur, next8], axis=0)
    return pltpu.roll(ext, n + SUBLANES - k, axis=0)[:n]


def _mm(a, b, mode, out_dtype, name, add=None, add_scale=1.0, b_k_start=0):
    if mode == "nn":
        (M, K), (K2, N) = a.shape, b.shape
    elif mode == "nt":
        (M, K), N = a.shape, b.shape[0]
        K2 = K if b.shape[1] >= b_k_start + K else None
    else:
        (K, M), (K2, N) = a.shape, b.shape
    assert K == K2, (a.shape, b.shape, mode)
    tm = _pick(M, (1024, 1408, 512, 256, 128))
    tn = _pick(N, (1408, 1024, 768, 512, 384, 256, 128))
    tk = K if K <= 2048 and b_k_start % K == 0 else _pick(K, (1408, 1024, 768, 512, 256, 128))
    assert b_k_start % tk == 0
    k0 = b_k_start // tk
    nk = K // tk
    dims = {"nn": NN, "nt": NT, "tn": TN}[mode]
    a_spec = pl.BlockSpec((tk, tm), lambda i, j, k: (k, i)) if mode == "tn" else pl.BlockSpec((tm, tk), lambda i, j, k: (i, k))
    b_spec = pl.BlockSpec((tn, tk), lambda i, j, k: (j, k + k0)) if mode == "nt" else pl.BlockSpec((tk, tn), lambda i, j, k: (k, j))
    o_spec = pl.BlockSpec((tm, tn), lambda i, j, k: (i, j))
    has_add = add is not None

    def body(*refs):
        a_ref, b_ref = refs[:2]
        add_ref = refs[2] if has_add else None
        o_ref = refs[2 + has_add]
        prod = _dot(a_ref[...].astype(BF16), b_ref[...].astype(BF16), dims)
        if nk == 1:
            if has_add:
                prod = prod + add_scale * add_ref[...].astype(F32)
            o_ref[...] = prod.astype(out_dtype)
            return
        acc = refs[3 + has_add]
        k = pl.program_id(2)

        @pl.when(k == 0)
        def _():
            if has_add:
                acc[...] = prod + add_scale * add_ref[...].astype(F32)
            else:
                acc[...] = prod

        @pl.when(k > 0)
        def _():
            acc[...] += prod

        @pl.when(k == nk - 1)
        def _():
            o_ref[...] = acc[...].astype(out_dtype)

    ins = [a, b] + ([add] if has_add else [])
    specs = [a_spec, b_spec] + ([o_spec] if has_add else [])
    return _pcall(body, name=name, grid=(M // tm, N // tn, nk), in_specs=specs, out_specs=o_spec,
                  out_shape=jax.ShapeDtypeStruct((M, N), out_dtype),
                  scratch_shapes=[pltpu.VMEM((tm, tn), F32)] if nk > 1 else [],
                  compiler_params=_cparams(("parallel", "parallel", "arbitrary")))(*ins)


def _row_tile(L):
    return _pick(L, (256, 128))


def _conv_row_tile(L, backward):
    return _pick(L, (512, 256, 128)) if backward else _pick(L, (1024, 512, 256, 128))


def _ln_fwd(h, mix, g, b, name):
    L, D = h.shape
    tl = _row_tile(L)
    alpha = _alpha()

    def body(h_ref, m_ref, g_ref, b_ref, r_ref, y_ref, yb_ref):
        r = alpha * h_ref[...] + m_ref[...]
        mu = jnp.mean(r, axis=-1, keepdims=True)
        xc = r - mu
        var = jnp.mean(xc * xc, axis=-1, keepdims=True)
        r_ref[...] = r
        y = xc * lax.rsqrt(var + LN_EPS) * g_ref[...] + b_ref[...]
        y_ref[...] = y
        yb_ref[...] = y.astype(BF16)

    row = pl.BlockSpec((tl, D), lambda i: (i, 0))
    vec = pl.BlockSpec((1, D), lambda i: (0, 0))
    return _pcall(body, name=name, grid=(L // tl,), in_specs=[row, row, vec, vec], out_specs=[row, row, row],
                  out_shape=[jax.ShapeDtypeStruct((L, D), F32)] * 2 + [jax.ShapeDtypeStruct((L, D), BF16)],
                  compiler_params=_cparams(("parallel",)))(h, mix, g.reshape(1, D), b.reshape(1, D))


def _ln_bwd(r, dy, g, name):
    L, D = r.shape
    tl = _row_tile(L)

    def body(r_ref, dy_ref, g_ref, dr_ref, drb_ref, dg_ref, db_ref):
        i = pl.program_id(0)
        r_ = r_ref[...]
        dy_ = dy_ref[...]
        mu = jnp.mean(r_, axis=-1, keepdims=True)
        xc = r_ - mu
        rstd = lax.rsqrt(jnp.mean(xc * xc, axis=-1, keepdims=True) + LN_EPS)
        xhat = xc * rstd
        dxh = dy_ * g_ref[...]
        dr = rstd * (dxh - jnp.mean(dxh, axis=-1, keepdims=True) - xhat * jnp.mean(dxh * xhat, axis=-1, keepdims=True))
        dr_ref[...] = dr
        drb_ref[...] = dr.astype(BF16)

        @pl.when(i == 0)
        def _():
            dg_ref[...] = jnp.zeros_like(dg_ref)
            db_ref[...] = jnp.zeros_like(db_ref)

        dg_ref[...] += _sum8(dy_ * xhat)
        db_ref[...] += _sum8(dy_)

    row = pl.BlockSpec((tl, D), lambda i: (i, 0))
    vec = pl.BlockSpec((1, D), lambda i: (0, 0))
    acc = pl.BlockSpec((SUBLANES, D), lambda i: (0, 0))
    return _pcall(body, name=name, grid=(L // tl,), in_specs=[row, row, vec], out_specs=[row, row, acc, acc],
                  out_shape=[jax.ShapeDtypeStruct((L, D), F32), jax.ShapeDtypeStruct((L, D), BF16),
                             jax.ShapeDtypeStruct((SUBLANES, D), F32), jax.ShapeDtypeStruct((SUBLANES, D), F32)],
                  compiler_params=_cparams(("arbitrary",)))(r, dy, g.reshape(1, D))


def _ple_fwd(h2, G, bg, E, name):
    L, D = h2.shape
    tl = _row_tile(L)

    def body(h_ref, g_ref, b_ref, e_ref, o_ref, ob_ref):
        o = h_ref[...] + _sigmoid(g_ref[...] + b_ref[...]) * e_ref[...]
        o_ref[...] = o
        ob_ref[...] = o.astype(BF16)

    row = pl.BlockSpec((tl, D), lambda i: (i, 0))
    vec = pl.BlockSpec((1, D), lambda i: (0, 0))
    return _pcall(body, name=name, grid=(L // tl,), in_specs=[row, row, vec, row], out_specs=[row, row],
                  out_shape=[jax.ShapeDtypeStruct((L, D), F32), jax.ShapeDtypeStruct((L, D), BF16)],
                  compiler_params=_cparams(("parallel",)))(h2, G, bg.reshape(1, D), E)


def _ple_bwd(dh3, G, bg, E, name):
    L, D = dh3.shape
    tl = _row_tile(L)

    def body(d_ref, g_ref, b_ref, e_ref, de_ref, dg_ref, db_ref):
        i = pl.program_id(0)
        d = d_ref[...]
        sg = _sigmoid(g_ref[...] + b_ref[...])
        de_ref[...] = (d * sg).astype(BF16)
        dgp = d * e_ref[...] * sg * (1.0 - sg)
        dg_ref[...] = dgp.astype(BF16)

        @pl.when(i == 0)
        def _():
            db_ref[...] = jnp.zeros_like(db_ref)

        db_ref[...] += _sum8(dgp)

    row = pl.BlockSpec((tl, D), lambda i: (i, 0))
    vec = pl.BlockSpec((1, D), lambda i: (0, 0))
    acc = pl.BlockSpec((SUBLANES, D), lambda i: (0, 0))
    return _pcall(body, name=name, grid=(L // tl,), in_specs=[row, row, vec, row], out_specs=[row, row, acc],
                  out_shape=[jax.ShapeDtypeStruct((L, D), BF16), jax.ShapeDtypeStruct((L, D), BF16),
                             jax.ShapeDtypeStruct((SUBLANES, D), F32)],
                  compiler_params=_cparams(("arbitrary",)))(dh3, G, bg.reshape(1, D), E)


def _loss_head(h, target, name):
    L, D = h.shape
    tl = _row_tile(L)

    def body(h_ref, t_ref, d_ref, s_ref):
        i = pl.program_id(0)
        e = h_ref[...] - t_ref[...]
        d_ref[...] = e * (1.0 / D)

        @pl.when(i == 0)
        def _():
            s_ref[...] = jnp.zeros_like(s_ref)

        s_ref[...] += _sum8(e * e)

    row = pl.BlockSpec((tl, D), lambda i: (i, 0))
    acc = pl.BlockSpec((SUBLANES, D), lambda i: (0, 0))
    return _pcall(body, name=name, grid=(L // tl,), in_specs=[row, row], out_specs=[row, acc],
                  out_shape=[jax.ShapeDtypeStruct((L, D), F32), jax.ShapeDtypeStruct((SUBLANES, D), F32)],
                  compiler_params=_cparams(("arbitrary",)))(h, target)


def _halo_prev(tl, ncol_blocks_fn):
    return lambda j, i: (jnp.maximum(i * (tl // SUBLANES) - 1, 0), ncol_blocks_fn(j))


STRIP_ROWS = 64


def _strip(s, R):
    return pl.ds(s * R if isinstance(s, int) else pl.multiple_of(s * R, R), R)


def _conv_taps(cur, prev, w_ref, K):
    acc = w_ref[K - 1:K, :] * cur
    for k in range(K - 1):
        acc = acc + w_ref[k:k + 1, :] * _shift_down(cur, prev, K - 1 - k)
    return acc


def _ffn_act_fwd(U, w, b, name):
    L, F2 = U.shape
    F = F2 // 2
    K = w.shape[0]
    tc = _pick(F, (1408, 256, 128))
    tl = _pick(L, (512, 256, 128)) if tc > 256 else _conv_row_tile(L, False)
    nj = F // tc

    def body(ug_ref, uv_ref, ugp_ref, uvp_ref, wg_ref, wv_ref, bg_ref, bv_ref, s_ref):
        i = pl.program_id(1)
        g = _conv_taps(ug_ref[...], jnp.where(i == 0, 0.0, ugp_ref[...]), wg_ref, K) + bg_ref[...]
        v = _conv_taps(uv_ref[...], jnp.where(i == 0, 0.0, uvp_ref[...]), wv_ref, K) + bv_ref[...]
        s_ref[...] = (g * _sigmoid(g) * v).astype(BF16)

    def both(shape, index):
        return [pl.BlockSpec(shape, lambda j, i: index(j, i)), pl.BlockSpec(shape, lambda j, i: index(j + nj, i))]

    b2 = b.reshape(1, F2)
    return _pcall(body, name=name, grid=(nj, L // tl),
                  in_specs=both((tl, tc), lambda j, i: (i, j)) + both((SUBLANES, tc), _halo_prev(tl, lambda j: j))
                  + both((K, tc), lambda j, i: (0, j)) + both((1, tc), lambda j, i: (0, j)),
                  out_specs=pl.BlockSpec((tl, tc), lambda j, i: (i, j)),
                  out_shape=jax.ShapeDtypeStruct((L, F), BF16),
                  compiler_params=_cparams(("parallel", "parallel")))(U, U, U, U, w, w, b2, b2)


def _halo_next(tl, L, rows):
    return lambda j, i: (jnp.minimum((i + 1) * (tl // rows), L // rows - 1), j)


BF16_ROWS = 16


def _ffn_act_bwd(U, dS, w, b, name):
    L, F2 = U.shape
    F = F2 // 2
    K = w.shape[0]
    tc = _pick(F, (256, 128))
    tl = _pick(L, (1024, 512, 256, 128))
    nl = L // tl
    nj = F // tc
    R = STRIP_ROWS
    ns = tl // R

    def body(ug_ref, uv_ref, ugp_ref, uvp_ref, ugn_ref, uvn_ref, ds_ref, dsn_ref, wg_ref, wv_ref, bg_ref, bv_ref,
             dug_ref, duv_ref, dwg_ref, dwv_ref, dbg_ref, dbv_ref):
        i = pl.program_id(1)

        @pl.when(i == 0)
        def _():
            for r in (dwg_ref, dwv_ref, dbg_ref, dbv_ref):
                r[...] = jnp.zeros_like(r)

        for lanes in (slice(c, c + LANES) for c in range(0, tc, LANES)):
            wts = [[w_ref[k:k + 1, lanes] for k in range(K)] for w_ref in (wg_ref, wv_ref)]
            bias = [b_ref[:, lanes] for b_ref in (bg_ref, bv_ref)]

            def strip(s):
                return _strip(s, R)

            def at_conv_out(x, xprev8, ds):
                sh = [[_shift_down(x[h], xprev8[h], K - 1 - k) for k in range(K)] for h in range(2)]
                g, v = (sum(wts[h][k] * sh[h][k] for k in range(K)) + bias[h] for h in range(2))
                sg = _sigmoid(g)
                return (ds * v * sg * (1.0 + g * (1.0 - sg)), ds * g * sg), sh

            def emit(rows, duc, duc_next8):
                for h, du_ref in enumerate((dug_ref, duv_ref)):
                    du = sum(wts[h][k] * _shift_up(duc[h], duc_next8[h], K - 1 - k) for k in range(K))
                    du_ref[rows, lanes] = du.astype(BF16)

            def accumulate(acc, duc, sh):
                dw, db = acc
                return (tuple(tuple(dw[h][k] + _sum8(duc[h] * sh[h][k]) for k in range(K)) for h in range(2)),
                        tuple(db[h] + _sum8(duc[h]) for h in range(2)))

            def last8(x):
                return tuple(a[R - SUBLANES:] for a in x)

            zero = jnp.zeros((SUBLANES, LANES), F32)
            x0 = (ug_ref[strip(0), lanes], uv_ref[strip(0), lanes])
            prev8 = (jnp.where(i == 0, 0.0, ugp_ref[:, lanes]), jnp.where(i == 0, 0.0, uvp_ref[:, lanes]))
            duc0, sh0 = at_conv_out(x0, prev8, ds_ref[strip(0), lanes].astype(F32))
            acc0 = accumulate(((((zero,) * K,) * 2), (zero,) * 2), duc0, sh0)

            def step(s, carry):
                xprev, ducprev, acc = carry
                x = (ug_ref[strip(s), lanes], uv_ref[strip(s), lanes])
                duc, sh = at_conv_out(x, last8(xprev), ds_ref[strip(s), lanes].astype(F32))
                emit(strip(s - 1), ducprev, tuple(d[:SUBLANES] for d in duc))
                return x, duc, accumulate(acc, duc, sh)

            xl, ducl, (dw, db) = lax.fori_loop(1, ns, step, (x0, duc0, acc0))
            ducn, _ = at_conv_out((ugn_ref[:, lanes], uvn_ref[:, lanes]), last8(xl), dsn_ref[:, lanes].astype(F32)[:SUBLANES])
            emit(strip(ns - 1), ducl, tuple(jnp.where(i == nl - 1, 0.0, d) for d in ducn))
            for h, (dw_ref, db_ref) in enumerate(((dwg_ref, dbg_ref), (dwv_ref, dbv_ref))):
                db_ref[:, lanes] += db[h]
                for k in range(K):
                    dw_ref[k * SUBLANES:(k + 1) * SUBLANES, lanes] += dw[h][k]

    def both(shape, index):
        return [pl.BlockSpec(shape, lambda j, i: index(j, i)), pl.BlockSpec(shape, lambda j, i: index(j + nj, i))]

    b2 = b.reshape(1, F2)
    du_specs, du_shapes = [pl.BlockSpec((tl, tc), lambda j, i: (i, j))] * 2, [jax.ShapeDtypeStruct((L, F), BF16)] * 2
    dw_specs = [pl.BlockSpec((K * SUBLANES, tc), lambda j, i: (0, j))] * 2
    dw_shapes = [jax.ShapeDtypeStruct((K * SUBLANES, F), F32)] * 2
    db_specs, db_shapes = [pl.BlockSpec((SUBLANES, tc), lambda j, i: (0, j))] * 2, [jax.ShapeDtypeStruct((SUBLANES, F), F32)] * 2
    return _pcall(body, name=name, grid=(nj, nl),
                  in_specs=both((tl, tc), lambda j, i: (i, j)) + both((SUBLANES, tc), _halo_prev(tl, lambda j: j))
                  + both((SUBLANES, tc), _halo_next(tl, L, SUBLANES))
                  + [pl.BlockSpec((tl, tc), lambda j, i: (i, j)), pl.BlockSpec((BF16_ROWS, tc), _halo_next(tl, L, BF16_ROWS))]
                  + both((K, tc), lambda j, i: (0, j)) + both((1, tc), lambda j, i: (0, j)),
                  out_specs=du_specs + dw_specs + db_specs, out_shape=du_shapes + dw_shapes + db_shapes,
                  compiler_params=_cparams(("parallel", "arbitrary")))(U, U, U, U, U, U, dS, dS, w, w, b2, b2)


def _sconv_fwd(Ac, w, name):
    L, C3 = Ac.shape
    C = C3 // 3
    K = w.shape[0]
    tl = _conv_row_tile(L, False)
    tc = LANES

    def body(a_ref, ap_ref, w_ref, y_ref):
        i = pl.program_id(1)
        a = a_ref[...]
        ap = ap_ref[...]
        p = a[:, tc:2 * tc] * a[:, 2 * tc:]
        pp = jnp.where(i == 0, 0.0, ap[:, tc:2 * tc] * ap[:, 2 * tc:])
        y_ref[...] = (a[:, :tc] * _conv_taps(p, pp, w_ref, K)).astype(BF16)

    return _pcall(body, name=name, grid=(C // tc, L // tl),
                  in_specs=[pl.BlockSpec((tl, 3 * tc), lambda j, i: (i, j)),
                            pl.BlockSpec((SUBLANES, 3 * tc), _halo_prev(tl, lambda j: j)),
                            pl.BlockSpec((K, tc), lambda j, i: (0, j))],
                  out_specs=pl.BlockSpec((tl, tc), lambda j, i: (i, j)),
                  out_shape=jax.ShapeDtypeStruct((L, C), BF16),
                  compiler_params=_cparams(("parallel", "parallel")))(Ac, Ac, w)


def _sconv_bwd_dc(Ac, dy, w, name):
    L, C3 = Ac.shape
    C = C3 // 3
    K = w.shape[0]
    tl = _conv_row_tile(L, False)
    tc = LANES

    def body(a_ref, ap_ref, dy_ref, dc_ref, dw_ref):
        i = pl.program_id(1)
        a = a_ref[...]
        ap = ap_ref[...]
        p = a[:, tc:2 * tc] * a[:, 2 * tc:]
        pp = jnp.where(i == 0, 0.0, ap[:, tc:2 * tc] * ap[:, 2 * tc:])
        dc = dy_ref[...] * a[:, :tc]
        dc_ref[...] = dc

        @pl.when(i == 0)
        def _():
            dw_ref[...] = jnp.zeros_like(dw_ref)

        for k in range(K):
            dw_ref[k * SUBLANES:(k + 1) * SUBLANES, :] += _sum8(dc * _shift_down(p, pp, K - 1 - k))

    return _pcall(body, name=name, grid=(C // tc, L // tl),
                  in_specs=[pl.BlockSpec((tl, 3 * tc), lambda j, i: (i, j)),
                            pl.BlockSpec((SUBLANES, 3 * tc), _halo_prev(tl, lambda j: j)),
                            pl.BlockSpec((tl, tc), lambda j, i: (i, j))],
                  out_specs=[pl.BlockSpec((tl, tc), lambda j, i: (i, j)),
                             pl.BlockSpec((K * SUBLANES, tc), lambda j, i: (0, j))],
                  out_shape=[jax.ShapeDtypeStruct((L, C), F32), jax.ShapeDtypeStruct((K * SUBLANES, C), F32)],
                  compiler_params=_cparams(("parallel", "arbitrary")))(Ac, Ac, dy)


def _sconv_bwd_da(Ac, dy, dc, w, name):
    L, C3 = Ac.shape
    C = C3 // 3
    K = w.shape[0]
    tl = _conv_row_tile(L, False)
    tc = LANES
    nl = L // tl

    def body(a_ref, ap_ref, dy_ref, dc_ref, dcn_ref, w_ref, o_ref):
        i = pl.program_id(1)
        a = a_ref[...]
        ap = ap_ref[...]
        gc, h = a[:, tc:2 * tc], a[:, 2 * tc:]
        p = gc * h
        pp = jnp.where(i == 0, 0.0, ap[:, tc:2 * tc] * ap[:, 2 * tc:])
        dgb = dy_ref[...] * _conv_taps(p, pp, w_ref, K)
        cur = dc_ref[...]
        nxt = jnp.where(i == nl - 1, 0.0, dcn_ref[...])
        dp = w_ref[K - 1:K, :] * cur
        for k in range(K - 1):
            dp = dp + w_ref[k:k + 1, :] * _shift_up(cur, nxt, K - 1 - k)
        o_ref[...] = jnp.concatenate([dgb, dp * h, dp * gc], axis=1).astype(BF16)

    return _pcall(body, name=name, grid=(C // tc, nl),
                  in_specs=[pl.BlockSpec((tl, 3 * tc), lambda j, i: (i, j)),
                            pl.BlockSpec((SUBLANES, 3 * tc), _halo_prev(tl, lambda j: j)),
                            pl.BlockSpec((tl, tc), lambda j, i: (i, j)),
                            pl.BlockSpec((tl, tc), lambda j, i: (i, j)),
                            pl.BlockSpec((SUBLANES, tc), lambda j, i: (jnp.minimum((i + 1) * (tl // SUBLANES), L // SUBLANES - 1), j)),
                            pl.BlockSpec((K, tc), lambda j, i: (0, j))],
                  out_specs=pl.BlockSpec((tl, 3 * tc), lambda j, i: (i, j)),
                  out_shape=jax.ShapeDtypeStruct((L, C3), BF16),
                  compiler_params=_cparams(("parallel", "parallel")))(Ac, Ac, dy, dc, dc, w)


def _fox_gate_fwd(Af, bf, name):
    L, W = Af.shape
    tl = _pick(L, (512, 256, 128))

    def body(a_ref, b_ref, f_ref, carry):
        i = pl.program_id(0)

        @pl.when(i == 0)
        def _():
            carry[...] = jnp.zeros_like(carry)

        z = a_ref[...] + b_ref[...]
        logf = jnp.minimum(z, 0.0) - jnp.log(1.0 + jnp.exp(-jnp.abs(z)))
        f = _tri_mm(_tri(tl), logf) + carry[...]
        f_ref[...] = f
        carry[...] = f[tl - 1:tl, :]

    row = pl.BlockSpec((tl, W), lambda i: (i, 0))
    return _pcall(body, name=name, grid=(L // tl,), in_specs=[row, pl.BlockSpec((1, W), lambda i: (0, 0))], out_specs=row,
                  out_shape=jax.ShapeDtypeStruct((L, W), F32), scratch_shapes=[pltpu.VMEM((1, W), F32)],
                  compiler_params=_cparams(("arbitrary",)))(Af, bf)


def _fox_gate_bwd(Af, bf, dF, name):
    L, W = Af.shape
    tl = _pick(L, (512, 256, 128))
    nl = L // tl

    def body(a_ref, b_ref, df_ref, o_ref, db_ref, carry):
        i = pl.program_id(0)

        @pl.when(i == 0)
        def _():
            carry[...] = jnp.zeros_like(carry)
            db_ref[...] = jnp.zeros_like(db_ref)

        z = a_ref[...] + b_ref[...]
        dlogf = _tri_mm(_tri(tl, upper=True), df_ref[...]) + carry[...]
        carry[...] = dlogf[0:1, :]
        dz = dlogf * _sigmoid(-z)
        o_ref[...] = dz
        db_ref[...] += _sum8(dz)

    row = pl.BlockSpec((tl, W), lambda i: (nl - 1 - i, 0))
    return _pcall(body, name=name, grid=(nl,),
                  in_specs=[row, pl.BlockSpec((1, W), lambda i: (0, 0)), row],
                  out_specs=[row, pl.BlockSpec((SUBLANES, W), lambda i: (0, 0))],
                  out_shape=[jax.ShapeDtypeStruct((L, W), F32), jax.ShapeDtypeStruct((SUBLANES, W), F32)],
                  scratch_shapes=[pltpu.VMEM((1, W), F32)],
                  compiler_params=_cparams(("arbitrary",)))(Af, bf, dF)


def _attn_tiles(L):
    t = _pick(L, (512, 256, 128))
    return t, t


def _attn_scores(q, k, fq, fk, diag, scale):
    s = _dot(q, k, NT) * scale + (fq - fk)
    if not diag:
        return s
    row = lax.broadcasted_iota(jnp.int32, s.shape, 0)
    col = lax.broadcasted_iota(jnp.int32, s.shape, 1)
    return jnp.where(col <= row, s, NEG)


def _attn_geometry():
    Dh = FOX_HEAD_DIM
    hpt = LANES // Dh
    return Dh, hpt, FOX_HEADS // hpt


def _head_lanes(shape, Dh, hpt):
    lane = lax.broadcasted_iota(jnp.int32, shape, len(shape) - 1)
    return [(lane >= h * Dh) & (lane < (h + 1) * Dh) for h in range(hpt)]


def _attn_specs(t, L, hpt, ng):
    return dict(
        col=lambda off: pl.BlockSpec((t, LANES), lambda g, i: (i, g + off)),
        full=lambda off: pl.BlockSpec((L, LANES), lambda g, i: (0, g + off)),
        hq=pl.BlockSpec((hpt, t, 1), lambda g, i: (g, i, 0)),
        hk_full=pl.BlockSpec((hpt, 1, L), lambda g, i: (g, 0, 0)),
        hk=pl.BlockSpec((hpt, 1, t), lambda g, i: (g, 0, i)))


def _attn_fwd(qkv, Fq, Fk, name, gather=()):
    L = qkv.shape[0]
    Dh, hpt, ng = _attn_geometry()
    t, _ = _attn_tiles(L)
    scale = Dh ** -0.5
    n = len(gather)
    nsteps = ng * (L // t)

    def body(*refs):
        q_ref, k_ref, v_ref, fq_ref, fk_ref = refs[:5]
        o_ref, lse_ref = refs[5 + n:7 + n]
        qi = pl.program_id(1)
        step = pl.program_id(0) * (L // t) + qi
        if n:
            start, forward, finish = _gather_phases(refs[5:5 + n], refs[7 + n:7 + 2 * n], *refs[7 + 2 * n:])
            pl.when(step == 0)(start)
            pl.when(step == nsteps // 2)(forward)
        sel = _head_lanes((t, LANES), Dh, hpt)
        q2 = q_ref[...]
        qh = [jnp.where(sel[h], q2, 0) for h in range(hpt)]
        fq = [fq_ref[h] for h in range(hpt)]

        def chunk(j, carry, diag):
            rows = pl.ds(pl.multiple_of(j * t, t), t)
            kc, vc = k_ref[rows, :], v_ref[rows, :]
            out = []
            for h in range(hpt):
                m, l, acc = carry[h]
                s = _attn_scores(qh[h], kc, fq[h], fk_ref[h, :, rows], diag, scale)
                m_new = jnp.maximum(m, jnp.max(s, axis=-1, keepdims=True))
                p = jnp.exp(s - m_new)
                a = jnp.exp(m - m_new)
                out.append((m_new, a * l + jnp.sum(p, axis=-1, keepdims=True), a * acc + _dot(p.astype(BF16), vc, NN)))
            return tuple(out)

        init = tuple((jnp.full((t, 1), NEG, F32), jnp.zeros((t, 1), F32), jnp.zeros((t, LANES), F32)) for _ in range(hpt))
        fin = chunk(qi, lax.fori_loop(0, qi, lambda j, c: chunk(j, c, False), init), True)
        o = jnp.zeros((t, LANES), F32)
        for h, (m, l, acc) in enumerate(fin):
            o = jnp.where(sel[h], acc / l, o)
            lse_ref[h] = m + jnp.log(l)
        o_ref[...] = o.astype(BF16)
        if n:
            pl.when(step == nsteps - 1)(finish)

    sp = _attn_specs(t, L, hpt, ng)
    return _pcall(body, name=name, grid=(ng, L // t),
                  in_specs=[sp["col"](0), sp["full"](ng), sp["full"](2 * ng), sp["hq"], sp["hk_full"]] + [HBM_SPEC] * n,
                  out_specs=[sp["col"](0), sp["hq"]] + [HBM_SPEC] * n,
                  out_shape=[jax.ShapeDtypeStruct((L, ng * LANES), BF16), jax.ShapeDtypeStruct((FOX_HEADS, L, 1), F32)]
                  + _gather_shapes(gather),
                  scratch_shapes=_gather_sems(n) if n else [],
                  compiler_params=_cparams(("arbitrary", "arbitrary") if n else ("parallel", "arbitrary")))(
        qkv, qkv, qkv, Fq, Fk, *gather)


def _attn_bwd(qkv, Fq, Fk, lse, do, name, exchange=()):
    L = qkv.shape[0]
    Dh, hpt, ng = _attn_geometry()
    _, tk = _attn_tiles(L)
    tq = _pick(L, (256, 128))
    nkc = L // tk
    scale = Dh ** -0.5

    n = len(exchange)
    nsteps = ng * (L // tq)

    def body(*refs):
        q_ref, k_ref, v_ref, fq_ref, fk_ref, lse_ref, do_ref = refs[:7]
        dq_ref, dk_ref, dv_ref, df_ref = refs[7 + n:11 + n]
        p_s, dp_s = refs[11 + 2 * n:13 + 2 * n]
        qi = pl.program_id(1)
        step = pl.program_id(0) * (L // tq) + qi
        if n:
            start, finish = _exchange_phases(refs[7:7 + n], refs[11 + n:11 + 2 * n], *refs[13 + 2 * n:])
            pl.when(step == 0)(start)

        @pl.when(qi == 0)
        def _():
            dk_ref[...] = jnp.zeros_like(dk_ref)
            dv_ref[...] = jnp.zeros_like(dv_ref)
            df_ref[...] = jnp.zeros_like(df_ref)

        sel = _head_lanes((tq, LANES), Dh, hpt)
        q2, do2 = q_ref[...], do_ref[...]
        jd = (qi * tq) // tk
        off = qi * tq - jd * tk
        dq = jnp.zeros((tq, LANES), F32)
        for h in range(hpt):
            qh, doh = jnp.where(sel[h], q2, 0), jnp.where(sel[h], do2, 0)
            fq, lse = fq_ref[h], lse_ref[h]

            def first(j, acc, diag):
                rows = pl.ds(pl.multiple_of(j * tk, tk), tk)
                s = _dot(qh, k_ref[rows, :], NT) * scale + (fq - fk_ref[h, :, rows])
                if diag:
                    row = lax.broadcasted_iota(jnp.int32, s.shape, 0) + off
                    s = jnp.where(lax.broadcasted_iota(jnp.int32, s.shape, 1) <= row, s, NEG)
                p = jnp.exp(s - lse)
                dp = _dot(doh, v_ref[rows, :], NT)
                p_s[j] = p
                dp_s[j] = dp
                return acc + jnp.sum(p * dp, axis=-1, keepdims=True)

            delta = first(jd, lax.fori_loop(0, jd, lambda j, c: first(j, c, False), jnp.zeros((tq, 1), F32)), True)

            def second(j, acc):
                rows = pl.ds(pl.multiple_of(j * tk, tk), tk)
                p = p_s[j]
                ds = p * (dp_s[j] - delta)
                dsb = ds.astype(BF16)
                dk_ref[rows, :] += _dot(dsb, qh, TN)
                dv_ref[rows, :] += _dot(p.astype(BF16), doh, TN)
                df_ref[h, :, rows] -= jnp.sum(ds, axis=0, keepdims=True)
                return acc + _dot(dsb, k_ref[rows, :], NN)

            dq = jnp.where(sel[h], lax.fori_loop(0, jd + 1, second, jnp.zeros((tq, LANES), F32)), dq)
        dq_ref[...] = (dq * scale).astype(BF16)

        @pl.when(qi == L // tq - 1)
        def _():
            dk_ref[...] *= scale

        if n:
            pl.when(step == nsteps - 1)(finish)

    sp = _attn_specs(tq, L, hpt, ng)
    return _pcall(body, name=name, grid=(ng, L // tq),
                  in_specs=[sp["col"](0), sp["full"](ng), sp["full"](2 * ng), sp["hq"], sp["hk_full"], sp["hq"], sp["col"](0)]
                  + [HBM_SPEC] * n,
                  out_specs=[sp["col"](0), sp["full"](0), sp["full"](0), sp["hk_full"]] + [HBM_SPEC] * n,
                  out_shape=[jax.ShapeDtypeStruct((L, ng * LANES), BF16), jax.ShapeDtypeStruct((L, ng * LANES), F32),
                             jax.ShapeDtypeStruct((L, ng * LANES), F32), jax.ShapeDtypeStruct((FOX_HEADS, 1, L), F32)]
                  + [jax.ShapeDtypeStruct(g.shape, g.dtype) for g in exchange],
                  scratch_shapes=[pltpu.VMEM((nkc, tq, tk), F32), pltpu.VMEM((nkc, tq, tk), F32)] + (_gather_sems(n) if n else []),
                  compiler_params=pltpu.CompilerParams(
                      vmem_limit_bytes=ATTN_BWD_VMEM_LIMIT,
                      dimension_semantics=("arbitrary", "arbitrary") if n else ("parallel", "arbitrary")))(
        qkv, qkv, qkv, Fq, Fk, lse, do, *exchange)


def _mconv_fwd(xr, w, b, name):
    L, C = xr.shape
    K = w.shape[0]
    tl = _conv_row_tile(L, False)
    tc = _pick(C, (512, 384, 256, 128))

    def body(x_ref, xp_ref, w_ref, b_ref, o_ref):
        i = pl.program_id(1)
        prev = jnp.where(i == 0, 0.0, xp_ref[...])
        pre = _conv_taps(x_ref[...], prev, w_ref, K) + b_ref[...]
        o_ref[...] = pre * _sigmoid(pre)

    return _pcall(body, name=name, grid=(C // tc, L // tl),
                  in_specs=[pl.BlockSpec((tl, tc), lambda j, i: (i, j)),
                            pl.BlockSpec((SUBLANES, tc), _halo_prev(tl, lambda j: j)),
                            pl.BlockSpec((K, tc), lambda j, i: (0, j)),
                            pl.BlockSpec((1, tc), lambda j, i: (0, j))],
                  out_specs=pl.BlockSpec((tl, tc), lambda j, i: (i, j)),
                  out_shape=jax.ShapeDtypeStruct((L, C), F32),
                  compiler_params=_cparams(("parallel", "parallel")))(xr, xr, w, b.reshape(1, C))


def _mconv_bwd(xr, dact, w, b, name):
    L, C = xr.shape
    K = w.shape[0]
    tl = _conv_row_tile(L, True)
    tc = _pick(C, (512, 384, 256, 128))
    nl = L // tl

    R = STRIP_ROWS
    ns = tl // R

    def body(x_ref, xp_ref, xn_ref, d_ref, dn_ref, w_ref, b_ref, o_ref, dw_ref, db_ref):
        i = pl.program_id(1)

        @pl.when(i == 0)
        def _():
            dw_ref[...] = jnp.zeros_like(dw_ref)
            db_ref[...] = jnp.zeros_like(db_ref)

        for lanes in (slice(c, c + LANES) for c in range(0, tc, LANES)):
            wts = [w_ref[k:k + 1, lanes] for k in range(K)]
            bias = b_ref[:, lanes]

            def at_conv_out(x, xprev8, d):
                sh = [_shift_down(x, xprev8, K - 1 - k) for k in range(K)]
                pre = sum(wts[k] * sh[k] for k in range(K)) + bias
                sg = _sigmoid(pre)
                return d * sg * (1.0 + pre * (1.0 - sg)), sh

            def emit(rows, dpre, dpre_next8):
                o_ref[rows, lanes] = sum(wts[k] * _shift_up(dpre, dpre_next8, K - 1 - k) for k in range(K)).astype(BF16)

            def accumulate(acc, dpre, sh):
                return tuple(acc[k] + _sum8(dpre * sh[k]) for k in range(K)) + (acc[K] + _sum8(dpre),)

            x0 = x_ref[_strip(0, R), lanes]
            dpre0, sh0 = at_conv_out(x0, jnp.where(i == 0, 0.0, xp_ref[:, lanes]), d_ref[_strip(0, R), lanes])
            acc0 = accumulate((jnp.zeros((SUBLANES, LANES), F32),) * (K + 1), dpre0, sh0)

            def step(s, carry):
                xprev, dprev, acc = carry
                x = x_ref[_strip(s, R), lanes]
                dpre, sh = at_conv_out(x, xprev[R - SUBLANES:], d_ref[_strip(s, R), lanes])
                emit(_strip(s - 1, R), dprev, dpre[:SUBLANES])
                return x, dpre, accumulate(acc, dpre, sh)

            xl, dl, acc = lax.fori_loop(1, ns, step, (x0, dpre0, acc0))
            dn, _ = at_conv_out(xn_ref[:, lanes], xl[R - SUBLANES:], dn_ref[:, lanes])
            emit(_strip(ns - 1, R), dl, jnp.where(i == nl - 1, 0.0, dn))
            db_ref[:, lanes] += acc[K]
            for k in range(K):
                dw_ref[k * SUBLANES:(k + 1) * SUBLANES, lanes] += acc[k]

    return _pcall(body, name=name, grid=(C // tc, nl),
                  in_specs=[pl.BlockSpec((tl, tc), lambda j, i: (i, j)),
                            pl.BlockSpec((SUBLANES, tc), _halo_prev(tl, lambda j: j)),
                            pl.BlockSpec((SUBLANES, tc), _halo_next(tl, L, SUBLANES)),
                            pl.BlockSpec((tl, tc), lambda j, i: (i, j)),
                            pl.BlockSpec((SUBLANES, tc), _halo_next(tl, L, SUBLANES)),
                            pl.BlockSpec((K, tc), lambda j, i: (0, j)),
                            pl.BlockSpec((1, tc), lambda j, i: (0, j))],
                  out_specs=[pl.BlockSpec((tl, tc), lambda j, i: (i, j)),
                             pl.BlockSpec((K * SUBLANES, tc), lambda j, i: (0, j)),
                             pl.BlockSpec((SUBLANES, tc), lambda j, i: (0, j))],
                  out_shape=[jax.ShapeDtypeStruct((L, C), BF16), jax.ShapeDtypeStruct((K * SUBLANES, C), F32),
                             jax.ShapeDtypeStruct((SUBLANES, C), F32)],
                  compiler_params=_cparams(("parallel", "arbitrary")))(xr, xr, xr, dact, dact, w, b.reshape(1, C))


def _head_selector(R, P, heads_first):
    shape = (R, R * P) if heads_first else (R * P, R)
    head = lax.broadcasted_iota(jnp.int32, shape, 0 if heads_first else 1)
    lane = lax.broadcasted_iota(jnp.int32, shape, 1 if heads_first else 0)
    d = lane - head * P
    return jnp.where((d >= 0) & (d < P), 1.0, 0.0).astype(BF16)


def _ssd_prelude(dtc_ref, dtr_ref, bc_ref, br_ref, ac_ref, ar_ref, Q, R, P):
    raw_c = dtc_ref[...] + bc_ref[...]
    dt_c = _softplus(raw_c)
    dt_r = _softplus(dtr_ref[...] + br_ref[...])
    A_c = -jnp.exp(ac_ref[...])
    acs_c = _tri_mm(_tri(Q), dt_c * A_c)
    acs_r = _tri_mm(_tri(Q, upper=True), dt_r * (-jnp.exp(ar_ref[...])), tri_first=False)
    ea_c = jnp.exp(acs_c)
    dte_c = jnp.exp(acs_c[Q - 1:Q, :] - acs_c)
    wide = _tri_mm(_head_selector(R, P, True), jnp.concatenate([dt_c, ea_c, dte_c], axis=0), tri_first=False)
    return dict(raw_c=raw_c, dt_c=dt_c, A_c=A_c, acs_c=acs_c, acs_r=acs_r, ea_c=ea_c,
                DT=wide[:Q], EA=wide[Q:2 * Q], DTE=wide[2 * Q:])


def _ssd_decay_tile(pre, r, mask):
    return jnp.exp(jnp.where(mask, pre["acs_c"][:, r:r + 1] - pre["acs_r"][r:r + 1, :], NEG))


def _ssd_specs(Q, R, P, N, G, inner, rev=None):
    cc = (lambda c: c) if rev is None else rev
    return dict(
        x=pl.BlockSpec((Q, R * P), lambda g, c: (cc(c), g)),
        b=pl.BlockSpec((Q, N), lambda g, c: (cc(c), inner // N + g)),
        c=pl.BlockSpec((Q, N), lambda g, c: (cc(c), inner // N + G + g)),
        dtc=pl.BlockSpec((None, Q, R), lambda g, c: (g, cc(c), 0)),
        dtr=pl.BlockSpec((None, R, Q), lambda g, c: (g, 0, cc(c))),
        pc=pl.BlockSpec((None, 1, R), lambda g, c: (g, 0, 0)),
        pr=pl.BlockSpec((None, R, 1), lambda g, c: (g, 0, 0)),
        px=pl.BlockSpec((None, 1, R * P), lambda g, c: (g, 0, 0)),
        st=pl.BlockSpec((None, None, N, R * P), lambda g, c: (cc(c), g, 0, 0)))


def _ssd_fwd(act, dtc, dtr, bias_c, bias_r, alog_c, alog_r, dsk_x, name):
    G, L, R = dtc.shape
    N, P, Q = SSM_STATE, SSM_HEAD_DIM, SSM_CHUNK
    RP = R * P
    inner = G * RP
    nc = L // Q

    def body(x_ref, b_ref, c_ref, dtc_ref, dtr_ref, bc_ref, br_ref, ac_ref, ar_ref, dk_ref, y_ref, hp_ref, st):
        c = pl.program_id(1)

        @pl.when(c == 0)
        def _():
            st[...] = jnp.zeros_like(st)

        pre = _ssd_prelude(dtc_ref, dtr_ref, bc_ref, br_ref, ac_ref, ar_ref, Q, R, P)
        X = x_ref[...]
        XT = X * pre["DT"]
        Bb = b_ref[...].astype(BF16)
        Cb = c_ref[...].astype(BF16)
        CB = _dot(Cb, Bb, NT)
        mask = lax.broadcasted_iota(jnp.int32, (Q, Q), 0) >= lax.broadcasted_iota(jnp.int32, (Q, Q), 1)
        low = lax.broadcasted_iota(jnp.int32, (Q, 2 * P), 1) < P
        pieces = []
        for k in range(R // 2):
            xt2 = XT[:, 2 * P * k:2 * P * (k + 1)]
            acc = None
            for half in range(2):
                Gm = CB * _ssd_decay_tile(pre, 2 * k + half, mask)
                part = _dot(Gm.astype(BF16), jnp.where(low == (half == 0), xt2, 0.0).astype(BF16), NN)
                acc = part if acc is None else acc + part
            pieces.append(acc)
        HP = st[...]
        hp_ref[...] = HP
        yoff = pre["EA"] * _dot(Cb, HP.astype(BF16), NN)
        st[...] = HP * pre["EA"][Q - 1:Q, :] + _dot(Bb, (XT * pre["DTE"]).astype(BF16), TN)
        y_ref[...] = jnp.concatenate(pieces, axis=1) + yoff + dk_ref[...] * X

    sp = _ssd_specs(Q, R, P, N, G, inner)
    return _pcall(body, name=name, grid=(G, nc),
                  in_specs=[sp["x"], sp["b"], sp["c"], sp["dtc"], sp["dtr"], sp["pc"], sp["pr"], sp["pc"], sp["pr"], sp["px"]],
                  out_specs=[sp["x"], sp["st"]],
                  out_shape=[jax.ShapeDtypeStruct((L, inner), F32), jax.ShapeDtypeStruct((nc, G, N, RP), F32)],
                  scratch_shapes=[pltpu.VMEM((N, RP), F32)],
                  compiler_params=_cparams(("parallel", "arbitrary")))(act, act, act, dtc, dtr, bias_c, bias_r, alog_c, alog_r, dsk_x)


def _ssd_bwd(act, dtc, dtr, bias_c, bias_r, alog_c, alog_r, dsk_x, hprev, dy, name):
    G, L, R = dtc.shape
    N, P, Q = SSM_STATE, SSM_HEAD_DIM, SSM_CHUNK
    RP = R * P
    inner = G * RP
    nc = L // Q

    def body(x_ref, b_ref, c_ref, dtc_ref, dtr_ref, bc_ref, br_ref, ac_ref, ar_ref, dk_ref, hp_ref, dy_ref,
             dx_ref, db_ref, dc_ref, ddt_ref, gbias_ref, galog_ref, gdsk_ref, dst):
        c = pl.program_id(1)

        @pl.when(c == 0)
        def _():
            dst[...] = jnp.zeros_like(dst)
            gbias_ref[...] = jnp.zeros_like(gbias_ref)
            galog_ref[...] = jnp.zeros_like(galog_ref)
            gdsk_ref[...] = jnp.zeros_like(gdsk_ref)

        pre = _ssd_prelude(dtc_ref, dtr_ref, bc_ref, br_ref, ac_ref, ar_ref, Q, R, P)
        DT, EA, DTE = pre["DT"], pre["EA"], pre["DTE"]
        E_END = EA[Q - 1:Q, :]
        X, DY = x_ref[...], dy_ref[...]
        XT = X * DT
        Bb = b_ref[...].astype(BF16)
        Cb = c_ref[...].astype(BF16)
        CB = _dot(Cb, Bb, NT)
        HP, dH = hp_ref[...], dst[...]
        HPb, dHb = HP.astype(BF16), dH.astype(BF16)
        EDY = EA * DY
        EDYb = EDY.astype(BF16)
        dC = _dot(EDYb, HPb, NT)
        dHP = _dot(Cb, EDYb, TN)
        da_off = EDY * _dot(Cb, HPb, NN)
        Z = _dot(Bb, dHb, NN)
        XD = XT * DTE
        dB = _dot(XD.astype(BF16), dHb, NT)
        dXT = DTE * Z
        t_x = XD * Z
        hh = jnp.sum(dH * HP, axis=0, keepdims=True) * E_END
        dst[...] = dHP + dH * E_END
        mask = lax.broadcasted_iota(jnp.int32, (Q, Q), 0) >= lax.broadcasted_iota(jnp.int32, (Q, Q), 1)
        eye = lax.broadcasted_iota(jnp.int32, (Q, Q), 0) == lax.broadcasted_iota(jnp.int32, (Q, Q), 1)
        low = lax.broadcasted_iota(jnp.int32, (Q, 2 * P), 1) < P
        lane = lax.broadcasted_iota(jnp.int32, (Q, R), 1)
        dCB = jnp.zeros((Q, Q), F32)
        da_mat = jnp.zeros((Q, R), F32)
        pieces = []
        for k in range(R // 2):
            sl = slice(2 * P * k, 2 * P * (k + 1))
            xt2, dy2 = XT[:, sl], DY[:, sl]
            acc = None
            for half in range(2):
                r = 2 * k + half
                sel = low == (half == 0)
                Lm = _ssd_decay_tile(pre, r, mask)
                Gm = CB * Lm
                dyb = jnp.where(sel, dy2, 0.0).astype(BF16)
                part = _dot(Gm.astype(BF16), dyb, TN)
                acc = part if acc is None else acc + part
                dG = jnp.where(mask, _dot(dyb, jnp.where(sel, xt2, 0.0).astype(BF16), NT), 0.0)
                Mm = dG * Gm
                dCB = dCB + dG * Lm
                colsum = jnp.sum(jnp.where(eye, jnp.sum(Mm, axis=0, keepdims=True), 0.0), axis=1, keepdims=True)
                da_mat = jnp.where(lane == r, jnp.sum(Mm, axis=1, keepdims=True) - colsum, da_mat)
            pieces.append(acc)
        dXT = dXT + jnp.concatenate(pieces, axis=1)
        dCBb = dCB.astype(BF16)
        dc_ref[...] = dC + _dot(dCBb, Bb, NN)
        db_ref[...] = dB + _dot(dCBb, Cb, TN)
        dx_ref[...] = dXT * DT + dk_ref[...] * DY
        pad = jnp.zeros((SUBLANES - 1, RP), F32)
        sums = _tri_mm(_head_selector(R, P, False), jnp.concatenate([da_off, t_x, dXT * X, DY * X, hh, pad], axis=0), tri_first=False)
        t = sums[Q:2 * Q]
        da_end = jnp.sum(t, axis=0, keepdims=True) + sums[4 * Q:4 * Q + 1]
        rowi = lax.broadcasted_iota(jnp.int32, (Q, R), 0)
        da_mat = da_mat + sums[:Q] - t + jnp.where(rowi == Q - 1, da_end, 0.0)
        ddtA = _tri_mm(_tri(Q, upper=True), da_mat)
        ddt_raw = (ddtA * pre["A_c"] + sums[2 * Q:3 * Q]) * _sigmoid(pre["raw_c"])
        ddt_ref[...] = ddt_raw
        gbias_ref[...] += jnp.sum(ddt_raw, axis=0, keepdims=True)
        galog_ref[...] += jnp.sum(ddtA * pre["dt_c"], axis=0, keepdims=True) * pre["A_c"]
        gdsk_ref[...] += jnp.sum(sums[3 * Q:4 * Q], axis=0, keepdims=True)

    sp = _ssd_specs(Q, R, P, N, G, inner, rev=lambda c: nc - 1 - c)
    bout = pl.BlockSpec((Q, N), lambda g, c: (nc - 1 - c, g))
    return _pcall(body, name=name, grid=(G, nc),
                  in_specs=[sp["x"], sp["b"], sp["c"], sp["dtc"], sp["dtr"], sp["pc"], sp["pr"], sp["pc"], sp["pr"], sp["px"],
                            sp["st"], sp["x"]],
                  out_specs=[sp["x"], bout, bout, sp["dtc"], sp["pc"], sp["pc"], sp["pc"]],
                  out_shape=[jax.ShapeDtypeStruct((L, inner), F32), jax.ShapeDtypeStruct((L, G * N), F32),
                             jax.ShapeDtypeStruct((L, G * N), F32), jax.ShapeDtypeStruct((G, L, R), F32),
                             jax.ShapeDtypeStruct((G, 1, R), F32), jax.ShapeDtypeStruct((G, 1, R), F32),
                             jax.ShapeDtypeStruct((G, 1, R), F32)],
                  scratch_shapes=[pltpu.VMEM((N, RP), F32)],
                  compiler_params=_cparams(("parallel", "arbitrary")))(
        act, act, act, dtc, dtr, bias_c, bias_r, alog_c, alog_r, dsk_x, hprev, dy)


def _gnorm_fwd(y, z, g, name):
    L, Dn = y.shape
    gs = Dn // SSM_GROUPS
    tl = _row_tile(L)

    def body(y_ref, z_ref, g_ref, o_ref):
        for k in range(SSM_GROUPS):
            sl = slice(k * gs, (k + 1) * gs)
            zz = z_ref[:, sl]
            u = y_ref[:, sl] * zz * _sigmoid(zz)
            rstd = lax.rsqrt(jnp.mean(u * u, axis=-1, keepdims=True) + RMS_EPS)
            o_ref[:, sl] = (u * rstd * g_ref[:, sl]).astype(BF16)

    row = pl.BlockSpec((tl, Dn), lambda i: (i, 0))
    return _pcall(body, name=name, grid=(L // tl,), in_specs=[row, row, pl.BlockSpec((1, Dn), lambda i: (0, 0))],
                  out_specs=row, out_shape=jax.ShapeDtypeStruct((L, Dn), BF16),
                  compiler_params=_cparams(("parallel",)))(y, z, g.reshape(1, Dn))


def _gnorm_bwd(y, z, g, dout, name):
    L, Dn = y.shape
    gs = Dn // SSM_GROUPS
    tl = _row_tile(L)

    def body(y_ref, z_ref, g_ref, d_ref, dy_ref, dz_ref, dg_ref):
        i = pl.program_id(0)

        @pl.when(i == 0)
        def _():
            dg_ref[...] = jnp.zeros_like(dg_ref)

        for k in range(SSM_GROUPS):
            sl = slice(k * gs, (k + 1) * gs)
            zz = z_ref[:, sl]
            yy = y_ref[:, sl]
            sg = _sigmoid(zz)
            sil = zz * sg
            u = yy * sil
            rstd = lax.rsqrt(jnp.mean(u * u, axis=-1, keepdims=True) + RMS_EPS)
            n = u * rstd
            d = d_ref[:, sl]
            dn = d * g_ref[:, sl]
            du = rstd * (dn - n * jnp.mean(dn * n, axis=-1, keepdims=True))
            dy_ref[:, sl] = du * sil
            dz_ref[:, sl] = (du * yy * sg * (1.0 + zz * (1.0 - sg))).astype(BF16)
            dg_ref[:, sl] += _sum8(d * n)

    row = pl.BlockSpec((tl, Dn), lambda i: (i, 0))
    return _pcall(body, name=name, grid=(L // tl,), in_specs=[row, row, pl.BlockSpec((1, Dn), lambda i: (0, 0)), row],
                  out_specs=[row, row, pl.BlockSpec((SUBLANES, Dn), lambda i: (0, 0))],
                  out_shape=[jax.ShapeDtypeStruct((L, Dn), F32), jax.ShapeDtypeStruct((L, Dn), BF16),
                             jax.ShapeDtypeStruct((SUBLANES, Dn), F32)],
                  compiler_params=_cparams(("arbitrary",)))(y, z, g.reshape(1, Dn), dout)


def _adamw(w, g, m, v, name):
    rows, W = w.shape
    tr = _pick(rows, (512, 256, 128, 64, 32, 16, 8))
    c1 = 1.0 / (1.0 - ADAM_B1 ** ADAM_STEP)
    c2 = 1.0 / (1.0 - ADAM_B2 ** ADAM_STEP)

    def body(w_ref, g_ref, m_ref, v_ref, d_ref, nm_ref, nv_ref):
        g_ = g_ref[...]
        nm = ADAM_B1 * m_ref[...] + (1.0 - ADAM_B1) * g_
        nv = ADAM_B2 * v_ref[...] + (1.0 - ADAM_B2) * (g_ * g_)
        nm_ref[...] = nm
        nv_ref[...] = nv
        d_ref[...] = -ADAM_LR * ((nm * c1) / (jnp.sqrt(nv * c2) + ADAM_EPS) + ADAM_WD * w_ref[...])

    blk = pl.BlockSpec((tr, W), lambda i: (i, 0))
    return _pcall(body, name=name, grid=(rows // tr,), in_specs=[blk] * 4, out_specs=[blk] * 3,
                  out_shape=[jax.ShapeDtypeStruct((rows, W), F32)] * 3, compiler_params=_cparams(("parallel",)))(w, g, m, v)


def _sum_slots(x, name, extra=None):
    n, rows, W = x.shape
    tr = _pick(rows, (512, 256, 128, 64, 32, 16, 8))
    has_extra = extra is not None

    def body(*refs):
        if has_extra:
            e_ref, x_ref, o_ref = refs
            acc = e_ref[...].astype(F32)
            start = 0
        else:
            x_ref, o_ref = refs
            acc = x_ref[0].astype(F32)
            start = 1
        for s in range(start, n):
            acc = acc + x_ref[s].astype(F32)
        o_ref[...] = acc

    blk = pl.BlockSpec((tr, W), lambda i: (i, 0))
    xblk = pl.BlockSpec((n, tr, W), lambda i: (0, i, 0))
    return _pcall(body, name=name, grid=(rows // tr,), in_specs=([blk] if has_extra else []) + [xblk], out_specs=blk,
                  out_shape=jax.ShapeDtypeStruct((rows, W), F32), compiler_params=_cparams(("parallel",)))(
        *(([extra] if has_extra else []) + [x]))


def _add_pairs(a, b, name):
    n, rows, W = a.shape
    tr = _pick(rows, (512, 256, 128, 64, 32, 16, 8))

    def body(a_ref, b_ref, o_ref):
        o_ref[...] = (a_ref[...].astype(F32) + b_ref[...].astype(F32)).astype(BF16)

    blk = pl.BlockSpec((None, tr, W), lambda s, i: (s, i, 0))
    return _pcall(body, name=name, grid=(n, rows // tr), in_specs=[blk, blk], out_specs=blk,
                  out_shape=jax.ShapeDtypeStruct((n, rows, W), BF16), compiler_params=_cparams(("parallel", "parallel")))(a, b)


MESH = pl.DeviceIdType.MESH
HBM_SPEC = pl.BlockSpec(memory_space=pl.ANY)


def _me():
    return lax.axis_index("x"), lax.axis_index("y"), lax.axis_index("c")


def _all_gather(arrs, name):
    n = len(arrs)

    def body(*refs):
        start, forward, finish = _gather_phases(refs[:n], refs[n:2 * n], *refs[2 * n:])
        start()
        forward()
        finish()

    return _pcall(body, name=name, in_specs=[HBM_SPEC] * n, out_specs=[HBM_SPEC] * n,
                  out_shape=_gather_shapes(arrs), scratch_shapes=_gather_sems(n))(*arrs)


def _gather_shapes(arrs):
    return [jax.ShapeDtypeStruct((N_DEV,) + a.shape, a.dtype) for a in arrs]


def _gather_sems(n):
    return [pltpu.SemaphoreType.DMA((7 * n,)), pltpu.SemaphoreType.DMA((7 * n,)), pltpu.SemaphoreType.DMA((n,))]


def _gather_phases(ins, outs, send_sems, recv_sems, local_sems):
    n = len(ins)
    x, y, c = _me()
    me, sib = (x, y, c), (x, y, 1 - c)
    chips = [(1 - x, y), (x, 1 - y), (1 - x, 1 - y)]

    def slot(a, dev):
        return outs[a].at[4 * dev[0] + 2 * dev[1] + dev[2]]

    def copy(a, k, block, to, src=None):
        return pltpu.make_async_remote_copy(src_ref=slot(a, block) if src is None else src, dst_ref=slot(a, block),
                                            send_sem=send_sems.at[a * 7 + k], recv_sem=recv_sems.at[a * 7 + k],
                                            device_id=to, device_id_type=MESH)

    def mine():
        return [pltpu.make_async_copy(ins[a], slot(a, me), local_sems.at[a]) for a in range(n)]

    def first():
        out = []
        for a in range(n):
            out.append(copy(a, 0, me, sib, src=ins[a]))
            out += [copy(a, 1 + j, me, (*chip, c), src=ins[a]) for j, chip in enumerate(chips)]
        return out

    def passed():
        return [copy(a, 4 + j, (*chip, c), sib) for j, chip in enumerate(chips) for a in range(n)]

    def start():
        for cp in mine() + first():
            cp.start()

    def forward():
        fws = passed()
        for j, chip in enumerate(chips):
            for a in range(n):
                copy(a, 1 + j, (*chip, c), me).wait_recv()
                fws[j * n + a].start()

    def finish():
        for a in range(n):
            copy(a, 0, sib, me).wait_recv()
            for j, chip in enumerate(chips):
                copy(a, 4 + j, (*chip, 1 - c), me).wait_recv()
        for cp in first() + passed():
            cp.wait_send()
        for cp in mine():
            cp.wait()

    return start, forward, finish


def _exchange_phases(gs, outs, send_sems, recv_sems, local_sems):
    n = len(gs)
    x, y, c = _me()
    my_slot = 4 * x + 2 * y + c
    flips = [(fx, fy, fc) for fx in (0, 1) for fy in (0, 1) for fc in (0, 1) if fx or fy or fc]

    def peer(f):
        return tuple(1 - v if flip else v for v, flip in zip((x, y, c), f))

    def copies():
        out = []
        for a in range(n):
            for k, f in enumerate(flips):
                px, py, pc = peer(f)
                out.append(pltpu.make_async_remote_copy(
                    src_ref=gs[a].at[4 * px + 2 * py + pc], dst_ref=outs[a].at[my_slot],
                    send_sem=send_sems.at[a * 7 + k], recv_sem=recv_sems.at[a * 7 + k],
                    device_id=(px, py, pc), device_id_type=MESH))
        return out

    def arrivals():
        out = []
        for a in range(n):
            for k, f in enumerate(flips):
                px, py, pc = peer(f)
                slot = outs[a].at[4 * px + 2 * py + pc]
                out.append(pltpu.make_async_remote_copy(src_ref=slot, dst_ref=slot, send_sem=send_sems.at[a * 7 + k],
                                                        recv_sem=recv_sems.at[a * 7 + k], device_id=(px, py, pc),
                                                        device_id_type=MESH))
        return out

    def mine():
        return [pltpu.make_async_copy(gs[a].at[my_slot], outs[a].at[my_slot], local_sems.at[a]) for a in range(n)]

    def start():
        for cp in mine() + copies():
            cp.start()

    def finish():
        for cp in arrivals():
            cp.wait_recv()
        for cp in copies():
            cp.wait_send()
        for cp in mine():
            cp.wait()

    return start, finish


def _rs_sibling(gs, name):
    n = len(gs)

    def body(*refs):
        g_refs, o_refs = refs[:n], refs[n:2 * n]
        send_sems, recv_sems = refs[2 * n:]
        x, y, c = _me()
        sib = (x, y, 1 - c)
        cps = [pltpu.make_async_remote_copy(src_ref=g_refs[a].at[2 * q + (1 - c)], dst_ref=o_refs[a].at[q],
                                            send_sem=send_sems.at[4 * a + q], recv_sem=recv_sems.at[4 * a + q],
                                            device_id=sib, device_id_type=MESH) for a in range(n) for q in range(4)]
        for cp in cps:
            cp.start()
        for cp in cps:
            cp.wait()

    return _pcall(body, name=name, in_specs=[HBM_SPEC] * n, out_specs=[HBM_SPEC] * n,
                  out_shape=[jax.ShapeDtypeStruct((4,) + g.shape[1:], g.dtype) for g in gs],
                  scratch_shapes=[pltpu.SemaphoreType.DMA((4 * n,)), pltpu.SemaphoreType.DMA((4 * n,))])(*gs)


def _rs_chips(ps, name):
    n = len(ps)

    def body(*refs):
        p_refs, o_refs = refs[:n], refs[n:2 * n]
        send_sems, recv_sems = refs[2 * n:]
        x, y, c = _me()
        chips = [(1 - x, y), (x, 1 - y), (1 - x, 1 - y)]
        cps = [pltpu.make_async_remote_copy(src_ref=p_refs[a].at[2 * chip[0] + chip[1]], dst_ref=o_refs[a].at[j],
                                            send_sem=send_sems.at[3 * a + j], recv_sem=recv_sems.at[3 * a + j],
                                            device_id=(*chip, c), device_id_type=MESH)
               for j, chip in enumerate(chips) for a in range(n)]
        for cp in cps:
            cp.start()
        for cp in cps:
            cp.wait()

    return _pcall(body, name=name, in_specs=[HBM_SPEC] * n, out_specs=[HBM_SPEC] * n,
                  out_shape=[jax.ShapeDtypeStruct((3,) + p.shape[1:], p.dtype) for p in ps],
                  scratch_shapes=[pltpu.SemaphoreType.DMA((3 * n,)), pltpu.SemaphoreType.DMA((3 * n,))])(*ps)


def _reduce_scatter(gs, name):
    x, y, c = _me()
    from_sib = _rs_sibling(gs, name + "_sib")
    pairs = []
    for a, (g, fs) in enumerate(zip(gs, from_sib)):
        own = g.reshape((4, 2) + g.shape[1:])
        pairs.append(_add_pairs(jnp.where(c == 0, own[:, 0], own[:, 1]), fs, f"{name}_pair{a}"))
    from_chips = _rs_chips(pairs, name + "_chips")
    return [_sum_slots(fc, f"{name}_sum{a}", extra=lax.dynamic_index_in_dim(p, 2 * x + y, axis=0, keepdims=False))
            for a, (p, fc) in enumerate(zip(pairs, from_chips))]


BIG = ("even_w_in", "even_w_out", "odd_w_in", "odd_w_out", "ffn_w_up", "ffn_w_down", "ple_w_proj", "ple_w_gate")
SMALL_SHARDED = ("even_conv_w", "odd_conv_w", "odd_conv_b", "odd_norm_g", "ffn_conv_w")
REPLICATED = ("even_b_f", "odd_dt_bias", "odd_a_log", "odd_d_skip", "ln_mix_g", "ln_mix_b", "ffn_conv_b",
              "ln_ffn_g", "ln_ffn_b", "ple_b_gate")
WEIGHTS = ("even_w_in", "even_b_f", "even_conv_w", "even_w_out", "odd_w_in", "odd_conv_w", "odd_conv_b", "odd_dt_bias",
           "odd_a_log", "odd_d_skip", "odd_norm_g", "odd_w_out", "ln_mix_g", "ln_mix_b", "ffn_w_up", "ffn_conv_w",
           "ffn_conv_b", "ffn_w_down", "ln_ffn_g", "ln_ffn_b", "ple_w_proj", "ple_w_gate", "ple_b_gate")


def _full_shapes():
    d = _dims()
    return {
        "even_w_in": ((1, D_MODEL, d["even_in"]), 2), "even_b_f": ((1, FOX_HEADS), None),
        "even_conv_w": ((1, CONV_WIDTH, CONV_DIM), 2), "even_w_out": ((1, d["even_mix"], D_MODEL), 1),
        "odd_w_in": ((1, D_MODEL, d["odd_in"]), 2), "odd_conv_w": ((1, SSM_CONV_WIDTH, d["conv_ch"]), 2),
        "odd_conv_b": ((1, d["conv_ch"]), 1), "odd_dt_bias": ((1, d["ssm_heads"]), None),
        "odd_a_log": ((1, d["ssm_heads"]), None), "odd_d_skip": ((1, d["ssm_heads"]), None),
        "odd_norm_g": ((1, d["ssm_inner"]), 1), "odd_w_out": ((1, d["ssm_inner"], D_MODEL), 1),
        "ln_mix_g": ((DEPTH, D_MODEL), None), "ln_mix_b": ((DEPTH, D_MODEL), None),
        "ffn_w_up": ((DEPTH, D_MODEL, 2 * D_FF), 2), "ffn_conv_w": ((DEPTH, FFN_CONV_WIDTH, 2 * D_FF), 2),
        "ffn_conv_b": ((DEPTH, 2 * D_FF), None), "ffn_w_down": ((DEPTH, D_FF, D_MODEL), 1),
        "ln_ffn_g": ((DEPTH, D_MODEL), None), "ln_ffn_b": ((DEPTH, D_MODEL), None),
        "ple_w_proj": ((DEPTH, PLE_DIM, D_MODEL), 2), "ple_w_gate": ((DEPTH, D_MODEL, D_MODEL), 1),
        "ple_b_gate": ((DEPTH, D_MODEL), None),
    }


def _shard_shape(name):
    shape, ax = _full_shapes()[name]
    if ax is None:
        return shape
    return tuple(s // N_DEV if i == ax else s for i, s in enumerate(shape))


def _as2d(a, lead=0):
    return a.reshape(a.shape[:lead] + (-1, a.shape[-1]))


def _part_rows(shape):
    n = int(np.prod(shape))
    return -(-(-(-n // PACK_W)) // SUBLANES) * SUBLANES


def _pack_small(parts):
    out = []
    for p in parts:
        n, rows = int(np.prod(p.shape)), _part_rows(p.shape)
        out.append(jnp.pad(p.reshape(-1).astype(F32), (0, rows * PACK_W - n)).reshape(rows, PACK_W))
    return jnp.concatenate(out, axis=0)


def _unpack_small(pack, shapes):
    lead = pack.shape[:-2]
    out, off = [], 0
    for s in shapes:
        n, rows = int(np.prod(s)), _part_rows(s)
        part = pack[..., off:off + rows, :].reshape(lead + (-1,))[..., :n]
        out.append(part.reshape(lead + tuple(s)))
        off += rows
    return out


def _assemble(gathered, name):
    shape, ax = _full_shapes()[name]
    return jnp.moveaxis(gathered, 0, ax).reshape(shape)


def _split_dest(full, name):
    shape, ax = _full_shapes()[name]
    sh = shape[:ax] + (N_DEV, shape[ax] // N_DEV) + shape[ax + 1:]
    return jnp.moveaxis(full.reshape(sh), ax, 0)


def _interleave_cols(w, parts, tc):
    C = w.shape[-1] // parts
    sh = w.shape[:-1]
    return w.reshape(sh + (parts, C // tc, tc)).swapaxes(-3, -2).reshape(sh + (parts * C,))


def _deinterleave_cols(w, parts, tc):
    C = w.shape[-1] // parts
    sh = w.shape[:-1]
    return w.reshape(sh + (C // tc, parts, tc)).swapaxes(-3, -2).reshape(sh + (parts * C,))


def _pad_cols(a, to):
    return jnp.pad(a, ((0, 0), (0, to - a.shape[1])))


def _tail_fwd(i, h_in, mix, p_i, W, sp):
    r1, h1, h1b = _ln_fwd(h_in, mix, sp["ln_mix_g"][i], sp["ln_mix_b"][i], f"ln_mix_fwd{i}")
    U = _mm(h1b, W["ffn_up"][i], "nn", F32, f"ffn_up{i}")
    S = _ffn_act_fwd(U, sp["ffn_conv_w"][i], sp["ffn_conv_b"][i], f"ffn_act_fwd{i}")
    ffn = _mm(S, W["ffn_down"][i], "nn", F32, f"ffn_down{i}")
    r2, h2, h2b = _ln_fwd(h1, ffn, sp["ln_ffn_g"][i], sp["ln_ffn_b"][i], f"ln_ffn_fwd{i}")
    G = _mm(h2b, W["ple_gate"][i], "nn", F32, f"ple_gate{i}")
    E = _mm(p_i, W["ple_proj"][i], "nn", F32, f"ple_proj{i}")
    h3, h3b = _ple_fwd(h2, G, sp["ple_b_gate"][i], E, f"ple_fwd{i}")
    return h3, h3b, dict(r1=r1, h1b=h1b, U=U, S=S, r2=r2, h2b=h2b, G=G, E=E, p=p_i)


def _tail_bwd(i, dh3, sv, W, sp, grads):
    alpha = _alpha()
    dE, dGp, dbg = _ple_bwd(dh3, sv["G"], sp["ple_b_gate"][i], sv["E"], f"ple_bwd{i}")
    grads["ple_b_gate"][i] = dbg.sum(0)
    grads["ple_w_proj"][i] = _mm(sv["p"], dE, "tn", F32, f"d_ple_proj{i}")
    grads["ple_w_gate"][i] = _mm(sv["h2b"], dGp, "tn", F32, f"d_ple_gate{i}")
    dh2 = _mm(dGp, W["ple_gate"][i], "nt", F32, f"dx_ple_gate{i}", add=dh3)
    dr2, dr2b, dg, db = _ln_bwd(sv["r2"], dh2, sp["ln_ffn_g"][i], f"ln_ffn_bwd{i}")
    grads["ln_ffn_g"][i], grads["ln_ffn_b"][i] = dg.sum(0), db.sum(0)
    grads["ffn_w_down"][i] = _mm(sv["S"], dr2b, "tn", F32, f"d_ffn_down{i}")
    dS = _mm(dr2b, W["ffn_down"][i], "nt", BF16, f"dx_ffn_down{i}")
    dUg, dUv, dwg, dwv, dbg, dbv = _ffn_act_bwd(sv["U"], dS, sp["ffn_conv_w"][i], sp["ffn_conv_b"][i], f"ffn_act_bwd{i}")
    K = FFN_CONV_WIDTH
    grads["ffn_conv_w"][i] = jnp.concatenate([dwg.reshape(K, SUBLANES, -1).sum(1), dwv.reshape(K, SUBLANES, -1).sum(1)], axis=1)
    grads["ffn_conv_b"][i] = jnp.concatenate([dbg.sum(0), dbv.sum(0)])
    grads["ffn_w_up"][i] = jnp.concatenate([_mm(sv["h1b"], dUg, "tn", F32, f"d_ffn_up_g{i}"),
                                            _mm(sv["h1b"], dUv, "tn", F32, f"d_ffn_up_v{i}")], axis=1)
    dh1 = _mm(dUg, W["ffn_up"][i], "nt", F32, f"dx_ffn_up_g{i}", add=dr2, add_scale=alpha)
    dh1 = _mm(dUv, W["ffn_up"][i], "nt", F32, f"dx_ffn_up_v{i}", add=dh1, b_k_start=D_FF)
    dr1, dr1b, dg, db = _ln_bwd(sv["r1"], dh1, sp["ln_mix_g"][i], f"ln_mix_bwd{i}")
    grads["ln_mix_g"][i], grads["ln_mix_b"][i] = dg.sum(0), db.sum(0)
    return dr1, dr1b


def _even_fwd(h, W, sp, full, gather):
    L = h.shape[0]
    H, Dh = FOX_HEADS, FOX_HEAD_DIM
    Ac = _mm(h, W["even_in_conv"], "nn", F32, "even_in_conv")
    qkv = _mm(h, W["even_in_qkv"], "nn", BF16, "even_in_qkv")
    Af = _mm(h, W["even_in_f"], "nn", F32, "even_in_f")
    y_a = _sconv_fwd(Ac, sp["even_conv_w_il"], "sconv_fwd")
    Fc = _fox_gate_fwd(Af, sp["even_b_f_pad"], "fox_gate_fwd")
    Fh = Fc[:, :H].T
    Fq, Fk = Fh.reshape(H, L, 1), Fh.reshape(H, 1, L)
    o, lse, *gathered = _attn_fwd(qkv, Fq, Fk, "attn_fwd", gather)
    for (n, i), g in zip(_late_units(), gathered):
        full[n][i] = _assemble_unit(g, n)
    _prepare_late(W, full)
    Y = jnp.concatenate([y_a, o], axis=1)
    mix = _mm(Y, W["even_out"], "nn", F32, "even_out")
    return mix, dict(h=h, Ac=Ac, Af=Af, qkv=qkv, Fq=Fq, Fk=Fk, lse=lse, Y=Y)


def _even_bwd(dmix, dres, sv, W, sp, grads, exchange):
    H, Dh = FOX_HEADS, FOX_HEAD_DIM
    C = CONV_DIM
    L = dmix.shape[0]
    grads["even_w_out"][0] = _mm(sv["Y"], dmix, "tn", F32, "d_even_out")
    exchange = [_split_unit(grads["even_w_out"][0], "even_w_out").astype(BF16)] + list(exchange)
    dY = _mm(dmix, W["even_out"], "nt", F32, "dx_even_out")
    dya = dY[:, :C]
    do = dY[:, C:].astype(BF16)
    dc, dcw = _sconv_bwd_dc(sv["Ac"], dya, sp["even_conv_w_il"], "sconv_bwd_dc")
    grads["even_conv_w"][0] = dcw.reshape(CONV_WIDTH, SUBLANES, -1).sum(1)
    dAc = _sconv_bwd_da(sv["Ac"], dya, dc, sp["even_conv_w_il"], "sconv_bwd_da")
    dq, dk, dv, dFk, *arrived = _attn_bwd(sv["qkv"], sv["Fq"], sv["Fk"], sv["lse"], do, "attn_bwd", exchange)
    dqkv = jnp.concatenate([dq, dk.astype(BF16), dv.astype(BF16)], axis=1)
    dF = _pad_cols(dFk.reshape(H, L).T, LANES)
    dAf, dbf = _fox_gate_bwd(sv["Af"], sp["even_b_f_pad"], dF, "fox_gate_bwd")
    grads["even_b_f"][0] = dbf.sum(0)[:H]
    h = sv["h"]
    gc = _deinterleave_cols(_mm(h, dAc, "tn", F32, "d_even_in_conv"), 3, LANES)
    gq = _mm(h, dqkv, "tn", F32, "d_even_in_qkv")
    gf = _mm(h, dAf, "tn", F32, "d_even_in_f")[:, :H]
    grads["even_w_in"][0] = jnp.concatenate([gc, gq, gf], axis=1)
    dh = _mm(dAc, W["even_in_conv"], "nt", F32, "dx_even_in_conv", add=dres, add_scale=_alpha())
    dh = _mm(dqkv, W["even_in_qkv"], "nt", F32, "dx_even_in_qkv", add=dh)
    dh = _mm(dAf, W["even_in_f"], "nt", F32, "dx_even_in_f", add=dh)
    return dh, arrived


def _group_layouts(v, G):
    R = v.shape[0] // G
    return v.reshape(G, 1, R), v.reshape(G, R, 1)


def _odd_fwd(h, W, sp):
    d = _dims()
    L = h.shape[0]
    Hs, G, N, P = d["ssm_heads"], SSM_GROUPS, SSM_STATE, SSM_HEAD_DIM
    R = Hs // G
    inner = d["ssm_inner"]
    z = _mm(h, W["odd_in_z"], "nn", F32, "odd_in_z")
    xr = _mm(h, W["odd_in_x"], "nn", F32, "odd_in_x")
    dtp = _mm(h, W["odd_in_dt"], "nn", F32, "odd_in_dt")
    act = _mconv_fwd(xr, sp["odd_conv_w"], sp["odd_conv_b"], "mconv_fwd")
    dtg = dtp[:, :Hs].reshape(L, G, R)
    dtc, dtr = dtg.transpose(1, 0, 2), dtg.transpose(1, 2, 0)
    dsk_x = jnp.repeat(sp["odd_d_skip"], P).reshape(G, 1, R * P)
    ssd_in = (act, dtc, dtr) + _group_layouts(sp["odd_dt_bias"], G) + _group_layouts(sp["odd_a_log"], G) + (dsk_x,)
    y, hprev = _ssd_fwd(*ssd_in, "ssd_fwd")
    u = _gnorm_fwd(y, z, sp["odd_norm_g"], "gnorm_fwd")
    mix = _mm(u, W["odd_out"], "nn", F32, "odd_out")
    return mix, dict(h=h, z=z, xr=xr, ssd_in=ssd_in, hprev=hprev, y=y, u=u)


def _odd_bwd(dmix, dres, sv, W, sp, grads):
    d = _dims()
    L = dmix.shape[0]
    Hs, G, N, P = d["ssm_heads"], SSM_GROUPS, SSM_STATE, SSM_HEAD_DIM
    grads["odd_w_out"][0] = _mm(sv["u"], dmix, "tn", F32, "d_odd_out")
    du = _mm(dmix, W["odd_out"], "nt", F32, "dx_odd_out")
    dy, dz, dg = _gnorm_bwd(sv["y"], sv["z"], sp["odd_norm_g"], du, "gnorm_bwd")
    grads["odd_norm_g"][0] = dg.sum(0)
    dxs, dB, dC, ddt, gbias, galog, gdsk = _ssd_bwd(*sv["ssd_in"], sv["hprev"], dy, "ssd_bwd")
    grads["odd_dt_bias"][0] = gbias.reshape(Hs)
    grads["odd_a_log"][0] = galog.reshape(Hs)
    grads["odd_d_skip"][0] = gdsk.reshape(Hs)
    dact = jnp.concatenate([dxs, dB, dC], axis=1)
    dxr, dcw, dcb = _mconv_bwd(sv["xr"], dact, sp["odd_conv_w"], sp["odd_conv_b"], "mconv_bwd")
    grads["odd_conv_w"][0] = dcw.reshape(SSM_CONV_WIDTH, SUBLANES, -1).sum(1)
    grads["odd_conv_b"][0] = dcb.sum(0)
    ddtp = _pad_cols(ddt.transpose(1, 0, 2).reshape(L, Hs), W["odd_in_dt"].shape[1])
    h = sv["h"]
    gz = _mm(h, dz, "tn", F32, "d_odd_in_z")
    gx = _mm(h, dxr, "tn", F32, "d_odd_in_x")
    gdt = _mm(h, ddtp, "tn", F32, "d_odd_in_dt")[:, :Hs]
    grads["odd_w_in"][0] = jnp.concatenate([gz, gx, gdt], axis=1)
    dh = _mm(dz, W["odd_in_z"], "nt", F32, "dx_odd_in_z", add=dres, add_scale=_alpha())
    dh = _mm(dxr, W["odd_in_x"], "nt", F32, "dx_odd_in_x", add=dh)
    dh = _mm(ddtp, W["odd_in_dt"], "nt", F32, "dx_odd_in_dt", add=dh)
    return dh


FIRST_UNIT = ("even_w_in", 0)


def _late_units():
    return [(n, i) for n in BIG for i in range(_full_shapes()[n][0][0]) if (n, i) != FIRST_UNIT]


def _assemble_unit(gathered, name):
    shape, ax = _full_shapes()[name]
    g = gathered.reshape((N_DEV,) + _shard_shape(name)[1:])
    return jnp.moveaxis(g, 0, ax - 1).reshape(shape[1:])


def _split_unit(full_layer, name):
    shape, ax = _full_shapes()[name]
    sh = shape[1:ax] + (N_DEV, shape[ax] // N_DEV) + shape[ax + 1:]
    return _as2d(jnp.moveaxis(full_layer.reshape(sh), ax - 1, 0), 1)


def _prepare_first(W, full):
    C, fd = CONV_DIM, _dims()["fox_dim"]
    ew = full["even_w_in"][0]
    W["even_in_conv"] = _interleave_cols(ew[:, :3 * C], 3, LANES)
    W["even_in_qkv"] = ew[:, 3 * C:3 * C + 3 * fd]
    W["even_in_f"] = _pad_cols(ew[:, 3 * C + 3 * fd:], LANES)


def _prepare_late(W, full):
    d = _dims()
    W["even_out"] = full["even_w_out"][0]
    ow = full["odd_w_in"][0]
    inner, cch, Hs = d["ssm_inner"], d["conv_ch"], d["ssm_heads"]
    W["odd_in_z"] = ow[:, :inner]
    W["odd_in_x"] = ow[:, inner:inner + cch]
    W["odd_in_dt"] = _pad_cols(ow[:, inner + cch:], -(-Hs // LANES) * LANES)
    W["odd_out"] = full["odd_w_out"][0]
    for key, name in (("ffn_up", "ffn_w_up"), ("ffn_down", "ffn_w_down"), ("ple_proj", "ple_w_proj"), ("ple_gate", "ple_w_gate")):
        W[key] = list(full[name])


def _prepare_small(full):
    sp = {}
    sp["even_conv_w_il"] = full["even_conv_w"][0]
    sp["even_b_f_pad"] = _pad_cols(full["even_b_f"], LANES)
    sp["odd_conv_w"] = full["odd_conv_w"][0]
    sp["odd_conv_b"] = full["odd_conv_b"][0]
    sp["odd_norm_g"] = full["odd_norm_g"][0]
    for n in ("odd_dt_bias", "odd_a_log", "odd_d_skip"):
        sp[n] = full[n][0]
    for n in ("ln_mix_g", "ln_mix_b", "ln_ffn_g", "ln_ffn_b", "ple_b_gate"):
        sp[n] = full[n]
    sp["ffn_conv_w"] = [full["ffn_conv_w"][i] for i in range(DEPTH)]
    sp["ffn_conv_b"] = [full["ffn_conv_b"][i] for i in range(DEPTH)]
    return sp


def _local_step(x, p, target, full, late_shards):
    sp = _prepare_small(full)
    W = {}
    _prepare_first(W, full)
    grads = {n: [None] * _full_shapes()[n][0][0] for n in WEIGHTS}
    pb = p.astype(BF16)
    mix0, sv_e = _even_fwd(x.astype(BF16), W, sp, full, late_shards)
    h3_0, h3_0b, sv_t0 = _tail_fwd(0, x, mix0, pb[0], W, sp)
    mix1, sv_o = _odd_fwd(h3_0b, W, sp)
    h3_1, _, sv_t1 = _tail_fwd(1, h3_0, mix1, pb[1], W, sp)
    dh, sq = _loss_head(h3_1, target, "loss_head")
    dr1, dr1b = _tail_bwd(1, dh, sv_t1, W, sp, grads)
    dh = _odd_bwd(dr1b, dr1, sv_o, W, sp, grads)
    dr1, dr1b = _tail_bwd(0, dh, sv_t0, W, sp, grads)
    outgoing = [_split_unit(grads[n][i], n).astype(BF16) for n, i in _late_units()[1:]]
    dx, incoming = _even_bwd(dr1b, dr1, sv_e, W, sp, grads, outgoing)
    reduced = {u: _sum_slots(r, f"sum_grads_{u[0]}{u[1]}") for u, r in zip(_late_units(), incoming)}
    grads = {n: v if n in BIG else jnp.stack(v) for n, v in grads.items()}
    return jnp.sum(sq), dx, grads, reduced


def kernel(x, p, even_w_in, even_b_f, even_conv_w, even_w_out, odd_w_in, odd_conv_w, odd_conv_b, odd_dt_bias, odd_a_log, odd_d_skip, odd_norm_g, odd_w_out, ln_mix_g, ln_mix_b, ffn_w_up, ffn_conv_w, ffn_conv_b, ffn_w_down, ln_ffn_g, ln_ffn_b, ple_w_proj, ple_w_gate, ple_b_gate, loss_target, m_even_w_in, m_even_b_f, m_even_conv_w, m_even_w_out, m_odd_w_in, m_odd_conv_w, m_odd_conv_b, m_odd_dt_bias, m_odd_a_log, m_odd_d_skip, m_odd_norm_g, m_odd_w_out, m_ln_mix_g, m_ln_mix_b, m_ffn_w_up, m_ffn_conv_w, m_ffn_conv_b, m_ffn_w_down, m_ln_ffn_g, m_ln_ffn_b, m_ple_w_proj, m_ple_w_gate, m_ple_b_gate, v_even_w_in, v_even_b_f, v_even_conv_w, v_even_w_out, v_odd_w_in, v_odd_conv_w, v_odd_conv_b, v_odd_dt_bias, v_odd_a_log, v_odd_d_skip, v_odd_norm_g, v_odd_w_out, v_ln_mix_g, v_ln_mix_b, v_ffn_w_up, v_ffn_conv_w, v_ffn_conv_b, v_ffn_w_down, v_ln_ffn_g, v_ln_ffn_b, v_ple_w_proj, v_ple_w_gate, v_ple_b_gate):
    args = locals()
    w = {n: args[n] for n in WEIGHTS}
    m = {n: args["m_" + n] for n in WEIGHTS}
    v = {n: args["v_" + n] for n in WEIGHTS}
    me = 4 * lax.axis_index("x") + 2 * lax.axis_index("y") + lax.axis_index("c")

    def shard(unit):
        return _as2d(w[unit[0]][unit[1]]).astype(BF16)

    first, small = _all_gather([shard(FIRST_UNIT), _pack_small([w[n] for n in SMALL_SHARDED])], "ag_weights")
    full = dict(w)
    for n in BIG:
        full[n] = [None] * _full_shapes()[n][0][0]
    full[FIRST_UNIT[0]][FIRST_UNIT[1]] = _assemble_unit(first, FIRST_UNIT[0])
    for n, g in zip(SMALL_SHARDED, _unpack_small(small, [_shard_shape(n) for n in SMALL_SHARDED])):
        full[n] = _assemble(g, n)

    sq, dx, grads, reduced = _local_step(x[0], p[:, 0], loss_target[0], full, [shard(u) for u in _late_units()])
    loss = lax.psum(0.5 * sq / D_MODEL, ("x", "y", "c"))

    n0, i0 = FIRST_UNIT
    (reduced[FIRST_UNIT],) = _reduce_scatter([_split_unit(grads[n0][i0], n0).astype(BF16)], "rs_grads")
    g_final = {n: jnp.stack([reduced[(n, i)] for i in range(_full_shapes()[n][0][0])]).reshape(_shard_shape(n)) for n in BIG}
    small_names = SMALL_SHARDED + REPLICATED
    (small_all,) = _all_gather([_pack_small([grads[n] for n in small_names])], "ag_small_grads")
    small_sum = _sum_slots(small_all, "sum_small_grads")
    for n, g in zip(small_names, _unpack_small(small_sum, [_full_shapes()[n][0] for n in small_names])):
        g_final[n] = lax.dynamic_index_in_dim(_split_dest(g, n), me, axis=0, keepdims=False) if n in SMALL_SHARDED else g

    out = {}
    for n in BIG:
        res = _adamw(*[_as2d(t[n]) for t in (w, g_final, m, v)], "adamw_" + n)
        out[n] = [r.reshape(_shard_shape(n)) for r in res]
    shapes = [_shard_shape(n) for n in small_names]
    res = _adamw(*[_pack_small([t[n] for n in small_names]) for t in (w, g_final, m, v)], "adamw_small")
    for n, d_, m_, v_ in zip(small_names, *[_unpack_small(r, shapes) for r in res]):
        out[n] = [d_, m_, v_]
    return (loss, dx[None], *[g_final[n] for n in WEIGHTS], *[out[n][0] for n in WEIGHTS],
            *[out[n][1] for n in WEIGHTS], *[out[n][2] for n in WEIGHTS])
```

```python
import jax
import jax.numpy as jnp
import numpy as np
from jax import lax
from jax.experimental import pallas as pl
from jax.experimental.pallas import tpu as pltpu

D_MODEL = 1024
SEQ = 8192
DEPTH = 2
CONV_DIM = 512
CONV_WIDTH = 3
FOX_HEADS = 8
FOX_HEAD_DIM = 64
SSM_HEAD_DIM = 64
SSM_GROUPS = 4
SSM_STATE = 128
SSM_CONV_WIDTH = 4
SSM_CHUNK = 128
D_FF = 2816
FFN_CONV_WIDTH = 3
PLE_DIM = 256
LN_EPS = 1e-5
RMS_EPS = 1e-5
ADAM_LR = 0.001
ADAM_B1 = 0.9
ADAM_B2 = 0.999
ADAM_EPS = 1e-08
ADAM_WD = 0.01
ADAM_STEP = 10
N_DEV = 8

F32 = jnp.float32
BF16 = jnp.bfloat16
NEG = -1e30
LANES = 128
SUBLANES = 8
PACK_W = 1024
VMEM_LIMIT = 48 * 1024 * 1024
ATTN_BWD_VMEM_LIMIT = 56 * 1024 * 1024


def _dims():
    fox_dim = FOX_HEADS * FOX_HEAD_DIM
    ssm_inner = 2 * D_MODEL
    ssm_heads = ssm_inner // SSM_HEAD_DIM
    conv_ch = ssm_inner + 2 * SSM_GROUPS * SSM_STATE
    return dict(fox_dim=fox_dim, even_in=3 * CONV_DIM + 3 * fox_dim + FOX_HEADS, even_mix=CONV_DIM + fox_dim,
                ssm_inner=ssm_inner, ssm_heads=ssm_heads, conv_ch=conv_ch, odd_in=ssm_inner + conv_ch + ssm_heads)


def _alpha():
    return (2.0 * DEPTH) ** 0.25


def _pick(dim, prefs):
    for p in prefs:
        if dim % p == 0:
            return p
    return dim


def _pcall(body, **kw):
    return pl.pallas_call(body, **kw)


def _cparams(sem=None, **kw):
    if sem is not None:
        kw["dimension_semantics"] = sem
    return pltpu.CompilerParams(vmem_limit_bytes=VMEM_LIMIT, **kw)


def _sigmoid(x):
    return 1.0 / (1.0 + jnp.exp(-x))


def _softplus(x):
    return jnp.maximum(x, 0.0) + jnp.log(1.0 + jnp.exp(-jnp.abs(x)))


def _sum8(x):
    n, c = x.shape
    return x.reshape(n // SUBLANES, SUBLANES, c).sum(axis=0)


def _dot(a, b, dims):
    return lax.dot_general(a, b, (dims, ((), ())), preferred_element_type=F32)


NN = ((1,), (0,))
NT = ((1,), (1,))
TN = ((0,), (0,))


def _split3(x):
    hi = x.astype(BF16)
    r1 = x - hi.astype(F32)
    mid = r1.astype(BF16)
    lo = (r1 - mid.astype(F32)).astype(BF16)
    return hi, mid, lo


def _tri_mm(tri_bf16, x, tri_first=True):
    if tri_first:
        return sum(_dot(tri_bf16, part, NN) for part in _split3(x))
    return sum(_dot(part, tri_bf16, NN) for part in _split3(x))


def _tri(n, upper=False):
    r = lax.broadcasted_iota(jnp.int32, (n, n), 0)
    c = lax.broadcasted_iota(jnp.int32, (n, n), 1)
    return jnp.where((r <= c) if upper else (r >= c), 1.0, 0.0).astype(BF16)


def _shift_down(cur, prev8, k):
    if k == 0:
        return cur
    ext = jnp.concatenate([prev8, cur], axis=0)
    return pltpu.roll(ext, k, axis=0)[SUBLANES:]


def _shift_up(cur, next8, k):
    if k == 0:
        return cur
    n = cur.shape[0]
    ext = jnp.concatenate([cur, next8], axis=0)
    return pltpu.roll(ext, n + SUBLANES - k, axis=0)[:n]


def _mm(a, b, mode, out_dtype, name, add=None, add_scale=1.0, b_k_start=0):
    if mode == "nn":
        (M, K), (K2, N) = a.shape, b.shape
    elif mode == "nt":
        (M, K), N, K2 = a.shape, b.shape[0], a.shape[1]
        assert b.shape[1] >= b_k_start + K, (a.shape, b.shape, b_k_start)
    else:
        (K, M), (K2, N) = a.shape, b.shape
    assert K == K2, (a.shape, b.shape, mode)
    tm = _pick(M, (1024, 1408, 512, 256, 128))
    tn = _pick(N, (1408, 1024, 768, 512, 384, 256, 128))
    tk = K if K <= 2048 and b_k_start % K == 0 else _pick(K, (1408, 1024, 768, 512, 256, 128))
    assert b_k_start % tk == 0
    k0 = b_k_start // tk
    nk = K // tk
    dims = {"nn": NN, "nt": NT, "tn": TN}[mode]
    a_spec = pl.BlockSpec((tk, tm), lambda i, j, k: (k, i)) if mode == "tn" else pl.BlockSpec((tm, tk), lambda i, j, k: (i, k))
    b_spec = pl.BlockSpec((tn, tk), lambda i, j, k: (j, k + k0)) if mode == "nt" else pl.BlockSpec((tk, tn), lambda i, j, k: (k, j))
    o_spec = pl.BlockSpec((tm, tn), lambda i, j, k: (i, j))
    has_add = add is not None

    def body(*refs):
        a_ref, b_ref = refs[:2]
        add_ref = refs[2] if has_add else None
        o_ref = refs[2 + has_add]
        prod = _dot(a_ref[...].astype(BF16), b_ref[...].astype(BF16), dims)
        if nk == 1:
            if has_add:
                prod = prod + add_scale * add_ref[...].astype(F32)
            o_ref[...] = prod.astype(out_dtype)
            return
        acc = refs[3 + has_add]
        k = pl.program_id(2)

        @pl.when(k == 0)
        def _():
            if has_add:
                acc[...] = prod + add_scale * add_ref[...].astype(F32)
            else:
                acc[...] = prod

        @pl.when(k > 0)
        def _():
            acc[...] += prod

        @pl.when(k == nk - 1)
        def _():
            o_ref[...] = acc[...].astype(out_dtype)

    ins = [a, b] + ([add] if has_add else [])
    specs = [a_spec, b_spec] + ([o_spec] if has_add else [])
    return _pcall(body, name=name, grid=(M // tm, N // tn, nk), in_specs=specs, out_specs=o_spec,
                  out_shape=jax.ShapeDtypeStruct((M, N), out_dtype),
                  scratch_shapes=[pltpu.VMEM((tm, tn), F32)] if nk > 1 else [],
                  compiler_params=_cparams(("parallel", "parallel", "arbitrary")))(*ins)


def _row_tile(L, wide=False):
    return _pick(L, (256, 128)) if wide else _pick(L, (512, 256, 128))


def _conv_row_tile(L, backward):
    return _pick(L, (512, 256, 128)) if backward else _pick(L, (1024, 512, 256, 128))


def _ln_fwd(h, mix, g, b, name):
    L, D = h.shape
    tl = _row_tile(L)
    alpha = _alpha()

    def body(h_ref, m_ref, g_ref, b_ref, r_ref, y_ref, yb_ref):
        r = alpha * h_ref[...] + m_ref[...]
        mu = jnp.mean(r, axis=-1, keepdims=True)
        xc = r - mu
        var = jnp.mean(xc * xc, axis=-1, keepdims=True)
        r_ref[...] = r
        y = xc * lax.rsqrt(var + LN_EPS) * g_ref[...] + b_ref[...]
        y_ref[...] = y
        yb_ref[...] = y.astype(BF16)

    row = pl.BlockSpec((tl, D), lambda i: (i, 0))
    vec = pl.BlockSpec((1, D), lambda i: (0, 0))
    return _pcall(body, name=name, grid=(L // tl,), in_specs=[row, row, vec, vec], out_specs=[row, row, row],
                  out_shape=[jax.ShapeDtypeStruct((L, D), F32)] * 2 + [jax.ShapeDtypeStruct((L, D), BF16)],
                  compiler_params=_cparams(("parallel",)))(h, mix, g.reshape(1, D), b.reshape(1, D))


def _ln_bwd(r, dy, g, name):
    L, D = r.shape
    tl = _row_tile(L)

    def body(r_ref, dy_ref, g_ref, dr_ref, drb_ref, dg_ref, db_ref):
        i = pl.program_id(0)
        r_ = r_ref[...]
        dy_ = dy_ref[...]
        mu = jnp.mean(r_, axis=-1, keepdims=True)
        xc = r_ - mu
        rstd = lax.rsqrt(jnp.mean(xc * xc, axis=-1, keepdims=True) + LN_EPS)
        xhat = xc * rstd
        dxh = dy_ * g_ref[...]
        dr = rstd * (dxh - jnp.mean(dxh, axis=-1, keepdims=True) - xhat * jnp.mean(dxh * xhat, axis=-1, keepdims=True))
        dr_ref[...] = dr
        drb_ref[...] = dr.astype(BF16)

        @pl.when(i == 0)
        def _():
            dg_ref[...] = jnp.zeros_like(dg_ref)
            db_ref[...] = jnp.zeros_like(db_ref)

        dg_ref[...] += _sum8(dy_ * xhat)
        db_ref[...] += _sum8(dy_)

    row = pl.BlockSpec((tl, D), lambda i: (i, 0))
    vec = pl.BlockSpec((1, D), lambda i: (0, 0))
    acc = pl.BlockSpec((SUBLANES, D), lambda i: (0, 0))
    return _pcall(body, name=name, grid=(L // tl,), in_specs=[row, row, vec], out_specs=[row, row, acc, acc],
                  out_shape=[jax.ShapeDtypeStruct((L, D), F32), jax.ShapeDtypeStruct((L, D), BF16),
                             jax.ShapeDtypeStruct((SUBLANES, D), F32), jax.ShapeDtypeStruct((SUBLANES, D), F32)],
                  compiler_params=_cparams(("arbitrary",)))(r, dy, g.reshape(1, D))


def _ple_fwd(h2, G, bg, E, name):
    L, D = h2.shape
    tl = _row_tile(L)

    def body(h_ref, g_ref, b_ref, e_ref, o_ref, ob_ref):
        o = h_ref[...] + _sigmoid(g_ref[...] + b_ref[...]) * e_ref[...]
        o_ref[...] = o
        ob_ref[...] = o.astype(BF16)

    row = pl.BlockSpec((tl, D), lambda i: (i, 0))
    vec = pl.BlockSpec((1, D), lambda i: (0, 0))
    return _pcall(body, name=name, grid=(L // tl,), in_specs=[row, row, vec, row], out_specs=[row, row],
                  out_shape=[jax.ShapeDtypeStruct((L, D), F32), jax.ShapeDtypeStruct((L, D), BF16)],
                  compiler_params=_cparams(("parallel",)))(h2, G, bg.reshape(1, D), E)


def _ple_bwd(dh3, G, bg, E, name):
    L, D = dh3.shape
    tl = _row_tile(L)

    def body(d_ref, g_ref, b_ref, e_ref, de_ref, dg_ref, db_ref):
        i = pl.program_id(0)
        d = d_ref[...]
        sg = _sigmoid(g_ref[...] + b_ref[...])
        de_ref[...] = (d * sg).astype(BF16)
        dgp = d * e_ref[...] * sg * (1.0 - sg)
        dg_ref[...] = dgp.astype(BF16)

        @pl.when(i == 0)
        def _():
            db_ref[...] = jnp.zeros_like(db_ref)

        db_ref[...] += _sum8(dgp)

    row = pl.BlockSpec((tl, D), lambda i: (i, 0))
    vec = pl.BlockSpec((1, D), lambda i: (0, 0))
    acc = pl.BlockSpec((SUBLANES, D), lambda i: (0, 0))
    return _pcall(body, name=name, grid=(L // tl,), in_specs=[row, row, vec, row], out_specs=[row, row, acc],
                  out_shape=[jax.ShapeDtypeStruct((L, D), BF16), jax.ShapeDtypeStruct((L, D), BF16),
                             jax.ShapeDtypeStruct((SUBLANES, D), F32)],
                  compiler_params=_cparams(("arbitrary",)))(dh3, G, bg.reshape(1, D), E)


def _loss_head(h, target, name):
    L, D = h.shape
    tl = _row_tile(L)

    def body(h_ref, t_ref, d_ref, s_ref):
        i = pl.program_id(0)
        e = h_ref[...] - t_ref[...]
        d_ref[...] = e * (1.0 / D)

        @pl.when(i == 0)
        def _():
            s_ref[...] = jnp.zeros_like(s_ref)

        s_ref[...] += _sum8(e * e)

    row = pl.BlockSpec((tl, D), lambda i: (i, 0))
    acc = pl.BlockSpec((SUBLANES, D), lambda i: (0, 0))
    return _pcall(body, name=name, grid=(L // tl,), in_specs=[row, row], out_specs=[row, acc],
                  out_shape=[jax.ShapeDtypeStruct((L, D), F32), jax.ShapeDtypeStruct((SUBLANES, D), F32)],
                  compiler_params=_cparams(("arbitrary",)))(h, target)


def _halo_prev(tl, ncol_blocks_fn):
    return lambda j, i: (jnp.maximum(i * (tl // SUBLANES) - 1, 0), ncol_blocks_fn(j))


STRIP_ROWS = 64


def _strip(s, R):
    return pl.ds(s * R if isinstance(s, int) else pl.multiple_of(s * R, R), R)


def _conv_taps(cur, prev, w_ref, K):
    acc = w_ref[K - 1:K, :] * cur
    for k in range(K - 1):
        acc = acc + w_ref[k:k + 1, :] * _shift_down(cur, prev, K - 1 - k)
    return acc


def _ffn_act_fwd(U, w, b, name):
    L, F2 = U.shape
    F = F2 // 2
    K = w.shape[0]
    tc = _pick(F, (1408, 256, 128))
    tl = _pick(L, (512, 256, 128)) if tc > 256 else _conv_row_tile(L, False)
    nj = F // tc

    def body(ug_ref, uv_ref, ugp_ref, uvp_ref, wg_ref, wv_ref, bg_ref, bv_ref, s_ref):
        i = pl.program_id(1)
        g = _conv_taps(ug_ref[...], jnp.where(i == 0, 0.0, ugp_ref[...]), wg_ref, K) + bg_ref[...]
        v = _conv_taps(uv_ref[...], jnp.where(i == 0, 0.0, uvp_ref[...]), wv_ref, K) + bv_ref[...]
        s_ref[...] = (g * _sigmoid(g) * v).astype(BF16)

    def both(shape, index):
        return [pl.BlockSpec(shape, lambda j, i: index(j, i)), pl.BlockSpec(shape, lambda j, i: index(j + nj, i))]

    b2 = b.reshape(1, F2)
    return _pcall(body, name=name, grid=(nj, L // tl),
                  in_specs=both((tl, tc), lambda j, i: (i, j)) + both((SUBLANES, tc), _halo_prev(tl, lambda j: j))
                  + both((K, tc), lambda j, i: (0, j)) + both((1, tc), lambda j, i: (0, j)),
                  out_specs=pl.BlockSpec((tl, tc), lambda j, i: (i, j)),
                  out_shape=jax.ShapeDtypeStruct((L, F), BF16),
                  compiler_params=_cparams(("parallel", "parallel")))(U, U, U, U, w, w, b2, b2)


def _halo_next(tl, L, rows):
    return lambda j, i: (jnp.minimum((i + 1) * (tl // rows), L // rows - 1), j)


BF16_ROWS = 16


def _ffn_act_bwd(U, dS, w, b, name):
    L, F2 = U.shape
    F = F2 // 2
    K = w.shape[0]
    tc = _pick(F, (256, 128))
    tl = _pick(L, (1024, 512, 256, 128))
    nl = L // tl
    nj = F // tc
    R = STRIP_ROWS
    ns = tl // R

    def body(ug_ref, uv_ref, ugp_ref, uvp_ref, ugn_ref, uvn_ref, ds_ref, dsn_ref, wg_ref, wv_ref, bg_ref, bv_ref,
             dug_ref, duv_ref, dwg_ref, dwv_ref, dbg_ref, dbv_ref):
        i = pl.program_id(1)

        @pl.when(i == 0)
        def _():
            for r in (dwg_ref, dwv_ref, dbg_ref, dbv_ref):
                r[...] = jnp.zeros_like(r)

        for lanes in (slice(c, c + LANES) for c in range(0, tc, LANES)):
            wts = [[w_ref[k:k + 1, lanes] for k in range(K)] for w_ref in (wg_ref, wv_ref)]
            bias = [b_ref[:, lanes] for b_ref in (bg_ref, bv_ref)]

            def strip(s):
                return _strip(s, R)

            def at_conv_out(x, xprev8, ds):
                sh = [[_shift_down(x[h], xprev8[h], K - 1 - k) for k in range(K)] for h in range(2)]
                g, v = (sum(wts[h][k] * sh[h][k] for k in range(K)) + bias[h] for h in range(2))
                sg = _sigmoid(g)
                return (ds * v * sg * (1.0 + g * (1.0 - sg)), ds * g * sg), sh

            def emit(rows, duc, duc_next8):
                for h, du_ref in enumerate((dug_ref, duv_ref)):
                    du = sum(wts[h][k] * _shift_up(duc[h], duc_next8[h], K - 1 - k) for k in range(K))
                    du_ref[rows, lanes] = du.astype(BF16)

            def accumulate(acc, duc, sh):
                dw, db = acc
                return (tuple(tuple(dw[h][k] + _sum8(duc[h] * sh[h][k]) for k in range(K)) for h in range(2)),
                        tuple(db[h] + _sum8(duc[h]) for h in range(2)))

            def last8(x):
                return tuple(a[R - SUBLANES:] for a in x)

            zero = jnp.zeros((SUBLANES, LANES), F32)
            x0 = (ug_ref[strip(0), lanes], uv_ref[strip(0), lanes])
            prev8 = (jnp.where(i == 0, 0.0, ugp_ref[:, lanes]), jnp.where(i == 0, 0.0, uvp_ref[:, lanes]))
            duc0, sh0 = at_conv_out(x0, prev8, ds_ref[strip(0), lanes].astype(F32))
            acc0 = accumulate(((((zero,) * K,) * 2), (zero,) * 2), duc0, sh0)

            def step(s, carry):
                xprev, ducprev, acc = carry
                x = (ug_ref[strip(s), lanes], uv_ref[strip(s), lanes])
                duc, sh = at_conv_out(x, last8(xprev), ds_ref[strip(s), lanes].astype(F32))
                emit(strip(s - 1), ducprev, tuple(d[:SUBLANES] for d in duc))
                return x, duc, accumulate(acc, duc, sh)

            xl, ducl, (dw, db) = lax.fori_loop(1, ns, step, (x0, duc0, acc0))
            ducn, _ = at_conv_out((ugn_ref[:, lanes], uvn_ref[:, lanes]), last8(xl), dsn_ref[:, lanes].astype(F32)[:SUBLANES])
            emit(strip(ns - 1), ducl, tuple(jnp.where(i == nl - 1, 0.0, d) for d in ducn))
            for h, (dw_ref, db_ref) in enumerate(((dwg_ref, dbg_ref), (dwv_ref, dbv_ref))):
                db_ref[:, lanes] += db[h]
                for k in range(K):
                    dw_ref[k * SUBLANES:(k + 1) * SUBLANES, lanes] += dw[h][k]

    def both(shape, index):
        return [pl.BlockSpec(shape, lambda j, i: index(j, i)), pl.BlockSpec(shape, lambda j, i: index(j + nj, i))]

    b2 = b.reshape(1, F2)
    du_specs, du_shapes = [pl.BlockSpec((tl, tc), lambda j, i: (i, j))] * 2, [jax.ShapeDtypeStruct((L, F), BF16)] * 2
    dw_specs = [pl.BlockSpec((K * SUBLANES, tc), lambda j, i: (0, j))] * 2
    dw_shapes = [jax.ShapeDtypeStruct((K * SUBLANES, F), F32)] * 2
    db_specs, db_shapes = [pl.BlockSpec((SUBLANES, tc), lambda j, i: (0, j))] * 2, [jax.ShapeDtypeStruct((SUBLANES, F), F32)] * 2
    return _pcall(body, name=name, grid=(nj, nl),
                  in_specs=both((tl, tc), lambda j, i: (i, j)) + both((SUBLANES, tc), _halo_prev(tl, lambda j: j))
                  + both((SUBLANES, tc), _halo_next(tl, L, SUBLANES))
                  + [pl.BlockSpec((tl, tc), lambda j, i: (i, j)), pl.BlockSpec((BF16_ROWS, tc), _halo_next(tl, L, BF16_ROWS))]
                  + both((K, tc), lambda j, i: (0, j)) + both((1, tc), lambda j, i: (0, j)),
                  out_specs=du_specs + dw_specs + db_specs, out_shape=du_shapes + dw_shapes + db_shapes,
                  compiler_params=_cparams(("parallel", "arbitrary")))(U, U, U, U, U, U, dS, dS, w, w, b2, b2)


def _sconv_fwd(Ac, w, name):
    L, C3 = Ac.shape
    C = C3 // 3
    K = w.shape[0]
    tl = _conv_row_tile(L, False)
    tc = LANES

    def body(a_ref, ap_ref, w_ref, y_ref):
        i = pl.program_id(1)
        a = a_ref[...]
        ap = ap_ref[...]
        p = a[:, tc:2 * tc] * a[:, 2 * tc:]
        pp = jnp.where(i == 0, 0.0, ap[:, tc:2 * tc] * ap[:, 2 * tc:])
        y_ref[...] = (a[:, :tc] * _conv_taps(p, pp, w_ref, K)).astype(BF16)

    return _pcall(body, name=name, grid=(C // tc, L // tl),
                  in_specs=[pl.BlockSpec((tl, 3 * tc), lambda j, i: (i, j)),
                            pl.BlockSpec((SUBLANES, 3 * tc), _halo_prev(tl, lambda j: j)),
                            pl.BlockSpec((K, tc), lambda j, i: (0, j))],
                  out_specs=pl.BlockSpec((tl, tc), lambda j, i: (i, j)),
                  out_shape=jax.ShapeDtypeStruct((L, C), BF16),
                  compiler_params=_cparams(("parallel", "parallel")))(Ac, Ac, w)


def _sconv_bwd_dc(Ac, dy, w, name):
    L, C3 = Ac.shape
    C = C3 // 3
    K = w.shape[0]
    tl = _conv_row_tile(L, False)
    tc = LANES

    def body(a_ref, ap_ref, dy_ref, dc_ref, dw_ref):
        i = pl.program_id(1)
        a = a_ref[...]
        ap = ap_ref[...]
        p = a[:, tc:2 * tc] * a[:, 2 * tc:]
        pp = jnp.where(i == 0, 0.0, ap[:, tc:2 * tc] * ap[:, 2 * tc:])
        dc = dy_ref[...] * a[:, :tc]
        dc_ref[...] = dc

        @pl.when(i == 0)
        def _():
            dw_ref[...] = jnp.zeros_like(dw_ref)

        for k in range(K):
            dw_ref[k * SUBLANES:(k + 1) * SUBLANES, :] += _sum8(dc * _shift_down(p, pp, K - 1 - k))

    return _pcall(body, name=name, grid=(C // tc, L // tl),
                  in_specs=[pl.BlockSpec((tl, 3 * tc), lambda j, i: (i, j)),
                            pl.BlockSpec((SUBLANES, 3 * tc), _halo_prev(tl, lambda j: j)),
                            pl.BlockSpec((tl, tc), lambda j, i: (i, j))],
                  out_specs=[pl.BlockSpec((tl, tc), lambda j, i: (i, j)),
                             pl.BlockSpec((K * SUBLANES, tc), lambda j, i: (0, j))],
                  out_shape=[jax.ShapeDtypeStruct((L, C), F32), jax.ShapeDtypeStruct((K * SUBLANES, C), F32)],
                  compiler_params=_cparams(("parallel", "arbitrary")))(Ac, Ac, dy)


def _sconv_bwd_da(Ac, dy, dc, w, name):
    L, C3 = Ac.shape
    C = C3 // 3
    K = w.shape[0]
    tl = _conv_row_tile(L, False)
    tc = LANES
    nl = L // tl

    def body(a_ref, ap_ref, dy_ref, dc_ref, dcn_ref, w_ref, o_ref):
        i = pl.program_id(1)
        a = a_ref[...]
        ap = ap_ref[...]
        gc, h = a[:, tc:2 * tc], a[:, 2 * tc:]
        p = gc * h
        pp = jnp.where(i == 0, 0.0, ap[:, tc:2 * tc] * ap[:, 2 * tc:])
        dgb = dy_ref[...] * _conv_taps(p, pp, w_ref, K)
        cur = dc_ref[...]
        nxt = jnp.where(i == nl - 1, 0.0, dcn_ref[...])
        dp = w_ref[K - 1:K, :] * cur
        for k in range(K - 1):
            dp = dp + w_ref[k:k + 1, :] * _shift_up(cur, nxt, K - 1 - k)
        o_ref[...] = jnp.concatenate([dgb, dp * h, dp * gc], axis=1).astype(BF16)

    return _pcall(body, name=name, grid=(C // tc, nl),
                  in_specs=[pl.BlockSpec((tl, 3 * tc), lambda j, i: (i, j)),
                            pl.BlockSpec((SUBLANES, 3 * tc), _halo_prev(tl, lambda j: j)),
                            pl.BlockSpec((tl, tc), lambda j, i: (i, j)),
                            pl.BlockSpec((tl, tc), lambda j, i: (i, j)),
                            pl.BlockSpec((SUBLANES, tc), lambda j, i: (jnp.minimum((i + 1) * (tl // SUBLANES), L // SUBLANES - 1), j)),
                            pl.BlockSpec((K, tc), lambda j, i: (0, j))],
                  out_specs=pl.BlockSpec((tl, 3 * tc), lambda j, i: (i, j)),
                  out_shape=jax.ShapeDtypeStruct((L, C3), BF16),
                  compiler_params=_cparams(("parallel", "parallel")))(Ac, Ac, dy, dc, dc, w)


def _fox_gate_fwd(Af, bf, name):
    L, W = Af.shape
    tl = _pick(L, (512, 256, 128))

    def body(a_ref, b_ref, f_ref, carry):
        i = pl.program_id(0)

        @pl.when(i == 0)
        def _():
            carry[...] = jnp.zeros_like(carry)

        z = a_ref[...] + b_ref[...]
        logf = jnp.minimum(z, 0.0) - jnp.log(1.0 + jnp.exp(-jnp.abs(z)))
        f = _tri_mm(_tri(tl), logf) + carry[...]
        f_ref[...] = f
        carry[...] = f[tl - 1:tl, :]

    row = pl.BlockSpec((tl, W), lambda i: (i, 0))
    return _pcall(body, name=name, grid=(L // tl,), in_specs=[row, pl.BlockSpec((1, W), lambda i: (0, 0))], out_specs=row,
                  out_shape=jax.ShapeDtypeStruct((L, W), F32), scratch_shapes=[pltpu.VMEM((1, W), F32)],
                  compiler_params=_cparams(("arbitrary",)))(Af, bf)


def _fox_gate_bwd(Af, bf, dF, name):
    L, W = Af.shape
    tl = _pick(L, (512, 256, 128))
    nl = L // tl

    def body(a_ref, b_ref, df_ref, o_ref, db_ref, carry):
        i = pl.program_id(0)

        @pl.when(i == 0)
        def _():
            carry[...] = jnp.zeros_like(carry)
            db_ref[...] = jnp.zeros_like(db_ref)

        z = a_ref[...] + b_ref[...]
        dlogf = _tri_mm(_tri(tl, upper=True), df_ref[...]) + carry[...]
        carry[...] = dlogf[0:1, :]
        dz = dlogf * _sigmoid(-z)
        o_ref[...] = dz
        db_ref[...] += _sum8(dz)

    row = pl.BlockSpec((tl, W), lambda i: (nl - 1 - i, 0))
    return _pcall(body, name=name, grid=(nl,),
                  in_specs=[row, pl.BlockSpec((1, W), lambda i: (0, 0)), row],
                  out_specs=[row, pl.BlockSpec((SUBLANES, W), lambda i: (0, 0))],
                  out_shape=[jax.ShapeDtypeStruct((L, W), F32), jax.ShapeDtypeStruct((SUBLANES, W), F32)],
                  scratch_shapes=[pltpu.VMEM((1, W), F32)],
                  compiler_params=_cparams(("arbitrary",)))(Af, bf, dF)


def _attn_tiles(L):
    t = _pick(L, (512, 256, 128))
    return t, t


def _attn_scores(q, k, fq, fk, diag, scale):
    s = _dot(q, k, NT) * scale + (fq - fk)
    if not diag:
        return s
    row = lax.broadcasted_iota(jnp.int32, s.shape, 0)
    col = lax.broadcasted_iota(jnp.int32, s.shape, 1)
    return jnp.where(col <= row, s, NEG)


def _attn_geometry():
    Dh = FOX_HEAD_DIM
    hpt = LANES // Dh
    return Dh, hpt, FOX_HEADS // hpt


def _head_lanes(shape, Dh, hpt):
    lane = lax.broadcasted_iota(jnp.int32, shape, len(shape) - 1)
    return [(lane >= h * Dh) & (lane < (h + 1) * Dh) for h in range(hpt)]


def _attn_specs(t, L, hpt, ng):
    return dict(
        col=lambda off: pl.BlockSpec((t, LANES), lambda g, i: (i, g + off)),
        full=lambda off: pl.BlockSpec((L, LANES), lambda g, i: (0, g + off)),
        hq=pl.BlockSpec((hpt, t, 1), lambda g, i: (g, i, 0)),
        hk_full=pl.BlockSpec((hpt, 1, L), lambda g, i: (g, 0, 0)),
        hk=pl.BlockSpec((hpt, 1, t), lambda g, i: (g, 0, i)))


def _attn_fwd(qkv, Fq, Fk, name, gather=()):
    L = qkv.shape[0]
    Dh, hpt, ng = _attn_geometry()
    t, _ = _attn_tiles(L)
    scale = Dh ** -0.5
    n = len(gather)
    nsteps = ng * (L // t)

    def body(*refs):
        q_ref, k_ref, v_ref, fq_ref, fk_ref = refs[:5]
        o_ref, lse_ref = refs[5 + n:7 + n]
        qi = pl.program_id(1)
        step = pl.program_id(0) * (L // t) + qi
        if n:
            start, forward, finish = _gather_phases(refs[5:5 + n], refs[7 + n:7 + 2 * n], *refs[7 + 2 * n:])
            pl.when(step == 0)(start)
            pl.when(step == nsteps // 2)(forward)
        sel = _head_lanes((t, LANES), Dh, hpt)
        q2 = q_ref[...]
        qh = [jnp.where(sel[h], q2, 0) for h in range(hpt)]
        fq = [fq_ref[h] for h in range(hpt)]

        def chunk(j, carry, diag):
            rows = pl.ds(pl.multiple_of(j * t, t), t)
            kc, vc = k_ref[rows, :], v_ref[rows, :]
            out = []
            for h in range(hpt):
                m, l, acc = carry[h]
                s = _attn_scores(qh[h], kc, fq[h], fk_ref[h, :, rows], diag, scale)
                m_new = jnp.maximum(m, jnp.max(s, axis=-1, keepdims=True))
                p = jnp.exp(s - m_new)
                a = jnp.exp(m - m_new)
                out.append((m_new, a * l + jnp.sum(p, axis=-1, keepdims=True), a * acc + _dot(p.astype(BF16), vc, NN)))
            return tuple(out)

        init = tuple((jnp.full((t, 1), NEG, F32), jnp.zeros((t, 1), F32), jnp.zeros((t, LANES), F32)) for _ in range(hpt))
        fin = chunk(qi, lax.fori_loop(0, qi, lambda j, c: chunk(j, c, False), init), True)
        o = jnp.zeros((t, LANES), F32)
        for h, (m, l, acc) in enumerate(fin):
            o = jnp.where(sel[h], acc / l, o)
            lse_ref[h] = m + jnp.log(l)
        o_ref[...] = o.astype(BF16)
        if n:
            pl.when(step == nsteps - 1)(finish)

    sp = _attn_specs(t, L, hpt, ng)
    return _pcall(body, name=name, grid=(ng, L // t),
                  in_specs=[sp["col"](0), sp["full"](ng), sp["full"](2 * ng), sp["hq"], sp["hk_full"]] + [HBM_SPEC] * n,
                  out_specs=[sp["col"](0), sp["hq"]] + [HBM_SPEC] * n,
                  out_shape=[jax.ShapeDtypeStruct((L, ng * LANES), BF16), jax.ShapeDtypeStruct((FOX_HEADS, L, 1), F32)]
                  + _gather_shapes(gather),
                  scratch_shapes=_gather_sems(n) if n else [],
                  compiler_params=_cparams(("arbitrary", "arbitrary") if n else ("parallel", "arbitrary")))(
        qkv, qkv, qkv, Fq, Fk, *gather)


def _attn_bwd(qkv, Fq, Fk, lse, do, name, exchange=()):
    L = qkv.shape[0]
    Dh, hpt, ng = _attn_geometry()
    _, tk = _attn_tiles(L)
    tq = _pick(L, (256, 128))
    nkc = L // tk
    scale = Dh ** -0.5

    n = len(exchange)
    nsteps = ng * (L // tq)

    def body(*refs):
        q_ref, k_ref, v_ref, fq_ref, fk_ref, lse_ref, do_ref = refs[:7]
        dq_ref, dk_ref, dv_ref, df_ref = refs[7 + n:11 + n]
        p_s, dp_s = refs[11 + 2 * n:13 + 2 * n]
        qi = pl.program_id(1)
        step = pl.program_id(0) * (L // tq) + qi
        if n:
            start, finish = _exchange_phases(refs[7:7 + n], refs[11 + n:11 + 2 * n], *refs[13 + 2 * n:])
            pl.when(step == 0)(start)

        @pl.when(qi == 0)
        def _():
            dk_ref[...] = jnp.zeros_like(dk_ref)
            dv_ref[...] = jnp.zeros_like(dv_ref)
            df_ref[...] = jnp.zeros_like(df_ref)

        sel = _head_lanes((tq, LANES), Dh, hpt)
        q2, do2 = q_ref[...], do_ref[...]
        jd = (qi * tq) // tk
        off = qi * tq - jd * tk
        dq = jnp.zeros((tq, LANES), F32)
        for h in range(hpt):
            qh, doh = jnp.where(sel[h], q2, 0), jnp.where(sel[h], do2, 0)
            fq, lse = fq_ref[h], lse_ref[h]

            def first(j, acc, diag):
                rows = pl.ds(pl.multiple_of(j * tk, tk), tk)
                s = _dot(qh, k_ref[rows, :], NT) * scale + (fq - fk_ref[h, :, rows])
                if diag:
                    row = lax.broadcasted_iota(jnp.int32, s.shape, 0) + off
                    s = jnp.where(lax.broadcasted_iota(jnp.int32, s.shape, 1) <= row, s, NEG)
                p = jnp.exp(s - lse)
                dp = _dot(doh, v_ref[rows, :], NT)
                p_s[j] = p
                dp_s[j] = dp
                return acc + jnp.sum(p * dp, axis=-1, keepdims=True)

            delta = first(jd, lax.fori_loop(0, jd, lambda j, c: first(j, c, False), jnp.zeros((tq, 1), F32)), True)

            def second(j, acc):
                rows = pl.ds(pl.multiple_of(j * tk, tk), tk)
                p = p_s[j]
                ds = p * (dp_s[j] - delta)
                dsb = ds.astype(BF16)
                dk_ref[rows, :] += _dot(dsb, qh, TN)
                dv_ref[rows, :] += _dot(p.astype(BF16), doh, TN)
                df_ref[h, :, rows] -= jnp.sum(ds, axis=0, keepdims=True)
                return acc + _dot(dsb, k_ref[rows, :], NN)

            dq = jnp.where(sel[h], lax.fori_loop(0, jd + 1, second, jnp.zeros((tq, LANES), F32)), dq)
        dq_ref[...] = (dq * scale).astype(BF16)

        @pl.when(qi == L // tq - 1)
        def _():
            dk_ref[...] *= scale

        if n:
            pl.when(step == nsteps - 1)(finish)

    sp = _attn_specs(tq, L, hpt, ng)
    return _pcall(body, name=name, grid=(ng, L // tq),
                  in_specs=[sp["col"](0), sp["full"](ng), sp["full"](2 * ng), sp["hq"], sp["hk_full"], sp["hq"], sp["col"](0)]
                  + [HBM_SPEC] * n,
                  out_specs=[sp["col"](0), sp["full"](0), sp["full"](0), sp["hk_full"]] + [HBM_SPEC] * n,
                  out_shape=[jax.ShapeDtypeStruct((L, ng * LANES), BF16), jax.ShapeDtypeStruct((L, ng * LANES), F32),
                             jax.ShapeDtypeStruct((L, ng * LANES), F32), jax.ShapeDtypeStruct((FOX_HEADS, 1, L), F32)]
                  + [jax.ShapeDtypeStruct(g.shape, g.dtype) for g in exchange],
                  scratch_shapes=[pltpu.VMEM((nkc, tq, tk), F32), pltpu.VMEM((nkc, tq, tk), F32)] + (_gather_sems(n) if n else []),
                  compiler_params=pltpu.CompilerParams(
                      vmem_limit_bytes=ATTN_BWD_VMEM_LIMIT,
                      dimension_semantics=("arbitrary", "arbitrary") if n else ("parallel", "arbitrary")))(
        qkv, qkv, qkv, Fq, Fk, lse, do, *exchange)


def _mconv_fwd(xr, w, b, name):
    L, C = xr.shape
    K = w.shape[0]
    tl = _conv_row_tile(L, False)
    tc = _pick(C, (512, 384, 256, 128))

    def body(x_ref, xp_ref, w_ref, b_ref, o_ref):
        i = pl.program_id(1)
        prev = jnp.where(i == 0, 0.0, xp_ref[...])
        pre = _conv_taps(x_ref[...], prev, w_ref, K) + b_ref[...]
        o_ref[...] = pre * _sigmoid(pre)

    return _pcall(body, name=name, grid=(C // tc, L // tl),
                  in_specs=[pl.BlockSpec((tl, tc), lambda j, i: (i, j)),
                            pl.BlockSpec((SUBLANES, tc), _halo_prev(tl, lambda j: j)),
                            pl.BlockSpec((K, tc), lambda j, i: (0, j)),
                            pl.BlockSpec((1, tc), lambda j, i: (0, j))],
                  out_specs=pl.BlockSpec((tl, tc), lambda j, i: (i, j)),
                  out_shape=jax.ShapeDtypeStruct((L, C), F32),
                  compiler_params=_cparams(("parallel", "parallel")))(xr, xr, w, b.reshape(1, C))


def _mconv_bwd(xr, dact, w, b, name):
    L, C = xr.shape
    K = w.shape[0]
    tl = _conv_row_tile(L, True)
    tc = _pick(C, (512, 384, 256, 128))
    nl = L // tl

    R = STRIP_ROWS
    ns = tl // R

    def body(x_ref, xp_ref, xn_ref, d_ref, dn_ref, w_ref, b_ref, o_ref, dw_ref, db_ref):
        i = pl.program_id(1)

        @pl.when(i == 0)
        def _():
            dw_ref[...] = jnp.zeros_like(dw_ref)
            db_ref[...] = jnp.zeros_like(db_ref)

        for lanes in (slice(c, c + LANES) for c in range(0, tc, LANES)):
            wts = [w_ref[k:k + 1, lanes] for k in range(K)]
            bias = b_ref[:, lanes]

            def at_conv_out(x, xprev8, d):
                sh = [_shift_down(x, xprev8, K - 1 - k) for k in range(K)]
                pre = sum(wts[k] * sh[k] for k in range(K)) + bias
                sg = _sigmoid(pre)
                return d * sg * (1.0 + pre * (1.0 - sg)), sh

            def emit(rows, dpre, dpre_next8):
                o_ref[rows, lanes] = sum(wts[k] * _shift_up(dpre, dpre_next8, K - 1 - k) for k in range(K)).astype(BF16)

            def accumulate(acc, dpre, sh):
                return tuple(acc[k] + _sum8(dpre * sh[k]) for k in range(K)) + (acc[K] + _sum8(dpre),)

            x0 = x_ref[_strip(0, R), lanes]
            dpre0, sh0 = at_conv_out(x0, jnp.where(i == 0, 0.0, xp_ref[:, lanes]), d_ref[_strip(0, R), lanes])
            acc0 = accumulate((jnp.zeros((SUBLANES, LANES), F32),) * (K + 1), dpre0, sh0)

            def step(s, carry):
                xprev, dprev, acc = carry
                x = x_ref[_strip(s, R), lanes]
                dpre, sh = at_conv_out(x, xprev[R - SUBLANES:], d_ref[_strip(s, R), lanes])
                emit(_strip(s - 1, R), dprev, dpre[:SUBLANES])
                return x, dpre, accumulate(acc, dpre, sh)

            xl, dl, acc = lax.fori_loop(1, ns, step, (x0, dpre0, acc0))
            dn, _ = at_conv_out(xn_ref[:, lanes], xl[R - SUBLANES:], dn_ref[:, lanes])
            emit(_strip(ns - 1, R), dl, jnp.where(i == nl - 1, 0.0, dn))
            db_ref[:, lanes] += acc[K]
            for k in range(K):
                dw_ref[k * SUBLANES:(k + 1) * SUBLANES, lanes] += acc[k]

    return _pcall(body, name=name, grid=(C // tc, nl),
                  in_specs=[pl.BlockSpec((tl, tc), lambda j, i: (i, j)),
                            pl.BlockSpec((SUBLANES, tc), _halo_prev(tl, lambda j: j)),
                            pl.BlockSpec((SUBLANES, tc), _halo_next(tl, L, SUBLANES)),
                            pl.BlockSpec((tl, tc), lambda j, i: (i, j)),
                            pl.BlockSpec((SUBLANES, tc), _halo_next(tl, L, SUBLANES)),
                            pl.BlockSpec((K, tc), lambda j, i: (0, j)),
                            pl.BlockSpec((1, tc), lambda j, i: (0, j))],
                  out_specs=[pl.BlockSpec((tl, tc), lambda j, i: (i, j)),
                             pl.BlockSpec((K * SUBLANES, tc), lambda j, i: (0, j)),
                             pl.BlockSpec((SUBLANES, tc), lambda j, i: (0, j))],
                  out_shape=[jax.ShapeDtypeStruct((L, C), BF16), jax.ShapeDtypeStruct((K * SUBLANES, C), F32),
                             jax.ShapeDtypeStruct((SUBLANES, C), F32)],
                  compiler_params=_cparams(("parallel", "arbitrary")))(xr, xr, xr, dact, dact, w, b.reshape(1, C))


def _head_selector(R, P, heads_first):
    shape = (R, R * P) if heads_first else (R * P, R)
    head = lax.broadcasted_iota(jnp.int32, shape, 0 if heads_first else 1)
    lane = lax.broadcasted_iota(jnp.int32, shape, 1 if heads_first else 0)
    d = lane - head * P
    return jnp.where((d >= 0) & (d < P), 1.0, 0.0).astype(BF16)


def _ssd_prelude(dtc_ref, dtr_ref, bc_ref, br_ref, ac_ref, ar_ref, Q, R, P):
    raw_c = dtc_ref[...] + bc_ref[...]
    dt_c = _softplus(raw_c)
    dt_r = _softplus(dtr_ref[...] + br_ref[...])
    A_c = -jnp.exp(ac_ref[...])
    acs_c = _tri_mm(_tri(Q), dt_c * A_c)
    acs_r = _tri_mm(_tri(Q, upper=True), dt_r * (-jnp.exp(ar_ref[...])), tri_first=False)
    ea_c = jnp.exp(acs_c)
    dte_c = jnp.exp(acs_c[Q - 1:Q, :] - acs_c)
    wide = _tri_mm(_head_selector(R, P, True), jnp.concatenate([dt_c, ea_c, dte_c], axis=0), tri_first=False)
    return dict(raw_c=raw_c, dt_c=dt_c, A_c=A_c, acs_c=acs_c, acs_r=acs_r, ea_c=ea_c,
                DT=wide[:Q], EA=wide[Q:2 * Q], DTE=wide[2 * Q:])


def _ssd_decay_tile(pre, r, mask):
    return jnp.exp(jnp.where(mask, pre["acs_c"][:, r:r + 1] - pre["acs_r"][r:r + 1, :], NEG))


def _ssd_specs(Q, R, P, N, G, inner, rev=None):
    cc = (lambda c: c) if rev is None else rev
    return dict(
        x=pl.BlockSpec((Q, R * P), lambda g, c: (cc(c), g)),
        b=pl.BlockSpec((Q, N), lambda g, c: (cc(c), inner // N + g)),
        c=pl.BlockSpec((Q, N), lambda g, c: (cc(c), inner // N + G + g)),
        dtc=pl.BlockSpec((None, Q, R), lambda g, c: (g, cc(c), 0)),
        dtr=pl.BlockSpec((None, R, Q), lambda g, c: (g, 0, cc(c))),
        pc=pl.BlockSpec((None, 1, R), lambda g, c: (g, 0, 0)),
        pr=pl.BlockSpec((None, R, 1), lambda g, c: (g, 0, 0)),
        px=pl.BlockSpec((None, 1, R * P), lambda g, c: (g, 0, 0)),
        st=pl.BlockSpec((None, None, N, R * P), lambda g, c: (cc(c), g, 0, 0)))


def _ssd_fwd(act, dtc, dtr, bias_c, bias_r, alog_c, alog_r, dsk_x, name):
    G, L, R = dtc.shape
    N, P, Q = SSM_STATE, SSM_HEAD_DIM, SSM_CHUNK
    RP = R * P
    inner = G * RP
    nc = L // Q

    def body(x_ref, b_ref, c_ref, dtc_ref, dtr_ref, bc_ref, br_ref, ac_ref, ar_ref, dk_ref, y_ref, hp_ref, st):
        c = pl.program_id(1)

        @pl.when(c == 0)
        def _():
            st[...] = jnp.zeros_like(st)

        pre = _ssd_prelude(dtc_ref, dtr_ref, bc_ref, br_ref, ac_ref, ar_ref, Q, R, P)
        X = x_ref[...]
        XT = X * pre["DT"]
        Bb = b_ref[...].astype(BF16)
        Cb = c_ref[...].astype(BF16)
        CB = _dot(Cb, Bb, NT)
        mask = lax.broadcasted_iota(jnp.int32, (Q, Q), 0) >= lax.broadcasted_iota(jnp.int32, (Q, Q), 1)
        low = lax.broadcasted_iota(jnp.int32, (Q, 2 * P), 1) < P
        pieces = []
        for k in range(R // 2):
            xt2 = XT[:, 2 * P * k:2 * P * (k + 1)]
            acc = None
            for half in range(2):
                Gm = CB * _ssd_decay_tile(pre, 2 * k + half, mask)
                part = _dot(Gm.astype(BF16), jnp.where(low == (half == 0), xt2, 0.0).astype(BF16), NN)
                acc = part if acc is None else acc + part
            pieces.append(acc)
        HP = st[...]
        hp_ref[...] = HP
        yoff = pre["EA"] * _dot(Cb, HP.astype(BF16), NN)
        st[...] = HP * pre["EA"][Q - 1:Q, :] + _dot(Bb, (XT * pre["DTE"]).astype(BF16), TN)
        y_ref[...] = jnp.concatenate(pieces, axis=1) + yoff + dk_ref[...] * X

    sp = _ssd_specs(Q, R, P, N, G, inner)
    return _pcall(body, name=name, grid=(G, nc),
                  in_specs=[sp["x"], sp["b"], sp["c"], sp["dtc"], sp["dtr"], sp["pc"], sp["pr"], sp["pc"], sp["pr"], sp["px"]],
                  out_specs=[sp["x"], sp["st"]],
                  out_shape=[jax.ShapeDtypeStruct((L, inner), F32), jax.ShapeDtypeStruct((nc, G, N, RP), F32)],
                  scratch_shapes=[pltpu.VMEM((N, RP), F32)],
                  compiler_params=_cparams(("parallel", "arbitrary")))(act, act, act, dtc, dtr, bias_c, bias_r, alog_c, alog_r, dsk_x)


def _ssd_bwd(act, dtc, dtr, bias_c, bias_r, alog_c, alog_r, dsk_x, hprev, dy, name):
    G, L, R = dtc.shape
    N, P, Q = SSM_STATE, SSM_HEAD_DIM, SSM_CHUNK
    RP = R * P
    inner = G * RP
    nc = L // Q

    def body(x_ref, b_ref, c_ref, dtc_ref, dtr_ref, bc_ref, br_ref, ac_ref, ar_ref, dk_ref, hp_ref, dy_ref,
             dx_ref, db_ref, dc_ref, ddt_ref, gbias_ref, galog_ref, gdsk_ref, dst):
        c = pl.program_id(1)

        @pl.when(c == 0)
        def _():
            dst[...] = jnp.zeros_like(dst)
            gbias_ref[...] = jnp.zeros_like(gbias_ref)
            galog_ref[...] = jnp.zeros_like(galog_ref)
            gdsk_ref[...] = jnp.zeros_like(gdsk_ref)

        pre = _ssd_prelude(dtc_ref, dtr_ref, bc_ref, br_ref, ac_ref, ar_ref, Q, R, P)
        DT, EA, DTE = pre["DT"], pre["EA"], pre["DTE"]
        E_END = EA[Q - 1:Q, :]
        X, DY = x_ref[...], dy_ref[...]
        XT = X * DT
        Bb = b_ref[...].astype(BF16)
        Cb = c_ref[...].astype(BF16)
        CB = _dot(Cb, Bb, NT)
        HP, dH = hp_ref[...], dst[...]
        HPb, dHb = HP.astype(BF16), dH.astype(BF16)
        EDY = EA * DY
        EDYb = EDY.astype(BF16)
        dC = _dot(EDYb, HPb, NT)
        dHP = _dot(Cb, EDYb, TN)
        da_off = EDY * _dot(Cb, HPb, NN)
        Z = _dot(Bb, dHb, NN)
        XD = XT * DTE
        dB = _dot(XD.astype(BF16), dHb, NT)
        dXT = DTE * Z
        t_x = XD * Z
        hh = jnp.sum(dH * HP, axis=0, keepdims=True) * E_END
        dst[...] = dHP + dH * E_END
        mask = lax.broadcasted_iota(jnp.int32, (Q, Q), 0) >= lax.broadcasted_iota(jnp.int32, (Q, Q), 1)
        eye = lax.broadcasted_iota(jnp.int32, (Q, Q), 0) == lax.broadcasted_iota(jnp.int32, (Q, Q), 1)
        low = lax.broadcasted_iota(jnp.int32, (Q, 2 * P), 1) < P
        lane = lax.broadcasted_iota(jnp.int32, (Q, R), 1)
        dCB = jnp.zeros((Q, Q), F32)
        da_mat = jnp.zeros((Q, R), F32)
        pieces = []
        for k in range(R // 2):
            sl = slice(2 * P * k, 2 * P * (k + 1))
            xt2, dy2 = XT[:, sl], DY[:, sl]
            acc = None
            for half in range(2):
                r = 2 * k + half
                sel = low == (half == 0)
                Lm = _ssd_decay_tile(pre, r, mask)
                Gm = CB * Lm
                dyb = jnp.where(sel, dy2, 0.0).astype(BF16)
                part = _dot(Gm.astype(BF16), dyb, TN)
                acc = part if acc is None else acc + part
                dG = jnp.where(mask, _dot(dyb, jnp.where(sel, xt2, 0.0).astype(BF16), NT), 0.0)
                Mm = dG * Gm
                dCB = dCB + dG * Lm
                colsum = jnp.sum(jnp.where(eye, jnp.sum(Mm, axis=0, keepdims=True), 0.0), axis=1, keepdims=True)
                da_mat = jnp.where(lane == r, jnp.sum(Mm, axis=1, keepdims=True) - colsum, da_mat)
            pieces.append(acc)
        dXT = dXT + jnp.concatenate(pieces, axis=1)
        dCBb = dCB.astype(BF16)
        dc_ref[...] = dC + _dot(dCBb, Bb, NN)
        db_ref[...] = dB + _dot(dCBb, Cb, TN)
        dx_ref[...] = dXT * DT + dk_ref[...] * DY
        pad = jnp.zeros((SUBLANES - 1, RP), F32)
        sums = _tri_mm(_head_selector(R, P, False), jnp.concatenate([da_off, t_x, dXT * X, DY * X, hh, pad], axis=0), tri_first=False)
        t = sums[Q:2 * Q]
        da_end = jnp.sum(t, axis=0, keepdims=True) + sums[4 * Q:4 * Q + 1]
        rowi = lax.broadcasted_iota(jnp.int32, (Q, R), 0)
        da_mat = da_mat + sums[:Q] - t + jnp.where(rowi == Q - 1, da_end, 0.0)
        ddtA = _tri_mm(_tri(Q, upper=True), da_mat)
        ddt_raw = (ddtA * pre["A_c"] + sums[2 * Q:3 * Q]) * _sigmoid(pre["raw_c"])
        ddt_ref[...] = ddt_raw
        gbias_ref[...] += jnp.sum(ddt_raw, axis=0, keepdims=True)
        galog_ref[...] += jnp.sum(ddtA * pre["dt_c"], axis=0, keepdims=True) * pre["A_c"]
        gdsk_ref[...] += jnp.sum(sums[3 * Q:4 * Q], axis=0, keepdims=True)

    sp = _ssd_specs(Q, R, P, N, G, inner, rev=lambda c: nc - 1 - c)
    bout = pl.BlockSpec((Q, N), lambda g, c: (nc - 1 - c, g))
    return _pcall(body, name=name, grid=(G, nc),
                  in_specs=[sp["x"], sp["b"], sp["c"], sp["dtc"], sp["dtr"], sp["pc"], sp["pr"], sp["pc"], sp["pr"], sp["px"],
                            sp["st"], sp["x"]],
                  out_specs=[sp["x"], bout, bout, sp["dtc"], sp["pc"], sp["pc"], sp["pc"]],
                  out_shape=[jax.ShapeDtypeStruct((L, inner), F32), jax.ShapeDtypeStruct((L, G * N), F32),
                             jax.ShapeDtypeStruct((L, G * N), F32), jax.ShapeDtypeStruct((G, L, R), F32),
                             jax.ShapeDtypeStruct((G, 1, R), F32), jax.ShapeDtypeStruct((G, 1, R), F32),
                             jax.ShapeDtypeStruct((G, 1, R), F32)],
                  scratch_shapes=[pltpu.VMEM((N, RP), F32)],
                  compiler_params=_cparams(("parallel", "arbitrary")))(
        act, act, act, dtc, dtr, bias_c, bias_r, alog_c, alog_r, dsk_x, hprev, dy)


def _gnorm_fwd(y, z, g, name):
    L, Dn = y.shape
    gs = Dn // SSM_GROUPS
    tl = _row_tile(L, wide=True)

    def body(y_ref, z_ref, g_ref, o_ref):
        for k in range(SSM_GROUPS):
            sl = slice(k * gs, (k + 1) * gs)
            zz = z_ref[:, sl]
            u = y_ref[:, sl] * zz * _sigmoid(zz)
            rstd = lax.rsqrt(jnp.mean(u * u, axis=-1, keepdims=True) + RMS_EPS)
            o_ref[:, sl] = (u * rstd * g_ref[:, sl]).astype(BF16)

    row = pl.BlockSpec((tl, Dn), lambda i: (i, 0))
    return _pcall(body, name=name, grid=(L // tl,), in_specs=[row, row, pl.BlockSpec((1, Dn), lambda i: (0, 0))],
                  out_specs=row, out_shape=jax.ShapeDtypeStruct((L, Dn), BF16),
                  compiler_params=_cparams(("parallel",)))(y, z, g.reshape(1, Dn))


def _gnorm_bwd(y, z, g, dout, name):
    L, Dn = y.shape
    gs = Dn // SSM_GROUPS
    tl = _row_tile(L, wide=True)

    def body(y_ref, z_ref, g_ref, d_ref, dy_ref, dz_ref, dg_ref):
        i = pl.program_id(0)

        @pl.when(i == 0)
        def _():
            dg_ref[...] = jnp.zeros_like(dg_ref)

        for k in range(SSM_GROUPS):
            sl = slice(k * gs, (k + 1) * gs)
            zz = z_ref[:, sl]
            yy = y_ref[:, sl]
            sg = _sigmoid(zz)
            sil = zz * sg
            u = yy * sil
            rstd = lax.rsqrt(jnp.mean(u * u, axis=-1, keepdims=True) + RMS_EPS)
            n = u * rstd
            d = d_ref[:, sl]
            dn = d * g_ref[:, sl]
            du = rstd * (dn - n * jnp.mean(dn * n, axis=-1, keepdims=True))
            dy_ref[:, sl] = du * sil
            dz_ref[:, sl] = (du * yy * sg * (1.0 + zz * (1.0 - sg))).astype(BF16)
            dg_ref[:, sl] += _sum8(d * n)

    row = pl.BlockSpec((tl, Dn), lambda i: (i, 0))
    return _pcall(body, name=name, grid=(L // tl,), in_specs=[row, row, pl.BlockSpec((1, Dn), lambda i: (0, 0)), row],
                  out_specs=[row, row, pl.BlockSpec((SUBLANES, Dn), lambda i: (0, 0))],
                  out_shape=[jax.ShapeDtypeStruct((L, Dn), F32), jax.ShapeDtypeStruct((L, Dn), BF16),
                             jax.ShapeDtypeStruct((SUBLANES, Dn), F32)],
                  compiler_params=_cparams(("arbitrary",)))(y, z, g.reshape(1, Dn), dout)


def _adamw(w, g, m, v, name):
    rows, W = w.shape
    tr = _pick(rows, (512, 256, 128, 64, 32, 16, 8))
    c1 = 1.0 / (1.0 - ADAM_B1 ** ADAM_STEP)
    c2 = 1.0 / (1.0 - ADAM_B2 ** ADAM_STEP)

    def body(w_ref, g_ref, m_ref, v_ref, d_ref, nm_ref, nv_ref):
        g_ = g_ref[...]
        nm = ADAM_B1 * m_ref[...] + (1.0 - ADAM_B1) * g_
        nv = ADAM_B2 * v_ref[...] + (1.0 - ADAM_B2) * (g_ * g_)
        nm_ref[...] = nm
        nv_ref[...] = nv
        d_ref[...] = -ADAM_LR * ((nm * c1) / (jnp.sqrt(nv * c2) + ADAM_EPS) + ADAM_WD * w_ref[...])

    blk = pl.BlockSpec((tr, W), lambda i: (i, 0))
    return _pcall(body, name=name, grid=(rows // tr,), in_specs=[blk] * 4, out_specs=[blk] * 3,
                  out_shape=[jax.ShapeDtypeStruct((rows, W), F32)] * 3, compiler_params=_cparams(("parallel",)))(w, g, m, v)


def _sum_slots(x, name, extra=None):
    n, rows, W = x.shape
    tr = _pick(rows, (512, 256, 128, 64, 32, 16, 8))
    has_extra = extra is not None

    def body(*refs):
        if has_extra:
            e_ref, x_ref, o_ref = refs
            acc = e_ref[...].astype(F32)
            start = 0
        else:
            x_ref, o_ref = refs
            acc = x_ref[0].astype(F32)
            start = 1
        for s in range(start, n):
            acc = acc + x_ref[s].astype(F32)
        o_ref[...] = acc

    blk = pl.BlockSpec((tr, W), lambda i: (i, 0))
    xblk = pl.BlockSpec((n, tr, W), lambda i: (0, i, 0))
    return _pcall(body, name=name, grid=(rows // tr,), in_specs=([blk] if has_extra else []) + [xblk], out_specs=blk,
                  out_shape=jax.ShapeDtypeStruct((rows, W), F32), compiler_params=_cparams(("parallel",)))(
        *(([extra] if has_extra else []) + [x]))


def _add_pairs(a, b, name):
    n, rows, W = a.shape
    tr = _pick(rows, (512, 256, 128, 64, 32, 16, 8))

    def body(a_ref, b_ref, o_ref):
        o_ref[...] = (a_ref[...].astype(F32) + b_ref[...].astype(F32)).astype(BF16)

    blk = pl.BlockSpec((None, tr, W), lambda s, i: (s, i, 0))
    return _pcall(body, name=name, grid=(n, rows // tr), in_specs=[blk, blk], out_specs=blk,
                  out_shape=jax.ShapeDtypeStruct((n, rows, W), BF16), compiler_params=_cparams(("parallel", "parallel")))(a, b)


MESH = pl.DeviceIdType.MESH
HBM_SPEC = pl.BlockSpec(memory_space=pl.ANY)


def _me():
    return lax.axis_index("x"), lax.axis_index("y"), lax.axis_index("c")


def _all_gather(arrs, name):
    n = len(arrs)

    def body(*refs):
        start, forward, finish = _gather_phases(refs[:n], refs[n:2 * n], *refs[2 * n:])
        start()
        forward()
        finish()

    return _pcall(body, name=name, in_specs=[HBM_SPEC] * n, out_specs=[HBM_SPEC] * n,
                  out_shape=_gather_shapes(arrs), scratch_shapes=_gather_sems(n))(*arrs)


def _gather_shapes(arrs):
    return [jax.ShapeDtypeStruct((N_DEV,) + a.shape, a.dtype) for a in arrs]


def _gather_sems(n):
    return [pltpu.SemaphoreType.DMA((7 * n,)), pltpu.SemaphoreType.DMA((7 * n,)), pltpu.SemaphoreType.DMA((n,))]


def _gather_phases(ins, outs, send_sems, recv_sems, local_sems):
    n = len(ins)
    x, y, c = _me()
    me, sib = (x, y, c), (x, y, 1 - c)
    chips = [(1 - x, y), (x, 1 - y), (1 - x, 1 - y)]

    def slot(a, dev):
        return outs[a].at[4 * dev[0] + 2 * dev[1] + dev[2]]

    def copy(a, k, block, to, src=None):
        return pltpu.make_async_remote_copy(src_ref=slot(a, block) if src is None else src, dst_ref=slot(a, block),
                                            send_sem=send_sems.at[a * 7 + k], recv_sem=recv_sems.at[a * 7 + k],
                                            device_id=to, device_id_type=MESH)

    def mine():
        return [pltpu.make_async_copy(ins[a], slot(a, me), local_sems.at[a]) for a in range(n)]

    def first():
        out = []
        for a in range(n):
            out.append(copy(a, 0, me, sib, src=ins[a]))
            out += [copy(a, 1 + j, me, (*chip, c), src=ins[a]) for j, chip in enumerate(chips)]
        return out

    def passed():
        return [copy(a, 4 + j, (*chip, c), sib) for j, chip in enumerate(chips) for a in range(n)]

    def start():
        for cp in mine() + first():
            cp.start()

    def forward():
        fws = passed()
        for j, chip in enumerate(chips):
            for a in range(n):
                copy(a, 1 + j, (*chip, c), me).wait_recv()
                fws[j * n + a].start()

    def finish():
        for a in range(n):
            copy(a, 0, sib, me).wait_recv()
            for j, chip in enumerate(chips):
                copy(a, 4 + j, (*chip, 1 - c), me).wait_recv()
        for cp in first() + passed():
            cp.wait_send()
        for cp in mine():
            cp.wait()

    return start, forward, finish


def _exchange_phases(gs, outs, send_sems, recv_sems, local_sems):
    n = len(gs)
    x, y, c = _me()
    my_slot = 4 * x + 2 * y + c
    flips = [(fx, fy, fc) for fx in (0, 1) for fy in (0, 1) for fc in (0, 1) if fx or fy or fc]

    def peer(f):
        return tuple(1 - v if flip else v for v, flip in zip((x, y, c), f))

    def copies():
        out = []
        for a in range(n):
            for k, f in enumerate(flips):
                px, py, pc = peer(f)
                out.append(pltpu.make_async_remote_copy(
                    src_ref=gs[a].at[4 * px + 2 * py + pc], dst_ref=outs[a].at[my_slot],
                    send_sem=send_sems.at[a * 7 + k], recv_sem=recv_sems.at[a * 7 + k],
                    device_id=(px, py, pc), device_id_type=MESH))
        return out

    def arrivals():
        out = []
        for a in range(n):
            for k, f in enumerate(flips):
                px, py, pc = peer(f)
                slot = outs[a].at[4 * px + 2 * py + pc]
                out.append(pltpu.make_async_remote_copy(src_ref=slot, dst_ref=slot, send_sem=send_sems.at[a * 7 + k],
                                                        recv_sem=recv_sems.at[a * 7 + k], device_id=(px, py, pc),
                                                        device_id_type=MESH))
        return out

    def mine():
        return [pltpu.make_async_copy(gs[a].at[my_slot], outs[a].at[my_slot], local_sems.at[a]) for a in range(n)]

    def start():
        for cp in mine() + copies():
            cp.start()

    def finish():
        for cp in arrivals():
            cp.wait_recv()
        for cp in copies():
            cp.wait_send()
        for cp in mine():
            cp.wait()

    return start, finish


def _rs_sibling(gs, name):
    n = len(gs)

    def body(*refs):
        g_refs, o_refs = refs[:n], refs[n:2 * n]
        send_sems, recv_sems = refs[2 * n:]
        x, y, c = _me()
        sib = (x, y, 1 - c)
        cps = [pltpu.make_async_remote_copy(src_ref=g_refs[a].at[2 * q + (1 - c)], dst_ref=o_refs[a].at[q],
                                            send_sem=send_sems.at[4 * a + q], recv_sem=recv_sems.at[4 * a + q],
                                            device_id=sib, device_id_type=MESH) for a in range(n) for q in range(4)]
        for cp in cps:
            cp.start()
        for cp in cps:
            cp.wait()

    return _pcall(body, name=name, in_specs=[HBM_SPEC] * n, out_specs=[HBM_SPEC] * n,
                  out_shape=[jax.ShapeDtypeStruct((4,) + g.shape[1:], g.dtype) for g in gs],
                  scratch_shapes=[pltpu.SemaphoreType.DMA((4 * n,)), pltpu.SemaphoreType.DMA((4 * n,))])(*gs)


def _rs_chips(ps, name):
    n = len(ps)

    def body(*refs):
        p_refs, o_refs = refs[:n], refs[n:2 * n]
        send_sems, recv_sems = refs[2 * n:]
        x, y, c = _me()
        chips = [(1 - x, y), (x, 1 - y), (1 - x, 1 - y)]
        cps = [pltpu.make_async_remote_copy(src_ref=p_refs[a].at[2 * chip[0] + chip[1]], dst_ref=o_refs[a].at[j],
                                            send_sem=send_sems.at[3 * a + j], recv_sem=recv_sems.at[3 * a + j],
                                            device_id=(*chip, c), device_id_type=MESH)
               for j, chip in enumerate(chips) for a in range(n)]
        for cp in cps:
            cp.start()
        for cp in cps:
            cp.wait()

    return _pcall(body, name=name, in_specs=[HBM_SPEC] * n, out_specs=[HBM_SPEC] * n,
                  out_shape=[jax.ShapeDtypeStruct((3,) + p.shape[1:], p.dtype) for p in ps],
                  scratch_shapes=[pltpu.SemaphoreType.DMA((3 * n,)), pltpu.SemaphoreType.DMA((3 * n,))])(*ps)


def _reduce_scatter(gs, name):
    x, y, c = _me()
    from_sib = _rs_sibling(gs, name + "_sib")
    pairs = []
    for a, (g, fs) in enumerate(zip(gs, from_sib)):
        own = g.reshape((4, 2) + g.shape[1:])
        pairs.append(_add_pairs(jnp.where(c == 0, own[:, 0], own[:, 1]), fs, f"{name}_pair{a}"))
    from_chips = _rs_chips(pairs, name + "_chips")
    return [_sum_slots(fc, f"{name}_sum{a}", extra=lax.dynamic_index_in_dim(p, 2 * x + y, axis=0, keepdims=False))
            for a, (p, fc) in enumerate(zip(pairs, from_chips))]


BIG = ("even_w_in", "even_w_out", "odd_w_in", "odd_w_out", "ffn_w_up", "ffn_w_down", "ple_w_proj", "ple_w_gate")
SMALL_SHARDED = ("even_conv_w", "odd_conv_w", "odd_conv_b", "odd_norm_g", "ffn_conv_w")
REPLICATED = ("even_b_f", "odd_dt_bias", "odd_a_log", "odd_d_skip", "ln_mix_g", "ln_mix_b", "ffn_conv_b",
              "ln_ffn_g", "ln_ffn_b", "ple_b_gate")
WEIGHTS = ("even_w_in", "even_b_f", "even_conv_w", "even_w_out", "odd_w_in", "odd_conv_w", "odd_conv_b", "odd_dt_bias",
           "odd_a_log", "odd_d_skip", "odd_norm_g", "odd_w_out", "ln_mix_g", "ln_mix_b", "ffn_w_up", "ffn_conv_w",
           "ffn_conv_b", "ffn_w_down", "ln_ffn_g", "ln_ffn_b", "ple_w_proj", "ple_w_gate", "ple_b_gate")


def _full_shapes():
    d = _dims()
    return {
        "even_w_in": ((1, D_MODEL, d["even_in"]), 2), "even_b_f": ((1, FOX_HEADS), None),
        "even_conv_w": ((1, CONV_WIDTH, CONV_DIM), 2), "even_w_out": ((1, d["even_mix"], D_MODEL), 1),
        "odd_w_in": ((1, D_MODEL, d["odd_in"]), 2), "odd_conv_w": ((1, SSM_CONV_WIDTH, d["conv_ch"]), 2),
        "odd_conv_b": ((1, d["conv_ch"]), 1), "odd_dt_bias": ((1, d["ssm_heads"]), None),
        "odd_a_log": ((1, d["ssm_heads"]), None), "odd_d_skip": ((1, d["ssm_heads"]), None),
        "odd_norm_g": ((1, d["ssm_inner"]), 1), "odd_w_out": ((1, d["ssm_inner"], D_MODEL), 1),
        "ln_mix_g": ((DEPTH, D_MODEL), None), "ln_mix_b": ((DEPTH, D_MODEL), None),
        "ffn_w_up": ((DEPTH, D_MODEL, 2 * D_FF), 2), "ffn_conv_w": ((DEPTH, FFN_CONV_WIDTH, 2 * D_FF), 2),
        "ffn_conv_b": ((DEPTH, 2 * D_FF), None), "ffn_w_down": ((DEPTH, D_FF, D_MODEL), 1),
        "ln_ffn_g": ((DEPTH, D_MODEL), None), "ln_ffn_b": ((DEPTH, D_MODEL), None),
        "ple_w_proj": ((DEPTH, PLE_DIM, D_MODEL), 2), "ple_w_gate": ((DEPTH, D_MODEL, D_MODEL), 1),
        "ple_b_gate": ((DEPTH, D_MODEL), None),
    }


def _shard_shape(name):
    shape, ax = _full_shapes()[name]
    if ax is None:
        return shape
    return tuple(s // N_DEV if i == ax else s for i, s in enumerate(shape))


def _as2d(a, lead=0):
    return a.reshape(a.shape[:lead] + (-1, a.shape[-1]))


def _part_rows(shape):
    n = int(np.prod(shape))
    return -(-(-(-n // PACK_W)) // SUBLANES) * SUBLANES


def _pack_small(parts):
    out = []
    for p in parts:
        n, rows = int(np.prod(p.shape)), _part_rows(p.shape)
        out.append(jnp.pad(p.reshape(-1).astype(F32), (0, rows * PACK_W - n)).reshape(rows, PACK_W))
    return jnp.concatenate(out, axis=0)


def _unpack_small(pack, shapes):
    lead = pack.shape[:-2]
    out, off = [], 0
    for s in shapes:
        n, rows = int(np.prod(s)), _part_rows(s)
        part = pack[..., off:off + rows, :].reshape(lead + (-1,))[..., :n]
        out.append(part.reshape(lead + tuple(s)))
        off += rows
    return out


def _assemble(gathered, name):
    shape, ax = _full_shapes()[name]
    return jnp.moveaxis(gathered, 0, ax).reshape(shape)


def _split_dest(full, name):
    shape, ax = _full_shapes()[name]
    sh = shape[:ax] + (N_DEV, shape[ax] // N_DEV) + shape[ax + 1:]
    return jnp.moveaxis(full.reshape(sh), ax, 0)


def _interleave_cols(w, parts, tc):
    C = w.shape[-1] // parts
    sh = w.shape[:-1]
    return w.reshape(sh + (parts, C // tc, tc)).swapaxes(-3, -2).reshape(sh + (parts * C,))


def _deinterleave_cols(w, parts, tc):
    C = w.shape[-1] // parts
    sh = w.shape[:-1]
    return w.reshape(sh + (C // tc, parts, tc)).swapaxes(-3, -2).reshape(sh + (parts * C,))


def _pad_cols(a, to):
    return jnp.pad(a, ((0, 0), (0, to - a.shape[1])))


def _tail_fwd(i, h_in, mix, p_i, W, sp):
    r1, h1, h1b = _ln_fwd(h_in, mix, sp["ln_mix_g"][i], sp["ln_mix_b"][i], f"ln_mix_fwd{i}")
    U = _mm(h1b, W["ffn_up"][i], "nn", F32, f"ffn_up{i}")
    S = _ffn_act_fwd(U, sp["ffn_conv_w"][i], sp["ffn_conv_b"][i], f"ffn_act_fwd{i}")
    ffn = _mm(S, W["ffn_down"][i], "nn", F32, f"ffn_down{i}")
    r2, h2, h2b = _ln_fwd(h1, ffn, sp["ln_ffn_g"][i], sp["ln_ffn_b"][i], f"ln_ffn_fwd{i}")
    G = _mm(h2b, W["ple_gate"][i], "nn", F32, f"ple_gate{i}")
    E = _mm(p_i, W["ple_proj"][i], "nn", F32, f"ple_proj{i}")
    h3, h3b = _ple_fwd(h2, G, sp["ple_b_gate"][i], E, f"ple_fwd{i}")
    return h3, h3b, dict(r1=r1, h1b=h1b, U=U, S=S, r2=r2, h2b=h2b, G=G, E=E, p=p_i)


def _tail_bwd(i, dh3, sv, W, sp, grads):
    alpha = _alpha()
    dE, dGp, dbg = _ple_bwd(dh3, sv["G"], sp["ple_b_gate"][i], sv["E"], f"ple_bwd{i}")
    grads["ple_b_gate"][i] = dbg.sum(0)
    grads["ple_w_proj"][i] = _mm(sv["p"], dE, "tn", F32, f"d_ple_proj{i}")
    grads["ple_w_gate"][i] = _mm(sv["h2b"], dGp, "tn", F32, f"d_ple_gate{i}")
    dh2 = _mm(dGp, W["ple_gate"][i], "nt", F32, f"dx_ple_gate{i}", add=dh3)
    dr2, dr2b, dg, db = _ln_bwd(sv["r2"], dh2, sp["ln_ffn_g"][i], f"ln_ffn_bwd{i}")
    grads["ln_ffn_g"][i], grads["ln_ffn_b"][i] = dg.sum(0), db.sum(0)
    grads["ffn_w_down"][i] = _mm(sv["S"], dr2b, "tn", F32, f"d_ffn_down{i}")
    dS = _mm(dr2b, W["ffn_down"][i], "nt", BF16, f"dx_ffn_down{i}")
    dUg, dUv, dwg, dwv, dbg, dbv = _ffn_act_bwd(sv["U"], dS, sp["ffn_conv_w"][i], sp["ffn_conv_b"][i], f"ffn_act_bwd{i}")
    K = FFN_CONV_WIDTH
    grads["ffn_conv_w"][i] = jnp.concatenate([dwg.reshape(K, SUBLANES, -1).sum(1), dwv.reshape(K, SUBLANES, -1).sum(1)], axis=1)
    grads["ffn_conv_b"][i] = jnp.concatenate([dbg.sum(0), dbv.sum(0)])
    grads["ffn_w_up"][i] = jnp.concatenate([_mm(sv["h1b"], dUg, "tn", F32, f"d_ffn_up_g{i}"),
                                            _mm(sv["h1b"], dUv, "tn", F32, f"d_ffn_up_v{i}")], axis=1)
    dh1 = _mm(dUg, W["ffn_up"][i], "nt", F32, f"dx_ffn_up_g{i}", add=dr2, add_scale=alpha)
    dh1 = _mm(dUv, W["ffn_up"][i], "nt", F32, f"dx_ffn_up_v{i}", add=dh1, b_k_start=D_FF)
    dr1, dr1b, dg, db = _ln_bwd(sv["r1"], dh1, sp["ln_mix_g"][i], f"ln_mix_bwd{i}")
    grads["ln_mix_g"][i], grads["ln_mix_b"][i] = dg.sum(0), db.sum(0)
    return dr1, dr1b


def _even_fwd(h, W, sp, full, gather):
    L = h.shape[0]
    H, Dh = FOX_HEADS, FOX_HEAD_DIM
    Ac = _mm(h, W["even_in_conv"], "nn", F32, "even_in_conv")
    qkv = _mm(h, W["even_in_qkv"], "nn", BF16, "even_in_qkv")
    Af = _mm(h, W["even_in_f"], "nn", F32, "even_in_f")
    y_a = _sconv_fwd(Ac, sp["even_conv_w_il"], "sconv_fwd")
    Fc = _fox_gate_fwd(Af, sp["even_b_f_pad"], "fox_gate_fwd")
    Fh = Fc[:, :H].T
    Fq, Fk = Fh.reshape(H, L, 1), Fh.reshape(H, 1, L)
    o, lse, *gathered = _attn_fwd(qkv, Fq, Fk, "attn_fwd", gather)
    for (n, i), g in zip(_late_units(), gathered):
        full[n][i] = _assemble_unit(g, n)
    _prepare_late(W, full)
    Y = jnp.concatenate([y_a, o], axis=1)
    mix = _mm(Y, W["even_out"], "nn", F32, "even_out")
    return mix, dict(h=h, Ac=Ac, Af=Af, qkv=qkv, Fq=Fq, Fk=Fk, lse=lse, Y=Y)


def _even_bwd(dmix, dres, sv, W, sp, grads, exchange):
    H, Dh = FOX_HEADS, FOX_HEAD_DIM
    C = CONV_DIM
    L = dmix.shape[0]
    grads["even_w_out"][0] = _mm(sv["Y"], dmix, "tn", F32, "d_even_out")
    exchange = [_split_unit(grads["even_w_out"][0], "even_w_out").astype(BF16)] + list(exchange)
    dY = _mm(dmix, W["even_out"], "nt", F32, "dx_even_out")
    dya = dY[:, :C]
    do = dY[:, C:].astype(BF16)
    dc, dcw = _sconv_bwd_dc(sv["Ac"], dya, sp["even_conv_w_il"], "sconv_bwd_dc")
    grads["even_conv_w"][0] = dcw.reshape(CONV_WIDTH, SUBLANES, -1).sum(1)
    dAc = _sconv_bwd_da(sv["Ac"], dya, dc, sp["even_conv_w_il"], "sconv_bwd_da")
    dq, dk, dv, dFk, *arrived = _attn_bwd(sv["qkv"], sv["Fq"], sv["Fk"], sv["lse"], do, "attn_bwd", exchange)
    dqkv = jnp.concatenate([dq, dk.astype(BF16), dv.astype(BF16)], axis=1)
    dF = _pad_cols(dFk.reshape(H, L).T, LANES)
    dAf, dbf = _fox_gate_bwd(sv["Af"], sp["even_b_f_pad"], dF, "fox_gate_bwd")
    grads["even_b_f"][0] = dbf.sum(0)[:H]
    h = sv["h"]
    gc = _deinterleave_cols(_mm(h, dAc, "tn", F32, "d_even_in_conv"), 3, LANES)
    gq = _mm(h, dqkv, "tn", F32, "d_even_in_qkv")
    gf = _mm(h, dAf, "tn", F32, "d_even_in_f")[:, :H]
    grads["even_w_in"][0] = jnp.concatenate([gc, gq, gf], axis=1)
    dh = _mm(dAc, W["even_in_conv"], "nt", F32, "dx_even_in_conv", add=dres, add_scale=_alpha())
    dh = _mm(dqkv, W["even_in_qkv"], "nt", F32, "dx_even_in_qkv", add=dh)
    dh = _mm(dAf, W["even_in_f"], "nt", F32, "dx_even_in_f", add=dh)
    return dh, arrived


def _group_layouts(v, G):
    R = v.shape[0] // G
    return v.reshape(G, 1, R), v.reshape(G, R, 1)


def _odd_fwd(h, W, sp):
    d = _dims()
    L = h.shape[0]
    Hs, G, N, P = d["ssm_heads"], SSM_GROUPS, SSM_STATE, SSM_HEAD_DIM
    R = Hs // G
    inner = d["ssm_inner"]
    z = _mm(h, W["odd_in_z"], "nn", F32, "odd_in_z")
    xr = _mm(h, W["odd_in_x"], "nn", F32, "odd_in_x")
    dtp = _mm(h, W["odd_in_dt"], "nn", F32, "odd_in_dt")
    act = _mconv_fwd(xr, sp["odd_conv_w"], sp["odd_conv_b"], "mconv_fwd")
    dtg = dtp[:, :Hs].reshape(L, G, R)
    dtc, dtr = dtg.transpose(1, 0, 2), dtg.transpose(1, 2, 0)
    dsk_x = jnp.repeat(sp["odd_d_skip"], P).reshape(G, 1, R * P)
    ssd_in = (act, dtc, dtr) + _group_layouts(sp["odd_dt_bias"], G) + _group_layouts(sp["odd_a_log"], G) + (dsk_x,)
    y, hprev = _ssd_fwd(*ssd_in, "ssd_fwd")
    u = _gnorm_fwd(y, z, sp["odd_norm_g"], "gnorm_fwd")
    mix = _mm(u, W["odd_out"], "nn", F32, "odd_out")
    return mix, dict(h=h, z=z, xr=xr, ssd_in=ssd_in, hprev=hprev, y=y, u=u)


def _odd_bwd(dmix, dres, sv, W, sp, grads):
    d = _dims()
    L = dmix.shape[0]
    Hs, G, N, P = d["ssm_heads"], SSM_GROUPS, SSM_STATE, SSM_HEAD_DIM
    grads["odd_w_out"][0] = _mm(sv["u"], dmix, "tn", F32, "d_odd_out")
    du = _mm(dmix, W["odd_out"], "nt", F32, "dx_odd_out")
    dy, dz, dg = _gnorm_bwd(sv["y"], sv["z"], sp["odd_norm_g"], du, "gnorm_bwd")
    grads["odd_norm_g"][0] = dg.sum(0)
    dxs, dB, dC, ddt, gbias, galog, gdsk = _ssd_bwd(*sv["ssd_in"], sv["hprev"], dy, "ssd_bwd")
    grads["odd_dt_bias"][0] = gbias.reshape(Hs)
    grads["odd_a_log"][0] = galog.reshape(Hs)
    grads["odd_d_skip"][0] = gdsk.reshape(Hs)
    dact = jnp.concatenate([dxs, dB, dC], axis=1)
    dxr, dcw, dcb = _mconv_bwd(sv["xr"], dact, sp["odd_conv_w"], sp["odd_conv_b"], "mconv_bwd")
    grads["odd_conv_w"][0] = dcw.reshape(SSM_CONV_WIDTH, SUBLANES, -1).sum(1)
    grads["odd_conv_b"][0] = dcb.sum(0)
    ddtp = _pad_cols(ddt.transpose(1, 0, 2).reshape(L, Hs), W["odd_in_dt"].shape[1])
    h = sv["h"]
    gz = _mm(h, dz, "tn", F32, "d_odd_in_z")
    gx = _mm(h, dxr, "tn", F32, "d_odd_in_x")
    gdt = _mm(h, ddtp, "tn", F32, "d_odd_in_dt")[:, :Hs]
    grads["odd_w_in"][0] = jnp.concatenate([gz, gx, gdt], axis=1)
    dh = _mm(dz, W["odd_in_z"], "nt", F32, "dx_odd_in_z", add=dres, add_scale=_alpha())
    dh = _mm(dxr, W["odd_in_x"], "nt", F32, "dx_odd_in_x", add=dh)
    dh = _mm(ddtp, W["odd_in_dt"], "nt", F32, "dx_odd_in_dt", add=dh)
    return dh


FIRST_UNIT = ("even_w_in", 0)


def _late_units():
    return [(n, i) for n in BIG for i in range(_full_shapes()[n][0][0]) if (n, i) != FIRST_UNIT]


def _assemble_unit(gathered, name):
    shape, ax = _full_shapes()[name]
    g = gathered.reshape((N_DEV,) + _shard_shape(name)[1:])
    return jnp.moveaxis(g, 0, ax - 1).reshape(shape[1:])


def _split_unit(full_layer, name):
    shape, ax = _full_shapes()[name]
    sh = shape[1:ax] + (N_DEV, shape[ax] // N_DEV) + shape[ax + 1:]
    return _as2d(jnp.moveaxis(full_layer.reshape(sh), ax - 1, 0), 1)


def _prepare_first(W, full):
    C, fd = CONV_DIM, _dims()["fox_dim"]
    ew = full["even_w_in"][0]
    W["even_in_conv"] = _interleave_cols(ew[:, :3 * C], 3, LANES)
    W["even_in_qkv"] = ew[:, 3 * C:3 * C + 3 * fd]
    W["even_in_f"] = _pad_cols(ew[:, 3 * C + 3 * fd:], LANES)


def _prepare_late(W, full):
    d = _dims()
    W["even_out"] = full["even_w_out"][0]
    ow = full["odd_w_in"][0]
    inner, cch, Hs = d["ssm_inner"], d["conv_ch"], d["ssm_heads"]
    W["odd_in_z"] = ow[:, :inner]
    W["odd_in_x"] = ow[:, inner:inner + cch]
    W["odd_in_dt"] = _pad_cols(ow[:, inner + cch:], -(-Hs // LANES) * LANES)
    W["odd_out"] = full["odd_w_out"][0]
    for key, name in (("ffn_up", "ffn_w_up"), ("ffn_down", "ffn_w_down"), ("ple_proj", "ple_w_proj"), ("ple_gate", "ple_w_gate")):
        W[key] = list(full[name])


def _prepare_small(full):
    sp = {}
    sp["even_conv_w_il"] = full["even_conv_w"][0]
    sp["even_b_f_pad"] = _pad_cols(full["even_b_f"], LANES)
    sp["odd_conv_w"] = full["odd_conv_w"][0]
    sp["odd_conv_b"] = full["odd_conv_b"][0]
    sp["odd_norm_g"] = full["odd_norm_g"][0]
    for n in ("odd_dt_bias", "odd_a_log", "odd_d_skip"):
        sp[n] = full[n][0]
    for n in ("ln_mix_g", "ln_mix_b", "ln_ffn_g", "ln_ffn_b", "ple_b_gate"):
        sp[n] = full[n]
    sp["ffn_conv_w"] = [full["ffn_conv_w"][i] for i in range(DEPTH)]
    sp["ffn_conv_b"] = [full["ffn_conv_b"][i] for i in range(DEPTH)]
    return sp


def _local_step(x, p, target, full, late_shards):
    sp = _prepare_small(full)
    W = {}
    _prepare_first(W, full)
    grads = {n: [None] * _full_shapes()[n][0][0] for n in WEIGHTS}
    pb = p.astype(BF16)
    mix0, sv_e = _even_fwd(x.astype(BF16), W, sp, full, late_shards)
    h3_0, h3_0b, sv_t0 = _tail_fwd(0, x, mix0, pb[0], W, sp)
    mix1, sv_o = _odd_fwd(h3_0b, W, sp)
    h3_1, _, sv_t1 = _tail_fwd(1, h3_0, mix1, pb[1], W, sp)
    dh, sq = _loss_head(h3_1, target, "loss_head")
    dr1, dr1b = _tail_bwd(1, dh, sv_t1, W, sp, grads)
    dh = _odd_bwd(dr1b, dr1, sv_o, W, sp, grads)
    dr1, dr1b = _tail_bwd(0, dh, sv_t0, W, sp, grads)
    outgoing = [_split_unit(grads[n][i], n).astype(BF16) for n, i in _late_units()[1:]]
    dx, incoming = _even_bwd(dr1b, dr1, sv_e, W, sp, grads, outgoing)
    reduced = {u: _sum_slots(r, f"sum_grads_{u[0]}{u[1]}") for u, r in zip(_late_units(), incoming)}
    grads = {n: v if n in BIG else jnp.stack(v) for n, v in grads.items()}
    return jnp.sum(sq), dx, grads, reduced


def kernel(x, p, even_w_in, even_b_f, even_conv_w, even_w_out, odd_w_in, odd_conv_w, odd_conv_b, odd_dt_bias, odd_a_log, odd_d_skip, odd_norm_g, odd_w_out, ln_mix_g, ln_mix_b, ffn_w_up, ffn_conv_w, ffn_conv_b, ffn_w_down, ln_ffn_g, ln_ffn_b, ple_w_proj, ple_w_gate, ple_b_gate, loss_target, m_even_w_in, m_even_b_f, m_even_conv_w, m_even_w_out, m_odd_w_in, m_odd_conv_w, m_odd_conv_b, m_odd_dt_bias, m_odd_a_log, m_odd_d_skip, m_odd_norm_g, m_odd_w_out, m_ln_mix_g, m_ln_mix_b, m_ffn_w_up, m_ffn_conv_w, m_ffn_conv_b, m_ffn_w_down, m_ln_ffn_g, m_ln_ffn_b, m_ple_w_proj, m_ple_w_gate, m_ple_b_gate, v_even_w_in, v_even_b_f, v_even_conv_w, v_even_w_out, v_odd_w_in, v_odd_conv_w, v_odd_conv_b, v_odd_dt_bias, v_odd_a_log, v_odd_d_skip, v_odd_norm_g, v_odd_w_out, v_ln_mix_g, v_ln_mix_b, v_ffn_w_up, v_ffn_conv_w, v_ffn_conv_b, v_ffn_w_down, v_ln_ffn_g, v_ln_ffn_b, v_ple_w_proj, v_ple_w_gate, v_ple_b_gate):
    args = locals()
    w = {n: args[n] for n in WEIGHTS}
    m = {n: args["m_" + n] for n in WEIGHTS}
    v = {n: args["v_" + n] for n in WEIGHTS}
    me = 4 * lax.axis_index("x") + 2 * lax.axis_index("y") + lax.axis_index("c")

    def shard(unit):
        return _as2d(w[unit[0]][unit[1]]).astype(BF16)

    first, small = _all_gather([shard(FIRST_UNIT), _pack_small([w[n] for n in SMALL_SHARDED])], "ag_weights")
    full = dict(w)
    for n in BIG:
        full[n] = [None] * _full_shapes()[n][0][0]
    full[FIRST_UNIT[0]][FIRST_UNIT[1]] = _assemble_unit(first, FIRST_UNIT[0])
    for n, g in zip(SMALL_SHARDED, _unpack_small(small, [_shard_shape(n) for n in SMALL_SHARDED])):
        full[n] = _assemble(g, n)

    sq, dx, grads, reduced = _local_step(x[0], p[:, 0], loss_target[0], full, [shard(u) for u in _late_units()])
    loss = lax.psum(0.5 * sq / D_MODEL, ("x", "y", "c"))

    n0, i0 = FIRST_UNIT
    (reduced[FIRST_UNIT],) = _reduce_scatter([_split_unit(grads[n0][i0], n0).astype(BF16)], "rs_grads")
    g_final = {n: jnp.stack([reduced[(n, i)] for i in range(_full_shapes()[n][0][0])]).reshape(_shard_shape(n)) for n in BIG}
    small_names = SMALL_SHARDED + REPLICATED
    (small_all,) = _all_gather([_pack_small([grads[n] for n in small_names])], "ag_small_grads")
    small_sum = _sum_slots(small_all, "sum_small_grads")
    for n, g in zip(small_names, _unpack_small(small_sum, [_full_shapes()[n][0] for n in small_names])):
        g_final[n] = lax.dynamic_index_in_dim(_split_dest(g, n), me, axis=0, keepdims=False) if n in SMALL_SHARDED else g

    out = {}
    for n in BIG:
        res = _adamw(*[_as2d(t[n]) for t in (w, g_final, m, v)], "adamw_" + n)
        out[n] = [r.reshape(_shard_shape(n)) for r in res]
    shapes = [_shard_shape(n) for n in small_names]
    res = _adamw(*[_pack_small([t[n] for n in small_names]) for t in (w, g_final, m, v)], "adamw_small")
    for n, d_, m_, v_ in zip(small_names, *[_unpack_small(r, shapes) for r in res]):
        out[n] = [d_, m_, v_]
    return (loss, dx[None], *[g_final[n] for n in WEIGHTS], *[out[n][0] for n in WEIGHTS],
            *[out[n][1] for n in WEIGHTS], *[out[n][2] for n in WEIGHTS])
```

```python
import jax
import jax.numpy as jnp
import numpy as np
from jax import lax
from jax.experimental import pallas as pl
from jax.experimental.pallas import tpu as pltpu

D_MODEL = 1024
SEQ = 8192
DEPTH = 2
CONV_DIM = 512
CONV_WIDTH = 3
FOX_HEADS = 8
FOX_HEAD_DIM = 64
SSM_HEAD_DIM = 64
SSM_GROUPS = 4
SSM_STATE = 128
SSM_CONV_WIDTH = 4
SSM_CHUNK = 128
D_FF = 2816
FFN_CONV_WIDTH = 3
PLE_DIM = 256
LN_EPS = 1e-5
RMS_EPS = 1e-5
ADAM_LR = 0.001
ADAM_B1 = 0.9
ADAM_B2 = 0.999
ADAM_EPS = 1e-08
ADAM_WD = 0.01
ADAM_STEP = 10
N_DEV = 8

F32 = jnp.float32
BF16 = jnp.bfloat16
NEG = -1e30
LANES = 128
SUBLANES = 8
PACK_W = 1024
VMEM_LIMIT = 48 * 1024 * 1024
ATTN_BWD_VMEM_LIMIT = 56 * 1024 * 1024


def _dims():
    fox_dim = FOX_HEADS * FOX_HEAD_DIM
    ssm_inner = 2 * D_MODEL
    ssm_heads = ssm_inner // SSM_HEAD_DIM
    conv_ch = ssm_inner + 2 * SSM_GROUPS * SSM_STATE
    return dict(fox_dim=fox_dim, even_in=3 * CONV_DIM + 3 * fox_dim + FOX_HEADS, even_mix=CONV_DIM + fox_dim,
                ssm_inner=ssm_inner, ssm_heads=ssm_heads, conv_ch=conv_ch, odd_in=ssm_inner + conv_ch + ssm_heads)


def _alpha():
    return (2.0 * DEPTH) ** 0.25


def _pick(dim, prefs):
    for p in prefs:
        if dim % p == 0:
            return p
    return dim


def _pcall(body, **kw):
    return pl.pallas_call(body, **kw)


def _cparams(sem=None, **kw):
    if sem is not None:
        kw["dimension_semantics"] = sem
    return pltpu.CompilerParams(vmem_limit_bytes=VMEM_LIMIT, **kw)


def _sigmoid(x):
    return 1.0 / (1.0 + jnp.exp(-x))


def _softplus(x):
    return jnp.maximum(x, 0.0) + jnp.log(1.0 + jnp.exp(-jnp.abs(x)))


def _sum8(x):
    n, c = x.shape
    return x.reshape(n // SUBLANES, SUBLANES, c).sum(axis=0)


def _dot(a, b, dims):
    return lax.dot_general(a, b, (dims, ((), ())), preferred_element_type=F32)


NN = ((1,), (0,))
NT = ((1,), (1,))
TN = ((0,), (0,))


def _split3(x):
    hi = x.astype(BF16)
    r1 = x - hi.astype(F32)
    mid = r1.astype(BF16)
    lo = (r1 - mid.astype(F32)).astype(BF16)
    return hi, mid, lo


def _tri_mm(tri_bf16, x, tri_first=True):
    if tri_first:
        return sum(_dot(tri_bf16, part, NN) for part in _split3(x))
    return sum(_dot(part, tri_bf16, NN) for part in _split3(x))


def _tri(n, upper=False):
    r = lax.broadcasted_iota(jnp.int32, (n, n), 0)
    c = lax.broadcasted_iota(jnp.int32, (n, n), 1)
    return jnp.where((r <= c) if upper else (r >= c), 1.0, 0.0).astype(BF16)


def _shift_down(cur, prev8, k):
    if k == 0:
        return cur
    ext = jnp.concatenate([prev8, cur], axis=0)
    return pltpu.roll(ext, k, axis=0)[SUBLANES:]


def _shift_up(cur, next8, k):
    if k == 0:
        return cur
    n = cur.shape[0]
    ext = jnp.concatenate([cur, next8], axis=0)
    return pltpu.roll(ext, n + SUBLANES - k, axis=0)[:n]


def _mm(a, b, mode, out_dtype, name, add=None, add_scale=1.0, b_k_start=0):
    if mode == "nn":
        (M, K), (K2, N) = a.shape, b.shape
    elif mode == "nt":
        (M, K), N, K2 = a.shape, b.shape[0], a.shape[1]
        assert b.shape[1] >= b_k_start + K, (a.shape, b.shape, b_k_start)
    else:
        (K, M), (K2, N) = a.shape, b.shape
    assert K == K2, (a.shape, b.shape, mode)
    tm = _pick(M, (1024, 1408, 512, 256, 128))
    tn = _pick(N, (1408, 1024, 768, 512, 384, 256, 128))
    tk = K if K <= 2048 and b_k_start % K == 0 else _pick(K, (1408, 1024, 768, 512, 256, 128))
    assert b_k_start % tk == 0
    k0 = b_k_start // tk
    nk = K // tk
    dims = {"nn": NN, "nt": NT, "tn": TN}[mode]
    a_spec = pl.BlockSpec((tk, tm), lambda i, j, k: (k, i)) if mode == "tn" else pl.BlockSpec((tm, tk), lambda i, j, k: (i, k))
    b_spec = pl.BlockSpec((tn, tk), lambda i, j, k: (j, k + k0)) if mode == "nt" else pl.BlockSpec((tk, tn), lambda i, j, k: (k, j))
    o_spec = pl.BlockSpec((tm, tn), lambda i, j, k: (i, j))
    has_add = add is not None

    def body(*refs):
        a_ref, b_ref = refs[:2]
        add_ref = refs[2] if has_add else None
        o_ref = refs[2 + has_add]
        prod = _dot(a_ref[...].astype(BF16), b_ref[...].astype(BF16), dims)
        if nk == 1:
            if has_add:
                prod = prod + add_scale * add_ref[...].astype(F32)
            o_ref[...] = prod.astype(out_dtype)
            return
        acc = refs[3 + has_add]
        k = pl.program_id(2)

        @pl.when(k == 0)
        def _():
            if has_add:
                acc[...] = prod + add_scale * add_ref[...].astype(F32)
            else:
                acc[...] = prod

        @pl.when(k > 0)
        def _():
            acc[...] += prod

        @pl.when(k == nk - 1)
        def _():
            o_ref[...] = acc[...].astype(out_dtype)

    ins = [a, b] + ([add] if has_add else [])
    specs = [a_spec, b_spec] + ([o_spec] if has_add else [])
    return _pcall(body, name=name, grid=(M // tm, N // tn, nk), in_specs=specs, out_specs=o_spec,
                  out_shape=jax.ShapeDtypeStruct((M, N), out_dtype),
                  scratch_shapes=[pltpu.VMEM((tm, tn), F32)] if nk > 1 else [],
                  compiler_params=_cparams(("parallel", "parallel", "arbitrary")))(*ins)


def _row_tile(L, wide=False):
    return _pick(L, (256, 128)) if wide else _pick(L, (512, 256, 128))


def _conv_row_tile(L, backward):
    return _pick(L, (2048, 1024, 512, 256, 128)) if backward else _pick(L, (1024, 512, 256, 128))


def _ln_fwd(h, mix, g, b, name):
    L, D = h.shape
    tl = _row_tile(L)
    alpha = _alpha()

    def body(h_ref, m_ref, g_ref, b_ref, r_ref, y_ref, yb_ref):
        r = alpha * h_ref[...] + m_ref[...]
        mu = jnp.mean(r, axis=-1, keepdims=True)
        xc = r - mu
        var = jnp.mean(xc * xc, axis=-1, keepdims=True)
        r_ref[...] = r
        y = xc * lax.rsqrt(var + LN_EPS) * g_ref[...] + b_ref[...]
        y_ref[...] = y
        yb_ref[...] = y.astype(BF16)

    row = pl.BlockSpec((tl, D), lambda i: (i, 0))
    vec = pl.BlockSpec((1, D), lambda i: (0, 0))
    return _pcall(body, name=name, grid=(L // tl,), in_specs=[row, row, vec, vec], out_specs=[row, row, row],
                  out_shape=[jax.ShapeDtypeStruct((L, D), F32)] * 2 + [jax.ShapeDtypeStruct((L, D), BF16)],
                  compiler_params=_cparams(("parallel",)))(h, mix, g.reshape(1, D), b.reshape(1, D))


def _ln_bwd(r, dy, g, name):
    L, D = r.shape
    tl = _row_tile(L)

    def body(r_ref, dy_ref, g_ref, dr_ref, drb_ref, dg_ref, db_ref):
        i = pl.program_id(0)
        r_ = r_ref[...]
        dy_ = dy_ref[...]
        mu = jnp.mean(r_, axis=-1, keepdims=True)
        xc = r_ - mu
        rstd = lax.rsqrt(jnp.mean(xc * xc, axis=-1, keepdims=True) + LN_EPS)
        xhat = xc * rstd
        dxh = dy_ * g_ref[...]
        dr = rstd * (dxh - jnp.mean(dxh, axis=-1, keepdims=True) - xhat * jnp.mean(dxh * xhat, axis=-1, keepdims=True))
        dr_ref[...] = dr
        drb_ref[...] = dr.astype(BF16)

        @pl.when(i == 0)
        def _():
            dg_ref[...] = jnp.zeros_like(dg_ref)
            db_ref[...] = jnp.zeros_like(db_ref)

        dg_ref[...] += _sum8(dy_ * xhat)
        db_ref[...] += _sum8(dy_)

    row = pl.BlockSpec((tl, D), lambda i: (i, 0))
    vec = pl.BlockSpec((1, D), lambda i: (0, 0))
    acc = pl.BlockSpec((SUBLANES, D), lambda i: (0, 0))
    return _pcall(body, name=name, grid=(L // tl,), in_specs=[row, row, vec], out_specs=[row, row, acc, acc],
                  out_shape=[jax.ShapeDtypeStruct((L, D), F32), jax.ShapeDtypeStruct((L, D), BF16),
                             jax.ShapeDtypeStruct((SUBLANES, D), F32), jax.ShapeDtypeStruct((SUBLANES, D), F32)],
                  compiler_params=_cparams(("arbitrary",)))(r, dy, g.reshape(1, D))


def _ple_fwd(h2, G, bg, E, name):
    L, D = h2.shape
    tl = _row_tile(L)

    def body(h_ref, g_ref, b_ref, e_ref, o_ref, ob_ref):
        o = h_ref[...] + _sigmoid(g_ref[...] + b_ref[...]) * e_ref[...]
        o_ref[...] = o
        ob_ref[...] = o.astype(BF16)

    row = pl.BlockSpec((tl, D), lambda i: (i, 0))
    vec = pl.BlockSpec((1, D), lambda i: (0, 0))
    return _pcall(body, name=name, grid=(L // tl,), in_specs=[row, row, vec, row], out_specs=[row, row],
                  out_shape=[jax.ShapeDtypeStruct((L, D), F32), jax.ShapeDtypeStruct((L, D), BF16)],
                  compiler_params=_cparams(("parallel",)))(h2, G, bg.reshape(1, D), E)


def _ple_bwd(dh3, G, bg, E, name):
    L, D = dh3.shape
    tl = _row_tile(L)

    def body(d_ref, g_ref, b_ref, e_ref, de_ref, dg_ref, db_ref):
        i = pl.program_id(0)
        d = d_ref[...]
        sg = _sigmoid(g_ref[...] + b_ref[...])
        de_ref[...] = (d * sg).astype(BF16)
        dgp = d * e_ref[...] * sg * (1.0 - sg)
        dg_ref[...] = dgp.astype(BF16)

        @pl.when(i == 0)
        def _():
            db_ref[...] = jnp.zeros_like(db_ref)

        db_ref[...] += _sum8(dgp)

    row = pl.BlockSpec((tl, D), lambda i: (i, 0))
    vec = pl.BlockSpec((1, D), lambda i: (0, 0))
    acc = pl.BlockSpec((SUBLANES, D), lambda i: (0, 0))
    return _pcall(body, name=name, grid=(L // tl,), in_specs=[row, row, vec, row], out_specs=[row, row, acc],
                  out_shape=[jax.ShapeDtypeStruct((L, D), BF16), jax.ShapeDtypeStruct((L, D), BF16),
                             jax.ShapeDtypeStruct((SUBLANES, D), F32)],
                  compiler_params=_cparams(("arbitrary",)))(dh3, G, bg.reshape(1, D), E)


def _loss_head(h, target, name):
    L, D = h.shape
    tl = _row_tile(L)

    def body(h_ref, t_ref, d_ref, s_ref):
        i = pl.program_id(0)
        e = h_ref[...] - t_ref[...]
        d_ref[...] = e * (1.0 / D)

        @pl.when(i == 0)
        def _():
            s_ref[...] = jnp.zeros_like(s_ref)

        s_ref[...] += _sum8(e * e)

    row = pl.BlockSpec((tl, D), lambda i: (i, 0))
    acc = pl.BlockSpec((SUBLANES, D), lambda i: (0, 0))
    return _pcall(body, name=name, grid=(L // tl,), in_specs=[row, row], out_specs=[row, acc],
                  out_shape=[jax.ShapeDtypeStruct((L, D), F32), jax.ShapeDtypeStruct((SUBLANES, D), F32)],
                  compiler_params=_cparams(("arbitrary",)))(h, target)


def _halo_prev(tl, ncol_blocks_fn):
    return lambda j, i: (jnp.maximum(i * (tl // SUBLANES) - 1, 0), ncol_blocks_fn(j))


STRIP_ROWS = 64


def _strip(s, R):
    return pl.ds(s * R if isinstance(s, int) else pl.multiple_of(s * R, R), R)


def _conv_taps(cur, prev, w_ref, K):
    acc = w_ref[K - 1:K, :] * cur
    for k in range(K - 1):
        acc = acc + w_ref[k:k + 1, :] * _shift_down(cur, prev, K - 1 - k)
    return acc


def _ffn_act_fwd(U, w, b, name):
    L, F2 = U.shape
    F = F2 // 2
    K = w.shape[0]
    tc = _pick(F, (1408, 256, 128))
    tl = _pick(L, (512, 256, 128)) if tc > 256 else _conv_row_tile(L, False)
    nj = F // tc

    def body(ug_ref, uv_ref, ugp_ref, uvp_ref, wg_ref, wv_ref, bg_ref, bv_ref, s_ref):
        i = pl.program_id(1)
        g = _conv_taps(ug_ref[...], jnp.where(i == 0, 0.0, ugp_ref[...]), wg_ref, K) + bg_ref[...]
        v = _conv_taps(uv_ref[...], jnp.where(i == 0, 0.0, uvp_ref[...]), wv_ref, K) + bv_ref[...]
        s_ref[...] = (g * _sigmoid(g) * v).astype(BF16)

    def both(shape, index):
        return [pl.BlockSpec(shape, lambda j, i: index(j, i)), pl.BlockSpec(shape, lambda j, i: index(j + nj, i))]

    b2 = b.reshape(1, F2)
    return _pcall(body, name=name, grid=(nj, L // tl),
                  in_specs=both((tl, tc), lambda j, i: (i, j)) + both((SUBLANES, tc), _halo_prev(tl, lambda j: j))
                  + both((K, tc), lambda j, i: (0, j)) + both((1, tc), lambda j, i: (0, j)),
                  out_specs=pl.BlockSpec((tl, tc), lambda j, i: (i, j)),
                  out_shape=jax.ShapeDtypeStruct((L, F), BF16),
                  compiler_params=_cparams(("parallel", "parallel")))(U, U, U, U, w, w, b2, b2)


def _halo_next(tl, L, rows):
    return lambda j, i: (jnp.minimum((i + 1) * (tl // rows), L // rows - 1), j)


BF16_ROWS = 16


def _ffn_act_bwd(U, dS, w, b, name):
    L, F2 = U.shape
    F = F2 // 2
    K = w.shape[0]
    tc = _pick(F, (256, 128))
    tl = _conv_row_tile(L, True)
    nl = L // tl
    nj = F // tc
    R = STRIP_ROWS
    ns = tl // R

    def body(ug_ref, uv_ref, ugp_ref, uvp_ref, ugn_ref, uvn_ref, ds_ref, dsn_ref, wg_ref, wv_ref, bg_ref, bv_ref,
             dug_ref, duv_ref, dwg_ref, dwv_ref, dbg_ref, dbv_ref):
        i = pl.program_id(1)

        @pl.when(i == 0)
        def _():
            for r in (dwg_ref, dwv_ref, dbg_ref, dbv_ref):
                r[...] = jnp.zeros_like(r)

        for lanes in (slice(c, c + LANES) for c in range(0, tc, LANES)):
            wts = [[w_ref[k:k + 1, lanes] for k in range(K)] for w_ref in (wg_ref, wv_ref)]
            bias = [b_ref[:, lanes] for b_ref in (bg_ref, bv_ref)]

            def strip(s):
                return _strip(s, R)

            def at_conv_out(x, xprev8, ds):
                sh = [[_shift_down(x[h], xprev8[h], K - 1 - k) for k in range(K)] for h in range(2)]
                g, v = (sum(wts[h][k] * sh[h][k] for k in range(K)) + bias[h] for h in range(2))
                sg = _sigmoid(g)
                return (ds * v * sg * (1.0 + g * (1.0 - sg)), ds * g * sg), sh

            def emit(rows, duc, duc_next8):
                for h, du_ref in enumerate((dug_ref, duv_ref)):
                    du = sum(wts[h][k] * _shift_up(duc[h], duc_next8[h], K - 1 - k) for k in range(K))
                    du_ref[rows, lanes] = du.astype(BF16)

            def accumulate(acc, duc, sh):
                dw, db = acc
                return (tuple(tuple(dw[h][k] + _sum8(duc[h] * sh[h][k]) for k in range(K)) for h in range(2)),
                        tuple(db[h] + _sum8(duc[h]) for h in range(2)))

            def last8(x):
                return tuple(a[R - SUBLANES:] for a in x)

            zero = jnp.zeros((SUBLANES, LANES), F32)
            x0 = (ug_ref[strip(0), lanes], uv_ref[strip(0), lanes])
            prev8 = (jnp.where(i == 0, 0.0, ugp_ref[:, lanes]), jnp.where(i == 0, 0.0, uvp_ref[:, lanes]))
            duc0, sh0 = at_conv_out(x0, prev8, ds_ref[strip(0), lanes].astype(F32))
            acc0 = accumulate(((((zero,) * K,) * 2), (zero,) * 2), duc0, sh0)

            def step(s, carry):
                xprev, ducprev, acc = carry
                x = (ug_ref[strip(s), lanes], uv_ref[strip(s), lanes])
                duc, sh = at_conv_out(x, last8(xprev), ds_ref[strip(s), lanes].astype(F32))
                emit(strip(s - 1), ducprev, tuple(d[:SUBLANES] for d in duc))
                return x, duc, accumulate(acc, duc, sh)

            xl, ducl, (dw, db) = lax.fori_loop(1, ns, step, (x0, duc0, acc0))
            ducn, _ = at_conv_out((ugn_ref[:, lanes], uvn_ref[:, lanes]), last8(xl), dsn_ref[:, lanes].astype(F32)[:SUBLANES])
            emit(strip(ns - 1), ducl, tuple(jnp.where(i == nl - 1, 0.0, d) for d in ducn))
            for h, (dw_ref, db_ref) in enumerate(((dwg_ref, dbg_ref), (dwv_ref, dbv_ref))):
                db_ref[:, lanes] += db[h]
                for k in range(K):
                    dw_ref[k * SUBLANES:(k + 1) * SUBLANES, lanes] += dw[h][k]

    def both(shape, index):
        return [pl.BlockSpec(shape, lambda j, i: index(j, i)), pl.BlockSpec(shape, lambda j, i: index(j + nj, i))]

    b2 = b.reshape(1, F2)
    du_specs, du_shapes = [pl.BlockSpec((tl, tc), lambda j, i: (i, j))] * 2, [jax.ShapeDtypeStruct((L, F), BF16)] * 2
    dw_specs = [pl.BlockSpec((K * SUBLANES, tc), lambda j, i: (0, j))] * 2
    dw_shapes = [jax.ShapeDtypeStruct((K * SUBLANES, F), F32)] * 2
    db_specs, db_shapes = [pl.BlockSpec((SUBLANES, tc), lambda j, i: (0, j))] * 2, [jax.ShapeDtypeStruct((SUBLANES, F), F32)] * 2
    return _pcall(body, name=name, grid=(nj, nl),
                  in_specs=both((tl, tc), lambda j, i: (i, j)) + both((SUBLANES, tc), _halo_prev(tl, lambda j: j))
                  + both((SUBLANES, tc), _halo_next(tl, L, SUBLANES))
                  + [pl.BlockSpec((tl, tc), lambda j, i: (i, j)), pl.BlockSpec((BF16_ROWS, tc), _halo_next(tl, L, BF16_ROWS))]
                  + both((K, tc), lambda j, i: (0, j)) + both((1, tc), lambda j, i: (0, j)),
                  out_specs=du_specs + dw_specs + db_specs, out_shape=du_shapes + dw_shapes + db_shapes,
                  compiler_params=_cparams(("parallel", "arbitrary")))(U, U, U, U, U, U, dS, dS, w, w, b2, b2)


def _sconv_fwd(Ac, w, name):
    L, C3 = Ac.shape
    C = C3 // 3
    K = w.shape[0]
    tl = _conv_row_tile(L, False)
    tc = LANES

    def body(a_ref, ap_ref, w_ref, y_ref):
        i = pl.program_id(1)
        a = a_ref[...]
        ap = ap_ref[...]
        p = a[:, tc:2 * tc] * a[:, 2 * tc:]
        pp = jnp.where(i == 0, 0.0, ap[:, tc:2 * tc] * ap[:, 2 * tc:])
        y_ref[...] = (a[:, :tc] * _conv_taps(p, pp, w_ref, K)).astype(BF16)

    return _pcall(body, name=name, grid=(C // tc, L // tl),
                  in_specs=[pl.BlockSpec((tl, 3 * tc), lambda j, i: (i, j)),
                            pl.BlockSpec((SUBLANES, 3 * tc), _halo_prev(tl, lambda j: j)),
                            pl.BlockSpec((K, tc), lambda j, i: (0, j))],
                  out_specs=pl.BlockSpec((tl, tc), lambda j, i: (i, j)),
                  out_shape=jax.ShapeDtypeStruct((L, C), BF16),
                  compiler_params=_cparams(("parallel", "parallel")))(Ac, Ac, w)


def _sconv_bwd_dc(Ac, dy, w, name):
    L, C3 = Ac.shape
    C = C3 // 3
    K = w.shape[0]
    tl = _conv_row_tile(L, False)
    tc = LANES

    def body(a_ref, ap_ref, dy_ref, dc_ref, dw_ref):
        i = pl.program_id(1)
        a = a_ref[...]
        ap = ap_ref[...]
        p = a[:, tc:2 * tc] * a[:, 2 * tc:]
        pp = jnp.where(i == 0, 0.0, ap[:, tc:2 * tc] * ap[:, 2 * tc:])
        dc = dy_ref[...] * a[:, :tc]
        dc_ref[...] = dc

        @pl.when(i == 0)
        def _():
            dw_ref[...] = jnp.zeros_like(dw_ref)

        for k in range(K):
            dw_ref[k * SUBLANES:(k + 1) * SUBLANES, :] += _sum8(dc * _shift_down(p, pp, K - 1 - k))

    return _pcall(body, name=name, grid=(C // tc, L // tl),
                  in_specs=[pl.BlockSpec((tl, 3 * tc), lambda j, i: (i, j)),
                            pl.BlockSpec((SUBLANES, 3 * tc), _halo_prev(tl, lambda j: j)),
                            pl.BlockSpec((tl, tc), lambda j, i: (i, j))],
                  out_specs=[pl.BlockSpec((tl, tc), lambda j, i: (i, j)),
                             pl.BlockSpec((K * SUBLANES, tc), lambda j, i: (0, j))],
                  out_shape=[jax.ShapeDtypeStruct((L, C), F32), jax.ShapeDtypeStruct((K * SUBLANES, C), F32)],
                  compiler_params=_cparams(("parallel", "arbitrary")))(Ac, Ac, dy)


def _sconv_bwd_da(Ac, dy, dc, w, name):
    L, C3 = Ac.shape
    C = C3 // 3
    K = w.shape[0]
    tl = _conv_row_tile(L, False)
    tc = LANES
    nl = L // tl

    def body(a_ref, ap_ref, dy_ref, dc_ref, dcn_ref, w_ref, o_ref):
        i = pl.program_id(1)
        a = a_ref[...]
        ap = ap_ref[...]
        gc, h = a[:, tc:2 * tc], a[:, 2 * tc:]
        p = gc * h
        pp = jnp.where(i == 0, 0.0, ap[:, tc:2 * tc] * ap[:, 2 * tc:])
        dgb = dy_ref[...] * _conv_taps(p, pp, w_ref, K)
        cur = dc_ref[...]
        nxt = jnp.where(i == nl - 1, 0.0, dcn_ref[...])
        dp = w_ref[K - 1:K, :] * cur
        for k in range(K - 1):
            dp = dp + w_ref[k:k + 1, :] * _shift_up(cur, nxt, K - 1 - k)
        o_ref[...] = jnp.concatenate([dgb, dp * h, dp * gc], axis=1).astype(BF16)

    return _pcall(body, name=name, grid=(C // tc, nl),
                  in_specs=[pl.BlockSpec((tl, 3 * tc), lambda j, i: (i, j)),
                            pl.BlockSpec((SUBLANES, 3 * tc), _halo_prev(tl, lambda j: j)),
                            pl.BlockSpec((tl, tc), lambda j, i: (i, j)),
                            pl.BlockSpec((tl, tc), lambda j, i: (i, j)),
                            pl.BlockSpec((SUBLANES, tc), lambda j, i: (jnp.minimum((i + 1) * (tl // SUBLANES), L // SUBLANES - 1), j)),
                            pl.BlockSpec((K, tc), lambda j, i: (0, j))],
                  out_specs=pl.BlockSpec((tl, 3 * tc), lambda j, i: (i, j)),
                  out_shape=jax.ShapeDtypeStruct((L, C3), BF16),
                  compiler_params=_cparams(("parallel", "parallel")))(Ac, Ac, dy, dc, dc, w)


def _fox_gate_fwd(Af, bf, name):
    L, W = Af.shape
    tl = _pick(L, (512, 256, 128))

    def body(a_ref, b_ref, f_ref, carry):
        i = pl.program_id(0)

        @pl.when(i == 0)
        def _():
            carry[...] = jnp.zeros_like(carry)

        z = a_ref[...] + b_ref[...]
        logf = jnp.minimum(z, 0.0) - jnp.log(1.0 + jnp.exp(-jnp.abs(z)))
        f = _tri_mm(_tri(tl), logf) + carry[...]
        f_ref[...] = f
        carry[...] = f[tl - 1:tl, :]

    row = pl.BlockSpec((tl, W), lambda i: (i, 0))
    return _pcall(body, name=name, grid=(L // tl,), in_specs=[row, pl.BlockSpec((1, W), lambda i: (0, 0))], out_specs=row,
                  out_shape=jax.ShapeDtypeStruct((L, W), F32), scratch_shapes=[pltpu.VMEM((1, W), F32)],
                  compiler_params=_cparams(("arbitrary",)))(Af, bf)


def _fox_gate_bwd(Af, bf, dF, name):
    L, W = Af.shape
    tl = _pick(L, (512, 256, 128))
    nl = L // tl

    def body(a_ref, b_ref, df_ref, o_ref, db_ref, carry):
        i = pl.program_id(0)

        @pl.when(i == 0)
        def _():
            carry[...] = jnp.zeros_like(carry)
            db_ref[...] = jnp.zeros_like(db_ref)

        z = a_ref[...] + b_ref[...]
        dlogf = _tri_mm(_tri(tl, upper=True), df_ref[...]) + carry[...]
        carry[...] = dlogf[0:1, :]
        dz = dlogf * _sigmoid(-z)
        o_ref[...] = dz
        db_ref[...] += _sum8(dz)

    row = pl.BlockSpec((tl, W), lambda i: (nl - 1 - i, 0))
    return _pcall(body, name=name, grid=(nl,),
                  in_specs=[row, pl.BlockSpec((1, W), lambda i: (0, 0)), row],
                  out_specs=[row, pl.BlockSpec((SUBLANES, W), lambda i: (0, 0))],
                  out_shape=[jax.ShapeDtypeStruct((L, W), F32), jax.ShapeDtypeStruct((SUBLANES, W), F32)],
                  scratch_shapes=[pltpu.VMEM((1, W), F32)],
                  compiler_params=_cparams(("arbitrary",)))(Af, bf, dF)


def _attn_tiles(L):
    t = _pick(L, (512, 256, 128))
    return t, t


def _attn_scores(q, k, fq, fk, diag, scale):
    s = _dot(q, k, NT) * scale + (fq - fk)
    if not diag:
        return s
    row = lax.broadcasted_iota(jnp.int32, s.shape, 0)
    col = lax.broadcasted_iota(jnp.int32, s.shape, 1)
    return jnp.where(col <= row, s, NEG)


def _attn_geometry():
    Dh = FOX_HEAD_DIM
    hpt = LANES // Dh
    return Dh, hpt, FOX_HEADS // hpt


def _head_lanes(shape, Dh, hpt):
    lane = lax.broadcasted_iota(jnp.int32, shape, len(shape) - 1)
    return [(lane >= h * Dh) & (lane < (h + 1) * Dh) for h in range(hpt)]


def _attn_specs(t, L, hpt, ng):
    return dict(
        col=lambda off: pl.BlockSpec((t, LANES), lambda g, i: (i, g + off)),
        full=lambda off: pl.BlockSpec((L, LANES), lambda g, i: (0, g + off)),
        hq=pl.BlockSpec((hpt, t, 1), lambda g, i: (g, i, 0)),
        hk_full=pl.BlockSpec((hpt, 1, L), lambda g, i: (g, 0, 0)),
        hk=pl.BlockSpec((hpt, 1, t), lambda g, i: (g, 0, i)))


def _attn_fwd(qkv, Fq, Fk, name, gather=()):
    L = qkv.shape[0]
    Dh, hpt, ng = _attn_geometry()
    t, _ = _attn_tiles(L)
    scale = Dh ** -0.5
    n = len(gather)
    nsteps = ng * (L // t)

    def body(*refs):
        q_ref, k_ref, v_ref, fq_ref, fk_ref = refs[:5]
        o_ref, lse_ref = refs[5 + n:7 + n]
        qi = pl.program_id(1)
        step = pl.program_id(0) * (L // t) + qi
        if n:
            start, forward, finish = _gather_phases(refs[5:5 + n], refs[7 + n:7 + 2 * n], *refs[7 + 2 * n:])
            pl.when(step == 0)(start)
            pl.when(step == nsteps // 2)(forward)
        sel = _head_lanes((t, LANES), Dh, hpt)
        q2 = q_ref[...]
        qh = [jnp.where(sel[h], q2, 0) for h in range(hpt)]
        fq = [fq_ref[h] for h in range(hpt)]

        def chunk(j, carry, diag):
            rows = pl.ds(pl.multiple_of(j * t, t), t)
            kc, vc = k_ref[rows, :], v_ref[rows, :]
            out = []
            for h in range(hpt):
                m, l, acc = carry[h]
                s = _attn_scores(qh[h], kc, fq[h], fk_ref[h, :, rows], diag, scale)
                m_new = jnp.maximum(m, jnp.max(s, axis=-1, keepdims=True))
                p = jnp.exp(s - m_new)
                a = jnp.exp(m - m_new)
                out.append((m_new, a * l + jnp.sum(p, axis=-1, keepdims=True), a * acc + _dot(p.astype(BF16), vc, NN)))
            return tuple(out)

        init = tuple((jnp.full((t, 1), NEG, F32), jnp.zeros((t, 1), F32), jnp.zeros((t, LANES), F32)) for _ in range(hpt))
        fin = chunk(qi, lax.fori_loop(0, qi, lambda j, c: chunk(j, c, False), init), True)
        o = jnp.zeros((t, LANES), F32)
        for h, (m, l, acc) in enumerate(fin):
            o = jnp.where(sel[h], acc / l, o)
            lse_ref[h] = m + jnp.log(l)
        o_ref[...] = o.astype(BF16)
        if n:
            pl.when(step == nsteps - 1)(finish)

    sp = _attn_specs(t, L, hpt, ng)
    return _pcall(body, name=name, grid=(ng, L // t),
                  in_specs=[sp["col"](0), sp["full"](ng), sp["full"](2 * ng), sp["hq"], sp["hk_full"]] + [HBM_SPEC] * n,
                  out_specs=[sp["col"](0), sp["hq"]] + [HBM_SPEC] * n,
                  out_shape=[jax.ShapeDtypeStruct((L, ng * LANES), BF16), jax.ShapeDtypeStruct((FOX_HEADS, L, 1), F32)]
                  + _gather_shapes(gather),
                  scratch_shapes=_gather_sems(n) if n else [],
                  compiler_params=_cparams(("arbitrary", "arbitrary") if n else ("parallel", "arbitrary")))(
        qkv, qkv, qkv, Fq, Fk, *gather)


def _attn_bwd(qkv, Fq, Fk, lse, do, name, exchange=()):
    L = qkv.shape[0]
    Dh, hpt, ng = _attn_geometry()
    _, tk = _attn_tiles(L)
    tq = _pick(L, (256, 128))
    nkc = L // tk
    scale = Dh ** -0.5

    n = len(exchange)
    nsteps = ng * (L // tq)

    def body(*refs):
        q_ref, k_ref, v_ref, fq_ref, fk_ref, lse_ref, do_ref = refs[:7]
        dq_ref, dk_ref, dv_ref, df_ref = refs[7 + n:11 + n]
        p_s, dp_s = refs[11 + 2 * n:13 + 2 * n]
        qi = pl.program_id(1)
        step = pl.program_id(0) * (L // tq) + qi
        if n:
            start, finish = _exchange_phases(refs[7:7 + n], refs[11 + n:11 + 2 * n], *refs[13 + 2 * n:])
            pl.when(step == 0)(start)

        @pl.when(qi == 0)
        def _():
            dk_ref[...] = jnp.zeros_like(dk_ref)
            dv_ref[...] = jnp.zeros_like(dv_ref)
            df_ref[...] = jnp.zeros_like(df_ref)

        sel = _head_lanes((tq, LANES), Dh, hpt)
        q2, do2 = q_ref[...], do_ref[...]
        jd = (qi * tq) // tk
        off = qi * tq - jd * tk
        dq = jnp.zeros((tq, LANES), F32)
        for h in range(hpt):
            qh, doh = jnp.where(sel[h], q2, 0), jnp.where(sel[h], do2, 0)
            fq, lse = fq_ref[h], lse_ref[h]

            def first(j, acc, diag):
                rows = pl.ds(pl.multiple_of(j * tk, tk), tk)
                s = _dot(qh, k_ref[rows, :], NT) * scale + (fq - fk_ref[h, :, rows])
                if diag:
                    row = lax.broadcasted_iota(jnp.int32, s.shape, 0) + off
                    s = jnp.where(lax.broadcasted_iota(jnp.int32, s.shape, 1) <= row, s, NEG)
                p = jnp.exp(s - lse)
                dp = _dot(doh, v_ref[rows, :], NT)
                p_s[j] = p
                dp_s[j] = dp
                return acc + jnp.sum(p * dp, axis=-1, keepdims=True)

            delta = first(jd, lax.fori_loop(0, jd, lambda j, c: first(j, c, False), jnp.zeros((tq, 1), F32)), True)

            def second(j, acc):
                rows = pl.ds(pl.multiple_of(j * tk, tk), tk)
                p = p_s[j]
                ds = p * (dp_s[j] - delta)
                dsb = ds.astype(BF16)
                dk_ref[rows, :] += _dot(dsb, qh, TN)
                dv_ref[rows, :] += _dot(p.astype(BF16), doh, TN)
                df_ref[h, :, rows] -= jnp.sum(ds, axis=0, keepdims=True)
                return acc + _dot(dsb, k_ref[rows, :], NN)

            dq = jnp.where(sel[h], lax.fori_loop(0, jd + 1, second, jnp.zeros((tq, LANES), F32)), dq)
        dq_ref[...] = (dq * scale).astype(BF16)

        @pl.when(qi == L // tq - 1)
        def _():
            dk_ref[...] *= scale

        if n:
            pl.when(step == nsteps - 1)(finish)

    sp = _attn_specs(tq, L, hpt, ng)
    return _pcall(body, name=name, grid=(ng, L // tq),
                  in_specs=[sp["col"](0), sp["full"](ng), sp["full"](2 * ng), sp["hq"], sp["hk_full"], sp["hq"], sp["col"](0)]
                  + [HBM_SPEC] * n,
                  out_specs=[sp["col"](0), sp["full"](0), sp["full"](0), sp["hk_full"]] + [HBM_SPEC] * n,
                  out_shape=[jax.ShapeDtypeStruct((L, ng * LANES), BF16), jax.ShapeDtypeStruct((L, ng * LANES), F32),
                             jax.ShapeDtypeStruct((L, ng * LANES), F32), jax.ShapeDtypeStruct((FOX_HEADS, 1, L), F32)]
                  + [jax.ShapeDtypeStruct(g.shape, g.dtype) for g in exchange],
                  scratch_shapes=[pltpu.VMEM((nkc, tq, tk), F32), pltpu.VMEM((nkc, tq, tk), F32)] + (_gather_sems(n) if n else []),
                  compiler_params=pltpu.CompilerParams(
                      vmem_limit_bytes=ATTN_BWD_VMEM_LIMIT,
                      dimension_semantics=("arbitrary", "arbitrary") if n else ("parallel", "arbitrary")))(
        qkv, qkv, qkv, Fq, Fk, lse, do, *exchange)


def _mconv_fwd(xr, w, b, name):
    L, C = xr.shape
    K = w.shape[0]
    tl = _conv_row_tile(L, False)
    tc = _pick(C, (512, 384, 256, 128))

    def body(x_ref, xp_ref, w_ref, b_ref, o_ref):
        i = pl.program_id(1)
        prev = jnp.where(i == 0, 0.0, xp_ref[...])
        pre = _conv_taps(x_ref[...], prev, w_ref, K) + b_ref[...]
        o_ref[...] = pre * _sigmoid(pre)

    return _pcall(body, name=name, grid=(C // tc, L // tl),
                  in_specs=[pl.BlockSpec((tl, tc), lambda j, i: (i, j)),
                            pl.BlockSpec((SUBLANES, tc), _halo_prev(tl, lambda j: j)),
                            pl.BlockSpec((K, tc), lambda j, i: (0, j)),
                            pl.BlockSpec((1, tc), lambda j, i: (0, j))],
                  out_specs=pl.BlockSpec((tl, tc), lambda j, i: (i, j)),
                  out_shape=jax.ShapeDtypeStruct((L, C), F32),
                  compiler_params=_cparams(("parallel", "parallel")))(xr, xr, w, b.reshape(1, C))


def _mconv_bwd(xr, dact, w, b, name):
    L, C = xr.shape
    K = w.shape[0]
    tl = _conv_row_tile(L, True)
    tc = _pick(C, (512, 384, 256, 128))
    nl = L // tl

    R = STRIP_ROWS
    ns = tl // R

    def body(x_ref, xp_ref, xn_ref, d_ref, dn_ref, w_ref, b_ref, o_ref, dw_ref, db_ref):
        i = pl.program_id(1)

        @pl.when(i == 0)
        def _():
            dw_ref[...] = jnp.zeros_like(dw_ref)
            db_ref[...] = jnp.zeros_like(db_ref)

        for lanes in (slice(c, c + LANES) for c in range(0, tc, LANES)):
            wts = [w_ref[k:k + 1, lanes] for k in range(K)]
            bias = b_ref[:, lanes]

            def at_conv_out(x, xprev8, d):
                sh = [_shift_down(x, xprev8, K - 1 - k) for k in range(K)]
                pre = sum(wts[k] * sh[k] for k in range(K)) + bias
                sg = _sigmoid(pre)
                return d * sg * (1.0 + pre * (1.0 - sg)), sh

            def emit(rows, dpre, dpre_next8):
                o_ref[rows, lanes] = sum(wts[k] * _shift_up(dpre, dpre_next8, K - 1 - k) for k in range(K)).astype(BF16)

            def accumulate(acc, dpre, sh):
                return tuple(acc[k] + _sum8(dpre * sh[k]) for k in range(K)) + (acc[K] + _sum8(dpre),)

            x0 = x_ref[_strip(0, R), lanes]
            dpre0, sh0 = at_conv_out(x0, jnp.where(i == 0, 0.0, xp_ref[:, lanes]), d_ref[_strip(0, R), lanes])
            acc0 = accumulate((jnp.zeros((SUBLANES, LANES), F32),) * (K + 1), dpre0, sh0)

            def step(s, carry):
                xprev, dprev, acc = carry
                x = x_ref[_strip(s, R), lanes]
                dpre, sh = at_conv_out(x, xprev[R - SUBLANES:], d_ref[_strip(s, R), lanes])
                emit(_strip(s - 1, R), dprev, dpre[:SUBLANES])
                return x, dpre, accumulate(acc, dpre, sh)

            xl, dl, acc = lax.fori_loop(1, ns, step, (x0, dpre0, acc0))
            dn, _ = at_conv_out(xn_ref[:, lanes], xl[R - SUBLANES:], dn_ref[:, lanes])
            emit(_strip(ns - 1, R), dl, jnp.where(i == nl - 1, 0.0, dn))
            db_ref[:, lanes] += acc[K]
            for k in range(K):
                dw_ref[k * SUBLANES:(k + 1) * SUBLANES, lanes] += acc[k]

    return _pcall(body, name=name, grid=(C // tc, nl),
                  in_specs=[pl.BlockSpec((tl, tc), lambda j, i: (i, j)),
                            pl.BlockSpec((SUBLANES, tc), _halo_prev(tl, lambda j: j)),
                            pl.BlockSpec((SUBLANES, tc), _halo_next(tl, L, SUBLANES)),
                            pl.BlockSpec((tl, tc), lambda j, i: (i, j)),
                            pl.BlockSpec((SUBLANES, tc), _halo_next(tl, L, SUBLANES)),
                            pl.BlockSpec((K, tc), lambda j, i: (0, j)),
                            pl.BlockSpec((1, tc), lambda j, i: (0, j))],
                  out_specs=[pl.BlockSpec((tl, tc), lambda j, i: (i, j)),
                             pl.BlockSpec((K * SUBLANES, tc), lambda j, i: (0, j)),
                             pl.BlockSpec((SUBLANES, tc), lambda j, i: (0, j))],
                  out_shape=[jax.ShapeDtypeStruct((L, C), BF16), jax.ShapeDtypeStruct((K * SUBLANES, C), F32),
                             jax.ShapeDtypeStruct((SUBLANES, C), F32)],
                  compiler_params=_cparams(("parallel", "arbitrary")))(xr, xr, xr, dact, dact, w, b.reshape(1, C))


def _head_selector(R, P, heads_first):
    shape = (R, R * P) if heads_first else (R * P, R)
    head = lax.broadcasted_iota(jnp.int32, shape, 0 if heads_first else 1)
    lane = lax.broadcasted_iota(jnp.int32, shape, 1 if heads_first else 0)
    d = lane - head * P
    return jnp.where((d >= 0) & (d < P), 1.0, 0.0).astype(BF16)


def _ssd_prelude(dtc_ref, dtr_ref, bc_ref, br_ref, ac_ref, ar_ref, Q, R, P):
    raw_c = dtc_ref[...] + bc_ref[...]
    dt_c = _softplus(raw_c)
    dt_r = _softplus(dtr_ref[...] + br_ref[...])
    A_c = -jnp.exp(ac_ref[...])
    acs_c = _tri_mm(_tri(Q), dt_c * A_c)
    acs_r = _tri_mm(_tri(Q, upper=True), dt_r * (-jnp.exp(ar_ref[...])), tri_first=False)
    ea_c = jnp.exp(acs_c)
    dte_c = jnp.exp(acs_c[Q - 1:Q, :] - acs_c)
    wide = _tri_mm(_head_selector(R, P, True), jnp.concatenate([dt_c, ea_c, dte_c], axis=0), tri_first=False)
    return dict(raw_c=raw_c, dt_c=dt_c, A_c=A_c, acs_c=acs_c, acs_r=acs_r, ea_c=ea_c,
                DT=wide[:Q], EA=wide[Q:2 * Q], DTE=wide[2 * Q:])


def _ssd_decay_tile(pre, r, mask):
    return jnp.exp(jnp.where(mask, pre["acs_c"][:, r:r + 1] - pre["acs_r"][r:r + 1, :], NEG))


def _ssd_specs(Q, R, P, N, G, inner, rev=None):
    cc = (lambda c: c) if rev is None else rev
    return dict(
        x=pl.BlockSpec((Q, R * P), lambda g, c: (cc(c), g)),
        b=pl.BlockSpec((Q, N), lambda g, c: (cc(c), inner // N + g)),
        c=pl.BlockSpec((Q, N), lambda g, c: (cc(c), inner // N + G + g)),
        dtc=pl.BlockSpec((None, Q, R), lambda g, c: (g, cc(c), 0)),
        dtr=pl.BlockSpec((None, R, Q), lambda g, c: (g, 0, cc(c))),
        pc=pl.BlockSpec((None, 1, R), lambda g, c: (g, 0, 0)),
        pr=pl.BlockSpec((None, R, 1), lambda g, c: (g, 0, 0)),
        px=pl.BlockSpec((None, 1, R * P), lambda g, c: (g, 0, 0)),
        st=pl.BlockSpec((None, None, N, R * P), lambda g, c: (cc(c), g, 0, 0)))


def _ssd_fwd(act, dtc, dtr, bias_c, bias_r, alog_c, alog_r, dsk_x, name):
    G, L, R = dtc.shape
    N, P, Q = SSM_STATE, SSM_HEAD_DIM, SSM_CHUNK
    RP = R * P
    inner = G * RP
    nc = L // Q

    def body(x_ref, b_ref, c_ref, dtc_ref, dtr_ref, bc_ref, br_ref, ac_ref, ar_ref, dk_ref, y_ref, hp_ref, st):
        c = pl.program_id(1)

        @pl.when(c == 0)
        def _():
            st[...] = jnp.zeros_like(st)

        pre = _ssd_prelude(dtc_ref, dtr_ref, bc_ref, br_ref, ac_ref, ar_ref, Q, R, P)
        X = x_ref[...]
        XT = X * pre["DT"]
        Bb = b_ref[...].astype(BF16)
        Cb = c_ref[...].astype(BF16)
        CB = _dot(Cb, Bb, NT)
        mask = lax.broadcasted_iota(jnp.int32, (Q, Q), 0) >= lax.broadcasted_iota(jnp.int32, (Q, Q), 1)
        low = lax.broadcasted_iota(jnp.int32, (Q, 2 * P), 1) < P
        pieces = []
        for k in range(R // 2):
            xt2 = XT[:, 2 * P * k:2 * P * (k + 1)]
            acc = None
            for half in range(2):
                Gm = CB * _ssd_decay_tile(pre, 2 * k + half, mask)
                part = _dot(Gm.astype(BF16), jnp.where(low == (half == 0), xt2, 0.0).astype(BF16), NN)
                acc = part if acc is None else acc + part
            pieces.append(acc)
        HP = st[...]
        hp_ref[...] = HP
        yoff = pre["EA"] * _dot(Cb, HP.astype(BF16), NN)
        st[...] = HP * pre["EA"][Q - 1:Q, :] + _dot(Bb, (XT * pre["DTE"]).astype(BF16), TN)
        y_ref[...] = jnp.concatenate(pieces, axis=1) + yoff + dk_ref[...] * X

    sp = _ssd_specs(Q, R, P, N, G, inner)
    return _pcall(body, name=name, grid=(G, nc),
                  in_specs=[sp["x"], sp["b"], sp["c"], sp["dtc"], sp["dtr"], sp["pc"], sp["pr"], sp["pc"], sp["pr"], sp["px"]],
                  out_specs=[sp["x"], sp["st"]],
                  out_shape=[jax.ShapeDtypeStruct((L, inner), F32), jax.ShapeDtypeStruct((nc, G, N, RP), F32)],
                  scratch_shapes=[pltpu.VMEM((N, RP), F32)],
                  compiler_params=_cparams(("parallel", "arbitrary")))(act, act, act, dtc, dtr, bias_c, bias_r, alog_c, alog_r, dsk_x)


def _ssd_bwd(act, dtc, dtr, bias_c, bias_r, alog_c, alog_r, dsk_x, hprev, dy, name):
    G, L, R = dtc.shape
    N, P, Q = SSM_STATE, SSM_HEAD_DIM, SSM_CHUNK
    RP = R * P
    inner = G * RP
    nc = L // Q

    def body(x_ref, b_ref, c_ref, dtc_ref, dtr_ref, bc_ref, br_ref, ac_ref, ar_ref, dk_ref, hp_ref, dy_ref,
             dx_ref, db_ref, dc_ref, ddt_ref, gbias_ref, galog_ref, gdsk_ref, dst):
        c = pl.program_id(1)

        @pl.when(c == 0)
        def _():
            dst[...] = jnp.zeros_like(dst)
            gbias_ref[...] = jnp.zeros_like(gbias_ref)
            galog_ref[...] = jnp.zeros_like(galog_ref)
            gdsk_ref[...] = jnp.zeros_like(gdsk_ref)

        pre = _ssd_prelude(dtc_ref, dtr_ref, bc_ref, br_ref, ac_ref, ar_ref, Q, R, P)
        DT, EA, DTE = pre["DT"], pre["EA"], pre["DTE"]
        E_END = EA[Q - 1:Q, :]
        X, DY = x_ref[...], dy_ref[...]
        XT = X * DT
        Bb = b_ref[...].astype(BF16)
        Cb = c_ref[...].astype(BF16)
        CB = _dot(Cb, Bb, NT)
        HP, dH = hp_ref[...], dst[...]
        HPb, dHb = HP.astype(BF16), dH.astype(BF16)
        EDY = EA * DY
        EDYb = EDY.astype(BF16)
        dC = _dot(EDYb, HPb, NT)
        dHP = _dot(Cb, EDYb, TN)
        da_off = EDY * _dot(Cb, HPb, NN)
        Z = _dot(Bb, dHb, NN)
        XD = XT * DTE
        dB = _dot(XD.astype(BF16), dHb, NT)
        dXT = DTE * Z
        t_x = XD * Z
        hh = jnp.sum(dH * HP, axis=0, keepdims=True) * E_END
        dst[...] = dHP + dH * E_END
        mask = lax.broadcasted_iota(jnp.int32, (Q, Q), 0) >= lax.broadcasted_iota(jnp.int32, (Q, Q), 1)
        eye = lax.broadcasted_iota(jnp.int32, (Q, Q), 0) == lax.broadcasted_iota(jnp.int32, (Q, Q), 1)
        low = lax.broadcasted_iota(jnp.int32, (Q, 2 * P), 1) < P
        lane = lax.broadcasted_iota(jnp.int32, (Q, R), 1)
        dCB = jnp.zeros((Q, Q), F32)
        da_mat = jnp.zeros((Q, R), F32)
        pieces = []
        for k in range(R // 2):
            sl = slice(2 * P * k, 2 * P * (k + 1))
            xt2, dy2 = XT[:, sl], DY[:, sl]
            acc = None
            for half in range(2):
                r = 2 * k + half
                sel = low == (half == 0)
                Lm = _ssd_decay_tile(pre, r, mask)
                Gm = CB * Lm
                dyb = jnp.where(sel, dy2, 0.0).astype(BF16)
                part = _dot(Gm.astype(BF16), dyb, TN)
                acc = part if acc is None else acc + part
                dG = jnp.where(mask, _dot(dyb, jnp.where(sel, xt2, 0.0).astype(BF16), NT), 0.0)
                Mm = dG * Gm
                dCB = dCB + dG * Lm
                colsum = jnp.sum(jnp.where(eye, jnp.sum(Mm, axis=0, keepdims=True), 0.0), axis=1, keepdims=True)
                da_mat = jnp.where(lane == r, jnp.sum(Mm, axis=1, keepdims=True) - colsum, da_mat)
            pieces.append(acc)
        dXT = dXT + jnp.concatenate(pieces, axis=1)
        dCBb = dCB.astype(BF16)
        dc_ref[...] = dC + _dot(dCBb, Bb, NN)
        db_ref[...] = dB + _dot(dCBb, Cb, TN)
        dx_ref[...] = dXT * DT + dk_ref[...] * DY
        pad = jnp.zeros((SUBLANES - 1, RP), F32)
        sums = _tri_mm(_head_selector(R, P, False), jnp.concatenate([da_off, t_x, dXT * X, DY * X, hh, pad], axis=0), tri_first=False)
        t = sums[Q:2 * Q]
        da_end = jnp.sum(t, axis=0, keepdims=True) + sums[4 * Q:4 * Q + 1]
        rowi = lax.broadcasted_iota(jnp.int32, (Q, R), 0)
        da_mat = da_mat + sums[:Q] - t + jnp.where(rowi == Q - 1, da_end, 0.0)
        ddtA = _tri_mm(_tri(Q, upper=True), da_mat)
        ddt_raw = (ddtA * pre["A_c"] + sums[2 * Q:3 * Q]) * _sigmoid(pre["raw_c"])
        ddt_ref[...] = ddt_raw
        gbias_ref[...] += jnp.sum(ddt_raw, axis=0, keepdims=True)
        galog_ref[...] += jnp.sum(ddtA * pre["dt_c"], axis=0, keepdims=True) * pre["A_c"]
        gdsk_ref[...] += jnp.sum(sums[3 * Q:4 * Q], axis=0, keepdims=True)

    sp = _ssd_specs(Q, R, P, N, G, inner, rev=lambda c: nc - 1 - c)
    bout = pl.BlockSpec((Q, N), lambda g, c: (nc - 1 - c, g))
    return _pcall(body, name=name, grid=(G, nc),
                  in_specs=[sp["x"], sp["b"], sp["c"], sp["dtc"], sp["dtr"], sp["pc"], sp["pr"], sp["pc"], sp["pr"], sp["px"],
                            sp["st"], sp["x"]],
                  out_specs=[sp["x"], bout, bout, sp["dtc"], sp["pc"], sp["pc"], sp["pc"]],
                  out_shape=[jax.ShapeDtypeStruct((L, inner), F32), jax.ShapeDtypeStruct((L, G * N), F32),
                             jax.ShapeDtypeStruct((L, G * N), F32), jax.ShapeDtypeStruct((G, L, R), F32),
                             jax.ShapeDtypeStruct((G, 1, R), F32), jax.ShapeDtypeStruct((G, 1, R), F32),
                             jax.ShapeDtypeStruct((G, 1, R), F32)],
                  scratch_shapes=[pltpu.VMEM((N, RP), F32)],
                  compiler_params=_cparams(("parallel", "arbitrary")))(
        act, act, act, dtc, dtr, bias_c, bias_r, alog_c, alog_r, dsk_x, hprev, dy)


def _gnorm_fwd(y, z, g, name):
    L, Dn = y.shape
    gs = Dn // SSM_GROUPS
    tl = _row_tile(L, wide=True)

    def body(y_ref, z_ref, g_ref, o_ref):
        for k in range(SSM_GROUPS):
            sl = slice(k * gs, (k + 1) * gs)
            zz = z_ref[:, sl]
            u = y_ref[:, sl] * zz * _sigmoid(zz)
            rstd = lax.rsqrt(jnp.mean(u * u, axis=-1, keepdims=True) + RMS_EPS)
            o_ref[:, sl] = (u * rstd * g_ref[:, sl]).astype(BF16)

    row = pl.BlockSpec((tl, Dn), lambda i: (i, 0))
    return _pcall(body, name=name, grid=(L // tl,), in_specs=[row, row, pl.BlockSpec((1, Dn), lambda i: (0, 0))],
                  out_specs=row, out_shape=jax.ShapeDtypeStruct((L, Dn), BF16),
                  compiler_params=_cparams(("parallel",)))(y, z, g.reshape(1, Dn))


def _gnorm_bwd(y, z, g, dout, name):
    L, Dn = y.shape
    gs = Dn // SSM_GROUPS
    tl = _row_tile(L, wide=True)

    def body(y_ref, z_ref, g_ref, d_ref, dy_ref, dz_ref, dg_ref):
        i = pl.program_id(0)

        @pl.when(i == 0)
        def _():
            dg_ref[...] = jnp.zeros_like(dg_ref)

        for k in range(SSM_GROUPS):
            sl = slice(k * gs, (k + 1) * gs)
            zz = z_ref[:, sl]
            yy = y_ref[:, sl]
            sg = _sigmoid(zz)
            sil = zz * sg
            u = yy * sil
            rstd = lax.rsqrt(jnp.mean(u * u, axis=-1, keepdims=True) + RMS_EPS)
            n = u * rstd
            d = d_ref[:, sl]
            dn = d * g_ref[:, sl]
            du = rstd * (dn - n * jnp.mean(dn * n, axis=-1, keepdims=True))
            dy_ref[:, sl] = du * sil
            dz_ref[:, sl] = (du * yy * sg * (1.0 + zz * (1.0 - sg))).astype(BF16)
            dg_ref[:, sl] += _sum8(d * n)

    row = pl.BlockSpec((tl, Dn), lambda i: (i, 0))
    return _pcall(body, name=name, grid=(L // tl,), in_specs=[row, row, pl.BlockSpec((1, Dn), lambda i: (0, 0)), row],
                  out_specs=[row, row, pl.BlockSpec((SUBLANES, Dn), lambda i: (0, 0))],
                  out_shape=[jax.ShapeDtypeStruct((L, Dn), F32), jax.ShapeDtypeStruct((L, Dn), BF16),
                             jax.ShapeDtypeStruct((SUBLANES, Dn), F32)],
                  compiler_params=_cparams(("arbitrary",)))(y, z, g.reshape(1, Dn), dout)


def _adamw(w, g, m, v, name):
    rows, W = w.shape
    tr = _pick(rows, (512, 256, 128, 64, 32, 16, 8))
    c1 = 1.0 / (1.0 - ADAM_B1 ** ADAM_STEP)
    c2 = 1.0 / (1.0 - ADAM_B2 ** ADAM_STEP)

    def body(w_ref, g_ref, m_ref, v_ref, d_ref, nm_ref, nv_ref):
        g_ = g_ref[...]
        nm = ADAM_B1 * m_ref[...] + (1.0 - ADAM_B1) * g_
        nv = ADAM_B2 * v_ref[...] + (1.0 - ADAM_B2) * (g_ * g_)
        nm_ref[...] = nm
        nv_ref[...] = nv
        d_ref[...] = -ADAM_LR * ((nm * c1) / (jnp.sqrt(nv * c2) + ADAM_EPS) + ADAM_WD * w_ref[...])

    blk = pl.BlockSpec((tr, W), lambda i: (i, 0))
    return _pcall(body, name=name, grid=(rows // tr,), in_specs=[blk] * 4, out_specs=[blk] * 3,
                  out_shape=[jax.ShapeDtypeStruct((rows, W), F32)] * 3, compiler_params=_cparams(("parallel",)))(w, g, m, v)


def _sum_slots(x, name, extra=None):
    n, rows, W = x.shape
    tr = _pick(rows, (512, 256, 128, 64, 32, 16, 8))
    has_extra = extra is not None

    def body(*refs):
        if has_extra:
            e_ref, x_ref, o_ref = refs
            acc = e_ref[...].astype(F32)
            start = 0
        else:
            x_ref, o_ref = refs
            acc = x_ref[0].astype(F32)
            start = 1
        for s in range(start, n):
            acc = acc + x_ref[s].astype(F32)
        o_ref[...] = acc

    blk = pl.BlockSpec((tr, W), lambda i: (i, 0))
    xblk = pl.BlockSpec((n, tr, W), lambda i: (0, i, 0))
    return _pcall(body, name=name, grid=(rows // tr,), in_specs=([blk] if has_extra else []) + [xblk], out_specs=blk,
                  out_shape=jax.ShapeDtypeStruct((rows, W), F32), compiler_params=_cparams(("parallel",)))(
        *(([extra] if has_extra else []) + [x]))


def _add_pairs(a, b, name):
    n, rows, W = a.shape
    tr = _pick(rows, (512, 256, 128, 64, 32, 16, 8))

    def body(a_ref, b_ref, o_ref):
        o_ref[...] = (a_ref[...].astype(F32) + b_ref[...].astype(F32)).astype(BF16)

    blk = pl.BlockSpec((None, tr, W), lambda s, i: (s, i, 0))
    return _pcall(body, name=name, grid=(n, rows // tr), in_specs=[blk, blk], out_specs=blk,
                  out_shape=jax.ShapeDtypeStruct((n, rows, W), BF16), compiler_params=_cparams(("parallel", "parallel")))(a, b)


MESH = pl.DeviceIdType.MESH
HBM_SPEC = pl.BlockSpec(memory_space=pl.ANY)


def _me():
    return lax.axis_index("x"), lax.axis_index("y"), lax.axis_index("c")


def _all_gather(arrs, name):
    n = len(arrs)

    def body(*refs):
        start, forward, finish = _gather_phases(refs[:n], refs[n:2 * n], *refs[2 * n:])
        start()
        forward()
        finish()

    return _pcall(body, name=name, in_specs=[HBM_SPEC] * n, out_specs=[HBM_SPEC] * n,
                  out_shape=_gather_shapes(arrs), scratch_shapes=_gather_sems(n))(*arrs)


def _gather_shapes(arrs):
    return [jax.ShapeDtypeStruct((N_DEV,) + a.shape, a.dtype) for a in arrs]


def _gather_sems(n):
    return [pltpu.SemaphoreType.DMA((7 * n,)), pltpu.SemaphoreType.DMA((7 * n,)), pltpu.SemaphoreType.DMA((n,))]


def _gather_phases(ins, outs, send_sems, recv_sems, local_sems):
    n = len(ins)
    x, y, c = _me()
    me, sib = (x, y, c), (x, y, 1 - c)
    chips = [(1 - x, y), (x, 1 - y), (1 - x, 1 - y)]

    def slot(a, dev):
        return outs[a].at[4 * dev[0] + 2 * dev[1] + dev[2]]

    def copy(a, k, block, to, src=None):
        return pltpu.make_async_remote_copy(src_ref=slot(a, block) if src is None else src, dst_ref=slot(a, block),
                                            send_sem=send_sems.at[a * 7 + k], recv_sem=recv_sems.at[a * 7 + k],
                                            device_id=to, device_id_type=MESH)

    def mine():
        return [pltpu.make_async_copy(ins[a], slot(a, me), local_sems.at[a]) for a in range(n)]

    def first():
        out = []
        for a in range(n):
            out.append(copy(a, 0, me, sib, src=ins[a]))
            out += [copy(a, 1 + j, me, (*chip, c), src=ins[a]) for j, chip in enumerate(chips)]
        return out

    def passed():
        return [copy(a, 4 + j, (*chip, c), sib) for j, chip in enumerate(chips) for a in range(n)]

    def start():
        for cp in mine() + first():
            cp.start()

    def forward():
        fws = passed()
        for j, chip in enumerate(chips):
            for a in range(n):
                copy(a, 1 + j, (*chip, c), me).wait_recv()
                fws[j * n + a].start()

    def finish():
        for a in range(n):
            copy(a, 0, sib, me).wait_recv()
            for j, chip in enumerate(chips):
                copy(a, 4 + j, (*chip, 1 - c), me).wait_recv()
        for cp in first() + passed():
            cp.wait_send()
        for cp in mine():
            cp.wait()

    return start, forward, finish


def _exchange_phases(gs, outs, send_sems, recv_sems, local_sems):
    n = len(gs)
    x, y, c = _me()
    my_slot = 4 * x + 2 * y + c
    flips = [(fx, fy, fc) for fx in (0, 1) for fy in (0, 1) for fc in (0, 1) if fx or fy or fc]

    def peer(f):
        return tuple(1 - v if flip else v for v, flip in zip((x, y, c), f))

    def copies():
        out = []
        for a in range(n):
            for k, f in enumerate(flips):
                px, py, pc = peer(f)
                out.append(pltpu.make_async_remote_copy(
                    src_ref=gs[a].at[4 * px + 2 * py + pc], dst_ref=outs[a].at[my_slot],
                    send_sem=send_sems.at[a * 7 + k], recv_sem=recv_sems.at[a * 7 + k],
                    device_id=(px, py, pc), device_id_type=MESH))
        return out

    def arrivals():
        out = []
        for a in range(n):
            for k, f in enumerate(flips):
                px, py, pc = peer(f)
                slot = outs[a].at[4 * px + 2 * py + pc]
                out.append(pltpu.make_async_remote_copy(src_ref=slot, dst_ref=slot, send_sem=send_sems.at[a * 7 + k],
                                                        recv_sem=recv_sems.at[a * 7 + k], device_id=(px, py, pc),
                                                        device_id_type=MESH))
        return out

    def mine():
        return [pltpu.make_async_copy(gs[a].at[my_slot], outs[a].at[my_slot], local_sems.at[a]) for a in range(n)]

    def start():
        for cp in mine() + copies():
            cp.start()

    def finish():
        for cp in arrivals():
            cp.wait_recv()
        for cp in copies():
            cp.wait_send()
        for cp in mine():
            cp.wait()

    return start, finish


def _rs_sibling(gs, name):
    n = len(gs)

    def body(*refs):
        g_refs, o_refs = refs[:n], refs[n:2 * n]
        send_sems, recv_sems = refs[2 * n:]
        x, y, c = _me()
        sib = (x, y, 1 - c)
        cps = [pltpu.make_async_remote_copy(src_ref=g_refs[a].at[2 * q + (1 - c)], dst_ref=o_refs[a].at[q],
                                            send_sem=send_sems.at[4 * a + q], recv_sem=recv_sems.at[4 * a + q],
                                            device_id=sib, device_id_type=MESH) for a in range(n) for q in range(4)]
        for cp in cps:
            cp.start()
        for cp in cps:
            cp.wait()

    return _pcall(body, name=name, in_specs=[HBM_SPEC] * n, out_specs=[HBM_SPEC] * n,
                  out_shape=[jax.ShapeDtypeStruct((4,) + g.shape[1:], g.dtype) for g in gs],
                  scratch_shapes=[pltpu.SemaphoreType.DMA((4 * n,)), pltpu.SemaphoreType.DMA((4 * n,))])(*gs)


def _rs_chips(ps, name):
    n = len(ps)

    def body(*refs):
        p_refs, o_refs = refs[:n], refs[n:2 * n]
        send_sems, recv_sems = refs[2 * n:]
        x, y, c = _me()
        chips = [(1 - x, y), (x, 1 - y), (1 - x, 1 - y)]
        cps = [pltpu.make_async_remote_copy(src_ref=p_refs[a].at[2 * chip[0] + chip[1]], dst_ref=o_refs[a].at[j],
                                            send_sem=send_sems.at[3 * a + j], recv_sem=recv_sems.at[3 * a + j],
                                            device_id=(*chip, c), device_id_type=MESH)
               for j, chip in enumerate(chips) for a in range(n)]
        for cp in cps:
            cp.start()
        for cp in cps:
            cp.wait()

    return _pcall(body, name=name, in_specs=[HBM_SPEC] * n, out_specs=[HBM_SPEC] * n,
                  out_shape=[jax.ShapeDtypeStruct((3,) + p.shape[1:], p.dtype) for p in ps],
                  scratch_shapes=[pltpu.SemaphoreType.DMA((3 * n,)), pltpu.SemaphoreType.DMA((3 * n,))])(*ps)


def _reduce_scatter(gs, name):
    x, y, c = _me()
    from_sib = _rs_sibling(gs, name + "_sib")
    pairs = []
    for a, (g, fs) in enumerate(zip(gs, from_sib)):
        own = g.reshape((4, 2) + g.shape[1:])
        pairs.append(_add_pairs(jnp.where(c == 0, own[:, 0], own[:, 1]), fs, f"{name}_pair{a}"))
    from_chips = _rs_chips(pairs, name + "_chips")
    return [_sum_slots(fc, f"{name}_sum{a}", extra=lax.dynamic_index_in_dim(p, 2 * x + y, axis=0, keepdims=False))
            for a, (p, fc) in enumerate(zip(pairs, from_chips))]


BIG = ("even_w_in", "even_w_out", "odd_w_in", "odd_w_out", "ffn_w_up", "ffn_w_down", "ple_w_proj", "ple_w_gate")
SMALL_SHARDED = ("even_conv_w", "odd_conv_w", "odd_conv_b", "odd_norm_g", "ffn_conv_w")
REPLICATED = ("even_b_f", "odd_dt_bias", "odd_a_log", "odd_d_skip", "ln_mix_g", "ln_mix_b", "ffn_conv_b",
              "ln_ffn_g", "ln_ffn_b", "ple_b_gate")
WEIGHTS = ("even_w_in", "even_b_f", "even_conv_w", "even_w_out", "odd_w_in", "odd_conv_w", "odd_conv_b", "odd_dt_bias",
           "odd_a_log", "odd_d_skip", "odd_norm_g", "odd_w_out", "ln_mix_g", "ln_mix_b", "ffn_w_up", "ffn_conv_w",
           "ffn_conv_b", "ffn_w_down", "ln_ffn_g", "ln_ffn_b", "ple_w_proj", "ple_w_gate", "ple_b_gate")


def _full_shapes():
    d = _dims()
    return {
        "even_w_in": ((1, D_MODEL, d["even_in"]), 2), "even_b_f": ((1, FOX_HEADS), None),
        "even_conv_w": ((1, CONV_WIDTH, CONV_DIM), 2), "even_w_out": ((1, d["even_mix"], D_MODEL), 1),
        "odd_w_in": ((1, D_MODEL, d["odd_in"]), 2), "odd_conv_w": ((1, SSM_CONV_WIDTH, d["conv_ch"]), 2),
        "odd_conv_b": ((1, d["conv_ch"]), 1), "odd_dt_bias": ((1, d["ssm_heads"]), None),
        "odd_a_log": ((1, d["ssm_heads"]), None), "odd_d_skip": ((1, d["ssm_heads"]), None),
        "odd_norm_g": ((1, d["ssm_inner"]), 1), "odd_w_out": ((1, d["ssm_inner"], D_MODEL), 1),
        "ln_mix_g": ((DEPTH, D_MODEL), None), "ln_mix_b": ((DEPTH, D_MODEL), None),
        "ffn_w_up": ((DEPTH, D_MODEL, 2 * D_FF), 2), "ffn_conv_w": ((DEPTH, FFN_CONV_WIDTH, 2 * D_FF), 2),
        "ffn_conv_b": ((DEPTH, 2 * D_FF), None), "ffn_w_down": ((DEPTH, D_FF, D_MODEL), 1),
        "ln_ffn_g": ((DEPTH, D_MODEL), None), "ln_ffn_b": ((DEPTH, D_MODEL), None),
        "ple_w_proj": ((DEPTH, PLE_DIM, D_MODEL), 2), "ple_w_gate": ((DEPTH, D_MODEL, D_MODEL), 1),
        "ple_b_gate": ((DEPTH, D_MODEL), None),
    }


def _shard_shape(name):
    shape, ax = _full_shapes()[name]
    if ax is None:
        return shape
    return tuple(s // N_DEV if i == ax else s for i, s in enumerate(shape))


def _as2d(a, lead=0):
    return a.reshape(a.shape[:lead] + (-1, a.shape[-1]))


def _part_rows(shape):
    n = int(np.prod(shape))
    return -(-(-(-n // PACK_W)) // SUBLANES) * SUBLANES


def _pack_small(parts):
    out = []
    for p in parts:
        n, rows = int(np.prod(p.shape)), _part_rows(p.shape)
        out.append(jnp.pad(p.reshape(-1).astype(F32), (0, rows * PACK_W - n)).reshape(rows, PACK_W))
    return jnp.concatenate(out, axis=0)


def _unpack_small(pack, shapes):
    lead = pack.shape[:-2]
    out, off = [], 0
    for s in shapes:
        n, rows = int(np.prod(s)), _part_rows(s)
        part = pack[..., off:off + rows, :].reshape(lead + (-1,))[..., :n]
        out.append(part.reshape(lead + tuple(s)))
        off += rows
    return out


def _assemble(gathered, name):
    shape, ax = _full_shapes()[name]
    return jnp.moveaxis(gathered, 0, ax).reshape(shape)


def _split_dest(full, name):
    shape, ax = _full_shapes()[name]
    sh = shape[:ax] + (N_DEV, shape[ax] // N_DEV) + shape[ax + 1:]
    return jnp.moveaxis(full.reshape(sh), ax, 0)


def _interleave_cols(w, parts, tc):
    C = w.shape[-1] // parts
    sh = w.shape[:-1]
    return w.reshape(sh + (parts, C // tc, tc)).swapaxes(-3, -2).reshape(sh + (parts * C,))


def _deinterleave_cols(w, parts, tc):
    C = w.shape[-1] // parts
    sh = w.shape[:-1]
    return w.reshape(sh + (C // tc, parts, tc)).swapaxes(-3, -2).reshape(sh + (parts * C,))


def _pad_cols(a, to):
    return jnp.pad(a, ((0, 0), (0, to - a.shape[1])))


def _tail_fwd(i, h_in, mix, p_i, W, sp):
    r1, h1, h1b = _ln_fwd(h_in, mix, sp["ln_mix_g"][i], sp["ln_mix_b"][i], f"ln_mix_fwd{i}")
    U = _mm(h1b, W["ffn_up"][i], "nn", F32, f"ffn_up{i}")
    S = _ffn_act_fwd(U, sp["ffn_conv_w"][i], sp["ffn_conv_b"][i], f"ffn_act_fwd{i}")
    ffn = _mm(S, W["ffn_down"][i], "nn", F32, f"ffn_down{i}")
    r2, h2, h2b = _ln_fwd(h1, ffn, sp["ln_ffn_g"][i], sp["ln_ffn_b"][i], f"ln_ffn_fwd{i}")
    G = _mm(h2b, W["ple_gate"][i], "nn", F32, f"ple_gate{i}")
    E = _mm(p_i, W["ple_proj"][i], "nn", F32, f"ple_proj{i}")
    h3, h3b = _ple_fwd(h2, G, sp["ple_b_gate"][i], E, f"ple_fwd{i}")
    return h3, h3b, dict(r1=r1, h1b=h1b, U=U, S=S, r2=r2, h2b=h2b, G=G, E=E, p=p_i)


def _tail_bwd(i, dh3, sv, W, sp, grads):
    alpha = _alpha()
    dE, dGp, dbg = _ple_bwd(dh3, sv["G"], sp["ple_b_gate"][i], sv["E"], f"ple_bwd{i}")
    grads["ple_b_gate"][i] = dbg.sum(0)
    grads["ple_w_proj"][i] = _mm(sv["p"], dE, "tn", F32, f"d_ple_proj{i}")
    grads["ple_w_gate"][i] = _mm(sv["h2b"], dGp, "tn", F32, f"d_ple_gate{i}")
    dh2 = _mm(dGp, W["ple_gate"][i], "nt", F32, f"dx_ple_gate{i}", add=dh3)
    dr2, dr2b, dg, db = _ln_bwd(sv["r2"], dh2, sp["ln_ffn_g"][i], f"ln_ffn_bwd{i}")
    grads["ln_ffn_g"][i], grads["ln_ffn_b"][i] = dg.sum(0), db.sum(0)
    grads["ffn_w_down"][i] = _mm(sv["S"], dr2b, "tn", F32, f"d_ffn_down{i}")
    dS = _mm(dr2b, W["ffn_down"][i], "nt", BF16, f"dx_ffn_down{i}")
    dUg, dUv, dwg, dwv, dbg, dbv = _ffn_act_bwd(sv["U"], dS, sp["ffn_conv_w"][i], sp["ffn_conv_b"][i], f"ffn_act_bwd{i}")
    K = FFN_CONV_WIDTH
    grads["ffn_conv_w"][i] = jnp.concatenate([dwg.reshape(K, SUBLANES, -1).sum(1), dwv.reshape(K, SUBLANES, -1).sum(1)], axis=1)
    grads["ffn_conv_b"][i] = jnp.concatenate([dbg.sum(0), dbv.sum(0)])
    grads["ffn_w_up"][i] = jnp.concatenate([_mm(sv["h1b"], dUg, "tn", F32, f"d_ffn_up_g{i}"),
                                            _mm(sv["h1b"], dUv, "tn", F32, f"d_ffn_up_v{i}")], axis=1)
    dh1 = _mm(dUg, W["ffn_up"][i], "nt", F32, f"dx_ffn_up_g{i}", add=dr2, add_scale=alpha)
    dh1 = _mm(dUv, W["ffn_up"][i], "nt", F32, f"dx_ffn_up_v{i}", add=dh1, b_k_start=D_FF)
    dr1, dr1b, dg, db = _ln_bwd(sv["r1"], dh1, sp["ln_mix_g"][i], f"ln_mix_bwd{i}")
    grads["ln_mix_g"][i], grads["ln_mix_b"][i] = dg.sum(0), db.sum(0)
    return dr1, dr1b


def _even_fwd(h, W, sp, full, gather):
    L = h.shape[0]
    H, Dh = FOX_HEADS, FOX_HEAD_DIM
    Ac = _mm(h, W["even_in_conv"], "nn", F32, "even_in_conv")
    qkv = _mm(h, W["even_in_qkv"], "nn", BF16, "even_in_qkv")
    Af = _mm(h, W["even_in_f"], "nn", F32, "even_in_f")
    y_a = _sconv_fwd(Ac, sp["even_conv_w_il"], "sconv_fwd")
    Fc = _fox_gate_fwd(Af, sp["even_b_f_pad"], "fox_gate_fwd")
    Fh = Fc[:, :H].T
    Fq, Fk = Fh.reshape(H, L, 1), Fh.reshape(H, 1, L)
    o, lse, *gathered = _attn_fwd(qkv, Fq, Fk, "attn_fwd", gather)
    for (n, i), g in zip(_late_units(), gathered):
        full[n][i] = _assemble_unit(g, n)
    _prepare_late(W, full)
    Y = jnp.concatenate([y_a, o], axis=1)
    mix = _mm(Y, W["even_out"], "nn", F32, "even_out")
    return mix, dict(h=h, Ac=Ac, Af=Af, qkv=qkv, Fq=Fq, Fk=Fk, lse=lse, Y=Y)


def _even_bwd(dmix, dres, sv, W, sp, grads, exchange):
    H, Dh = FOX_HEADS, FOX_HEAD_DIM
    C = CONV_DIM
    L = dmix.shape[0]
    grads["even_w_out"][0] = _mm(sv["Y"], dmix, "tn", F32, "d_even_out")
    exchange = [_split_unit(grads["even_w_out"][0], "even_w_out").astype(BF16)] + list(exchange)
    dY = _mm(dmix, W["even_out"], "nt", F32, "dx_even_out")
    dya = dY[:, :C]
    do = dY[:, C:].astype(BF16)
    dc, dcw = _sconv_bwd_dc(sv["Ac"], dya, sp["even_conv_w_il"], "sconv_bwd_dc")
    grads["even_conv_w"][0] = dcw.reshape(CONV_WIDTH, SUBLANES, -1).sum(1)
    dAc = _sconv_bwd_da(sv["Ac"], dya, dc, sp["even_conv_w_il"], "sconv_bwd_da")
    dq, dk, dv, dFk, *arrived = _attn_bwd(sv["qkv"], sv["Fq"], sv["Fk"], sv["lse"], do, "attn_bwd", exchange)
    dqkv = jnp.concatenate([dq, dk.astype(BF16), dv.astype(BF16)], axis=1)
    dF = _pad_cols(dFk.reshape(H, L).T, LANES)
    dAf, dbf = _fox_gate_bwd(sv["Af"], sp["even_b_f_pad"], dF, "fox_gate_bwd")
    grads["even_b_f"][0] = dbf.sum(0)[:H]
    h = sv["h"]
    gc = _deinterleave_cols(_mm(h, dAc, "tn", F32, "d_even_in_conv"), 3, LANES)
    gq = _mm(h, dqkv, "tn", F32, "d_even_in_qkv")
    gf = _mm(h, dAf, "tn", F32, "d_even_in_f")[:, :H]
    grads["even_w_in"][0] = jnp.concatenate([gc, gq, gf], axis=1)
    dh = _mm(dAc, W["even_in_conv"], "nt", F32, "dx_even_in_conv", add=dres, add_scale=_alpha())
    dh = _mm(dqkv, W["even_in_qkv"], "nt", F32, "dx_even_in_qkv", add=dh)
    dh = _mm(dAf, W["even_in_f"], "nt", F32, "dx_even_in_f", add=dh)
    return dh, arrived


def _group_layouts(v, G):
    R = v.shape[0] // G
    return v.reshape(G, 1, R), v.reshape(G, R, 1)


def _odd_fwd(h, W, sp):
    d = _dims()
    L = h.shape[0]
    Hs, G, N, P = d["ssm_heads"], SSM_GROUPS, SSM_STATE, SSM_HEAD_DIM
    R = Hs // G
    inner = d["ssm_inner"]
    z = _mm(h, W["odd_in_z"], "nn", F32, "odd_in_z")
    xr = _mm(h, W["odd_in_x"], "nn", F32, "odd_in_x")
    dtp = _mm(h, W["odd_in_dt"], "nn", F32, "odd_in_dt")
    act = _mconv_fwd(xr, sp["odd_conv_w"], sp["odd_conv_b"], "mconv_fwd")
    dtg = dtp[:, :Hs].reshape(L, G, R)
    dtc, dtr = dtg.transpose(1, 0, 2), dtg.transpose(1, 2, 0)
    dsk_x = jnp.repeat(sp["odd_d_skip"], P).reshape(G, 1, R * P)
    ssd_in = (act, dtc, dtr) + _group_layouts(sp["odd_dt_bias"], G) + _group_layouts(sp["odd_a_log"], G) + (dsk_x,)
    y, hprev = _ssd_fwd(*ssd_in, "ssd_fwd")
    u = _gnorm_fwd(y, z, sp["odd_norm_g"], "gnorm_fwd")
    mix = _mm(u, W["odd_out"], "nn", F32, "odd_out")
    return mix, dict(h=h, z=z, xr=xr, ssd_in=ssd_in, hprev=hprev, y=y, u=u)


def _odd_bwd(dmix, dres, sv, W, sp, grads):
    d = _dims()
    L = dmix.shape[0]
    Hs, G, N, P = d["ssm_heads"], SSM_GROUPS, SSM_STATE, SSM_HEAD_DIM
    grads["odd_w_out"][0] = _mm(sv["u"], dmix, "tn", F32, "d_odd_out")
    du = _mm(dmix, W["odd_out"], "nt", F32, "dx_odd_out")
    dy, dz, dg = _gnorm_bwd(sv["y"], sv["z"], sp["odd_norm_g"], du, "gnorm_bwd")
    grads["odd_norm_g"][0] = dg.sum(0)
    dxs, dB, dC, ddt, gbias, galog, gdsk = _ssd_bwd(*sv["ssd_in"], sv["hprev"], dy, "ssd_bwd")
    grads["odd_dt_bias"][0] = gbias.reshape(Hs)
    grads["odd_a_log"][0] = galog.reshape(Hs)
    grads["odd_d_skip"][0] = gdsk.reshape(Hs)
    dact = jnp.concatenate([dxs, dB, dC], axis=1)
    dxr, dcw, dcb = _mconv_bwd(sv["xr"], dact, sp["odd_conv_w"], sp["odd_conv_b"], "mconv_bwd")
    grads["odd_conv_w"][0] = dcw.reshape(SSM_CONV_WIDTH, SUBLANES, -1).sum(1)
    grads["odd_conv_b"][0] = dcb.sum(0)
    ddtp = _pad_cols(ddt.transpose(1, 0, 2).reshape(L, Hs), W["odd_in_dt"].shape[1])
    h = sv["h"]
    gz = _mm(h, dz, "tn", F32, "d_odd_in_z")
    gx = _mm(h, dxr, "tn", F32, "d_odd_in_x")
    gdt = _mm(h, ddtp, "tn", F32, "d_odd_in_dt")[:, :Hs]
    grads["odd_w_in"][0] = jnp.concatenate([gz, gx, gdt], axis=1)
    dh = _mm(dz, W["odd_in_z"], "nt", F32, "dx_odd_in_z", add=dres, add_scale=_alpha())
    dh = _mm(dxr, W["odd_in_x"], "nt", F32, "dx_odd_in_x", add=dh)
    dh = _mm(ddtp, W["odd_in_dt"], "nt", F32, "dx_odd_in_dt", add=dh)
    return dh


FIRST_UNIT = ("even_w_in", 0)


def _late_units():
    return [(n, i) for n in BIG for i in range(_full_shapes()[n][0][0]) if (n, i) != FIRST_UNIT]


def _assemble_unit(gathered, name):
    shape, ax = _full_shapes()[name]
    g = gathered.reshape((N_DEV,) + _shard_shape(name)[1:])
    return jnp.moveaxis(g, 0, ax - 1).reshape(shape[1:])


def _split_unit(full_layer, name):
    shape, ax = _full_shapes()[name]
    sh = shape[1:ax] + (N_DEV, shape[ax] // N_DEV) + shape[ax + 1:]
    return _as2d(jnp.moveaxis(full_layer.reshape(sh), ax - 1, 0), 1)


def _prepare_first(W, full):
    C, fd = CONV_DIM, _dims()["fox_dim"]
    ew = full["even_w_in"][0]
    W["even_in_conv"] = _interleave_cols(ew[:, :3 * C], 3, LANES)
    W["even_in_qkv"] = ew[:, 3 * C:3 * C + 3 * fd]
    W["even_in_f"] = _pad_cols(ew[:, 3 * C + 3 * fd:], LANES)


def _prepare_late(W, full):
    d = _dims()
    W["even_out"] = full["even_w_out"][0]
    ow = full["odd_w_in"][0]
    inner, cch, Hs = d["ssm_inner"], d["conv_ch"], d["ssm_heads"]
    W["odd_in_z"] = ow[:, :inner]
    W["odd_in_x"] = ow[:, inner:inner + cch]
    W["odd_in_dt"] = _pad_cols(ow[:, inner + cch:], -(-Hs // LANES) * LANES)
    W["odd_out"] = full["odd_w_out"][0]
    for key, name in (("ffn_up", "ffn_w_up"), ("ffn_down", "ffn_w_down"), ("ple_proj", "ple_w_proj"), ("ple_gate", "ple_w_gate")):
        W[key] = list(full[name])


def _prepare_small(full):
    sp = {}
    sp["even_conv_w_il"] = full["even_conv_w"][0]
    sp["even_b_f_pad"] = _pad_cols(full["even_b_f"], LANES)
    sp["odd_conv_w"] = full["odd_conv_w"][0]
    sp["odd_conv_b"] = full["odd_conv_b"][0]
    sp["odd_norm_g"] = full["odd_norm_g"][0]
    for n in ("odd_dt_bias", "odd_a_log", "odd_d_skip"):
        sp[n] = full[n][0]
    for n in ("ln_mix_g", "ln_mix_b", "ln_ffn_g", "ln_ffn_b", "ple_b_gate"):
        sp[n] = full[n]
    sp["ffn_conv_w"] = [full["ffn_conv_w"][i] for i in range(DEPTH)]
    sp["ffn_conv_b"] = [full["ffn_conv_b"][i] for i in range(DEPTH)]
    return sp


def _local_step(x, p, target, full, late_shards):
    sp = _prepare_small(full)
    W = {}
    _prepare_first(W, full)
    grads = {n: [None] * _full_shapes()[n][0][0] for n in WEIGHTS}
    pb = p.astype(BF16)
    mix0, sv_e = _even_fwd(x.astype(BF16), W, sp, full, late_shards)
    h3_0, h3_0b, sv_t0 = _tail_fwd(0, x, mix0, pb[0], W, sp)
    mix1, sv_o = _odd_fwd(h3_0b, W, sp)
    h3_1, _, sv_t1 = _tail_fwd(1, h3_0, mix1, pb[1], W, sp)
    dh, sq = _loss_head(h3_1, target, "loss_head")
    dr1, dr1b = _tail_bwd(1, dh, sv_t1, W, sp, grads)
    dh = _odd_bwd(dr1b, dr1, sv_o, W, sp, grads)
    dr1, dr1b = _tail_bwd(0, dh, sv_t0, W, sp, grads)
    outgoing = [_split_unit(grads[n][i], n).astype(BF16) for n, i in _late_units()[1:]]
    dx, incoming = _even_bwd(dr1b, dr1, sv_e, W, sp, grads, outgoing)
    reduced = {u: _sum_slots(r, f"sum_grads_{u[0]}{u[1]}") for u, r in zip(_late_units(), incoming)}
    grads = {n: v if n in BIG else jnp.stack(v) for n, v in grads.items()}
    return jnp.sum(sq), dx, grads, reduced


def kernel(x, p, even_w_in, even_b_f, even_conv_w, even_w_out, odd_w_in, odd_conv_w, odd_conv_b, odd_dt_bias, odd_a_log, odd_d_skip, odd_norm_g, odd_w_out, ln_mix_g, ln_mix_b, ffn_w_up, ffn_conv_w, ffn_conv_b, ffn_w_down, ln_ffn_g, ln_ffn_b, ple_w_proj, ple_w_gate, ple_b_gate, loss_target, m_even_w_in, m_even_b_f, m_even_conv_w, m_even_w_out, m_odd_w_in, m_odd_conv_w, m_odd_conv_b, m_odd_dt_bias, m_odd_a_log, m_odd_d_skip, m_odd_norm_g, m_odd_w_out, m_ln_mix_g, m_ln_mix_b, m_ffn_w_up, m_ffn_conv_w, m_ffn_conv_b, m_ffn_w_down, m_ln_ffn_g, m_ln_ffn_b, m_ple_w_proj, m_ple_w_gate, m_ple_b_gate, v_even_w_in, v_even_b_f, v_even_conv_w, v_even_w_out, v_odd_w_in, v_odd_conv_w, v_odd_conv_b, v_odd_dt_bias, v_odd_a_log, v_odd_d_skip, v_odd_norm_g, v_odd_w_out, v_ln_mix_g, v_ln_mix_b, v_ffn_w_up, v_ffn_conv_w, v_ffn_conv_b, v_ffn_w_down, v_ln_ffn_g, v_ln_ffn_b, v_ple_w_proj, v_ple_w_gate, v_ple_b_gate):
    args = locals()
    w = {n: args[n] for n in WEIGHTS}
    m = {n: args["m_" + n] for n in WEIGHTS}
    v = {n: args["v_" + n] for n in WEIGHTS}
    me = 4 * lax.axis_index("x") + 2 * lax.axis_index("y") + lax.axis_index("c")

    def shard(unit):
        return _as2d(w[unit[0]][unit[1]]).astype(BF16)

    first, small = _all_gather([shard(FIRST_UNIT), _pack_small([w[n] for n in SMALL_SHARDED])], "ag_weights")
    full = dict(w)
    for n in BIG:
        full[n] = [None] * _full_shapes()[n][0][0]
    full[FIRST_UNIT[0]][FIRST_UNIT[1]] = _assemble_unit(first, FIRST_UNIT[0])
    for n, g in zip(SMALL_SHARDED, _unpack_small(small, [_shard_shape(n) for n in SMALL_SHARDED])):
        full[n] = _assemble(g, n)

    sq, dx, grads, reduced = _local_step(x[0], p[:, 0], loss_target[0], full, [shard(u) for u in _late_units()])
    loss = lax.psum(0.5 * sq / D_MODEL, ("x", "y", "c"))

    n0, i0 = FIRST_UNIT
    (reduced[FIRST_UNIT],) = _reduce_scatter([_split_unit(grads[n0][i0], n0).astype(BF16)], "rs_grads")
    g_final = {n: jnp.stack([reduced[(n, i)] for i in range(_full_shapes()[n][0][0])]).reshape(_shard_shape(n)) for n in BIG}
    small_names = SMALL_SHARDED + REPLICATED
    (small_all,) = _all_gather([_pack_small([grads[n] for n in small_names])], "ag_small_grads")
    small_sum = _sum_slots(small_all, "sum_small_grads")
    for n, g in zip(small_names, _unpack_small(small_sum, [_full_shapes()[n][0] for n in small_names])):
        g_final[n] = lax.dynamic_index_in_dim(_split_dest(g, n), me, axis=0, keepdims=False) if n in SMALL_SHARDED else g

    out = {}
    for n in BIG:
        res = _adamw(*[_as2d(t[n]) for t in (w, g_final, m, v)], "adamw_" + n)
        out[n] = [r.reshape(_shard_shape(n)) for r in res]
    shapes = [_shard_shape(n) for n in small_names]
    res = _adamw(*[_pack_small([t[n] for n in small_names]) for t in (w, g_final, m, v)], "adamw_small")
    for n, d_, m_, v_ in zip(small_names, *[_unpack_small(r, shapes) for r in res]):
        out[n] = [d_, m_, v_]
    return (loss, dx[None], *[g_final[n] for n in WEIGHTS], *[out[n][0] for n in WEIGHTS],
            *[out[n][1] for n in WEIGHTS], *[out[n][2] for n in WEIGHTS])
```

```python
import jax
import jax.numpy as jnp
import numpy as np
from jax import lax
from jax.experimental import pallas as pl
from jax.experimental.pallas import tpu as pltpu

D_MODEL = 1024
SEQ = 8192
DEPTH = 2
CONV_DIM = 512
CONV_WIDTH = 3
FOX_HEADS = 8
FOX_HEAD_DIM = 64
SSM_HEAD_DIM = 64
SSM_GROUPS = 4
SSM_STATE = 128
SSM_CONV_WIDTH = 4
SSM_CHUNK = 128
D_FF = 2816
FFN_CONV_WIDTH = 3
PLE_DIM = 256
LN_EPS = 1e-5
RMS_EPS = 1e-5
ADAM_LR = 0.001
ADAM_B1 = 0.9
ADAM_B2 = 0.999
ADAM_EPS = 1e-08
ADAM_WD = 0.01
ADAM_STEP = 10
N_DEV = 8

F32 = jnp.float32
BF16 = jnp.bfloat16
NEG = -1e30
LANES = 128
SUBLANES = 8
PACK_W = 1024
VMEM_LIMIT = 48 * 1024 * 1024
ATTN_BWD_VMEM_LIMIT = 56 * 1024 * 1024


def _dims():
    fox_dim = FOX_HEADS * FOX_HEAD_DIM
    ssm_inner = 2 * D_MODEL
    ssm_heads = ssm_inner // SSM_HEAD_DIM
    conv_ch = ssm_inner + 2 * SSM_GROUPS * SSM_STATE
    return dict(fox_dim=fox_dim, even_in=3 * CONV_DIM + 3 * fox_dim + FOX_HEADS, even_mix=CONV_DIM + fox_dim,
                ssm_inner=ssm_inner, ssm_heads=ssm_heads, conv_ch=conv_ch, odd_in=ssm_inner + conv_ch + ssm_heads)


def _alpha():
    return (2.0 * DEPTH) ** 0.25


def _pick(dim, prefs):
    for p in prefs:
        if dim % p == 0:
            return p
    return dim


def _pcall(body, **kw):
    return pl.pallas_call(body, **kw)


def _cparams(sem=None, **kw):
    if sem is not None:
        kw["dimension_semantics"] = sem
    return pltpu.CompilerParams(vmem_limit_bytes=VMEM_LIMIT, **kw)


def _sigmoid(x):
    return 1.0 / (1.0 + jnp.exp(-x))


def _softplus(x):
    return jnp.maximum(x, 0.0) + jnp.log(1.0 + jnp.exp(-jnp.abs(x)))


def _sum8(x):
    n, c = x.shape
    return x.reshape(n // SUBLANES, SUBLANES, c).sum(axis=0)


def _dot(a, b, dims):
    return lax.dot_general(a, b, (dims, ((), ())), preferred_element_type=F32)


NN = ((1,), (0,))
NT = ((1,), (1,))
TN = ((0,), (0,))


def _split3(x):
    hi = x.astype(BF16)
    r1 = x - hi.astype(F32)
    mid = r1.astype(BF16)
    lo = (r1 - mid.astype(F32)).astype(BF16)
    return hi, mid, lo


def _tri_mm(tri_bf16, x, tri_first=True):
    if tri_first:
        return sum(_dot(tri_bf16, part, NN) for part in _split3(x))
    return sum(_dot(part, tri_bf16, NN) for part in _split3(x))


def _tri(n, upper=False):
    r = lax.broadcasted_iota(jnp.int32, (n, n), 0)
    c = lax.broadcasted_iota(jnp.int32, (n, n), 1)
    return jnp.where((r <= c) if upper else (r >= c), 1.0, 0.0).astype(BF16)


def _shift_down(cur, prev8, k):
    if k == 0:
        return cur
    ext = jnp.concatenate([prev8, cur], axis=0)
    return pltpu.roll(ext, k, axis=0)[SUBLANES:]


def _shift_up(cur, next8, k):
    if k == 0:
        return cur
    n = cur.shape[0]
    ext = jnp.concatenate([cur, next8], axis=0)
    return pltpu.roll(ext, n + SUBLANES - k, axis=0)[:n]


def _mm(a, b, mode, out_dtype, name, add=None, add_scale=1.0, b_k_start=0):
    if mode == "nn":
        (M, K), (K2, N) = a.shape, b.shape
    elif mode == "nt":
        (M, K), N, K2 = a.shape, b.shape[0], a.shape[1]
        assert b.shape[1] >= b_k_start + K, (a.shape, b.shape, b_k_start)
    else:
        (K, M), (K2, N) = a.shape, b.shape
    assert K == K2, (a.shape, b.shape, mode)
    tm = _pick(M, (1024, 1408, 512, 256, 128))
    tn = _pick(N, (1408, 1024, 768, 512, 384, 256, 128))
    tk = K if K <= 2048 and b_k_start % K == 0 else _pick(K, (1408, 1024, 768, 512, 256, 128))
    assert b_k_start % tk == 0
    k0 = b_k_start // tk
    nk = K // tk
    dims = {"nn": NN, "nt": NT, "tn": TN}[mode]
    a_spec = pl.BlockSpec((tk, tm), lambda i, j, k: (k, i)) if mode == "tn" else pl.BlockSpec((tm, tk), lambda i, j, k: (i, k))
    b_spec = pl.BlockSpec((tn, tk), lambda i, j, k: (j, k + k0)) if mode == "nt" else pl.BlockSpec((tk, tn), lambda i, j, k: (k, j))
    o_spec = pl.BlockSpec((tm, tn), lambda i, j, k: (i, j))
    has_add = add is not None

    def body(*refs):
        a_ref, b_ref = refs[:2]
        add_ref = refs[2] if has_add else None
        o_ref = refs[2 + has_add]
        prod = _dot(a_ref[...].astype(BF16), b_ref[...].astype(BF16), dims)
        if nk == 1:
            if has_add:
                prod = prod + add_scale * add_ref[...].astype(F32)
            o_ref[...] = prod.astype(out_dtype)
            return
        acc = refs[3 + has_add]
        k = pl.program_id(2)

        @pl.when(k == 0)
        def _():
            if has_add:
                acc[...] = prod + add_scale * add_ref[...].astype(F32)
            else:
                acc[...] = prod

        @pl.when(k > 0)
        def _():
            acc[...] += prod

        @pl.when(k == nk - 1)
        def _():
            o_ref[...] = acc[...].astype(out_dtype)

    ins = [a, b] + ([add] if has_add else [])
    specs = [a_spec, b_spec] + ([o_spec] if has_add else [])
    return _pcall(body, name=name, grid=(M // tm, N // tn, nk), in_specs=specs, out_specs=o_spec,
                  out_shape=jax.ShapeDtypeStruct((M, N), out_dtype),
                  scratch_shapes=[pltpu.VMEM((tm, tn), F32)] if nk > 1 else [],
                  compiler_params=_cparams(("parallel", "parallel", "arbitrary")))(*ins)


def _row_tile(L, wide=False):
    return _pick(L, (256, 128)) if wide else _pick(L, (512, 256, 128))


def _conv_row_tile(L, backward):
    return _pick(L, (2048, 1024, 512, 256, 128)) if backward else _pick(L, (1024, 512, 256, 128))


def _ln_fwd(h, mix, g, b, name):
    L, D = h.shape
    tl = _row_tile(L)
    alpha = _alpha()

    def body(h_ref, m_ref, g_ref, b_ref, r_ref, y_ref, yb_ref):
        r = alpha * h_ref[...] + m_ref[...]
        mu = jnp.mean(r, axis=-1, keepdims=True)
        xc = r - mu
        var = jnp.mean(xc * xc, axis=-1, keepdims=True)
        r_ref[...] = r
        y = xc * lax.rsqrt(var + LN_EPS) * g_ref[...] + b_ref[...]
        y_ref[...] = y
        yb_ref[...] = y.astype(BF16)

    row = pl.BlockSpec((tl, D), lambda i: (i, 0))
    vec = pl.BlockSpec((1, D), lambda i: (0, 0))
    return _pcall(body, name=name, grid=(L // tl,), in_specs=[row, row, vec, vec], out_specs=[row, row, row],
                  out_shape=[jax.ShapeDtypeStruct((L, D), F32)] * 2 + [jax.ShapeDtypeStruct((L, D), BF16)],
                  compiler_params=_cparams(("parallel",)))(h, mix, g.reshape(1, D), b.reshape(1, D))


def _ln_bwd(r, dy, g, name):
    L, D = r.shape
    tl = _row_tile(L)

    def body(r_ref, dy_ref, g_ref, dr_ref, drb_ref, dg_ref, db_ref):
        i = pl.program_id(0)
        r_ = r_ref[...]
        dy_ = dy_ref[...]
        mu = jnp.mean(r_, axis=-1, keepdims=True)
        xc = r_ - mu
        rstd = lax.rsqrt(jnp.mean(xc * xc, axis=-1, keepdims=True) + LN_EPS)
        xhat = xc * rstd
        dxh = dy_ * g_ref[...]
        dr = rstd * (dxh - jnp.mean(dxh, axis=-1, keepdims=True) - xhat * jnp.mean(dxh * xhat, axis=-1, keepdims=True))
        dr_ref[...] = dr
        drb_ref[...] = dr.astype(BF16)

        @pl.when(i == 0)
        def _():
            dg_ref[...] = jnp.zeros_like(dg_ref)
            db_ref[...] = jnp.zeros_like(db_ref)

        dg_ref[...] += _sum8(dy_ * xhat)
        db_ref[...] += _sum8(dy_)

    row = pl.BlockSpec((tl, D), lambda i: (i, 0))
    vec = pl.BlockSpec((1, D), lambda i: (0, 0))
    acc = pl.BlockSpec((SUBLANES, D), lambda i: (0, 0))
    return _pcall(body, name=name, grid=(L // tl,), in_specs=[row, row, vec], out_specs=[row, row, acc, acc],
                  out_shape=[jax.ShapeDtypeStruct((L, D), F32), jax.ShapeDtypeStruct((L, D), BF16),
                             jax.ShapeDtypeStruct((SUBLANES, D), F32), jax.ShapeDtypeStruct((SUBLANES, D), F32)],
                  compiler_params=_cparams(("arbitrary",)))(r, dy, g.reshape(1, D))


def _ple_fwd(h2, G, bg, E, name):
    L, D = h2.shape
    tl = _row_tile(L)

    def body(h_ref, g_ref, b_ref, e_ref, o_ref, ob_ref):
        o = h_ref[...] + _sigmoid(g_ref[...] + b_ref[...]) * e_ref[...]
        o_ref[...] = o
        ob_ref[...] = o.astype(BF16)

    row = pl.BlockSpec((tl, D), lambda i: (i, 0))
    vec = pl.BlockSpec((1, D), lambda i: (0, 0))
    return _pcall(body, name=name, grid=(L // tl,), in_specs=[row, row, vec, row], out_specs=[row, row],
                  out_shape=[jax.ShapeDtypeStruct((L, D), F32), jax.ShapeDtypeStruct((L, D), BF16)],
                  compiler_params=_cparams(("parallel",)))(h2, G, bg.reshape(1, D), E)


def _ple_bwd(dh3, G, bg, E, name):
    L, D = dh3.shape
    tl = _row_tile(L)

    def body(d_ref, g_ref, b_ref, e_ref, de_ref, dg_ref, db_ref):
        i = pl.program_id(0)
        d = d_ref[...]
        sg = _sigmoid(g_ref[...] + b_ref[...])
        de_ref[...] = (d * sg).astype(BF16)
        dgp = d * e_ref[...] * sg * (1.0 - sg)
        dg_ref[...] = dgp.astype(BF16)

        @pl.when(i == 0)
        def _():
            db_ref[...] = jnp.zeros_like(db_ref)

        db_ref[...] += _sum8(dgp)

    row = pl.BlockSpec((tl, D), lambda i: (i, 0))
    vec = pl.BlockSpec((1, D), lambda i: (0, 0))
    acc = pl.BlockSpec((SUBLANES, D), lambda i: (0, 0))
    return _pcall(body, name=name, grid=(L // tl,), in_specs=[row, row, vec, row], out_specs=[row, row, acc],
                  out_shape=[jax.ShapeDtypeStruct((L, D), BF16), jax.ShapeDtypeStruct((L, D), BF16),
                             jax.ShapeDtypeStruct((SUBLANES, D), F32)],
                  compiler_params=_cparams(("arbitrary",)))(dh3, G, bg.reshape(1, D), E)


def _loss_head(h, target, name):
    L, D = h.shape
    tl = _row_tile(L)

    def body(h_ref, t_ref, d_ref, s_ref):
        i = pl.program_id(0)
        e = h_ref[...] - t_ref[...]
        d_ref[...] = e * (1.0 / D)

        @pl.when(i == 0)
        def _():
            s_ref[...] = jnp.zeros_like(s_ref)

        s_ref[...] += _sum8(e * e)

    row = pl.BlockSpec((tl, D), lambda i: (i, 0))
    acc = pl.BlockSpec((SUBLANES, D), lambda i: (0, 0))
    return _pcall(body, name=name, grid=(L // tl,), in_specs=[row, row], out_specs=[row, acc],
                  out_shape=[jax.ShapeDtypeStruct((L, D), F32), jax.ShapeDtypeStruct((SUBLANES, D), F32)],
                  compiler_params=_cparams(("arbitrary",)))(h, target)


def _halo_prev(tl, ncol_blocks_fn):
    return lambda j, i: (jnp.maximum(i * (tl // SUBLANES) - 1, 0), ncol_blocks_fn(j))


STRIP_ROWS = 64


def _strip(s, R):
    return pl.ds(s * R if isinstance(s, int) else pl.multiple_of(s * R, R), R)


def _conv_taps(cur, prev, w_ref, K):
    acc = w_ref[K - 1:K, :] * cur
    for k in range(K - 1):
        acc = acc + w_ref[k:k + 1, :] * _shift_down(cur, prev, K - 1 - k)
    return acc


def _ffn_act_fwd(U, w, b, name):
    L, F2 = U.shape
    F = F2 // 2
    K = w.shape[0]
    tc = _pick(F, (1408, 256, 128))
    tl = _pick(L, (512, 256, 128)) if tc > 256 else _conv_row_tile(L, False)
    nj = F // tc

    def body(ug_ref, uv_ref, ugp_ref, uvp_ref, wg_ref, wv_ref, bg_ref, bv_ref, s_ref):
        i = pl.program_id(1)
        g = _conv_taps(ug_ref[...], jnp.where(i == 0, 0.0, ugp_ref[...]), wg_ref, K) + bg_ref[...]
        v = _conv_taps(uv_ref[...], jnp.where(i == 0, 0.0, uvp_ref[...]), wv_ref, K) + bv_ref[...]
        s_ref[...] = (g * _sigmoid(g) * v).astype(BF16)

    def both(shape, index):
        return [pl.BlockSpec(shape, lambda j, i: index(j, i)), pl.BlockSpec(shape, lambda j, i: index(j + nj, i))]

    b2 = b.reshape(1, F2)
    return _pcall(body, name=name, grid=(nj, L // tl),
                  in_specs=both((tl, tc), lambda j, i: (i, j)) + both((SUBLANES, tc), _halo_prev(tl, lambda j: j))
                  + both((K, tc), lambda j, i: (0, j)) + both((1, tc), lambda j, i: (0, j)),
                  out_specs=pl.BlockSpec((tl, tc), lambda j, i: (i, j)),
                  out_shape=jax.ShapeDtypeStruct((L, F), BF16),
                  compiler_params=_cparams(("parallel", "parallel")))(U, U, U, U, w, w, b2, b2)


def _halo_next(tl, L, rows):
    return lambda j, i: (jnp.minimum((i + 1) * (tl // rows), L // rows - 1), j)


BF16_ROWS = 16


def _ffn_act_bwd(U, dS, w, b, name):
    L, F2 = U.shape
    F = F2 // 2
    K = w.shape[0]
    tc = _pick(F, (256, 128))
    tl = _conv_row_tile(L, True)
    nl = L // tl
    nj = F // tc
    R = STRIP_ROWS
    ns = tl // R

    def body(ug_ref, uv_ref, ugp_ref, uvp_ref, ugn_ref, uvn_ref, ds_ref, dsn_ref, wg_ref, wv_ref, bg_ref, bv_ref,
             dug_ref, duv_ref, dwg_ref, dwv_ref, dbg_ref, dbv_ref):
        i = pl.program_id(1)

        @pl.when(i == 0)
        def _():
            for r in (dwg_ref, dwv_ref, dbg_ref, dbv_ref):
                r[...] = jnp.zeros_like(r)

        for lanes in (slice(c, c + LANES) for c in range(0, tc, LANES)):
            wts = [[w_ref[k:k + 1, lanes] for k in range(K)] for w_ref in (wg_ref, wv_ref)]
            bias = [b_ref[:, lanes] for b_ref in (bg_ref, bv_ref)]

            def strip(s):
                return _strip(s, R)

            def at_conv_out(x, xprev8, ds):
                sh = [[_shift_down(x[h], xprev8[h], K - 1 - k) for k in range(K)] for h in range(2)]
                g, v = (sum(wts[h][k] * sh[h][k] for k in range(K)) + bias[h] for h in range(2))
                sg = _sigmoid(g)
                return (ds * v * sg * (1.0 + g * (1.0 - sg)), ds * g * sg), sh

            def emit(rows, duc, duc_next8):
                for h, du_ref in enumerate((dug_ref, duv_ref)):
                    du = sum(wts[h][k] * _shift_up(duc[h], duc_next8[h], K - 1 - k) for k in range(K))
                    du_ref[rows, lanes] = du.astype(BF16)

            def accumulate(acc, duc, sh):
                dw, db = acc
                return (tuple(tuple(dw[h][k] + _sum8(duc[h] * sh[h][k]) for k in range(K)) for h in range(2)),
                        tuple(db[h] + _sum8(duc[h]) for h in range(2)))

            def last8(x):
                return tuple(a[R - SUBLANES:] for a in x)

            zero = jnp.zeros((SUBLANES, LANES), F32)
            x0 = (ug_ref[strip(0), lanes], uv_ref[strip(0), lanes])
            prev8 = (jnp.where(i == 0, 0.0, ugp_ref[:, lanes]), jnp.where(i == 0, 0.0, uvp_ref[:, lanes]))
            duc0, sh0 = at_conv_out(x0, prev8, ds_ref[strip(0), lanes].astype(F32))
            acc0 = accumulate(((((zero,) * K,) * 2), (zero,) * 2), duc0, sh0)

            def step(s, carry):
                xprev, ducprev, acc = carry
                x = (ug_ref[strip(s), lanes], uv_ref[strip(s), lanes])
                duc, sh = at_conv_out(x, last8(xprev), ds_ref[strip(s), lanes].astype(F32))
                emit(strip(s - 1), ducprev, tuple(d[:SUBLANES] for d in duc))
                return x, duc, accumulate(acc, duc, sh)

            xl, ducl, (dw, db) = lax.fori_loop(1, ns, step, (x0, duc0, acc0))
            ducn, _ = at_conv_out((ugn_ref[:, lanes], uvn_ref[:, lanes]), last8(xl), dsn_ref[:, lanes].astype(F32)[:SUBLANES])
            emit(strip(ns - 1), ducl, tuple(jnp.where(i == nl - 1, 0.0, d) for d in ducn))
            for h, (dw_ref, db_ref) in enumerate(((dwg_ref, dbg_ref), (dwv_ref, dbv_ref))):
                db_ref[:, lanes] += db[h]
                for k in range(K):
                    dw_ref[k * SUBLANES:(k + 1) * SUBLANES, lanes] += dw[h][k]

    def both(shape, index):
        return [pl.BlockSpec(shape, lambda j, i: index(j, i)), pl.BlockSpec(shape, lambda j, i: index(j + nj, i))]

    b2 = b.reshape(1, F2)
    du_specs, du_shapes = [pl.BlockSpec((tl, tc), lambda j, i: (i, j))] * 2, [jax.ShapeDtypeStruct((L, F), BF16)] * 2
    dw_specs = [pl.BlockSpec((K * SUBLANES, tc), lambda j, i: (0, j))] * 2
    dw_shapes = [jax.ShapeDtypeStruct((K * SUBLANES, F), F32)] * 2
    db_specs, db_shapes = [pl.BlockSpec((SUBLANES, tc), lambda j, i: (0, j))] * 2, [jax.ShapeDtypeStruct((SUBLANES, F), F32)] * 2
    return _pcall(body, name=name, grid=(nj, nl),
                  in_specs=both((tl, tc), lambda j, i: (i, j)) + both((SUBLANES, tc), _halo_prev(tl, lambda j: j))
                  + both((SUBLANES, tc), _halo_next(tl, L, SUBLANES))
                  + [pl.BlockSpec((tl, tc), lambda j, i: (i, j)), pl.BlockSpec((BF16_ROWS, tc), _halo_next(tl, L, BF16_ROWS))]
                  + both((K, tc), lambda j, i: (0, j)) + both((1, tc), lambda j, i: (0, j)),
                  out_specs=du_specs + dw_specs + db_specs, out_shape=du_shapes + dw_shapes + db_shapes,
                  compiler_params=_cparams(("parallel", "arbitrary")))(U, U, U, U, U, U, dS, dS, w, w, b2, b2)


def _sconv_fwd(Ac, w, name):
    L, C3 = Ac.shape
    C = C3 // 3
    K = w.shape[0]
    tl = _conv_row_tile(L, False)
    tc = LANES

    def body(a_ref, ap_ref, w_ref, y_ref):
        i = pl.program_id(1)
        a = a_ref[...]
        ap = ap_ref[...]
        p = a[:, tc:2 * tc] * a[:, 2 * tc:]
        pp = jnp.where(i == 0, 0.0, ap[:, tc:2 * tc] * ap[:, 2 * tc:])
        y_ref[...] = (a[:, :tc] * _conv_taps(p, pp, w_ref, K)).astype(BF16)

    return _pcall(body, name=name, grid=(C // tc, L // tl),
                  in_specs=[pl.BlockSpec((tl, 3 * tc), lambda j, i: (i, j)),
                            pl.BlockSpec((SUBLANES, 3 * tc), _halo_prev(tl, lambda j: j)),
                            pl.BlockSpec((K, tc), lambda j, i: (0, j))],
                  out_specs=pl.BlockSpec((tl, tc), lambda j, i: (i, j)),
                  out_shape=jax.ShapeDtypeStruct((L, C), BF16),
                  compiler_params=_cparams(("parallel", "parallel")))(Ac, Ac, w)


def _sconv_bwd_dc(Ac, dy, w, name):
    L, C3 = Ac.shape
    C = C3 // 3
    K = w.shape[0]
    tl = _conv_row_tile(L, False)
    tc = LANES

    def body(a_ref, ap_ref, dy_ref, dc_ref, dw_ref):
        i = pl.program_id(1)
        a = a_ref[...]
        ap = ap_ref[...]
        p = a[:, tc:2 * tc] * a[:, 2 * tc:]
        pp = jnp.where(i == 0, 0.0, ap[:, tc:2 * tc] * ap[:, 2 * tc:])
        dc = dy_ref[...] * a[:, :tc]
        dc_ref[...] = dc

        @pl.when(i == 0)
        def _():
            dw_ref[...] = jnp.zeros_like(dw_ref)

        for k in range(K):
            dw_ref[k * SUBLANES:(k + 1) * SUBLANES, :] += _sum8(dc * _shift_down(p, pp, K - 1 - k))

    return _pcall(body, name=name, grid=(C // tc, L // tl),
                  in_specs=[pl.BlockSpec((tl, 3 * tc), lambda j, i: (i, j)),
                            pl.BlockSpec((SUBLANES, 3 * tc), _halo_prev(tl, lambda j: j)),
                            pl.BlockSpec((tl, tc), lambda j, i: (i, j))],
                  out_specs=[pl.BlockSpec((tl, tc), lambda j, i: (i, j)),
                             pl.BlockSpec((K * SUBLANES, tc), lambda j, i: (0, j))],
                  out_shape=[jax.ShapeDtypeStruct((L, C), F32), jax.ShapeDtypeStruct((K * SUBLANES, C), F32)],
                  compiler_params=_cparams(("parallel", "arbitrary")))(Ac, Ac, dy)


def _sconv_bwd_da(Ac, dy, dc, w, name):
    L, C3 = Ac.shape
    C = C3 // 3
    K = w.shape[0]
    tl = _conv_row_tile(L, False)
    tc = LANES
    nl = L // tl

    def body(a_ref, ap_ref, dy_ref, dc_ref, dcn_ref, w_ref, o_ref):
        i = pl.program_id(1)
        a = a_ref[...]
        ap = ap_ref[...]
        gc, h = a[:, tc:2 * tc], a[:, 2 * tc:]
        p = gc * h
        pp = jnp.where(i == 0, 0.0, ap[:, tc:2 * tc] * ap[:, 2 * tc:])
        dgb = dy_ref[...] * _conv_taps(p, pp, w_ref, K)
        cur = dc_ref[...]
        nxt = jnp.where(i == nl - 1, 0.0, dcn_ref[...])
        dp = w_ref[K - 1:K, :] * cur
        for k in range(K - 1):
            dp = dp + w_ref[k:k + 1, :] * _shift_up(cur, nxt, K - 1 - k)
        o_ref[...] = jnp.concatenate([dgb, dp * h, dp * gc], axis=1).astype(BF16)

    return _pcall(body, name=name, grid=(C // tc, nl),
                  in_specs=[pl.BlockSpec((tl, 3 * tc), lambda j, i: (i, j)),
                            pl.BlockSpec((SUBLANES, 3 * tc), _halo_prev(tl, lambda j: j)),
                            pl.BlockSpec((tl, tc), lambda j, i: (i, j)),
                            pl.BlockSpec((tl, tc), lambda j, i: (i, j)),
                            pl.BlockSpec((SUBLANES, tc), lambda j, i: (jnp.minimum((i + 1) * (tl // SUBLANES), L // SUBLANES - 1), j)),
                            pl.BlockSpec((K, tc), lambda j, i: (0, j))],
                  out_specs=pl.BlockSpec((tl, 3 * tc), lambda j, i: (i, j)),
                  out_shape=jax.ShapeDtypeStruct((L, C3), BF16),
                  compiler_params=_cparams(("parallel", "parallel")))(Ac, Ac, dy, dc, dc, w)


def _fox_gate_fwd(Af, bf, name):
    L, W = Af.shape
    tl = _pick(L, (512, 256, 128))

    def body(a_ref, b_ref, f_ref, carry):
        i = pl.program_id(0)

        @pl.when(i == 0)
        def _():
            carry[...] = jnp.zeros_like(carry)

        z = a_ref[...] + b_ref[...]
        logf = jnp.minimum(z, 0.0) - jnp.log(1.0 + jnp.exp(-jnp.abs(z)))
        f = _tri_mm(_tri(tl), logf) + carry[...]
        f_ref[...] = f
        carry[...] = f[tl - 1:tl, :]

    row = pl.BlockSpec((tl, W), lambda i: (i, 0))
    return _pcall(body, name=name, grid=(L // tl,), in_specs=[row, pl.BlockSpec((1, W), lambda i: (0, 0))], out_specs=row,
                  out_shape=jax.ShapeDtypeStruct((L, W), F32), scratch_shapes=[pltpu.VMEM((1, W), F32)],
                  compiler_params=_cparams(("arbitrary",)))(Af, bf)


def _fox_gate_bwd(Af, bf, dF, name):
    L, W = Af.shape
    tl = _pick(L, (512, 256, 128))
    nl = L // tl

    def body(a_ref, b_ref, df_ref, o_ref, db_ref, carry):
        i = pl.program_id(0)

        @pl.when(i == 0)
        def _():
            carry[...] = jnp.zeros_like(carry)
            db_ref[...] = jnp.zeros_like(db_ref)

        z = a_ref[...] + b_ref[...]
        dlogf = _tri_mm(_tri(tl, upper=True), df_ref[...]) + carry[...]
        carry[...] = dlogf[0:1, :]
        dz = dlogf * _sigmoid(-z)
        o_ref[...] = dz
        db_ref[...] += _sum8(dz)

    row = pl.BlockSpec((tl, W), lambda i: (nl - 1 - i, 0))
    return _pcall(body, name=name, grid=(nl,),
                  in_specs=[row, pl.BlockSpec((1, W), lambda i: (0, 0)), row],
                  out_specs=[row, pl.BlockSpec((SUBLANES, W), lambda i: (0, 0))],
                  out_shape=[jax.ShapeDtypeStruct((L, W), F32), jax.ShapeDtypeStruct((SUBLANES, W), F32)],
                  scratch_shapes=[pltpu.VMEM((1, W), F32)],
                  compiler_params=_cparams(("arbitrary",)))(Af, bf, dF)


def _attn_tiles(L):
    t = _pick(L, (512, 256, 128))
    return t, t


def _attn_scores(q, k, fq, fk, diag, scale):
    s = _dot(q, k, NT) * scale + (fq - fk)
    if not diag:
        return s
    row = lax.broadcasted_iota(jnp.int32, s.shape, 0)
    col = lax.broadcasted_iota(jnp.int32, s.shape, 1)
    return jnp.where(col <= row, s, NEG)


def _attn_geometry():
    Dh = FOX_HEAD_DIM
    hpt = LANES // Dh
    return Dh, hpt, FOX_HEADS // hpt


def _head_lanes(shape, Dh, hpt):
    lane = lax.broadcasted_iota(jnp.int32, shape, len(shape) - 1)
    return [(lane >= h * Dh) & (lane < (h + 1) * Dh) for h in range(hpt)]


def _attn_specs(t, L, hpt, ng):
    return dict(
        col=lambda off: pl.BlockSpec((t, LANES), lambda g, i: (i, g + off)),
        full=lambda off: pl.BlockSpec((L, LANES), lambda g, i: (0, g + off)),
        hq=pl.BlockSpec((hpt, t, 1), lambda g, i: (g, i, 0)),
        hk_full=pl.BlockSpec((hpt, 1, L), lambda g, i: (g, 0, 0)),
        hk=pl.BlockSpec((hpt, 1, t), lambda g, i: (g, 0, i)))


def _attn_fwd(qkv, Fq, Fk, name, gather=()):
    L = qkv.shape[0]
    Dh, hpt, ng = _attn_geometry()
    t, _ = _attn_tiles(L)
    scale = Dh ** -0.5
    n = len(gather)
    nsteps = ng * (L // t)

    def body(*refs):
        q_ref, k_ref, v_ref, fq_ref, fk_ref = refs[:5]
        o_ref, lse_ref = refs[5 + n:7 + n]
        qi = pl.program_id(1)
        step = pl.program_id(0) * (L // t) + qi
        if n:
            start, forward, finish = _gather_phases(refs[5:5 + n], refs[7 + n:7 + 2 * n], *refs[7 + 2 * n:])
            pl.when(step == 0)(start)
            pl.when(step == nsteps // 2)(forward)
        sel = _head_lanes((t, LANES), Dh, hpt)
        q2 = q_ref[...]
        qh = [jnp.where(sel[h], q2, 0) for h in range(hpt)]
        fq = [fq_ref[h] for h in range(hpt)]

        def chunk(j, carry, diag):
            rows = pl.ds(pl.multiple_of(j * t, t), t)
            kc, vc = k_ref[rows, :], v_ref[rows, :]
            out = []
            for h in range(hpt):
                m, l, acc = carry[h]
                s = _attn_scores(qh[h], kc, fq[h], fk_ref[h, :, rows], diag, scale)
                m_new = jnp.maximum(m, jnp.max(s, axis=-1, keepdims=True))
                p = jnp.exp(s - m_new)
                a = jnp.exp(m - m_new)
                out.append((m_new, a * l + jnp.sum(p, axis=-1, keepdims=True), a * acc + _dot(p.astype(BF16), vc, NN)))
            return tuple(out)

        init = tuple((jnp.full((t, 1), NEG, F32), jnp.zeros((t, 1), F32), jnp.zeros((t, LANES), F32)) for _ in range(hpt))
        fin = chunk(qi, lax.fori_loop(0, qi, lambda j, c: chunk(j, c, False), init), True)
        o = jnp.zeros((t, LANES), F32)
        for h, (m, l, acc) in enumerate(fin):
            o = jnp.where(sel[h], acc / l, o)
            lse_ref[h] = m + jnp.log(l)
        o_ref[...] = o.astype(BF16)
        if n:
            pl.when(step == nsteps - 1)(finish)

    sp = _attn_specs(t, L, hpt, ng)
    return _pcall(body, name=name, grid=(ng, L // t),
                  in_specs=[sp["col"](0), sp["full"](ng), sp["full"](2 * ng), sp["hq"], sp["hk_full"]] + [HBM_SPEC] * n,
                  out_specs=[sp["col"](0), sp["hq"]] + [HBM_SPEC] * n,
                  out_shape=[jax.ShapeDtypeStruct((L, ng * LANES), BF16), jax.ShapeDtypeStruct((FOX_HEADS, L, 1), F32)]
                  + _gather_shapes(gather),
                  scratch_shapes=_gather_sems(n) if n else [],
                  compiler_params=_cparams(("arbitrary", "arbitrary") if n else ("parallel", "arbitrary")))(
        qkv, qkv, qkv, Fq, Fk, *gather)


def _attn_bwd(qkv, Fq, Fk, lse, do, name, exchange=()):
    L = qkv.shape[0]
    Dh, hpt, ng = _attn_geometry()
    _, tk = _attn_tiles(L)
    tq = _pick(L, (256, 128))
    nkc = L // tk
    scale = Dh ** -0.5

    n = len(exchange)
    nsteps = ng * (L // tq)

    def body(*refs):
        q_ref, k_ref, v_ref, fq_ref, fk_ref, lse_ref, do_ref = refs[:7]
        dq_ref, dk_ref, dv_ref, df_ref = refs[7 + n:11 + n]
        p_s, dp_s = refs[11 + 2 * n:13 + 2 * n]
        qi = pl.program_id(1)
        step = pl.program_id(0) * (L // tq) + qi
        if n:
            start, finish = _exchange_phases(refs[7:7 + n], refs[11 + n:11 + 2 * n], *refs[13 + 2 * n:])
            pl.when(step == 0)(start)

        @pl.when(qi == 0)
        def _():
            dk_ref[...] = jnp.zeros_like(dk_ref)
            dv_ref[...] = jnp.zeros_like(dv_ref)
            df_ref[...] = jnp.zeros_like(df_ref)

        sel = _head_lanes((tq, LANES), Dh, hpt)
        q2, do2 = q_ref[...], do_ref[...]
        jd = (qi * tq) // tk
        off = qi * tq - jd * tk
        dq = jnp.zeros((tq, LANES), F32)
        for h in range(hpt):
            qh, doh = jnp.where(sel[h], q2, 0), jnp.where(sel[h], do2, 0)
            fq, lse = fq_ref[h], lse_ref[h]

            def first(j, acc, diag):
                rows = pl.ds(pl.multiple_of(j * tk, tk), tk)
                s = _dot(qh, k_ref[rows, :], NT) * scale + (fq - fk_ref[h, :, rows])
                if diag:
                    row = lax.broadcasted_iota(jnp.int32, s.shape, 0) + off
                    s = jnp.where(lax.broadcasted_iota(jnp.int32, s.shape, 1) <= row, s, NEG)
                p = jnp.exp(s - lse)
                dp = _dot(doh, v_ref[rows, :], NT)
                p_s[j] = p
                dp_s[j] = dp
                return acc + jnp.sum(p * dp, axis=-1, keepdims=True)

            delta = first(jd, lax.fori_loop(0, jd, lambda j, c: first(j, c, False), jnp.zeros((tq, 1), F32)), True)

            def second(j, acc):
                rows = pl.ds(pl.multiple_of(j * tk, tk), tk)
                p = p_s[j]
                ds = p * (dp_s[j] - delta)
                dsb = ds.astype(BF16)
                dk_ref[rows, :] += _dot(dsb, qh, TN)
                dv_ref[rows, :] += _dot(p.astype(BF16), doh, TN)
                df_ref[h, :, rows] -= jnp.sum(ds, axis=0, keepdims=True)
                return acc + _dot(dsb, k_ref[rows, :], NN)

            dq = jnp.where(sel[h], lax.fori_loop(0, jd + 1, second, jnp.zeros((tq, LANES), F32)), dq)
        dq_ref[...] = (dq * scale).astype(BF16)

        @pl.when(qi == L // tq - 1)
        def _():
            dk_ref[...] *= scale

        if n:
            pl.when(step == nsteps - 1)(finish)

    sp = _attn_specs(tq, L, hpt, ng)
    return _pcall(body, name=name, grid=(ng, L // tq),
                  in_specs=[sp["col"](0), sp["full"](ng), sp["full"](2 * ng), sp["hq"], sp["hk_full"], sp["hq"], sp["col"](0)]
                  + [HBM_SPEC] * n,
                  out_specs=[sp["col"](0), sp["full"](0), sp["full"](0), sp["hk_full"]] + [HBM_SPEC] * n,
                  out_shape=[jax.ShapeDtypeStruct((L, ng * LANES), BF16), jax.ShapeDtypeStruct((L, ng * LANES), F32),
                             jax.ShapeDtypeStruct((L, ng * LANES), F32), jax.ShapeDtypeStruct((FOX_HEADS, 1, L), F32)]
                  + [jax.ShapeDtypeStruct(g.shape, g.dtype) for g in exchange],
                  scratch_shapes=[pltpu.VMEM((nkc, tq, tk), F32), pltpu.VMEM((nkc, tq, tk), F32)] + (_gather_sems(n) if n else []),
                  compiler_params=pltpu.CompilerParams(
                      vmem_limit_bytes=ATTN_BWD_VMEM_LIMIT,
                      dimension_semantics=("arbitrary", "arbitrary") if n else ("parallel", "arbitrary")))(
        qkv, qkv, qkv, Fq, Fk, lse, do, *exchange)


def _mconv_fwd(xr, w, b, name):
    L, C = xr.shape
    K = w.shape[0]
    tl = _conv_row_tile(L, False)
    tc = _pick(C, (512, 384, 256, 128))

    def body(x_ref, xp_ref, w_ref, b_ref, o_ref):
        i = pl.program_id(1)
        prev = jnp.where(i == 0, 0.0, xp_ref[...])
        pre = _conv_taps(x_ref[...], prev, w_ref, K) + b_ref[...]
        o_ref[...] = pre * _sigmoid(pre)

    return _pcall(body, name=name, grid=(C // tc, L // tl),
                  in_specs=[pl.BlockSpec((tl, tc), lambda j, i: (i, j)),
                            pl.BlockSpec((SUBLANES, tc), _halo_prev(tl, lambda j: j)),
                            pl.BlockSpec((K, tc), lambda j, i: (0, j)),
                            pl.BlockSpec((1, tc), lambda j, i: (0, j))],
                  out_specs=pl.BlockSpec((tl, tc), lambda j, i: (i, j)),
                  out_shape=jax.ShapeDtypeStruct((L, C), F32),
                  compiler_params=_cparams(("parallel", "parallel")))(xr, xr, w, b.reshape(1, C))


def _mconv_bwd(xr, dact, w, b, name):
    L, C = xr.shape
    K = w.shape[0]
    tl = _conv_row_tile(L, True)
    tc = _pick(C, (512, 384, 256, 128))
    nl = L // tl

    R = STRIP_ROWS
    ns = tl // R

    def body(x_ref, xp_ref, xn_ref, d_ref, dn_ref, w_ref, b_ref, o_ref, dw_ref, db_ref):
        i = pl.program_id(1)

        @pl.when(i == 0)
        def _():
            dw_ref[...] = jnp.zeros_like(dw_ref)
            db_ref[...] = jnp.zeros_like(db_ref)

        for lanes in (slice(c, c + LANES) for c in range(0, tc, LANES)):
            wts = [w_ref[k:k + 1, lanes] for k in range(K)]
            bias = b_ref[:, lanes]

            def at_conv_out(x, xprev8, d):
                sh = [_shift_down(x, xprev8, K - 1 - k) for k in range(K)]
                pre = sum(wts[k] * sh[k] for k in range(K)) + bias
                sg = _sigmoid(pre)
                return d * sg * (1.0 + pre * (1.0 - sg)), sh

            def emit(rows, dpre, dpre_next8):
                o_ref[rows, lanes] = sum(wts[k] * _shift_up(dpre, dpre_next8, K - 1 - k) for k in range(K)).astype(BF16)

            def accumulate(acc, dpre, sh):
                return tuple(acc[k] + _sum8(dpre * sh[k]) for k in range(K)) + (acc[K] + _sum8(dpre),)

            x0 = x_ref[_strip(0, R), lanes]
            dpre0, sh0 = at_conv_out(x0, jnp.where(i == 0, 0.0, xp_ref[:, lanes]), d_ref[_strip(0, R), lanes])
            acc0 = accumulate((jnp.zeros((SUBLANES, LANES), F32),) * (K + 1), dpre0, sh0)

            def step(s, carry):
                xprev, dprev, acc = carry
                x = x_ref[_strip(s, R), lanes]
                dpre, sh = at_conv_out(x, xprev[R - SUBLANES:], d_ref[_strip(s, R), lanes])
                emit(_strip(s - 1, R), dprev, dpre[:SUBLANES])
                return x, dpre, accumulate(acc, dpre, sh)

            xl, dl, acc = lax.fori_loop(1, ns, step, (x0, dpre0, acc0))
            dn, _ = at_conv_out(xn_ref[:, lanes], xl[R - SUBLANES:], dn_ref[:, lanes])
            emit(_strip(ns - 1, R), dl, jnp.where(i == nl - 1, 0.0, dn))
            db_ref[:, lanes] += acc[K]
            for k in range(K):
                dw_ref[k * SUBLANES:(k + 1) * SUBLANES, lanes] += acc[k]

    return _pcall(body, name=name, grid=(C // tc, nl),
                  in_specs=[pl.BlockSpec((tl, tc), lambda j, i: (i, j)),
                            pl.BlockSpec((SUBLANES, tc), _halo_prev(tl, lambda j: j)),
                            pl.BlockSpec((SUBLANES, tc), _halo_next(tl, L, SUBLANES)),
                            pl.BlockSpec((tl, tc), lambda j, i: (i, j)),
                            pl.BlockSpec((SUBLANES, tc), _halo_next(tl, L, SUBLANES)),
                            pl.BlockSpec((K, tc), lambda j, i: (0, j)),
                            pl.BlockSpec((1, tc), lambda j, i: (0, j))],
                  out_specs=[pl.BlockSpec((tl, tc), lambda j, i: (i, j)),
                             pl.BlockSpec((K * SUBLANES, tc), lambda j, i: (0, j)),
                             pl.BlockSpec((SUBLANES, tc), lambda j, i: (0, j))],
                  out_shape=[jax.ShapeDtypeStruct((L, C), BF16), jax.ShapeDtypeStruct((K * SUBLANES, C), F32),
                             jax.ShapeDtypeStruct((SUBLANES, C), F32)],
                  compiler_params=_cparams(("parallel", "arbitrary")))(xr, xr, xr, dact, dact, w, b.reshape(1, C))


def _head_selector(R, P, heads_first):
    shape = (R, R * P) if heads_first else (R * P, R)
    head = lax.broadcasted_iota(jnp.int32, shape, 0 if heads_first else 1)
    lane = lax.broadcasted_iota(jnp.int32, shape, 1 if heads_first else 0)
    d = lane - head * P
    return jnp.where((d >= 0) & (d < P), 1.0, 0.0).astype(BF16)


def _ssd_prelude(dtc_ref, dtr_ref, bc_ref, br_ref, ac_ref, ar_ref, Q, R, P):
    raw_c = dtc_ref[...] + bc_ref[...]
    dt_c = _softplus(raw_c)
    dt_r = _softplus(dtr_ref[...] + br_ref[...])
    A_c = -jnp.exp(ac_ref[...])
    acs_c = _tri_mm(_tri(Q), dt_c * A_c)
    acs_r = _tri_mm(_tri(Q, upper=True), dt_r * (-jnp.exp(ar_ref[...])), tri_first=False)
    ea_c = jnp.exp(acs_c)
    dte_c = jnp.exp(acs_c[Q - 1:Q, :] - acs_c)
    wide = _tri_mm(_head_selector(R, P, True), jnp.concatenate([dt_c, ea_c, dte_c], axis=0), tri_first=False)
    return dict(raw_c=raw_c, dt_c=dt_c, A_c=A_c, acs_c=acs_c, acs_r=acs_r, ea_c=ea_c,
                DT=wide[:Q], EA=wide[Q:2 * Q], DTE=wide[2 * Q:])


def _ssd_decay_tile(pre, r, mask):
    return jnp.exp(jnp.where(mask, pre["acs_c"][:, r:r + 1] - pre["acs_r"][r:r + 1, :], NEG))


def _ssd_specs(Q, R, P, N, G, inner, rev=None):
    cc = (lambda c: c) if rev is None else rev
    return dict(
        x=pl.BlockSpec((Q, R * P), lambda g, c: (cc(c), g)),
        b=pl.BlockSpec((Q, N), lambda g, c: (cc(c), inner // N + g)),
        c=pl.BlockSpec((Q, N), lambda g, c: (cc(c), inner // N + G + g)),
        dtc=pl.BlockSpec((None, Q, R), lambda g, c: (g, cc(c), 0)),
        dtr=pl.BlockSpec((None, R, Q), lambda g, c: (g, 0, cc(c))),
        pc=pl.BlockSpec((None, 1, R), lambda g, c: (g, 0, 0)),
        pr=pl.BlockSpec((None, R, 1), lambda g, c: (g, 0, 0)),
        px=pl.BlockSpec((None, 1, R * P), lambda g, c: (g, 0, 0)),
        st=pl.BlockSpec((None, None, N, R * P), lambda g, c: (cc(c), g, 0, 0)))


def _ssd_fwd(act, dtc, dtr, bias_c, bias_r, alog_c, alog_r, dsk_x, name):
    G, L, R = dtc.shape
    N, P, Q = SSM_STATE, SSM_HEAD_DIM, SSM_CHUNK
    RP = R * P
    inner = G * RP
    nc = L // Q

    def body(x_ref, b_ref, c_ref, dtc_ref, dtr_ref, bc_ref, br_ref, ac_ref, ar_ref, dk_ref, y_ref, hp_ref, st):
        c = pl.program_id(1)

        @pl.when(c == 0)
        def _():
            st[...] = jnp.zeros_like(st)

        pre = _ssd_prelude(dtc_ref, dtr_ref, bc_ref, br_ref, ac_ref, ar_ref, Q, R, P)
        X = x_ref[...]
        XT = X * pre["DT"]
        Bb = b_ref[...].astype(BF16)
        Cb = c_ref[...].astype(BF16)
        CB = _dot(Cb, Bb, NT)
        mask = lax.broadcasted_iota(jnp.int32, (Q, Q), 0) >= lax.broadcasted_iota(jnp.int32, (Q, Q), 1)
        low = lax.broadcasted_iota(jnp.int32, (Q, 2 * P), 1) < P
        pieces = []
        for k in range(R // 2):
            xt2 = XT[:, 2 * P * k:2 * P * (k + 1)]
            acc = None
            for half in range(2):
                Gm = CB * _ssd_decay_tile(pre, 2 * k + half, mask)
                part = _dot(Gm.astype(BF16), jnp.where(low == (half == 0), xt2, 0.0).astype(BF16), NN)
                acc = part if acc is None else acc + part
            pieces.append(acc)
        HP = st[...]
        hp_ref[...] = HP
        yoff = pre["EA"] * _dot(Cb, HP.astype(BF16), NN)
        st[...] = HP * pre["EA"][Q - 1:Q, :] + _dot(Bb, (XT * pre["DTE"]).astype(BF16), TN)
        y_ref[...] = jnp.concatenate(pieces, axis=1) + yoff + dk_ref[...] * X

    sp = _ssd_specs(Q, R, P, N, G, inner)
    return _pcall(body, name=name, grid=(G, nc),
                  in_specs=[sp["x"], sp["b"], sp["c"], sp["dtc"], sp["dtr"], sp["pc"], sp["pr"], sp["pc"], sp["pr"], sp["px"]],
                  out_specs=[sp["x"], sp["st"]],
                  out_shape=[jax.ShapeDtypeStruct((L, inner), F32), jax.ShapeDtypeStruct((nc, G, N, RP), F32)],
                  scratch_shapes=[pltpu.VMEM((N, RP), F32)],
                  compiler_params=_cparams(("parallel", "arbitrary")))(act, act, act, dtc, dtr, bias_c, bias_r, alog_c, alog_r, dsk_x)


def _ssd_bwd(act, dtc, dtr, bias_c, bias_r, alog_c, alog_r, dsk_x, hprev, dy, name):
    G, L, R = dtc.shape
    N, P, Q = SSM_STATE, SSM_HEAD_DIM, SSM_CHUNK
    RP = R * P
    inner = G * RP
    nc = L // Q

    def body(x_ref, b_ref, c_ref, dtc_ref, dtr_ref, bc_ref, br_ref, ac_ref, ar_ref, dk_ref, hp_ref, dy_ref,
             dx_ref, db_ref, dc_ref, ddt_ref, gbias_ref, galog_ref, gdsk_ref, dst):
        c = pl.program_id(1)

        @pl.when(c == 0)
        def _():
            dst[...] = jnp.zeros_like(dst)
            gbias_ref[...] = jnp.zeros_like(gbias_ref)
            galog_ref[...] = jnp.zeros_like(galog_ref)
            gdsk_ref[...] = jnp.zeros_like(gdsk_ref)

        pre = _ssd_prelude(dtc_ref, dtr_ref, bc_ref, br_ref, ac_ref, ar_ref, Q, R, P)
        DT, EA, DTE = pre["DT"], pre["EA"], pre["DTE"]
        E_END = EA[Q - 1:Q, :]
        X, DY = x_ref[...], dy_ref[...]
        XT = X * DT
        Bb = b_ref[...].astype(BF16)
        Cb = c_ref[...].astype(BF16)
        CB = _dot(Cb, Bb, NT)
        HP, dH = hp_ref[...], dst[...]
        HPb, dHb = HP.astype(BF16), dH.astype(BF16)
        EDY = EA * DY
        EDYb = EDY.astype(BF16)
        dC = _dot(EDYb, HPb, NT)
        dHP = _dot(Cb, EDYb, TN)
        da_off = EDY * _dot(Cb, HPb, NN)
        Z = _dot(Bb, dHb, NN)
        XD = XT * DTE
        dB = _dot(XD.astype(BF16), dHb, NT)
        dXT = DTE * Z
        t_x = XD * Z
        hh = jnp.sum(dH * HP, axis=0, keepdims=True) * E_END
        dst[...] = dHP + dH * E_END
        mask = lax.broadcasted_iota(jnp.int32, (Q, Q), 0) >= lax.broadcasted_iota(jnp.int32, (Q, Q), 1)
        eye = lax.broadcasted_iota(jnp.int32, (Q, Q), 0) == lax.broadcasted_iota(jnp.int32, (Q, Q), 1)
        low = lax.broadcasted_iota(jnp.int32, (Q, 2 * P), 1) < P
        lane = lax.broadcasted_iota(jnp.int32, (Q, R), 1)
        dCB = jnp.zeros((Q, Q), F32)
        da_mat = jnp.zeros((Q, R), F32)
        pieces = []
        for k in range(R // 2):
            sl = slice(2 * P * k, 2 * P * (k + 1))
            xt2, dy2 = XT[:, sl], DY[:, sl]
            acc = None
            for half in range(2):
                r = 2 * k + half
                sel = low == (half == 0)
                Lm = _ssd_decay_tile(pre, r, mask)
                Gm = CB * Lm
                dyb = jnp.where(sel, dy2, 0.0).astype(BF16)
                part = _dot(Gm.astype(BF16), dyb, TN)
                acc = part if acc is None else acc + part
                dG = jnp.where(mask, _dot(dyb, jnp.where(sel, xt2, 0.0).astype(BF16), NT), 0.0)
                Mm = dG * Gm
                dCB = dCB + dG * Lm
                colsum = jnp.sum(jnp.where(eye, jnp.sum(Mm, axis=0, keepdims=True), 0.0), axis=1, keepdims=True)
                da_mat = jnp.where(lane == r, jnp.sum(Mm, axis=1, keepdims=True) - colsum, da_mat)
            pieces.append(acc)
        dXT = dXT + jnp.concatenate(pieces, axis=1)
        dCBb = dCB.astype(BF16)
        dc_ref[...] = dC + _dot(dCBb, Bb, NN)
        db_ref[...] = dB + _dot(dCBb, Cb, TN)
        dx_ref[...] = dXT * DT + dk_ref[...] * DY
        pad = jnp.zeros((SUBLANES - 1, RP), F32)
        sums = _tri_mm(_head_selector(R, P, False), jnp.concatenate([da_off, t_x, dXT * X, DY * X, hh, pad], axis=0), tri_first=False)
        t = sums[Q:2 * Q]
        da_end = jnp.sum(t, axis=0, keepdims=True) + sums[4 * Q:4 * Q + 1]
        rowi = lax.broadcasted_iota(jnp.int32, (Q, R), 0)
        da_mat = da_mat + sums[:Q] - t + jnp.where(rowi == Q - 1, da_end, 0.0)
        ddtA = _tri_mm(_tri(Q, upper=True), da_mat)
        ddt_raw = (ddtA * pre["A_c"] + sums[2 * Q:3 * Q]) * _sigmoid(pre["raw_c"])
        ddt_ref[...] = ddt_raw
        gbias_ref[...] += jnp.sum(ddt_raw, axis=0, keepdims=True)
        galog_ref[...] += jnp.sum(ddtA * pre["dt_c"], axis=0, keepdims=True) * pre["A_c"]
        gdsk_ref[...] += jnp.sum(sums[3 * Q:4 * Q], axis=0, keepdims=True)

    sp = _ssd_specs(Q, R, P, N, G, inner, rev=lambda c: nc - 1 - c)
    bout = pl.BlockSpec((Q, N), lambda g, c: (nc - 1 - c, g))
    return _pcall(body, name=name, grid=(G, nc),
                  in_specs=[sp["x"], sp["b"], sp["c"], sp["dtc"], sp["dtr"], sp["pc"], sp["pr"], sp["pc"], sp["pr"], sp["px"],
                            sp["st"], sp["x"]],
                  out_specs=[sp["x"], bout, bout, sp["dtc"], sp["pc"], sp["pc"], sp["pc"]],
                  out_shape=[jax.ShapeDtypeStruct((L, inner), F32), jax.ShapeDtypeStruct((L, G * N), F32),
                             jax.ShapeDtypeStruct((L, G * N), F32), jax.ShapeDtypeStruct((G, L, R), F32),
                             jax.ShapeDtypeStruct((G, 1, R), F32), jax.ShapeDtypeStruct((G, 1, R), F32),
                             jax.ShapeDtypeStruct((G, 1, R), F32)],
                  scratch_shapes=[pltpu.VMEM((N, RP), F32)],
                  compiler_params=_cparams(("parallel", "arbitrary")))(
        act, act, act, dtc, dtr, bias_c, bias_r, alog_c, alog_r, dsk_x, hprev, dy)


def _gnorm_fwd(y, z, g, name):
    L, Dn = y.shape
    gs = Dn // SSM_GROUPS
    tl = _row_tile(L, wide=True)

    def body(y_ref, z_ref, g_ref, o_ref):
        for k in range(SSM_GROUPS):
            sl = slice(k * gs, (k + 1) * gs)
            zz = z_ref[:, sl]
            u = y_ref[:, sl] * zz * _sigmoid(zz)
            rstd = lax.rsqrt(jnp.mean(u * u, axis=-1, keepdims=True) + RMS_EPS)
            o_ref[:, sl] = (u * rstd * g_ref[:, sl]).astype(BF16)

    row = pl.BlockSpec((tl, Dn), lambda i: (i, 0))
    return _pcall(body, name=name, grid=(L // tl,), in_specs=[row, row, pl.BlockSpec((1, Dn), lambda i: (0, 0))],
                  out_specs=row, out_shape=jax.ShapeDtypeStruct((L, Dn), BF16),
                  compiler_params=_cparams(("parallel",)))(y, z, g.reshape(1, Dn))


def _gnorm_bwd(y, z, g, dout, name):
    L, Dn = y.shape
    gs = Dn // SSM_GROUPS
    tl = _row_tile(L, wide=True)

    def body(y_ref, z_ref, g_ref, d_ref, dy_ref, dz_ref, dg_ref):
        i = pl.program_id(0)

        @pl.when(i == 0)
        def _():
            dg_ref[...] = jnp.zeros_like(dg_ref)

        for k in range(SSM_GROUPS):
            sl = slice(k * gs, (k + 1) * gs)
            zz = z_ref[:, sl]
            yy = y_ref[:, sl]
            sg = _sigmoid(zz)
            sil = zz * sg
            u = yy * sil
            rstd = lax.rsqrt(jnp.mean(u * u, axis=-1, keepdims=True) + RMS_EPS)
            n = u * rstd
            d = d_ref[:, sl]
            dn = d * g_ref[:, sl]
            du = rstd * (dn - n * jnp.mean(dn * n, axis=-1, keepdims=True))
            dy_ref[:, sl] = du * sil
            dz_ref[:, sl] = (du * yy * sg * (1.0 + zz * (1.0 - sg))).astype(BF16)
            dg_ref[:, sl] += _sum8(d * n)

    row = pl.BlockSpec((tl, Dn), lambda i: (i, 0))
    return _pcall(body, name=name, grid=(L // tl,), in_specs=[row, row, pl.BlockSpec((1, Dn), lambda i: (0, 0)), row],
                  out_specs=[row, row, pl.BlockSpec((SUBLANES, Dn), lambda i: (0, 0))],
                  out_shape=[jax.ShapeDtypeStruct((L, Dn), F32), jax.ShapeDtypeStruct((L, Dn), BF16),
                             jax.ShapeDtypeStruct((SUBLANES, Dn), F32)],
                  compiler_params=_cparams(("arbitrary",)))(y, z, g.reshape(1, Dn), dout)


def _adamw(w, g, m, v, name):
    rows, W = w.shape
    tr = _pick(rows, (512, 256, 128, 64, 32, 16, 8))
    c1 = 1.0 / (1.0 - ADAM_B1 ** ADAM_STEP)
    c2 = 1.0 / (1.0 - ADAM_B2 ** ADAM_STEP)

    def body(w_ref, g_ref, m_ref, v_ref, d_ref, nm_ref, nv_ref):
        g_ = g_ref[...]
        nm = ADAM_B1 * m_ref[...] + (1.0 - ADAM_B1) * g_
        nv = ADAM_B2 * v_ref[...] + (1.0 - ADAM_B2) * (g_ * g_)
        nm_ref[...] = nm
        nv_ref[...] = nv
        d_ref[...] = -ADAM_LR * ((nm * c1) / (jnp.sqrt(nv * c2) + ADAM_EPS) + ADAM_WD * w_ref[...])

    blk = pl.BlockSpec((tr, W), lambda i: (i, 0))
    return _pcall(body, name=name, grid=(rows // tr,), in_specs=[blk] * 4, out_specs=[blk] * 3,
                  out_shape=[jax.ShapeDtypeStruct((rows, W), F32)] * 3, compiler_params=_cparams(("parallel",)))(w, g, m, v)


def _sum_slots(x, name, extra=None):
    n, rows, W = x.shape
    tr = _pick(rows, (512, 256, 128, 64, 32, 16, 8))
    has_extra = extra is not None

    def body(*refs):
        if has_extra:
            e_ref, x_ref, o_ref = refs
            acc = e_ref[...].astype(F32)
            start = 0
        else:
            x_ref, o_ref = refs
            acc = x_ref[0].astype(F32)
            start = 1
        for s in range(start, n):
            acc = acc + x_ref[s].astype(F32)
        o_ref[...] = acc

    blk = pl.BlockSpec((tr, W), lambda i: (i, 0))
    xblk = pl.BlockSpec((n, tr, W), lambda i: (0, i, 0))
    return _pcall(body, name=name, grid=(rows // tr,), in_specs=([blk] if has_extra else []) + [xblk], out_specs=blk,
                  out_shape=jax.ShapeDtypeStruct((rows, W), F32), compiler_params=_cparams(("parallel",)))(
        *(([extra] if has_extra else []) + [x]))


def _add_pairs(a, b, name):
    n, rows, W = a.shape
    tr = _pick(rows, (512, 256, 128, 64, 32, 16, 8))

    def body(a_ref, b_ref, o_ref):
        o_ref[...] = (a_ref[...].astype(F32) + b_ref[...].astype(F32)).astype(BF16)

    blk = pl.BlockSpec((None, tr, W), lambda s, i: (s, i, 0))
    return _pcall(body, name=name, grid=(n, rows // tr), in_specs=[blk, blk], out_specs=blk,
                  out_shape=jax.ShapeDtypeStruct((n, rows, W), BF16), compiler_params=_cparams(("parallel", "parallel")))(a, b)


MESH = pl.DeviceIdType.MESH
HBM_SPEC = pl.BlockSpec(memory_space=pl.ANY)


def _me():
    return lax.axis_index("x"), lax.axis_index("y"), lax.axis_index("c")


def _all_gather(arrs, name):
    n = len(arrs)

    def body(*refs):
        start, forward, finish = _gather_phases(refs[:n], refs[n:2 * n], *refs[2 * n:])
        start()
        forward()
        finish()

    return _pcall(body, name=name, in_specs=[HBM_SPEC] * n, out_specs=[HBM_SPEC] * n,
                  out_shape=_gather_shapes(arrs), scratch_shapes=_gather_sems(n))(*arrs)


def _gather_shapes(arrs):
    return [jax.ShapeDtypeStruct((N_DEV,) + a.shape, a.dtype) for a in arrs]


def _gather_sems(n):
    return [pltpu.SemaphoreType.DMA((7 * n,)), pltpu.SemaphoreType.DMA((7 * n,)), pltpu.SemaphoreType.DMA((n,))]


def _gather_phases(ins, outs, send_sems, recv_sems, local_sems):
    n = len(ins)
    x, y, c = _me()
    me, sib = (x, y, c), (x, y, 1 - c)
    chips = [(1 - x, y), (x, 1 - y), (1 - x, 1 - y)]

    def slot(a, dev):
        return outs[a].at[4 * dev[0] + 2 * dev[1] + dev[2]]

    def copy(a, k, block, to, src=None):
        return pltpu.make_async_remote_copy(src_ref=slot(a, block) if src is None else src, dst_ref=slot(a, block),
                                            send_sem=send_sems.at[a * 7 + k], recv_sem=recv_sems.at[a * 7 + k],
                                            device_id=to, device_id_type=MESH)

    def mine():
        return [pltpu.make_async_copy(ins[a], slot(a, me), local_sems.at[a]) for a in range(n)]

    def first():
        out = []
        for a in range(n):
            out.append(copy(a, 0, me, sib, src=ins[a]))
            out += [copy(a, 1 + j, me, (*chip, c), src=ins[a]) for j, chip in enumerate(chips)]
        return out

    def passed():
        return [copy(a, 4 + j, (*chip, c), sib) for j, chip in enumerate(chips) for a in range(n)]

    def start():
        for cp in mine() + first():
            cp.start()

    def forward():
        fws = passed()
        for j, chip in enumerate(chips):
            for a in range(n):
                copy(a, 1 + j, (*chip, c), me).wait_recv()
                fws[j * n + a].start()

    def finish():
        for a in range(n):
            copy(a, 0, sib, me).wait_recv()
            for j, chip in enumerate(chips):
                copy(a, 4 + j, (*chip, 1 - c), me).wait_recv()
        for cp in first() + passed():
            cp.wait_send()
        for cp in mine():
            cp.wait()

    return start, forward, finish


def _exchange_phases(gs, outs, send_sems, recv_sems, local_sems):
    n = len(gs)
    x, y, c = _me()
    my_slot = 4 * x + 2 * y + c
    flips = [(fx, fy, fc) for fx in (0, 1) for fy in (0, 1) for fc in (0, 1) if fx or fy or fc]

    def peer(f):
        return tuple(1 - v if flip else v for v, flip in zip((x, y, c), f))

    def copies():
        out = []
        for a in range(n):
            for k, f in enumerate(flips):
                px, py, pc = peer(f)
                out.append(pltpu.make_async_remote_copy(
                    src_ref=gs[a].at[4 * px + 2 * py + pc], dst_ref=outs[a].at[my_slot],
                    send_sem=send_sems.at[a * 7 + k], recv_sem=recv_sems.at[a * 7 + k],
                    device_id=(px, py, pc), device_id_type=MESH))
        return out

    def arrivals():
        out = []
        for a in range(n):
            for k, f in enumerate(flips):
                px, py, pc = peer(f)
                slot = outs[a].at[4 * px + 2 * py + pc]
                out.append(pltpu.make_async_remote_copy(src_ref=slot, dst_ref=slot, send_sem=send_sems.at[a * 7 + k],
                                                        recv_sem=recv_sems.at[a * 7 + k], device_id=(px, py, pc),
                                                        device_id_type=MESH))
        return out

    def mine():
        return [pltpu.make_async_copy(gs[a].at[my_slot], outs[a].at[my_slot], local_sems.at[a]) for a in range(n)]

    def start():
        for cp in mine() + copies():
            cp.start()

    def finish():
        for cp in arrivals():
            cp.wait_recv()
        for cp in copies():
            cp.wait_send()
        for cp in mine():
            cp.wait()

    return start, finish


def _rs_sibling(gs, name):
    n = len(gs)

    def body(*refs):
        g_refs, o_refs = refs[:n], refs[n:2 * n]
        send_sems, recv_sems = refs[2 * n:]
        x, y, c = _me()
        sib = (x, y, 1 - c)
        cps = [pltpu.make_async_remote_copy(src_ref=g_refs[a].at[2 * q + (1 - c)], dst_ref=o_refs[a].at[q],
                                            send_sem=send_sems.at[4 * a + q], recv_sem=recv_sems.at[4 * a + q],
                                            device_id=sib, device_id_type=MESH) for a in range(n) for q in range(4)]
        for cp in cps:
            cp.start()
        for cp in cps:
            cp.wait()

    return _pcall(body, name=name, in_specs=[HBM_SPEC] * n, out_specs=[HBM_SPEC] * n,
                  out_shape=[jax.ShapeDtypeStruct((4,) + g.shape[1:], g.dtype) for g in gs],
                  scratch_shapes=[pltpu.SemaphoreType.DMA((4 * n,)), pltpu.SemaphoreType.DMA((4 * n,))])(*gs)


def _rs_chips(ps, name):
    n = len(ps)

    def body(*refs):
        p_refs, o_refs = refs[:n], refs[n:2 * n]
        send_sems, recv_sems = refs[2 * n:]
        x, y, c = _me()
        chips = [(1 - x, y), (x, 1 - y), (1 - x, 1 - y)]
        cps = [pltpu.make_async_remote_copy(src_ref=p_refs[a].at[2 * chip[0] + chip[1]], dst_ref=o_refs[a].at[j],
                                            send_sem=send_sems.at[3 * a + j], recv_sem=recv_sems.at[3 * a + j],
                                            device_id=(*chip, c), device_id_type=MESH)
               for j, chip in enumerate(chips) for a in range(n)]
        for cp in cps:
            cp.start()
        for cp in cps:
            cp.wait()

    return _pcall(body, name=name, in_specs=[HBM_SPEC] * n, out_specs=[HBM_SPEC] * n,
                  out_shape=[jax.ShapeDtypeStruct((3,) + p.shape[1:], p.dtype) for p in ps],
                  scratch_shapes=[pltpu.SemaphoreType.DMA((3 * n,)), pltpu.SemaphoreType.DMA((3 * n,))])(*ps)


def _reduce_scatter(gs, name):
    x, y, c = _me()
    from_sib = _rs_sibling(gs, name + "_sib")
    pairs = []
    for a, (g, fs) in enumerate(zip(gs, from_sib)):
        own = g.reshape((4, 2) + g.shape[1:])
        pairs.append(_add_pairs(jnp.where(c == 0, own[:, 0], own[:, 1]), fs, f"{name}_pair{a}"))
    from_chips = _rs_chips(pairs, name + "_chips")
    return [_sum_slots(fc, f"{name}_sum{a}", extra=lax.dynamic_index_in_dim(p, 2 * x + y, axis=0, keepdims=False))
            for a, (p, fc) in enumerate(zip(pairs, from_chips))]


BIG = ("even_w_in", "even_w_out", "odd_w_in", "odd_w_out", "ffn_w_up", "ffn_w_down", "ple_w_proj", "ple_w_gate")
SMALL_SHARDED = ("even_conv_w", "odd_conv_w", "odd_conv_b", "odd_norm_g", "ffn_conv_w")
REPLICATED = ("even_b_f", "odd_dt_bias", "odd_a_log", "odd_d_skip", "ln_mix_g", "ln_mix_b", "ffn_conv_b",
              "ln_ffn_g", "ln_ffn_b", "ple_b_gate")
WEIGHTS = ("even_w_in", "even_b_f", "even_conv_w", "even_w_out", "odd_w_in", "odd_conv_w", "odd_conv_b", "odd_dt_bias",
           "odd_a_log", "odd_d_skip", "odd_norm_g", "odd_w_out", "ln_mix_g", "ln_mix_b", "ffn_w_up", "ffn_conv_w",
           "ffn_conv_b", "ffn_w_down", "ln_ffn_g", "ln_ffn_b", "ple_w_proj", "ple_w_gate", "ple_b_gate")


def _full_shapes():
    d = _dims()
    return {
        "even_w_in": ((1, D_MODEL, d["even_in"]), 2), "even_b_f": ((1, FOX_HEADS), None),
        "even_conv_w": ((1, CONV_WIDTH, CONV_DIM), 2), "even_w_out": ((1, d["even_mix"], D_MODEL), 1),
        "odd_w_in": ((1, D_MODEL, d["odd_in"]), 2), "odd_conv_w": ((1, SSM_CONV_WIDTH, d["conv_ch"]), 2),
        "odd_conv_b": ((1, d["conv_ch"]), 1), "odd_dt_bias": ((1, d["ssm_heads"]), None),
        "odd_a_log": ((1, d["ssm_heads"]), None), "odd_d_skip": ((1, d["ssm_heads"]), None),
        "odd_norm_g": ((1, d["ssm_inner"]), 1), "odd_w_out": ((1, d["ssm_inner"], D_MODEL), 1),
        "ln_mix_g": ((DEPTH, D_MODEL), None), "ln_mix_b": ((DEPTH, D_MODEL), None),
        "ffn_w_up": ((DEPTH, D_MODEL, 2 * D_FF), 2), "ffn_conv_w": ((DEPTH, FFN_CONV_WIDTH, 2 * D_FF), 2),
        "ffn_conv_b": ((DEPTH, 2 * D_FF), None), "ffn_w_down": ((DEPTH, D_FF, D_MODEL), 1),
        "ln_ffn_g": ((DEPTH, D_MODEL), None), "ln_ffn_b": ((DEPTH, D_MODEL), None),
        "ple_w_proj": ((DEPTH, PLE_DIM, D_MODEL), 2), "ple_w_gate": ((DEPTH, D_MODEL, D_MODEL), 1),
        "ple_b_gate": ((DEPTH, D_MODEL), None),
    }


def _shard_shape(name):
    shape, ax = _full_shapes()[name]
    if ax is None:
        return shape
    return tuple(s // N_DEV if i == ax else s for i, s in enumerate(shape))


def _as2d(a, lead=0):
    return a.reshape(a.shape[:lead] + (-1, a.shape[-1]))


def _part_rows(shape):
    n = int(np.prod(shape))
    return -(-(-(-n // PACK_W)) // SUBLANES) * SUBLANES


def _pack_small(parts):
    out = []
    for p in parts:
        n, rows = int(np.prod(p.shape)), _part_rows(p.shape)
        out.append(jnp.pad(p.reshape(-1).astype(F32), (0, rows * PACK_W - n)).reshape(rows, PACK_W))
    return jnp.concatenate(out, axis=0)


def _unpack_small(pack, shapes):
    lead = pack.shape[:-2]
    out, off = [], 0
    for s in shapes:
        n, rows = int(np.prod(s)), _part_rows(s)
        part = pack[..., off:off + rows, :].reshape(lead + (-1,))[..., :n]
        out.append(part.reshape(lead + tuple(s)))
        off += rows
    return out


def _assemble(gathered, name):
    shape, ax = _full_shapes()[name]
    return jnp.moveaxis(gathered, 0, ax).reshape(shape)


def _split_dest(full, name):
    shape, ax = _full_shapes()[name]
    sh = shape[:ax] + (N_DEV, shape[ax] // N_DEV) + shape[ax + 1:]
    return jnp.moveaxis(full.reshape(sh), ax, 0)


def _interleave_cols(w, parts, tc):
    C = w.shape[-1] // parts
    sh = w.shape[:-1]
    return w.reshape(sh + (parts, C // tc, tc)).swapaxes(-3, -2).reshape(sh + (parts * C,))


def _deinterleave_cols(w, parts, tc):
    C = w.shape[-1] // parts
    sh = w.shape[:-1]
    return w.reshape(sh + (C // tc, parts, tc)).swapaxes(-3, -2).reshape(sh + (parts * C,))


def _pad_cols(a, to):
    return jnp.pad(a, ((0, 0), (0, to - a.shape[1])))


def _tail_fwd(i, h_in, mix, p_i, W, sp):
    r1, h1, h1b = _ln_fwd(h_in, mix, sp["ln_mix_g"][i], sp["ln_mix_b"][i], f"ln_mix_fwd{i}")
    U = _mm(h1b, W["ffn_up"][i], "nn", F32, f"ffn_up{i}")
    S = _ffn_act_fwd(U, sp["ffn_conv_w"][i], sp["ffn_conv_b"][i], f"ffn_act_fwd{i}")
    ffn = _mm(S, W["ffn_down"][i], "nn", F32, f"ffn_down{i}")
    r2, h2, h2b = _ln_fwd(h1, ffn, sp["ln_ffn_g"][i], sp["ln_ffn_b"][i], f"ln_ffn_fwd{i}")
    G = _mm(h2b, W["ple_gate"][i], "nn", F32, f"ple_gate{i}")
    E = _mm(p_i, W["ple_proj"][i], "nn", F32, f"ple_proj{i}")
    h3, h3b = _ple_fwd(h2, G, sp["ple_b_gate"][i], E, f"ple_fwd{i}")
    return h3, h3b, dict(r1=r1, h1b=h1b, U=U, S=S, r2=r2, h2b=h2b, G=G, E=E, p=p_i)


def _tail_bwd(i, dh3, sv, W, sp, grads):
    alpha = _alpha()
    dE, dGp, dbg = _ple_bwd(dh3, sv["G"], sp["ple_b_gate"][i], sv["E"], f"ple_bwd{i}")
    grads["ple_b_gate"][i] = dbg.sum(0)
    grads["ple_w_proj"][i] = _mm(sv["p"], dE, "tn", F32, f"d_ple_proj{i}")
    grads["ple_w_gate"][i] = _mm(sv["h2b"], dGp, "tn", F32, f"d_ple_gate{i}")
    dh2 = _mm(dGp, W["ple_gate"][i], "nt", F32, f"dx_ple_gate{i}", add=dh3)
    dr2, dr2b, dg, db = _ln_bwd(sv["r2"], dh2, sp["ln_ffn_g"][i], f"ln_ffn_bwd{i}")
    grads["ln_ffn_g"][i], grads["ln_ffn_b"][i] = dg.sum(0), db.sum(0)
    grads["ffn_w_down"][i] = _mm(sv["S"], dr2b, "tn", F32, f"d_ffn_down{i}")
    dS = _mm(dr2b, W["ffn_down"][i], "nt", BF16, f"dx_ffn_down{i}")
    dUg, dUv, dwg, dwv, dbg, dbv = _ffn_act_bwd(sv["U"], dS, sp["ffn_conv_w"][i], sp["ffn_conv_b"][i], f"ffn_act_bwd{i}")
    K = FFN_CONV_WIDTH
    grads["ffn_conv_w"][i] = jnp.concatenate([dwg.reshape(K, SUBLANES, -1).sum(1), dwv.reshape(K, SUBLANES, -1).sum(1)], axis=1)
    grads["ffn_conv_b"][i] = jnp.concatenate([dbg.sum(0), dbv.sum(0)])
    grads["ffn_w_up"][i] = jnp.concatenate([_mm(sv["h1b"], dUg, "tn", F32, f"d_ffn_up_g{i}"),
                                            _mm(sv["h1b"], dUv, "tn", F32, f"d_ffn_up_v{i}")], axis=1)
    dh1 = _mm(dUg, W["ffn_up"][i], "nt", F32, f"dx_ffn_up_g{i}", add=dr2, add_scale=alpha)
    dh1 = _mm(dUv, W["ffn_up"][i], "nt", F32, f"dx_ffn_up_v{i}", add=dh1, b_k_start=D_FF)
    dr1, dr1b, dg, db = _ln_bwd(sv["r1"], dh1, sp["ln_mix_g"][i], f"ln_mix_bwd{i}")
    grads["ln_mix_g"][i], grads["ln_mix_b"][i] = dg.sum(0), db.sum(0)
    return dr1, dr1b


def _even_fwd(h, W, sp, full, gather):
    L = h.shape[0]
    H, Dh = FOX_HEADS, FOX_HEAD_DIM
    Ac = _mm(h, W["even_in_conv"], "nn", F32, "even_in_conv")
    qkv = _mm(h, W["even_in_qkv"], "nn", BF16, "even_in_qkv")
    Af = _mm(h, W["even_in_f"], "nn", F32, "even_in_f")
    y_a = _sconv_fwd(Ac, sp["even_conv_w_il"], "sconv_fwd")
    Fc = _fox_gate_fwd(Af, sp["even_b_f_pad"], "fox_gate_fwd")
    Fh = Fc[:, :H].T
    Fq, Fk = Fh.reshape(H, L, 1), Fh.reshape(H, 1, L)
    o, lse, *gathered = _attn_fwd(qkv, Fq, Fk, "attn_fwd", gather)
    for (n, i), g in zip(_late_units(), gathered):
        full[n][i] = _assemble_unit(g, n)
    _prepare_late(W, full)
    Y = jnp.concatenate([y_a, o], axis=1)
    mix = _mm(Y, W["even_out"], "nn", F32, "even_out")
    return mix, dict(h=h, Ac=Ac, Af=Af, qkv=qkv, Fq=Fq, Fk=Fk, lse=lse, Y=Y)


def _even_bwd(dmix, dres, sv, W, sp, grads, exchange):
    H, Dh = FOX_HEADS, FOX_HEAD_DIM
    C = CONV_DIM
    L = dmix.shape[0]
    grads["even_w_out"][0] = _mm(sv["Y"], dmix, "tn", F32, "d_even_out")
    exchange = [_split_unit(grads["even_w_out"][0], "even_w_out").astype(BF16)] + list(exchange)
    dY = _mm(dmix, W["even_out"], "nt", F32, "dx_even_out")
    dya = dY[:, :C]
    do = dY[:, C:].astype(BF16)
    dc, dcw = _sconv_bwd_dc(sv["Ac"], dya, sp["even_conv_w_il"], "sconv_bwd_dc")
    grads["even_conv_w"][0] = dcw.reshape(CONV_WIDTH, SUBLANES, -1).sum(1)
    dAc = _sconv_bwd_da(sv["Ac"], dya, dc, sp["even_conv_w_il"], "sconv_bwd_da")
    dq, dk, dv, dFk, *arrived = _attn_bwd(sv["qkv"], sv["Fq"], sv["Fk"], sv["lse"], do, "attn_bwd", exchange)
    dqkv = jnp.concatenate([dq, dk.astype(BF16), dv.astype(BF16)], axis=1)
    dF = _pad_cols(dFk.reshape(H, L).T, LANES)
    dAf, dbf = _fox_gate_bwd(sv["Af"], sp["even_b_f_pad"], dF, "fox_gate_bwd")
    grads["even_b_f"][0] = dbf.sum(0)[:H]
    h = sv["h"]
    gc = _deinterleave_cols(_mm(h, dAc, "tn", F32, "d_even_in_conv"), 3, LANES)
    gq = _mm(h, dqkv, "tn", F32, "d_even_in_qkv")
    gf = _mm(h, dAf, "tn", F32, "d_even_in_f")[:, :H]
    grads["even_w_in"][0] = jnp.concatenate([gc, gq, gf], axis=1)
    dh = _mm(dAc, W["even_in_conv"], "nt", F32, "dx_even_in_conv", add=dres, add_scale=_alpha())
    dh = _mm(dqkv, W["even_in_qkv"], "nt", F32, "dx_even_in_qkv", add=dh)
    dh = _mm(dAf, W["even_in_f"], "nt", F32, "dx_even_in_f", add=dh)
    return dh, arrived


def _group_layouts(v, G):
    R = v.shape[0] // G
    return v.reshape(G, 1, R), v.reshape(G, R, 1)


def _odd_fwd(h, W, sp):
    d = _dims()
    L = h.shape[0]
    Hs, G, N, P = d["ssm_heads"], SSM_GROUPS, SSM_STATE, SSM_HEAD_DIM
    R = Hs // G
    inner = d["ssm_inner"]
    z = _mm(h, W["odd_in_z"], "nn", F32, "odd_in_z")
    xr = _mm(h, W["odd_in_x"], "nn", F32, "odd_in_x")
    dtp = _mm(h, W["odd_in_dt"], "nn", F32, "odd_in_dt")
    act = _mconv_fwd(xr, sp["odd_conv_w"], sp["odd_conv_b"], "mconv_fwd")
    dtg = dtp[:, :Hs].reshape(L, G, R)
    dtc, dtr = dtg.transpose(1, 0, 2), dtg.transpose(1, 2, 0)
    dsk_x = jnp.repeat(sp["odd_d_skip"], P).reshape(G, 1, R * P)
    ssd_in = (act, dtc, dtr) + _group_layouts(sp["odd_dt_bias"], G) + _group_layouts(sp["odd_a_log"], G) + (dsk_x,)
    y, hprev = _ssd_fwd(*ssd_in, "ssd_fwd")
    u = _gnorm_fwd(y, z, sp["odd_norm_g"], "gnorm_fwd")
    mix = _mm(u, W["odd_out"], "nn", F32, "odd_out")
    return mix, dict(h=h, z=z, xr=xr, ssd_in=ssd_in, hprev=hprev, y=y, u=u)


def _odd_bwd(dmix, dres, sv, W, sp, grads):
    d = _dims()
    L = dmix.shape[0]
    Hs, G, N, P = d["ssm_heads"], SSM_GROUPS, SSM_STATE, SSM_HEAD_DIM
    grads["odd_w_out"][0] = _mm(sv["u"], dmix, "tn", F32, "d_odd_out")
    du = _mm(dmix, W["odd_out"], "nt", BF16, "dx_odd_out")
    dy, dz, dg = _gnorm_bwd(sv["y"], sv["z"], sp["odd_norm_g"], du, "gnorm_bwd")
    grads["odd_norm_g"][0] = dg.sum(0)
    dxs, dB, dC, ddt, gbias, galog, gdsk = _ssd_bwd(*sv["ssd_in"], sv["hprev"], dy, "ssd_bwd")
    grads["odd_dt_bias"][0] = gbias.reshape(Hs)
    grads["odd_a_log"][0] = galog.reshape(Hs)
    grads["odd_d_skip"][0] = gdsk.reshape(Hs)
    dact = jnp.concatenate([dxs, dB, dC], axis=1)
    dxr, dcw, dcb = _mconv_bwd(sv["xr"], dact, sp["odd_conv_w"], sp["odd_conv_b"], "mconv_bwd")
    grads["odd_conv_w"][0] = dcw.reshape(SSM_CONV_WIDTH, SUBLANES, -1).sum(1)
    grads["odd_conv_b"][0] = dcb.sum(0)
    ddtp = _pad_cols(ddt.transpose(1, 0, 2).reshape(L, Hs), W["odd_in_dt"].shape[1])
    h = sv["h"]
    gz = _mm(h, dz, "tn", F32, "d_odd_in_z")
    gx = _mm(h, dxr, "tn", F32, "d_odd_in_x")
    gdt = _mm(h, ddtp, "tn", F32, "d_odd_in_dt")[:, :Hs]
    grads["odd_w_in"][0] = jnp.concatenate([gz, gx, gdt], axis=1)
    dh = _mm(dz, W["odd_in_z"], "nt", F32, "dx_odd_in_z", add=dres, add_scale=_alpha())
    dh = _mm(dxr, W["odd_in_x"], "nt", F32, "dx_odd_in_x", add=dh)
    dh = _mm(ddtp, W["odd_in_dt"], "nt", F32, "dx_odd_in_dt", add=dh)
    return dh


FIRST_UNIT = ("even_w_in", 0)


def _late_units():
    return [(n, i) for n in BIG for i in range(_full_shapes()[n][0][0]) if (n, i) != FIRST_UNIT]


def _assemble_unit(gathered, name):
    shape, ax = _full_shapes()[name]
    g = gathered.reshape((N_DEV,) + _shard_shape(name)[1:])
    return jnp.moveaxis(g, 0, ax - 1).reshape(shape[1:])


def _split_unit(full_layer, name):
    shape, ax = _full_shapes()[name]
    sh = shape[1:ax] + (N_DEV, shape[ax] // N_DEV) + shape[ax + 1:]
    return _as2d(jnp.moveaxis(full_layer.reshape(sh), ax - 1, 0), 1)


def _prepare_first(W, full):
    C, fd = CONV_DIM, _dims()["fox_dim"]
    ew = full["even_w_in"][0]
    W["even_in_conv"] = _interleave_cols(ew[:, :3 * C], 3, LANES)
    W["even_in_qkv"] = ew[:, 3 * C:3 * C + 3 * fd]
    W["even_in_f"] = _pad_cols(ew[:, 3 * C + 3 * fd:], LANES)


def _prepare_late(W, full):
    d = _dims()
    W["even_out"] = full["even_w_out"][0]
    ow = full["odd_w_in"][0]
    inner, cch, Hs = d["ssm_inner"], d["conv_ch"], d["ssm_heads"]
    W["odd_in_z"] = ow[:, :inner]
    W["odd_in_x"] = ow[:, inner:inner + cch]
    W["odd_in_dt"] = _pad_cols(ow[:, inner + cch:], -(-Hs // LANES) * LANES)
    W["odd_out"] = full["odd_w_out"][0]
    for key, name in (("ffn_up", "ffn_w_up"), ("ffn_down", "ffn_w_down"), ("ple_proj", "ple_w_proj"), ("ple_gate", "ple_w_gate")):
        W[key] = list(full[name])


def _prepare_small(full):
    sp = {}
    sp["even_conv_w_il"] = full["even_conv_w"][0]
    sp["even_b_f_pad"] = _pad_cols(full["even_b_f"], LANES)
    sp["odd_conv_w"] = full["odd_conv_w"][0]
    sp["odd_conv_b"] = full["odd_conv_b"][0]
    sp["odd_norm_g"] = full["odd_norm_g"][0]
    for n in ("odd_dt_bias", "odd_a_log", "odd_d_skip"):
        sp[n] = full[n][0]
    for n in ("ln_mix_g", "ln_mix_b", "ln_ffn_g", "ln_ffn_b", "ple_b_gate"):
        sp[n] = full[n]
    sp["ffn_conv_w"] = [full["ffn_conv_w"][i] for i in range(DEPTH)]
    sp["ffn_conv_b"] = [full["ffn_conv_b"][i] for i in range(DEPTH)]
    return sp


def _local_step(x, p, target, full, late_shards):
    sp = _prepare_small(full)
    W = {}
    _prepare_first(W, full)
    grads = {n: [None] * _full_shapes()[n][0][0] for n in WEIGHTS}
    pb = p.astype(BF16)
    mix0, sv_e = _even_fwd(x.astype(BF16), W, sp, full, late_shards)
    h3_0, h3_0b, sv_t0 = _tail_fwd(0, x, mix0, pb[0], W, sp)
    mix1, sv_o = _odd_fwd(h3_0b, W, sp)
    h3_1, _, sv_t1 = _tail_fwd(1, h3_0, mix1, pb[1], W, sp)
    dh, sq = _loss_head(h3_1, target, "loss_head")
    dr1, dr1b = _tail_bwd(1, dh, sv_t1, W, sp, grads)
    dh = _odd_bwd(dr1b, dr1, sv_o, W, sp, grads)
    dr1, dr1b = _tail_bwd(0, dh, sv_t0, W, sp, grads)
    outgoing = [_split_unit(grads[n][i], n).astype(BF16) for n, i in _late_units()[1:]]
    dx, incoming = _even_bwd(dr1b, dr1, sv_e, W, sp, grads, outgoing)
    reduced = {u: _sum_slots(r, f"sum_grads_{u[0]}{u[1]}") for u, r in zip(_late_units(), incoming)}
    grads = {n: v if n in BIG else jnp.stack(v) for n, v in grads.items()}
    return jnp.sum(sq), dx, grads, reduced


def kernel(x, p, even_w_in, even_b_f, even_conv_w, even_w_out, odd_w_in, odd_conv_w, odd_conv_b, odd_dt_bias, odd_a_log, odd_d_skip, odd_norm_g, odd_w_out, ln_mix_g, ln_mix_b, ffn_w_up, ffn_conv_w, ffn_conv_b, ffn_w_down, ln_ffn_g, ln_ffn_b, ple_w_proj, ple_w_gate, ple_b_gate, loss_target, m_even_w_in, m_even_b_f, m_even_conv_w, m_even_w_out, m_odd_w_in, m_odd_conv_w, m_odd_conv_b, m_odd_dt_bias, m_odd_a_log, m_odd_d_skip, m_odd_norm_g, m_odd_w_out, m_ln_mix_g, m_ln_mix_b, m_ffn_w_up, m_ffn_conv_w, m_ffn_conv_b, m_ffn_w_down, m_ln_ffn_g, m_ln_ffn_b, m_ple_w_proj, m_ple_w_gate, m_ple_b_gate, v_even_w_in, v_even_b_f, v_even_conv_w, v_even_w_out, v_odd_w_in, v_odd_conv_w, v_odd_conv_b, v_odd_dt_bias, v_odd_a_log, v_odd_d_skip, v_odd_norm_g, v_odd_w_out, v_ln_mix_g, v_ln_mix_b, v_ffn_w_up, v_ffn_conv_w, v_ffn_conv_b, v_ffn_w_down, v_ln_ffn_g, v_ln_ffn_b, v_ple_w_proj, v_ple_w_gate, v_ple_b_gate):
    args = locals()
    w = {n: args[n] for n in WEIGHTS}
    m = {n: args["m_" + n] for n in WEIGHTS}
    v = {n: args["v_" + n] for n in WEIGHTS}
    me = 4 * lax.axis_index("x") + 2 * lax.axis_index("y") + lax.axis_index("c")

    def shard(unit):
        return _as2d(w[unit[0]][unit[1]]).astype(BF16)

    first, small = _all_gather([shard(FIRST_UNIT), _pack_small([w[n] for n in SMALL_SHARDED])], "ag_weights")
    full = dict(w)
    for n in BIG:
        full[n] = [None] * _full_shapes()[n][0][0]
    full[FIRST_UNIT[0]][FIRST_UNIT[1]] = _assemble_unit(first, FIRST_UNIT[0])
    for n, g in zip(SMALL_SHARDED, _unpack_small(small, [_shard_shape(n) for n in SMALL_SHARDED])):
        full[n] = _assemble(g, n)

    sq, dx, grads, reduced = _local_step(x[0], p[:, 0], loss_target[0], full, [shard(u) for u in _late_units()])
    loss = lax.psum(0.5 * sq / D_MODEL, ("x", "y", "c"))

    n0, i0 = FIRST_UNIT
    (reduced[FIRST_UNIT],) = _reduce_scatter([_split_unit(grads[n0][i0], n0).astype(BF16)], "rs_grads")
    g_final = {n: jnp.stack([reduced[(n, i)] for i in range(_full_shapes()[n][0][0])]).reshape(_shard_shape(n)) for n in BIG}
    small_names = SMALL_SHARDED + REPLICATED
    (small_all,) = _all_gather([_pack_small([grads[n] for n in small_names])], "ag_small_grads")
    small_sum = _sum_slots(small_all, "sum_small_grads")
    for n, g in zip(small_names, _unpack_small(small_sum, [_full_shapes()[n][0] for n in small_names])):
        g_final[n] = lax.dynamic_index_in_dim(_split_dest(g, n), me, axis=0, keepdims=False) if n in SMALL_SHARDED else g

    out = {}
    for n in BIG:
        res = _adamw(*[_as2d(t[n]) for t in (w, g_final, m, v)], "adamw_" + n)
        out[n] = [r.reshape(_shard_shape(n)) for r in res]
    shapes = [_shard_shape(n) for n in small_names]
    res = _adamw(*[_pack_small([t[n] for n in small_names]) for t in (w, g_final, m, v)], "adamw_small")
    for n, d_, m_, v_ in zip(small_names, *[_unpack_small(r, shapes) for r in res]):
        out[n] = [d_, m_, v_]
    return (loss, dx[None], *[g_final[n] for n in WEIGHTS], *[out[n][0] for n in WEIGHTS],
            *[out[n][1] for n in WEIGHTS], *[out[n][2] for n in WEIGHTS])
```
